```python
import math
import jax, jax.numpy as jnp
from jax import lax
import numpy as np

D_MODEL = 1024
BATCH = 32
SEQ = 256
DEPTH = 2
DEC_BATCH = 2
DEC_SEQ = 1024
PAST_LEN = 512

GRID_W = 64
Q_BLOCK = 128
ROPE_THETA = 10000.0
EPS = 1e-6
N_MOD = 6
HEAD_DIM = 64
MIX_W = D_MODEL
MLA_HEADS = 6
MLA_NOPE = 32
MLA_ROPE = 32
MLA_V = 64
KV_RANK = 128
GQA_HEADS = 6
GQA_KV_HEADS = 2
GQA_GROUP = GQA_HEADS // GQA_KV_HEADS
DIFF_HEADS = 4
DIFF_QK = 32
DIFF_V = 64
MLA_W = MLA_HEADS * MLA_V
GQA_W = GQA_HEADS * HEAD_DIM
DIFF_W = DIFF_HEADS * DIFF_V
IN_SIZES = (MLA_HEADS * (MLA_NOPE + MLA_ROPE), KV_RANK, MLA_ROPE,
            GQA_HEADS * HEAD_DIM, GQA_KV_HEADS * HEAD_DIM, GQA_KV_HEADS * HEAD_DIM,
            DIFF_HEADS * 2 * DIFF_QK, DIFF_HEADS * 2 * DIFF_QK, DIFF_HEADS * DIFF_V)
IN_COLS = sum(IN_SIZES)
N_GROUPS = 4
EXPERTS_PER_GROUP = 4
N_EXPERTS = N_GROUPS * EXPERTS_PER_GROUP
TOP_K = 2
D_FF_EXPERT = 256

kernel_name = 'hybrid_diffusion_prefix_trunk_step'


def rms_norm(x, g):
    xf = x.astype(jnp.float32)
    y = xf * lax.rsqrt(jnp.mean(xf * xf, axis=-1, keepdims=True) + EPS)
    return (y * g.astype(jnp.float32)).astype(x.dtype)


def axial_rope(n_tokens, rot_dim):
    rows = n_tokens // GRID_W
    row = jnp.repeat(jnp.arange(rows, dtype=jnp.float32), GRID_W)
    col = jnp.tile(jnp.arange(GRID_W, dtype=jnp.float32), rows)
    quarter = rot_dim // 4
    inv = ROPE_THETA ** (-jnp.arange(quarter, dtype=jnp.float32) / quarter)
    ang = jnp.concatenate([row[:, None] * inv, col[:, None] * inv], axis=-1)
    return jnp.cos(ang), jnp.sin(ang)


def apply_rope(x, cos, sin):
    half = x.shape[-1] // 2
    shape = (cos.shape[0],) + (1,) * (x.ndim - 3) + (half,)
    cos = cos.reshape(shape).astype(x.dtype)
    sin = sin.reshape(shape).astype(x.dtype)
    x1, x2 = x[..., :half], x[..., half:]
    return jnp.concatenate([x1 * cos - x2 * sin, x1 * sin + x2 * cos], axis=-1)


def attend(q, k, v):
    B, Sq, KH, G, dk = q.shape
    dv = v.shape[-1]
    nb = Sq // Q_BLOCK
    scale = dk ** -0.5
    qb = jnp.moveaxis(q.reshape(B, nb, Q_BLOCK, KH, G, dk), 1, 0)

    def one(qblk):
        s = jnp.einsum('bqhgd,bkhd->bhgqk', qblk, k).astype(jnp.float32) * scale
        p = jax.nn.softmax(s, axis=-1)
        return jnp.einsum('bhgqk,bkhe->bqhge', p.astype(v.dtype), v)

    o = lax.map(one, qb)
    return jnp.moveaxis(o, 0, 1).reshape(B, Sq, KH, G, dv)


def split_cols(z):
    out, start = [], 0
    for size in IN_SIZES:
        out.append(z[..., start:start + size])
        start += size
    return out


def token_tensors(h, w_in, kv_g, qn_g, kn_g, rope):
    B, S, _ = h.shape
    z = jnp.einsum('bsd,de->bse', h, w_in)
    pa, pckv, pkr, pqg, pkg, pvg, pqd, pkd, pvd = split_cols(z)
    q_a = pa.reshape(B, S, MLA_HEADS, MLA_NOPE + MLA_ROPE)
    ckv = rms_norm(pckv, kv_g)
    k_rope = pkr
    q_g = rms_norm(pqg.reshape(B, S, GQA_KV_HEADS, GQA_GROUP, HEAD_DIM), qn_g)
    k_g = rms_norm(pkg.reshape(B, S, GQA_KV_HEADS, HEAD_DIM), kn_g)
    v_g = pvg.reshape(B, S, GQA_KV_HEADS, HEAD_DIM)
    q_d = pqd.reshape(B, S, DIFF_HEADS, 2, DIFF_QK)
    k_d = pkd.reshape(B, S, DIFF_HEADS, 2, DIFF_QK)
    v_d = pvd.reshape(B, S, DIFF_HEADS, DIFF_V)
    if rope is not None:
        (c32, s32), (c64, s64) = rope
        q_a = jnp.concatenate([q_a[..., :MLA_NOPE], apply_rope(q_a[..., MLA_NOPE:], c32, s32)], axis=-1)
        k_rope = apply_rope(k_rope, c32, s32)
        q_g = apply_rope(q_g, c64, s64)
        k_g = apply_rope(k_g, c64, s64)
        q_d = apply_rope(q_d, c32, s32)
        k_d = apply_rope(k_d, c32, s32)
    return (q_a, q_g, q_d), (ckv, k_rope, k_g, v_g, k_d, v_d)


def mix(qs, kvs, w_uk, w_uv, lam_p, diff_g, lam_init, w_out):
    q_a, q_g, q_d = qs
    ckv, k_rope, k_g, v_g, k_d, v_d = kvs
    B, S = q_a.shape[:2]
    Sk = ckv.shape[1]
    k_nope = jnp.einsum('bkr,rhd->bkhd', ckv, w_uk)
    v_a = jnp.einsum('bkr,rhe->bkhe', ckv, w_uv)
    k_a = jnp.concatenate([k_nope, jnp.broadcast_to(k_rope[:, :, None, :], (B, Sk, MLA_HEADS, MLA_ROPE))], axis=-1)
    o_a = attend(q_a[:, :, :, None, :], k_a, v_a).reshape(B, S, MLA_W)
    o_g = attend(q_g, k_g, v_g).reshape(B, S, GQA_W)
    lp = lam_p.astype(jnp.float32)
    lam = jnp.exp(jnp.sum(lp[0] * lp[1])) - jnp.exp(jnp.sum(lp[2] * lp[3])) + lam_init
    o1 = attend(q_d[:, :, :, 0:1, :], k_d[:, :, :, 0], v_d)[:, :, :, 0]
    o2 = attend(q_d[:, :, :, 1:2, :], k_d[:, :, :, 1], v_d)[:, :, :, 0]
    o_d = rms_norm(o1.astype(jnp.float32) - lam * o2.astype(jnp.float32), diff_g) * (1.0 - lam_init)
    o_d = o_d.astype(o1.dtype).reshape(B, S, DIFF_W)
    o = jnp.concatenate([o_a, o_g, o_d], axis=-1)
    return jnp.einsum('bse,ed->bsd', o, w_out)


def hier_moe(h, w_grp, b_grp, w_rtr, b_rtr, w_gate, w_up, w_down):
    B, S, D = h.shape
    t = h.reshape(B * S, D)
    T = t.shape[0]
    g_prob = jax.nn.softmax((t @ w_grp + b_grp).astype(jnp.float32), axis=-1)
    g_top, g_idx = lax.top_k(g_prob, 1)
    e_logits = (t @ w_rtr + b_rtr).astype(jnp.float32).reshape(T, N_GROUPS, EXPERTS_PER_GROUP)
    e_sel = jnp.take_along_axis(e_logits, jnp.broadcast_to(g_idx[:, :, None], (T, 1, EXPERTS_PER_GROUP)), axis=1)[:, 0]
    e_top, e_idx = lax.top_k(jax.nn.softmax(e_sel, axis=-1), TOP_K)
    e_top = e_top / jnp.sum(e_top, axis=-1, keepdims=True)
    expert_id = g_idx * EXPERTS_PER_GROUP + e_idx
    weights = g_top * e_top
    gates = jnp.sum(jax.nn.one_hot(expert_id, N_EXPERTS, dtype=jnp.float32) * weights[..., None], axis=1)
    a = jnp.einsum('td,edf->tef', t, w_gate)
    u = jnp.einsum('td,edf->tef', t, w_up)
    hid = jax.nn.silu(a) * u * gates.astype(t.dtype)[..., None]
    y = jnp.einsum('tef,efd->td', hid, w_down)
    return y.reshape(B, S, D)


def block(x, cond, i, rope, ctx_kv, p):
    (norm1_g, norm2_g, w_mod, b_mod, w_in, kv_g, w_uk, w_uv, qn_g, kn_g, diff_lambda, diff_g, w_out,
     w_grp, b_grp, w_rtr, b_rtr, w_gate, w_up, w_down) = p
    mod = (jnp.einsum('bd,de->be', jax.nn.silu(cond), w_mod[i]) + b_mod[i]).reshape(cond.shape[0], N_MOD, D_MODEL)[:, :, None, :]
    shift1, scale1, gate1, shift2, scale2, gate2 = (mod[:, j] for j in range(N_MOD))
    h = rms_norm(x, norm1_g[i]) * (1 + scale1) + shift1
    qs, kv = token_tensors(h, w_in[i], kv_g[i], qn_g[i], kn_g[i], rope)
    if ctx_kv is None:
        kv_read = kv
    else:
        kv_read = tuple(jnp.concatenate([cached, own], axis=1) for cached, own in zip(ctx_kv, kv))
    lam_init = 0.8 - 0.6 * math.exp(-0.3 * i)
    x = x + gate1 * mix(qs, kv_read, w_uk[i], w_uv[i], diff_lambda[i], diff_g[i], lam_init, w_out[i])
    h = rms_norm(x, norm2_g[i]) * (1 + scale2) + shift2
    x = x + gate2 * hier_moe(h, w_grp[i], b_grp[i], w_rtr[i], b_rtr[i], w_gate[i], w_up[i], w_down[i])
    return x, kv


def setup_inputs(seed: int = 0) -> dict:
    key = jax.random.key(seed)
    ks = jax.random.split(key, 32)
    n = lambda k, shape, s=1.0: jax.random.normal(k, shape, jnp.float32) * s
    D = D_MODEL
    return {
        'x_prompt': n(ks[0], (BATCH, SEQ, D)),
        'x_sample': n(ks[1], (DEC_BATCH, DEC_SEQ, D)),
        'c': n(ks[2], (DEC_BATCH, D)),
        'cache_mla_ckv': n(ks[3], (DEC_BATCH, DEPTH, PAST_LEN, KV_RANK)),
        'cache_mla_krope': n(ks[4], (DEC_BATCH, DEPTH, PAST_LEN, MLA_ROPE)),
        'cache_gqa_k': n(ks[5], (DEC_BATCH, DEPTH, PAST_LEN, GQA_KV_HEADS, HEAD_DIM)),
        'cache_gqa_v': n(ks[6], (DEC_BATCH, DEPTH, PAST_LEN, GQA_KV_HEADS, HEAD_DIM)),
        'cache_diff_k': n(ks[7], (DEC_BATCH, DEPTH, PAST_LEN, DIFF_HEADS, 2, DIFF_QK)),
        'cache_diff_v': n(ks[8], (DEC_BATCH, DEPTH, PAST_LEN, DIFF_HEADS, DIFF_V)),
        'c_ctx': n(ks[9], (D,)),
        'norm1_g': 1.0 + n(ks[10], (DEPTH, D), 0.1),
        'norm2_g': 1.0 + n(ks[11], (DEPTH, D), 0.1),
        'w_mod': n(ks[12], (DEPTH, D, N_MOD * D), D ** -0.5),
        'b_mod': n(ks[13], (DEPTH, N_MOD * D), 0.02),
        'w_in': n(ks[14], (DEPTH, D, IN_COLS), D ** -0.5),
        'mla_kv_norm_g': 1.0 + n(ks[15], (DEPTH, KV_RANK), 0.1),
        'mla_w_uk': n(ks[16], (DEPTH, KV_RANK, MLA_HEADS, MLA_NOPE), KV_RANK ** -0.5),
        'mla_w_uv': n(ks[17], (DEPTH, KV_RANK, MLA_HEADS, MLA_V), KV_RANK ** -0.5),
        'gqa_q_norm_g': 1.0 + n(ks[18], (DEPTH, HEAD_DIM), 0.1),
        'gqa_k_norm_g': 1.0 + n(ks[19], (DEPTH, HEAD_DIM), 0.1),
        'diff_lambda': n(ks[20], (DEPTH, 4, DIFF_QK), 0.1),
        'diff_norm_g': 1.0 + n(ks[21], (DEPTH, DIFF_V), 0.1),
        'w_out': n(ks[22], (DEPTH, MIX_W, D), MIX_W ** -0.5),
        'moe_w_group': n(ks[23], (DEPTH, D, N_GROUPS), D ** -0.5),
        'moe_b_group': n(ks[24], (DEPTH, N_GROUPS), 0.01),
        'moe_w_router': n(ks[25], (DEPTH, D, N_EXPERTS), D ** -0.5),
        'moe_b_router': n(ks[26], (DEPTH, N_EXPERTS), 0.01),
        'moe_w_gate': n(ks[27], (DEPTH, N_EXPERTS, D, D_FF_EXPERT), D ** -0.5),
        'moe_w_up': n(ks[28], (DEPTH, N_EXPERTS, D, D_FF_EXPERT), D ** -0.5),
        'moe_w_down': n(ks[29], (DEPTH, N_EXPERTS, D_FF_EXPERT, D), D_FF_EXPERT ** -0.5),
        'final_norm_g': 1.0 + n(ks[30], (D,), 0.1),
    }


def reference(x_prompt, x_sample, c, cache_mla_ckv, cache_mla_krope, cache_gqa_k, cache_gqa_v,
              cache_diff_k, cache_diff_v, c_ctx, norm1_g, norm2_g, w_mod, b_mod, w_in,
              mla_kv_norm_g, mla_w_uk, mla_w_uv, gqa_q_norm_g, gqa_k_norm_g, diff_lambda, diff_norm_g,
              w_out, moe_w_group, moe_b_group, moe_w_router, moe_b_router, moe_w_gate, moe_w_up,
              moe_w_down, final_norm_g):
    p = (norm1_g, norm2_g, w_mod, b_mod, w_in, mla_kv_norm_g, mla_w_uk, mla_w_uv, gqa_q_norm_g,
         gqa_k_norm_g, diff_lambda, diff_norm_g, w_out, moe_w_group, moe_b_group, moe_w_router,
         moe_b_router, moe_w_gate, moe_w_up, moe_w_down)
    x = x_prompt
    ctx_states = []
    for i in range(DEPTH):
        x, kv = block(x, c_ctx[None, :], i, None, None, p)
        ctx_states.append(kv)
    y_prompt = rms_norm(x, final_norm_g)
    new_mla_ckv = jnp.stack([s[0] for s in ctx_states], axis=1)
    new_mla_krope = jnp.stack([s[1] for s in ctx_states], axis=1)
    new_gqa_k = jnp.stack([s[2] for s in ctx_states], axis=1)
    new_gqa_v = jnp.stack([s[3] for s in ctx_states], axis=1)
    new_diff_k = jnp.stack([s[4] for s in ctx_states], axis=1)
    new_diff_v = jnp.stack([s[5] for s in ctx_states], axis=1)
    n_lat = x_sample.shape[1]
    rope = (axial_rope(n_lat, MLA_ROPE), axial_rope(n_lat, HEAD_DIM))
    caches = (cache_mla_ckv, cache_mla_krope, cache_gqa_k, cache_gqa_v, cache_diff_k, cache_diff_v)
    x = x_sample
    for i in range(DEPTH):
        ctx_kv = tuple(cch[:, i] for cch in caches)
        x, _ = block(x, c, i, rope, ctx_kv, p)
    y_sample = rms_norm(x, final_norm_g)
    return (y_prompt, y_sample, new_mla_ckv, new_mla_krope, new_gqa_k, new_gqa_v, new_diff_k, new_diff_v)
```

```python
import functools
import math

import jax
import jax.numpy as jnp
from jax.experimental import pallas as pl
from jax.experimental.pallas import tpu as pltpu

D_MODEL = 1024
DEPTH = 2
PAST_LEN = 512
GRID_W = 64
ROPE_THETA = 10000.0
EPS = 1e-6
N_MOD = 6
HEAD_DIM = 64
MLA_HEADS = 6
MLA_NOPE = 32
MLA_ROPE = 32
MLA_V = 64
KV_RANK = 128
GQA_HEADS = 6
GQA_KV_HEADS = 2
GQA_GROUP = GQA_HEADS // GQA_KV_HEADS
DIFF_HEADS = 4
DIFF_QK = 32
DIFF_V = 64
N_GROUPS = 4
EXPERTS_PER_GROUP = 4
N_EXPERTS = N_GROUPS * EXPERTS_PER_GROUP
D_FF_EXPERT = 256

LANES = 128
MOD_ROWS = 8

Z_QA, Z_CKV, Z_QG, Z_KG, Z_VG, Z_QD, Z_KD, Z_VD, Z_KR = 0, 384, 512, 896, 1024, 1152, 1408, 1664, 1920
Z_COLS = 2048
Q_A, Q_G, Q_D, Q_COLS = 0, 384, 768, 1024
KV_KA, KV_VA, KV_KG, KV_VG, KV_KD, KV_VD, KV_COLS = 0, 384, 768, 896, 1024, 1280, 1536
C_CKV, C_KR, C_KG, C_VG, C_KD, C_VD = 128, 32, 128, 128, 256, 256

VMEM_LIMIT = 56 * 1024 * 1024

bf16 = jnp.bfloat16
f32 = jnp.float32


def _dot(a, b):
    return jnp.dot(a, b, preferred_element_type=f32)


def _dot_nt(a, b):
    return jax.lax.dot_general(a, b, (((1,), (1,)), ((), ())), preferred_element_type=f32)


def _rms(x, width):
    return x * jax.lax.rsqrt(jnp.sum(x * x, axis=-1, keepdims=True) * (1.0 / width) + EPS)


def _params(sem):
    return pltpu.CompilerParams(dimension_semantics=sem, vmem_limit_bytes=VMEM_LIMIT)


def _mod_kernel(cond_ref, w_ref, b_ref, o_ref):
    c = cond_ref[...]
    s = (c * (1.0 / (1.0 + jnp.exp(-c)))).astype(bf16)
    o_ref[0] = _dot(s, w_ref[0].astype(bf16)) + b_ref[0]


def _modulation(cond, w_mod, b_mod):
    return pl.pallas_call(
        _mod_kernel,
        out_shape=jax.ShapeDtypeStruct((DEPTH, MOD_ROWS, N_MOD * D_MODEL), f32),
        grid=(DEPTH, N_MOD),
        in_specs=[
            pl.BlockSpec((MOD_ROWS, D_MODEL), lambda i, j: (0, 0)),
            pl.BlockSpec((1, D_MODEL, D_MODEL), lambda i, j: (i, 0, j)),
            pl.BlockSpec((1, 1, D_MODEL), lambda i, j: (i, 0, j)),
        ],
        out_specs=pl.BlockSpec((1, MOD_ROWS, D_MODEL), lambda i, j: (i, 0, j)),
        compiler_params=_params(("arbitrary", "arbitrary")),
        name="modulation",
    )(cond, w_mod, b_mod.reshape(DEPTH, 1, N_MOD * D_MODEL))


def _swap_halves(x, half):
    lane = jax.lax.broadcasted_iota(jnp.int32, x.shape, 1)
    fwd = pltpu.roll(x, LANES - half, 1)
    bwd = pltpu.roll(x, half, 1)
    return jnp.where((lane & (2 * half - 1)) < half, fwd, bwd)


def _rope_block(x, cos, sin, half):
    return x * cos + _swap_halves(x, half) * sin


def _pre_kernel(rope, write_cache, *refs):
    it = iter(refs)
    x_ref, mod_ref, g1_ref, w_in_ref, kvg_ref, qkg_ref, seg_ref, wka_ref, wuv_ref = (next(it) for _ in range(9))
    if rope:
        ca_ref, sa_ref, c32_ref, s32_ref, c64_ref, s64_ref = (next(it) for _ in range(6))
        next(it)
    q_ref, kv_ref = next(it), next(it)
    if write_cache:
        cckv_ref, ckr_ref, ckg_ref, cvg_ref, ckd_ref, cvd_ref = (next(it) for _ in range(6))

    x = x_ref[...]
    shift1 = mod_ref[0, 0, 0:1, :]
    scale1 = mod_ref[0, 0, 1:2, :]
    h = (_rms(x, D_MODEL) * g1_ref[0]) * (1.0 + scale1) + shift1
    z = _dot(h.astype(bf16), w_in_ref[0])

    ckv = _rms(z[:, Z_CKV:Z_CKV + KV_RANK], KV_RANK) * kvg_ref[0]

    qk = z[:, Z_QG:Z_VG]
    sq = qk * qk
    sq_hi = sq.astype(bf16)
    sq_lo = (sq - sq_hi.astype(f32)).astype(bf16)
    seg = seg_ref[...]
    ms = (_dot(sq_hi, seg) + _dot(sq_lo, seg)) * (1.0 / HEAD_DIM)
    qk = qk * jax.lax.rsqrt(ms + EPS) * qkg_ref[0]

    def blocks(arr, n):
        return [arr[:, LANES * j:LANES * (j + 1)] for j in range(n)]

    qa = blocks(z[:, Z_QA:Z_QA + 384], 3)
    qkb = blocks(qk, 4)
    qd = blocks(z[:, Z_QD:Z_QD + 256], 2)
    kd = blocks(z[:, Z_KD:Z_KD + 256], 2)
    kr = z[:, Z_KR:Z_KR + LANES]
    if rope:
        ca, sa, c32, s32, c64, s64 = (r[...] for r in (ca_ref, sa_ref, c32_ref, s32_ref, c64_ref, s64_ref))
        qa = [_rope_block(b, ca, sa, MLA_ROPE // 2) for b in qa]
        qkb = [_rope_block(b, c64, s64, HEAD_DIM // 2) for b in qkb]
        qd = [_rope_block(b, c32, s32, DIFF_QK // 2) for b in qd]
        kd = [_rope_block(b, c32, s32, DIFF_QK // 2) for b in kd]
        kr = _rope_block(kr, c32, s32, MLA_ROPE // 2)

    vg = z[:, Z_VG:Z_VG + 128]
    vd = z[:, Z_VD:Z_VD + 256]
    ckv_b = ckv.astype(bf16)
    k_a = _dot(jnp.concatenate([ckv_b, kr.astype(bf16)], axis=1), wka_ref[0])
    v_a = _dot(ckv_b, wuv_ref[0])

    for j in range(3):
        q_ref[:, Q_A + LANES * j:Q_A + LANES * (j + 1)] = qa[j].astype(bf16)
        q_ref[:, Q_G + LANES * j:Q_G + LANES * (j + 1)] = qkb[j].astype(bf16)
    for j in range(2):
        q_ref[:, Q_D + LANES * j:Q_D + LANES * (j + 1)] = qd[j].astype(bf16)
        kv_ref[0, 0, :, KV_KD + LANES * j:KV_KD + LANES * (j + 1)] = kd[j].astype(bf16)
    kv_ref[0, 0, :, KV_KA:KV_KA + 384] = k_a.astype(bf16)
    kv_ref[0, 0, :, KV_VA:KV_VA + 384] = v_a.astype(bf16)
    kv_ref[0, 0, :, KV_KG:KV_KG + 128] = qkb[3].astype(bf16)
    kv_ref[0, 0, :, KV_VG:KV_VG + 128] = vg.astype(bf16)
    kv_ref[0, 0, :, KV_VD:KV_VD + 256] = vd.astype(bf16)
    if write_cache:
        cckv_ref[...] = ckv
        ckr_ref[...] = kr[:, :MLA_ROPE]
        ckg_ref[...] = qkb[3]
        cvg_ref[...] = vg
        for j in range(2):
            ckd_ref[:, LANES * j:LANES * (j + 1)] = kd[j]
        cvd_ref[...] = vd


def _pre(layer, x, S, mod, mod_row, w, tile, rope_tabs=None, kv_buf=None):
    n = x.shape[0]
    rope = rope_tabs is not None
    write_cache = not rope
    lay = lambda t: (layer, 0, 0)
    in_specs = [
        pl.BlockSpec((tile, D_MODEL), lambda t: (t, 0)),
        pl.BlockSpec((1, 1, N_MOD, D_MODEL), lambda t: (layer, mod_row(t * tile), 0, 0)),
        pl.BlockSpec((1, 1, D_MODEL), lay),
        pl.BlockSpec((1, D_MODEL, Z_COLS), lay),
        pl.BlockSpec((1, 1, KV_RANK), lay),
        pl.BlockSpec((1, 1, 512), lay),
        pl.BlockSpec((512, 512), lambda t: (0, 0)),
        pl.BlockSpec((1, 256, 384), lay),
        pl.BlockSpec((1, KV_RANK, 384), lay),
    ]
    args = [x.reshape(n, D_MODEL), mod, w["g1"], w["w_in"], w["kv_g"], w["qk_g"], w["seg"], w["w_ka"], w["w_uv"]]
    aliases = {}
    if rope:
        per_b = S // tile
        off = PAST_LEN // tile
        in_specs += [pl.BlockSpec((tile, LANES), lambda t: (t % per_b, 0))] * 6
        args += list(rope_tabs)
        in_specs.append(pl.BlockSpec(memory_space=pl.ANY))
        args.append(kv_buf)
        aliases = {len(args) - 1: 1}
        kv_shape = kv_buf.shape
        kv_index = lambda t: (layer, t // per_b, off + t % per_b, 0)
    else:
        kv_shape = (1, 1, n, KV_COLS)
        kv_index = lambda t: (0, 0, t, 0)
    out_shape = [jax.ShapeDtypeStruct((n, Q_COLS), bf16), jax.ShapeDtypeStruct(kv_shape, bf16)]
    out_specs = [
        pl.BlockSpec((tile, Q_COLS), lambda t: (t, 0)),
        pl.BlockSpec((1, 1, tile, KV_COLS), kv_index),
    ]
    if write_cache:
        for width in (C_CKV, C_KR, C_KG, C_VG, C_KD, C_VD):
            out_shape.append(jax.ShapeDtypeStruct((n, width), f32))
            out_specs.append(pl.BlockSpec((tile, width), lambda t: (t, 0)))
    return pl.pallas_call(
        functools.partial(_pre_kernel, rope, write_cache),
        out_shape=out_shape,
        grid=(n // tile,),
        in_specs=in_specs,
        out_specs=out_specs,
        input_output_aliases=aliases,
        compiler_params=_params(("arbitrary",)),
        name="pre_latent" if rope else "pre_context",
    )(*args)


def _cache_kernel(ckv_ref, kr_ref, kg_ref, vg_ref, kd_ref, vd_ref, wka_ref, wuv_ref, kv_ref):
    ckv_b = ckv_ref[0, 0].astype(bf16)
    wka = wka_ref[0]
    k_a = _dot(ckv_b, wka[:KV_RANK]) + _dot(kr_ref[0, 0].astype(bf16), wka[KV_RANK:KV_RANK + MLA_ROPE])
    kv_ref[0, 0, :, KV_KA:KV_KA + 384] = k_a.astype(bf16)
    kv_ref[0, 0, :, KV_VA:KV_VA + 384] = _dot(ckv_b, wuv_ref[0]).astype(bf16)
    kv_ref[0, 0, :, KV_KG:KV_KG + 128] = kg_ref[0, 0].astype(bf16)
    kv_ref[0, 0, :, KV_VG:KV_VG + 128] = vg_ref[0, 0].astype(bf16)
    kv_ref[0, 0, :, KV_KD:KV_KD + 256] = kd_ref[0, 0].astype(bf16)
    kv_ref[0, 0, :, KV_VD:KV_VD + 256] = vd_ref[0, 0].astype(bf16)


def _cache_rows(caches, w, n_lat):
    ckv, kr, kg, vg, kd, vd = caches
    B = ckv.shape[0]
    spec = lambda width: pl.BlockSpec((1, 1, PAST_LEN, width), lambda i, b: (b, i, 0, 0))
    return pl.pallas_call(
        _cache_kernel,
        out_shape=jax.ShapeDtypeStruct((DEPTH, B, PAST_LEN + n_lat, KV_COLS), bf16),
        grid=(DEPTH, B),
        in_specs=[spec(C_CKV), spec(C_KR), spec(C_KG), spec(C_VG), spec(C_KD), spec(C_VD),
                  pl.BlockSpec((1, 256, 384), lambda i, b: (i, 0, 0)),
                  pl.BlockSpec((1, KV_RANK, 384), lambda i, b: (i, 0, 0))],
        out_specs=pl.BlockSpec((1, 1, PAST_LEN, KV_COLS), lambda i, b: (i, b, 0, 0)),
        compiler_params=_params(("arbitrary", "arbitrary")),
        name="cache_rows",
    )(ckv, kr, kg.reshape(B, DEPTH, PAST_LEN, C_KG), vg.reshape(B, DEPTH, PAST_LEN, C_VG),
      kd.reshape(B, DEPTH, PAST_LEN, C_KD), vd.reshape(B, DEPTH, PAST_LEN, C_VD), w["w_ka"], w["w_uv"])


def _softmax_pv(s, v):
    m = jnp.max(s, axis=-1, keepdims=True)
    p = jnp.exp(s - m)
    l = jnp.sum(p, axis=-1, keepdims=True)
    return _dot(p.astype(bf16), v) / l


def _attn_kernel(lam_init, q_ref, kv_ref, lam_ref, dg_ref, o_ref):
    def q(off, width):
        return q_ref[:, off:off + width]

    def kv(off, width):
        return kv_ref[0, 0, :, off:off + width]

    heads = []
    for h in range(MLA_HEADS):
        s = _dot_nt(q(Q_A + 64 * h, 64) * 0.125, kv(KV_KA + 64 * h, 64))
        heads.append(_softmax_pv(s, kv(KV_VA + MLA_V * h, MLA_V)))
    for h in range(GQA_HEADS):
        g = h // GQA_GROUP
        s = _dot_nt(q(Q_G + 64 * h, 64) * 0.125, kv(KV_KG + 64 * g, 64))
        heads.append(_softmax_pv(s, kv(KV_VG + 64 * g, 64)))

    lp = lam_ref[0]
    e1 = jnp.exp(jnp.sum(lp[0:1] * lp[1:2], axis=-1, keepdims=True))
    e2 = jnp.exp(jnp.sum(lp[2:3] * lp[3:4], axis=-1, keepdims=True))
    lam = e1 - e2 + lam_init
    scale_d = DIFF_QK ** -0.5
    for h in range(DIFF_HEADS):
        v = kv(KV_VD + DIFF_V * h, DIFF_V)
        o = []
        for c in range(2):
            off = 64 * h + DIFF_QK * c
            s = _dot_nt(q(Q_D + off, DIFF_QK), kv(KV_KD + off, DIFF_QK)) * scale_d
            o.append(_softmax_pv(s, v))
        heads.append(_rms(o[0] - lam * o[1], DIFF_V) * dg_ref[0] * (1.0 - lam_init))

    for j in range(len(heads) // 2):
        o_ref[:, LANES * j:LANES * (j + 1)] = jnp.concatenate(heads[2 * j:2 * j + 2], axis=1).astype(bf16)


def _attention(layer, q, kv, s_kv, kv_index, w, tile):
    n = q.shape[0]
    lam_init = 0.8 - 0.6 * math.exp(-0.3 * layer)
    return pl.pallas_call(
        functools.partial(_attn_kernel, lam_init),
        out_shape=jax.ShapeDtypeStruct((n, D_MODEL), bf16),
        grid=(n // tile,),
        in_specs=[
            pl.BlockSpec((tile, Q_COLS), lambda t: (t, 0)),
            pl.BlockSpec((1, 1, s_kv, KV_COLS), kv_index),
            pl.BlockSpec((1, 4, DIFF_QK), lambda t: (layer, 0, 0)),
            pl.BlockSpec((1, 1, DIFF_V), lambda t: (layer, 0, 0)),
        ],
        out_specs=pl.BlockSpec((tile, D_MODEL), lambda t: (t, 0)),
        compiler_params=_params(("arbitrary",)),
        name="attention",
    )(q, kv, w["lam"], w["diff_g"])


def _post_kernel(o_ref, x_ref, mod_ref, w_out_ref, g2_ref, wr_ref, br_ref, x1_ref, h2_ref, gates_ref):
    gate1 = mod_ref[0, 0, 2:3, :]
    shift2 = mod_ref[0, 0, 3:4, :]
    scale2 = mod_ref[0, 0, 4:5, :]
    x1 = x_ref[...] + gate1 * _dot(o_ref[...], w_out_ref[0])
    x1_ref[...] = x1
    h2 = ((_rms(x1, D_MODEL) * g2_ref[0]) * (1.0 + scale2) + shift2).astype(bf16)
    h2_ref[...] = h2

    logits = _dot(h2, wr_ref[0]) + br_ref[0]
    lane = jax.lax.broadcasted_iota(jnp.int32, logits.shape, 1)
    lane_f = lane.astype(f32)
    big = float(LANES)
    neg = -jnp.inf

    def first_lane(mask):
        return jnp.min(jnp.where(mask, lane_f, big), axis=-1, keepdims=True)

    gmask = (lane >= N_EXPERTS) & (lane < N_EXPERTS + N_GROUPS)
    gl = jnp.where(gmask, logits, neg)
    ge = jnp.where(gmask, jnp.exp(gl - jnp.max(gl, axis=-1, keepdims=True)), 0.0)
    gprob = ge / jnp.sum(ge, axis=-1, keepdims=True)
    g_top = jnp.max(gprob, axis=-1, keepdims=True)
    g_idx = first_lane(gmask & (gprob == g_top)) - float(N_EXPERTS)

    emask = (lane < N_EXPERTS) & ((lane >> 2).astype(f32) == g_idx)
    el = jnp.where(emask, logits, neg)
    ee = jnp.where(emask, jnp.exp(el - jnp.max(el, axis=-1, keepdims=True)), 0.0)
    ep = ee / jnp.sum(ee, axis=-1, keepdims=True)
    p1 = jnp.max(jnp.where(emask, ep, -1.0), axis=-1, keepdims=True)
    i1 = first_lane(emask & (ep == p1))
    rest = emask & (lane_f != i1)
    p2 = jnp.max(jnp.where(rest, ep, -1.0), axis=-1, keepdims=True)
    i2 = first_lane(rest & (ep == p2))
    tot = p1 + p2
    w1 = g_top * (p1 / tot)
    w2 = g_top * (p2 / tot)
    gates_ref[...] = jnp.where(lane_f == i1, w1, 0.0) + jnp.where(lane_f == i2, w2, 0.0)


def _post(layer, o, x, mod, mod_row, w, tile):
    n = o.shape[0]
    lay = lambda t: (layer, 0, 0)
    row = lambda t: (t, 0)
    return pl.pallas_call(
        _post_kernel,
        out_shape=[jax.ShapeDtypeStruct((n, D_MODEL), f32), jax.ShapeDtypeStruct((n, D_MODEL), bf16),
                   jax.ShapeDtypeStruct((n, LANES), f32)],
        grid=(n // tile,),
        in_specs=[
            pl.BlockSpec((tile, D_MODEL), row),
            pl.BlockSpec((tile, D_MODEL), row),
            pl.BlockSpec((1, 1, N_MOD, D_MODEL), lambda t: (layer, mod_row(t * tile), 0, 0)),
            pl.BlockSpec((1, D_MODEL, D_MODEL), lay),
            pl.BlockSpec((1, 1, D_MODEL), lay),
            pl.BlockSpec((1, D_MODEL, LANES), lay),
            pl.BlockSpec((1, 1, LANES), lay),
        ],
        out_specs=[pl.BlockSpec((tile, D_MODEL), row), pl.BlockSpec((tile, D_MODEL), row),
                   pl.BlockSpec((tile, LANES), row)],
        compiler_params=_params(("arbitrary",)),
        name="post_attention",
    )(o, x.reshape(n, D_MODEL), mod, w["w_out"], w["g2"], w["w_r"], w["b_r"])


def _moe_kernel(final, x1_ref, h2_ref, gates_ref, mod_ref, wg_ref, wu_ref, wd_ref, fg_ref, o_ref, acc_ref):
    e = pl.program_id(1)

    @pl.when(e == 0)
    def _():
        acc_ref[...] = jnp.zeros_like(acc_ref)

    h2 = h2_ref[...]
    gates = gates_ref[...]
    lane = jax.lax.broadcasted_iota(jnp.int32, gates.shape, 1)
    gate = jnp.sum(jnp.where(lane == e, gates, 0.0), axis=-1, keepdims=True)
    a = _dot(h2, wg_ref[0, 0])
    u = _dot(h2, wu_ref[0, 0])
    hid = (a * (1.0 / (1.0 + jnp.exp(-a)))) * u * gate
    acc_ref[...] += _dot(hid.astype(bf16), wd_ref[0, 0])

    @pl.when(e == N_EXPERTS - 1)
    def _():
        x2 = x1_ref[...] + mod_ref[0, 0, 5:6, :] * acc_ref[...]
        if final:
            x2 = _rms(x2, D_MODEL) * fg_ref[...]
        o_ref[...] = x2


def _moe(layer, x1, h2, gates, mod, mod_row, w, tile, final):
    n = x1.shape[0]
    row = lambda t, e: (t, 0)
    return pl.pallas_call(
        functools.partial(_moe_kernel, final),
        out_shape=jax.ShapeDtypeStruct((n, D_MODEL), f32),
        grid=(n // tile, N_EXPERTS),
        in_specs=[
            pl.BlockSpec((tile, D_MODEL), row),
            pl.BlockSpec((tile, D_MODEL), row),
            pl.BlockSpec((tile, LANES), row),
            pl.BlockSpec((1, 1, N_MOD, D_MODEL), lambda t, e: (layer, mod_row(t * tile), 0, 0)),
            pl.BlockSpec((1, 1, D_MODEL, D_FF_EXPERT), lambda t, e: (layer, e, 0, 0)),
            pl.BlockSpec((1, 1, D_MODEL, D_FF_EXPERT), lambda t, e: (layer, e, 0, 0)),
            pl.BlockSpec((1, 1, D_FF_EXPERT, D_MODEL), lambda t, e: (layer, e, 0, 0)),
            pl.BlockSpec((1, D_MODEL), lambda t, e: (0, 0)),
        ],
        out_specs=pl.BlockSpec((tile, D_MODEL), row),
        scratch_shapes=[pltpu.VMEM((tile, D_MODEL), f32)],
        compiler_params=_params(("arbitrary", "arbitrary")),
        name="experts",
    )(x1, h2, gates, mod, w["w_gate"], w["w_up"], w["w_down"], w["final_g"])


def _rope_tables(n_tokens):
    rows = n_tokens // GRID_W
    row = jnp.repeat(jnp.arange(rows, dtype=f32), GRID_W)
    col = jnp.tile(jnp.arange(GRID_W, dtype=f32), rows)

    def cs(rot_dim):
        quarter = rot_dim // 4
        inv = ROPE_THETA ** (-jnp.arange(quarter, dtype=f32) / quarter)
        ang = jnp.concatenate([row[:, None] * inv, col[:, None] * inv], axis=-1)
        return jnp.cos(ang), jnp.sin(ang)

    c32, s32 = cs(MLA_ROPE)
    c64, s64 = cs(HEAD_DIM)
    ones = jnp.ones((n_tokens, MLA_NOPE), f32)
    zeros = jnp.zeros((n_tokens, MLA_NOPE), f32)
    rep = lambda parts: jnp.tile(jnp.concatenate(parts, axis=-1), (1, LANES // sum(p.shape[-1] for p in parts)))
    return (rep([ones, c32, c32]), rep([zeros, -s32, s32]), rep([c32, c32]), rep([-s32, s32]),
            rep([c64, c64]), rep([-s64, s64]))


def _layout_weights(norm1_g, norm2_g, w_in, mla_kv_norm_g, mla_w_uk, mla_w_uv, gqa_q_norm_g, gqa_k_norm_g,
                    diff_lambda, diff_norm_g, w_out, moe_w_group, moe_b_group, moe_w_router, moe_b_router,
                    moe_w_gate, moe_w_up, moe_w_down, final_norm_g):
    pieces, start = [], 0
    for size in (384, 128, 32, 384, 128, 128, 256, 256, 256):
        pieces.append(w_in[:, :, start:start + size])
        start += size
    pa, pckv, pkr, pqg, pkg, pvg, pqd, pkd, pvd = pieces
    pad = jnp.zeros((DEPTH, D_MODEL, Z_COLS - Z_KR - MLA_ROPE), w_in.dtype)
    w_in_r = jnp.concatenate([pa, pckv, pqg, pkg, pvg, pqd, pkd, pvd, pkr, pad], axis=-1).astype(bf16)

    eye = jnp.eye(MLA_ROPE, dtype=f32)
    top = jnp.concatenate([mla_w_uk, jnp.zeros((DEPTH, KV_RANK, MLA_HEADS, MLA_ROPE), f32)], axis=-1)
    mid = jnp.concatenate([jnp.zeros((MLA_ROPE, MLA_HEADS, MLA_NOPE), f32),
                           jnp.broadcast_to(eye[:, None, :], (MLA_ROPE, MLA_HEADS, MLA_ROPE))], axis=-1)
    w_ka = jnp.concatenate([top.reshape(DEPTH, KV_RANK, 384),
                            jnp.broadcast_to(mid.reshape(1, MLA_ROPE, 384), (DEPTH, MLA_ROPE, 384)),
                            jnp.zeros((DEPTH, 256 - KV_RANK - MLA_ROPE, 384), f32)], axis=1).astype(bf16)

    seg_id = jnp.arange(512) // HEAD_DIM
    seg = (seg_id[:, None] == seg_id[None, :]).astype(bf16)
    qk_g = jnp.concatenate([jnp.tile(gqa_q_norm_g, (1, GQA_HEADS)), jnp.tile(gqa_k_norm_g, (1, GQA_KV_HEADS))], axis=-1)
    w_r = jnp.concatenate([moe_w_router, moe_w_group,
                           jnp.zeros((DEPTH, D_MODEL, LANES - N_EXPERTS - N_GROUPS), f32)], axis=-1).astype(bf16)
    b_r = jnp.concatenate([moe_b_router, moe_b_group, jnp.zeros((DEPTH, LANES - N_EXPERTS - N_GROUPS), f32)], axis=-1)
    return dict(
        g1=norm1_g.reshape(DEPTH, 1, D_MODEL), g2=norm2_g.reshape(DEPTH, 1, D_MODEL), w_in=w_in_r,
        kv_g=mla_kv_norm_g.reshape(DEPTH, 1, KV_RANK), qk_g=qk_g.reshape(DEPTH, 1, 512), seg=seg, w_ka=w_ka,
        w_uv=mla_w_uv.reshape(DEPTH, KV_RANK, 384).astype(bf16), lam=diff_lambda,
        diff_g=diff_norm_g.reshape(DEPTH, 1, DIFF_V), w_out=w_out.astype(bf16), w_r=w_r,
        b_r=b_r.reshape(DEPTH, 1, LANES), w_gate=moe_w_gate.astype(bf16), w_up=moe_w_up.astype(bf16),
        w_down=moe_w_down.astype(bf16), final_g=final_norm_g.reshape(1, D_MODEL))


PRE_TILE = 512
CTX_ATTN_TILE = 256
LAT_ATTN_TILE = 256
POST_TILE = 512
MOE_TILE = 1024


def kernel(x_prompt, x_sample, c, cache_mla_ckv, cache_mla_krope, cache_gqa_k, cache_gqa_v, cache_diff_k, cache_diff_v, c_ctx, norm1_g, norm2_g, w_mod, b_mod, w_in, mla_kv_norm_g, mla_w_uk, mla_w_uv, gqa_q_norm_g, gqa_k_norm_g, diff_lambda, diff_norm_g, w_out, moe_w_group, moe_b_group, moe_w_router, moe_b_router, moe_w_gate, moe_w_up, moe_w_down, final_norm_g):
    B, S, _ = x_prompt.shape
    Bl, Sl, _ = x_sample.shape
    assert S == CTX_ATTN_TILE and Bl + 1 <= MOD_ROWS
    w = _layout_weights(norm1_g, norm2_g, w_in, mla_kv_norm_g, mla_w_uk, mla_w_uv, gqa_q_norm_g, gqa_k_norm_g,
                        diff_lambda, diff_norm_g, w_out, moe_w_group, moe_b_group, moe_w_router, moe_b_router,
                        moe_w_gate, moe_w_up, moe_w_down, final_norm_g)
    cond = jnp.concatenate([c_ctx[None, :], c, jnp.zeros((MOD_ROWS - 1 - Bl, D_MODEL), f32)], axis=0)
    mod = _modulation(cond, w_mod, b_mod).reshape(DEPTH, MOD_ROWS, N_MOD, D_MODEL)
    ctx_row = lambda token: 0
    lat_row = lambda token: 1 + token // Sl

    x = x_prompt.reshape(B * S, D_MODEL)
    cache_rows = []
    for i in range(DEPTH):
        q, kv, *cache = _pre(i, x, S, mod, ctx_row, w, PRE_TILE)
        cache_rows.append(cache)
        o = _attention(i, q, kv, S, lambda t: (0, 0, t, 0), w, CTX_ATTN_TILE)
        x1, h2, gates = _post(i, o, x, mod, ctx_row, w, POST_TILE)
        x = _moe(i, x1, h2, gates, mod, ctx_row, w, MOE_TILE, final=(i == DEPTH - 1))
    y_prompt = x.reshape(B, S, D_MODEL)
    stacked = [jnp.stack([cache_rows[i][j].reshape(B, S, -1) for i in range(DEPTH)], axis=1) for j in range(6)]
    new_mla_ckv, new_mla_krope = stacked[0], stacked[1]
    new_gqa_k = stacked[2].reshape(B, DEPTH, S, GQA_KV_HEADS, HEAD_DIM)
    new_gqa_v = stacked[3].reshape(B, DEPTH, S, GQA_KV_HEADS, HEAD_DIM)
    new_diff_k = stacked[4].reshape(B, DEPTH, S, DIFF_HEADS, 2, DIFF_QK)
    new_diff_v = stacked[5].reshape(B, DEPTH, S, DIFF_HEADS, DIFF_V)

    tabs = _rope_tables(Sl)
    kv_all = _cache_rows((cache_mla_ckv, cache_mla_krope, cache_gqa_k, cache_gqa_v, cache_diff_k, cache_diff_v), w, Sl)
    x = x_sample.reshape(Bl * Sl, D_MODEL)
    per_b = Sl // LAT_ATTN_TILE
    for i in range(DEPTH):
        q, kv_all = _pre(i, x, Sl, mod, lat_row, w, PRE_TILE, rope_tabs=tabs, kv_buf=kv_all)
        o = _attention(i, q, kv_all, PAST_LEN + Sl, lambda t, i=i: (i, t // per_b, 0, 0), w, LAT_ATTN_TILE)
        x1, h2, gates = _post(i, o, x, mod, lat_row, w, POST_TILE)
        x = _moe(i, x1, h2, gates, mod, lat_row, w, MOE_TILE, final=(i == DEPTH - 1))
    y_sample = x.reshape(Bl, Sl, D_MODEL)
    return (y_prompt, y_sample, new_mla_ckv, new_mla_krope, new_gqa_k, new_gqa_v, new_diff_k, new_diff_v)
```

```python
import functools
import math

import jax
import jax.numpy as jnp
from jax.experimental import pallas as pl
from jax.experimental.pallas import tpu as pltpu

D_MODEL = 1024
DEPTH = 2
PAST_LEN = 512
GRID_W = 64
ROPE_THETA = 10000.0
EPS = 1e-6
LOG2E = 1.4426950408889634
N_MOD = 6
HEAD_DIM = 64
MLA_HEADS = 6
MLA_NOPE = 32
MLA_ROPE = 32
MLA_V = 64
KV_RANK = 128
GQA_HEADS = 6
GQA_KV_HEADS = 2
GQA_GROUP = GQA_HEADS // GQA_KV_HEADS
DIFF_HEADS = 4
DIFF_QK = 32
DIFF_V = 64
N_GROUPS = 4
EXPERTS_PER_GROUP = 4
N_EXPERTS = N_GROUPS * EXPERTS_PER_GROUP
D_FF_EXPERT = 256

LANES = 128
MOD_ROWS = 8

Z_QA, Z_CKV, Z_QG, Z_KG, Z_VG, Z_QD, Z_KD, Z_VD, Z_KR = 0, 384, 512, 896, 1024, 1152, 1408, 1664, 1920
Z_COLS = 2048
Q_A, Q_G, Q_D, Q_COLS = 0, 384, 768, 1024
KV_KA, KV_VA, KV_KG, KV_VG, KV_KD, KV_VD, KV_COLS = 0, 384, 768, 896, 1024, 1280, 1536
C_CKV, C_KR, C_KG, C_VG, C_KD, C_VD = 128, 32, 128, 128, 256, 256

VMEM_LIMIT = 56 * 1024 * 1024

bf16 = jnp.bfloat16
f32 = jnp.float32


def _dot(a, b):
    return jnp.dot(a, b, preferred_element_type=f32)


def _dot_nt(a, b):
    return jax.lax.dot_general(a, b, (((1,), (1,)), ((), ())), preferred_element_type=f32)


def _rms(x, width):
    return x * jax.lax.rsqrt(jnp.sum(x * x, axis=-1, keepdims=True) * (1.0 / width) + EPS)


def _params(sem):
    return pltpu.CompilerParams(dimension_semantics=sem, vmem_limit_bytes=VMEM_LIMIT)


def _mod_kernel(cond_ref, w_ref, b_ref, o_ref):
    c = cond_ref[...]
    s = (c * (1.0 / (1.0 + jnp.exp(-c)))).astype(bf16)
    o_ref[0] = _dot(s, w_ref[0].astype(bf16)) + b_ref[0]


def _modulation(cond, w_mod, b_mod):
    return pl.pallas_call(
        _mod_kernel,
        out_shape=jax.ShapeDtypeStruct((DEPTH, MOD_ROWS, N_MOD * D_MODEL), f32),
        grid=(DEPTH, N_MOD),
        in_specs=[
            pl.BlockSpec((MOD_ROWS, D_MODEL), lambda i, j: (0, 0)),
            pl.BlockSpec((1, D_MODEL, D_MODEL), lambda i, j: (i, 0, j)),
            pl.BlockSpec((1, 1, D_MODEL), lambda i, j: (i, 0, j)),
        ],
        out_specs=pl.BlockSpec((1, MOD_ROWS, D_MODEL), lambda i, j: (i, 0, j)),
        compiler_params=_params(("arbitrary", "arbitrary")),
        name="modulation",
    )(cond, w_mod, b_mod.reshape(DEPTH, 1, N_MOD * D_MODEL))


def _swap_halves(x, half):
    lane = jax.lax.broadcasted_iota(jnp.int32, x.shape, 1)
    fwd = pltpu.roll(x, LANES - half, 1)
    bwd = pltpu.roll(x, half, 1)
    return jnp.where((lane & (2 * half - 1)) < half, fwd, bwd)


def _rope_block(x, cos, sin, half):
    return x * cos + _swap_halves(x, half) * sin


def _pre_kernel(rope, write_cache, *refs):
    it = iter(refs)
    x_ref, mod_ref, g1_ref, w_in_ref, kvg_ref, qkg_ref, seg_ref, wka_ref, wuv_ref = (next(it) for _ in range(9))
    if rope:
        ca_ref, sa_ref, c32_ref, s32_ref, c64_ref, s64_ref = (next(it) for _ in range(6))
        next(it)
    q_ref, kv_ref = next(it), next(it)
    if write_cache:
        cckv_ref, ckr_ref, ckg_ref, cvg_ref, ckd_ref, cvd_ref = (next(it) for _ in range(6))

    x = x_ref[...]
    shift1 = mod_ref[0, 0, 0:1, :]
    scale1 = mod_ref[0, 0, 1:2, :]
    h = (_rms(x, D_MODEL) * g1_ref[0]) * (1.0 + scale1) + shift1
    z = _dot(h.astype(bf16), w_in_ref[0])

    ckv = _rms(z[:, Z_CKV:Z_CKV + KV_RANK], KV_RANK) * kvg_ref[0]

    qk = z[:, Z_QG:Z_VG]
    sq = qk * qk
    sq_hi = sq.astype(bf16)
    sq_lo = (sq - sq_hi.astype(f32)).astype(bf16)
    seg = seg_ref[...]
    ms = (_dot(sq_hi, seg) + _dot(sq_lo, seg)) * (1.0 / HEAD_DIM)
    qk = qk * jax.lax.rsqrt(ms + EPS) * qkg_ref[0]

    def blocks(arr, n):
        return [arr[:, LANES * j:LANES * (j + 1)] for j in range(n)]

    qa = blocks(z[:, Z_QA:Z_QA + 384], 3)
    qkb = blocks(qk, 4)
    qd = blocks(z[:, Z_QD:Z_QD + 256], 2)
    kd = blocks(z[:, Z_KD:Z_KD + 256], 2)
    kr = z[:, Z_KR:Z_KR + LANES]
    if rope:
        ca, sa, c32, s32, c64, s64 = (r[...] for r in (ca_ref, sa_ref, c32_ref, s32_ref, c64_ref, s64_ref))
        qa = [_rope_block(b, ca, sa, MLA_ROPE // 2) for b in qa]
        qkb = [_rope_block(b, c64, s64, HEAD_DIM // 2) for b in qkb]
        qd = [_rope_block(b, c32, s32, DIFF_QK // 2) for b in qd]
        kd = [_rope_block(b, c32, s32, DIFF_QK // 2) for b in kd]
        kr = _rope_block(kr, c32, s32, MLA_ROPE // 2)

    vg = z[:, Z_VG:Z_VG + 128]
    vd = z[:, Z_VD:Z_VD + 256]
    ckv_b = ckv.astype(bf16)
    k_a = _dot(jnp.concatenate([ckv_b, kr.astype(bf16)], axis=1), wka_ref[0])
    v_a = _dot(ckv_b, wuv_ref[0])

    for j in range(3):
        q_ref[:, Q_A + LANES * j:Q_A + LANES * (j + 1)] = (qa[j] * (HEAD_DIM ** -0.5 * LOG2E)).astype(bf16)
        q_ref[:, Q_G + LANES * j:Q_G + LANES * (j + 1)] = (qkb[j] * (HEAD_DIM ** -0.5 * LOG2E)).astype(bf16)
    for j in range(2):
        q_ref[:, Q_D + LANES * j:Q_D + LANES * (j + 1)] = (qd[j] * (DIFF_QK ** -0.5 * LOG2E)).astype(bf16)
        kv_ref[0, 0, :, KV_KD + LANES * j:KV_KD + LANES * (j + 1)] = kd[j].astype(bf16)
    kv_ref[0, 0, :, KV_KA:KV_KA + 384] = k_a.astype(bf16)
    kv_ref[0, 0, :, KV_VA:KV_VA + 384] = v_a.astype(bf16)
    kv_ref[0, 0, :, KV_KG:KV_KG + 128] = qkb[3].astype(bf16)
    kv_ref[0, 0, :, KV_VG:KV_VG + 128] = vg.astype(bf16)
    kv_ref[0, 0, :, KV_VD:KV_VD + 256] = vd.astype(bf16)
    if write_cache:
        cckv_ref[...] = ckv
        ckr_ref[...] = kr[:, :MLA_ROPE]
        ckg_ref[...] = qkb[3]
        cvg_ref[...] = vg
        for j in range(2):
            ckd_ref[:, LANES * j:LANES * (j + 1)] = kd[j]
        cvd_ref[...] = vd


def _pre(layer, x, S, mod, mod_row, w, tile, rope_tabs=None, kv_buf=None):
    n = x.shape[0]
    rope = rope_tabs is not None
    write_cache = not rope
    lay = lambda t: (layer, 0, 0)
    in_specs = [
        pl.BlockSpec((tile, D_MODEL), lambda t: (t, 0)),
        pl.BlockSpec((1, 1, N_MOD, D_MODEL), lambda t: (layer, mod_row(t * tile), 0, 0)),
        pl.BlockSpec((1, 1, D_MODEL), lay),
        pl.BlockSpec((1, D_MODEL, Z_COLS), lay),
        pl.BlockSpec((1, 1, KV_RANK), lay),
        pl.BlockSpec((1, 1, 512), lay),
        pl.BlockSpec((512, 512), lambda t: (0, 0)),
        pl.BlockSpec((1, 256, 384), lay),
        pl.BlockSpec((1, KV_RANK, 384), lay),
    ]
    args = [x.reshape(n, D_MODEL), mod, w["g1"], w["w_in"], w["kv_g"], w["qk_g"], w["seg"], w["w_ka"], w["w_uv"]]
    aliases = {}
    if rope:
        per_b = S // tile
        off = PAST_LEN // tile
        in_specs += [pl.BlockSpec((tile, LANES), lambda t: (t % per_b, 0))] * 6
        args += list(rope_tabs)
        in_specs.append(pl.BlockSpec(memory_space=pl.ANY))
        args.append(kv_buf)
        aliases = {len(args) - 1: 1}
        kv_shape = kv_buf.shape
        kv_index = lambda t: (layer, t // per_b, off + t % per_b, 0)
    else:
        kv_shape = (1, 1, n, KV_COLS)
        kv_index = lambda t: (0, 0, t, 0)
    out_shape = [jax.ShapeDtypeStruct((n, Q_COLS), bf16), jax.ShapeDtypeStruct(kv_shape, bf16)]
    out_specs = [
        pl.BlockSpec((tile, Q_COLS), lambda t: (t, 0)),
        pl.BlockSpec((1, 1, tile, KV_COLS), kv_index),
    ]
    if write_cache:
        for width in (C_CKV, C_KR, C_KG, C_VG, C_KD, C_VD):
            out_shape.append(jax.ShapeDtypeStruct((n, width), f32))
            out_specs.append(pl.BlockSpec((tile, width), lambda t: (t, 0)))
    return pl.pallas_call(
        functools.partial(_pre_kernel, rope, write_cache),
        out_shape=out_shape,
        grid=(n // tile,),
        in_specs=in_specs,
        out_specs=out_specs,
        input_output_aliases=aliases,
        compiler_params=_params(("arbitrary",)),
        name="pre_latent" if rope else "pre_context",
    )(*args)


def _cache_kernel(ckv_ref, kr_ref, kg_ref, vg_ref, kd_ref, vd_ref, wka_ref, wuv_ref, kv_ref):
    ckv_b = ckv_ref[0, 0].astype(bf16)
    wka = wka_ref[0]
    k_a = _dot(ckv_b, wka[:KV_RANK]) + _dot(kr_ref[0, 0].astype(bf16), wka[KV_RANK:KV_RANK + MLA_ROPE])
    kv_ref[0, 0, :, KV_KA:KV_KA + 384] = k_a.astype(bf16)
    kv_ref[0, 0, :, KV_VA:KV_VA + 384] = _dot(ckv_b, wuv_ref[0]).astype(bf16)
    kv_ref[0, 0, :, KV_KG:KV_KG + 128] = kg_ref[0, 0].astype(bf16)
    kv_ref[0, 0, :, KV_VG:KV_VG + 128] = vg_ref[0, 0].astype(bf16)
    kv_ref[0, 0, :, KV_KD:KV_KD + 256] = kd_ref[0, 0].astype(bf16)
    kv_ref[0, 0, :, KV_VD:KV_VD + 256] = vd_ref[0, 0].astype(bf16)


def _cache_rows(caches, w, n_lat):
    ckv, kr, kg, vg, kd, vd = caches
    B = ckv.shape[0]
    spec = lambda width: pl.BlockSpec((1, 1, PAST_LEN, width), lambda i, b: (b, i, 0, 0))
    return pl.pallas_call(
        _cache_kernel,
        out_shape=jax.ShapeDtypeStruct((DEPTH, B, PAST_LEN + n_lat, KV_COLS), bf16),
        grid=(DEPTH, B),
        in_specs=[spec(C_CKV), spec(C_KR), spec(C_KG), spec(C_VG), spec(C_KD), spec(C_VD),
                  pl.BlockSpec((1, 256, 384), lambda i, b: (i, 0, 0)),
                  pl.BlockSpec((1, KV_RANK, 384), lambda i, b: (i, 0, 0))],
        out_specs=pl.BlockSpec((1, 1, PAST_LEN, KV_COLS), lambda i, b: (i, b, 0, 0)),
        compiler_params=_params(("arbitrary", "arbitrary")),
        name="cache_rows",
    )(ckv, kr, kg.reshape(B, DEPTH, PAST_LEN, C_KG), vg.reshape(B, DEPTH, PAST_LEN, C_VG),
      kd.reshape(B, DEPTH, PAST_LEN, C_KD), vd.reshape(B, DEPTH, PAST_LEN, C_VD), w["w_ka"], w["w_uv"])


_SCORE_HEADS = (
    [(Q_A + 64 * h, KV_KA + 64 * h, 64, KV_VA + MLA_V * h) for h in range(MLA_HEADS)]
    + [(Q_G + 64 * h, KV_KG + 64 * (h // GQA_GROUP), 64, KV_VG + 64 * (h // GQA_GROUP)) for h in range(GQA_HEADS)]
    + [(Q_D + 64 * h + DIFF_QK * c, KV_KD + 64 * h + DIFF_QK * c, DIFF_QK, KV_VD + DIFF_V * h)
       for h in range(DIFF_HEADS) for c in range(2)])


def _attn_kernel(lam_init, per_round, q_ref, kv_ref, lam_ref, dg_ref, o_ref, s_ref, p_ref):
    outs = []
    for start in range(0, len(_SCORE_HEADS), per_round):
        chunk = _SCORE_HEADS[start:start + per_round]
        for j, (q_off, k_off, width, _) in enumerate(chunk):
            s_ref[j] = _dot_nt(q_ref[:, q_off:q_off + width], kv_ref[0, 0, :, k_off:k_off + width])
        s = s_ref[...]
        p = jnp.exp2(s - jnp.max(s, axis=-1, keepdims=True))
        inv = 1.0 / jnp.sum(p, axis=-1, keepdims=True)
        p_ref[...] = p.astype(bf16)
        for j, (_, _, _, v_off) in enumerate(chunk):
            outs.append(_dot(p_ref[j], kv_ref[0, 0, :, v_off:v_off + DIFF_V]) * inv[j])

    lp = lam_ref[0]
    e1 = jnp.exp(jnp.sum(lp[0:1] * lp[1:2], axis=-1, keepdims=True))
    e2 = jnp.exp(jnp.sum(lp[2:3] * lp[3:4], axis=-1, keepdims=True))
    lam = e1 - e2 + lam_init
    heads = outs[:MLA_HEADS + GQA_HEADS]
    for h in range(DIFF_HEADS):
        o1, o2 = outs[MLA_HEADS + GQA_HEADS + 2 * h:MLA_HEADS + GQA_HEADS + 2 * h + 2]
        heads.append(_rms(o1 - lam * o2, DIFF_V) * dg_ref[0] * (1.0 - lam_init))
    for j in range(len(heads) // 2):
        o_ref[:, LANES * j:LANES * (j + 1)] = jnp.concatenate(heads[2 * j:2 * j + 2], axis=1).astype(bf16)


def _attention(layer, q, kv, s_kv, kv_index, w, tile, per_round):
    n = q.shape[0]
    lam_init = 0.8 - 0.6 * math.exp(-0.3 * layer)
    assert len(_SCORE_HEADS) % per_round == 0
    return pl.pallas_call(
        functools.partial(_attn_kernel, lam_init, per_round),
        out_shape=jax.ShapeDtypeStruct((n, D_MODEL), bf16),
        scratch_shapes=[pltpu.VMEM((per_round, tile, s_kv), f32), pltpu.VMEM((per_round, tile, s_kv), bf16)],
        grid=(n // tile,),
        in_specs=[
            pl.BlockSpec((tile, Q_COLS), lambda t: (t, 0)),
            pl.BlockSpec((1, 1, s_kv, KV_COLS), kv_index),
            pl.BlockSpec((1, 4, DIFF_QK), lambda t: (layer, 0, 0)),
            pl.BlockSpec((1, 1, DIFF_V), lambda t: (layer, 0, 0)),
        ],
        out_specs=pl.BlockSpec((tile, D_MODEL), lambda t: (t, 0)),
        compiler_params=_params(("arbitrary",)),
        name="attention",
    )(q, kv, w["lam"], w["diff_g"])


def _post_kernel(o_ref, x_ref, mod_ref, w_out_ref, g2_ref, wr_ref, br_ref, x1_ref, h2_ref, gates_ref):
    gate1 = mod_ref[0, 0, 2:3, :]
    shift2 = mod_ref[0, 0, 3:4, :]
    scale2 = mod_ref[0, 0, 4:5, :]
    x1 = x_ref[...] + gate1 * _dot(o_ref[...], w_out_ref[0])
    x1_ref[...] = x1
    h2 = ((_rms(x1, D_MODEL) * g2_ref[0]) * (1.0 + scale2) + shift2).astype(bf16)
    h2_ref[...] = h2

    logits = _dot(h2, wr_ref[0]) + br_ref[0]
    lane = jax.lax.broadcasted_iota(jnp.int32, logits.shape, 1)
    lane_f = lane.astype(f32)
    big = float(LANES)
    neg = -jnp.inf

    def first_lane(mask):
        return jnp.min(jnp.where(mask, lane_f, big), axis=-1, keepdims=True)

    gmask = (lane >= N_EXPERTS) & (lane < N_EXPERTS + N_GROUPS)
    gl = jnp.where(gmask, logits, neg)
    ge = jnp.where(gmask, jnp.exp(gl - jnp.max(gl, axis=-1, keepdims=True)), 0.0)
    gprob = ge / jnp.sum(ge, axis=-1, keepdims=True)
    g_top = jnp.max(gprob, axis=-1, keepdims=True)
    g_idx = first_lane(gmask & (gprob == g_top)) - float(N_EXPERTS)

    emask = (lane < N_EXPERTS) & ((lane >> 2).astype(f32) == g_idx)
    el = jnp.where(emask, logits, neg)
    ee = jnp.where(emask, jnp.exp(el - jnp.max(el, axis=-1, keepdims=True)), 0.0)
    ep = ee / jnp.sum(ee, axis=-1, keepdims=True)
    p1 = jnp.max(jnp.where(emask, ep, -1.0), axis=-1, keepdims=True)
    i1 = first_lane(emask & (ep == p1))
    rest = emask & (lane_f != i1)
    p2 = jnp.max(jnp.where(rest, ep, -1.0), axis=-1, keepdims=True)
    i2 = first_lane(rest & (ep == p2))
    tot = p1 + p2
    w1 = g_top * (p1 / tot)
    w2 = g_top * (p2 / tot)
    gates_ref[...] = jnp.where(lane_f == i1, w1, 0.0) + jnp.where(lane_f == i2, w2, 0.0)


def _post(layer, o, x, mod, mod_row, w, tile):
    n = o.shape[0]
    lay = lambda t: (layer, 0, 0)
    row = lambda t: (t, 0)
    return pl.pallas_call(
        _post_kernel,
        out_shape=[jax.ShapeDtypeStruct((n, D_MODEL), f32), jax.ShapeDtypeStruct((n, D_MODEL), bf16),
                   jax.ShapeDtypeStruct((n, LANES), f32)],
        grid=(n // tile,),
        in_specs=[
            pl.BlockSpec((tile, D_MODEL), row),
            pl.BlockSpec((tile, D_MODEL), row),
            pl.BlockSpec((1, 1, N_MOD, D_MODEL), lambda t: (layer, mod_row(t * tile), 0, 0)),
            pl.BlockSpec((1, D_MODEL, D_MODEL), lay),
            pl.BlockSpec((1, 1, D_MODEL), lay),
            pl.BlockSpec((1, D_MODEL, LANES), lay),
            pl.BlockSpec((1, 1, LANES), lay),
        ],
        out_specs=[pl.BlockSpec((tile, D_MODEL), row), pl.BlockSpec((tile, D_MODEL), row),
                   pl.BlockSpec((tile, LANES), row)],
        compiler_params=_params(("arbitrary",)),
        name="post_attention",
    )(o, x.reshape(n, D_MODEL), mod, w["w_out"], w["g2"], w["w_r"], w["b_r"])


def _moe_kernel(final, x1_ref, h2_ref, gates_ref, mod_ref, wg_ref, wu_ref, wd_ref, fg_ref, o_ref, acc_ref):
    e = pl.program_id(1)

    @pl.when(e == 0)
    def _():
        acc_ref[...] = jnp.zeros_like(acc_ref)

    h2 = h2_ref[...]
    gates = gates_ref[...]
    lane = jax.lax.broadcasted_iota(jnp.int32, gates.shape, 1)
    gate = jnp.sum(jnp.where(lane == e, gates, 0.0), axis=-1, keepdims=True)
    a = _dot(h2, wg_ref[0, 0])
    u = _dot(h2, wu_ref[0, 0])
    hid = (a * (1.0 / (1.0 + jnp.exp(-a)))) * u * gate
    acc_ref[...] += _dot(hid.astype(bf16), wd_ref[0, 0])

    @pl.when(e == N_EXPERTS - 1)
    def _():
        x2 = x1_ref[...] + mod_ref[0, 0, 5:6, :] * acc_ref[...]
        if final:
            x2 = _rms(x2, D_MODEL) * fg_ref[...]
        o_ref[...] = x2


def _moe(layer, x1, h2, gates, mod, mod_row, w, tile, final):
    n = x1.shape[0]
    row = lambda t, e: (t, 0)
    return pl.pallas_call(
        functools.partial(_moe_kernel, final),
        out_shape=jax.ShapeDtypeStruct((n, D_MODEL), f32),
        grid=(n // tile, N_EXPERTS),
        in_specs=[
            pl.BlockSpec((tile, D_MODEL), row),
            pl.BlockSpec((tile, D_MODEL), row),
            pl.BlockSpec((tile, LANES), row),
            pl.BlockSpec((1, 1, N_MOD, D_MODEL), lambda t, e: (layer, mod_row(t * tile), 0, 0)),
            pl.BlockSpec((1, 1, D_MODEL, D_FF_EXPERT), lambda t, e: (layer, e, 0, 0)),
            pl.BlockSpec((1, 1, D_MODEL, D_FF_EXPERT), lambda t, e: (layer, e, 0, 0)),
            pl.BlockSpec((1, 1, D_FF_EXPERT, D_MODEL), lambda t, e: (layer, e, 0, 0)),
            pl.BlockSpec((1, D_MODEL), lambda t, e: (0, 0)),
        ],
        out_specs=pl.BlockSpec((tile, D_MODEL), row),
        scratch_shapes=[pltpu.VMEM((tile, D_MODEL), f32)],
        compiler_params=_params(("arbitrary", "arbitrary")),
        name="experts",
    )(x1, h2, gates, mod, w["w_gate"], w["w_up"], w["w_down"], w["final_g"])


def _rope_tables(n_tokens):
    rows = n_tokens // GRID_W
    row = jnp.repeat(jnp.arange(rows, dtype=f32), GRID_W)
    col = jnp.tile(jnp.arange(GRID_W, dtype=f32), rows)

    def cs(rot_dim):
        quarter = rot_dim // 4
        inv = ROPE_THETA ** (-jnp.arange(quarter, dtype=f32) / quarter)
        ang = jnp.concatenate([row[:, None] * inv, col[:, None] * inv], axis=-1)
        return jnp.cos(ang), jnp.sin(ang)

    c32, s32 = cs(MLA_ROPE)
    c64, s64 = cs(HEAD_DIM)
    ones = jnp.ones((n_tokens, MLA_NOPE), f32)
    zeros = jnp.zeros((n_tokens, MLA_NOPE), f32)
    rep = lambda parts: jnp.tile(jnp.concatenate(parts, axis=-1), (1, LANES // sum(p.shape[-1] for p in parts)))
    return (rep([ones, c32, c32]), rep([zeros, -s32, s32]), rep([c32, c32]), rep([-s32, s32]),
            rep([c64, c64]), rep([-s64, s64]))


def _layout_weights(norm1_g, norm2_g, w_in, mla_kv_norm_g, mla_w_uk, mla_w_uv, gqa_q_norm_g, gqa_k_norm_g,
                    diff_lambda, diff_norm_g, w_out, moe_w_group, moe_b_group, moe_w_router, moe_b_router,
                    moe_w_gate, moe_w_up, moe_w_down, final_norm_g):
    pieces, start = [], 0
    for size in (384, 128, 32, 384, 128, 128, 256, 256, 256):
        pieces.append(w_in[:, :, start:start + size])
        start += size
    pa, pckv, pkr, pqg, pkg, pvg, pqd, pkd, pvd = pieces
    pad = jnp.zeros((DEPTH, D_MODEL, Z_COLS - Z_KR - MLA_ROPE), w_in.dtype)
    w_in_r = jnp.concatenate([pa, pckv, pqg, pkg, pvg, pqd, pkd, pvd, pkr, pad], axis=-1).astype(bf16)

    eye = jnp.eye(MLA_ROPE, dtype=f32)
    top = jnp.concatenate([mla_w_uk, jnp.zeros((DEPTH, KV_RANK, MLA_HEADS, MLA_ROPE), f32)], axis=-1)
    mid = jnp.concatenate([jnp.zeros((MLA_ROPE, MLA_HEADS, MLA_NOPE), f32),
                           jnp.broadcast_to(eye[:, None, :], (MLA_ROPE, MLA_HEADS, MLA_ROPE))], axis=-1)
    w_ka = jnp.concatenate([top.reshape(DEPTH, KV_RANK, 384),
                            jnp.broadcast_to(mid.reshape(1, MLA_ROPE, 384), (DEPTH, MLA_ROPE, 384)),
                            jnp.zeros((DEPTH, 256 - KV_RANK - MLA_ROPE, 384), f32)], axis=1).astype(bf16)

    seg_id = jnp.arange(512) // HEAD_DIM
    seg = (seg_id[:, None] == seg_id[None, :]).astype(bf16)
    qk_g = jnp.concatenate([jnp.tile(gqa_q_norm_g, (1, GQA_HEADS)), jnp.tile(gqa_k_norm_g, (1, GQA_KV_HEADS))], axis=-1)
    w_r = jnp.concatenate([moe_w_router, moe_w_group,
                           jnp.zeros((DEPTH, D_MODEL, LANES - N_EXPERTS - N_GROUPS), f32)], axis=-1).astype(bf16)
    b_r = jnp.concatenate([moe_b_router, moe_b_group, jnp.zeros((DEPTH, LANES - N_EXPERTS - N_GROUPS), f32)], axis=-1)
    return dict(
        g1=norm1_g.reshape(DEPTH, 1, D_MODEL), g2=norm2_g.reshape(DEPTH, 1, D_MODEL), w_in=w_in_r,
        kv_g=mla_kv_norm_g.reshape(DEPTH, 1, KV_RANK), qk_g=qk_g.reshape(DEPTH, 1, 512), seg=seg, w_ka=w_ka,
        w_uv=mla_w_uv.reshape(DEPTH, KV_RANK, 384).astype(bf16), lam=diff_lambda,
        diff_g=diff_norm_g.reshape(DEPTH, 1, DIFF_V), w_out=w_out.astype(bf16), w_r=w_r,
        b_r=b_r.reshape(DEPTH, 1, LANES), w_gate=moe_w_gate.astype(bf16), w_up=moe_w_up.astype(bf16),
        w_down=moe_w_down.astype(bf16), final_g=final_norm_g.reshape(1, D_MODEL))


PRE_TILE = 512
CTX_ATTN_TILE = 256
LAT_ATTN_TILE = 256
CTX_HEADS_PER_ROUND = 20
LAT_HEADS_PER_ROUND = 5
POST_TILE = 512
MOE_TILE = 1024


def kernel(x_prompt, x_sample, c, cache_mla_ckv, cache_mla_krope, cache_gqa_k, cache_gqa_v, cache_diff_k, cache_diff_v, c_ctx, norm1_g, norm2_g, w_mod, b_mod, w_in, mla_kv_norm_g, mla_w_uk, mla_w_uv, gqa_q_norm_g, gqa_k_norm_g, diff_lambda, diff_norm_g, w_out, moe_w_group, moe_b_group, moe_w_router, moe_b_router, moe_w_gate, moe_w_up, moe_w_down, final_norm_g):
    B, S, _ = x_prompt.shape
    Bl, Sl, _ = x_sample.shape
    assert S == CTX_ATTN_TILE and Bl + 1 <= MOD_ROWS
    w = _layout_weights(norm1_g, norm2_g, w_in, mla_kv_norm_g, mla_w_uk, mla_w_uv, gqa_q_norm_g, gqa_k_norm_g,
                        diff_lambda, diff_norm_g, w_out, moe_w_group, moe_b_group, moe_w_router, moe_b_router,
                        moe_w_gate, moe_w_up, moe_w_down, final_norm_g)
    cond = jnp.concatenate([c_ctx[None, :], c, jnp.zeros((MOD_ROWS - 1 - Bl, D_MODEL), f32)], axis=0)
    mod = _modulation(cond, w_mod, b_mod).reshape(DEPTH, MOD_ROWS, N_MOD, D_MODEL)
    ctx_row = lambda token: 0
    lat_row = lambda token: 1 + token // Sl

    x = x_prompt.reshape(B * S, D_MODEL)
    cache_rows = []
    for i in range(DEPTH):
        q, kv, *cache = _pre(i, x, S, mod, ctx_row, w, PRE_TILE)
        cache_rows.append(cache)
        o = _attention(i, q, kv, S, lambda t: (0, 0, t, 0), w, CTX_ATTN_TILE, CTX_HEADS_PER_ROUND)
        x1, h2, gates = _post(i, o, x, mod, ctx_row, w, POST_TILE)
        x = _moe(i, x1, h2, gates, mod, ctx_row, w, MOE_TILE, final=(i == DEPTH - 1))
    y_prompt = x.reshape(B, S, D_MODEL)
    stacked = [jnp.stack([cache_rows[i][j].reshape(B, S, -1) for i in range(DEPTH)], axis=1) for j in range(6)]
    new_mla_ckv, new_mla_krope = stacked[0], stacked[1]
    new_gqa_k = stacked[2].reshape(B, DEPTH, S, GQA_KV_HEADS, HEAD_DIM)
    new_gqa_v = stacked[3].reshape(B, DEPTH, S, GQA_KV_HEADS, HEAD_DIM)
    new_diff_k = stacked[4].reshape(B, DEPTH, S, DIFF_HEADS, 2, DIFF_QK)
    new_diff_v = stacked[5].reshape(B, DEPTH, S, DIFF_HEADS, DIFF_V)

    tabs = _rope_tables(Sl)
    kv_all = _cache_rows((cache_mla_ckv, cache_mla_krope, cache_gqa_k, cache_gqa_v, cache_diff_k, cache_diff_v), w, Sl)
    x = x_sample.reshape(Bl * Sl, D_MODEL)
    per_b = Sl // LAT_ATTN_TILE
    for i in range(DEPTH):
        q, kv_all = _pre(i, x, Sl, mod, lat_row, w, PRE_TILE, rope_tabs=tabs, kv_buf=kv_all)
        o = _attention(i, q, kv_all, PAST_LEN + Sl, lambda t, i=i: (i, t // per_b, 0, 0), w, LAT_ATTN_TILE,
                       LAT_HEADS_PER_ROUND)
        x1, h2, gates = _post(i, o, x, mod, lat_row, w, POST_TILE)
        x = _moe(i, x1, h2, gates, mod, lat_row, w, MOE_TILE, final=(i == DEPTH - 1))
    y_sample = x.reshape(Bl, Sl, D_MODEL)
    return (y_prompt, y_sample, new_mla_ckv, new_mla_krope, new_gqa_k, new_gqa_v, new_diff_k, new_diff_v)
```

```python
import functools
import math

import jax
import jax.numpy as jnp
import numpy as np
from jax.experimental import pallas as pl
from jax.experimental.pallas import tpu as pltpu

D_MODEL = 1024
DEPTH = 2
PAST_LEN = 512
GRID_W = 64
ROPE_THETA = 10000.0
EPS = 1e-6
LOG2E = 1.4426950408889634
N_MOD = 6
HEAD_DIM = 64
MLA_HEADS = 6
MLA_NOPE = 32
MLA_ROPE = 32
MLA_V = 64
KV_RANK = 128
GQA_HEADS = 6
GQA_KV_HEADS = 2
GQA_GROUP = GQA_HEADS // GQA_KV_HEADS
DIFF_HEADS = 4
DIFF_QK = 32
DIFF_V = 64
N_GROUPS = 4
EXPERTS_PER_GROUP = 4
N_EXPERTS = N_GROUPS * EXPERTS_PER_GROUP
D_FF_EXPERT = 256

LANES = 128
MOD_ROWS = 8

IN_COLS = 1952
IN_KR = 512
Z_QA, Z_CKV, Z_QG, Z_KG, Z_VG, Z_QD, Z_KD, Z_VD, Z_KR = 0, 384, 512, 896, 1024, 1152, 1408, 1664, 1920
Z_COLS = 2048
Q_A, Q_G, Q_D, Q_COLS = 0, 384, 768, 1024
KV_KA, KV_VA, KV_KG, KV_VG, KV_KD, KV_VD, KV_COLS = 0, 384, 768, 896, 1024, 1280, 1536
CACHE_WIDTHS = (128, 32, 128, 128, 256, 256)

VMEM_LIMIT = 56 * 1024 * 1024

bf16 = jnp.bfloat16
f32 = jnp.float32


def _dot(a, b):
    return jnp.dot(a, b, preferred_element_type=f32)


def _dot_nt(a, b):
    return jax.lax.dot_general(a, b, (((1,), (1,)), ((), ())), preferred_element_type=f32)


def _rms(x, width):
    return x * jax.lax.rsqrt(jnp.sum(x * x, axis=-1, keepdims=True) * (1.0 / width) + EPS)


def _silu(x):
    return x * (1.0 / (1.0 + jnp.exp(-x)))


def _params(sem):
    return pltpu.CompilerParams(dimension_semantics=sem, vmem_limit_bytes=VMEM_LIMIT)


def _mod_kernel(cond_ref, w_ref, b_ref, o_ref):
    o_ref[0] = _dot(_silu(cond_ref[...]).astype(bf16), w_ref[0].astype(bf16)) + b_ref[0]


def _modulation(cond, w_mod, b_mod):
    return pl.pallas_call(
        _mod_kernel,
        out_shape=jax.ShapeDtypeStruct((DEPTH, MOD_ROWS, N_MOD * D_MODEL), f32),
        grid=(DEPTH, N_MOD),
        in_specs=[
            pl.BlockSpec((MOD_ROWS, D_MODEL), lambda i, j: (0, 0)),
            pl.BlockSpec((1, D_MODEL, D_MODEL), lambda i, j: (i, 0, j)),
            pl.BlockSpec((1, 1, D_MODEL), lambda i, j: (i, 0, j)),
        ],
        out_specs=pl.BlockSpec((1, MOD_ROWS, D_MODEL), lambda i, j: (i, 0, j)),
        compiler_params=_params(("arbitrary", "arbitrary")),
        name="modulation",
    )(cond, w_mod, b_mod.reshape(DEPTH, 1, N_MOD * D_MODEL))


def _swap_halves(x, half):
    lane = jax.lax.broadcasted_iota(jnp.int32, x.shape, 1)
    fwd = pltpu.roll(x, LANES - half, 1)
    bwd = pltpu.roll(x, half, 1)
    return jnp.where((lane & (2 * half - 1)) < half, fwd, bwd)


def _rope_block(x, cos, sin, half):
    return x * cos + _swap_halves(x, half) * sin


def _pre_kernel(rope, n_prev, *refs):
    it = iter(refs)
    x_ref, mod_ref, g1_ref, w_in_ref, kvg_ref, qkg_ref, seg_ref, wka_ref, wuv_ref = (next(it) for _ in range(9))
    if rope:
        ca_ref, sa_ref, c32_ref, s32_ref, c64_ref, s64_ref = (next(it) for _ in range(6))
    prev_refs = [next(it) for _ in range(n_prev)]
    q_ref, kv_ref = next(it), next(it)
    cache_refs = [] if rope else [next(it) for _ in range(len(CACHE_WIDTHS))]
    w_scr = next(it)

    @pl.when(pl.program_id(0) == 0)
    def _():
        rows = 256
        for r in range(0, D_MODEL, rows):
            wf = w_in_ref[0, r:r + rows, :]
            w_scr[r:r + rows, 0:IN_KR] = wf[:, 0:IN_KR].astype(bf16)
            w_scr[r:r + rows, IN_KR:Z_KR] = wf[:, IN_KR + MLA_ROPE:IN_COLS].astype(bf16)
            w_scr[r:r + rows, Z_KR:Z_KR + MLA_ROPE] = wf[:, IN_KR:IN_KR + MLA_ROPE].astype(bf16)
            w_scr[r:r + rows, Z_KR + MLA_ROPE:Z_COLS] = jnp.zeros((rows, Z_COLS - Z_KR - MLA_ROPE), bf16)

    x = x_ref[...]
    shift1 = mod_ref[0, 0, 0:1, :]
    scale1 = mod_ref[0, 0, 1:2, :]
    h = (_rms(x, D_MODEL) * g1_ref[0]) * (1.0 + scale1) + shift1
    z = _dot(h.astype(bf16), w_scr[...])

    ckv = _rms(z[:, Z_CKV:Z_CKV + KV_RANK], KV_RANK) * kvg_ref[0]

    qk = z[:, Z_QG:Z_VG]
    sq = qk * qk
    sq_hi = sq.astype(bf16)
    sq_lo = (sq - sq_hi.astype(f32)).astype(bf16)
    seg = seg_ref[...]
    ms = (_dot(sq_hi, seg) + _dot(sq_lo, seg)) * (1.0 / HEAD_DIM)
    qk = qk * jax.lax.rsqrt(ms + EPS) * qkg_ref[0]

    def blocks(arr, n):
        return [arr[:, LANES * j:LANES * (j + 1)] for j in range(n)]

    qa = blocks(z[:, Z_QA:Z_QA + 384], 3)
    qkb = blocks(qk, 4)
    qd = blocks(z[:, Z_QD:Z_QD + 256], 2)
    kd = blocks(z[:, Z_KD:Z_KD + 256], 2)
    kr = z[:, Z_KR:Z_KR + LANES]
    if rope:
        ca, sa, c32, s32, c64, s64 = (r[...] for r in (ca_ref, sa_ref, c32_ref, s32_ref, c64_ref, s64_ref))
        qa = [_rope_block(b, ca, sa, MLA_ROPE // 2) for b in qa]
        qkb = [_rope_block(b, c64, s64, HEAD_DIM // 2) for b in qkb]
        qd = [_rope_block(b, c32, s32, DIFF_QK // 2) for b in qd]
        kd = [_rope_block(b, c32, s32, DIFF_QK // 2) for b in kd]
        kr = _rope_block(kr, c32, s32, MLA_ROPE // 2)

    vg = z[:, Z_VG:Z_VG + 128]
    vd = z[:, Z_VD:Z_VD + 256]
    ckv_b = ckv.astype(bf16)
    k_a = _dot(jnp.concatenate([ckv_b, kr.astype(bf16)], axis=1), wka_ref[0])
    v_a = _dot(ckv_b, wuv_ref[0].astype(bf16))

    for j in range(3):
        q_ref[:, Q_A + LANES * j:Q_A + LANES * (j + 1)] = (qa[j] * (HEAD_DIM ** -0.5 * LOG2E)).astype(bf16)
        q_ref[:, Q_G + LANES * j:Q_G + LANES * (j + 1)] = (qkb[j] * (HEAD_DIM ** -0.5 * LOG2E)).astype(bf16)
    for j in range(2):
        q_ref[:, Q_D + LANES * j:Q_D + LANES * (j + 1)] = (qd[j] * (DIFF_QK ** -0.5 * LOG2E)).astype(bf16)
        kv_ref[:, KV_KD + LANES * j:KV_KD + LANES * (j + 1)] = kd[j].astype(bf16)
    kv_ref[:, KV_KA:KV_KA + 384] = k_a.astype(bf16)
    kv_ref[:, KV_VA:KV_VA + 384] = v_a.astype(bf16)
    kv_ref[:, KV_KG:KV_KG + 128] = qkb[3].astype(bf16)
    kv_ref[:, KV_VG:KV_VG + 128] = vg.astype(bf16)
    kv_ref[:, KV_VD:KV_VD + 256] = vd.astype(bf16)
    if not rope:
        rows = [ckv, kr[:, :MLA_ROPE], qkb[3], vg, jnp.concatenate(kd, axis=1), vd]
        if n_prev:
            for prev, out, new in zip(prev_refs, cache_refs, rows):
                reqs, _, seq, width = out.shape
                out[:, 0] = prev[...].reshape(reqs, seq, width)
                out[:, 1] = new.reshape(reqs, seq, width)
        else:
            for out, new in zip(cache_refs, rows):
                out[...] = new


def _pre(layer, x, seq, mod, mod_row, w, tile, rope_tabs=None, prev_cache=()):
    n = x.shape[0]
    rope = rope_tabs is not None
    lay = lambda t: (layer, 0, 0)
    row = lambda t: (t, 0)
    in_specs = [
        pl.BlockSpec((tile, D_MODEL), row),
        pl.BlockSpec((1, 1, N_MOD, D_MODEL), lambda t: (layer, mod_row(t * tile), 0, 0)),
        pl.BlockSpec((1, 1, D_MODEL), lay),
        pl.BlockSpec((1, D_MODEL, IN_COLS), lay),
        pl.BlockSpec((1, 1, KV_RANK), lay),
        pl.BlockSpec((1, 1, 512), lay),
        pl.BlockSpec((512, 512), lambda t: (0, 0)),
        pl.BlockSpec((1, 256, 384), lay),
        pl.BlockSpec((1, KV_RANK, 384), lay),
    ]
    args = [x, mod, w["g1"], w["w_in"], w["kv_g"], w["qk_g"], w["seg"], w["w_ka"], w["w_uv"]]
    if rope:
        per_b = seq // tile
        in_specs += [pl.BlockSpec((tile, LANES), lambda t: (t % per_b, 0))] * 6
        args += list(rope_tabs)
    out_shape = [jax.ShapeDtypeStruct((n, Q_COLS), bf16), jax.ShapeDtypeStruct((n, KV_COLS), bf16)]
    out_specs = [pl.BlockSpec((tile, Q_COLS), row), pl.BlockSpec((tile, KV_COLS), row)]
    if not rope:
        reqs = tile // seq
        for width, prev in zip(CACHE_WIDTHS, prev_cache or (None,) * len(CACHE_WIDTHS)):
            if prev is None:
                out_shape.append(jax.ShapeDtypeStruct((n, width), f32))
                out_specs.append(pl.BlockSpec((tile, width), row))
            else:
                in_specs.append(pl.BlockSpec((tile, width), row))
                args.append(prev)
                out_shape.append(jax.ShapeDtypeStruct((n // seq, DEPTH, seq, width), f32))
                out_specs.append(pl.BlockSpec((reqs, DEPTH, seq, width), lambda t: (t, 0, 0, 0)))
    return pl.pallas_call(
        functools.partial(_pre_kernel, rope, len(prev_cache)),
        out_shape=out_shape,
        grid=(n // tile,),
        in_specs=in_specs,
        out_specs=out_specs,
        scratch_shapes=[pltpu.VMEM((D_MODEL, Z_COLS), bf16)],
        compiler_params=_params(("arbitrary",)),
        name="pre_latent" if rope else "pre_context",
    )(*args)


def _cache_kernel(ckv_ref, kr_ref, kg_ref, vg_ref, kd_ref, vd_ref, wka_ref, wuv_ref, kv_ref):
    ckv_b = ckv_ref[0, 0].astype(bf16)
    wka = wka_ref[0]
    k_a = _dot(ckv_b, wka[:KV_RANK]) + _dot(kr_ref[0, 0].astype(bf16), wka[KV_RANK:KV_RANK + MLA_ROPE])
    kv_ref[0, 0, :, KV_KA:KV_KA + 384] = k_a.astype(bf16)
    kv_ref[0, 0, :, KV_VA:KV_VA + 384] = _dot(ckv_b, wuv_ref[0].astype(bf16)).astype(bf16)
    kv_ref[0, 0, :, KV_KG:KV_KG + 128] = kg_ref[0, 0].astype(bf16)
    kv_ref[0, 0, :, KV_VG:KV_VG + 128] = vg_ref[0, 0].astype(bf16)
    kv_ref[0, 0, :, KV_KD:KV_KD + 256] = kd_ref[0, 0].astype(bf16)
    kv_ref[0, 0, :, KV_VD:KV_VD + 256] = vd_ref[0, 0].astype(bf16)


def _cache_rows(caches, w):
    B = caches[0].shape[0]
    spec = lambda width: pl.BlockSpec((1, 1, PAST_LEN, width), lambda i, b: (b, i, 0, 0))
    return pl.pallas_call(
        _cache_kernel,
        out_shape=jax.ShapeDtypeStruct((DEPTH, B, PAST_LEN, KV_COLS), bf16),
        grid=(DEPTH, B),
        in_specs=[spec(width) for width in CACHE_WIDTHS]
        + [pl.BlockSpec((1, 256, 384), lambda i, b: (i, 0, 0)), pl.BlockSpec((1, KV_RANK, 384), lambda i, b: (i, 0, 0))],
        out_specs=pl.BlockSpec((1, 1, PAST_LEN, KV_COLS), lambda i, b: (i, b, 0, 0)),
        compiler_params=_params(("arbitrary", "arbitrary")),
        name="cache_rows",
    )(*[c.reshape(B, DEPTH, PAST_LEN, width) for c, width in zip(caches, CACHE_WIDTHS)], w["w_ka"], w["w_uv"])


_SCORE_HEADS = (
    [(Q_A + 64 * h, KV_KA + 64 * h, 64, KV_VA + MLA_V * h) for h in range(MLA_HEADS)]
    + [(Q_G + 64 * h, KV_KG + 64 * (h // GQA_GROUP), 64, KV_VG + 64 * (h // GQA_GROUP)) for h in range(GQA_HEADS)]
    + [(Q_D + 64 * h + DIFF_QK * c, KV_KD + 64 * h + DIFF_QK * c, DIFF_QK, KV_VD + DIFF_V * h)
       for h in range(DIFF_HEADS) for c in range(2)])


def _attn_kernel(lam_init, per_round, n_src, q_ref, *refs):
    kv_refs = refs[:n_src]
    lam_ref, dg_ref, o_ref, s_ref, p_ref = refs[n_src:]
    spans, start = [], 0
    for r in kv_refs:
        spans.append((r, start, r.shape[2]))
        start += r.shape[2]

    outs = []
    for first in range(0, len(_SCORE_HEADS), per_round):
        chunk = _SCORE_HEADS[first:first + per_round]
        for j, (q_off, k_off, width, _) in enumerate(chunk):
            for r, lo, size in spans:
                s_ref[j, :, lo:lo + size] = _dot_nt(q_ref[:, q_off:q_off + width], r[0, 0, :, k_off:k_off + width])
        s = s_ref[...]
        p = jnp.exp2(s - jnp.max(s, axis=-1, keepdims=True))
        inv = 1.0 / jnp.sum(p, axis=-1, keepdims=True)
        p_ref[...] = p.astype(bf16)
        for j, (_, _, _, v_off) in enumerate(chunk):
            o = sum(_dot(p_ref[j, :, lo:lo + size], r[0, 0, :, v_off:v_off + DIFF_V]) for r, lo, size in spans)
            outs.append(o * inv[j])

    lp = lam_ref[0]
    e1 = jnp.exp(jnp.sum(lp[0:1] * lp[1:2], axis=-1, keepdims=True))
    e2 = jnp.exp(jnp.sum(lp[2:3] * lp[3:4], axis=-1, keepdims=True))
    lam = e1 - e2 + lam_init
    heads = outs[:MLA_HEADS + GQA_HEADS]
    for h in range(DIFF_HEADS):
        o1, o2 = outs[MLA_HEADS + GQA_HEADS + 2 * h:MLA_HEADS + GQA_HEADS + 2 * h + 2]
        heads.append(_rms(o1 - lam * o2, DIFF_V) * dg_ref[0] * (1.0 - lam_init))
    for j in range(len(heads) // 2):
        o_ref[:, LANES * j:LANES * (j + 1)] = jnp.concatenate(heads[2 * j:2 * j + 2], axis=1).astype(bf16)


def _attention(layer, q, sources, w, tile, per_round):
    n = q.shape[0]
    lam_init = 0.8 - 0.6 * math.exp(-0.3 * layer)
    s_kv = sum(rows for _, rows, _ in sources)
    assert len(_SCORE_HEADS) % per_round == 0
    return pl.pallas_call(
        functools.partial(_attn_kernel, lam_init, per_round, len(sources)),
        out_shape=jax.ShapeDtypeStruct((n, D_MODEL), bf16),
        scratch_shapes=[pltpu.VMEM((per_round, tile, s_kv), f32), pltpu.VMEM((per_round, tile, s_kv), bf16)],
        grid=(n // tile,),
        in_specs=[pl.BlockSpec((tile, Q_COLS), lambda t: (t, 0))]
        + [pl.BlockSpec((1, 1, rows, KV_COLS), index) for _, rows, index in sources]
        + [pl.BlockSpec((1, 4, DIFF_QK), lambda t: (layer, 0, 0)), pl.BlockSpec((1, 1, DIFF_V), lambda t: (layer, 0, 0))],
        out_specs=pl.BlockSpec((tile, D_MODEL), lambda t: (t, 0)),
        compiler_params=_params(("arbitrary",)),
        name="attention",
    )(q, *[arr for arr, _, _ in sources], w["lam"], w["diff_g"])


def _post_kernel(o_ref, x_ref, mod_ref, w_out_ref, g2_ref, wg_ref, bg_ref, we_ref, be_ref,
                 x1_ref, h2_ref, gates_ref, w_scr):
    @pl.when(pl.program_id(0) == 0)
    def _():
        w_scr[...] = w_out_ref[0].astype(bf16)

    gate1 = mod_ref[0, 0, 2:3, :]
    shift2 = mod_ref[0, 0, 3:4, :]
    scale2 = mod_ref[0, 0, 4:5, :]
    x1 = x_ref[...] + gate1 * _dot(o_ref[...], w_scr[...])
    x1_ref[...] = x1
    h2 = ((_rms(x1, D_MODEL) * g2_ref[0]) * (1.0 + scale2) + shift2).astype(bf16)
    h2_ref[...] = h2

    def first_lane(mask, lane_f):
        return jnp.min(jnp.where(mask, lane_f, float(LANES)), axis=-1, keepdims=True)

    gl = _dot(h2, wg_ref[0].astype(bf16)) + bg_ref[0]
    glane = jax.lax.broadcasted_iota(jnp.int32, gl.shape, 1).astype(f32)
    ge = jnp.exp(gl - jnp.max(gl, axis=-1, keepdims=True))
    gprob = ge / jnp.sum(ge, axis=-1, keepdims=True)
    g_top = jnp.max(gprob, axis=-1, keepdims=True)
    g_idx = first_lane(gprob == g_top, glane)

    el = _dot(h2, we_ref[0].astype(bf16)) + be_ref[0]
    lane = jax.lax.broadcasted_iota(jnp.int32, el.shape, 1)
    lane_f = lane.astype(f32)
    emask = (lane >> 2).astype(f32) == g_idx
    em = jnp.where(emask, el, -jnp.inf)
    ee = jnp.where(emask, jnp.exp(em - jnp.max(em, axis=-1, keepdims=True)), 0.0)
    ep = ee / jnp.sum(ee, axis=-1, keepdims=True)
    p1 = jnp.max(jnp.where(emask, ep, -1.0), axis=-1, keepdims=True)
    i1 = first_lane(emask & (ep == p1), lane_f)
    rest = emask & (lane_f != i1)
    p2 = jnp.max(jnp.where(rest, ep, -1.0), axis=-1, keepdims=True)
    i2 = first_lane(rest & (ep == p2), lane_f)
    tot = p1 + p2
    w1 = g_top * (p1 / tot)
    w2 = g_top * (p2 / tot)
    gates_ref[...] = jnp.where(lane_f == i1, w1, 0.0) + jnp.where(lane_f == i2, w2, 0.0)


def _post(layer, o, x, mod, mod_row, w, tile):
    n = o.shape[0]
    lay = lambda t: (layer, 0, 0)
    row = lambda t: (t, 0)
    return pl.pallas_call(
        _post_kernel,
        out_shape=[jax.ShapeDtypeStruct((n, D_MODEL), f32), jax.ShapeDtypeStruct((n, D_MODEL), bf16),
                   jax.ShapeDtypeStruct((n, N_EXPERTS), f32)],
        grid=(n // tile,),
        in_specs=[
            pl.BlockSpec((tile, D_MODEL), row),
            pl.BlockSpec((tile, D_MODEL), row),
            pl.BlockSpec((1, 1, N_MOD, D_MODEL), lambda t: (layer, mod_row(t * tile), 0, 0)),
            pl.BlockSpec((1, D_MODEL, D_MODEL), lay),
            pl.BlockSpec((1, 1, D_MODEL), lay),
            pl.BlockSpec((1, D_MODEL, N_GROUPS), lay),
            pl.BlockSpec((1, 1, N_GROUPS), lay),
            pl.BlockSpec((1, D_MODEL, N_EXPERTS), lay),
            pl.BlockSpec((1, 1, N_EXPERTS), lay),
        ],
        out_specs=[pl.BlockSpec((tile, D_MODEL), row), pl.BlockSpec((tile, D_MODEL), row),
                   pl.BlockSpec((tile, N_EXPERTS), row)],
        scratch_shapes=[pltpu.VMEM((D_MODEL, D_MODEL), bf16)],
        compiler_params=_params(("arbitrary",)),
        name="post_attention",
    )(o, x, mod, w["w_out"], w["g2"], w["w_grp"], w["b_grp"], w["w_rtr"], w["b_rtr"])


def _moe_kernel(final, x1_ref, h2_ref, gates_ref, mod_ref, wg_ref, wu_ref, wd_ref, fg_ref, o_ref, acc_ref):
    e = pl.program_id(1)

    @pl.when(e == 0)
    def _():
        acc_ref[...] = jnp.zeros_like(acc_ref)

    h2 = h2_ref[...]
    gates = gates_ref[...]
    lane = jax.lax.broadcasted_iota(jnp.int32, gates.shape, 1)
    gate = jnp.sum(jnp.where(lane == e, gates, 0.0), axis=-1, keepdims=True)
    a = _dot(h2, wg_ref[0, 0].astype(bf16))
    u = _dot(h2, wu_ref[0, 0].astype(bf16))
    hid = _silu(a) * u * gate
    acc_ref[...] += _dot(hid.astype(bf16), wd_ref[0, 0].astype(bf16))

    @pl.when(e == N_EXPERTS - 1)
    def _():
        x2 = x1_ref[...] + mod_ref[0, 0, 5:6, :] * acc_ref[...]
        if final:
            x2 = _rms(x2, D_MODEL) * fg_ref[...]
        o_ref[...] = x2


def _moe(layer, x1, h2, gates, mod, mod_row, w, tile, final):
    n = x1.shape[0]
    row = lambda t, e: (t, 0)
    return pl.pallas_call(
        functools.partial(_moe_kernel, final),
        out_shape=jax.ShapeDtypeStruct((n, D_MODEL), f32),
        grid=(n // tile, N_EXPERTS),
        in_specs=[
            pl.BlockSpec((tile, D_MODEL), row),
            pl.BlockSpec((tile, D_MODEL), row),
            pl.BlockSpec((tile, N_EXPERTS), row),
            pl.BlockSpec((1, 1, N_MOD, D_MODEL), lambda t, e: (layer, mod_row(t * tile), 0, 0)),
            pl.BlockSpec((1, 1, D_MODEL, D_FF_EXPERT), lambda t, e: (layer, e, 0, 0)),
            pl.BlockSpec((1, 1, D_MODEL, D_FF_EXPERT), lambda t, e: (layer, e, 0, 0)),
            pl.BlockSpec((1, 1, D_FF_EXPERT, D_MODEL), lambda t, e: (layer, e, 0, 0)),
            pl.BlockSpec((1, D_MODEL), lambda t, e: (0, 0)),
        ],
        out_specs=pl.BlockSpec((tile, D_MODEL), row),
        scratch_shapes=[pltpu.VMEM((tile, D_MODEL), f32)],
        compiler_params=_params(("arbitrary", "arbitrary")),
        name="experts",
    )(x1, h2, gates, mod, w["w_gate"], w["w_up"], w["w_down"], w["final_g"])


def _rope_tables(n_tokens):
    pos = np.arange(n_tokens)
    row = (pos // GRID_W).astype(np.float64)
    col = (pos % GRID_W).astype(np.float64)

    def cs(rot_dim):
        quarter = rot_dim // 4
        inv = ROPE_THETA ** (-np.arange(quarter, dtype=np.float64) / quarter)
        ang = np.concatenate([row[:, None] * inv, col[:, None] * inv], axis=-1)
        return np.cos(ang), np.sin(ang)

    c32, s32 = cs(MLA_ROPE)
    c64, s64 = cs(HEAD_DIM)
    ones = np.ones((n_tokens, MLA_NOPE))
    zeros = np.zeros((n_tokens, MLA_NOPE))

    def rep(parts):
        period = np.concatenate(parts, axis=-1)
        return jnp.asarray(np.tile(period, (1, LANES // period.shape[-1])), f32)

    return (rep([ones, c32, c32]), rep([zeros, -s32, s32]), rep([c32, c32]), rep([-s32, s32]),
            rep([c64, c64]), rep([-s64, s64]))


def _layout_weights(norm1_g, norm2_g, w_in, mla_kv_norm_g, mla_w_uk, mla_w_uv, gqa_q_norm_g, gqa_k_norm_g,
                    diff_lambda, diff_norm_g, w_out, moe_w_group, moe_b_group, moe_w_router, moe_b_router,
                    moe_w_gate, moe_w_up, moe_w_down, final_norm_g):
    eye = jnp.eye(MLA_ROPE, dtype=f32)
    top = jnp.concatenate([mla_w_uk, jnp.zeros((DEPTH, KV_RANK, MLA_HEADS, MLA_ROPE), f32)], axis=-1)
    mid = jnp.concatenate([jnp.zeros((MLA_ROPE, MLA_HEADS, MLA_NOPE), f32),
                           jnp.broadcast_to(eye[:, None, :], (MLA_ROPE, MLA_HEADS, MLA_ROPE))], axis=-1)
    w_ka = jnp.concatenate([top.reshape(DEPTH, KV_RANK, 384),
                            jnp.broadcast_to(mid.reshape(1, MLA_ROPE, 384), (DEPTH, MLA_ROPE, 384)),
                            jnp.zeros((DEPTH, 256 - KV_RANK - MLA_ROPE, 384), f32)], axis=1).astype(bf16)
    seg_id = np.arange(512) // HEAD_DIM
    seg = jnp.asarray(seg_id[:, None] == seg_id[None, :], bf16)
    qk_g = jnp.concatenate([jnp.tile(gqa_q_norm_g, (1, GQA_HEADS)), jnp.tile(gqa_k_norm_g, (1, GQA_KV_HEADS))], axis=-1)
    return dict(
        g1=norm1_g.reshape(DEPTH, 1, D_MODEL), g2=norm2_g.reshape(DEPTH, 1, D_MODEL), w_in=w_in,
        kv_g=mla_kv_norm_g.reshape(DEPTH, 1, KV_RANK), qk_g=qk_g.reshape(DEPTH, 1, 512), seg=seg, w_ka=w_ka,
        w_uv=mla_w_uv.reshape(DEPTH, KV_RANK, 384), lam=diff_lambda, diff_g=diff_norm_g.reshape(DEPTH, 1, DIFF_V),
        w_out=w_out, w_grp=moe_w_group, b_grp=moe_b_group.reshape(DEPTH, 1, N_GROUPS), w_rtr=moe_w_router,
        b_rtr=moe_b_router.reshape(DEPTH, 1, N_EXPERTS), w_gate=moe_w_gate, w_up=moe_w_up, w_down=moe_w_down,
        final_g=final_norm_g.reshape(1, D_MODEL))


PRE_TILE = 512
CTX_ATTN_TILE = 256
LAT_ATTN_TILE = 256
CTX_HEADS_PER_ROUND = 20
LAT_HEADS_PER_ROUND = 5
POST_TILE = 512
MOE_TILE = 1024


def kernel(x_prompt, x_sample, c, cache_mla_ckv, cache_mla_krope, cache_gqa_k, cache_gqa_v, cache_diff_k, cache_diff_v, c_ctx, norm1_g, norm2_g, w_mod, b_mod, w_in, mla_kv_norm_g, mla_w_uk, mla_w_uv, gqa_q_norm_g, gqa_k_norm_g, diff_lambda, diff_norm_g, w_out, moe_w_group, moe_b_group, moe_w_router, moe_b_router, moe_w_gate, moe_w_up, moe_w_down, final_norm_g):
    B, S, _ = x_prompt.shape
    Bl, Sl, _ = x_sample.shape
    assert S == CTX_ATTN_TILE and Bl + 1 <= MOD_ROWS and DEPTH == 2
    w = _layout_weights(norm1_g, norm2_g, w_in, mla_kv_norm_g, mla_w_uk, mla_w_uv, gqa_q_norm_g, gqa_k_norm_g,
                        diff_lambda, diff_norm_g, w_out, moe_w_group, moe_b_group, moe_w_router, moe_b_router,
                        moe_w_gate, moe_w_up, moe_w_down, final_norm_g)
    cond = jnp.concatenate([c_ctx[None, :], c, jnp.zeros((MOD_ROWS - 1 - Bl, D_MODEL), f32)], axis=0)
    mod = _modulation(cond, w_mod, b_mod).reshape(DEPTH, MOD_ROWS, N_MOD, D_MODEL)
    ctx_row = lambda token: 0
    lat_row = lambda token: 1 + token // Sl

    x = x_prompt.reshape(B * S, D_MODEL)
    cache = ()
    for i in range(DEPTH):
        q, kv, *cache = _pre(i, x, S, mod, ctx_row, w, PRE_TILE, prev_cache=cache)
        own = (kv.reshape(1, 1, B * S, KV_COLS), S, lambda t: (0, 0, t, 0))
        o = _attention(i, q, [own], w, CTX_ATTN_TILE, CTX_HEADS_PER_ROUND)
        x1, h2, gates = _post(i, o, x, mod, ctx_row, w, POST_TILE)
        x = _moe(i, x1, h2, gates, mod, ctx_row, w, MOE_TILE, final=(i == DEPTH - 1))
    y_prompt = x.reshape(B, S, D_MODEL)
    new_mla_ckv, new_mla_krope = cache[0], cache[1]
    new_gqa_k = cache[2].reshape(B, DEPTH, S, GQA_KV_HEADS, HEAD_DIM)
    new_gqa_v = cache[3].reshape(B, DEPTH, S, GQA_KV_HEADS, HEAD_DIM)
    new_diff_k = cache[4].reshape(B, DEPTH, S, DIFF_HEADS, 2, DIFF_QK)
    new_diff_v = cache[5].reshape(B, DEPTH, S, DIFF_HEADS, DIFF_V)

    tabs = _rope_tables(Sl)
    kv_past = _cache_rows((cache_mla_ckv, cache_mla_krope, cache_gqa_k, cache_gqa_v, cache_diff_k, cache_diff_v), w)
    x = x_sample.reshape(Bl * Sl, D_MODEL)
    per_b = Sl // LAT_ATTN_TILE
    for i in range(DEPTH):
        q, kv = _pre(i, x, Sl, mod, lat_row, w, PRE_TILE, rope_tabs=tabs)
        past = (kv_past, PAST_LEN, lambda t, i=i: (i, t // per_b, 0, 0))
        own = (kv.reshape(1, Bl, Sl, KV_COLS), Sl, lambda t: (0, t // per_b, 0, 0))
        o = _attention(i, q, [past, own], w, LAT_ATTN_TILE, LAT_HEADS_PER_ROUND)
        x1, h2, gates = _post(i, o, x, mod, lat_row, w, POST_TILE)
        x = _moe(i, x1, h2, gates, mod, lat_row, w, MOE_TILE, final=(i == DEPTH - 1))
    y_sample = x.reshape(Bl, Sl, D_MODEL)
    return (y_prompt, y_sample, new_mla_ckv, new_mla_krope, new_gqa_k, new_gqa_v, new_diff_k, new_diff_v)
```

```python
import functools
import math

import jax
import jax.numpy as jnp
import numpy as np
from jax.experimental import pallas as pl
from jax.experimental.pallas import tpu as pltpu
from jax.experimental.pallas import tpu_sc as plsc

D_MODEL = 1024
DEPTH = 2
PAST_LEN = 512
GRID_W = 64
ROPE_THETA = 10000.0
EPS = 1e-6
LOG2E = 1.4426950408889634
N_MOD = 6
HEAD_DIM = 64
MLA_HEADS = 6
MLA_NOPE = 32
MLA_ROPE = 32
MLA_V = 64
KV_RANK = 128
GQA_HEADS = 6
GQA_KV_HEADS = 2
GQA_GROUP = GQA_HEADS // GQA_KV_HEADS
DIFF_HEADS = 4
DIFF_QK = 32
DIFF_V = 64
N_GROUPS = 4
EXPERTS_PER_GROUP = 4
N_EXPERTS = N_GROUPS * EXPERTS_PER_GROUP
D_FF_EXPERT = 256

LANES = 128
MOD_ROWS = 8

IN_COLS = 1952
IN_KR = 512
Z_QA, Z_CKV, Z_QG, Z_KG, Z_VG, Z_QD, Z_KD, Z_VD, Z_KR = 0, 384, 512, 896, 1024, 1152, 1408, 1664, 1920
Z_COLS = 2048
Q_A, Q_G, Q_D, Q_COLS = 0, 384, 768, 1024
KV_KA, KV_VA, KV_KG, KV_VG, KV_KD, KV_VD, KV_COLS = 0, 384, 768, 896, 1024, 1280, 1536
CACHE_WIDTHS = (128, 32, 128, 128, 256, 256)

PAIR_LO = (0, 0, 0, 1, 1, 2)
PAIR_HI = (1, 2, 3, 3, 2, 3)
N_PAIRS = len(PAIR_LO)
N_CLASSES = N_GROUPS * N_PAIRS
HX_COLS = D_MODEL + LANES
EXPERT_TILE = 128

SC_CORES, SC_SUBCORES = 2, 16
SC_CHUNK = 64
SC_ROWS = SC_CORES * SC_SUBCORES * SC_CHUNK

VMEM_LIMIT = 56 * 1024 * 1024

bf16 = jnp.bfloat16
f32 = jnp.float32
i32 = jnp.int32


def _dot(a, b):
    return jnp.dot(a, b, preferred_element_type=f32)


def _dot_nt(a, b):
    return jax.lax.dot_general(a, b, (((1,), (1,)), ((), ())), preferred_element_type=f32)


def _rms(x, width):
    return x * jax.lax.rsqrt(jnp.sum(x * x, axis=-1, keepdims=True) * (1.0 / width) + EPS)


def _silu(x):
    return x * (1.0 / (1.0 + jnp.exp(-x)))


def _params(sem):
    return pltpu.CompilerParams(dimension_semantics=sem, vmem_limit_bytes=VMEM_LIMIT)


def _mod_kernel(cond_ref, w_ref, b_ref, o_ref):
    o_ref[0] = _dot(_silu(cond_ref[...]).astype(bf16), w_ref[0].astype(bf16)) + b_ref[0]


def _modulation(cond, w_mod, b_mod):
    return pl.pallas_call(
        _mod_kernel,
        out_shape=jax.ShapeDtypeStruct((DEPTH, MOD_ROWS, N_MOD * D_MODEL), f32),
        grid=(DEPTH, N_MOD),
        in_specs=[
            pl.BlockSpec((MOD_ROWS, D_MODEL), lambda i, j: (0, 0)),
            pl.BlockSpec((1, D_MODEL, D_MODEL), lambda i, j: (i, 0, j)),
            pl.BlockSpec((1, 1, D_MODEL), lambda i, j: (i, 0, j)),
        ],
        out_specs=pl.BlockSpec((1, MOD_ROWS, D_MODEL), lambda i, j: (i, 0, j)),
        compiler_params=_params(("arbitrary", "arbitrary")),
        name="modulation",
    )(cond, w_mod, b_mod.reshape(DEPTH, 1, N_MOD * D_MODEL))


def _swap_halves(x, half):
    lane = jax.lax.broadcasted_iota(i32, x.shape, 1)
    fwd = pltpu.roll(x, LANES - half, 1)
    bwd = pltpu.roll(x, half, 1)
    return jnp.where((lane & (2 * half - 1)) < half, fwd, bwd)


def _rope_block(x, cos, sin, half):
    return x * cos + _swap_halves(x, half) * sin


def _pre_kernel(rope, n_prev, resid, *refs):
    it = iter(refs)
    x_ref, mod_ref, g1_ref, w_in_ref, kvg_ref, qkg_ref, seg_ref, wka_ref, wuv_ref = (next(it) for _ in range(9))
    if resid:
        y_ref, pmod_ref = next(it), next(it)
    if rope:
        ca_ref, sa_ref, c32_ref, s32_ref, c64_ref, s64_ref = (next(it) for _ in range(6))
    prev_refs = [next(it) for _ in range(n_prev)]
    q_ref, kv_ref = next(it), next(it)
    if resid:
        x2_ref = next(it)
    cache_refs = [] if rope else [next(it) for _ in range(len(CACHE_WIDTHS))]
    w_scr = next(it)

    @pl.when(pl.program_id(0) == 0)
    def _():
        rows = 256
        for r in range(0, D_MODEL, rows):
            wf = w_in_ref[0, r:r + rows, :]
            w_scr[r:r + rows, 0:IN_KR] = wf[:, 0:IN_KR].astype(bf16)
            w_scr[r:r + rows, IN_KR:Z_KR] = wf[:, IN_KR + MLA_ROPE:IN_COLS].astype(bf16)
            w_scr[r:r + rows, Z_KR:Z_KR + MLA_ROPE] = wf[:, IN_KR:IN_KR + MLA_ROPE].astype(bf16)
            w_scr[r:r + rows, Z_KR + MLA_ROPE:Z_COLS] = jnp.zeros((rows, Z_COLS - Z_KR - MLA_ROPE), bf16)

    x = x_ref[...]
    if resid:
        x = x + pmod_ref[0, 0, 5:6, :] * y_ref[...]
        x2_ref[...] = x
    shift1 = mod_ref[0, 0, 0:1, :]
    scale1 = mod_ref[0, 0, 1:2, :]
    h = (_rms(x, D_MODEL) * g1_ref[0]) * (1.0 + scale1) + shift1
    z = _dot(h.astype(bf16), w_scr[...])

    ckv = _rms(z[:, Z_CKV:Z_CKV + KV_RANK], KV_RANK) * kvg_ref[0]

    qk = z[:, Z_QG:Z_VG]
    sq = qk * qk
    sq_hi = sq.astype(bf16)
    sq_lo = (sq - sq_hi.astype(f32)).astype(bf16)
    seg = seg_ref[...]
    ms = (_dot(sq_hi, seg) + _dot(sq_lo, seg)) * (1.0 / HEAD_DIM)
    qk = qk * jax.lax.rsqrt(ms + EPS) * qkg_ref[0]

    def blocks(arr, n):
        return [arr[:, LANES * j:LANES * (j + 1)] for j in range(n)]

    qa = blocks(z[:, Z_QA:Z_QA + 384], 3)
    qkb = blocks(qk, 4)
    qd = blocks(z[:, Z_QD:Z_QD + 256], 2)
    kd = blocks(z[:, Z_KD:Z_KD + 256], 2)
    kr = z[:, Z_KR:Z_KR + LANES]
    if rope:
        ca, sa, c32, s32, c64, s64 = (r[...] for r in (ca_ref, sa_ref, c32_ref, s32_ref, c64_ref, s64_ref))
        qa = [_rope_block(b, ca, sa, MLA_ROPE // 2) for b in qa]
        qkb = [_rope_block(b, c64, s64, HEAD_DIM // 2) for b in qkb]
        qd = [_rope_block(b, c32, s32, DIFF_QK // 2) for b in qd]
        kd = [_rope_block(b, c32, s32, DIFF_QK // 2) for b in kd]
        kr = _rope_block(kr, c32, s32, MLA_ROPE // 2)

    vg = z[:, Z_VG:Z_VG + 128]
    vd = z[:, Z_VD:Z_VD + 256]
    ckv_b = ckv.astype(bf16)
    k_a = _dot(jnp.concatenate([ckv_b, kr.astype(bf16)], axis=1), wka_ref[0])
    v_a = _dot(ckv_b, wuv_ref[0].astype(bf16))

    for j in range(3):
        q_ref[:, Q_A + LANES * j:Q_A + LANES * (j + 1)] = (qa[j] * (HEAD_DIM ** -0.5 * LOG2E)).astype(bf16)
        q_ref[:, Q_G + LANES * j:Q_G + LANES * (j + 1)] = (qkb[j] * (HEAD_DIM ** -0.5 * LOG2E)).astype(bf16)
    for j in range(2):
        q_ref[:, Q_D + LANES * j:Q_D + LANES * (j + 1)] = (qd[j] * (DIFF_QK ** -0.5 * LOG2E)).astype(bf16)
        kv_ref[:, KV_KD + LANES * j:KV_KD + LANES * (j + 1)] = kd[j].astype(bf16)
    kv_ref[:, KV_KA:KV_KA + 384] = k_a.astype(bf16)
    kv_ref[:, KV_VA:KV_VA + 384] = v_a.astype(bf16)
    kv_ref[:, KV_KG:KV_KG + 128] = qkb[3].astype(bf16)
    kv_ref[:, KV_VG:KV_VG + 128] = vg.astype(bf16)
    kv_ref[:, KV_VD:KV_VD + 256] = vd.astype(bf16)
    if not rope:
        rows = [ckv, kr[:, :MLA_ROPE], qkb[3], vg, jnp.concatenate(kd, axis=1), vd]
        if n_prev:
            for prev, out, new in zip(prev_refs, cache_refs, rows):
                reqs, _, seq, width = out.shape
                out[:, 0] = prev[...].reshape(reqs, seq, width)
                out[:, 1] = new.reshape(reqs, seq, width)
        else:
            for out, new in zip(cache_refs, rows):
                out[...] = new


def _pre(layer, x, n, row0, seq, mod, mod_row, w, tile, rope_tabs=None, prev_cache=(), resid=None):
    rope = rope_tabs is not None
    lay = lambda t: (layer, 0, 0)
    row = lambda t: (t, 0)
    off = row0 // tile
    src_row = lambda t: (off + t, 0)
    in_specs = [
        pl.BlockSpec((tile, D_MODEL), src_row),
        pl.BlockSpec((1, 1, N_MOD, D_MODEL), lambda t: (layer, mod_row(t * tile), 0, 0)),
        pl.BlockSpec((1, 1, D_MODEL), lay),
        pl.BlockSpec((1, D_MODEL, IN_COLS), lay),
        pl.BlockSpec((1, 1, KV_RANK), lay),
        pl.BlockSpec((1, 1, 512), lay),
        pl.BlockSpec((512, 512), lambda t: (0, 0)),
        pl.BlockSpec((1, 256, 384), lay),
        pl.BlockSpec((1, KV_RANK, 384), lay),
    ]
    args = [x, mod, w["g1"], w["w_in"], w["kv_g"], w["qk_g"], w["seg"], w["w_ka"], w["w_uv"]]
    if resid is not None:
        in_specs += [pl.BlockSpec((tile, D_MODEL), src_row),
                     pl.BlockSpec((1, 1, N_MOD, D_MODEL), lambda t: (layer - 1, mod_row(t * tile), 0, 0))]
        args += [resid, mod]
    if rope:
        per_b = seq // tile
        in_specs += [pl.BlockSpec((tile, LANES), lambda t: (t % per_b, 0))] * 6
        args += list(rope_tabs)
    out_shape = [jax.ShapeDtypeStruct((n, Q_COLS), bf16), jax.ShapeDtypeStruct((n, KV_COLS), bf16)]
    out_specs = [pl.BlockSpec((tile, Q_COLS), row), pl.BlockSpec((tile, KV_COLS), row)]
    if resid is not None:
        out_shape.append(jax.ShapeDtypeStruct((n, D_MODEL), f32))
        out_specs.append(pl.BlockSpec((tile, D_MODEL), row))
    if not rope:
        reqs = tile // seq
        for width, prev in zip(CACHE_WIDTHS, prev_cache or (None,) * len(CACHE_WIDTHS)):
            if prev is None:
                out_shape.append(jax.ShapeDtypeStruct((n, width), f32))
                out_specs.append(pl.BlockSpec((tile, width), row))
            else:
                in_specs.append(pl.BlockSpec((tile, width), row))
                args.append(prev)
                out_shape.append(jax.ShapeDtypeStruct((n // seq, DEPTH, seq, width), f32))
                out_specs.append(pl.BlockSpec((reqs, DEPTH, seq, width), lambda t: (t, 0, 0, 0)))
    return pl.pallas_call(
        functools.partial(_pre_kernel, rope, len(prev_cache), resid is not None),
        out_shape=out_shape,
        grid=(n // tile,),
        in_specs=in_specs,
        out_specs=out_specs,
        scratch_shapes=[pltpu.VMEM((D_MODEL, Z_COLS), bf16)],
        compiler_params=_params(("arbitrary",)),
        name="pre_latent" if rope else "pre_context",
    )(*args)


def _cache_kernel(ckv_ref, kr_ref, kg_ref, vg_ref, kd_ref, vd_ref, wka_ref, wuv_ref, kv_ref):
    ckv_b = ckv_ref[0, 0].astype(bf16)
    wka = wka_ref[0]
    k_a = _dot(ckv_b, wka[:KV_RANK]) + _dot(kr_ref[0, 0].astype(bf16), wka[KV_RANK:KV_RANK + MLA_ROPE])
    kv_ref[0, 0, :, KV_KA:KV_KA + 384] = k_a.astype(bf16)
    kv_ref[0, 0, :, KV_VA:KV_VA + 384] = _dot(ckv_b, wuv_ref[0].astype(bf16)).astype(bf16)
    kv_ref[0, 0, :, KV_KG:KV_KG + 128] = kg_ref[0, 0].astype(bf16)
    kv_ref[0, 0, :, KV_VG:KV_VG + 128] = vg_ref[0, 0].astype(bf16)
    kv_ref[0, 0, :, KV_KD:KV_KD + 256] = kd_ref[0, 0].astype(bf16)
    kv_ref[0, 0, :, KV_VD:KV_VD + 256] = vd_ref[0, 0].astype(bf16)


def _cache_rows(caches, w):
    B = caches[0].shape[0]
    spec = lambda width: pl.BlockSpec((1, 1, PAST_LEN, width), lambda i, b: (b, i, 0, 0))
    return pl.pallas_call(
        _cache_kernel,
        out_shape=jax.ShapeDtypeStruct((DEPTH, B, PAST_LEN, KV_COLS), bf16),
        grid=(DEPTH, B),
        in_specs=[spec(width) for width in CACHE_WIDTHS]
        + [pl.BlockSpec((1, 256, 384), lambda i, b: (i, 0, 0)), pl.BlockSpec((1, KV_RANK, 384), lambda i, b: (i, 0, 0))],
        out_specs=pl.BlockSpec((1, 1, PAST_LEN, KV_COLS), lambda i, b: (i, b, 0, 0)),
        compiler_params=_params(("arbitrary", "arbitrary")),
        name="cache_rows",
    )(*[c.reshape(B, DEPTH, PAST_LEN, width) for c, width in zip(caches, CACHE_WIDTHS)], w["w_ka"], w["w_uv"])


_SCORE_HEADS = (
    [(Q_A + 64 * h, KV_KA + 64 * h, 64, KV_VA + MLA_V * h) for h in range(MLA_HEADS)]
    + [(Q_G + 64 * h, KV_KG + 64 * (h // GQA_GROUP), 64, KV_VG + 64 * (h // GQA_GROUP)) for h in range(GQA_HEADS)]
    + [(Q_D + 64 * h + DIFF_QK * c, KV_KD + 64 * h + DIFF_QK * c, DIFF_QK, KV_VD + DIFF_V * h)
       for h in range(DIFF_HEADS) for c in range(2)])


def _attn_kernel(lam_init, per_round, n_src, q_ref, *refs):
    kv_refs = refs[:n_src]
    lam_ref, dg_ref, o_ref, s_ref, p_ref = refs[n_src:]
    spans, start = [], 0
    for r in kv_refs:
        spans.append((r, start, r.shape[2]))
        start += r.shape[2]

    outs = []
    for first in range(0, len(_SCORE_HEADS), per_round):
        chunk = _SCORE_HEADS[first:first + per_round]
        for j, (q_off, k_off, width, _) in enumerate(chunk):
            for r, lo, size in spans:
                s_ref[j, :, lo:lo + size] = _dot_nt(q_ref[:, q_off:q_off + width], r[0, 0, :, k_off:k_off + width])
        s = s_ref[...]
        p = jnp.exp2(s - jnp.max(s, axis=-1, keepdims=True))
        inv = 1.0 / jnp.sum(p, axis=-1, keepdims=True)
        p_ref[...] = p.astype(bf16)
        for j, (_, _, _, v_off) in enumerate(chunk):
            o = sum(_dot(p_ref[j, :, lo:lo + size], r[0, 0, :, v_off:v_off + DIFF_V]) for r, lo, size in spans)
            outs.append(o * inv[j])

    lp = lam_ref[0]
    e1 = jnp.exp(jnp.sum(lp[0:1] * lp[1:2], axis=-1, keepdims=True))
    e2 = jnp.exp(jnp.sum(lp[2:3] * lp[3:4], axis=-1, keepdims=True))
    lam = e1 - e2 + lam_init
    heads = outs[:MLA_HEADS + GQA_HEADS]
    for h in range(DIFF_HEADS):
        o1, o2 = outs[MLA_HEADS + GQA_HEADS + 2 * h:MLA_HEADS + GQA_HEADS + 2 * h + 2]
        heads.append(_rms(o1 - lam * o2, DIFF_V) * dg_ref[0] * (1.0 - lam_init))
    for j in range(len(heads) // 2):
        o_ref[:, LANES * j:LANES * (j + 1)] = jnp.concatenate(heads[2 * j:2 * j + 2], axis=1).astype(bf16)


def _attention(layer, q, sources, w, tile, per_round):
    n = q.shape[0]
    lam_init = 0.8 - 0.6 * math.exp(-0.3 * layer)
    s_kv = sum(rows for _, rows, _ in sources)
    assert len(_SCORE_HEADS) % per_round == 0
    return pl.pallas_call(
        functools.partial(_attn_kernel, lam_init, per_round, len(sources)),
        out_shape=jax.ShapeDtypeStruct((n, D_MODEL), bf16),
        scratch_shapes=[pltpu.VMEM((per_round, tile, s_kv), f32), pltpu.VMEM((per_round, tile, s_kv), bf16)],
        grid=(n // tile,),
        in_specs=[pl.BlockSpec((tile, Q_COLS), lambda t: (t, 0))]
        + [pl.BlockSpec((1, 1, rows, KV_COLS), index) for _, rows, index in sources]
        + [pl.BlockSpec((1, 4, DIFF_QK), lambda t: (layer, 0, 0)), pl.BlockSpec((1, 1, DIFF_V), lambda t: (layer, 0, 0))],
        out_specs=pl.BlockSpec((tile, D_MODEL), lambda t: (t, 0)),
        compiler_params=_params(("arbitrary",)),
        name="attention",
    )(q, *[arr for arr, _, _ in sources], w["lam"], w["diff_g"])


def _post_kernel(merge, *refs):
    it = iter(refs)
    o_ref, x_ref, mod_ref, w_out_ref, g2_ref, wg_ref, bg_ref, we_ref, be_ref = (next(it) for _ in range(9))
    if merge:
        next(it), next(it), next(it)
    x1_ref, hx_ref, cls_ref, w_scr = (next(it) for _ in range(4))

    @pl.when(pl.program_id(0) == 0)
    def _():
        w_scr[...] = w_out_ref[0].astype(bf16)

    gate1 = mod_ref[0, 0, 2:3, :]
    shift2 = mod_ref[0, 0, 3:4, :]
    scale2 = mod_ref[0, 0, 4:5, :]
    x1 = x_ref[...] + gate1 * _dot(o_ref[...], w_scr[...])
    x1_ref[...] = x1
    h2 = ((_rms(x1, D_MODEL) * g2_ref[0]) * (1.0 + scale2) + shift2).astype(bf16)
    hx_ref[:, 0:D_MODEL] = h2.astype(f32)

    def first_lane(mask, lane_f):
        return jnp.min(jnp.where(mask, lane_f, float(LANES)), axis=-1, keepdims=True)

    gl = _dot(h2, wg_ref[0].astype(bf16)) + bg_ref[0]
    glane = jax.lax.broadcasted_iota(i32, gl.shape, 1).astype(f32)
    ge = jnp.exp(gl - jnp.max(gl, axis=-1, keepdims=True))
    gprob = ge / jnp.sum(ge, axis=-1, keepdims=True)
    g_top = jnp.max(gprob, axis=-1, keepdims=True)
    g_idx = first_lane(gprob == g_top, glane)

    el = _dot(h2, we_ref[0].astype(bf16)) + be_ref[0]
    lane = jax.lax.broadcasted_iota(i32, el.shape, 1)
    lane_f = lane.astype(f32)
    emask = (lane >> 2).astype(f32) == g_idx
    em = jnp.where(emask, el, -jnp.inf)
    ee = jnp.where(emask, jnp.exp(em - jnp.max(em, axis=-1, keepdims=True)), 0.0)
    ep = ee / jnp.sum(ee, axis=-1, keepdims=True)
    p1 = jnp.max(jnp.where(emask, ep, -1.0), axis=-1, keepdims=True)
    i1 = first_lane(emask & (ep == p1), lane_f)
    rest = emask & (lane_f != i1)
    p2 = jnp.max(jnp.where(rest, ep, -1.0), axis=-1, keepdims=True)
    i2 = first_lane(rest & (ep == p2), lane_f)
    tot = p1 + p2
    w1 = g_top * (p1 / tot)
    w2 = g_top * (p2 / tot)

    lo = jnp.minimum(i1, i2) - EXPERTS_PER_GROUP * g_idx
    hi = jnp.maximum(i1, i2) - EXPERTS_PER_GROUP * g_idx
    pair = jnp.where(lo == 0.0, hi - 1.0, jnp.where(lo == 1.0, jnp.where(hi == 3.0, 3.0, 4.0), 5.0))
    cls_ref[...] = (N_PAIRS * g_idx + pair).astype(i32)
    g_lo = jnp.where(i1 < i2, w1, w2)
    g_hi = jnp.where(i1 < i2, w2, w1)
    tail_lane = jax.lax.broadcasted_iota(i32, (h2.shape[0], LANES), 1)
    hx_ref[:, D_MODEL:HX_COLS] = jnp.where(tail_lane == 0, g_lo, jnp.where(tail_lane == 1, g_hi, 0.0))


def _post(layer, o, x, row0, total, mod, mod_row, w, tile, merged=None):
    n = o.shape[0]
    lay = lambda t: (layer, 0, 0)
    row = lambda t: (t, 0)
    off = row0 // tile
    out_row = lambda t: (off + t, 0)
    in_specs = [
        pl.BlockSpec((tile, D_MODEL), row),
        pl.BlockSpec((tile, D_MODEL), row),
        pl.BlockSpec((1, 1, N_MOD, D_MODEL), lambda t: (layer, mod_row(t * tile), 0, 0)),
        pl.BlockSpec((1, D_MODEL, D_MODEL), lay),
        pl.BlockSpec((1, 1, D_MODEL), lay),
        pl.BlockSpec((1, D_MODEL, N_GROUPS), lay),
        pl.BlockSpec((1, 1, N_GROUPS), lay),
        pl.BlockSpec((1, D_MODEL, N_EXPERTS), lay),
        pl.BlockSpec((1, 1, N_EXPERTS), lay),
    ]
    args = [o, x, mod, w["w_out"], w["g2"], w["w_grp"], w["b_grp"], w["w_rtr"], w["b_rtr"]]
    aliases = {}
    if merged is not None:
        aliases = {len(args) + j: j for j in range(3)}
        in_specs += [pl.BlockSpec(memory_space=pl.ANY)] * 3
        args += list(merged)
    return pl.pallas_call(
        functools.partial(_post_kernel, merged is not None),
        out_shape=[jax.ShapeDtypeStruct((total, D_MODEL), f32), jax.ShapeDtypeStruct((total, HX_COLS), f32),
                   jax.ShapeDtypeStruct((total, 1), i32)],
        grid=(n // tile,),
        in_specs=in_specs,
        out_specs=[pl.BlockSpec((tile, D_MODEL), out_row), pl.BlockSpec((tile, HX_COLS), out_row),
                   pl.BlockSpec((tile, 1), out_row)],
        scratch_shapes=[pltpu.VMEM((D_MODEL, D_MODEL), bf16)],
        input_output_aliases=aliases,
        compiler_params=_params(("arbitrary",)),
        name="post_attention",
    )(*args)


def _plan(cls, max_tiles):
    n = cls.shape[0]
    onehot = (cls[:, None] == jnp.arange(N_CLASSES, dtype=i32)[None, :]).astype(i32)
    csum = jnp.cumsum(onehot, axis=0)
    rank = jnp.sum(onehot * csum, axis=1) - 1
    tiles = (csum[-1] + EXPERT_TILE - 1) // EXPERT_TILE
    ends = jnp.cumsum(tiles)
    slot = (ends - tiles)[cls] * EXPERT_TILE + rank
    src = jnp.zeros((max_tiles * EXPERT_TILE,), i32).at[slot].set(jnp.arange(n, dtype=i32))
    n_tiles = ends[-1:]
    k = jnp.minimum(jnp.arange(max_tiles, dtype=i32), n_tiles - 1)
    tile_cls = jnp.minimum(jnp.searchsorted(ends, k, side="right").astype(i32), N_CLASSES - 1)
    group, pair = tile_cls // N_PAIRS, tile_cls % N_PAIRS
    e_lo = EXPERTS_PER_GROUP * group + jnp.asarray(PAIR_LO, i32)[pair]
    e_hi = EXPERTS_PER_GROUP * group + jnp.asarray(PAIR_HI, i32)[pair]
    return slot, src, e_lo, e_hi, n_tiles


def _gather_rows(src, idx):
    n = idx.shape[0]
    width = src.shape[1]
    workers = SC_CORES * SC_SUBCORES
    per_worker = n // workers
    assert n % SC_ROWS == 0
    mesh = plsc.VectorSubcoreMesh(core_axis_name="c", subcore_axis_name="s")

    @functools.partial(
        pl.kernel, mesh=mesh, out_type=jax.ShapeDtypeStruct((n, width), src.dtype),
        scratch_types=[pltpu.VMEM((SC_CHUNK,), i32), pltpu.VMEM((SC_CHUNK, width), src.dtype),
                       pltpu.SemaphoreType.DMA])
    def gather(src_hbm, idx_hbm, out_hbm, idx_v, rows_v, sem):
        wid = jax.lax.axis_index("s") * SC_CORES + jax.lax.axis_index("c")
        base = wid * per_worker

        @pl.loop(0, per_worker // SC_CHUNK)
        def _(j):
            off = base + j * SC_CHUNK
            pltpu.sync_copy(idx_hbm.at[pl.ds(off, SC_CHUNK)], idx_v)
            pltpu.async_copy(src_hbm.at[idx_v], rows_v, sem).wait()
            pltpu.sync_copy(rows_v, out_hbm.at[pl.ds(off, SC_CHUNK)])

    return gather(src, idx)


def _expert_kernel(lo_ref, hi_ref, nt_ref, xs_ref, wg_lo, wu_lo, wd_lo, wg_hi, wu_hi, wd_hi, ys_ref, *scr):
    k = pl.program_id(0)
    prev = jnp.maximum(k - 1, 0)
    for e_ref, srcs, dsts in ((lo_ref, (wg_lo, wu_lo, wd_lo), scr[0:3]), (hi_ref, (wg_hi, wu_hi, wd_hi), scr[3:6])):
        @pl.when((k == 0) | (e_ref[k] != e_ref[prev]))
        def _():
            for src, dst in zip(srcs, dsts):
                dst[...] = src[0, 0].astype(bf16)

    @pl.when(k < nt_ref[0])
    def _():
        x = xs_ref[:, 0:D_MODEL].astype(bf16)
        gates = xs_ref[:, D_MODEL:HX_COLS]
        y = None
        for lane, (wg, wu, wd) in enumerate((scr[0:3], scr[3:6])):
            hid = _silu(_dot(x, wg[...])) * _dot(x, wu[...]) * gates[:, lane:lane + 1]
            part = _dot(hid.astype(bf16), wd[...])
            y = part if y is None else y + part
        ys_ref[...] = y


def _experts(layer, xs, e_lo, e_hi, n_tiles, w):
    max_tiles = e_lo.shape[0]
    row = lambda k, lo, hi, nt: (jnp.minimum(k, nt[0] - 1), 0)
    up = lambda which: pl.BlockSpec((1, 1, D_MODEL, D_FF_EXPERT),
                                    lambda k, lo, hi, nt: (layer, (lo, hi)[which][k], 0, 0))
    down = lambda which: pl.BlockSpec((1, 1, D_FF_EXPERT, D_MODEL),
                                      lambda k, lo, hi, nt: (layer, (lo, hi)[which][k], 0, 0))
    return pl.pallas_call(
        _expert_kernel,
        out_shape=jax.ShapeDtypeStruct((max_tiles * EXPERT_TILE, D_MODEL), f32),
        grid_spec=pltpu.PrefetchScalarGridSpec(
            num_scalar_prefetch=3, grid=(max_tiles,),
            in_specs=[pl.BlockSpec((EXPERT_TILE, HX_COLS), row), up(0), up(0), down(0), up(1), up(1), down(1)],
            out_specs=pl.BlockSpec((EXPERT_TILE, D_MODEL), row),
            scratch_shapes=[pltpu.VMEM((D_MODEL, D_FF_EXPERT), bf16), pltpu.VMEM((D_MODEL, D_FF_EXPERT), bf16),
                            pltpu.VMEM((D_FF_EXPERT, D_MODEL), bf16)] * 2),
        compiler_params=_params(("arbitrary",)),
        name="experts",
    )(e_lo, e_hi, n_tiles, xs, w["w_gate"], w["w_up"], w["w_down"], w["w_gate"], w["w_up"], w["w_down"])


def _final_kernel(x1_ref, y_ref, mod_ref, fg_ref, o_ref):
    o_ref[...] = _rms(x1_ref[...] + mod_ref[0, 0, 5:6, :] * y_ref[...], D_MODEL) * fg_ref[...]


def _final(x1, y, n, row0, mod, mod_row, w, tile):
    off = row0 // tile
    src_row = lambda t: (off + t, 0)
    return pl.pallas_call(
        _final_kernel,
        out_shape=jax.ShapeDtypeStruct((n, D_MODEL), f32),
        grid=(n // tile,),
        in_specs=[pl.BlockSpec((tile, D_MODEL), src_row), pl.BlockSpec((tile, D_MODEL), src_row),
                  pl.BlockSpec((1, 1, N_MOD, D_MODEL), lambda t: (DEPTH - 1, mod_row(t * tile), 0, 0)),
                  pl.BlockSpec((1, D_MODEL), lambda t: (0, 0))],
        out_specs=pl.BlockSpec((tile, D_MODEL), lambda t: (t, 0)),
        compiler_params=_params(("arbitrary",)),
        name="final_norm",
    )(x1, y, mod, w["final_g"])


def _rope_tables(n_tokens):
    pos = np.arange(n_tokens)
    row = (pos // GRID_W).astype(np.float64)
    col = (pos % GRID_W).astype(np.float64)

    def cs(rot_dim):
        quarter = rot_dim // 4
        inv = ROPE_THETA ** (-np.arange(quarter, dtype=np.float64) / quarter)
        ang = np.concatenate([row[:, None] * inv, col[:, None] * inv], axis=-1)
        return np.cos(ang), np.sin(ang)

    c32, s32 = cs(MLA_ROPE)
    c64, s64 = cs(HEAD_DIM)
    ones = np.ones((n_tokens, MLA_NOPE))
    zeros = np.zeros((n_tokens, MLA_NOPE))

    def rep(parts):
        period = np.concatenate(parts, axis=-1)
        return jnp.asarray(np.tile(period, (1, LANES // period.shape[-1])), f32)

    return (rep([ones, c32, c32]), rep([zeros, -s32, s32]), rep([c32, c32]), rep([-s32, s32]),
            rep([c64, c64]), rep([-s64, s64]))


def _layout_weights(norm1_g, norm2_g, w_in, mla_kv_norm_g, mla_w_uk, mla_w_uv, gqa_q_norm_g, gqa_k_norm_g,
                    diff_lambda, diff_norm_g, w_out, moe_w_group, moe_b_group, moe_w_router, moe_b_router,
                    moe_w_gate, moe_w_up, moe_w_down, final_norm_g):
    eye = jnp.eye(MLA_ROPE, dtype=f32)
    top = jnp.concatenate([mla_w_uk, jnp.zeros((DEPTH, KV_RANK, MLA_HEADS, MLA_ROPE), f32)], axis=-1)
    mid = jnp.concatenate([jnp.zeros((MLA_ROPE, MLA_HEADS, MLA_NOPE), f32),
                           jnp.broadcast_to(eye[:, None, :], (MLA_ROPE, MLA_HEADS, MLA_ROPE))], axis=-1)
    w_ka = jnp.concatenate([top.reshape(DEPTH, KV_RANK, 384),
                            jnp.broadcast_to(mid.reshape(1, MLA_ROPE, 384), (DEPTH, MLA_ROPE, 384)),
                            jnp.zeros((DEPTH, 256 - KV_RANK - MLA_ROPE, 384), f32)], axis=1).astype(bf16)
    seg_id = np.arange(512) // HEAD_DIM
    seg = jnp.asarray(seg_id[:, None] == seg_id[None, :], bf16)
    qk_g = jnp.concatenate([jnp.tile(gqa_q_norm_g, (1, GQA_HEADS)), jnp.tile(gqa_k_norm_g, (1, GQA_KV_HEADS))], axis=-1)
    return dict(
        g1=norm1_g.reshape(DEPTH, 1, D_MODEL), g2=norm2_g.reshape(DEPTH, 1, D_MODEL), w_in=w_in,
        kv_g=mla_kv_norm_g.reshape(DEPTH, 1, KV_RANK), qk_g=qk_g.reshape(DEPTH, 1, 512), seg=seg, w_ka=w_ka,
        w_uv=mla_w_uv.reshape(DEPTH, KV_RANK, 384), lam=diff_lambda, diff_g=diff_norm_g.reshape(DEPTH, 1, DIFF_V),
        w_out=w_out, w_grp=moe_w_group, b_grp=moe_b_group.reshape(DEPTH, 1, N_GROUPS), w_rtr=moe_w_router,
        b_rtr=moe_b_router.reshape(DEPTH, 1, N_EXPERTS), w_gate=moe_w_gate, w_up=moe_w_up, w_down=moe_w_down,
        final_g=final_norm_g.reshape(1, D_MODEL))


PRE_TILE = 512
CTX_ATTN_TILE = 256
LAT_ATTN_TILE = 256
CTX_HEADS_PER_ROUND = 20
LAT_HEADS_PER_ROUND = 5
POST_TILE = 512
FINAL_TILE = 1024


def kernel(x_prompt, x_sample, c, cache_mla_ckv, cache_mla_krope, cache_gqa_k, cache_gqa_v, cache_diff_k, cache_diff_v, c_ctx, norm1_g, norm2_g, w_mod, b_mod, w_in, mla_kv_norm_g, mla_w_uk, mla_w_uv, gqa_q_norm_g, gqa_k_norm_g, diff_lambda, diff_norm_g, w_out, moe_w_group, moe_b_group, moe_w_router, moe_b_router, moe_w_gate, moe_w_up, moe_w_down, final_norm_g):
    B, S, _ = x_prompt.shape
    Bl, Sl, _ = x_sample.shape
    n_ctx, n_lat = B * S, Bl * Sl
    total = n_ctx + n_lat
    assert S == CTX_ATTN_TILE and Bl + 1 <= MOD_ROWS and DEPTH == 2 and total % SC_ROWS == 0
    slot_rows = -(-(total + N_CLASSES * EXPERT_TILE) // SC_ROWS) * SC_ROWS
    max_tiles = slot_rows // EXPERT_TILE
    w = _layout_weights(norm1_g, norm2_g, w_in, mla_kv_norm_g, mla_w_uk, mla_w_uv, gqa_q_norm_g, gqa_k_norm_g,
                        diff_lambda, diff_norm_g, w_out, moe_w_group, moe_b_group, moe_w_router, moe_b_router,
                        moe_w_gate, moe_w_up, moe_w_down, final_norm_g)
    cond = jnp.concatenate([c_ctx[None, :], c, jnp.zeros((MOD_ROWS - 1 - Bl, D_MODEL), f32)], axis=0)
    mod = _modulation(cond, w_mod, b_mod).reshape(DEPTH, MOD_ROWS, N_MOD, D_MODEL)
    ctx_row = lambda token: 0
    lat_row = lambda token: 1 + token // Sl
    tabs = _rope_tables(Sl)
    kv_past = _cache_rows((cache_mla_ckv, cache_mla_krope, cache_gqa_k, cache_gqa_v, cache_diff_k, cache_diff_v), w)
    per_b = Sl // LAT_ATTN_TILE

    x_ctx, x_lat = x_prompt.reshape(n_ctx, D_MODEL), x_sample.reshape(n_lat, D_MODEL)
    cache = ()
    x1 = y = None
    for i in range(DEPTH):
        if i == 0:
            q, kv, *cache = _pre(i, x_ctx, n_ctx, 0, S, mod, ctx_row, w, PRE_TILE)
        else:
            q, kv, x_ctx, *cache = _pre(i, x1, n_ctx, 0, S, mod, ctx_row, w, PRE_TILE, prev_cache=cache, resid=y)
        own = (kv.reshape(1, 1, n_ctx, KV_COLS), S, lambda t: (0, 0, t, 0))
        o = _attention(i, q, [own], w, CTX_ATTN_TILE, CTX_HEADS_PER_ROUND)
        if i == 0:
            q_l, kv_l = _pre(i, x_lat, n_lat, 0, Sl, mod, lat_row, w, PRE_TILE, rope_tabs=tabs)
        else:
            q_l, kv_l, x_lat = _pre(i, x1, n_lat, n_ctx, Sl, mod, lat_row, w, PRE_TILE, rope_tabs=tabs, resid=y)
        past = (kv_past, PAST_LEN, lambda t, i=i: (i, t // per_b, 0, 0))
        own = (kv_l.reshape(1, Bl, Sl, KV_COLS), Sl, lambda t: (0, t // per_b, 0, 0))
        o_l = _attention(i, q_l, [past, own], w, LAT_ATTN_TILE, LAT_HEADS_PER_ROUND)
        merged = _post(i, o, x_ctx, 0, total, mod, ctx_row, w, POST_TILE)
        x1, hx, cls = _post(i, o_l, x_lat, n_ctx, total, mod, lat_row, w, POST_TILE, merged=merged)
        slot, src, e_lo, e_hi, n_tiles = _plan(cls.reshape(total), max_tiles)
        xs = _gather_rows(hx, src)
        ys = _experts(i, xs, e_lo, e_hi, n_tiles, w)
        y = _gather_rows(ys, slot)

    y_prompt = _final(x1, y, n_ctx, 0, mod, ctx_row, w, FINAL_TILE).reshape(B, S, D_MODEL)
    y_sample = _final(x1, y, n_lat, n_ctx, mod, lat_row, w, FINAL_TILE).reshape(Bl, Sl, D_MODEL)
    new_mla_ckv, new_mla_krope = cache[0], cache[1]
    new_gqa_k = cache[2].reshape(B, DEPTH, S, GQA_KV_HEADS, HEAD_DIM)
    new_gqa_v = cache[3].reshape(B, DEPTH, S, GQA_KV_HEADS, HEAD_DIM)
    new_diff_k = cache[4].reshape(B, DEPTH, S, DIFF_HEADS, 2, DIFF_QK)
    new_diff_v = cache[5].reshape(B, DEPTH, S, DIFF_HEADS, DIFF_V)
    return (y_prompt, y_sample, new_mla_ckv, new_mla_krope, new_gqa_k, new_gqa_v, new_diff_k, new_diff_v)
```

```python
import functools
import math

import jax
import jax.numpy as jnp
import numpy as np
from jax.experimental import pallas as pl
from jax.experimental.pallas import tpu as pltpu
from jax.experimental.pallas import tpu_sc as plsc

D_MODEL = 1024
DEPTH = 2
PAST_LEN = 512
GRID_W = 64
ROPE_THETA = 10000.0
EPS = 1e-6
LOG2E = 1.4426950408889634
N_MOD = 6
HEAD_DIM = 64
MLA_HEADS = 6
MLA_NOPE = 32
MLA_ROPE = 32
MLA_V = 64
KV_RANK = 128
GQA_HEADS = 6
GQA_KV_HEADS = 2
GQA_GROUP = GQA_HEADS // GQA_KV_HEADS
DIFF_HEADS = 4
DIFF_QK = 32
DIFF_V = 64
N_GROUPS = 4
EXPERTS_PER_GROUP = 4
N_EXPERTS = N_GROUPS * EXPERTS_PER_GROUP
D_FF_EXPERT = 256

LANES = 128
MOD_ROWS = 8

IN_COLS = 1952
IN_KR = 512
Z_QA, Z_CKV, Z_QG, Z_KG, Z_VG, Z_QD, Z_KD, Z_VD, Z_KR = 0, 384, 512, 896, 1024, 1152, 1408, 1664, 1920
Z_COLS = 2048
Q_A, Q_G, Q_D, Q_COLS = 0, 384, 768, 1024
KV_KA, KV_VA, KV_KG, KV_VG, KV_KD, KV_VD, KV_COLS = 0, 384, 768, 896, 1024, 1280, 1536
CACHE_WIDTHS = (128, 32, 128, 128, 256, 256)

PAIR_LO = (0, 0, 0, 1, 1, 2)
PAIR_HI = (1, 2, 3, 3, 2, 3)
N_PAIRS = len(PAIR_LO)
N_CLASSES = N_GROUPS * N_PAIRS
HX_COLS = D_MODEL + LANES
EXPERT_TILE = 256

SC_CORES, SC_SUBCORES = 2, 16
SC_CHUNK = 64
SC_ROWS = SC_CORES * SC_SUBCORES * SC_CHUNK

VMEM_LIMIT = 56 * 1024 * 1024

bf16 = jnp.bfloat16
f32 = jnp.float32
i32 = jnp.int32


def _dot(a, b):
    return jnp.dot(a, b, preferred_element_type=f32)


def _dot_nt(a, b):
    return jax.lax.dot_general(a, b, (((1,), (1,)), ((), ())), preferred_element_type=f32)


def _rms(x, width):
    return x * jax.lax.rsqrt(jnp.sum(x * x, axis=-1, keepdims=True) * (1.0 / width) + EPS)


def _silu(x):
    return x * (1.0 / (1.0 + jnp.exp(-x)))


def _params(sem):
    return pltpu.CompilerParams(dimension_semantics=sem, vmem_limit_bytes=VMEM_LIMIT)


def _mod_kernel(cond_ref, w_ref, b_ref, o_ref):
    o_ref[0] = _dot(_silu(cond_ref[...]).astype(bf16), w_ref[0].astype(bf16)) + b_ref[0]


def _modulation(cond, w_mod, b_mod):
    return pl.pallas_call(
        _mod_kernel,
        out_shape=jax.ShapeDtypeStruct((DEPTH, MOD_ROWS, N_MOD * D_MODEL), f32),
        grid=(DEPTH, N_MOD),
        in_specs=[
            pl.BlockSpec((MOD_ROWS, D_MODEL), lambda i, j: (0, 0)),
            pl.BlockSpec((1, D_MODEL, D_MODEL), lambda i, j: (i, 0, j)),
            pl.BlockSpec((1, 1, D_MODEL), lambda i, j: (i, 0, j)),
        ],
        out_specs=pl.BlockSpec((1, MOD_ROWS, D_MODEL), lambda i, j: (i, 0, j)),
        compiler_params=_params(("arbitrary", "arbitrary")),
        name="modulation",
    )(cond, w_mod, b_mod.reshape(DEPTH, 1, N_MOD * D_MODEL))


def _swap_halves(x, half):
    lane = jax.lax.broadcasted_iota(i32, x.shape, 1)
    fwd = pltpu.roll(x, LANES - half, 1)
    bwd = pltpu.roll(x, half, 1)
    return jnp.where((lane & (2 * half - 1)) < half, fwd, bwd)


def _rope_block(x, cos, sin, half):
    return x * cos + _swap_halves(x, half) * sin


def _pre_kernel(rope, n_prev, resid, *refs):
    it = iter(refs)
    x_ref, mod_ref, g1_ref, w_in_ref, kvg_ref, qkg_ref, seg_ref, wka_ref, wuv_ref = (next(it) for _ in range(9))
    if resid:
        y_ref, pmod_ref = next(it), next(it)
    if rope:
        ca_ref, sa_ref, c32_ref, s32_ref, c64_ref, s64_ref = (next(it) for _ in range(6))
    prev_refs = [next(it) for _ in range(n_prev)]
    q_ref, kv_ref = next(it), next(it)
    if resid:
        x2_ref = next(it)
    cache_refs = [] if rope else [next(it) for _ in range(len(CACHE_WIDTHS))]
    w_scr = next(it)

    @pl.when(pl.program_id(0) == 0)
    def _():
        rows = 256
        for r in range(0, D_MODEL, rows):
            wf = w_in_ref[0, r:r + rows, :]
            w_scr[r:r + rows, 0:IN_KR] = wf[:, 0:IN_KR].astype(bf16)
            w_scr[r:r + rows, IN_KR:Z_KR] = wf[:, IN_KR + MLA_ROPE:IN_COLS].astype(bf16)
            w_scr[r:r + rows, Z_KR:Z_KR + MLA_ROPE] = wf[:, IN_KR:IN_KR + MLA_ROPE].astype(bf16)
            w_scr[r:r + rows, Z_KR + MLA_ROPE:Z_COLS] = jnp.zeros((rows, Z_COLS - Z_KR - MLA_ROPE), bf16)

    x = x_ref[...]
    if resid:
        x = x + pmod_ref[0, 0, 5:6, :] * y_ref[...]
        x2_ref[...] = x
    shift1 = mod_ref[0, 0, 0:1, :]
    scale1 = mod_ref[0, 0, 1:2, :]
    h = (_rms(x, D_MODEL) * g1_ref[0]) * (1.0 + scale1) + shift1
    z = _dot(h.astype(bf16), w_scr[...])

    ckv = _rms(z[:, Z_CKV:Z_CKV + KV_RANK], KV_RANK) * kvg_ref[0]

    qk = z[:, Z_QG:Z_VG]
    sq = qk * qk
    sq_hi = sq.astype(bf16)
    sq_lo = (sq - sq_hi.astype(f32)).astype(bf16)
    seg = seg_ref[...]
    ms = (_dot(sq_hi, seg) + _dot(sq_lo, seg)) * (1.0 / HEAD_DIM)
    qk = qk * jax.lax.rsqrt(ms + EPS) * qkg_ref[0]

    def blocks(arr, n):
        return [arr[:, LANES * j:LANES * (j + 1)] for j in range(n)]

    qa = blocks(z[:, Z_QA:Z_QA + 384], 3)
    qkb = blocks(qk, 4)
    qd = blocks(z[:, Z_QD:Z_QD + 256], 2)
    kd = blocks(z[:, Z_KD:Z_KD + 256], 2)
    kr = z[:, Z_KR:Z_KR + LANES]
    if rope:
        ca, sa, c32, s32, c64, s64 = (r[...] for r in (ca_ref, sa_ref, c32_ref, s32_ref, c64_ref, s64_ref))
        qa = [_rope_block(b, ca, sa, MLA_ROPE // 2) for b in qa]
        qkb = [_rope_block(b, c64, s64, HEAD_DIM // 2) for b in qkb]
        qd = [_rope_block(b, c32, s32, DIFF_QK // 2) for b in qd]
        kd = [_rope_block(b, c32, s32, DIFF_QK // 2) for b in kd]
        kr = _rope_block(kr, c32, s32, MLA_ROPE // 2)

    vg = z[:, Z_VG:Z_VG + 128]
    vd = z[:, Z_VD:Z_VD + 256]
    ckv_b = ckv.astype(bf16)
    k_a = _dot(jnp.concatenate([ckv_b, kr.astype(bf16)], axis=1), wka_ref[0])
    v_a = _dot(ckv_b, wuv_ref[0].astype(bf16))

    for j in range(3):
        q_ref[:, Q_A + LANES * j:Q_A + LANES * (j + 1)] = (qa[j] * (HEAD_DIM ** -0.5 * LOG2E)).astype(bf16)
        q_ref[:, Q_G + LANES * j:Q_G + LANES * (j + 1)] = (qkb[j] * (HEAD_DIM ** -0.5 * LOG2E)).astype(bf16)
    for j in range(2):
        q_ref[:, Q_D + LANES * j:Q_D + LANES * (j + 1)] = (qd[j] * (DIFF_QK ** -0.5 * LOG2E)).astype(bf16)
        kv_ref[:, KV_KD + LANES * j:KV_KD + LANES * (j + 1)] = kd[j].astype(bf16)
    kv_ref[:, KV_KA:KV_KA + 384] = k_a.astype(bf16)
    kv_ref[:, KV_VA:KV_VA + 384] = v_a.astype(bf16)
    kv_ref[:, KV_KG:KV_KG + 128] = qkb[3].astype(bf16)
    kv_ref[:, KV_VG:KV_VG + 128] = vg.astype(bf16)
    kv_ref[:, KV_VD:KV_VD + 256] = vd.astype(bf16)
    if not rope:
        rows = [ckv, kr[:, :MLA_ROPE], qkb[3], vg, jnp.concatenate(kd, axis=1), vd]
        if n_prev:
            for prev, out, new in zip(prev_refs, cache_refs, rows):
                reqs, _, seq, width = out.shape
                out[:, 0] = prev[...].reshape(reqs, seq, width)
                out[:, 1] = new.reshape(reqs, seq, width)
        else:
            for out, new in zip(cache_refs, rows):
                out[...] = new


def _pre(layer, x, n, row0, seq, mod, mod_row, w, tile, rope_tabs=None, prev_cache=(), resid=None):
    rope = rope_tabs is not None
    lay = lambda t: (layer, 0, 0)
    row = lambda t: (t, 0)
    off = row0 // tile
    src_row = lambda t: (off + t, 0)
    in_specs = [
        pl.BlockSpec((tile, D_MODEL), src_row),
        pl.BlockSpec((1, 1, N_MOD, D_MODEL), lambda t: (layer, mod_row(t * tile), 0, 0)),
        pl.BlockSpec((1, 1, D_MODEL), lay),
        pl.BlockSpec((1, D_MODEL, IN_COLS), lay),
        pl.BlockSpec((1, 1, KV_RANK), lay),
        pl.BlockSpec((1, 1, 512), lay),
        pl.BlockSpec((512, 512), lambda t: (0, 0)),
        pl.BlockSpec((1, 256, 384), lay),
        pl.BlockSpec((1, KV_RANK, 384), lay),
    ]
    args = [x, mod, w["g1"], w["w_in"], w["kv_g"], w["qk_g"], w["seg"], w["w_ka"], w["w_uv"]]
    if resid is not None:
        in_specs += [pl.BlockSpec((tile, D_MODEL), src_row),
                     pl.BlockSpec((1, 1, N_MOD, D_MODEL), lambda t: (layer - 1, mod_row(t * tile), 0, 0))]
        args += [resid, mod]
    if rope:
        per_b = seq // tile
        in_specs += [pl.BlockSpec((tile, LANES), lambda t: (t % per_b, 0))] * 6
        args += list(rope_tabs)
    out_shape = [jax.ShapeDtypeStruct((n, Q_COLS), bf16), jax.ShapeDtypeStruct((n, KV_COLS), bf16)]
    out_specs = [pl.BlockSpec((tile, Q_COLS), row), pl.BlockSpec((tile, KV_COLS), row)]
    if resid is not None:
        out_shape.append(jax.ShapeDtypeStruct((n, D_MODEL), f32))
        out_specs.append(pl.BlockSpec((tile, D_MODEL), row))
    if not rope:
        reqs = tile // seq
        for width, prev in zip(CACHE_WIDTHS, prev_cache or (None,) * len(CACHE_WIDTHS)):
            if prev is None:
                out_shape.append(jax.ShapeDtypeStruct((n, width), f32))
                out_specs.append(pl.BlockSpec((tile, width), row))
            else:
                in_specs.append(pl.BlockSpec((tile, width), row))
                args.append(prev)
                out_shape.append(jax.ShapeDtypeStruct((n // seq, DEPTH, seq, width), f32))
                out_specs.append(pl.BlockSpec((reqs, DEPTH, seq, width), lambda t: (t, 0, 0, 0)))
    return pl.pallas_call(
        functools.partial(_pre_kernel, rope, len(prev_cache), resid is not None),
        out_shape=out_shape,
        grid=(n // tile,),
        in_specs=in_specs,
        out_specs=out_specs,
        scratch_shapes=[pltpu.VMEM((D_MODEL, Z_COLS), bf16)],
        compiler_params=_params(("arbitrary",)),
        name="pre_latent" if rope else "pre_context",
    )(*args)


def _cache_kernel(ckv_ref, kr_ref, kg_ref, vg_ref, kd_ref, vd_ref, wka_ref, wuv_ref, kv_ref):
    ckv_b = ckv_ref[0, 0].astype(bf16)
    wka = wka_ref[0]
    k_a = _dot(ckv_b, wka[:KV_RANK]) + _dot(kr_ref[0, 0].astype(bf16), wka[KV_RANK:KV_RANK + MLA_ROPE])
    kv_ref[0, 0, :, KV_KA:KV_KA + 384] = k_a.astype(bf16)
    kv_ref[0, 0, :, KV_VA:KV_VA + 384] = _dot(ckv_b, wuv_ref[0].astype(bf16)).astype(bf16)
    kv_ref[0, 0, :, KV_KG:KV_KG + 128] = kg_ref[0, 0].astype(bf16)
    kv_ref[0, 0, :, KV_VG:KV_VG + 128] = vg_ref[0, 0].astype(bf16)
    kv_ref[0, 0, :, KV_KD:KV_KD + 256] = kd_ref[0, 0].astype(bf16)
    kv_ref[0, 0, :, KV_VD:KV_VD + 256] = vd_ref[0, 0].astype(bf16)


def _cache_rows(caches, w):
    B = caches[0].shape[0]
    spec = lambda width: pl.BlockSpec((1, 1, PAST_LEN, width), lambda i, b: (b, i, 0, 0))
    return pl.pallas_call(
        _cache_kernel,
        out_shape=jax.ShapeDtypeStruct((DEPTH, B, PAST_LEN, KV_COLS), bf16),
        grid=(DEPTH, B),
        in_specs=[spec(width) for width in CACHE_WIDTHS]
        + [pl.BlockSpec((1, 256, 384), lambda i, b: (i, 0, 0)), pl.BlockSpec((1, KV_RANK, 384), lambda i, b: (i, 0, 0))],
        out_specs=pl.BlockSpec((1, 1, PAST_LEN, KV_COLS), lambda i, b: (i, b, 0, 0)),
        compiler_params=_params(("arbitrary", "arbitrary")),
        name="cache_rows",
    )(*[c.reshape(B, DEPTH, PAST_LEN, width) for c, width in zip(caches, CACHE_WIDTHS)], w["w_ka"], w["w_uv"])


_SCORE_HEADS = (
    [(Q_A + 64 * h, KV_KA + 64 * h, 64, KV_VA + MLA_V * h) for h in range(MLA_HEADS)]
    + [(Q_G + 64 * h, KV_KG + 64 * (h // GQA_GROUP), 64, KV_VG + 64 * (h // GQA_GROUP)) for h in range(GQA_HEADS)]
    + [(Q_D + 64 * h + DIFF_QK * c, KV_KD + 64 * h + DIFF_QK * c, DIFF_QK, KV_VD + DIFF_V * h)
       for h in range(DIFF_HEADS) for c in range(2)])


def _attn_kernel(lam_init, per_round, n_src, q_ref, *refs):
    kv_refs = refs[:n_src]
    lam_ref, dg_ref, o_ref, s_ref, p_ref = refs[n_src:]
    spans, start = [], 0
    for r in kv_refs:
        spans.append((r, start, r.shape[2]))
        start += r.shape[2]

    outs = []
    for first in range(0, len(_SCORE_HEADS), per_round):
        chunk = _SCORE_HEADS[first:first + per_round]
        for j, (q_off, k_off, width, _) in enumerate(chunk):
            for r, lo, size in spans:
                s_ref[j, :, lo:lo + size] = _dot_nt(q_ref[:, q_off:q_off + width], r[0, 0, :, k_off:k_off + width])
        s = s_ref[...]
        p = jnp.exp2(s - jnp.max(s, axis=-1, keepdims=True))
        inv = 1.0 / jnp.sum(p, axis=-1, keepdims=True)
        p_ref[...] = p.astype(bf16)
        for j, (_, _, _, v_off) in enumerate(chunk):
            o = sum(_dot(p_ref[j, :, lo:lo + size], r[0, 0, :, v_off:v_off + DIFF_V]) for r, lo, size in spans)
            outs.append(o * inv[j])

    lp = lam_ref[0]
    e1 = jnp.exp(jnp.sum(lp[0:1] * lp[1:2], axis=-1, keepdims=True))
    e2 = jnp.exp(jnp.sum(lp[2:3] * lp[3:4], axis=-1, keepdims=True))
    lam = e1 - e2 + lam_init
    heads = outs[:MLA_HEADS + GQA_HEADS]
    for h in range(DIFF_HEADS):
        o1, o2 = outs[MLA_HEADS + GQA_HEADS + 2 * h:MLA_HEADS + GQA_HEADS + 2 * h + 2]
        heads.append(_rms(o1 - lam * o2, DIFF_V) * dg_ref[0] * (1.0 - lam_init))
    for j in range(len(heads) // 2):
        o_ref[:, LANES * j:LANES * (j + 1)] = jnp.concatenate(heads[2 * j:2 * j + 2], axis=1).astype(bf16)


def _attention(layer, q, sources, w, tile, per_round):
    n = q.shape[0]
    lam_init = 0.8 - 0.6 * math.exp(-0.3 * layer)
    s_kv = sum(rows for _, rows, _ in sources)
    assert len(_SCORE_HEADS) % per_round == 0
    return pl.pallas_call(
        functools.partial(_attn_kernel, lam_init, per_round, len(sources)),
        out_shape=jax.ShapeDtypeStruct((n, D_MODEL), bf16),
        scratch_shapes=[pltpu.VMEM((per_round, tile, s_kv), f32), pltpu.VMEM((per_round, tile, s_kv), bf16)],
        grid=(n // tile,),
        in_specs=[pl.BlockSpec((tile, Q_COLS), lambda t: (t, 0))]
        + [pl.BlockSpec((1, 1, rows, KV_COLS), index) for _, rows, index in sources]
        + [pl.BlockSpec((1, 4, DIFF_QK), lambda t: (layer, 0, 0)), pl.BlockSpec((1, 1, DIFF_V), lambda t: (layer, 0, 0))],
        out_specs=pl.BlockSpec((tile, D_MODEL), lambda t: (t, 0)),
        compiler_params=_params(("arbitrary",)),
        name="attention",
    )(q, *[arr for arr, _, _ in sources], w["lam"], w["diff_g"])


def _post_kernel(merge, *refs):
    it = iter(refs)
    o_ref, x_ref, mod_ref, w_out_ref, g2_ref, wg_ref, bg_ref, we_ref, be_ref = (next(it) for _ in range(9))
    if merge:
        next(it), next(it), next(it)
    x1_ref, hx_ref, cls_ref, w_scr = (next(it) for _ in range(4))

    @pl.when(pl.program_id(0) == 0)
    def _():
        w_scr[...] = w_out_ref[0].astype(bf16)

    gate1 = mod_ref[0, 0, 2:3, :]
    shift2 = mod_ref[0, 0, 3:4, :]
    scale2 = mod_ref[0, 0, 4:5, :]
    x1 = x_ref[...] + gate1 * _dot(o_ref[...], w_scr[...])
    x1_ref[...] = x1
    h2 = ((_rms(x1, D_MODEL) * g2_ref[0]) * (1.0 + scale2) + shift2).astype(bf16)
    hx_ref[:, 0:D_MODEL] = h2.astype(f32)

    def first_lane(mask, lane_f):
        return jnp.min(jnp.where(mask, lane_f, float(LANES)), axis=-1, keepdims=True)

    gl = _dot(h2, wg_ref[0].astype(bf16)) + bg_ref[0]
    glane = jax.lax.broadcasted_iota(i32, gl.shape, 1).astype(f32)
    ge = jnp.exp(gl - jnp.max(gl, axis=-1, keepdims=True))
    gprob = ge / jnp.sum(ge, axis=-1, keepdims=True)
    g_top = jnp.max(gprob, axis=-1, keepdims=True)
    g_idx = first_lane(gprob == g_top, glane)

    el = _dot(h2, we_ref[0].astype(bf16)) + be_ref[0]
    lane = jax.lax.broadcasted_iota(i32, el.shape, 1)
    lane_f = lane.astype(f32)
    emask = (lane >> 2).astype(f32) == g_idx
    em = jnp.where(emask, el, -jnp.inf)
    ee = jnp.where(emask, jnp.exp(em - jnp.max(em, axis=-1, keepdims=True)), 0.0)
    ep = ee / jnp.sum(ee, axis=-1, keepdims=True)
    p1 = jnp.max(jnp.where(emask, ep, -1.0), axis=-1, keepdims=True)
    i1 = first_lane(emask & (ep == p1), lane_f)
    rest = emask & (lane_f != i1)
    p2 = jnp.max(jnp.where(rest, ep, -1.0), axis=-1, keepdims=True)
    i2 = first_lane(rest & (ep == p2), lane_f)
    tot = p1 + p2
    w1 = g_top * (p1 / tot)
    w2 = g_top * (p2 / tot)

    lo = jnp.minimum(i1, i2) - EXPERTS_PER_GROUP * g_idx
    hi = jnp.maximum(i1, i2) - EXPERTS_PER_GROUP * g_idx
    pair = jnp.where(lo == 0.0, hi - 1.0, jnp.where(lo == 1.0, jnp.where(hi == 3.0, 3.0, 4.0), 5.0))
    cls_ref[...] = (N_PAIRS * g_idx + pair).astype(i32)
    g_lo = jnp.where(i1 < i2, w1, w2)
    g_hi = jnp.where(i1 < i2, w2, w1)
    tail_lane = jax.lax.broadcasted_iota(i32, (h2.shape[0], LANES), 1)
    hx_ref[:, D_MODEL:HX_COLS] = jnp.where(tail_lane == 0, g_lo, jnp.where(tail_lane == 1, g_hi, 0.0))


def _post(layer, o, x, row0, total, mod, mod_row, w, tile, merged=None):
    n = o.shape[0]
    lay = lambda t: (layer, 0, 0)
    row = lambda t: (t, 0)
    off = row0 // tile
    out_row = lambda t: (off + t, 0)
    in_specs = [
        pl.BlockSpec((tile, D_MODEL), row),
        pl.BlockSpec((tile, D_MODEL), row),
        pl.BlockSpec((1, 1, N_MOD, D_MODEL), lambda t: (layer, mod_row(t * tile), 0, 0)),
        pl.BlockSpec((1, D_MODEL, D_MODEL), lay),
        pl.BlockSpec((1, 1, D_MODEL), lay),
        pl.BlockSpec((1, D_MODEL, N_GROUPS), lay),
        pl.BlockSpec((1, 1, N_GROUPS), lay),
        pl.BlockSpec((1, D_MODEL, N_EXPERTS), lay),
        pl.BlockSpec((1, 1, N_EXPERTS), lay),
    ]
    args = [o, x, mod, w["w_out"], w["g2"], w["w_grp"], w["b_grp"], w["w_rtr"], w["b_rtr"]]
    aliases = {}
    if merged is not None:
        aliases = {len(args) + j: j for j in range(3)}
        in_specs += [pl.BlockSpec(memory_space=pl.ANY)] * 3
        args += list(merged)
    return pl.pallas_call(
        functools.partial(_post_kernel, merged is not None),
        out_shape=[jax.ShapeDtypeStruct((total, D_MODEL), f32), jax.ShapeDtypeStruct((total, HX_COLS), f32),
                   jax.ShapeDtypeStruct((total, 1), i32)],
        grid=(n // tile,),
        in_specs=in_specs,
        out_specs=[pl.BlockSpec((tile, D_MODEL), out_row), pl.BlockSpec((tile, HX_COLS), out_row),
                   pl.BlockSpec((tile, 1), out_row)],
        scratch_shapes=[pltpu.VMEM((D_MODEL, D_MODEL), bf16)],
        input_output_aliases=aliases,
        compiler_params=_params(("arbitrary",)),
        name="post_attention",
    )(*args)


PLAN_CHUNK = 256
TAB_ROWS = LANES


def _plan_kernel(cls_ref, slot_ref, tab_ref, rank_scr):
    n = cls_ref.shape[0]
    lane = jax.lax.broadcasted_iota(i32, (PLAN_CHUNK, LANES), 1)
    r = jax.lax.broadcasted_iota(i32, (PLAN_CHUNK, PLAN_CHUNK), 0)
    c = jax.lax.broadcasted_iota(i32, (PLAN_CHUNK, PLAN_CHUNK), 1)
    before = (c < r).astype(bf16)

    def count(b, seen):
        rows = pl.ds(pl.multiple_of(b * PLAN_CHUNK, PLAN_CHUNK), PLAN_CHUNK)
        onehot = (cls_ref[rows, :] == lane).astype(f32)
        ahead = _dot(before, onehot.astype(bf16)) + seen
        rank_scr[rows, :] = jnp.sum(onehot * ahead, axis=-1, keepdims=True)
        return seen + jnp.sum(onehot, axis=0, keepdims=True)

    counts = jax.lax.fori_loop(0, n // PLAN_CHUNK, count, jnp.zeros((1, LANES), f32))
    tiles = jnp.floor((counts + (EXPERT_TILE - 1)) * (1.0 / EXPERT_TILE))
    rr = jax.lax.broadcasted_iota(i32, (LANES, LANES), 0)
    cc = jax.lax.broadcasted_iota(i32, (LANES, LANES), 1)
    ends = _dot(jnp.broadcast_to(tiles, (8, LANES)).astype(bf16), (rr <= cc).astype(bf16))[0:1]
    starts = ends - tiles

    def place(b, carry):
        rows = pl.ds(pl.multiple_of(b * PLAN_CHUNK, PLAN_CHUNK), PLAN_CHUNK)
        first = jnp.sum(jnp.where(cls_ref[rows, :] == lane, starts, 0.0), axis=-1, keepdims=True)
        slot_ref[rows, :] = (first * EXPERT_TILE + rank_scr[rows, :]).astype(i32)
        return carry

    jax.lax.fori_loop(0, n // PLAN_CHUNK, place, 0)

    tl = jax.lax.broadcasted_iota(i32, (TAB_ROWS, LANES), 1)
    n_tiles = jnp.sum(jnp.where(tl[0:1] == N_CLASSES - 1, ends, 0.0), axis=-1, keepdims=True)
    k = jnp.minimum(jax.lax.broadcasted_iota(i32, (TAB_ROWS, 1), 0).astype(f32), n_tiles - 1.0)
    cls_k = jnp.sum(jnp.where((tl < N_CLASSES) & (ends <= k), 1.0, 0.0), axis=-1, keepdims=True)
    cls_k = jnp.minimum(cls_k, N_CLASSES - 1.0)
    mine = tl.astype(f32) == cls_k
    used = jnp.sum(jnp.where(mine, counts, 0.0), axis=-1, keepdims=True)
    first = jnp.sum(jnp.where(mine, starts, 0.0), axis=-1, keepdims=True)
    valid = jnp.clip(used - (k - first) * EXPERT_TILE, 0.0, float(EXPERT_TILE))
    group = jnp.floor((cls_k + 0.5) * (1.0 / N_PAIRS))
    pair = cls_k - N_PAIRS * group
    lo = hi = jnp.zeros_like(pair)
    for p in range(N_PAIRS):
        lo = jnp.where(pair == p, float(PAIR_LO[p]), lo)
        hi = jnp.where(pair == p, float(PAIR_HI[p]), hi)
    e_lo = EXPERTS_PER_GROUP * group + lo
    e_hi = EXPERTS_PER_GROUP * group + hi
    tab = jnp.where(tl == 0, e_lo, jnp.where(tl == 1, e_hi, jnp.where(tl == 2, valid, jnp.where(tl == 3, n_tiles, 0.0))))
    tab_ref[...] = tab.astype(i32)


def _plan(cls, max_tiles):
    n = cls.shape[0]
    assert n % PLAN_CHUNK == 0 and max_tiles <= TAB_ROWS
    slot, tab = pl.pallas_call(
        _plan_kernel,
        out_shape=[jax.ShapeDtypeStruct((n, 1), i32), jax.ShapeDtypeStruct((TAB_ROWS, LANES), i32)],
        scratch_shapes=[pltpu.VMEM((n, 1), f32)],
        compiler_params=_params(None),
        name="dispatch_plan",
    )(cls)
    return slot.reshape(n), tab[:max_tiles, 0], tab[:max_tiles, 1], tab[:max_tiles, 2], tab[0, 3:4]


def _move_rows(src, idx, n_out, scatter):
    n = idx.shape[0]
    width = src.shape[1]
    per_worker = n // (SC_CORES * SC_SUBCORES)
    assert n % SC_ROWS == 0
    mesh = plsc.VectorSubcoreMesh(core_axis_name="c", subcore_axis_name="s")

    @functools.partial(
        pl.kernel, mesh=mesh, out_type=jax.ShapeDtypeStruct((n_out, width), src.dtype),
        scratch_types=[pltpu.VMEM((SC_CHUNK,), i32), pltpu.VMEM((SC_CHUNK, width), src.dtype),
                       pltpu.SemaphoreType.DMA])
    def move(src_hbm, idx_hbm, out_hbm, idx_v, rows_v, sem):
        wid = jax.lax.axis_index("s") * SC_CORES + jax.lax.axis_index("c")
        base = wid * per_worker

        @pl.loop(0, per_worker // SC_CHUNK)
        def _(j):
            off = base + j * SC_CHUNK
            pltpu.sync_copy(idx_hbm.at[pl.ds(off, SC_CHUNK)], idx_v)
            if scatter:
                pltpu.sync_copy(src_hbm.at[pl.ds(off, SC_CHUNK)], rows_v)
                pltpu.async_copy(rows_v, out_hbm.at[idx_v], sem).wait()
            else:
                pltpu.async_copy(src_hbm.at[idx_v], rows_v, sem).wait()
                pltpu.sync_copy(rows_v, out_hbm.at[pl.ds(off, SC_CHUNK)])

    return move(src, idx)


def _expert_kernel(lo_ref, hi_ref, valid_ref, nt_ref, xs_ref, wg_lo, wu_lo, wd_lo, wg_hi, wu_hi, wd_hi, ys_ref, *scr):
    k = pl.program_id(0)
    prev = jnp.maximum(k - 1, 0)
    for e_ref, srcs, dsts in ((lo_ref, (wg_lo, wu_lo, wd_lo), scr[0:3]), (hi_ref, (wg_hi, wu_hi, wd_hi), scr[3:6])):
        @pl.when((k == 0) | (e_ref[k] != e_ref[prev]))
        def _():
            for src, dst in zip(srcs, dsts):
                dst[...] = src[0, 0].astype(bf16)

    @pl.when(k < nt_ref[0])
    def _():
        live = jax.lax.broadcasted_iota(i32, (EXPERT_TILE, 1), 0) < valid_ref[k]
        x = jnp.where(live, xs_ref[:, 0:D_MODEL], 0.0).astype(bf16)
        gates = jnp.where(live, xs_ref[:, D_MODEL:HX_COLS], 0.0)
        y = None
        for lane, (wg, wu, wd) in enumerate((scr[0:3], scr[3:6])):
            hid = _silu(_dot(x, wg[...])) * _dot(x, wu[...]) * gates[:, lane:lane + 1]
            part = _dot(hid.astype(bf16), wd[...])
            y = part if y is None else y + part
        ys_ref[...] = y


def _experts(layer, xs, e_lo, e_hi, valid, n_tiles, w):
    max_tiles = e_lo.shape[0]
    row = lambda k, lo, hi, valid, nt: (jnp.minimum(k, nt[0] - 1), 0)
    up = lambda which: pl.BlockSpec((1, 1, D_MODEL, D_FF_EXPERT),
                                    lambda k, lo, hi, valid, nt: (layer, (lo, hi)[which][k], 0, 0))
    down = lambda which: pl.BlockSpec((1, 1, D_FF_EXPERT, D_MODEL),
                                      lambda k, lo, hi, valid, nt: (layer, (lo, hi)[which][k], 0, 0))
    return pl.pallas_call(
        _expert_kernel,
        out_shape=jax.ShapeDtypeStruct((max_tiles * EXPERT_TILE, D_MODEL), f32),
        grid_spec=pltpu.PrefetchScalarGridSpec(
            num_scalar_prefetch=4, grid=(max_tiles,),
            in_specs=[pl.BlockSpec((EXPERT_TILE, HX_COLS), row), up(0), up(0), down(0), up(1), up(1), down(1)],
            out_specs=pl.BlockSpec((EXPERT_TILE, D_MODEL), row),
            scratch_shapes=[pltpu.VMEM((D_MODEL, D_FF_EXPERT), bf16), pltpu.VMEM((D_MODEL, D_FF_EXPERT), bf16),
                            pltpu.VMEM((D_FF_EXPERT, D_MODEL), bf16)] * 2),
        compiler_params=_params(("arbitrary",)),
        name="experts",
    )(e_lo, e_hi, valid, n_tiles, xs, w["w_gate"], w["w_up"], w["w_down"], w["w_gate"], w["w_up"], w["w_down"])


def _final_kernel(x1_ref, y_ref, mod_ref, fg_ref, o_ref):
    o_ref[...] = _rms(x1_ref[...] + mod_ref[0, 0, 5:6, :] * y_ref[...], D_MODEL) * fg_ref[...]


def _final(x1, y, n, row0, mod, mod_row, w, tile):
    off = row0 // tile
    src_row = lambda t: (off + t, 0)
    return pl.pallas_call(
        _final_kernel,
        out_shape=jax.ShapeDtypeStruct((n, D_MODEL), f32),
        grid=(n // tile,),
        in_specs=[pl.BlockSpec((tile, D_MODEL), src_row), pl.BlockSpec((tile, D_MODEL), src_row),
                  pl.BlockSpec((1, 1, N_MOD, D_MODEL), lambda t: (DEPTH - 1, mod_row(t * tile), 0, 0)),
                  pl.BlockSpec((1, D_MODEL), lambda t: (0, 0))],
        out_specs=pl.BlockSpec((tile, D_MODEL), lambda t: (t, 0)),
        compiler_params=_params(("arbitrary",)),
        name="final_norm",
    )(x1, y, mod, w["final_g"])


def _rope_tables(n_tokens):
    pos = np.arange(n_tokens)
    row = (pos // GRID_W).astype(np.float64)
    col = (pos % GRID_W).astype(np.float64)

    def cs(rot_dim):
        quarter = rot_dim // 4
        inv = ROPE_THETA ** (-np.arange(quarter, dtype=np.float64) / quarter)
        ang = np.concatenate([row[:, None] * inv, col[:, None] * inv], axis=-1)
        return np.cos(ang), np.sin(ang)

    c32, s32 = cs(MLA_ROPE)
    c64, s64 = cs(HEAD_DIM)
    ones = np.ones((n_tokens, MLA_NOPE))
    zeros = np.zeros((n_tokens, MLA_NOPE))

    def rep(parts):
        period = np.concatenate(parts, axis=-1)
        return jnp.asarray(np.tile(period, (1, LANES // period.shape[-1])), f32)

    return (rep([ones, c32, c32]), rep([zeros, -s32, s32]), rep([c32, c32]), rep([-s32, s32]),
            rep([c64, c64]), rep([-s64, s64]))


def _layout_weights(norm1_g, norm2_g, w_in, mla_kv_norm_g, mla_w_uk, mla_w_uv, gqa_q_norm_g, gqa_k_norm_g,
                    diff_lambda, diff_norm_g, w_out, moe_w_group, moe_b_group, moe_w_router, moe_b_router,
                    moe_w_gate, moe_w_up, moe_w_down, final_norm_g):
    eye = jnp.eye(MLA_ROPE, dtype=f32)
    top = jnp.concatenate([mla_w_uk, jnp.zeros((DEPTH, KV_RANK, MLA_HEADS, MLA_ROPE), f32)], axis=-1)
    mid = jnp.concatenate([jnp.zeros((MLA_ROPE, MLA_HEADS, MLA_NOPE), f32),
                           jnp.broadcast_to(eye[:, None, :], (MLA_ROPE, MLA_HEADS, MLA_ROPE))], axis=-1)
    w_ka = jnp.concatenate([top.reshape(DEPTH, KV_RANK, 384),
                            jnp.broadcast_to(mid.reshape(1, MLA_ROPE, 384), (DEPTH, MLA_ROPE, 384)),
                            jnp.zeros((DEPTH, 256 - KV_RANK - MLA_ROPE, 384), f32)], axis=1).astype(bf16)
    seg_id = np.arange(512) // HEAD_DIM
    seg = jnp.asarray(seg_id[:, None] == seg_id[None, :], bf16)
    qk_g = jnp.concatenate([jnp.tile(gqa_q_norm_g, (1, GQA_HEADS)), jnp.tile(gqa_k_norm_g, (1, GQA_KV_HEADS))], axis=-1)
    return dict(
        g1=norm1_g.reshape(DEPTH, 1, D_MODEL), g2=norm2_g.reshape(DEPTH, 1, D_MODEL), w_in=w_in,
        kv_g=mla_kv_norm_g.reshape(DEPTH, 1, KV_RANK), qk_g=qk_g.reshape(DEPTH, 1, 512), seg=seg, w_ka=w_ka,
        w_uv=mla_w_uv.reshape(DEPTH, KV_RANK, 384), lam=diff_lambda, diff_g=diff_norm_g.reshape(DEPTH, 1, DIFF_V),
        w_out=w_out, w_grp=moe_w_group, b_grp=moe_b_group.reshape(DEPTH, 1, N_GROUPS), w_rtr=moe_w_router,
        b_rtr=moe_b_router.reshape(DEPTH, 1, N_EXPERTS), w_gate=moe_w_gate, w_up=moe_w_up, w_down=moe_w_down,
        final_g=final_norm_g.reshape(1, D_MODEL))


PRE_TILE = 512
CTX_ATTN_TILE = 256
LAT_ATTN_TILE = 256
CTX_HEADS_PER_ROUND = 20
LAT_HEADS_PER_ROUND = 5
POST_TILE = 512
FINAL_TILE = 1024


def kernel(x_prompt, x_sample, c, cache_mla_ckv, cache_mla_krope, cache_gqa_k, cache_gqa_v, cache_diff_k, cache_diff_v, c_ctx, norm1_g, norm2_g, w_mod, b_mod, w_in, mla_kv_norm_g, mla_w_uk, mla_w_uv, gqa_q_norm_g, gqa_k_norm_g, diff_lambda, diff_norm_g, w_out, moe_w_group, moe_b_group, moe_w_router, moe_b_router, moe_w_gate, moe_w_up, moe_w_down, final_norm_g):
    B, S, _ = x_prompt.shape
    Bl, Sl, _ = x_sample.shape
    n_ctx, n_lat = B * S, Bl * Sl
    total = n_ctx + n_lat
    assert S == CTX_ATTN_TILE and Bl + 1 <= MOD_ROWS and DEPTH == 2 and total % SC_ROWS == 0
    slot_rows = -(-(total + N_CLASSES * EXPERT_TILE) // SC_ROWS) * SC_ROWS
    max_tiles = slot_rows // EXPERT_TILE
    w = _layout_weights(norm1_g, norm2_g, w_in, mla_kv_norm_g, mla_w_uk, mla_w_uv, gqa_q_norm_g, gqa_k_norm_g,
                        diff_lambda, diff_norm_g, w_out, moe_w_group, moe_b_group, moe_w_router, moe_b_router,
                        moe_w_gate, moe_w_up, moe_w_down, final_norm_g)
    cond = jnp.concatenate([c_ctx[None, :], c, jnp.zeros((MOD_ROWS - 1 - Bl, D_MODEL), f32)], axis=0)
    mod = _modulation(cond, w_mod, b_mod).reshape(DEPTH, MOD_ROWS, N_MOD, D_MODEL)
    ctx_row = lambda token: 0
    lat_row = lambda token: 1 + token // Sl
    tabs = _rope_tables(Sl)
    kv_past = _cache_rows((cache_mla_ckv, cache_mla_krope, cache_gqa_k, cache_gqa_v, cache_diff_k, cache_diff_v), w)
    per_b = Sl // LAT_ATTN_TILE

    x_ctx, x_lat = x_prompt.reshape(n_ctx, D_MODEL), x_sample.reshape(n_lat, D_MODEL)
    cache = ()
    x1 = y = None
    for i in range(DEPTH):
        if i == 0:
            q, kv, *cache = _pre(i, x_ctx, n_ctx, 0, S, mod, ctx_row, w, PRE_TILE)
        else:
            q, kv, x_ctx, *cache = _pre(i, x1, n_ctx, 0, S, mod, ctx_row, w, PRE_TILE, prev_cache=cache, resid=y)
        own = (kv.reshape(1, 1, n_ctx, KV_COLS), S, lambda t: (0, 0, t, 0))
        o = _attention(i, q, [own], w, CTX_ATTN_TILE, CTX_HEADS_PER_ROUND)
        if i == 0:
            q_l, kv_l = _pre(i, x_lat, n_lat, 0, Sl, mod, lat_row, w, PRE_TILE, rope_tabs=tabs)
        else:
            q_l, kv_l, x_lat = _pre(i, x1, n_lat, n_ctx, Sl, mod, lat_row, w, PRE_TILE, rope_tabs=tabs, resid=y)
        past = (kv_past, PAST_LEN, lambda t, i=i: (i, t // per_b, 0, 0))
        own = (kv_l.reshape(1, Bl, Sl, KV_COLS), Sl, lambda t: (0, t // per_b, 0, 0))
        o_l = _attention(i, q_l, [past, own], w, LAT_ATTN_TILE, LAT_HEADS_PER_ROUND)
        merged = _post(i, o, x_ctx, 0, total, mod, ctx_row, w, POST_TILE)
        x1, hx, cls = _post(i, o_l, x_lat, n_ctx, total, mod, lat_row, w, POST_TILE, merged=merged)
        slot, e_lo, e_hi, valid, n_tiles = _plan(cls, max_tiles)
        xs = _move_rows(hx, slot, slot_rows, scatter=True)
        ys = _experts(i, xs, e_lo, e_hi, valid, n_tiles, w)
        y = _move_rows(ys, slot, total, scatter=False)

    y_prompt = _final(x1, y, n_ctx, 0, mod, ctx_row, w, FINAL_TILE).reshape(B, S, D_MODEL)
    y_sample = _final(x1, y, n_lat, n_ctx, mod, lat_row, w, FINAL_TILE).reshape(Bl, Sl, D_MODEL)
    new_mla_ckv, new_mla_krope = cache[0], cache[1]
    new_gqa_k = cache[2].reshape(B, DEPTH, S, GQA_KV_HEADS, HEAD_DIM)
    new_gqa_v = cache[3].reshape(B, DEPTH, S, GQA_KV_HEADS, HEAD_DIM)
    new_diff_k = cache[4].reshape(B, DEPTH, S, DIFF_HEADS, 2, DIFF_QK)
    new_diff_v = cache[5].reshape(B, DEPTH, S, DIFF_HEADS, DIFF_V)
    return (y_prompt, y_sample, new_mla_ckv, new_mla_krope, new_gqa_k, new_gqa_v, new_diff_k, new_diff_v)
```

```python
import functools
import math

import jax
import jax.numpy as jnp
import numpy as np
from jax.experimental import pallas as pl
from jax.experimental.pallas import tpu as pltpu
from jax.experimental.pallas import tpu_sc as plsc

D_MODEL = 1024
DEPTH = 2
PAST_LEN = 512
GRID_W = 64
ROPE_THETA = 10000.0
EPS = 1e-6
LOG2E = 1.4426950408889634
N_MOD = 6
HEAD_DIM = 64
MLA_HEADS = 6
MLA_NOPE = 32
MLA_ROPE = 32
MLA_V = 64
KV_RANK = 128
GQA_HEADS = 6
GQA_KV_HEADS = 2
GQA_GROUP = GQA_HEADS // GQA_KV_HEADS
DIFF_HEADS = 4
DIFF_QK = 32
DIFF_V = 64
N_GROUPS = 4
EXPERTS_PER_GROUP = 4
N_EXPERTS = N_GROUPS * EXPERTS_PER_GROUP
D_FF_EXPERT = 256

LANES = 128
MOD_ROWS = 8

IN_COLS = 1952
IN_KR = 512
Z_QA, Z_CKV, Z_QG, Z_KG, Z_VG, Z_QD, Z_KD, Z_VD, Z_KR = 0, 384, 512, 896, 1024, 1152, 1408, 1664, 1920
Z_COLS = 2048
Q_A, Q_G, Q_D, Q_COLS = 0, 384, 768, 1024
KV_KA, KV_VA, KV_KG, KV_VG, KV_KD, KV_VD, KV_COLS = 0, 384, 768, 896, 1024, 1280, 1536
CACHE_WIDTHS = (128, 32, 128, 128, 256, 256)

PAIR_LO = (0, 0, 0, 1, 1, 2)
PAIR_HI = (1, 2, 3, 3, 2, 3)
N_PAIRS = len(PAIR_LO)
N_CLASSES = N_GROUPS * N_PAIRS
HX_COLS = D_MODEL + LANES
EXPERT_TILE = 256

SC_CORES, SC_SUBCORES = 2, 16
SC_CHUNK = 64
SC_ROWS = SC_CORES * SC_SUBCORES * SC_CHUNK

VMEM_LIMIT = 56 * 1024 * 1024

bf16 = jnp.bfloat16
f32 = jnp.float32
i32 = jnp.int32


def _dot(a, b):
    return jnp.dot(a, b, preferred_element_type=f32)


def _dot_nt(a, b):
    return jax.lax.dot_general(a, b, (((1,), (1,)), ((), ())), preferred_element_type=f32)


def _rms(x, width):
    return x * jax.lax.rsqrt(jnp.sum(x * x, axis=-1, keepdims=True) * (1.0 / width) + EPS)


def _silu(x):
    return x * (1.0 / (1.0 + jnp.exp(-x)))


def _params(sem):
    return pltpu.CompilerParams(dimension_semantics=sem, vmem_limit_bytes=VMEM_LIMIT)


def _mod_kernel(cond_ref, w_ref, b_ref, o_ref):
    o_ref[0] = _dot(_silu(cond_ref[...]).astype(bf16), w_ref[0].astype(bf16)) + b_ref[0]


def _modulation(cond, w_mod, b_mod):
    return pl.pallas_call(
        _mod_kernel,
        out_shape=jax.ShapeDtypeStruct((DEPTH, MOD_ROWS, N_MOD * D_MODEL), f32),
        grid=(DEPTH, N_MOD),
        in_specs=[
            pl.BlockSpec((MOD_ROWS, D_MODEL), lambda i, j: (0, 0)),
            pl.BlockSpec((1, D_MODEL, D_MODEL), lambda i, j: (i, 0, j)),
            pl.BlockSpec((1, 1, D_MODEL), lambda i, j: (i, 0, j)),
        ],
        out_specs=pl.BlockSpec((1, MOD_ROWS, D_MODEL), lambda i, j: (i, 0, j)),
        compiler_params=_params(("arbitrary", "arbitrary")),
        name="modulation",
    )(cond, w_mod, b_mod.reshape(DEPTH, 1, N_MOD * D_MODEL))


def _swap_halves(x, half):
    lane = jax.lax.broadcasted_iota(i32, x.shape, 1)
    fwd = pltpu.roll(x, LANES - half, 1)
    bwd = pltpu.roll(x, half, 1)
    return jnp.where((lane & (2 * half - 1)) < half, fwd, bwd)


def _rope_block(x, cos, sin, half):
    return x * cos + _swap_halves(x, half) * sin


def _pre_kernel(rope, n_prev, resid, *refs):
    it = iter(refs)
    x_ref, mod_ref, g1_ref, w_in_ref, kvg_ref, qkg_ref, seg_ref, wka_ref, wuv_ref = (next(it) for _ in range(9))
    if resid:
        y_ref, pmod_ref = next(it), next(it)
    if rope:
        ca_ref, sa_ref, c32_ref, s32_ref, c64_ref, s64_ref = (next(it) for _ in range(6))
    prev_refs = [next(it) for _ in range(n_prev)]
    q_ref, kv_ref = next(it), next(it)
    if resid:
        x2_ref = next(it)
    cache_refs = [] if rope else [next(it) for _ in range(len(CACHE_WIDTHS))]
    w_scr = next(it)

    @pl.when(pl.program_id(0) == 0)
    def _():
        rows = 256
        for r in range(0, D_MODEL, rows):
            wf = w_in_ref[0, r:r + rows, :]
            w_scr[r:r + rows, 0:IN_KR] = wf[:, 0:IN_KR].astype(bf16)
            w_scr[r:r + rows, IN_KR:Z_KR] = wf[:, IN_KR + MLA_ROPE:IN_COLS].astype(bf16)
            w_scr[r:r + rows, Z_KR:Z_KR + MLA_ROPE] = wf[:, IN_KR:IN_KR + MLA_ROPE].astype(bf16)
            w_scr[r:r + rows, Z_KR + MLA_ROPE:Z_COLS] = jnp.zeros((rows, Z_COLS - Z_KR - MLA_ROPE), bf16)

    x = x_ref[...]
    if resid:
        x = x + pmod_ref[0, 0, 5:6, :] * y_ref[...]
        x2_ref[...] = x
    shift1 = mod_ref[0, 0, 0:1, :]
    scale1 = mod_ref[0, 0, 1:2, :]
    h = (_rms(x, D_MODEL) * g1_ref[0]) * (1.0 + scale1) + shift1
    z = _dot(h.astype(bf16), w_scr[...])

    ckv = _rms(z[:, Z_CKV:Z_CKV + KV_RANK], KV_RANK) * kvg_ref[0]

    qk = z[:, Z_QG:Z_VG]
    sq = qk * qk
    sq_hi = sq.astype(bf16)
    sq_lo = (sq - sq_hi.astype(f32)).astype(bf16)
    seg = seg_ref[...]
    ms = (_dot(sq_hi, seg) + _dot(sq_lo, seg)) * (1.0 / HEAD_DIM)
    qk = qk * jax.lax.rsqrt(ms + EPS) * qkg_ref[0]

    def blocks(arr, n):
        return [arr[:, LANES * j:LANES * (j + 1)] for j in range(n)]

    qa = blocks(z[:, Z_QA:Z_QA + 384], 3)
    qkb = blocks(qk, 4)
    qd = blocks(z[:, Z_QD:Z_QD + 256], 2)
    kd = blocks(z[:, Z_KD:Z_KD + 256], 2)
    kr = z[:, Z_KR:Z_KR + LANES]
    if rope:
        ca, sa, c32, s32, c64, s64 = (r[...] for r in (ca_ref, sa_ref, c32_ref, s32_ref, c64_ref, s64_ref))
        qa = [_rope_block(b, ca, sa, MLA_ROPE // 2) for b in qa]
        qkb = [_rope_block(b, c64, s64, HEAD_DIM // 2) for b in qkb]
        qd = [_rope_block(b, c32, s32, DIFF_QK // 2) for b in qd]
        kd = [_rope_block(b, c32, s32, DIFF_QK // 2) for b in kd]
        kr = _rope_block(kr, c32, s32, MLA_ROPE // 2)

    vg = z[:, Z_VG:Z_VG + 128]
    vd = z[:, Z_VD:Z_VD + 256]
    ckv_b = ckv.astype(bf16)
    k_a = _dot(jnp.concatenate([ckv_b, kr.astype(bf16)], axis=1), wka_ref[0])
    v_a = _dot(ckv_b, wuv_ref[0].astype(bf16))

    for j in range(3):
        q_ref[:, Q_A + LANES * j:Q_A + LANES * (j + 1)] = (qa[j] * (HEAD_DIM ** -0.5 * LOG2E)).astype(bf16)
        q_ref[:, Q_G + LANES * j:Q_G + LANES * (j + 1)] = (qkb[j] * (HEAD_DIM ** -0.5 * LOG2E)).astype(bf16)
    for j in range(2):
        q_ref[:, Q_D + LANES * j:Q_D + LANES * (j + 1)] = (qd[j] * (DIFF_QK ** -0.5 * LOG2E)).astype(bf16)
        kv_ref[:, KV_KD + LANES * j:KV_KD + LANES * (j + 1)] = kd[j].astype(bf16)
    kv_ref[:, KV_KA:KV_KA + 384] = k_a.astype(bf16)
    kv_ref[:, KV_VA:KV_VA + 384] = v_a.astype(bf16)
    kv_ref[:, KV_KG:KV_KG + 128] = qkb[3].astype(bf16)
    kv_ref[:, KV_VG:KV_VG + 128] = vg.astype(bf16)
    kv_ref[:, KV_VD:KV_VD + 256] = vd.astype(bf16)
    if not rope:
        rows = [ckv, kr[:, :MLA_ROPE], qkb[3], vg, jnp.concatenate(kd, axis=1), vd]
        for out, new in zip(cache_refs, rows):
            reqs, _, seq, width = out.shape
            out[:, 0] = new.reshape(reqs, seq, width)


def _pre(layer, x, n, row0, seq, mod, mod_row, w, tile, rope_tabs=None, prev_cache=(), resid=None):
    rope = rope_tabs is not None
    lay = lambda t: (layer, 0, 0)
    row = lambda t: (t, 0)
    off = row0 // tile
    src_row = lambda t: (off + t, 0)
    in_specs = [
        pl.BlockSpec((tile, D_MODEL), src_row),
        pl.BlockSpec((1, 1, N_MOD, D_MODEL), lambda t: (layer, mod_row(t * tile), 0, 0)),
        pl.BlockSpec((1, 1, D_MODEL), lay),
        pl.BlockSpec((1, D_MODEL, IN_COLS), lay),
        pl.BlockSpec((1, 1, KV_RANK), lay),
        pl.BlockSpec((1, 1, 512), lay),
        pl.BlockSpec((512, 512), lambda t: (0, 0)),
        pl.BlockSpec((1, 256, 384), lay),
        pl.BlockSpec((1, KV_RANK, 384), lay),
    ]
    args = [x, mod, w["g1"], w["w_in"], w["kv_g"], w["qk_g"], w["seg"], w["w_ka"], w["w_uv"]]
    if resid is not None:
        in_specs += [pl.BlockSpec((tile, D_MODEL), src_row),
                     pl.BlockSpec((1, 1, N_MOD, D_MODEL), lambda t: (layer - 1, mod_row(t * tile), 0, 0))]
        args += [resid, mod]
    if rope:
        per_b = seq // tile
        in_specs += [pl.BlockSpec((tile, LANES), lambda t: (t % per_b, 0))] * 6
        args += list(rope_tabs)
    out_shape = [jax.ShapeDtypeStruct((n, Q_COLS), bf16), jax.ShapeDtypeStruct((n, KV_COLS), bf16)]
    out_specs = [pl.BlockSpec((tile, Q_COLS), row), pl.BlockSpec((tile, KV_COLS), row)]
    if resid is not None:
        out_shape.append(jax.ShapeDtypeStruct((n, D_MODEL), f32))
        out_specs.append(pl.BlockSpec((tile, D_MODEL), row))
    aliases = {}
    if not rope:
        reqs = tile // seq
        for j, width in enumerate(CACHE_WIDTHS):
            if prev_cache:
                aliases[len(args)] = len(out_shape)
                in_specs.append(pl.BlockSpec(memory_space=pl.ANY))
                args.append(prev_cache[j])
            out_shape.append(jax.ShapeDtypeStruct((n // seq, DEPTH, seq, width), f32))
            out_specs.append(pl.BlockSpec((reqs, 1, seq, width), lambda t: (t, layer, 0, 0)))
    return pl.pallas_call(
        functools.partial(_pre_kernel, rope, len(prev_cache), resid is not None),
        out_shape=out_shape,
        grid=(n // tile,),
        in_specs=in_specs,
        out_specs=out_specs,
        input_output_aliases=aliases,
        scratch_shapes=[pltpu.VMEM((D_MODEL, Z_COLS), bf16)],
        compiler_params=_params(("arbitrary",)),
        name="pre_latent" if rope else "pre_context",
    )(*args)


def _cache_kernel(ckv_ref, kr_ref, kg_ref, vg_ref, kd_ref, vd_ref, wka_ref, wuv_ref, kv_ref):
    ckv_b = ckv_ref[0, 0].astype(bf16)
    wka = wka_ref[0]
    k_a = _dot(ckv_b, wka[:KV_RANK]) + _dot(kr_ref[0, 0].astype(bf16), wka[KV_RANK:KV_RANK + MLA_ROPE])
    kv_ref[0, 0, :, KV_KA:KV_KA + 384] = k_a.astype(bf16)
    kv_ref[0, 0, :, KV_VA:KV_VA + 384] = _dot(ckv_b, wuv_ref[0].astype(bf16)).astype(bf16)
    kv_ref[0, 0, :, KV_KG:KV_KG + 128] = kg_ref[0, 0].astype(bf16)
    kv_ref[0, 0, :, KV_VG:KV_VG + 128] = vg_ref[0, 0].astype(bf16)
    kv_ref[0, 0, :, KV_KD:KV_KD + 256] = kd_ref[0, 0].astype(bf16)
    kv_ref[0, 0, :, KV_VD:KV_VD + 256] = vd_ref[0, 0].astype(bf16)


def _cache_rows(caches, w):
    B = caches[0].shape[0]
    spec = lambda width: pl.BlockSpec((1, 1, PAST_LEN, width), lambda i, b: (b, i, 0, 0))
    return pl.pallas_call(
        _cache_kernel,
        out_shape=jax.ShapeDtypeStruct((DEPTH, B, PAST_LEN, KV_COLS), bf16),
        grid=(DEPTH, B),
        in_specs=[spec(width) for width in CACHE_WIDTHS]
        + [pl.BlockSpec((1, 256, 384), lambda i, b: (i, 0, 0)), pl.BlockSpec((1, KV_RANK, 384), lambda i, b: (i, 0, 0))],
        out_specs=pl.BlockSpec((1, 1, PAST_LEN, KV_COLS), lambda i, b: (i, b, 0, 0)),
        compiler_params=_params(("arbitrary", "arbitrary")),
        name="cache_rows",
    )(*[c.reshape(B, DEPTH, PAST_LEN, width) for c, width in zip(caches, CACHE_WIDTHS)], w["w_ka"], w["w_uv"])


_SCORE_HEADS = (
    [(Q_A + 64 * h, KV_KA + 64 * h, 64, KV_VA + MLA_V * h) for h in range(MLA_HEADS)]
    + [(Q_G + 64 * h, KV_KG + 64 * (h // GQA_GROUP), 64, KV_VG + 64 * (h // GQA_GROUP)) for h in range(GQA_HEADS)]
    + [(Q_D + 64 * h + DIFF_QK * c, KV_KD + 64 * h + DIFF_QK * c, DIFF_QK, KV_VD + DIFF_V * h)
       for h in range(DIFF_HEADS) for c in range(2)])


def _attn_kernel(lam_init, per_round, n_src, q_ref, *refs):
    kv_refs = refs[:n_src]
    lam_ref, dg_ref, o_ref, s_ref, p_ref = refs[n_src:]
    spans, start = [], 0
    for r in kv_refs:
        spans.append((r, start, r.shape[2]))
        start += r.shape[2]

    outs = []
    for first in range(0, len(_SCORE_HEADS), per_round):
        chunk = _SCORE_HEADS[first:first + per_round]
        for j, (q_off, k_off, width, _) in enumerate(chunk):
            for r, lo, size in spans:
                s_ref[j, :, lo:lo + size] = _dot_nt(q_ref[:, q_off:q_off + width], r[0, 0, :, k_off:k_off + width])
        s = s_ref[...]
        p = jnp.exp2(s - jnp.max(s, axis=-1, keepdims=True))
        inv = 1.0 / jnp.sum(p, axis=-1, keepdims=True)
        p_ref[...] = p.astype(bf16)
        for j, (_, _, _, v_off) in enumerate(chunk):
            o = sum(_dot(p_ref[j, :, lo:lo + size], r[0, 0, :, v_off:v_off + DIFF_V]) for r, lo, size in spans)
            outs.append(o * inv[j])

    lp = lam_ref[0]
    e1 = jnp.exp(jnp.sum(lp[0:1] * lp[1:2], axis=-1, keepdims=True))
    e2 = jnp.exp(jnp.sum(lp[2:3] * lp[3:4], axis=-1, keepdims=True))
    lam = e1 - e2 + lam_init
    heads = outs[:MLA_HEADS + GQA_HEADS]
    for h in range(DIFF_HEADS):
        o1, o2 = outs[MLA_HEADS + GQA_HEADS + 2 * h:MLA_HEADS + GQA_HEADS + 2 * h + 2]
        heads.append(_rms(o1 - lam * o2, DIFF_V) * dg_ref[0] * (1.0 - lam_init))
    for j in range(len(heads) // 2):
        o_ref[:, LANES * j:LANES * (j + 1)] = jnp.concatenate(heads[2 * j:2 * j + 2], axis=1).astype(bf16)


def _attention(layer, q, sources, w, tile, per_round):
    n = q.shape[0]
    lam_init = 0.8 - 0.6 * math.exp(-0.3 * layer)
    s_kv = sum(rows for _, rows, _ in sources)
    assert len(_SCORE_HEADS) % per_round == 0
    return pl.pallas_call(
        functools.partial(_attn_kernel, lam_init, per_round, len(sources)),
        out_shape=jax.ShapeDtypeStruct((n, D_MODEL), bf16),
        scratch_shapes=[pltpu.VMEM((per_round, tile, s_kv), f32), pltpu.VMEM((per_round, tile, s_kv), bf16)],
        grid=(n // tile,),
        in_specs=[pl.BlockSpec((tile, Q_COLS), lambda t: (t, 0))]
        + [pl.BlockSpec((1, 1, rows, KV_COLS), index) for _, rows, index in sources]
        + [pl.BlockSpec((1, 4, DIFF_QK), lambda t: (layer, 0, 0)), pl.BlockSpec((1, 1, DIFF_V), lambda t: (layer, 0, 0))],
        out_specs=pl.BlockSpec((tile, D_MODEL), lambda t: (t, 0)),
        compiler_params=_params(("arbitrary",)),
        name="attention",
    )(q, *[arr for arr, _, _ in sources], w["lam"], w["diff_g"])


def _post_kernel(merge, *refs):
    it = iter(refs)
    o_ref, x_ref, mod_ref, w_out_ref, g2_ref, wg_ref, bg_ref, we_ref, be_ref = (next(it) for _ in range(9))
    if merge:
        next(it), next(it), next(it)
    x1_ref, hx_ref, cls_ref, w_scr = (next(it) for _ in range(4))

    @pl.when(pl.program_id(0) == 0)
    def _():
        w_scr[...] = w_out_ref[0].astype(bf16)

    gate1 = mod_ref[0, 0, 2:3, :]
    shift2 = mod_ref[0, 0, 3:4, :]
    scale2 = mod_ref[0, 0, 4:5, :]
    x1 = x_ref[...] + gate1 * _dot(o_ref[...], w_scr[...])
    x1_ref[...] = x1
    h2 = ((_rms(x1, D_MODEL) * g2_ref[0]) * (1.0 + scale2) + shift2).astype(bf16)
    hx_ref[:, 0:D_MODEL] = h2.astype(f32)

    def first_lane(mask, lane_f):
        return jnp.min(jnp.where(mask, lane_f, float(LANES)), axis=-1, keepdims=True)

    gl = _dot(h2, wg_ref[0].astype(bf16)) + bg_ref[0]
    glane = jax.lax.broadcasted_iota(i32, gl.shape, 1).astype(f32)
    ge = jnp.exp(gl - jnp.max(gl, axis=-1, keepdims=True))
    gprob = ge / jnp.sum(ge, axis=-1, keepdims=True)
    g_top = jnp.max(gprob, axis=-1, keepdims=True)
    g_idx = first_lane(gprob == g_top, glane)

    el = _dot(h2, we_ref[0].astype(bf16)) + be_ref[0]
    lane = jax.lax.broadcasted_iota(i32, el.shape, 1)
    lane_f = lane.astype(f32)
    emask = (lane >> 2).astype(f32) == g_idx
    em = jnp.where(emask, el, -jnp.inf)
    ee = jnp.where(emask, jnp.exp(em - jnp.max(em, axis=-1, keepdims=True)), 0.0)
    ep = ee / jnp.sum(ee, axis=-1, keepdims=True)
    p1 = jnp.max(jnp.where(emask, ep, -1.0), axis=-1, keepdims=True)
    i1 = first_lane(emask & (ep == p1), lane_f)
    rest = emask & (lane_f != i1)
    p2 = jnp.max(jnp.where(rest, ep, -1.0), axis=-1, keepdims=True)
    i2 = first_lane(rest & (ep == p2), lane_f)
    tot = p1 + p2
    w1 = g_top * (p1 / tot)
    w2 = g_top * (p2 / tot)

    lo = jnp.minimum(i1, i2) - EXPERTS_PER_GROUP * g_idx
    hi = jnp.maximum(i1, i2) - EXPERTS_PER_GROUP * g_idx
    pair = jnp.where(lo == 0.0, hi - 1.0, jnp.where(lo == 1.0, jnp.where(hi == 3.0, 3.0, 4.0), 5.0))
    cls_ref[...] = (N_PAIRS * g_idx + pair).astype(i32)
    g_lo = jnp.where(i1 < i2, w1, w2)
    g_hi = jnp.where(i1 < i2, w2, w1)
    tail_lane = jax.lax.broadcasted_iota(i32, (h2.shape[0], LANES), 1)
    hx_ref[:, D_MODEL:HX_COLS] = jnp.where(tail_lane == 0, g_lo, jnp.where(tail_lane == 1, g_hi, 0.0))


def _post(layer, o, x, row0, total, mod, mod_row, w, tile, merged=None):
    n = o.shape[0]
    lay = lambda t: (layer, 0, 0)
    row = lambda t: (t, 0)
    off = row0 // tile
    out_row = lambda t: (off + t, 0)
    in_specs = [
        pl.BlockSpec((tile, D_MODEL), row),
        pl.BlockSpec((tile, D_MODEL), row),
        pl.BlockSpec((1, 1, N_MOD, D_MODEL), lambda t: (layer, mod_row(t * tile), 0, 0)),
        pl.BlockSpec((1, D_MODEL, D_MODEL), lay),
        pl.BlockSpec((1, 1, D_MODEL), lay),
        pl.BlockSpec((1, D_MODEL, N_GROUPS), lay),
        pl.BlockSpec((1, 1, N_GROUPS), lay),
        pl.BlockSpec((1, D_MODEL, N_EXPERTS), lay),
        pl.BlockSpec((1, 1, N_EXPERTS), lay),
    ]
    args = [o, x, mod, w["w_out"], w["g2"], w["w_grp"], w["b_grp"], w["w_rtr"], w["b_rtr"]]
    aliases = {}
    if merged is not None:
        aliases = {len(args) + j: j for j in range(3)}
        in_specs += [pl.BlockSpec(memory_space=pl.ANY)] * 3
        args += list(merged)
    return pl.pallas_call(
        functools.partial(_post_kernel, merged is not None),
        out_shape=[jax.ShapeDtypeStruct((total, D_MODEL), f32), jax.ShapeDtypeStruct((total, HX_COLS), f32),
                   jax.ShapeDtypeStruct((total, 1), i32)],
        grid=(n // tile,),
        in_specs=in_specs,
        out_specs=[pl.BlockSpec((tile, D_MODEL), out_row), pl.BlockSpec((tile, HX_COLS), out_row),
                   pl.BlockSpec((tile, 1), out_row)],
        scratch_shapes=[pltpu.VMEM((D_MODEL, D_MODEL), bf16)],
        input_output_aliases=aliases,
        compiler_params=_params(("arbitrary",)),
        name="post_attention",
    )(*args)


PLAN_CHUNK = 1024
TAB_ROWS = LANES


def _plan_kernel(cls_ref, slot_ref, tab_ref, rank_scr):
    n = cls_ref.shape[0]
    lane = jax.lax.broadcasted_iota(i32, (PLAN_CHUNK, LANES), 1)
    r = jax.lax.broadcasted_iota(i32, (PLAN_CHUNK, PLAN_CHUNK), 0)
    c = jax.lax.broadcasted_iota(i32, (PLAN_CHUNK, PLAN_CHUNK), 1)
    before = (c < r).astype(bf16)

    def count(b, seen):
        rows = pl.ds(pl.multiple_of(b * PLAN_CHUNK, PLAN_CHUNK), PLAN_CHUNK)
        onehot = (cls_ref[rows, :] == lane).astype(f32)
        ahead = _dot(before, onehot.astype(bf16)) + seen
        rank_scr[rows, :] = jnp.sum(onehot * ahead, axis=-1, keepdims=True)
        return seen + jnp.sum(onehot, axis=0, keepdims=True)

    counts = jax.lax.fori_loop(0, n // PLAN_CHUNK, count, jnp.zeros((1, LANES), f32))
    tiles = jnp.floor((counts + (EXPERT_TILE - 1)) * (1.0 / EXPERT_TILE))
    rr = jax.lax.broadcasted_iota(i32, (LANES, LANES), 0)
    cc = jax.lax.broadcasted_iota(i32, (LANES, LANES), 1)
    ends = _dot(jnp.broadcast_to(tiles, (8, LANES)).astype(bf16), (rr <= cc).astype(bf16))[0:1]
    starts = ends - tiles

    def place(b, carry):
        rows = pl.ds(pl.multiple_of(b * PLAN_CHUNK, PLAN_CHUNK), PLAN_CHUNK)
        first = jnp.sum(jnp.where(cls_ref[rows, :] == lane, starts, 0.0), axis=-1, keepdims=True)
        slot_ref[rows, :] = (first * EXPERT_TILE + rank_scr[rows, :]).astype(i32)
        return carry

    jax.lax.fori_loop(0, n // PLAN_CHUNK, place, 0)

    tl = jax.lax.broadcasted_iota(i32, (TAB_ROWS, LANES), 1)
    n_tiles = jnp.sum(jnp.where(tl[0:1] == N_CLASSES - 1, ends, 0.0), axis=-1, keepdims=True)
    k = jnp.minimum(jax.lax.broadcasted_iota(i32, (TAB_ROWS, 1), 0).astype(f32), n_tiles - 1.0)
    cls_k = jnp.sum(jnp.where((tl < N_CLASSES) & (ends <= k), 1.0, 0.0), axis=-1, keepdims=True)
    cls_k = jnp.minimum(cls_k, N_CLASSES - 1.0)
    mine = tl.astype(f32) == cls_k
    used = jnp.sum(jnp.where(mine, counts, 0.0), axis=-1, keepdims=True)
    first = jnp.sum(jnp.where(mine, starts, 0.0), axis=-1, keepdims=True)
    valid = jnp.clip(used - (k - first) * EXPERT_TILE, 0.0, float(EXPERT_TILE))
    group = jnp.floor((cls_k + 0.5) * (1.0 / N_PAIRS))
    pair = cls_k - N_PAIRS * group
    lo = hi = jnp.zeros_like(pair)
    for p in range(N_PAIRS):
        lo = jnp.where(pair == p, float(PAIR_LO[p]), lo)
        hi = jnp.where(pair == p, float(PAIR_HI[p]), hi)
    e_lo = EXPERTS_PER_GROUP * group + lo
    e_hi = EXPERTS_PER_GROUP * group + hi
    tab = jnp.where(tl == 0, e_lo, jnp.where(tl == 1, e_hi, jnp.where(tl == 2, valid, jnp.where(tl == 3, n_tiles, 0.0))))
    tab_ref[...] = tab.astype(i32)


def _plan(cls, max_tiles):
    n = cls.shape[0]
    assert n % PLAN_CHUNK == 0 and max_tiles <= TAB_ROWS
    slot, tab = pl.pallas_call(
        _plan_kernel,
        out_shape=[jax.ShapeDtypeStruct((n, 1), i32), jax.ShapeDtypeStruct((TAB_ROWS, LANES), i32)],
        scratch_shapes=[pltpu.VMEM((n, 1), f32)],
        compiler_params=_params(None),
        name="dispatch_plan",
    )(cls)
    return slot.reshape(n), tab[:max_tiles, 0], tab[:max_tiles, 1], tab[:max_tiles, 2], tab[0, 3:4]


def _move_rows(src, idx, n_out, scatter):
    n = idx.shape[0]
    width = src.shape[1]
    per_worker = n // (SC_CORES * SC_SUBCORES)
    assert n % SC_ROWS == 0
    mesh = plsc.VectorSubcoreMesh(core_axis_name="c", subcore_axis_name="s")

    @functools.partial(
        pl.kernel, mesh=mesh, out_type=jax.ShapeDtypeStruct((n_out, width), src.dtype),
        scratch_types=[pltpu.VMEM((SC_CHUNK,), i32), pltpu.VMEM((SC_CHUNK, width), src.dtype),
                       pltpu.SemaphoreType.DMA])
    def move(src_hbm, idx_hbm, out_hbm, idx_v, rows_v, sem):
        wid = jax.lax.axis_index("s") * SC_CORES + jax.lax.axis_index("c")
        base = wid * per_worker

        @pl.loop(0, per_worker // SC_CHUNK)
        def _(j):
            off = base + j * SC_CHUNK
            pltpu.sync_copy(idx_hbm.at[pl.ds(off, SC_CHUNK)], idx_v)
            if scatter:
                pltpu.sync_copy(src_hbm.at[pl.ds(off, SC_CHUNK)], rows_v)
                pltpu.async_copy(rows_v, out_hbm.at[idx_v], sem).wait()
            else:
                pltpu.async_copy(src_hbm.at[idx_v], rows_v, sem).wait()
                pltpu.sync_copy(rows_v, out_hbm.at[pl.ds(off, SC_CHUNK)])

    return move(src, idx)


def _expert_kernel(lo_ref, hi_ref, valid_ref, nt_ref, xs_ref, wg_ref, wu_ref, wd_ref, ys_ref, wg_scr, wu_scr, wd_scr):
    k = pl.program_id(0)
    prev = jnp.maximum(k - 1, 0)

    @pl.when((k == 0) | (lo_ref[k] // EXPERTS_PER_GROUP != lo_ref[prev] // EXPERTS_PER_GROUP))
    def _():
        for j in range(EXPERTS_PER_GROUP):
            wg_scr[j] = wg_ref[0, 0, j].astype(bf16)
            wu_scr[j] = wu_ref[0, 0, j].astype(bf16)
            wd_scr[j] = wd_ref[0, 0, j].astype(bf16)

    @pl.when(k < nt_ref[0])
    def _():
        live = jax.lax.broadcasted_iota(i32, (EXPERT_TILE, 1), 0) < valid_ref[k]
        x = jnp.where(live, xs_ref[:, 0:D_MODEL], 0.0).astype(bf16)
        gates = jnp.where(live, xs_ref[:, D_MODEL:HX_COLS], 0.0)
        y = None
        for lane, e_ref in enumerate((lo_ref, hi_ref)):
            j = e_ref[k] % EXPERTS_PER_GROUP
            hid = _silu(_dot(x, wg_scr[j])) * _dot(x, wu_scr[j]) * gates[:, lane:lane + 1]
            part = _dot(hid.astype(bf16), wd_scr[j])
            y = part if y is None else y + part
        ys_ref[...] = y


def _experts(layer, xs, e_lo, e_hi, valid, n_tiles, w):
    max_tiles = e_lo.shape[0]
    row = lambda k, lo, hi, valid, nt: (jnp.minimum(k, nt[0] - 1), 0)
    group = lambda k, lo, hi, valid, nt: (layer, lo[k] // EXPERTS_PER_GROUP, 0, 0, 0)
    by_group = lambda a: a.reshape(DEPTH, N_GROUPS, EXPERTS_PER_GROUP, *a.shape[2:])
    return pl.pallas_call(
        _expert_kernel,
        out_shape=jax.ShapeDtypeStruct((max_tiles * EXPERT_TILE, D_MODEL), f32),
        grid_spec=pltpu.PrefetchScalarGridSpec(
            num_scalar_prefetch=4, grid=(max_tiles,),
            in_specs=[pl.BlockSpec((EXPERT_TILE, HX_COLS), row),
                      pl.BlockSpec((1, 1, EXPERTS_PER_GROUP, D_MODEL, D_FF_EXPERT), group),
                      pl.BlockSpec((1, 1, EXPERTS_PER_GROUP, D_MODEL, D_FF_EXPERT), group),
                      pl.BlockSpec((1, 1, EXPERTS_PER_GROUP, D_FF_EXPERT, D_MODEL), group)],
            out_specs=pl.BlockSpec((EXPERT_TILE, D_MODEL), row),
            scratch_shapes=[pltpu.VMEM((EXPERTS_PER_GROUP, D_MODEL, D_FF_EXPERT), bf16),
                            pltpu.VMEM((EXPERTS_PER_GROUP, D_MODEL, D_FF_EXPERT), bf16),
                            pltpu.VMEM((EXPERTS_PER_GROUP, D_FF_EXPERT, D_MODEL), bf16)]),
        compiler_params=_params(("arbitrary",)),
        name="experts",
    )(e_lo, e_hi, valid, n_tiles, xs, by_group(w["w_gate"]), by_group(w["w_up"]), by_group(w["w_down"]))


def _final_kernel(x1_ref, y_ref, mod_ref, fg_ref, o_ref):
    o_ref[...] = _rms(x1_ref[...] + mod_ref[0, 0, 5:6, :] * y_ref[...], D_MODEL) * fg_ref[...]


def _final(x1, y, n, row0, mod, mod_row, w, tile):
    off = row0 // tile
    src_row = lambda t: (off + t, 0)
    return pl.pallas_call(
        _final_kernel,
        out_shape=jax.ShapeDtypeStruct((n, D_MODEL), f32),
        grid=(n // tile,),
        in_specs=[pl.BlockSpec((tile, D_MODEL), src_row), pl.BlockSpec((tile, D_MODEL), src_row),
                  pl.BlockSpec((1, 1, N_MOD, D_MODEL), lambda t: (DEPTH - 1, mod_row(t * tile), 0, 0)),
                  pl.BlockSpec((1, D_MODEL), lambda t: (0, 0))],
        out_specs=pl.BlockSpec((tile, D_MODEL), lambda t: (t, 0)),
        compiler_params=_params(("arbitrary",)),
        name="final_norm",
    )(x1, y, mod, w["final_g"])


def _rope_tables(n_tokens):
    pos = np.arange(n_tokens)
    row = (pos // GRID_W).astype(np.float64)
    col = (pos % GRID_W).astype(np.float64)

    def cs(rot_dim):
        quarter = rot_dim // 4
        inv = ROPE_THETA ** (-np.arange(quarter, dtype=np.float64) / quarter)
        ang = np.concatenate([row[:, None] * inv, col[:, None] * inv], axis=-1)
        return np.cos(ang), np.sin(ang)

    c32, s32 = cs(MLA_ROPE)
    c64, s64 = cs(HEAD_DIM)
    ones = np.ones((n_tokens, MLA_NOPE))
    zeros = np.zeros((n_tokens, MLA_NOPE))

    def rep(parts):
        period = np.concatenate(parts, axis=-1)
        return jnp.asarray(np.tile(period, (1, LANES // period.shape[-1])), f32)

    return (rep([ones, c32, c32]), rep([zeros, -s32, s32]), rep([c32, c32]), rep([-s32, s32]),
            rep([c64, c64]), rep([-s64, s64]))


def _layout_weights(norm1_g, norm2_g, w_in, mla_kv_norm_g, mla_w_uk, mla_w_uv, gqa_q_norm_g, gqa_k_norm_g,
                    diff_lambda, diff_norm_g, w_out, moe_w_group, moe_b_group, moe_w_router, moe_b_router,
                    moe_w_gate, moe_w_up, moe_w_down, final_norm_g):
    eye = jnp.eye(MLA_ROPE, dtype=f32)
    top = jnp.concatenate([mla_w_uk, jnp.zeros((DEPTH, KV_RANK, MLA_HEADS, MLA_ROPE), f32)], axis=-1)
    mid = jnp.concatenate([jnp.zeros((MLA_ROPE, MLA_HEADS, MLA_NOPE), f32),
                           jnp.broadcast_to(eye[:, None, :], (MLA_ROPE, MLA_HEADS, MLA_ROPE))], axis=-1)
    w_ka = jnp.concatenate([top.reshape(DEPTH, KV_RANK, 384),
                            jnp.broadcast_to(mid.reshape(1, MLA_ROPE, 384), (DEPTH, MLA_ROPE, 384)),
                            jnp.zeros((DEPTH, 256 - KV_RANK - MLA_ROPE, 384), f32)], axis=1).astype(bf16)
    seg_id = np.arange(512) // HEAD_DIM
    seg = jnp.asarray(seg_id[:, None] == seg_id[None, :], bf16)
    qk_g = jnp.concatenate([jnp.tile(gqa_q_norm_g, (1, GQA_HEADS)), jnp.tile(gqa_k_norm_g, (1, GQA_KV_HEADS))], axis=-1)
    return dict(
        g1=norm1_g.reshape(DEPTH, 1, D_MODEL), g2=norm2_g.reshape(DEPTH, 1, D_MODEL), w_in=w_in,
        kv_g=mla_kv_norm_g.reshape(DEPTH, 1, KV_RANK), qk_g=qk_g.reshape(DEPTH, 1, 512), seg=seg, w_ka=w_ka,
        w_uv=mla_w_uv.reshape(DEPTH, KV_RANK, 384), lam=diff_lambda, diff_g=diff_norm_g.reshape(DEPTH, 1, DIFF_V),
        w_out=w_out, w_grp=moe_w_group, b_grp=moe_b_group.reshape(DEPTH, 1, N_GROUPS), w_rtr=moe_w_router,
        b_rtr=moe_b_router.reshape(DEPTH, 1, N_EXPERTS), w_gate=moe_w_gate, w_up=moe_w_up, w_down=moe_w_down,
        final_g=final_norm_g.reshape(1, D_MODEL))


PRE_TILE = 512
CTX_ATTN_TILE = 256
LAT_ATTN_TILE = 256
CTX_HEADS_PER_ROUND = 20
LAT_HEADS_PER_ROUND = 5
POST_TILE = 512
FINAL_TILE = 1024


def kernel(x_prompt, x_sample, c, cache_mla_ckv, cache_mla_krope, cache_gqa_k, cache_gqa_v, cache_diff_k, cache_diff_v, c_ctx, norm1_g, norm2_g, w_mod, b_mod, w_in, mla_kv_norm_g, mla_w_uk, mla_w_uv, gqa_q_norm_g, gqa_k_norm_g, diff_lambda, diff_norm_g, w_out, moe_w_group, moe_b_group, moe_w_router, moe_b_router, moe_w_gate, moe_w_up, moe_w_down, final_norm_g):
    B, S, _ = x_prompt.shape
    Bl, Sl, _ = x_sample.shape
    n_ctx, n_lat = B * S, Bl * Sl
    total = n_ctx + n_lat
    assert S == CTX_ATTN_TILE and Bl + 1 <= MOD_ROWS and DEPTH == 2 and total % SC_ROWS == 0
    slot_rows = -(-(total + N_CLASSES * EXPERT_TILE) // SC_ROWS) * SC_ROWS
    max_tiles = slot_rows // EXPERT_TILE
    w = _layout_weights(norm1_g, norm2_g, w_in, mla_kv_norm_g, mla_w_uk, mla_w_uv, gqa_q_norm_g, gqa_k_norm_g,
                        diff_lambda, diff_norm_g, w_out, moe_w_group, moe_b_group, moe_w_router, moe_b_router,
                        moe_w_gate, moe_w_up, moe_w_down, final_norm_g)
    cond = jnp.concatenate([c_ctx[None, :], c, jnp.zeros((MOD_ROWS - 1 - Bl, D_MODEL), f32)], axis=0)
    mod = _modulation(cond, w_mod, b_mod).reshape(DEPTH, MOD_ROWS, N_MOD, D_MODEL)
    ctx_row = lambda token: 0
    lat_row = lambda token: 1 + token // Sl
    tabs = _rope_tables(Sl)
    kv_past = _cache_rows((cache_mla_ckv, cache_mla_krope, cache_gqa_k, cache_gqa_v, cache_diff_k, cache_diff_v), w)
    per_b = Sl // LAT_ATTN_TILE

    x_ctx, x_lat = x_prompt.reshape(n_ctx, D_MODEL), x_sample.reshape(n_lat, D_MODEL)
    cache = ()
    x1 = y = None
    for i in range(DEPTH):
        if i == 0:
            q, kv, *cache = _pre(i, x_ctx, n_ctx, 0, S, mod, ctx_row, w, PRE_TILE)
        else:
            q, kv, x_ctx, *cache = _pre(i, x1, n_ctx, 0, S, mod, ctx_row, w, PRE_TILE, prev_cache=cache, resid=y)
        own = (kv.reshape(1, 1, n_ctx, KV_COLS), S, lambda t: (0, 0, t, 0))
        o = _attention(i, q, [own], w, CTX_ATTN_TILE, CTX_HEADS_PER_ROUND)
        if i == 0:
            q_l, kv_l = _pre(i, x_lat, n_lat, 0, Sl, mod, lat_row, w, PRE_TILE, rope_tabs=tabs)
        else:
            q_l, kv_l, x_lat = _pre(i, x1, n_lat, n_ctx, Sl, mod, lat_row, w, PRE_TILE, rope_tabs=tabs, resid=y)
        past = (kv_past, PAST_LEN, lambda t, i=i: (i, t // per_b, 0, 0))
        own = (kv_l.reshape(1, Bl, Sl, KV_COLS), Sl, lambda t: (0, t // per_b, 0, 0))
        o_l = _attention(i, q_l, [past, own], w, LAT_ATTN_TILE, LAT_HEADS_PER_ROUND)
        merged = _post(i, o, x_ctx, 0, total, mod, ctx_row, w, POST_TILE)
        x1, hx, cls = _post(i, o_l, x_lat, n_ctx, total, mod, lat_row, w, POST_TILE, merged=merged)
        slot, e_lo, e_hi, valid, n_tiles = _plan(cls, max_tiles)
        xs = _move_rows(hx, slot, slot_rows, scatter=True)
        ys = _experts(i, xs, e_lo, e_hi, valid, n_tiles, w)
        y = _move_rows(ys, slot, total, scatter=False)

    y_prompt = _final(x1, y, n_ctx, 0, mod, ctx_row, w, FINAL_TILE).reshape(B, S, D_MODEL)
    y_sample = _final(x1, y, n_lat, n_ctx, mod, lat_row, w, FINAL_TILE).reshape(Bl, Sl, D_MODEL)
    new_mla_ckv, new_mla_krope = cache[0], cache[1]
    new_gqa_k = cache[2].reshape(B, DEPTH, S, GQA_KV_HEADS, HEAD_DIM)
    new_gqa_v = cache[3].reshape(B, DEPTH, S, GQA_KV_HEADS, HEAD_DIM)
    new_diff_k = cache[4].reshape(B, DEPTH, S, DIFF_HEADS, 2, DIFF_QK)
    new_diff_v = cache[5].reshape(B, DEPTH, S, DIFF_HEADS, DIFF_V)
    return (y_prompt, y_sample, new_mla_ckv, new_mla_krope, new_gqa_k, new_gqa_v, new_diff_k, new_diff_v)
```

```python
import functools
import math

import jax
import jax.numpy as jnp
import numpy as np
from jax.experimental import pallas as pl
from jax.experimental.pallas import tpu as pltpu
from jax.experimental.pallas import tpu_sc as plsc

D_MODEL = 1024
DEPTH = 2
PAST_LEN = 512
GRID_W = 64
ROPE_THETA = 10000.0
EPS = 1e-6
LOG2E = 1.4426950408889634
N_MOD = 6
HEAD_DIM = 64
MLA_HEADS = 6
MLA_NOPE = 32
MLA_ROPE = 32
MLA_V = 64
KV_RANK = 128
GQA_HEADS = 6
GQA_KV_HEADS = 2
GQA_GROUP = GQA_HEADS // GQA_KV_HEADS
DIFF_HEADS = 4
DIFF_QK = 32
DIFF_V = 64
N_GROUPS = 4
EXPERTS_PER_GROUP = 4
N_EXPERTS = N_GROUPS * EXPERTS_PER_GROUP
D_FF_EXPERT = 256

LANES = 128
MOD_ROWS = 8

IN_COLS = 1952
IN_KR = 512
Z_QA, Z_CKV, Z_QG, Z_KG, Z_VG, Z_QD, Z_KD, Z_VD, Z_KR = 0, 384, 512, 896, 1024, 1152, 1408, 1664, 1920
Z_COLS = 2048
Q_A, Q_G, Q_D, Q_COLS = 0, 384, 768, 1024
KV_KA, KV_VA, KV_KG, KV_VG, KV_KD, KV_VD, KV_COLS = 0, 384, 768, 896, 1024, 1280, 1536
CACHE_WIDTHS = (128, 32, 128, 128, 256, 256)

PAIR_LO = (0, 0, 0, 1, 1, 2)
PAIR_HI = (1, 2, 3, 3, 2, 3)
N_PAIRS = len(PAIR_LO)
N_CLASSES = N_GROUPS * N_PAIRS
HX_HALF = D_MODEL // 2
HX_COLS = HX_HALF + LANES
EXPERT_TILE = 256

SC_CORES, SC_SUBCORES = 2, 16
SC_CHUNK = 64
SC_ROWS = SC_CORES * SC_SUBCORES * SC_CHUNK

VMEM_LIMIT = 56 * 1024 * 1024

bf16 = jnp.bfloat16
f32 = jnp.float32
i32 = jnp.int32


def _dot(a, b):
    return jnp.dot(a, b, preferred_element_type=f32)


def _dot_nt(a, b):
    return jax.lax.dot_general(a, b, (((1,), (1,)), ((), ())), preferred_element_type=f32)


def _rms(x, width):
    return x * jax.lax.rsqrt(jnp.sum(x * x, axis=-1, keepdims=True) * (1.0 / width) + EPS)


def _silu(x):
    return x * (1.0 / (1.0 + jnp.exp(-x)))


def _params(sem):
    return pltpu.CompilerParams(dimension_semantics=sem, vmem_limit_bytes=VMEM_LIMIT)


def _mod_kernel(cond_ref, w_ref, b_ref, o_ref):
    o_ref[0] = _dot(_silu(cond_ref[...]).astype(bf16), w_ref[0].astype(bf16)) + b_ref[0]


def _modulation(cond, w_mod, b_mod):
    return pl.pallas_call(
        _mod_kernel,
        out_shape=jax.ShapeDtypeStruct((DEPTH, MOD_ROWS, N_MOD * D_MODEL), f32),
        grid=(DEPTH, N_MOD),
        in_specs=[
            pl.BlockSpec((MOD_ROWS, D_MODEL), lambda i, j: (0, 0)),
            pl.BlockSpec((1, D_MODEL, D_MODEL), lambda i, j: (i, 0, j)),
            pl.BlockSpec((1, 1, D_MODEL), lambda i, j: (i, 0, j)),
        ],
        out_specs=pl.BlockSpec((1, MOD_ROWS, D_MODEL), lambda i, j: (i, 0, j)),
        compiler_params=_params(("arbitrary", "arbitrary")),
        name="modulation",
    )(cond, w_mod, b_mod.reshape(DEPTH, 1, N_MOD * D_MODEL))


def _swap_halves(x, half):
    lane = jax.lax.broadcasted_iota(i32, x.shape, 1)
    fwd = pltpu.roll(x, LANES - half, 1)
    bwd = pltpu.roll(x, half, 1)
    return jnp.where((lane & (2 * half - 1)) < half, fwd, bwd)


def _rope_block(x, cos, sin, half):
    return x * cos + _swap_halves(x, half) * sin


def _pre_kernel(rope, n_prev, resid, *refs):
    it = iter(refs)
    x_ref, mod_ref, g1_ref, w_in_ref, kvg_ref, qkg_ref, seg_ref, wka_ref, wuv_ref = (next(it) for _ in range(9))
    if resid:
        y_ref, pmod_ref = next(it), next(it)
    if rope:
        ca_ref, sa_ref, c32_ref, s32_ref, c64_ref, s64_ref = (next(it) for _ in range(6))
    prev_refs = [next(it) for _ in range(n_prev)]
    q_ref, kv_ref = next(it), next(it)
    if resid:
        x2_ref = next(it)
    cache_refs = [] if rope else [next(it) for _ in range(len(CACHE_WIDTHS))]
    w_scr = next(it)

    @pl.when(pl.program_id(0) == 0)
    def _():
        rows = 256
        for r in range(0, D_MODEL, rows):
            wf = w_in_ref[0, r:r + rows, :]
            w_scr[r:r + rows, 0:IN_KR] = wf[:, 0:IN_KR].astype(bf16)
            w_scr[r:r + rows, IN_KR:Z_KR] = wf[:, IN_KR + MLA_ROPE:IN_COLS].astype(bf16)
            w_scr[r:r + rows, Z_KR:Z_KR + MLA_ROPE] = wf[:, IN_KR:IN_KR + MLA_ROPE].astype(bf16)
            w_scr[r:r + rows, Z_KR + MLA_ROPE:Z_COLS] = jnp.zeros((rows, Z_COLS - Z_KR - MLA_ROPE), bf16)

    x = x_ref[...]
    if resid:
        x = x + pmod_ref[0, 0, 5:6, :] * y_ref[...]
        x2_ref[...] = x
    shift1 = mod_ref[0, 0, 0:1, :]
    scale1 = mod_ref[0, 0, 1:2, :]
    h = (_rms(x, D_MODEL) * g1_ref[0]) * (1.0 + scale1) + shift1
    z = _dot(h.astype(bf16), w_scr[...])

    ckv = _rms(z[:, Z_CKV:Z_CKV + KV_RANK], KV_RANK) * kvg_ref[0]

    qk = z[:, Z_QG:Z_VG]
    sq = qk * qk
    sq_hi = sq.astype(bf16)
    sq_lo = (sq - sq_hi.astype(f32)).astype(bf16)
    seg = seg_ref[...]
    ms = (_dot(sq_hi, seg) + _dot(sq_lo, seg)) * (1.0 / HEAD_DIM)
    qk = qk * jax.lax.rsqrt(ms + EPS) * qkg_ref[0]

    def blocks(arr, n):
        return [arr[:, LANES * j:LANES * (j + 1)] for j in range(n)]

    qa = blocks(z[:, Z_QA:Z_QA + 384], 3)
    qkb = blocks(qk, 4)
    qd = blocks(z[:, Z_QD:Z_QD + 256], 2)
    kd = blocks(z[:, Z_KD:Z_KD + 256], 2)
    kr = z[:, Z_KR:Z_KR + LANES]
    if rope:
        ca, sa, c32, s32, c64, s64 = (r[...] for r in (ca_ref, sa_ref, c32_ref, s32_ref, c64_ref, s64_ref))
        qa = [_rope_block(b, ca, sa, MLA_ROPE // 2) for b in qa]
        qkb = [_rope_block(b, c64, s64, HEAD_DIM // 2) for b in qkb]
        qd = [_rope_block(b, c32, s32, DIFF_QK // 2) for b in qd]
        kd = [_rope_block(b, c32, s32, DIFF_QK // 2) for b in kd]
        kr = _rope_block(kr, c32, s32, MLA_ROPE // 2)

    vg = z[:, Z_VG:Z_VG + 128]
    vd = z[:, Z_VD:Z_VD + 256]
    ckv_b = ckv.astype(bf16)
    k_a = _dot(jnp.concatenate([ckv_b, kr.astype(bf16)], axis=1), wka_ref[0])
    v_a = _dot(ckv_b, wuv_ref[0].astype(bf16))

    for j in range(3):
        q_ref[:, Q_A + LANES * j:Q_A + LANES * (j + 1)] = (qa[j] * (HEAD_DIM ** -0.5 * LOG2E)).astype(bf16)
        q_ref[:, Q_G + LANES * j:Q_G + LANES * (j + 1)] = (qkb[j] * (HEAD_DIM ** -0.5 * LOG2E)).astype(bf16)
    for j in range(2):
        q_ref[:, Q_D + LANES * j:Q_D + LANES * (j + 1)] = (qd[j] * (DIFF_QK ** -0.5 * LOG2E)).astype(bf16)
        kv_ref[:, KV_KD + LANES * j:KV_KD + LANES * (j + 1)] = kd[j].astype(bf16)
    kv_ref[:, KV_KA:KV_KA + 384] = k_a.astype(bf16)
    kv_ref[:, KV_VA:KV_VA + 384] = v_a.astype(bf16)
    kv_ref[:, KV_KG:KV_KG + 128] = qkb[3].astype(bf16)
    kv_ref[:, KV_VG:KV_VG + 128] = vg.astype(bf16)
    kv_ref[:, KV_VD:KV_VD + 256] = vd.astype(bf16)
    if not rope:
        rows = [ckv, kr[:, :MLA_ROPE], qkb[3], vg, jnp.concatenate(kd, axis=1), vd]
        for out, new in zip(cache_refs, rows):
            reqs, _, seq, width = out.shape
            out[:, 0] = new.reshape(reqs, seq, width)


def _pre(layer, x, n, row0, seq, mod, mod_row, w, tile, rope_tabs=None, prev_cache=(), resid=None):
    rope = rope_tabs is not None
    lay = lambda t: (layer, 0, 0)
    row = lambda t: (t, 0)
    off = row0 // tile
    src_row = lambda t: (off + t, 0)
    in_specs = [
        pl.BlockSpec((tile, D_MODEL), src_row),
        pl.BlockSpec((1, 1, N_MOD, D_MODEL), lambda t: (layer, mod_row(t * tile), 0, 0)),
        pl.BlockSpec((1, 1, D_MODEL), lay),
        pl.BlockSpec((1, D_MODEL, IN_COLS), lay),
        pl.BlockSpec((1, 1, KV_RANK), lay),
        pl.BlockSpec((1, 1, 512), lay),
        pl.BlockSpec((512, 512), lambda t: (0, 0)),
        pl.BlockSpec((1, 256, 384), lay),
        pl.BlockSpec((1, KV_RANK, 384), lay),
    ]
    args = [x, mod, w["g1"], w["w_in"], w["kv_g"], w["qk_g"], w["seg"], w["w_ka"], w["w_uv"]]
    if resid is not None:
        in_specs += [pl.BlockSpec((tile, D_MODEL), src_row),
                     pl.BlockSpec((1, 1, N_MOD, D_MODEL), lambda t: (layer - 1, mod_row(t * tile), 0, 0))]
        args += [resid, mod]
    if rope:
        per_b = seq // tile
        in_specs += [pl.BlockSpec((tile, LANES), lambda t: (t % per_b, 0))] * 6
        args += list(rope_tabs)
    out_shape = [jax.ShapeDtypeStruct((n, Q_COLS), bf16), jax.ShapeDtypeStruct((n, KV_COLS), bf16)]
    out_specs = [pl.BlockSpec((tile, Q_COLS), row), pl.BlockSpec((tile, KV_COLS), row)]
    if resid is not None:
        out_shape.append(jax.ShapeDtypeStruct((n, D_MODEL), f32))
        out_specs.append(pl.BlockSpec((tile, D_MODEL), row))
    aliases = {}
    if not rope:
        reqs = tile // seq
        for j, width in enumerate(CACHE_WIDTHS):
            if prev_cache:
                aliases[len(args)] = len(out_shape)
                in_specs.append(pl.BlockSpec(memory_space=pl.ANY))
                args.append(prev_cache[j])
            out_shape.append(jax.ShapeDtypeStruct((n // seq, DEPTH, seq, width), f32))
            out_specs.append(pl.BlockSpec((reqs, 1, seq, width), lambda t: (t, layer, 0, 0)))
    return pl.pallas_call(
        functools.partial(_pre_kernel, rope, len(prev_cache), resid is not None),
        out_shape=out_shape,
        grid=(n // tile,),
        in_specs=in_specs,
        out_specs=out_specs,
        input_output_aliases=aliases,
        scratch_shapes=[pltpu.VMEM((D_MODEL, Z_COLS), bf16)],
        compiler_params=_params(("arbitrary",)),
        name="pre_latent" if rope else "pre_context",
    )(*args)


def _cache_kernel(ckv_ref, kr_ref, kg_ref, vg_ref, kd_ref, vd_ref, wka_ref, wuv_ref, kv_ref):
    ckv_b = ckv_ref[0, 0].astype(bf16)
    wka = wka_ref[0]
    k_a = _dot(ckv_b, wka[:KV_RANK]) + _dot(kr_ref[0, 0].astype(bf16), wka[KV_RANK:KV_RANK + MLA_ROPE])
    kv_ref[0, 0, :, KV_KA:KV_KA + 384] = k_a.astype(bf16)
    kv_ref[0, 0, :, KV_VA:KV_VA + 384] = _dot(ckv_b, wuv_ref[0].astype(bf16)).astype(bf16)
    kv_ref[0, 0, :, KV_KG:KV_KG + 128] = kg_ref[0, 0].astype(bf16)
    kv_ref[0, 0, :, KV_VG:KV_VG + 128] = vg_ref[0, 0].astype(bf16)
    kv_ref[0, 0, :, KV_KD:KV_KD + 256] = kd_ref[0, 0].astype(bf16)
    kv_ref[0, 0, :, KV_VD:KV_VD + 256] = vd_ref[0, 0].astype(bf16)


def _cache_rows(caches, w):
    B = caches[0].shape[0]
    spec = lambda width: pl.BlockSpec((1, 1, PAST_LEN, width), lambda i, b: (b, i, 0, 0))
    return pl.pallas_call(
        _cache_kernel,
        out_shape=jax.ShapeDtypeStruct((DEPTH, B, PAST_LEN, KV_COLS), bf16),
        grid=(DEPTH, B),
        in_specs=[spec(width) for width in CACHE_WIDTHS]
        + [pl.BlockSpec((1, 256, 384), lambda i, b: (i, 0, 0)), pl.BlockSpec((1, KV_RANK, 384), lambda i, b: (i, 0, 0))],
        out_specs=pl.BlockSpec((1, 1, PAST_LEN, KV_COLS), lambda i, b: (i, b, 0, 0)),
        compiler_params=_params(("arbitrary", "arbitrary")),
        name="cache_rows",
    )(*[c.reshape(B, DEPTH, PAST_LEN, width) for c, width in zip(caches, CACHE_WIDTHS)], w["w_ka"], w["w_uv"])


_SCORE_HEADS = (
    [(Q_A + 64 * h, KV_KA + 64 * h, 64, KV_VA + MLA_V * h) for h in range(MLA_HEADS)]
    + [(Q_G + 64 * h, KV_KG + 64 * (h // GQA_GROUP), 64, KV_VG + 64 * (h // GQA_GROUP)) for h in range(GQA_HEADS)]
    + [(Q_D + 64 * h + DIFF_QK * c, KV_KD + 64 * h + DIFF_QK * c, DIFF_QK, KV_VD + DIFF_V * h)
       for h in range(DIFF_HEADS) for c in range(2)])


def _attn_kernel(lam_init, per_round, n_src, q_ref, *refs):
    kv_refs = refs[:n_src]
    lam_ref, dg_ref, o_ref, s_ref, p_ref = refs[n_src:]
    spans, start = [], 0
    for r in kv_refs:
        spans.append((r, start, r.shape[2]))
        start += r.shape[2]

    outs = []
    for first in range(0, len(_SCORE_HEADS), per_round):
        chunk = _SCORE_HEADS[first:first + per_round]
        for j, (q_off, k_off, width, _) in enumerate(chunk):
            for r, lo, size in spans:
                s_ref[j, :, lo:lo + size] = _dot_nt(q_ref[:, q_off:q_off + width], r[0, 0, :, k_off:k_off + width])
        s = s_ref[...]
        p = jnp.exp2(s - jnp.max(s, axis=-1, keepdims=True))
        inv = 1.0 / jnp.sum(p, axis=-1, keepdims=True)
        p_ref[...] = p.astype(bf16)
        for j, (_, _, _, v_off) in enumerate(chunk):
            o = sum(_dot(p_ref[j, :, lo:lo + size], r[0, 0, :, v_off:v_off + DIFF_V]) for r, lo, size in spans)
            outs.append(o * inv[j])

    lp = lam_ref[0]
    e1 = jnp.exp(jnp.sum(lp[0:1] * lp[1:2], axis=-1, keepdims=True))
    e2 = jnp.exp(jnp.sum(lp[2:3] * lp[3:4], axis=-1, keepdims=True))
    lam = e1 - e2 + lam_init
    heads = outs[:MLA_HEADS + GQA_HEADS]
    for h in range(DIFF_HEADS):
        o1, o2 = outs[MLA_HEADS + GQA_HEADS + 2 * h:MLA_HEADS + GQA_HEADS + 2 * h + 2]
        heads.append(_rms(o1 - lam * o2, DIFF_V) * dg_ref[0] * (1.0 - lam_init))
    for j in range(len(heads) // 2):
        o_ref[:, LANES * j:LANES * (j + 1)] = jnp.concatenate(heads[2 * j:2 * j + 2], axis=1).astype(bf16)


def _attention(layer, q, sources, w, tile, per_round):
    n = q.shape[0]
    lam_init = 0.8 - 0.6 * math.exp(-0.3 * layer)
    s_kv = sum(rows for _, rows, _ in sources)
    assert len(_SCORE_HEADS) % per_round == 0
    return pl.pallas_call(
        functools.partial(_attn_kernel, lam_init, per_round, len(sources)),
        out_shape=jax.ShapeDtypeStruct((n, D_MODEL), bf16),
        scratch_shapes=[pltpu.VMEM((per_round, tile, s_kv), f32), pltpu.VMEM((per_round, tile, s_kv), bf16)],
        grid=(n // tile,),
        in_specs=[pl.BlockSpec((tile, Q_COLS), lambda t: (t, 0))]
        + [pl.BlockSpec((1, 1, rows, KV_COLS), index) for _, rows, index in sources]
        + [pl.BlockSpec((1, 4, DIFF_QK), lambda t: (layer, 0, 0)), pl.BlockSpec((1, 1, DIFF_V), lambda t: (layer, 0, 0))],
        out_specs=pl.BlockSpec((tile, D_MODEL), lambda t: (t, 0)),
        compiler_params=_params(("arbitrary",)),
        name="attention",
    )(q, *[arr for arr, _, _ in sources], w["lam"], w["diff_g"])


def _post_kernel(merge, *refs):
    it = iter(refs)
    o_ref, x_ref, mod_ref, w_out_ref, g2_ref, wg_ref, bg_ref, we_ref, be_ref = (next(it) for _ in range(9))
    if merge:
        next(it), next(it), next(it)
    x1_ref, hx_ref, cls_ref, w_scr = (next(it) for _ in range(4))

    @pl.when(pl.program_id(0) == 0)
    def _():
        w_scr[...] = w_out_ref[0].astype(bf16)

    gate1 = mod_ref[0, 0, 2:3, :]
    shift2 = mod_ref[0, 0, 3:4, :]
    scale2 = mod_ref[0, 0, 4:5, :]
    x1 = x_ref[...] + gate1 * _dot(o_ref[...], w_scr[...])
    x1_ref[...] = x1
    h2 = ((_rms(x1, D_MODEL) * g2_ref[0]) * (1.0 + scale2) + shift2).astype(bf16)
    bits = pltpu.bitcast(h2.astype(f32), i32)
    hx_ref[:, 0:HX_HALF] = bits[:, 0:HX_HALF] | jax.lax.shift_right_logical(bits[:, HX_HALF:D_MODEL], 16)

    def first_lane(mask, lane_f):
        return jnp.min(jnp.where(mask, lane_f, float(LANES)), axis=-1, keepdims=True)

    gl = _dot(h2, wg_ref[0].astype(bf16)) + bg_ref[0]
    glane = jax.lax.broadcasted_iota(i32, gl.shape, 1).astype(f32)
    ge = jnp.exp(gl - jnp.max(gl, axis=-1, keepdims=True))
    gprob = ge / jnp.sum(ge, axis=-1, keepdims=True)
    g_top = jnp.max(gprob, axis=-1, keepdims=True)
    g_idx = first_lane(gprob == g_top, glane)

    el = _dot(h2, we_ref[0].astype(bf16)) + be_ref[0]
    lane = jax.lax.broadcasted_iota(i32, el.shape, 1)
    lane_f = lane.astype(f32)
    emask = (lane >> 2).astype(f32) == g_idx
    em = jnp.where(emask, el, -jnp.inf)
    ee = jnp.where(emask, jnp.exp(em - jnp.max(em, axis=-1, keepdims=True)), 0.0)
    ep = ee / jnp.sum(ee, axis=-1, keepdims=True)
    p1 = jnp.max(jnp.where(emask, ep, -1.0), axis=-1, keepdims=True)
    i1 = first_lane(emask & (ep == p1), lane_f)
    rest = emask & (lane_f != i1)
    p2 = jnp.max(jnp.where(rest, ep, -1.0), axis=-1, keepdims=True)
    i2 = first_lane(rest & (ep == p2), lane_f)
    tot = p1 + p2
    w1 = g_top * (p1 / tot)
    w2 = g_top * (p2 / tot)

    lo = jnp.minimum(i1, i2) - EXPERTS_PER_GROUP * g_idx
    hi = jnp.maximum(i1, i2) - EXPERTS_PER_GROUP * g_idx
    pair = jnp.where(lo == 0.0, hi - 1.0, jnp.where(lo == 1.0, jnp.where(hi == 3.0, 3.0, 4.0), 5.0))
    cls_ref[...] = (N_PAIRS * g_idx + pair).astype(i32)
    g_lo = jnp.where(i1 < i2, w1, w2)
    g_hi = jnp.where(i1 < i2, w2, w1)
    tail_lane = jax.lax.broadcasted_iota(i32, (h2.shape[0], LANES), 1)
    hx_ref[:, HX_HALF:HX_COLS] = pltpu.bitcast(
        jnp.where(tail_lane == 0, g_lo, jnp.where(tail_lane == 1, g_hi, 0.0)), i32)


def _post(layer, o, x, row0, total, mod, mod_row, w, tile, merged=None):
    n = o.shape[0]
    lay = lambda t: (layer, 0, 0)
    row = lambda t: (t, 0)
    off = row0 // tile
    out_row = lambda t: (off + t, 0)
    in_specs = [
        pl.BlockSpec((tile, D_MODEL), row),
        pl.BlockSpec((tile, D_MODEL), row),
        pl.BlockSpec((1, 1, N_MOD, D_MODEL), lambda t: (layer, mod_row(t * tile), 0, 0)),
        pl.BlockSpec((1, D_MODEL, D_MODEL), lay),
        pl.BlockSpec((1, 1, D_MODEL), lay),
        pl.BlockSpec((1, D_MODEL, N_GROUPS), lay),
        pl.BlockSpec((1, 1, N_GROUPS), lay),
        pl.BlockSpec((1, D_MODEL, N_EXPERTS), lay),
        pl.BlockSpec((1, 1, N_EXPERTS), lay),
    ]
    args = [o, x, mod, w["w_out"], w["g2"], w["w_grp"], w["b_grp"], w["w_rtr"], w["b_rtr"]]
    aliases = {}
    if merged is not None:
        aliases = {len(args) + j: j for j in range(3)}
        in_specs += [pl.BlockSpec(memory_space=pl.ANY)] * 3
        args += list(merged)
    return pl.pallas_call(
        functools.partial(_post_kernel, merged is not None),
        out_shape=[jax.ShapeDtypeStruct((total, D_MODEL), f32), jax.ShapeDtypeStruct((total, HX_COLS), i32),
                   jax.ShapeDtypeStruct((total, 1), i32)],
        grid=(n // tile,),
        in_specs=in_specs,
        out_specs=[pl.BlockSpec((tile, D_MODEL), out_row), pl.BlockSpec((tile, HX_COLS), out_row),
                   pl.BlockSpec((tile, 1), out_row)],
        scratch_shapes=[pltpu.VMEM((D_MODEL, D_MODEL), bf16)],
        input_output_aliases=aliases,
        compiler_params=_params(("arbitrary",)),
        name="post_attention",
    )(*args)


PLAN_CHUNK = 1024
TAB_ROWS = LANES


def _plan_kernel(cls_ref, slot_ref, tab_ref, rank_scr):
    n = cls_ref.shape[0]
    lane = jax.lax.broadcasted_iota(i32, (PLAN_CHUNK, LANES), 1)
    r = jax.lax.broadcasted_iota(i32, (PLAN_CHUNK, PLAN_CHUNK), 0)
    c = jax.lax.broadcasted_iota(i32, (PLAN_CHUNK, PLAN_CHUNK), 1)
    before = (c < r).astype(bf16)

    def count(b, seen):
        rows = pl.ds(pl.multiple_of(b * PLAN_CHUNK, PLAN_CHUNK), PLAN_CHUNK)
        onehot = (cls_ref[rows, :] == lane).astype(f32)
        ahead = _dot(before, onehot.astype(bf16)) + seen
        rank_scr[rows, :] = jnp.sum(onehot * ahead, axis=-1, keepdims=True)
        return seen + jnp.sum(onehot, axis=0, keepdims=True)

    counts = jax.lax.fori_loop(0, n // PLAN_CHUNK, count, jnp.zeros((1, LANES), f32))
    tiles = jnp.floor((counts + (EXPERT_TILE - 1)) * (1.0 / EXPERT_TILE))
    rr = jax.lax.broadcasted_iota(i32, (LANES, LANES), 0)
    cc = jax.lax.broadcasted_iota(i32, (LANES, LANES), 1)
    ends = _dot(jnp.broadcast_to(tiles, (8, LANES)).astype(bf16), (rr <= cc).astype(bf16))[0:1]
    starts = ends - tiles

    def place(b, carry):
        rows = pl.ds(pl.multiple_of(b * PLAN_CHUNK, PLAN_CHUNK), PLAN_CHUNK)
        first = jnp.sum(jnp.where(cls_ref[rows, :] == lane, starts, 0.0), axis=-1, keepdims=True)
        slot_ref[rows, :] = (first * EXPERT_TILE + rank_scr[rows, :]).astype(i32)
        return carry

    jax.lax.fori_loop(0, n // PLAN_CHUNK, place, 0)

    tl = jax.lax.broadcasted_iota(i32, (TAB_ROWS, LANES), 1)
    n_tiles = jnp.sum(jnp.where(tl[0:1] == N_CLASSES - 1, ends, 0.0), axis=-1, keepdims=True)
    k = jnp.minimum(jax.lax.broadcasted_iota(i32, (TAB_ROWS, 1), 0).astype(f32), n_tiles - 1.0)
    cls_k = jnp.sum(jnp.where((tl < N_CLASSES) & (ends <= k), 1.0, 0.0), axis=-1, keepdims=True)
    cls_k = jnp.minimum(cls_k, N_CLASSES - 1.0)
    mine = tl.astype(f32) == cls_k
    used = jnp.sum(jnp.where(mine, counts, 0.0), axis=-1, keepdims=True)
    first = jnp.sum(jnp.where(mine, starts, 0.0), axis=-1, keepdims=True)
    valid = jnp.clip(used - (k - first) * EXPERT_TILE, 0.0, float(EXPERT_TILE))
    group = jnp.floor((cls_k + 0.5) * (1.0 / N_PAIRS))
    pair = cls_k - N_PAIRS * group
    lo = hi = jnp.zeros_like(pair)
    for p in range(N_PAIRS):
        lo = jnp.where(pair == p, float(PAIR_LO[p]), lo)
        hi = jnp.where(pair == p, float(PAIR_HI[p]), hi)
    e_lo = EXPERTS_PER_GROUP * group + lo
    e_hi = EXPERTS_PER_GROUP * group + hi
    tab = jnp.where(tl == 0, e_lo, jnp.where(tl == 1, e_hi, jnp.where(tl == 2, valid, jnp.where(tl == 3, n_tiles, 0.0))))
    tab_ref[...] = tab.astype(i32)


def _plan(cls, max_tiles):
    n = cls.shape[0]
    assert n % PLAN_CHUNK == 0 and max_tiles <= TAB_ROWS
    slot, tab = pl.pallas_call(
        _plan_kernel,
        out_shape=[jax.ShapeDtypeStruct((n, 1), i32), jax.ShapeDtypeStruct((TAB_ROWS, LANES), i32)],
        scratch_shapes=[pltpu.VMEM((n, 1), f32)],
        compiler_params=_params(None),
        name="dispatch_plan",
    )(cls)
    return slot.reshape(n), tab[:max_tiles, 0], tab[:max_tiles, 1], tab[:max_tiles, 2], tab[0, 3:4]


def _move_rows(src, idx, n_out, scatter):
    n = idx.shape[0]
    width = src.shape[1]
    per_worker = n // (SC_CORES * SC_SUBCORES)
    assert n % SC_ROWS == 0
    mesh = plsc.VectorSubcoreMesh(core_axis_name="c", subcore_axis_name="s")

    @functools.partial(
        pl.kernel, mesh=mesh, out_type=jax.ShapeDtypeStruct((n_out, width), src.dtype),
        scratch_types=[pltpu.VMEM((SC_CHUNK,), i32), pltpu.VMEM((SC_CHUNK, width), src.dtype),
                       pltpu.SemaphoreType.DMA])
    def move(src_hbm, idx_hbm, out_hbm, idx_v, rows_v, sem):
        wid = jax.lax.axis_index("s") * SC_CORES + jax.lax.axis_index("c")
        base = wid * per_worker

        @pl.loop(0, per_worker // SC_CHUNK)
        def _(j):
            off = base + j * SC_CHUNK
            pltpu.sync_copy(idx_hbm.at[pl.ds(off, SC_CHUNK)], idx_v)
            if scatter:
                pltpu.sync_copy(src_hbm.at[pl.ds(off, SC_CHUNK)], rows_v)
                pltpu.async_copy(rows_v, out_hbm.at[idx_v], sem).wait()
            else:
                pltpu.async_copy(src_hbm.at[idx_v], rows_v, sem).wait()
                pltpu.sync_copy(rows_v, out_hbm.at[pl.ds(off, SC_CHUNK)])

    return move(src, idx)


def _expert_kernel(lo_ref, hi_ref, valid_ref, nt_ref, xs_ref, wg_ref, wu_ref, wd_ref, ys_ref, wg_scr, wu_scr, wd_scr):
    k = pl.program_id(0)
    prev = jnp.maximum(k - 1, 0)

    @pl.when((k == 0) | (lo_ref[k] // EXPERTS_PER_GROUP != lo_ref[prev] // EXPERTS_PER_GROUP))
    def _():
        for j in range(EXPERTS_PER_GROUP):
            wg_scr[j] = wg_ref[0, 0, j].astype(bf16)
            wu_scr[j] = wu_ref[0, 0, j].astype(bf16)
            wd_scr[j] = wd_ref[0, 0, j].astype(bf16)

    @pl.when(k < nt_ref[0])
    def _():
        live = jax.lax.broadcasted_iota(i32, (EXPERT_TILE, 1), 0) < valid_ref[k]
        words = jnp.where(live, xs_ref[:, 0:HX_HALF], 0)
        x = jnp.concatenate([pltpu.bitcast(words & -65536, f32), pltpu.bitcast(words << 16, f32)], axis=1).astype(bf16)
        gates = pltpu.bitcast(jnp.where(live, xs_ref[:, HX_HALF:HX_COLS], 0), f32)
        y = None
        for lane, e_ref in enumerate((lo_ref, hi_ref)):
            j = e_ref[k] % EXPERTS_PER_GROUP
            hid = _silu(_dot(x, wg_scr[j])) * _dot(x, wu_scr[j]) * gates[:, lane:lane + 1]
            part = _dot(hid.astype(bf16), wd_scr[j])
            y = part if y is None else y + part
        ys_ref[...] = y


def _experts(layer, xs, e_lo, e_hi, valid, n_tiles, w):
    max_tiles = e_lo.shape[0]
    row = lambda k, lo, hi, valid, nt: (jnp.minimum(k, nt[0] - 1), 0)
    group = lambda k, lo, hi, valid, nt: (layer, lo[k] // EXPERTS_PER_GROUP, 0, 0, 0)
    by_group = lambda a: a.reshape(DEPTH, N_GROUPS, EXPERTS_PER_GROUP, *a.shape[2:])
    return pl.pallas_call(
        _expert_kernel,
        out_shape=jax.ShapeDtypeStruct((max_tiles * EXPERT_TILE, D_MODEL), f32),
        grid_spec=pltpu.PrefetchScalarGridSpec(
            num_scalar_prefetch=4, grid=(max_tiles,),
            in_specs=[pl.BlockSpec((EXPERT_TILE, HX_COLS), row),
                      pl.BlockSpec((1, 1, EXPERTS_PER_GROUP, D_MODEL, D_FF_EXPERT), group),
                      pl.BlockSpec((1, 1, EXPERTS_PER_GROUP, D_MODEL, D_FF_EXPERT), group),
                      pl.BlockSpec((1, 1, EXPERTS_PER_GROUP, D_FF_EXPERT, D_MODEL), group)],
            out_specs=pl.BlockSpec((EXPERT_TILE, D_MODEL), row),
            scratch_shapes=[pltpu.VMEM((EXPERTS_PER_GROUP, D_MODEL, D_FF_EXPERT), bf16),
                            pltpu.VMEM((EXPERTS_PER_GROUP, D_MODEL, D_FF_EXPERT), bf16),
                            pltpu.VMEM((EXPERTS_PER_GROUP, D_FF_EXPERT, D_MODEL), bf16)]),
        compiler_params=_params(("arbitrary",)),
        name="experts",
    )(e_lo, e_hi, valid, n_tiles, xs, by_group(w["w_gate"]), by_group(w["w_up"]), by_group(w["w_down"]))


def _final_kernel(x1_ref, y_ref, mod_ref, fg_ref, o_ref):
    o_ref[...] = _rms(x1_ref[...] + mod_ref[0, 0, 5:6, :] * y_ref[...], D_MODEL) * fg_ref[...]


def _final(x1, y, n, row0, mod, mod_row, w, tile):
    off = row0 // tile
    src_row = lambda t: (off + t, 0)
    return pl.pallas_call(
        _final_kernel,
        out_shape=jax.ShapeDtypeStruct((n, D_MODEL), f32),
        grid=(n // tile,),
        in_specs=[pl.BlockSpec((tile, D_MODEL), src_row), pl.BlockSpec((tile, D_MODEL), src_row),
                  pl.BlockSpec((1, 1, N_MOD, D_MODEL), lambda t: (DEPTH - 1, mod_row(t * tile), 0, 0)),
                  pl.BlockSpec((1, D_MODEL), lambda t: (0, 0))],
        out_specs=pl.BlockSpec((tile, D_MODEL), lambda t: (t, 0)),
        compiler_params=_params(("arbitrary",)),
        name="final_norm",
    )(x1, y, mod, w["final_g"])


def _rope_tables(n_tokens):
    pos = np.arange(n_tokens)
    row = (pos // GRID_W).astype(np.float64)
    col = (pos % GRID_W).astype(np.float64)

    def cs(rot_dim):
        quarter = rot_dim // 4
        inv = ROPE_THETA ** (-np.arange(quarter, dtype=np.float64) / quarter)
        ang = np.concatenate([row[:, None] * inv, col[:, None] * inv], axis=-1)
        return np.cos(ang), np.sin(ang)

    c32, s32 = cs(MLA_ROPE)
    c64, s64 = cs(HEAD_DIM)
    ones = np.ones((n_tokens, MLA_NOPE))
    zeros = np.zeros((n_tokens, MLA_NOPE))

    def rep(parts):
        period = np.concatenate(parts, axis=-1)
        return jnp.asarray(np.tile(period, (1, LANES // period.shape[-1])), f32)

    return (rep([ones, c32, c32]), rep([zeros, -s32, s32]), rep([c32, c32]), rep([-s32, s32]),
            rep([c64, c64]), rep([-s64, s64]))


def _layout_weights(norm1_g, norm2_g, w_in, mla_kv_norm_g, mla_w_uk, mla_w_uv, gqa_q_norm_g, gqa_k_norm_g,
                    diff_lambda, diff_norm_g, w_out, moe_w_group, moe_b_group, moe_w_router, moe_b_router,
                    moe_w_gate, moe_w_up, moe_w_down, final_norm_g):
    eye = jnp.eye(MLA_ROPE, dtype=f32)
    top = jnp.concatenate([mla_w_uk, jnp.zeros((DEPTH, KV_RANK, MLA_HEADS, MLA_ROPE), f32)], axis=-1)
    mid = jnp.concatenate([jnp.zeros((MLA_ROPE, MLA_HEADS, MLA_NOPE), f32),
                           jnp.broadcast_to(eye[:, None, :], (MLA_ROPE, MLA_HEADS, MLA_ROPE))], axis=-1)
    w_ka = jnp.concatenate([top.reshape(DEPTH, KV_RANK, 384),
                            jnp.broadcast_to(mid.reshape(1, MLA_ROPE, 384), (DEPTH, MLA_ROPE, 384)),
                            jnp.zeros((DEPTH, 256 - KV_RANK - MLA_ROPE, 384), f32)], axis=1).astype(bf16)
    seg_id = np.arange(512) // HEAD_DIM
    seg = jnp.asarray(seg_id[:, None] == seg_id[None, :], bf16)
    qk_g = jnp.concatenate([jnp.tile(gqa_q_norm_g, (1, GQA_HEADS)), jnp.tile(gqa_k_norm_g, (1, GQA_KV_HEADS))], axis=-1)
    return dict(
        g1=norm1_g.reshape(DEPTH, 1, D_MODEL), g2=norm2_g.reshape(DEPTH, 1, D_MODEL), w_in=w_in,
        kv_g=mla_kv_norm_g.reshape(DEPTH, 1, KV_RANK), qk_g=qk_g.reshape(DEPTH, 1, 512), seg=seg, w_ka=w_ka,
        w_uv=mla_w_uv.reshape(DEPTH, KV_RANK, 384), lam=diff_lambda, diff_g=diff_norm_g.reshape(DEPTH, 1, DIFF_V),
        w_out=w_out, w_grp=moe_w_group, b_grp=moe_b_group.reshape(DEPTH, 1, N_GROUPS), w_rtr=moe_w_router,
        b_rtr=moe_b_router.reshape(DEPTH, 1, N_EXPERTS), w_gate=moe_w_gate, w_up=moe_w_up, w_down=moe_w_down,
        final_g=final_norm_g.reshape(1, D_MODEL))


PRE_TILE = 512
CTX_ATTN_TILE = 256
LAT_ATTN_TILE = 256
CTX_HEADS_PER_ROUND = 20
LAT_HEADS_PER_ROUND = 5
POST_TILE = 512
FINAL_TILE = 1024


def kernel(x_prompt, x_sample, c, cache_mla_ckv, cache_mla_krope, cache_gqa_k, cache_gqa_v, cache_diff_k, cache_diff_v, c_ctx, norm1_g, norm2_g, w_mod, b_mod, w_in, mla_kv_norm_g, mla_w_uk, mla_w_uv, gqa_q_norm_g, gqa_k_norm_g, diff_lambda, diff_norm_g, w_out, moe_w_group, moe_b_group, moe_w_router, moe_b_router, moe_w_gate, moe_w_up, moe_w_down, final_norm_g):
    B, S, _ = x_prompt.shape
    Bl, Sl, _ = x_sample.shape
    n_ctx, n_lat = B * S, Bl * Sl
    total = n_ctx + n_lat
    assert S == CTX_ATTN_TILE and Bl + 1 <= MOD_ROWS and DEPTH == 2 and total % SC_ROWS == 0
    slot_rows = -(-(total + N_CLASSES * EXPERT_TILE) // SC_ROWS) * SC_ROWS
    max_tiles = slot_rows // EXPERT_TILE
    w = _layout_weights(norm1_g, norm2_g, w_in, mla_kv_norm_g, mla_w_uk, mla_w_uv, gqa_q_norm_g, gqa_k_norm_g,
                        diff_lambda, diff_norm_g, w_out, moe_w_group, moe_b_group, moe_w_router, moe_b_router,
                        moe_w_gate, moe_w_up, moe_w_down, final_norm_g)
    cond = jnp.concatenate([c_ctx[None, :], c, jnp.zeros((MOD_ROWS - 1 - Bl, D_MODEL), f32)], axis=0)
    mod = _modulation(cond, w_mod, b_mod).reshape(DEPTH, MOD_ROWS, N_MOD, D_MODEL)
    ctx_row = lambda token: 0
    lat_row = lambda token: 1 + token // Sl
    tabs = _rope_tables(Sl)
    kv_past = _cache_rows((cache_mla_ckv, cache_mla_krope, cache_gqa_k, cache_gqa_v, cache_diff_k, cache_diff_v), w)
    per_b = Sl // LAT_ATTN_TILE

    x_ctx, x_lat = x_prompt.reshape(n_ctx, D_MODEL), x_sample.reshape(n_lat, D_MODEL)
    cache = ()
    x1 = y = None
    for i in range(DEPTH):
        if i == 0:
            q, kv, *cache = _pre(i, x_ctx, n_ctx, 0, S, mod, ctx_row, w, PRE_TILE)
        else:
            q, kv, x_ctx, *cache = _pre(i, x1, n_ctx, 0, S, mod, ctx_row, w, PRE_TILE, prev_cache=cache, resid=y)
        own = (kv.reshape(1, 1, n_ctx, KV_COLS), S, lambda t: (0, 0, t, 0))
        o = _attention(i, q, [own], w, CTX_ATTN_TILE, CTX_HEADS_PER_ROUND)
        if i == 0:
            q_l, kv_l = _pre(i, x_lat, n_lat, 0, Sl, mod, lat_row, w, PRE_TILE, rope_tabs=tabs)
        else:
            q_l, kv_l, x_lat = _pre(i, x1, n_lat, n_ctx, Sl, mod, lat_row, w, PRE_TILE, rope_tabs=tabs, resid=y)
        past = (kv_past, PAST_LEN, lambda t, i=i: (i, t // per_b, 0, 0))
        own = (kv_l.reshape(1, Bl, Sl, KV_COLS), Sl, lambda t: (0, t // per_b, 0, 0))
        o_l = _attention(i, q_l, [past, own], w, LAT_ATTN_TILE, LAT_HEADS_PER_ROUND)
        merged = _post(i, o, x_ctx, 0, total, mod, ctx_row, w, POST_TILE)
        x1, hx, cls = _post(i, o_l, x_lat, n_ctx, total, mod, lat_row, w, POST_TILE, merged=merged)
        slot, e_lo, e_hi, valid, n_tiles = _plan(cls, max_tiles)
        xs = _move_rows(hx, slot, slot_rows, scatter=True)
        ys = _experts(i, xs, e_lo, e_hi, valid, n_tiles, w)
        y = _move_rows(ys, slot, total, scatter=False)

    y_prompt = _final(x1, y, n_ctx, 0, mod, ctx_row, w, FINAL_TILE).reshape(B, S, D_MODEL)
    y_sample = _final(x1, y, n_lat, n_ctx, mod, lat_row, w, FINAL_TILE).reshape(Bl, Sl, D_MODEL)
    new_mla_ckv, new_mla_krope = cache[0], cache[1]
    new_gqa_k = cache[2].reshape(B, DEPTH, S, GQA_KV_HEADS, HEAD_DIM)
    new_gqa_v = cache[3].reshape(B, DEPTH, S, GQA_KV_HEADS, HEAD_DIM)
    new_diff_k = cache[4].reshape(B, DEPTH, S, DIFF_HEADS, 2, DIFF_QK)
    new_diff_v = cache[5].reshape(B, DEPTH, S, DIFF_HEADS, DIFF_V)
    return (y_prompt, y_sample, new_mla_ckv, new_mla_krope, new_gqa_k, new_gqa_v, new_diff_k, new_diff_v)
```

```python
import functools
import math

import jax
import jax.numpy as jnp
import numpy as np
from jax.experimental import pallas as pl
from jax.experimental.pallas import tpu as pltpu
from jax.experimental.pallas import tpu_sc as plsc

D_MODEL = 1024
DEPTH = 2
PAST_LEN = 512
GRID_W = 64
ROPE_THETA = 10000.0
EPS = 1e-6
LOG2E = 1.4426950408889634
N_MOD = 6
HEAD_DIM = 64
MLA_HEADS = 6
MLA_NOPE = 32
MLA_ROPE = 32
MLA_V = 64
KV_RANK = 128
GQA_HEADS = 6
GQA_KV_HEADS = 2
GQA_GROUP = GQA_HEADS // GQA_KV_HEADS
DIFF_HEADS = 4
DIFF_QK = 32
DIFF_V = 64
N_GROUPS = 4
EXPERTS_PER_GROUP = 4
N_EXPERTS = N_GROUPS * EXPERTS_PER_GROUP
D_FF_EXPERT = 256

LANES = 128
MOD_ROWS = 8

IN_COLS = 1952
IN_KR = 512
Z_QA, Z_CKV, Z_QG, Z_KG, Z_VG, Z_QD, Z_KD, Z_VD, Z_KR = 0, 384, 512, 896, 1024, 1152, 1408, 1664, 1920
Z_COLS = 2048
Q_A, Q_G, Q_D, Q_COLS = 0, 384, 768, 1024
KV_KA, KV_VA, KV_KG, KV_VG, KV_KD, KV_VD, KV_COLS = 0, 384, 768, 896, 1024, 1280, 1536
CACHE_WIDTHS = (128, 32, 128, 128, 256, 256)

PAIR_LO = (0, 0, 0, 1, 1, 2)
PAIR_HI = (1, 2, 3, 3, 2, 3)
N_PAIRS = len(PAIR_LO)
N_CLASSES = N_GROUPS * N_PAIRS
HX_HALF = D_MODEL // 2
HX_COLS = HX_HALF + LANES
EXPERT_TILE = 256

SC_CORES, SC_SUBCORES = 2, 16
SC_CHUNK = 64
SC_ROWS = SC_CORES * SC_SUBCORES * SC_CHUNK

VMEM_LIMIT = 56 * 1024 * 1024

bf16 = jnp.bfloat16
f32 = jnp.float32
i32 = jnp.int32


def _dot(a, b):
    return jnp.dot(a, b, preferred_element_type=f32)


def _dot_nt(a, b):
    return jax.lax.dot_general(a, b, (((1,), (1,)), ((), ())), preferred_element_type=f32)


def _rms(x, width):
    return x * jax.lax.rsqrt(jnp.sum(x * x, axis=-1, keepdims=True) * (1.0 / width) + EPS)


def _silu(x):
    return x * (1.0 / (1.0 + jnp.exp(-x)))


def _params(sem):
    return pltpu.CompilerParams(dimension_semantics=sem, vmem_limit_bytes=VMEM_LIMIT)


def _mod_kernel(cond_ref, w_ref, b_ref, o_ref):
    o_ref[0] = _dot(_silu(cond_ref[...]).astype(bf16), w_ref[0].astype(bf16)) + b_ref[0]


def _modulation(cond, w_mod, b_mod):
    return pl.pallas_call(
        _mod_kernel,
        out_shape=jax.ShapeDtypeStruct((DEPTH, MOD_ROWS, N_MOD * D_MODEL), f32),
        grid=(DEPTH, N_MOD),
        in_specs=[
            pl.BlockSpec((MOD_ROWS, D_MODEL), lambda i, j: (0, 0)),
            pl.BlockSpec((1, D_MODEL, D_MODEL), lambda i, j: (i, 0, j)),
            pl.BlockSpec((1, 1, D_MODEL), lambda i, j: (i, 0, j)),
        ],
        out_specs=pl.BlockSpec((1, MOD_ROWS, D_MODEL), lambda i, j: (i, 0, j)),
        compiler_params=_params(("arbitrary", "arbitrary")),
        name="modulation",
    )(cond, w_mod, b_mod.reshape(DEPTH, 1, N_MOD * D_MODEL))


def _swap_halves(x, half):
    lane = jax.lax.broadcasted_iota(i32, x.shape, 1)
    fwd = pltpu.roll(x, LANES - half, 1)
    bwd = pltpu.roll(x, half, 1)
    return jnp.where((lane & (2 * half - 1)) < half, fwd, bwd)


def _rope_block(x, cos, sin, half):
    return x * cos + _swap_halves(x, half) * sin


def _pre_kernel(rope, n_prev, resid, *refs):
    it = iter(refs)
    x_ref, mod_ref, g1_ref, w_in_ref, kvg_ref, qkg_ref, seg_ref, wka_ref, wuv_ref = (next(it) for _ in range(9))
    if resid:
        y_ref, pmod_ref = next(it), next(it)
    if rope:
        ca_ref, sa_ref, c32_ref, s32_ref, c64_ref, s64_ref = (next(it) for _ in range(6))
    prev_refs = [next(it) for _ in range(n_prev)]
    q_ref, kv_ref = next(it), next(it)
    if resid:
        x2_ref = next(it)
    cache_refs = [] if rope else [next(it) for _ in range(len(CACHE_WIDTHS))]
    w_scr = next(it)

    @pl.when(pl.program_id(0) == 0)
    def _():
        w_scr[0:IN_KR] = w_in_ref[0, 0:IN_KR].astype(bf16)
        w_scr[IN_KR:Z_KR] = w_in_ref[0, IN_KR + MLA_ROPE:IN_COLS].astype(bf16)
        w_scr[Z_KR:Z_KR + MLA_ROPE] = w_in_ref[0, IN_KR:IN_KR + MLA_ROPE].astype(bf16)
        w_scr[Z_KR + MLA_ROPE:Z_COLS] = jnp.zeros((Z_COLS - Z_KR - MLA_ROPE, D_MODEL), bf16)

    x = x_ref[...]
    if resid:
        x = x + pmod_ref[0, 0, 5:6, :] * y_ref[...]
        x2_ref[...] = x
    shift1 = mod_ref[0, 0, 0:1, :]
    scale1 = mod_ref[0, 0, 1:2, :]
    h = (_rms(x, D_MODEL) * g1_ref[0]) * (1.0 + scale1) + shift1
    z = _dot_nt(h.astype(bf16), w_scr[...])

    ckv = _rms(z[:, Z_CKV:Z_CKV + KV_RANK], KV_RANK) * kvg_ref[0]

    qk = z[:, Z_QG:Z_VG]
    sq = qk * qk
    sq_hi = sq.astype(bf16)
    sq_lo = (sq - sq_hi.astype(f32)).astype(bf16)
    seg = seg_ref[...]
    ms = (_dot(sq_hi, seg) + _dot(sq_lo, seg)) * (1.0 / HEAD_DIM)
    qk = qk * jax.lax.rsqrt(ms + EPS) * qkg_ref[0]

    def blocks(arr, n):
        return [arr[:, LANES * j:LANES * (j + 1)] for j in range(n)]

    qa = blocks(z[:, Z_QA:Z_QA + 384], 3)
    qkb = blocks(qk, 4)
    qd = blocks(z[:, Z_QD:Z_QD + 256], 2)
    kd = blocks(z[:, Z_KD:Z_KD + 256], 2)
    kr = z[:, Z_KR:Z_KR + LANES]
    if rope:
        ca, sa, c32, s32, c64, s64 = (r[...] for r in (ca_ref, sa_ref, c32_ref, s32_ref, c64_ref, s64_ref))
        qa = [_rope_block(b, ca, sa, MLA_ROPE // 2) for b in qa]
        qkb = [_rope_block(b, c64, s64, HEAD_DIM // 2) for b in qkb]
        qd = [_rope_block(b, c32, s32, DIFF_QK // 2) for b in qd]
        kd = [_rope_block(b, c32, s32, DIFF_QK // 2) for b in kd]
        kr = _rope_block(kr, c32, s32, MLA_ROPE // 2)

    vg = z[:, Z_VG:Z_VG + 128]
    vd = z[:, Z_VD:Z_VD + 256]
    ckv_b = ckv.astype(bf16)
    k_a = _dot(jnp.concatenate([ckv_b, kr.astype(bf16)], axis=1), wka_ref[0])
    v_a = _dot(ckv_b, wuv_ref[0].astype(bf16))

    for j in range(3):
        q_ref[:, Q_A + LANES * j:Q_A + LANES * (j + 1)] = (qa[j] * (HEAD_DIM ** -0.5 * LOG2E)).astype(bf16)
        q_ref[:, Q_G + LANES * j:Q_G + LANES * (j + 1)] = (qkb[j] * (HEAD_DIM ** -0.5 * LOG2E)).astype(bf16)
    for j in range(2):
        q_ref[:, Q_D + LANES * j:Q_D + LANES * (j + 1)] = (qd[j] * (DIFF_QK ** -0.5 * LOG2E)).astype(bf16)
        kv_ref[:, KV_KD + LANES * j:KV_KD + LANES * (j + 1)] = kd[j].astype(bf16)
    kv_ref[:, KV_KA:KV_KA + 384] = k_a.astype(bf16)
    kv_ref[:, KV_VA:KV_VA + 384] = v_a.astype(bf16)
    kv_ref[:, KV_KG:KV_KG + 128] = qkb[3].astype(bf16)
    kv_ref[:, KV_VG:KV_VG + 128] = vg.astype(bf16)
    kv_ref[:, KV_VD:KV_VD + 256] = vd.astype(bf16)
    if not rope:
        rows = [ckv, kr[:, :MLA_ROPE], qkb[3], vg, jnp.concatenate(kd, axis=1), vd]
        for out, new in zip(cache_refs, rows):
            reqs, _, seq, width = out.shape
            out[:, 0] = new.reshape(reqs, seq, width)


def _pre(layer, x, n, row0, seq, mod, mod_row, w, tile, rope_tabs=None, prev_cache=(), resid=None):
    rope = rope_tabs is not None
    lay = lambda t: (layer, 0, 0)
    row = lambda t: (t, 0)
    off = row0 // tile
    src_row = lambda t: (off + t, 0)
    in_specs = [
        pl.BlockSpec((tile, D_MODEL), src_row),
        pl.BlockSpec((1, 1, N_MOD, D_MODEL), lambda t: (layer, mod_row(t * tile), 0, 0)),
        pl.BlockSpec((1, 1, D_MODEL), lay),
        pl.BlockSpec((1, IN_COLS, D_MODEL), lay),
        pl.BlockSpec((1, 1, KV_RANK), lay),
        pl.BlockSpec((1, 1, 512), lay),
        pl.BlockSpec((512, 512), lambda t: (0, 0)),
        pl.BlockSpec((1, 256, 384), lay),
        pl.BlockSpec((1, KV_RANK, 384), lay),
    ]
    args = [x, mod, w["g1"], w["w_in"], w["kv_g"], w["qk_g"], w["seg"], w["w_ka"], w["w_uv"]]
    if resid is not None:
        in_specs += [pl.BlockSpec((tile, D_MODEL), src_row),
                     pl.BlockSpec((1, 1, N_MOD, D_MODEL), lambda t: (layer - 1, mod_row(t * tile), 0, 0))]
        args += [resid, mod]
    if rope:
        per_b = seq // tile
        in_specs += [pl.BlockSpec((tile, LANES), lambda t: (t % per_b, 0))] * 6
        args += list(rope_tabs)
    out_shape = [jax.ShapeDtypeStruct((n, Q_COLS), bf16), jax.ShapeDtypeStruct((n, KV_COLS), bf16)]
    out_specs = [pl.BlockSpec((tile, Q_COLS), row), pl.BlockSpec((tile, KV_COLS), row)]
    if resid is not None:
        out_shape.append(jax.ShapeDtypeStruct((n, D_MODEL), f32))
        out_specs.append(pl.BlockSpec((tile, D_MODEL), row))
    aliases = {}
    if not rope:
        reqs = tile // seq
        for j, width in enumerate(CACHE_WIDTHS):
            if prev_cache:
                aliases[len(args)] = len(out_shape)
                in_specs.append(pl.BlockSpec(memory_space=pl.ANY))
                args.append(prev_cache[j])
            out_shape.append(jax.ShapeDtypeStruct((n // seq, DEPTH, seq, width), f32))
            out_specs.append(pl.BlockSpec((reqs, 1, seq, width), lambda t: (t, layer, 0, 0)))
    return pl.pallas_call(
        functools.partial(_pre_kernel, rope, len(prev_cache), resid is not None),
        out_shape=out_shape,
        grid=(n // tile,),
        in_specs=in_specs,
        out_specs=out_specs,
        input_output_aliases=aliases,
        scratch_shapes=[pltpu.VMEM((Z_COLS, D_MODEL), bf16)],
        compiler_params=_params(("arbitrary",)),
        name="pre_latent" if rope else "pre_context",
    )(*args)


def _cache_kernel(ckv_ref, kr_ref, kg_ref, vg_ref, kd_ref, vd_ref, wka_ref, wuv_ref, kv_ref):
    ckv_b = ckv_ref[0, 0].astype(bf16)
    wka = wka_ref[0]
    k_a = _dot(ckv_b, wka[:KV_RANK]) + _dot(kr_ref[0, 0].astype(bf16), wka[KV_RANK:KV_RANK + MLA_ROPE])
    kv_ref[0, 0, :, KV_KA:KV_KA + 384] = k_a.astype(bf16)
    kv_ref[0, 0, :, KV_VA:KV_VA + 384] = _dot(ckv_b, wuv_ref[0].astype(bf16)).astype(bf16)
    kv_ref[0, 0, :, KV_KG:KV_KG + 128] = kg_ref[0, 0].astype(bf16)
    kv_ref[0, 0, :, KV_VG:KV_VG + 128] = vg_ref[0, 0].astype(bf16)
    kv_ref[0, 0, :, KV_KD:KV_KD + 256] = kd_ref[0, 0].astype(bf16)
    kv_ref[0, 0, :, KV_VD:KV_VD + 256] = vd_ref[0, 0].astype(bf16)


def _cache_rows(caches, w):
    B = caches[0].shape[0]
    spec = lambda width: pl.BlockSpec((1, 1, PAST_LEN, width), lambda i, b: (b, i, 0, 0))
    return pl.pallas_call(
        _cache_kernel,
        out_shape=jax.ShapeDtypeStruct((DEPTH, B, PAST_LEN, KV_COLS), bf16),
        grid=(DEPTH, B),
        in_specs=[spec(width) for width in CACHE_WIDTHS]
        + [pl.BlockSpec((1, 256, 384), lambda i, b: (i, 0, 0)), pl.BlockSpec((1, KV_RANK, 384), lambda i, b: (i, 0, 0))],
        out_specs=pl.BlockSpec((1, 1, PAST_LEN, KV_COLS), lambda i, b: (i, b, 0, 0)),
        compiler_params=_params(("arbitrary", "arbitrary")),
        name="cache_rows",
    )(*[c.reshape(B, DEPTH, PAST_LEN, width) for c, width in zip(caches, CACHE_WIDTHS)], w["w_ka"], w["w_uv"])


_SCORE_HEADS = (
    [(Q_A + 64 * h, KV_KA + 64 * h, 64, KV_VA + MLA_V * h) for h in range(MLA_HEADS)]
    + [(Q_G + 64 * h, KV_KG + 64 * (h // GQA_GROUP), 64, KV_VG + 64 * (h // GQA_GROUP)) for h in range(GQA_HEADS)]
    + [(Q_D + 64 * h + DIFF_QK * c, KV_KD + 64 * h + DIFF_QK * c, DIFF_QK, KV_VD + DIFF_V * h)
       for h in range(DIFF_HEADS) for c in range(2)])


def _attn_kernel(lam_init, per_round, n_src, q_ref, *refs):
    kv_refs = refs[:n_src]
    lam_ref, dg_ref, o_ref, s_ref, p_ref = refs[n_src:]
    spans, start = [], 0
    for r in kv_refs:
        spans.append((r, start, r.shape[2]))
        start += r.shape[2]

    outs = []
    for first in range(0, len(_SCORE_HEADS), per_round):
        chunk = _SCORE_HEADS[first:first + per_round]
        for j, (q_off, k_off, width, _) in enumerate(chunk):
            for r, lo, size in spans:
                s_ref[j, :, lo:lo + size] = _dot_nt(q_ref[:, q_off:q_off + width], r[0, 0, :, k_off:k_off + width])
        s = s_ref[...]
        p = jnp.exp2(s - jnp.max(s, axis=-1, keepdims=True))
        inv = 1.0 / jnp.sum(p, axis=-1, keepdims=True)
        p_ref[...] = p.astype(bf16)
        for j, (_, _, _, v_off) in enumerate(chunk):
            o = sum(_dot(p_ref[j, :, lo:lo + size], r[0, 0, :, v_off:v_off + DIFF_V]) for r, lo, size in spans)
            outs.append(o * inv[j])

    lp = lam_ref[0]
    e1 = jnp.exp(jnp.sum(lp[0:1] * lp[1:2], axis=-1, keepdims=True))
    e2 = jnp.exp(jnp.sum(lp[2:3] * lp[3:4], axis=-1, keepdims=True))
    lam = e1 - e2 + lam_init
    heads = outs[:MLA_HEADS + GQA_HEADS]
    for h in range(DIFF_HEADS):
        o1, o2 = outs[MLA_HEADS + GQA_HEADS + 2 * h:MLA_HEADS + GQA_HEADS + 2 * h + 2]
        heads.append(_rms(o1 - lam * o2, DIFF_V) * dg_ref[0] * (1.0 - lam_init))
    for j in range(len(heads) // 2):
        o_ref[:, LANES * j:LANES * (j + 1)] = jnp.concatenate(heads[2 * j:2 * j + 2], axis=1).astype(bf16)


def _attention(layer, q, sources, w, tile, per_round):
    n = q.shape[0]
    lam_init = 0.8 - 0.6 * math.exp(-0.3 * layer)
    s_kv = sum(rows for _, rows, _ in sources)
    assert len(_SCORE_HEADS) % per_round == 0
    return pl.pallas_call(
        functools.partial(_attn_kernel, lam_init, per_round, len(sources)),
        out_shape=jax.ShapeDtypeStruct((n, D_MODEL), bf16),
        scratch_shapes=[pltpu.VMEM((per_round, tile, s_kv), f32), pltpu.VMEM((per_round, tile, s_kv), bf16)],
        grid=(n // tile,),
        in_specs=[pl.BlockSpec((tile, Q_COLS), lambda t: (t, 0))]
        + [pl.BlockSpec((1, 1, rows, KV_COLS), index) for _, rows, index in sources]
        + [pl.BlockSpec((1, 4, DIFF_QK), lambda t: (layer, 0, 0)), pl.BlockSpec((1, 1, DIFF_V), lambda t: (layer, 0, 0))],
        out_specs=pl.BlockSpec((tile, D_MODEL), lambda t: (t, 0)),
        compiler_params=_params(("arbitrary",)),
        name="attention",
    )(q, *[arr for arr, _, _ in sources], w["lam"], w["diff_g"])


def _post_kernel(merge, *refs):
    it = iter(refs)
    o_ref, x_ref, mod_ref, w_out_ref, g2_ref, wg_ref, bg_ref, we_ref, be_ref = (next(it) for _ in range(9))
    if merge:
        next(it), next(it), next(it)
    x1_ref, hx_ref, cls_ref, w_scr = (next(it) for _ in range(4))

    @pl.when(pl.program_id(0) == 0)
    def _():
        w_scr[...] = w_out_ref[0].astype(bf16)

    gate1 = mod_ref[0, 0, 2:3, :]
    shift2 = mod_ref[0, 0, 3:4, :]
    scale2 = mod_ref[0, 0, 4:5, :]
    x1 = x_ref[...] + gate1 * _dot(o_ref[...], w_scr[...])
    x1_ref[...] = x1
    h2 = ((_rms(x1, D_MODEL) * g2_ref[0]) * (1.0 + scale2) + shift2).astype(bf16)
    bits = pltpu.bitcast(h2.astype(f32), i32)
    hx_ref[:, 0:HX_HALF] = bits[:, 0:HX_HALF] | jax.lax.shift_right_logical(bits[:, HX_HALF:D_MODEL], 16)

    def first_lane(mask, lane_f):
        return jnp.min(jnp.where(mask, lane_f, float(LANES)), axis=-1, keepdims=True)

    gl = _dot(h2, wg_ref[0].astype(bf16)) + bg_ref[0]
    glane = jax.lax.broadcasted_iota(i32, gl.shape, 1).astype(f32)
    ge = jnp.exp(gl - jnp.max(gl, axis=-1, keepdims=True))
    gprob = ge / jnp.sum(ge, axis=-1, keepdims=True)
    g_top = jnp.max(gprob, axis=-1, keepdims=True)
    g_idx = first_lane(gprob == g_top, glane)

    el = _dot(h2, we_ref[0].astype(bf16)) + be_ref[0]
    lane = jax.lax.broadcasted_iota(i32, el.shape, 1)
    lane_f = lane.astype(f32)
    emask = (lane >> 2).astype(f32) == g_idx
    em = jnp.where(emask, el, -jnp.inf)
    ee = jnp.where(emask, jnp.exp(em - jnp.max(em, axis=-1, keepdims=True)), 0.0)
    ep = ee / jnp.sum(ee, axis=-1, keepdims=True)
    p1 = jnp.max(jnp.where(emask, ep, -1.0), axis=-1, keepdims=True)
    i1 = first_lane(emask & (ep == p1), lane_f)
    rest = emask & (lane_f != i1)
    p2 = jnp.max(jnp.where(rest, ep, -1.0), axis=-1, keepdims=True)
    i2 = first_lane(rest & (ep == p2), lane_f)
    tot = p1 + p2
    w1 = g_top * (p1 / tot)
    w2 = g_top * (p2 / tot)

    lo = jnp.minimum(i1, i2) - EXPERTS_PER_GROUP * g_idx
    hi = jnp.maximum(i1, i2) - EXPERTS_PER_GROUP * g_idx
    pair = jnp.where(lo == 0.0, hi - 1.0, jnp.where(lo == 1.0, jnp.where(hi == 3.0, 3.0, 4.0), 5.0))
    cls_ref[...] = (N_PAIRS * g_idx + pair).astype(i32)
    g_lo = jnp.where(i1 < i2, w1, w2)
    g_hi = jnp.where(i1 < i2, w2, w1)
    tail_lane = jax.lax.broadcasted_iota(i32, (h2.shape[0], LANES), 1)
    hx_ref[:, HX_HALF:HX_COLS] = pltpu.bitcast(
        jnp.where(tail_lane == 0, g_lo, jnp.where(tail_lane == 1, g_hi, 0.0)), i32)


def _post(layer, o, x, row0, total, mod, mod_row, w, tile, merged=None):
    n = o.shape[0]
    lay = lambda t: (layer, 0, 0)
    row = lambda t: (t, 0)
    off = row0 // tile
    out_row = lambda t: (off + t, 0)
    in_specs = [
        pl.BlockSpec((tile, D_MODEL), row),
        pl.BlockSpec((tile, D_MODEL), row),
        pl.BlockSpec((1, 1, N_MOD, D_MODEL), lambda t: (layer, mod_row(t * tile), 0, 0)),
        pl.BlockSpec((1, D_MODEL, D_MODEL), lay),
        pl.BlockSpec((1, 1, D_MODEL), lay),
        pl.BlockSpec((1, D_MODEL, N_GROUPS), lay),
        pl.BlockSpec((1, 1, N_GROUPS), lay),
        pl.BlockSpec((1, D_MODEL, N_EXPERTS), lay),
        pl.BlockSpec((1, 1, N_EXPERTS), lay),
    ]
    args = [o, x, mod, w["w_out"], w["g2"], w["w_grp"], w["b_grp"], w["w_rtr"], w["b_rtr"]]
    aliases = {}
    if merged is not None:
        aliases = {len(args) + j: j for j in range(3)}
        in_specs += [pl.BlockSpec(memory_space=pl.ANY)] * 3
        args += list(merged)
    return pl.pallas_call(
        functools.partial(_post_kernel, merged is not None),
        out_shape=[jax.ShapeDtypeStruct((total, D_MODEL), f32), jax.ShapeDtypeStruct((total, HX_COLS), i32),
                   jax.ShapeDtypeStruct((total, 1), i32)],
        grid=(n // tile,),
        in_specs=in_specs,
        out_specs=[pl.BlockSpec((tile, D_MODEL), out_row), pl.BlockSpec((tile, HX_COLS), out_row),
                   pl.BlockSpec((tile, 1), out_row)],
        scratch_shapes=[pltpu.VMEM((D_MODEL, D_MODEL), bf16)],
        input_output_aliases=aliases,
        compiler_params=_params(("arbitrary",)),
        name="post_attention",
    )(*args)


PLAN_CHUNK = 1024
TAB_ROWS = LANES


def _plan_kernel(cls_ref, slot_ref, tab_ref, rank_scr):
    n = cls_ref.shape[0]
    lane = jax.lax.broadcasted_iota(i32, (PLAN_CHUNK, LANES), 1)
    r = jax.lax.broadcasted_iota(i32, (PLAN_CHUNK, PLAN_CHUNK), 0)
    c = jax.lax.broadcasted_iota(i32, (PLAN_CHUNK, PLAN_CHUNK), 1)
    before = (c < r).astype(bf16)

    def count(b, seen):
        rows = pl.ds(pl.multiple_of(b * PLAN_CHUNK, PLAN_CHUNK), PLAN_CHUNK)
        onehot = (cls_ref[rows, :] == lane).astype(f32)
        ahead = _dot(before, onehot.astype(bf16)) + seen
        rank_scr[rows, :] = jnp.sum(onehot * ahead, axis=-1, keepdims=True)
        return seen + jnp.sum(onehot, axis=0, keepdims=True)

    counts = jax.lax.fori_loop(0, n // PLAN_CHUNK, count, jnp.zeros((1, LANES), f32))
    tiles = jnp.floor((counts + (EXPERT_TILE - 1)) * (1.0 / EXPERT_TILE))
    rr = jax.lax.broadcasted_iota(i32, (LANES, LANES), 0)
    cc = jax.lax.broadcasted_iota(i32, (LANES, LANES), 1)
    ends = _dot(jnp.broadcast_to(tiles, (8, LANES)).astype(bf16), (rr <= cc).astype(bf16))[0:1]
    starts = ends - tiles

    def place(b, carry):
        rows = pl.ds(pl.multiple_of(b * PLAN_CHUNK, PLAN_CHUNK), PLAN_CHUNK)
        first = jnp.sum(jnp.where(cls_ref[rows, :] == lane, starts, 0.0), axis=-1, keepdims=True)
        slot_ref[rows, :] = (first * EXPERT_TILE + rank_scr[rows, :]).astype(i32)
        return carry

    jax.lax.fori_loop(0, n // PLAN_CHUNK, place, 0)

    tl = jax.lax.broadcasted_iota(i32, (TAB_ROWS, LANES), 1)
    n_tiles = jnp.sum(jnp.where(tl[0:1] == N_CLASSES - 1, ends, 0.0), axis=-1, keepdims=True)
    k = jnp.minimum(jax.lax.broadcasted_iota(i32, (TAB_ROWS, 1), 0).astype(f32), n_tiles - 1.0)
    cls_k = jnp.sum(jnp.where((tl < N_CLASSES) & (ends <= k), 1.0, 0.0), axis=-1, keepdims=True)
    cls_k = jnp.minimum(cls_k, N_CLASSES - 1.0)
    mine = tl.astype(f32) == cls_k
    used = jnp.sum(jnp.where(mine, counts, 0.0), axis=-1, keepdims=True)
    first = jnp.sum(jnp.where(mine, starts, 0.0), axis=-1, keepdims=True)
    valid = jnp.clip(used - (k - first) * EXPERT_TILE, 0.0, float(EXPERT_TILE))
    group = jnp.floor((cls_k + 0.5) * (1.0 / N_PAIRS))
    pair = cls_k - N_PAIRS * group
    lo = hi = jnp.zeros_like(pair)
    for p in range(N_PAIRS):
        lo = jnp.where(pair == p, float(PAIR_LO[p]), lo)
        hi = jnp.where(pair == p, float(PAIR_HI[p]), hi)
    e_lo = EXPERTS_PER_GROUP * group + lo
    e_hi = EXPERTS_PER_GROUP * group + hi
    tab = jnp.where(tl == 0, e_lo, jnp.where(tl == 1, e_hi, jnp.where(tl == 2, valid, jnp.where(tl == 3, n_tiles, 0.0))))
    tab_ref[...] = tab.astype(i32)


def _plan(cls, max_tiles):
    n = cls.shape[0]
    assert n % PLAN_CHUNK == 0 and max_tiles <= TAB_ROWS
    slot, tab = pl.pallas_call(
        _plan_kernel,
        out_shape=[jax.ShapeDtypeStruct((n, 1), i32), jax.ShapeDtypeStruct((TAB_ROWS, LANES), i32)],
        scratch_shapes=[pltpu.VMEM((n, 1), f32)],
        compiler_params=_params(None),
        name="dispatch_plan",
    )(cls)
    return slot.reshape(n), tab[:max_tiles, 0], tab[:max_tiles, 1], tab[:max_tiles, 2], tab[0, 3:4]


def _move_rows(src, idx, n_out, scatter):
    n = idx.shape[0]
    width = src.shape[1]
    per_worker = n // (SC_CORES * SC_SUBCORES)
    assert n % SC_ROWS == 0
    mesh = plsc.VectorSubcoreMesh(core_axis_name="c", subcore_axis_name="s")

    @functools.partial(
        pl.kernel, mesh=mesh, out_type=jax.ShapeDtypeStruct((n_out, width), src.dtype),
        scratch_types=[pltpu.VMEM((SC_CHUNK,), i32), pltpu.VMEM((SC_CHUNK, width), src.dtype),
                       pltpu.SemaphoreType.DMA])
    def move(src_hbm, idx_hbm, out_hbm, idx_v, rows_v, sem):
        wid = jax.lax.axis_index("s") * SC_CORES + jax.lax.axis_index("c")
        base = wid * per_worker

        @pl.loop(0, per_worker // SC_CHUNK)
        def _(j):
            off = base + j * SC_CHUNK
            pltpu.sync_copy(idx_hbm.at[pl.ds(off, SC_CHUNK)], idx_v)
            if scatter:
                pltpu.sync_copy(src_hbm.at[pl.ds(off, SC_CHUNK)], rows_v)
                pltpu.async_copy(rows_v, out_hbm.at[idx_v], sem).wait()
            else:
                pltpu.async_copy(src_hbm.at[idx_v], rows_v, sem).wait()
                pltpu.sync_copy(rows_v, out_hbm.at[pl.ds(off, SC_CHUNK)])

    return move(src, idx)


def _expert_kernel(lo_ref, hi_ref, valid_ref, nt_ref, xs_ref, wg_ref, wu_ref, wd_ref, ys_ref, wg_scr, wu_scr, wd_scr):
    k = pl.program_id(0)
    prev = jnp.maximum(k - 1, 0)

    @pl.when((k == 0) | (lo_ref[k] // EXPERTS_PER_GROUP != lo_ref[prev] // EXPERTS_PER_GROUP))
    def _():
        for j in range(EXPERTS_PER_GROUP):
            wg_scr[j] = wg_ref[0, 0, j].astype(bf16)
            wu_scr[j] = wu_ref[0, 0, j].astype(bf16)
            wd_scr[j] = wd_ref[0, 0, j].astype(bf16)

    @pl.when(k < nt_ref[0])
    def _():
        live = jax.lax.broadcasted_iota(i32, (EXPERT_TILE, 1), 0) < valid_ref[k]
        words = jnp.where(live, xs_ref[:, 0:HX_HALF], 0)
        x = jnp.concatenate([pltpu.bitcast(words & -65536, f32), pltpu.bitcast(words << 16, f32)], axis=1).astype(bf16)
        gates = pltpu.bitcast(jnp.where(live, xs_ref[:, HX_HALF:HX_COLS], 0), f32)
        y = None
        for lane, e_ref in enumerate((lo_ref, hi_ref)):
            j = e_ref[k] % EXPERTS_PER_GROUP
            hid = _silu(_dot(x, wg_scr[j])) * _dot(x, wu_scr[j]) * gates[:, lane:lane + 1]
            part = _dot(hid.astype(bf16), wd_scr[j])
            y = part if y is None else y + part
        ys_ref[...] = y


def _experts(layer, xs, e_lo, e_hi, valid, n_tiles, w):
    max_tiles = e_lo.shape[0]
    row = lambda k, lo, hi, valid, nt: (jnp.minimum(k, nt[0] - 1), 0)
    group = lambda k, lo, hi, valid, nt: (layer, lo[k] // EXPERTS_PER_GROUP, 0, 0, 0)
    by_group = lambda a: a.reshape(DEPTH, N_GROUPS, EXPERTS_PER_GROUP, *a.shape[2:])
    return pl.pallas_call(
        _expert_kernel,
        out_shape=jax.ShapeDtypeStruct((max_tiles * EXPERT_TILE, D_MODEL), f32),
        grid_spec=pltpu.PrefetchScalarGridSpec(
            num_scalar_prefetch=4, grid=(max_tiles,),
            in_specs=[pl.BlockSpec((EXPERT_TILE, HX_COLS), row),
                      pl.BlockSpec((1, 1, EXPERTS_PER_GROUP, D_MODEL, D_FF_EXPERT), group),
                      pl.BlockSpec((1, 1, EXPERTS_PER_GROUP, D_MODEL, D_FF_EXPERT), group),
                      pl.BlockSpec((1, 1, EXPERTS_PER_GROUP, D_FF_EXPERT, D_MODEL), group)],
            out_specs=pl.BlockSpec((EXPERT_TILE, D_MODEL), row),
            scratch_shapes=[pltpu.VMEM((EXPERTS_PER_GROUP, D_MODEL, D_FF_EXPERT), bf16),
                            pltpu.VMEM((EXPERTS_PER_GROUP, D_MODEL, D_FF_EXPERT), bf16),
                            pltpu.VMEM((EXPERTS_PER_GROUP, D_FF_EXPERT, D_MODEL), bf16)]),
        compiler_params=_params(("arbitrary",)),
        name="experts",
    )(e_lo, e_hi, valid, n_tiles, xs, by_group(w["w_gate"]), by_group(w["w_up"]), by_group(w["w_down"]))


def _final_kernel(x1_ref, y_ref, mod_ref, fg_ref, o_ref):
    o_ref[...] = _rms(x1_ref[...] + mod_ref[0, 0, 5:6, :] * y_ref[...], D_MODEL) * fg_ref[...]


def _final(x1, y, n, row0, mod, mod_row, w, tile):
    off = row0 // tile
    src_row = lambda t: (off + t, 0)
    return pl.pallas_call(
        _final_kernel,
        out_shape=jax.ShapeDtypeStruct((n, D_MODEL), f32),
        grid=(n // tile,),
        in_specs=[pl.BlockSpec((tile, D_MODEL), src_row), pl.BlockSpec((tile, D_MODEL), src_row),
                  pl.BlockSpec((1, 1, N_MOD, D_MODEL), lambda t: (DEPTH - 1, mod_row(t * tile), 0, 0)),
                  pl.BlockSpec((1, D_MODEL), lambda t: (0, 0))],
        out_specs=pl.BlockSpec((tile, D_MODEL), lambda t: (t, 0)),
        compiler_params=_params(("arbitrary",)),
        name="final_norm",
    )(x1, y, mod, w["final_g"])


def _rope_tables(n_tokens):
    pos = np.arange(n_tokens)
    row = (pos // GRID_W).astype(np.float64)
    col = (pos % GRID_W).astype(np.float64)

    def cs(rot_dim):
        quarter = rot_dim // 4
        inv = ROPE_THETA ** (-np.arange(quarter, dtype=np.float64) / quarter)
        ang = np.concatenate([row[:, None] * inv, col[:, None] * inv], axis=-1)
        return np.cos(ang), np.sin(ang)

    c32, s32 = cs(MLA_ROPE)
    c64, s64 = cs(HEAD_DIM)
    ones = np.ones((n_tokens, MLA_NOPE))
    zeros = np.zeros((n_tokens, MLA_NOPE))

    def rep(parts):
        period = np.concatenate(parts, axis=-1)
        return jnp.asarray(np.tile(period, (1, LANES // period.shape[-1])), f32)

    return (rep([ones, c32, c32]), rep([zeros, -s32, s32]), rep([c32, c32]), rep([-s32, s32]),
            rep([c64, c64]), rep([-s64, s64]))


def _layout_weights(norm1_g, norm2_g, w_in, mla_kv_norm_g, mla_w_uk, mla_w_uv, gqa_q_norm_g, gqa_k_norm_g,
                    diff_lambda, diff_norm_g, w_out, moe_w_group, moe_b_group, moe_w_router, moe_b_router,
                    moe_w_gate, moe_w_up, moe_w_down, final_norm_g):
    eye = jnp.eye(MLA_ROPE, dtype=f32)
    top = jnp.concatenate([mla_w_uk, jnp.zeros((DEPTH, KV_RANK, MLA_HEADS, MLA_ROPE), f32)], axis=-1)
    mid = jnp.concatenate([jnp.zeros((MLA_ROPE, MLA_HEADS, MLA_NOPE), f32),
                           jnp.broadcast_to(eye[:, None, :], (MLA_ROPE, MLA_HEADS, MLA_ROPE))], axis=-1)
    w_ka = jnp.concatenate([top.reshape(DEPTH, KV_RANK, 384),
                            jnp.broadcast_to(mid.reshape(1, MLA_ROPE, 384), (DEPTH, MLA_ROPE, 384)),
                            jnp.zeros((DEPTH, 256 - KV_RANK - MLA_ROPE, 384), f32)], axis=1).astype(bf16)
    seg_id = np.arange(512) // HEAD_DIM
    seg = jnp.asarray(seg_id[:, None] == seg_id[None, :], bf16)
    qk_g = jnp.concatenate([jnp.tile(gqa_q_norm_g, (1, GQA_HEADS)), jnp.tile(gqa_k_norm_g, (1, GQA_KV_HEADS))], axis=-1)
    return dict(
        g1=norm1_g.reshape(DEPTH, 1, D_MODEL), g2=norm2_g.reshape(DEPTH, 1, D_MODEL),
        w_in=jnp.swapaxes(w_in, 1, 2),
        kv_g=mla_kv_norm_g.reshape(DEPTH, 1, KV_RANK), qk_g=qk_g.reshape(DEPTH, 1, 512), seg=seg, w_ka=w_ka,
        w_uv=mla_w_uv.reshape(DEPTH, KV_RANK, 384), lam=diff_lambda, diff_g=diff_norm_g.reshape(DEPTH, 1, DIFF_V),
        w_out=w_out, w_grp=moe_w_group, b_grp=moe_b_group.reshape(DEPTH, 1, N_GROUPS), w_rtr=moe_w_router,
        b_rtr=moe_b_router.reshape(DEPTH, 1, N_EXPERTS), w_gate=moe_w_gate, w_up=moe_w_up, w_down=moe_w_down,
        final_g=final_norm_g.reshape(1, D_MODEL))


PRE_TILE = 512
CTX_ATTN_TILE = 256
LAT_ATTN_TILE = 256
CTX_HEADS_PER_ROUND = 20
LAT_HEADS_PER_ROUND = 5
POST_TILE = 512
FINAL_TILE = 1024


def kernel(x_prompt, x_sample, c, cache_mla_ckv, cache_mla_krope, cache_gqa_k, cache_gqa_v, cache_diff_k, cache_diff_v, c_ctx, norm1_g, norm2_g, w_mod, b_mod, w_in, mla_kv_norm_g, mla_w_uk, mla_w_uv, gqa_q_norm_g, gqa_k_norm_g, diff_lambda, diff_norm_g, w_out, moe_w_group, moe_b_group, moe_w_router, moe_b_router, moe_w_gate, moe_w_up, moe_w_down, final_norm_g):
    B, S, _ = x_prompt.shape
    Bl, Sl, _ = x_sample.shape
    n_ctx, n_lat = B * S, Bl * Sl
    total = n_ctx + n_lat
    assert S == CTX_ATTN_TILE and Bl + 1 <= MOD_ROWS and DEPTH == 2 and total % SC_ROWS == 0
    slot_rows = -(-(total + N_CLASSES * EXPERT_TILE) // SC_ROWS) * SC_ROWS
    max_tiles = slot_rows // EXPERT_TILE
    w = _layout_weights(norm1_g, norm2_g, w_in, mla_kv_norm_g, mla_w_uk, mla_w_uv, gqa_q_norm_g, gqa_k_norm_g,
                        diff_lambda, diff_norm_g, w_out, moe_w_group, moe_b_group, moe_w_router, moe_b_router,
                        moe_w_gate, moe_w_up, moe_w_down, final_norm_g)
    cond = jnp.concatenate([c_ctx[None, :], c, jnp.zeros((MOD_ROWS - 1 - Bl, D_MODEL), f32)], axis=0)
    mod = _modulation(cond, w_mod, b_mod).reshape(DEPTH, MOD_ROWS, N_MOD, D_MODEL)
    ctx_row = lambda token: 0
    lat_row = lambda token: 1 + token // Sl
    tabs = _rope_tables(Sl)
    kv_past = _cache_rows((cache_mla_ckv, cache_mla_krope, cache_gqa_k, cache_gqa_v, cache_diff_k, cache_diff_v), w)
    per_b = Sl // LAT_ATTN_TILE

    x_ctx, x_lat = x_prompt.reshape(n_ctx, D_MODEL), x_sample.reshape(n_lat, D_MODEL)
    cache = ()
    x1 = y = None
    for i in range(DEPTH):
        if i == 0:
            q, kv, *cache = _pre(i, x_ctx, n_ctx, 0, S, mod, ctx_row, w, PRE_TILE)
        else:
            q, kv, x_ctx, *cache = _pre(i, x1, n_ctx, 0, S, mod, ctx_row, w, PRE_TILE, prev_cache=cache, resid=y)
        own = (kv.reshape(1, 1, n_ctx, KV_COLS), S, lambda t: (0, 0, t, 0))
        o = _attention(i, q, [own], w, CTX_ATTN_TILE, CTX_HEADS_PER_ROUND)
        if i == 0:
            q_l, kv_l = _pre(i, x_lat, n_lat, 0, Sl, mod, lat_row, w, PRE_TILE, rope_tabs=tabs)
        else:
            q_l, kv_l, x_lat = _pre(i, x1, n_lat, n_ctx, Sl, mod, lat_row, w, PRE_TILE, rope_tabs=tabs, resid=y)
        past = (kv_past, PAST_LEN, lambda t, i=i: (i, t // per_b, 0, 0))
        own = (kv_l.reshape(1, Bl, Sl, KV_COLS), Sl, lambda t: (0, t // per_b, 0, 0))
        o_l = _attention(i, q_l, [past, own], w, LAT_ATTN_TILE, LAT_HEADS_PER_ROUND)
        merged = _post(i, o, x_ctx, 0, total, mod, ctx_row, w, POST_TILE)
        x1, hx, cls = _post(i, o_l, x_lat, n_ctx, total, mod, lat_row, w, POST_TILE, merged=merged)
        slot, e_lo, e_hi, valid, n_tiles = _plan(cls, max_tiles)
        xs = _move_rows(hx, slot, slot_rows, scatter=True)
        ys = _experts(i, xs, e_lo, e_hi, valid, n_tiles, w)
        y = _move_rows(ys, slot, total, scatter=False)

    y_prompt = _final(x1, y, n_ctx, 0, mod, ctx_row, w, FINAL_TILE).reshape(B, S, D_MODEL)
    y_sample = _final(x1, y, n_lat, n_ctx, mod, lat_row, w, FINAL_TILE).reshape(Bl, Sl, D_MODEL)
    new_mla_ckv, new_mla_krope = cache[0], cache[1]
    new_gqa_k = cache[2].reshape(B, DEPTH, S, GQA_KV_HEADS, HEAD_DIM)
    new_gqa_v = cache[3].reshape(B, DEPTH, S, GQA_KV_HEADS, HEAD_DIM)
    new_diff_k = cache[4].reshape(B, DEPTH, S, DIFF_HEADS, 2, DIFF_QK)
    new_diff_v = cache[5].reshape(B, DEPTH, S, DIFF_HEADS, DIFF_V)
    return (y_prompt, y_sample, new_mla_ckv, new_mla_krope, new_gqa_k, new_gqa_v, new_diff_k, new_diff_v)
```

```python
import functools
import math

import jax
import jax.numpy as jnp
import numpy as np
from jax.experimental import pallas as pl
from jax.experimental.pallas import tpu as pltpu
from jax.experimental.pallas import tpu_sc as plsc

D_MODEL = 1024
DEPTH = 2
PAST_LEN = 512
GRID_W = 64
ROPE_THETA = 10000.0
EPS = 1e-6
LOG2E = 1.4426950408889634
N_MOD = 6
HEAD_DIM = 64
MLA_HEADS = 6
MLA_NOPE = 32
MLA_ROPE = 32
MLA_V = 64
KV_RANK = 128
GQA_HEADS = 6
GQA_KV_HEADS = 2
GQA_GROUP = GQA_HEADS // GQA_KV_HEADS
DIFF_HEADS = 4
DIFF_QK = 32
DIFF_V = 64
N_GROUPS = 4
EXPERTS_PER_GROUP = 4
N_EXPERTS = N_GROUPS * EXPERTS_PER_GROUP
D_FF_EXPERT = 256

LANES = 128
MOD_ROWS = 8

IN_COLS = 1952
IN_KR = 512
Z_QA, Z_CKV, Z_QG, Z_KG, Z_VG, Z_QD, Z_KD, Z_VD, Z_KR = 0, 384, 512, 896, 1024, 1152, 1408, 1664, 1920
Z_COLS = 2048
Q_A, Q_G, Q_D, Q_COLS = 0, 384, 768, 1024
KV_KA, KV_VA, KV_KG, KV_VG, KV_KD, KV_VD, KV_COLS = 0, 384, 768, 896, 1024, 1280, 1536
CACHE_WIDTHS = (128, 32, 128, 128, 256, 256)

PAIR_LO = (0, 0, 0, 1, 1, 2)
PAIR_HI = (1, 2, 3, 3, 2, 3)
N_PAIRS = len(PAIR_LO)
N_CLASSES = N_GROUPS * N_PAIRS
HX_HALF = D_MODEL // 2
HX_COLS = HX_HALF + LANES
EXPERT_TILE = 256

SC_CORES, SC_SUBCORES = 2, 16
SC_CHUNK = 64
SC_ROWS = SC_CORES * SC_SUBCORES * SC_CHUNK

VMEM_LIMIT = 56 * 1024 * 1024

bf16 = jnp.bfloat16
f32 = jnp.float32
i32 = jnp.int32


def _dot(a, b):
    return jnp.dot(a, b, preferred_element_type=f32)


def _dot_nt(a, b):
    return jax.lax.dot_general(a, b, (((1,), (1,)), ((), ())), preferred_element_type=f32)


def _rms(x, width):
    return x * jax.lax.rsqrt(jnp.sum(x * x, axis=-1, keepdims=True) * (1.0 / width) + EPS)


def _silu(x):
    return x * (1.0 / (1.0 + jnp.exp(-x)))


def _params(sem):
    return pltpu.CompilerParams(dimension_semantics=sem, vmem_limit_bytes=VMEM_LIMIT)


def _mod_kernel(cond_ref, w_ref, b_ref, o_ref):
    o_ref[0] = _dot(_silu(cond_ref[...]).astype(bf16), w_ref[0].astype(bf16)) + b_ref[0]


def _modulation(cond, w_mod, b_mod):
    return pl.pallas_call(
        _mod_kernel,
        out_shape=jax.ShapeDtypeStruct((DEPTH, MOD_ROWS, N_MOD * D_MODEL), f32),
        grid=(DEPTH, N_MOD),
        in_specs=[
            pl.BlockSpec((MOD_ROWS, D_MODEL), lambda i, j: (0, 0)),
            pl.BlockSpec((1, D_MODEL, D_MODEL), lambda i, j: (i, 0, j)),
            pl.BlockSpec((1, 1, D_MODEL), lambda i, j: (i, 0, j)),
        ],
        out_specs=pl.BlockSpec((1, MOD_ROWS, D_MODEL), lambda i, j: (i, 0, j)),
        compiler_params=_params(("arbitrary", "arbitrary")),
        name="modulation",
    )(cond, w_mod, b_mod.reshape(DEPTH, 1, N_MOD * D_MODEL))


def _swap_halves(x, half):
    lane = jax.lax.broadcasted_iota(i32, x.shape, 1)
    fwd = pltpu.roll(x, LANES - half, 1)
    bwd = pltpu.roll(x, half, 1)
    return jnp.where((lane & (2 * half - 1)) < half, fwd, bwd)


def _rope_block(x, cos, sin, half):
    return x * cos + _swap_halves(x, half) * sin


def _pre_kernel(rope, n_prev, resid, *refs):
    it = iter(refs)
    x_ref, mod_ref, g1_ref, w_in_ref, kvg_ref, qkg_ref, seg_ref, wka_ref, wuv_ref = (next(it) for _ in range(9))
    if resid:
        y_ref, pmod_ref = next(it), next(it)
    if rope:
        ca_ref, sa_ref, c32_ref, s32_ref, c64_ref, s64_ref = (next(it) for _ in range(6))
    prev_refs = [next(it) for _ in range(n_prev)]
    q_ref, kv_ref = next(it), next(it)
    if resid:
        x2_ref = next(it)
    cache_refs = [] if rope else [next(it) for _ in range(len(CACHE_WIDTHS))]
    w_scr = next(it)

    @pl.when(pl.program_id(0) == 0)
    def _():
        w_scr[0:IN_KR] = w_in_ref[0, 0:IN_KR].astype(bf16)
        w_scr[IN_KR:Z_KR] = w_in_ref[0, IN_KR + MLA_ROPE:IN_COLS].astype(bf16)
        w_scr[Z_KR:Z_KR + MLA_ROPE] = w_in_ref[0, IN_KR:IN_KR + MLA_ROPE].astype(bf16)
        w_scr[Z_KR + MLA_ROPE:Z_COLS] = jnp.zeros((Z_COLS - Z_KR - MLA_ROPE, D_MODEL), bf16)

    x = x_ref[...]
    if resid:
        x = x + pmod_ref[0, 0, 5:6, :] * y_ref[...]
        x2_ref[...] = x
    shift1 = mod_ref[0, 0, 0:1, :]
    scale1 = mod_ref[0, 0, 1:2, :]
    h = (_rms(x, D_MODEL) * g1_ref[0]) * (1.0 + scale1) + shift1
    z = _dot_nt(h.astype(bf16), w_scr[...])

    ckv = _rms(z[:, Z_CKV:Z_CKV + KV_RANK], KV_RANK) * kvg_ref[0]

    qk = z[:, Z_QG:Z_VG]
    sq = qk * qk
    sq_hi = sq.astype(bf16)
    sq_lo = (sq - sq_hi.astype(f32)).astype(bf16)
    seg = seg_ref[...]
    ms = (_dot(sq_hi, seg) + _dot(sq_lo, seg)) * (1.0 / HEAD_DIM)
    qk = qk * jax.lax.rsqrt(ms + EPS) * qkg_ref[0]

    def blocks(arr, n):
        return [arr[:, LANES * j:LANES * (j + 1)] for j in range(n)]

    qa = blocks(z[:, Z_QA:Z_QA + 384], 3)
    qkb = blocks(qk, 4)
    qd = blocks(z[:, Z_QD:Z_QD + 256], 2)
    kd = blocks(z[:, Z_KD:Z_KD + 256], 2)
    kr = z[:, Z_KR:Z_KR + LANES]
    if rope:
        ca, sa, c32, s32, c64, s64 = (r[...] for r in (ca_ref, sa_ref, c32_ref, s32_ref, c64_ref, s64_ref))
        qa = [_rope_block(b, ca, sa, MLA_ROPE // 2) for b in qa]
        qkb = [_rope_block(b, c64, s64, HEAD_DIM // 2) for b in qkb]
        qd = [_rope_block(b, c32, s32, DIFF_QK // 2) for b in qd]
        kd = [_rope_block(b, c32, s32, DIFF_QK // 2) for b in kd]
        kr = _rope_block(kr, c32, s32, MLA_ROPE // 2)

    vg = z[:, Z_VG:Z_VG + 128]
    vd = z[:, Z_VD:Z_VD + 256]
    ckv_b = ckv.astype(bf16)
    k_a = _dot(jnp.concatenate([ckv_b, kr.astype(bf16)], axis=1), wka_ref[0])
    v_a = _dot(ckv_b, wuv_ref[0].astype(bf16))

    for j in range(3):
        q_ref[:, Q_A + LANES * j:Q_A + LANES * (j + 1)] = (qa[j] * (HEAD_DIM ** -0.5 * LOG2E)).astype(bf16)
        q_ref[:, Q_G + LANES * j:Q_G + LANES * (j + 1)] = (qkb[j] * (HEAD_DIM ** -0.5 * LOG2E)).astype(bf16)
    for j in range(2):
        q_ref[:, Q_D + LANES * j:Q_D + LANES * (j + 1)] = (qd[j] * (DIFF_QK ** -0.5 * LOG2E)).astype(bf16)
        kv_ref[:, KV_KD + LANES * j:KV_KD + LANES * (j + 1)] = kd[j].astype(bf16)
    kv_ref[:, KV_KA:KV_KA + 384] = k_a.astype(bf16)
    kv_ref[:, KV_VA:KV_VA + 384] = v_a.astype(bf16)
    kv_ref[:, KV_KG:KV_KG + 128] = qkb[3].astype(bf16)
    kv_ref[:, KV_VG:KV_VG + 128] = vg.astype(bf16)
    kv_ref[:, KV_VD:KV_VD + 256] = vd.astype(bf16)
    if not rope:
        rows = [ckv, kr[:, :MLA_ROPE], qkb[3], vg, jnp.concatenate(kd, axis=1), vd]
        for out, new in zip(cache_refs, rows):
            reqs, _, seq, width = out.shape
            out[:, 0] = new.reshape(reqs, seq, width)


def _pre(layer, x, n, row0, seq, mod, mod_row, w, tile, rope_tabs=None, prev_cache=(), resid=None):
    rope = rope_tabs is not None
    lay = lambda t: (layer, 0, 0)
    row = lambda t: (t, 0)
    off = row0 // tile
    src_row = lambda t: (off + t, 0)
    in_specs = [
        pl.BlockSpec((tile, D_MODEL), src_row),
        pl.BlockSpec((1, 1, N_MOD, D_MODEL), lambda t: (layer, mod_row(t * tile), 0, 0)),
        pl.BlockSpec((1, 1, D_MODEL), lay),
        pl.BlockSpec((1, IN_COLS, D_MODEL), lay),
        pl.BlockSpec((1, 1, KV_RANK), lay),
        pl.BlockSpec((1, 1, 512), lay),
        pl.BlockSpec((512, 512), lambda t: (0, 0)),
        pl.BlockSpec((1, 256, 384), lay),
        pl.BlockSpec((1, KV_RANK, 384), lay),
    ]
    args = [x, mod, w["g1"], w["w_in"], w["kv_g"], w["qk_g"], w["seg"], w["w_ka"], w["w_uv"]]
    if resid is not None:
        in_specs += [pl.BlockSpec((tile, D_MODEL), src_row),
                     pl.BlockSpec((1, 1, N_MOD, D_MODEL), lambda t: (layer - 1, mod_row(t * tile), 0, 0))]
        args += [resid, mod]
    if rope:
        per_b = seq // tile
        in_specs += [pl.BlockSpec((tile, LANES), lambda t: (t % per_b, 0))] * 6
        args += list(rope_tabs)
    out_shape = [jax.ShapeDtypeStruct((n, Q_COLS), bf16), jax.ShapeDtypeStruct((n, KV_COLS), bf16)]
    out_specs = [pl.BlockSpec((tile, Q_COLS), row), pl.BlockSpec((tile, KV_COLS), row)]
    if resid is not None:
        out_shape.append(jax.ShapeDtypeStruct((n, D_MODEL), f32))
        out_specs.append(pl.BlockSpec((tile, D_MODEL), row))
    aliases = {}
    if not rope:
        reqs = tile // seq
        for j, width in enumerate(CACHE_WIDTHS):
            if prev_cache:
                aliases[len(args)] = len(out_shape)
                in_specs.append(pl.BlockSpec(memory_space=pl.ANY))
                args.append(prev_cache[j])
            out_shape.append(jax.ShapeDtypeStruct((n // seq, DEPTH, seq, width), f32))
            out_specs.append(pl.BlockSpec((reqs, 1, seq, width), lambda t: (t, layer, 0, 0)))
    return pl.pallas_call(
        functools.partial(_pre_kernel, rope, len(prev_cache), resid is not None),
        out_shape=out_shape,
        grid=(n // tile,),
        in_specs=in_specs,
        out_specs=out_specs,
        input_output_aliases=aliases,
        scratch_shapes=[pltpu.VMEM((Z_COLS, D_MODEL), bf16)],
        compiler_params=_params(("arbitrary",)),
        name="pre_latent" if rope else "pre_context",
    )(*args)


def _cache_kernel(ckv_ref, kr_ref, kg_ref, vg_ref, kd_ref, vd_ref, wka_ref, wuv_ref, kv_ref):
    ckv_b = ckv_ref[0, 0].astype(bf16)
    wka = wka_ref[0]
    k_a = _dot(ckv_b, wka[:KV_RANK]) + _dot(kr_ref[0, 0].astype(bf16), wka[KV_RANK:KV_RANK + MLA_ROPE])
    kv_ref[0, 0, :, KV_KA:KV_KA + 384] = k_a.astype(bf16)
    kv_ref[0, 0, :, KV_VA:KV_VA + 384] = _dot(ckv_b, wuv_ref[0].astype(bf16)).astype(bf16)
    kv_ref[0, 0, :, KV_KG:KV_KG + 128] = kg_ref[0, 0].astype(bf16)
    kv_ref[0, 0, :, KV_VG:KV_VG + 128] = vg_ref[0, 0].astype(bf16)
    kv_ref[0, 0, :, KV_KD:KV_KD + 256] = kd_ref[0, 0].astype(bf16)
    kv_ref[0, 0, :, KV_VD:KV_VD + 256] = vd_ref[0, 0].astype(bf16)


def _cache_rows(caches, w):
    B = caches[0].shape[0]
    spec = lambda width: pl.BlockSpec((1, 1, PAST_LEN, width), lambda i, b: (b, i, 0, 0))
    return pl.pallas_call(
        _cache_kernel,
        out_shape=jax.ShapeDtypeStruct((DEPTH, B, PAST_LEN, KV_COLS), bf16),
        grid=(DEPTH, B),
        in_specs=[spec(width) for width in CACHE_WIDTHS]
        + [pl.BlockSpec((1, 256, 384), lambda i, b: (i, 0, 0)), pl.BlockSpec((1, KV_RANK, 384), lambda i, b: (i, 0, 0))],
        out_specs=pl.BlockSpec((1, 1, PAST_LEN, KV_COLS), lambda i, b: (i, b, 0, 0)),
        compiler_params=_params(("arbitrary", "arbitrary")),
        name="cache_rows",
    )(*[c.reshape(B, DEPTH, PAST_LEN, width) for c, width in zip(caches, CACHE_WIDTHS)], w["w_ka"], w["w_uv"])


_SCORE_HEADS = (
    [(Q_A + 64 * h, KV_KA + 64 * h, 64, KV_VA + MLA_V * h) for h in range(MLA_HEADS)]
    + [(Q_G + 64 * h, KV_KG + 64 * (h // GQA_GROUP), 64, KV_VG + 64 * (h // GQA_GROUP)) for h in range(GQA_HEADS)]
    + [(Q_D + 64 * h + DIFF_QK * c, KV_KD + 64 * h + DIFF_QK * c, DIFF_QK, KV_VD + DIFF_V * h)
       for h in range(DIFF_HEADS) for c in range(2)])


def _attn_kernel(lam_init, per_round, n_src, q_ref, *refs):
    kv_refs = refs[:n_src]
    lam_ref, dg_ref, o_ref, s_ref, p_ref = refs[n_src:]
    spans, start = [], 0
    for r in kv_refs:
        spans.append((r, start, r.shape[2]))
        start += r.shape[2]

    outs = []
    for first in range(0, len(_SCORE_HEADS), per_round):
        chunk = _SCORE_HEADS[first:first + per_round]
        for j, (q_off, k_off, width, _) in enumerate(chunk):
            for r, lo, size in spans:
                s_ref[j, :, lo:lo + size] = _dot_nt(q_ref[:, q_off:q_off + width], r[0, 0, :, k_off:k_off + width])
        s = s_ref[...]
        p = jnp.exp2(s - jnp.max(s, axis=-1, keepdims=True))
        inv = 1.0 / jnp.sum(p, axis=-1, keepdims=True)
        p_ref[...] = p.astype(bf16)
        for j, (_, _, _, v_off) in enumerate(chunk):
            o = sum(_dot(p_ref[j, :, lo:lo + size], r[0, 0, :, v_off:v_off + DIFF_V]) for r, lo, size in spans)
            outs.append(o * inv[j])

    lp = lam_ref[0]
    e1 = jnp.exp(jnp.sum(lp[0:1] * lp[1:2], axis=-1, keepdims=True))
    e2 = jnp.exp(jnp.sum(lp[2:3] * lp[3:4], axis=-1, keepdims=True))
    lam = e1 - e2 + lam_init
    heads = outs[:MLA_HEADS + GQA_HEADS]
    for h in range(DIFF_HEADS):
        o1, o2 = outs[MLA_HEADS + GQA_HEADS + 2 * h:MLA_HEADS + GQA_HEADS + 2 * h + 2]
        heads.append(_rms(o1 - lam * o2, DIFF_V) * dg_ref[0] * (1.0 - lam_init))
    for j in range(len(heads) // 2):
        o_ref[:, LANES * j:LANES * (j + 1)] = jnp.concatenate(heads[2 * j:2 * j + 2], axis=1).astype(bf16)


def _attention(layer, q, sources, w, tile, per_round):
    n = q.shape[0]
    lam_init = 0.8 - 0.6 * math.exp(-0.3 * layer)
    s_kv = sum(rows for _, rows, _ in sources)
    assert len(_SCORE_HEADS) % per_round == 0
    return pl.pallas_call(
        functools.partial(_attn_kernel, lam_init, per_round, len(sources)),
        out_shape=jax.ShapeDtypeStruct((n, D_MODEL), bf16),
        scratch_shapes=[pltpu.VMEM((per_round, tile, s_kv), f32), pltpu.VMEM((per_round, tile, s_kv), bf16)],
        grid=(n // tile,),
        in_specs=[pl.BlockSpec((tile, Q_COLS), lambda t: (t, 0))]
        + [pl.BlockSpec((1, 1, rows, KV_COLS), index) for _, rows, index in sources]
        + [pl.BlockSpec((1, 4, DIFF_QK), lambda t: (layer, 0, 0)), pl.BlockSpec((1, 1, DIFF_V), lambda t: (layer, 0, 0))],
        out_specs=pl.BlockSpec((tile, D_MODEL), lambda t: (t, 0)),
        compiler_params=_params(("arbitrary",)),
        name="attention",
    )(q, *[arr for arr, _, _ in sources], w["lam"], w["diff_g"])


def _post_kernel(merge, *refs):
    it = iter(refs)
    o_ref, x_ref, mod_ref, w_out_ref, g2_ref, wg_ref, bg_ref, we_ref, be_ref = (next(it) for _ in range(9))
    if merge:
        next(it), next(it), next(it)
    x1_ref, hx_ref, cls_ref, w_scr = (next(it) for _ in range(4))

    @pl.when(pl.program_id(0) == 0)
    def _():
        w_scr[...] = w_out_ref[0].astype(bf16)

    gate1 = mod_ref[0, 0, 2:3, :]
    shift2 = mod_ref[0, 0, 3:4, :]
    scale2 = mod_ref[0, 0, 4:5, :]
    x1 = x_ref[...] + gate1 * _dot(o_ref[...], w_scr[...])
    x1_ref[...] = x1
    h2 = ((_rms(x1, D_MODEL) * g2_ref[0]) * (1.0 + scale2) + shift2).astype(bf16)
    bits = pltpu.bitcast(h2.astype(f32), i32)
    hx_ref[:, 0:HX_HALF] = bits[:, 0:HX_HALF] | jax.lax.shift_right_logical(bits[:, HX_HALF:D_MODEL], 16)

    def first_lane(mask, lane_f):
        return jnp.min(jnp.where(mask, lane_f, float(LANES)), axis=-1, keepdims=True)

    gl = _dot(h2, wg_ref[0].astype(bf16)) + bg_ref[0]
    glane = jax.lax.broadcasted_iota(i32, gl.shape, 1).astype(f32)
    ge = jnp.exp(gl - jnp.max(gl, axis=-1, keepdims=True))
    gprob = ge / jnp.sum(ge, axis=-1, keepdims=True)
    g_top = jnp.max(gprob, axis=-1, keepdims=True)
    g_idx = first_lane(gprob == g_top, glane)

    el = _dot(h2, we_ref[0].astype(bf16)) + be_ref[0]
    lane = jax.lax.broadcasted_iota(i32, el.shape, 1)
    lane_f = lane.astype(f32)
    emask = (lane >> 2).astype(f32) == g_idx
    em = jnp.where(emask, el, -jnp.inf)
    ee = jnp.where(emask, jnp.exp(em - jnp.max(em, axis=-1, keepdims=True)), 0.0)
    ep = ee / jnp.sum(ee, axis=-1, keepdims=True)
    p1 = jnp.max(jnp.where(emask, ep, -1.0), axis=-1, keepdims=True)
    i1 = first_lane(emask & (ep == p1), lane_f)
    rest = emask & (lane_f != i1)
    p2 = jnp.max(jnp.where(rest, ep, -1.0), axis=-1, keepdims=True)
    i2 = first_lane(rest & (ep == p2), lane_f)
    tot = p1 + p2
    w1 = g_top * (p1 / tot)
    w2 = g_top * (p2 / tot)

    lo = jnp.minimum(i1, i2) - EXPERTS_PER_GROUP * g_idx
    hi = jnp.maximum(i1, i2) - EXPERTS_PER_GROUP * g_idx
    pair = jnp.where(lo == 0.0, hi - 1.0, jnp.where(lo == 1.0, jnp.where(hi == 3.0, 3.0, 4.0), 5.0))
    cls_ref[...] = (N_PAIRS * g_idx + pair).astype(i32)
    g_lo = jnp.where(i1 < i2, w1, w2)
    g_hi = jnp.where(i1 < i2, w2, w1)
    tail_lane = jax.lax.broadcasted_iota(i32, (h2.shape[0], LANES), 1)
    hx_ref[:, HX_HALF:HX_COLS] = pltpu.bitcast(
        jnp.where(tail_lane == 0, g_lo, jnp.where(tail_lane == 1, g_hi, 0.0)), i32)


def _post(layer, o, x, row0, total, mod, mod_row, w, tile, merged=None):
    n = o.shape[0]
    lay = lambda t: (layer, 0, 0)
    row = lambda t: (t, 0)
    off = row0 // tile
    out_row = lambda t: (off + t, 0)
    in_specs = [
        pl.BlockSpec((tile, D_MODEL), row),
        pl.BlockSpec((tile, D_MODEL), row),
        pl.BlockSpec((1, 1, N_MOD, D_MODEL), lambda t: (layer, mod_row(t * tile), 0, 0)),
        pl.BlockSpec((1, D_MODEL, D_MODEL), lay),
        pl.BlockSpec((1, 1, D_MODEL), lay),
        pl.BlockSpec((1, D_MODEL, N_GROUPS), lay),
        pl.BlockSpec((1, 1, N_GROUPS), lay),
        pl.BlockSpec((1, D_MODEL, N_EXPERTS), lay),
        pl.BlockSpec((1, 1, N_EXPERTS), lay),
    ]
    args = [o, x, mod, w["w_out"], w["g2"], w["w_grp"], w["b_grp"], w["w_rtr"], w["b_rtr"]]
    aliases = {}
    if merged is not None:
        aliases = {len(args) + j: j for j in range(3)}
        in_specs += [pl.BlockSpec(memory_space=pl.ANY)] * 3
        args += list(merged)
    return pl.pallas_call(
        functools.partial(_post_kernel, merged is not None),
        out_shape=[jax.ShapeDtypeStruct((total, D_MODEL), f32), jax.ShapeDtypeStruct((total, HX_COLS), i32),
                   jax.ShapeDtypeStruct((total, 1), i32)],
        grid=(n // tile,),
        in_specs=in_specs,
        out_specs=[pl.BlockSpec((tile, D_MODEL), out_row), pl.BlockSpec((tile, HX_COLS), out_row),
                   pl.BlockSpec((tile, 1), out_row)],
        scratch_shapes=[pltpu.VMEM((D_MODEL, D_MODEL), bf16)],
        input_output_aliases=aliases,
        compiler_params=_params(("arbitrary",)),
        name="post_attention",
    )(*args)


def _ctx_kernel(lam_init, n_prev, resid, *refs):
    it = iter(refs)
    pre_in = [next(it) for _ in range(9 + (2 if resid else 0) + n_prev)]
    lam_ref, dg_ref = next(it), next(it)
    post_w = [next(it) for _ in range(6)]
    cache_refs = [next(it) for _ in range(len(CACHE_WIDTHS))]
    x1_ref, hx_ref, cls_ref = (next(it) for _ in range(3))
    w_in_scr, q_scr, kv_scr, o_scr, s_scr, p_scr, w_out_scr, x2_scr = (next(it) for _ in range(8))
    x_ref, mod_ref = pre_in[0], pre_in[1]
    x2_ref = [x2_scr] if resid else []

    _pre_kernel(False, n_prev, resid, *pre_in, q_scr, kv_scr.at[0, 0], *x2_ref, *cache_refs, w_in_scr)
    _attn_kernel(lam_init, len(_SCORE_HEADS), 1, q_scr, kv_scr, lam_ref, dg_ref, o_scr, s_scr, p_scr)
    _post_kernel(False, o_scr, x2_scr if resid else x_ref, mod_ref, *post_w, x1_ref, hx_ref, cls_ref, w_out_scr)


def _ctx_layer(layer, x, n, seq, total, mod, w, prev_cache=(), resid=None):
    lay = lambda t: (layer, 0, 0)
    row = lambda t: (t, 0)
    mod_spec = lambda l: pl.BlockSpec((1, 1, N_MOD, D_MODEL), lambda t: (l, 0, 0, 0))
    in_specs = [
        pl.BlockSpec((seq, D_MODEL), row), mod_spec(layer),
        pl.BlockSpec((1, 1, D_MODEL), lay), pl.BlockSpec((1, IN_COLS, D_MODEL), lay),
        pl.BlockSpec((1, 1, KV_RANK), lay), pl.BlockSpec((1, 1, 512), lay), pl.BlockSpec((512, 512), lambda t: (0, 0)),
        pl.BlockSpec((1, 256, 384), lay), pl.BlockSpec((1, KV_RANK, 384), lay),
    ]
    args = [x, mod, w["g1"], w["w_in"], w["kv_g"], w["qk_g"], w["seg"], w["w_ka"], w["w_uv"]]
    if resid is not None:
        in_specs += [pl.BlockSpec((seq, D_MODEL), row), mod_spec(layer - 1)]
        args += [resid, mod]
    out_shape, out_specs, aliases = [], [], {}
    for j, width in enumerate(CACHE_WIDTHS):
        if prev_cache:
            aliases[len(args)] = len(out_shape)
            in_specs.append(pl.BlockSpec(memory_space=pl.ANY))
            args.append(prev_cache[j])
        out_shape.append(jax.ShapeDtypeStruct((n // seq, DEPTH, seq, width), f32))
        out_specs.append(pl.BlockSpec((1, 1, seq, width), lambda t: (t, layer, 0, 0)))
    in_specs += [
        pl.BlockSpec((1, 4, DIFF_QK), lay), pl.BlockSpec((1, 1, DIFF_V), lay),
        pl.BlockSpec((1, D_MODEL, D_MODEL), lay), pl.BlockSpec((1, 1, D_MODEL), lay),
        pl.BlockSpec((1, D_MODEL, N_GROUPS), lay), pl.BlockSpec((1, 1, N_GROUPS), lay),
        pl.BlockSpec((1, D_MODEL, N_EXPERTS), lay), pl.BlockSpec((1, 1, N_EXPERTS), lay),
    ]
    args += [w["lam"], w["diff_g"], w["w_out"], w["g2"], w["w_grp"], w["b_grp"], w["w_rtr"], w["b_rtr"]]
    out_shape += [jax.ShapeDtypeStruct((total, D_MODEL), f32), jax.ShapeDtypeStruct((total, HX_COLS), i32),
                  jax.ShapeDtypeStruct((total, 1), i32)]
    out_specs += [pl.BlockSpec((seq, D_MODEL), row), pl.BlockSpec((seq, HX_COLS), row), pl.BlockSpec((seq, 1), row)]
    heads = len(_SCORE_HEADS)
    return pl.pallas_call(
        functools.partial(_ctx_kernel, 0.8 - 0.6 * math.exp(-0.3 * layer), len(prev_cache), resid is not None),
        out_shape=out_shape,
        grid=(n // seq,),
        in_specs=in_specs,
        out_specs=out_specs,
        scratch_shapes=[pltpu.VMEM((Z_COLS, D_MODEL), bf16), pltpu.VMEM((seq, Q_COLS), bf16),
                        pltpu.VMEM((1, 1, seq, KV_COLS), bf16), pltpu.VMEM((seq, D_MODEL), bf16),
                        pltpu.VMEM((heads, seq, seq), f32), pltpu.VMEM((heads, seq, seq), bf16),
                        pltpu.VMEM((D_MODEL, D_MODEL), bf16), pltpu.VMEM((seq, D_MODEL), f32)],
        input_output_aliases=aliases,
        compiler_params=_params(("arbitrary",)),
        name="context_layer",
    )(*args)


PLAN_CHUNK = 1024
TAB_ROWS = LANES


def _plan_kernel(cls_ref, slot_ref, tab_ref, rank_scr):
    n = cls_ref.shape[0]
    lane = jax.lax.broadcasted_iota(i32, (PLAN_CHUNK, LANES), 1)
    r = jax.lax.broadcasted_iota(i32, (PLAN_CHUNK, PLAN_CHUNK), 0)
    c = jax.lax.broadcasted_iota(i32, (PLAN_CHUNK, PLAN_CHUNK), 1)
    before = (c < r).astype(bf16)

    def count(b, seen):
        rows = pl.ds(pl.multiple_of(b * PLAN_CHUNK, PLAN_CHUNK), PLAN_CHUNK)
        onehot = (cls_ref[rows, :] == lane).astype(f32)
        ahead = _dot(before, onehot.astype(bf16)) + seen
        rank_scr[rows, :] = jnp.sum(onehot * ahead, axis=-1, keepdims=True)
        return seen + jnp.sum(onehot, axis=0, keepdims=True)

    counts = jax.lax.fori_loop(0, n // PLAN_CHUNK, count, jnp.zeros((1, LANES), f32))
    tiles = jnp.floor((counts + (EXPERT_TILE - 1)) * (1.0 / EXPERT_TILE))
    rr = jax.lax.broadcasted_iota(i32, (LANES, LANES), 0)
    cc = jax.lax.broadcasted_iota(i32, (LANES, LANES), 1)
    ends = _dot(jnp.broadcast_to(tiles, (8, LANES)).astype(bf16), (rr <= cc).astype(bf16))[0:1]
    starts = ends - tiles

    def place(b, carry):
        rows = pl.ds(pl.multiple_of(b * PLAN_CHUNK, PLAN_CHUNK), PLAN_CHUNK)
        first = jnp.sum(jnp.where(cls_ref[rows, :] == lane, starts, 0.0), axis=-1, keepdims=True)
        slot_ref[rows, :] = (first * EXPERT_TILE + rank_scr[rows, :]).astype(i32)
        return carry

    jax.lax.fori_loop(0, n // PLAN_CHUNK, place, 0)

    tl = jax.lax.broadcasted_iota(i32, (TAB_ROWS, LANES), 1)
    n_tiles = jnp.sum(jnp.where(tl[0:1] == N_CLASSES - 1, ends, 0.0), axis=-1, keepdims=True)
    k = jnp.minimum(jax.lax.broadcasted_iota(i32, (TAB_ROWS, 1), 0).astype(f32), n_tiles - 1.0)
    cls_k = jnp.sum(jnp.where((tl < N_CLASSES) & (ends <= k), 1.0, 0.0), axis=-1, keepdims=True)
    cls_k = jnp.minimum(cls_k, N_CLASSES - 1.0)
    mine = tl.astype(f32) == cls_k
    used = jnp.sum(jnp.where(mine, counts, 0.0), axis=-1, keepdims=True)
    first = jnp.sum(jnp.where(mine, starts, 0.0), axis=-1, keepdims=True)
    valid = jnp.clip(used - (k - first) * EXPERT_TILE, 0.0, float(EXPERT_TILE))
    group = jnp.floor((cls_k + 0.5) * (1.0 / N_PAIRS))
    pair = cls_k - N_PAIRS * group
    lo = hi = jnp.zeros_like(pair)
    for p in range(N_PAIRS):
        lo = jnp.where(pair == p, float(PAIR_LO[p]), lo)
        hi = jnp.where(pair == p, float(PAIR_HI[p]), hi)
    e_lo = EXPERTS_PER_GROUP * group + lo
    e_hi = EXPERTS_PER_GROUP * group + hi
    tab = jnp.where(tl == 0, e_lo, jnp.where(tl == 1, e_hi, jnp.where(tl == 2, valid, jnp.where(tl == 3, n_tiles, 0.0))))
    tab_ref[...] = tab.astype(i32)


def _plan(cls, max_tiles):
    n = cls.shape[0]
    assert n % PLAN_CHUNK == 0 and max_tiles <= TAB_ROWS
    slot, tab = pl.pallas_call(
        _plan_kernel,
        out_shape=[jax.ShapeDtypeStruct((n, 1), i32), jax.ShapeDtypeStruct((TAB_ROWS, LANES), i32)],
        scratch_shapes=[pltpu.VMEM((n, 1), f32)],
        compiler_params=_params(None),
        name="dispatch_plan",
    )(cls)
    return slot.reshape(n), tab[:max_tiles, 0], tab[:max_tiles, 1], tab[:max_tiles, 2], tab[0, 3:4]


def _move_rows(src, idx, n_out, scatter):
    n = idx.shape[0]
    width = src.shape[1]
    per_worker = n // (SC_CORES * SC_SUBCORES)
    assert n % SC_ROWS == 0
    mesh = plsc.VectorSubcoreMesh(core_axis_name="c", subcore_axis_name="s")

    @functools.partial(
        pl.kernel, mesh=mesh, out_type=jax.ShapeDtypeStruct((n_out, width), src.dtype),
        scratch_types=[pltpu.VMEM((SC_CHUNK,), i32), pltpu.VMEM((SC_CHUNK, width), src.dtype),
                       pltpu.SemaphoreType.DMA])
    def move(src_hbm, idx_hbm, out_hbm, idx_v, rows_v, sem):
        wid = jax.lax.axis_index("s") * SC_CORES + jax.lax.axis_index("c")
        base = wid * per_worker

        @pl.loop(0, per_worker // SC_CHUNK)
        def _(j):
            off = base + j * SC_CHUNK
            pltpu.sync_copy(idx_hbm.at[pl.ds(off, SC_CHUNK)], idx_v)
            if scatter:
                pltpu.sync_copy(src_hbm.at[pl.ds(off, SC_CHUNK)], rows_v)
                pltpu.async_copy(rows_v, out_hbm.at[idx_v], sem).wait()
            else:
                pltpu.async_copy(src_hbm.at[idx_v], rows_v, sem).wait()
                pltpu.sync_copy(rows_v, out_hbm.at[pl.ds(off, SC_CHUNK)])

    return move(src, idx)


def _expert_kernel(lo_ref, hi_ref, valid_ref, nt_ref, xs_ref, wg_ref, wu_ref, wd_ref, ys_ref, wg_scr, wu_scr, wd_scr):
    k = pl.program_id(0)
    prev = jnp.maximum(k - 1, 0)

    @pl.when((k == 0) | (lo_ref[k] // EXPERTS_PER_GROUP != lo_ref[prev] // EXPERTS_PER_GROUP))
    def _():
        for j in range(EXPERTS_PER_GROUP):
            wg_scr[j] = wg_ref[0, 0, j].astype(bf16)
            wu_scr[j] = wu_ref[0, 0, j].astype(bf16)
            wd_scr[j] = wd_ref[0, 0, j].astype(bf16)

    @pl.when(k < nt_ref[0])
    def _():
        live = jax.lax.broadcasted_iota(i32, (EXPERT_TILE, 1), 0) < valid_ref[k]
        words = jnp.where(live, xs_ref[:, 0:HX_HALF], 0)
        x = jnp.concatenate([pltpu.bitcast(words & -65536, f32), pltpu.bitcast(words << 16, f32)], axis=1).astype(bf16)
        gates = pltpu.bitcast(jnp.where(live, xs_ref[:, HX_HALF:HX_COLS], 0), f32)
        y = None
        for lane, e_ref in enumerate((lo_ref, hi_ref)):
            j = e_ref[k] % EXPERTS_PER_GROUP
            hid = _silu(_dot(x, wg_scr[j])) * _dot(x, wu_scr[j]) * gates[:, lane:lane + 1]
            part = _dot(hid.astype(bf16), wd_scr[j])
            y = part if y is None else y + part
        ys_ref[...] = y


def _experts(layer, xs, e_lo, e_hi, valid, n_tiles, w):
    max_tiles = e_lo.shape[0]
    row = lambda k, lo, hi, valid, nt: (jnp.minimum(k, nt[0] - 1), 0)
    group = lambda k, lo, hi, valid, nt: (layer, lo[k] // EXPERTS_PER_GROUP, 0, 0, 0)
    by_group = lambda a: a.reshape(DEPTH, N_GROUPS, EXPERTS_PER_GROUP, *a.shape[2:])
    return pl.pallas_call(
        _expert_kernel,
        out_shape=jax.ShapeDtypeStruct((max_tiles * EXPERT_TILE, D_MODEL), f32),
        grid_spec=pltpu.PrefetchScalarGridSpec(
            num_scalar_prefetch=4, grid=(max_tiles,),
            in_specs=[pl.BlockSpec((EXPERT_TILE, HX_COLS), row),
                      pl.BlockSpec((1, 1, EXPERTS_PER_GROUP, D_MODEL, D_FF_EXPERT), group),
                      pl.BlockSpec((1, 1, EXPERTS_PER_GROUP, D_MODEL, D_FF_EXPERT), group),
                      pl.BlockSpec((1, 1, EXPERTS_PER_GROUP, D_FF_EXPERT, D_MODEL), group)],
            out_specs=pl.BlockSpec((EXPERT_TILE, D_MODEL), row),
            scratch_shapes=[pltpu.VMEM((EXPERTS_PER_GROUP, D_MODEL, D_FF_EXPERT), bf16),
                            pltpu.VMEM((EXPERTS_PER_GROUP, D_MODEL, D_FF_EXPERT), bf16),
                            pltpu.VMEM((EXPERTS_PER_GROUP, D_FF_EXPERT, D_MODEL), bf16)]),
        compiler_params=_params(("arbitrary",)),
        name="experts",
    )(e_lo, e_hi, valid, n_tiles, xs, by_group(w["w_gate"]), by_group(w["w_up"]), by_group(w["w_down"]))


def _final_kernel(x1_ref, y_ref, mod_ref, fg_ref, o_ref):
    o_ref[...] = _rms(x1_ref[...] + mod_ref[0, 0, 5:6, :] * y_ref[...], D_MODEL) * fg_ref[...]


def _final(x1, y, n, row0, mod, mod_row, w, tile):
    off = row0 // tile
    src_row = lambda t: (off + t, 0)
    return pl.pallas_call(
        _final_kernel,
        out_shape=jax.ShapeDtypeStruct((n, D_MODEL), f32),
        grid=(n // tile,),
        in_specs=[pl.BlockSpec((tile, D_MODEL), src_row), pl.BlockSpec((tile, D_MODEL), src_row),
                  pl.BlockSpec((1, 1, N_MOD, D_MODEL), lambda t: (DEPTH - 1, mod_row(t * tile), 0, 0)),
                  pl.BlockSpec((1, D_MODEL), lambda t: (0, 0))],
        out_specs=pl.BlockSpec((tile, D_MODEL), lambda t: (t, 0)),
        compiler_params=_params(("arbitrary",)),
        name="final_norm",
    )(x1, y, mod, w["final_g"])


def _rope_tables(n_tokens):
    pos = np.arange(n_tokens)
    row = (pos // GRID_W).astype(np.float64)
    col = (pos % GRID_W).astype(np.float64)

    def cs(rot_dim):
        quarter = rot_dim // 4
        inv = ROPE_THETA ** (-np.arange(quarter, dtype=np.float64) / quarter)
        ang = np.concatenate([row[:, None] * inv, col[:, None] * inv], axis=-1)
        return np.cos(ang), np.sin(ang)

    c32, s32 = cs(MLA_ROPE)
    c64, s64 = cs(HEAD_DIM)
    ones = np.ones((n_tokens, MLA_NOPE))
    zeros = np.zeros((n_tokens, MLA_NOPE))

    def rep(parts):
        period = np.concatenate(parts, axis=-1)
        return jnp.asarray(np.tile(period, (1, LANES // period.shape[-1])), f32)

    return (rep([ones, c32, c32]), rep([zeros, -s32, s32]), rep([c32, c32]), rep([-s32, s32]),
            rep([c64, c64]), rep([-s64, s64]))


def _layout_weights(norm1_g, norm2_g, w_in, mla_kv_norm_g, mla_w_uk, mla_w_uv, gqa_q_norm_g, gqa_k_norm_g,
                    diff_lambda, diff_norm_g, w_out, moe_w_group, moe_b_group, moe_w_router, moe_b_router,
                    moe_w_gate, moe_w_up, moe_w_down, final_norm_g):
    eye = jnp.eye(MLA_ROPE, dtype=f32)
    top = jnp.concatenate([mla_w_uk, jnp.zeros((DEPTH, KV_RANK, MLA_HEADS, MLA_ROPE), f32)], axis=-1)
    mid = jnp.concatenate([jnp.zeros((MLA_ROPE, MLA_HEADS, MLA_NOPE), f32),
                           jnp.broadcast_to(eye[:, None, :], (MLA_ROPE, MLA_HEADS, MLA_ROPE))], axis=-1)
    w_ka = jnp.concatenate([top.reshape(DEPTH, KV_RANK, 384),
                            jnp.broadcast_to(mid.reshape(1, MLA_ROPE, 384), (DEPTH, MLA_ROPE, 384)),
                            jnp.zeros((DEPTH, 256 - KV_RANK - MLA_ROPE, 384), f32)], axis=1).astype(bf16)
    seg_id = np.arange(512) // HEAD_DIM
    seg = jnp.asarray(seg_id[:, None] == seg_id[None, :], bf16)
    qk_g = jnp.concatenate([jnp.tile(gqa_q_norm_g, (1, GQA_HEADS)), jnp.tile(gqa_k_norm_g, (1, GQA_KV_HEADS))], axis=-1)
    return dict(
        g1=norm1_g.reshape(DEPTH, 1, D_MODEL), g2=norm2_g.reshape(DEPTH, 1, D_MODEL),
        w_in=jnp.swapaxes(w_in, 1, 2),
        kv_g=mla_kv_norm_g.reshape(DEPTH, 1, KV_RANK), qk_g=qk_g.reshape(DEPTH, 1, 512), seg=seg, w_ka=w_ka,
        w_uv=mla_w_uv.reshape(DEPTH, KV_RANK, 384), lam=diff_lambda, diff_g=diff_norm_g.reshape(DEPTH, 1, DIFF_V),
        w_out=w_out, w_grp=moe_w_group, b_grp=moe_b_group.reshape(DEPTH, 1, N_GROUPS), w_rtr=moe_w_router,
        b_rtr=moe_b_router.reshape(DEPTH, 1, N_EXPERTS), w_gate=moe_w_gate, w_up=moe_w_up, w_down=moe_w_down,
        final_g=final_norm_g.reshape(1, D_MODEL))


PRE_TILE = 512
CTX_ATTN_TILE = 256
LAT_ATTN_TILE = 256
CTX_HEADS_PER_ROUND = 20
LAT_HEADS_PER_ROUND = 5
POST_TILE = 512
FINAL_TILE = 1024


def kernel(x_prompt, x_sample, c, cache_mla_ckv, cache_mla_krope, cache_gqa_k, cache_gqa_v, cache_diff_k, cache_diff_v, c_ctx, norm1_g, norm2_g, w_mod, b_mod, w_in, mla_kv_norm_g, mla_w_uk, mla_w_uv, gqa_q_norm_g, gqa_k_norm_g, diff_lambda, diff_norm_g, w_out, moe_w_group, moe_b_group, moe_w_router, moe_b_router, moe_w_gate, moe_w_up, moe_w_down, final_norm_g):
    B, S, _ = x_prompt.shape
    Bl, Sl, _ = x_sample.shape
    n_ctx, n_lat = B * S, Bl * Sl
    total = n_ctx + n_lat
    assert S == CTX_ATTN_TILE and Bl + 1 <= MOD_ROWS and DEPTH == 2 and total % SC_ROWS == 0
    slot_rows = -(-(total + N_CLASSES * EXPERT_TILE) // SC_ROWS) * SC_ROWS
    max_tiles = slot_rows // EXPERT_TILE
    w = _layout_weights(norm1_g, norm2_g, w_in, mla_kv_norm_g, mla_w_uk, mla_w_uv, gqa_q_norm_g, gqa_k_norm_g,
                        diff_lambda, diff_norm_g, w_out, moe_w_group, moe_b_group, moe_w_router, moe_b_router,
                        moe_w_gate, moe_w_up, moe_w_down, final_norm_g)
    cond = jnp.concatenate([c_ctx[None, :], c, jnp.zeros((MOD_ROWS - 1 - Bl, D_MODEL), f32)], axis=0)
    mod = _modulation(cond, w_mod, b_mod).reshape(DEPTH, MOD_ROWS, N_MOD, D_MODEL)
    ctx_row = lambda token: 0
    lat_row = lambda token: 1 + token // Sl
    tabs = _rope_tables(Sl)
    kv_past = _cache_rows((cache_mla_ckv, cache_mla_krope, cache_gqa_k, cache_gqa_v, cache_diff_k, cache_diff_v), w)
    per_b = Sl // LAT_ATTN_TILE

    x_ctx, x_lat = x_prompt.reshape(n_ctx, D_MODEL), x_sample.reshape(n_lat, D_MODEL)
    cache = ()
    x1 = y = None
    for i in range(DEPTH):
        if i == 0:
            *cache, x1_c, hx_c, cls_c = _ctx_layer(i, x_ctx, n_ctx, S, total, mod, w)
        else:
            *cache, x1_c, hx_c, cls_c = _ctx_layer(i, x1, n_ctx, S, total, mod, w, prev_cache=cache, resid=y)
        if i == 0:
            q_l, kv_l = _pre(i, x_lat, n_lat, 0, Sl, mod, lat_row, w, PRE_TILE, rope_tabs=tabs)
        else:
            q_l, kv_l, x_lat = _pre(i, x1, n_lat, n_ctx, Sl, mod, lat_row, w, PRE_TILE, rope_tabs=tabs, resid=y)
        past = (kv_past, PAST_LEN, lambda t, i=i: (i, t // per_b, 0, 0))
        own = (kv_l.reshape(1, Bl, Sl, KV_COLS), Sl, lambda t: (0, t // per_b, 0, 0))
        o_l = _attention(i, q_l, [past, own], w, LAT_ATTN_TILE, LAT_HEADS_PER_ROUND)
        x1, hx, cls = _post(i, o_l, x_lat, n_ctx, total, mod, lat_row, w, POST_TILE, merged=(x1_c, hx_c, cls_c))
        slot, e_lo, e_hi, valid, n_tiles = _plan(cls, max_tiles)
        xs = _move_rows(hx, slot, slot_rows, scatter=True)
        ys = _experts(i, xs, e_lo, e_hi, valid, n_tiles, w)
        y = _move_rows(ys, slot, total, scatter=False)

    y_prompt = _final(x1, y, n_ctx, 0, mod, ctx_row, w, FINAL_TILE).reshape(B, S, D_MODEL)
    y_sample = _final(x1, y, n_lat, n_ctx, mod, lat_row, w, FINAL_TILE).reshape(Bl, Sl, D_MODEL)
    new_mla_ckv, new_mla_krope = cache[0], cache[1]
    new_gqa_k = cache[2].reshape(B, DEPTH, S, GQA_KV_HEADS, HEAD_DIM)
    new_gqa_v = cache[3].reshape(B, DEPTH, S, GQA_KV_HEADS, HEAD_DIM)
    new_diff_k = cache[4].reshape(B, DEPTH, S, DIFF_HEADS, 2, DIFF_QK)
    new_diff_v = cache[5].reshape(B, DEPTH, S, DIFF_HEADS, DIFF_V)
    return (y_prompt, y_sample, new_mla_ckv, new_mla_krope, new_gqa_k, new_gqa_v, new_diff_k, new_diff_v)
```

```python
import functools
import math

import jax
import jax.numpy as jnp
import numpy as np
from jax.experimental import pallas as pl
from jax.experimental.pallas import tpu as pltpu
from jax.experimental.pallas import tpu_sc as plsc

D_MODEL = 1024
DEPTH = 2
PAST_LEN = 512
GRID_W = 64
ROPE_THETA = 10000.0
EPS = 1e-6
LOG2E = 1.4426950408889634
N_MOD = 6
HEAD_DIM = 64
MLA_HEADS = 6
MLA_NOPE = 32
MLA_ROPE = 32
MLA_V = 64
KV_RANK = 128
GQA_HEADS = 6
GQA_KV_HEADS = 2
GQA_GROUP = GQA_HEADS // GQA_KV_HEADS
DIFF_HEADS = 4
DIFF_QK = 32
DIFF_V = 64
N_GROUPS = 4
EXPERTS_PER_GROUP = 4
N_EXPERTS = N_GROUPS * EXPERTS_PER_GROUP
D_FF_EXPERT = 256

LANES = 128
MOD_ROWS = 8

IN_COLS = 1952
IN_KR = 512
Z_QA, Z_CKV, Z_QG, Z_KG, Z_VG, Z_QD, Z_KD, Z_VD, Z_KR = 0, 384, 512, 896, 1024, 1152, 1408, 1664, 1920
Z_COLS = 2048
Q_A, Q_G, Q_D, Q_COLS = 0, 384, 768, 1024
KV_KA, KV_VA, KV_KG, KV_VG, KV_KD, KV_VD, KV_COLS = 0, 384, 768, 896, 1024, 1280, 1536
CACHE_WIDTHS = (128, 32, 128, 128, 256, 256)

PAIR_LO = (0, 0, 0, 1, 1, 2)
PAIR_HI = (1, 2, 3, 3, 2, 3)
N_PAIRS = len(PAIR_LO)
N_CLASSES = N_GROUPS * N_PAIRS
HX_HALF = D_MODEL // 2
HX_COLS = HX_HALF + LANES
EXPERT_TILE = 256

SC_CORES, SC_SUBCORES = 2, 16
SC_BUFFER_BYTES = 400 * 1024
SC_ROWS = SC_CORES * SC_SUBCORES * 8

VMEM_LIMIT = 56 * 1024 * 1024

bf16 = jnp.bfloat16
f32 = jnp.float32
i32 = jnp.int32


def _dot(a, b):
    return jnp.dot(a, b, preferred_element_type=f32)


def _dot_nt(a, b):
    return jax.lax.dot_general(a, b, (((1,), (1,)), ((), ())), preferred_element_type=f32)


def _rms(x, width):
    return x * jax.lax.rsqrt(jnp.sum(x * x, axis=-1, keepdims=True) * (1.0 / width) + EPS)


def _silu(x):
    return x * (1.0 / (1.0 + jnp.exp(-x)))


def _params(sem):
    return pltpu.CompilerParams(dimension_semantics=sem, vmem_limit_bytes=VMEM_LIMIT)


def _mod_kernel(cond_ref, w_ref, b_ref, o_ref):
    o_ref[0] = _dot(_silu(cond_ref[...]).astype(bf16), w_ref[0].astype(bf16)) + b_ref[0]


def _modulation(cond, w_mod, b_mod):
    return pl.pallas_call(
        _mod_kernel,
        out_shape=jax.ShapeDtypeStruct((DEPTH, MOD_ROWS, N_MOD * D_MODEL), f32),
        grid=(DEPTH, N_MOD),
        in_specs=[
            pl.BlockSpec((MOD_ROWS, D_MODEL), lambda i, j: (0, 0)),
            pl.BlockSpec((1, D_MODEL, D_MODEL), lambda i, j: (i, 0, j)),
            pl.BlockSpec((1, 1, D_MODEL), lambda i, j: (i, 0, j)),
        ],
        out_specs=pl.BlockSpec((1, MOD_ROWS, D_MODEL), lambda i, j: (i, 0, j)),
        compiler_params=_params(("arbitrary", "arbitrary")),
        name="modulation",
    )(cond, w_mod, b_mod.reshape(DEPTH, 1, N_MOD * D_MODEL))


def _swap_halves(x, half):
    lane = jax.lax.broadcasted_iota(i32, x.shape, 1)
    fwd = pltpu.roll(x, LANES - half, 1)
    bwd = pltpu.roll(x, half, 1)
    return jnp.where((lane & (2 * half - 1)) < half, fwd, bwd)


def _rope_block(x, cos, sin, half):
    return x * cos + _swap_halves(x, half) * sin


def _pre_kernel(rope, n_prev, resid, *refs):
    it = iter(refs)
    x_ref, mod_ref, g1_ref, w_in_ref, kvg_ref, qkg_ref, seg_ref, wka_ref, wuv_ref = (next(it) for _ in range(9))
    if resid:
        y_ref, pmod_ref = next(it), next(it)
    if rope:
        ca_ref, sa_ref, c32_ref, s32_ref, c64_ref, s64_ref = (next(it) for _ in range(6))
    prev_refs = [next(it) for _ in range(n_prev)]
    q_ref, kv_ref = next(it), next(it)
    if resid:
        x2_ref = next(it)
    cache_refs = [] if rope else [next(it) for _ in range(len(CACHE_WIDTHS))]
    w_scr = next(it)

    @pl.when(pl.program_id(0) == 0)
    def _():
        w_scr[0:IN_KR] = w_in_ref[0, 0:IN_KR].astype(bf16)
        w_scr[IN_KR:Z_KR] = w_in_ref[0, IN_KR + MLA_ROPE:IN_COLS].astype(bf16)
        w_scr[Z_KR:Z_KR + MLA_ROPE] = w_in_ref[0, IN_KR:IN_KR + MLA_ROPE].astype(bf16)
        w_scr[Z_KR + MLA_ROPE:Z_COLS] = jnp.zeros((Z_COLS - Z_KR - MLA_ROPE, D_MODEL), bf16)

    x = x_ref[...]
    if resid:
        x = x + pmod_ref[0, 0, 5:6, :] * y_ref[...]
        x2_ref[...] = x
    shift1 = mod_ref[0, 0, 0:1, :]
    scale1 = mod_ref[0, 0, 1:2, :]
    h = (_rms(x, D_MODEL) * g1_ref[0]) * (1.0 + scale1) + shift1
    z = _dot_nt(h.astype(bf16), w_scr[...])

    ckv = _rms(z[:, Z_CKV:Z_CKV + KV_RANK], KV_RANK) * kvg_ref[0]

    qk = z[:, Z_QG:Z_VG]
    sq = qk * qk
    sq_hi = sq.astype(bf16)
    sq_lo = (sq - sq_hi.astype(f32)).astype(bf16)
    seg = seg_ref[...]
    ms = (_dot(sq_hi, seg) + _dot(sq_lo, seg)) * (1.0 / HEAD_DIM)
    qk = qk * jax.lax.rsqrt(ms + EPS) * qkg_ref[0]

    def blocks(arr, n):
        return [arr[:, LANES * j:LANES * (j + 1)] for j in range(n)]

    qa = blocks(z[:, Z_QA:Z_QA + 384], 3)
    qkb = blocks(qk, 4)
    qd = blocks(z[:, Z_QD:Z_QD + 256], 2)
    kd = blocks(z[:, Z_KD:Z_KD + 256], 2)
    kr = z[:, Z_KR:Z_KR + LANES]
    if rope:
        ca, sa, c32, s32, c64, s64 = (r[...] for r in (ca_ref, sa_ref, c32_ref, s32_ref, c64_ref, s64_ref))
        qa = [_rope_block(b, ca, sa, MLA_ROPE // 2) for b in qa]
        qkb = [_rope_block(b, c64, s64, HEAD_DIM // 2) for b in qkb]
        qd = [_rope_block(b, c32, s32, DIFF_QK // 2) for b in qd]
        kd = [_rope_block(b, c32, s32, DIFF_QK // 2) for b in kd]
        kr = _rope_block(kr, c32, s32, MLA_ROPE // 2)

    vg = z[:, Z_VG:Z_VG + 128]
    vd = z[:, Z_VD:Z_VD + 256]
    ckv_b = ckv.astype(bf16)
    k_a = _dot(jnp.concatenate([ckv_b, kr.astype(bf16)], axis=1), wka_ref[0])
    v_a = _dot(ckv_b, wuv_ref[0].astype(bf16))

    for j in range(3):
        q_ref[:, Q_A + LANES * j:Q_A + LANES * (j + 1)] = (qa[j] * (HEAD_DIM ** -0.5 * LOG2E)).astype(bf16)
        q_ref[:, Q_G + LANES * j:Q_G + LANES * (j + 1)] = (qkb[j] * (HEAD_DIM ** -0.5 * LOG2E)).astype(bf16)
    for j in range(2):
        q_ref[:, Q_D + LANES * j:Q_D + LANES * (j + 1)] = (qd[j] * (DIFF_QK ** -0.5 * LOG2E)).astype(bf16)
        kv_ref[:, KV_KD + LANES * j:KV_KD + LANES * (j + 1)] = kd[j].astype(bf16)
    kv_ref[:, KV_KA:KV_KA + 384] = k_a.astype(bf16)
    kv_ref[:, KV_VA:KV_VA + 384] = v_a.astype(bf16)
    kv_ref[:, KV_KG:KV_KG + 128] = qkb[3].astype(bf16)
    kv_ref[:, KV_VG:KV_VG + 128] = vg.astype(bf16)
    kv_ref[:, KV_VD:KV_VD + 256] = vd.astype(bf16)
    if not rope:
        rows = [ckv, kr[:, :MLA_ROPE], qkb[3], vg, jnp.concatenate(kd, axis=1), vd]
        for out, new in zip(cache_refs, rows):
            reqs, _, seq, width = out.shape
            out[:, 0] = new.reshape(reqs, seq, width)


def _pre(layer, x, n, row0, seq, mod, mod_row, w, tile, rope_tabs=None, prev_cache=(), resid=None):
    rope = rope_tabs is not None
    lay = lambda t: (layer, 0, 0)
    row = lambda t: (t, 0)
    off = row0 // tile
    src_row = lambda t: (off + t, 0)
    in_specs = [
        pl.BlockSpec((tile, D_MODEL), src_row),
        pl.BlockSpec((1, 1, N_MOD, D_MODEL), lambda t: (layer, mod_row(t * tile), 0, 0)),
        pl.BlockSpec((1, 1, D_MODEL), lay),
        pl.BlockSpec((1, IN_COLS, D_MODEL), lay),
        pl.BlockSpec((1, 1, KV_RANK), lay),
        pl.BlockSpec((1, 1, 512), lay),
        pl.BlockSpec((512, 512), lambda t: (0, 0)),
        pl.BlockSpec((1, 256, 384), lay),
        pl.BlockSpec((1, KV_RANK, 384), lay),
    ]
    args = [x, mod, w["g1"], w["w_in"], w["kv_g"], w["qk_g"], w["seg"], w["w_ka"], w["w_uv"]]
    if resid is not None:
        in_specs += [pl.BlockSpec((tile, D_MODEL), src_row),
                     pl.BlockSpec((1, 1, N_MOD, D_MODEL), lambda t: (layer - 1, mod_row(t * tile), 0, 0))]
        args += [resid, mod]
    if rope:
        per_b = seq // tile
        in_specs += [pl.BlockSpec((tile, LANES), lambda t: (t % per_b, 0))] * 6
        args += list(rope_tabs)
    out_shape = [jax.ShapeDtypeStruct((n, Q_COLS), bf16), jax.ShapeDtypeStruct((n, KV_COLS), bf16)]
    out_specs = [pl.BlockSpec((tile, Q_COLS), row), pl.BlockSpec((tile, KV_COLS), row)]
    if resid is not None:
        out_shape.append(jax.ShapeDtypeStruct((n, D_MODEL), f32))
        out_specs.append(pl.BlockSpec((tile, D_MODEL), row))
    aliases = {}
    if not rope:
        reqs = tile // seq
        for j, width in enumerate(CACHE_WIDTHS):
            if prev_cache:
                aliases[len(args)] = len(out_shape)
                in_specs.append(pl.BlockSpec(memory_space=pl.ANY))
                args.append(prev_cache[j])
            out_shape.append(jax.ShapeDtypeStruct((n // seq, DEPTH, seq, width), f32))
            out_specs.append(pl.BlockSpec((reqs, 1, seq, width), lambda t: (t, layer, 0, 0)))
    return pl.pallas_call(
        functools.partial(_pre_kernel, rope, len(prev_cache), resid is not None),
        out_shape=out_shape,
        grid=(n // tile,),
        in_specs=in_specs,
        out_specs=out_specs,
        input_output_aliases=aliases,
        scratch_shapes=[pltpu.VMEM((Z_COLS, D_MODEL), bf16)],
        compiler_params=_params(("arbitrary",)),
        name="pre_latent" if rope else "pre_context",
    )(*args)


def _cache_kernel(ckv_ref, kr_ref, kg_ref, vg_ref, kd_ref, vd_ref, wka_ref, wuv_ref, kv_ref):
    ckv_b = ckv_ref[0, 0].astype(bf16)
    wka = wka_ref[0]
    k_a = _dot(ckv_b, wka[:KV_RANK]) + _dot(kr_ref[0, 0].astype(bf16), wka[KV_RANK:KV_RANK + MLA_ROPE])
    kv_ref[0, 0, :, KV_KA:KV_KA + 384] = k_a.astype(bf16)
    kv_ref[0, 0, :, KV_VA:KV_VA + 384] = _dot(ckv_b, wuv_ref[0].astype(bf16)).astype(bf16)
    kv_ref[0, 0, :, KV_KG:KV_KG + 128] = kg_ref[0, 0].astype(bf16)
    kv_ref[0, 0, :, KV_VG:KV_VG + 128] = vg_ref[0, 0].astype(bf16)
    kv_ref[0, 0, :, KV_KD:KV_KD + 256] = kd_ref[0, 0].astype(bf16)
    kv_ref[0, 0, :, KV_VD:KV_VD + 256] = vd_ref[0, 0].astype(bf16)


def _cache_rows(caches, w):
    B = caches[0].shape[0]
    spec = lambda width: pl.BlockSpec((1, 1, PAST_LEN, width), lambda i, b: (b, i, 0, 0))
    return pl.pallas_call(
        _cache_kernel,
        out_shape=jax.ShapeDtypeStruct((DEPTH, B, PAST_LEN, KV_COLS), bf16),
        grid=(DEPTH, B),
        in_specs=[spec(width) for width in CACHE_WIDTHS]
        + [pl.BlockSpec((1, 256, 384), lambda i, b: (i, 0, 0)), pl.BlockSpec((1, KV_RANK, 384), lambda i, b: (i, 0, 0))],
        out_specs=pl.BlockSpec((1, 1, PAST_LEN, KV_COLS), lambda i, b: (i, b, 0, 0)),
        compiler_params=_params(("arbitrary", "arbitrary")),
        name="cache_rows",
    )(*[c.reshape(B, DEPTH, PAST_LEN, width) for c, width in zip(caches, CACHE_WIDTHS)], w["w_ka"], w["w_uv"])


_SCORE_HEADS = (
    [(Q_A + 64 * h, KV_KA + 64 * h, 64, KV_VA + MLA_V * h) for h in range(MLA_HEADS)]
    + [(Q_G + 64 * h, KV_KG + 64 * (h // GQA_GROUP), 64, KV_VG + 64 * (h // GQA_GROUP)) for h in range(GQA_HEADS)]
    + [(Q_D + 64 * h + DIFF_QK * c, KV_KD + 64 * h + DIFF_QK * c, DIFF_QK, KV_VD + DIFF_V * h)
       for h in range(DIFF_HEADS) for c in range(2)])


def _attn_kernel(lam_init, per_round, n_src, q_ref, *refs):
    kv_refs = refs[:n_src]
    lam_ref, dg_ref, o_ref, s_ref, p_ref = refs[n_src:]
    spans, start = [], 0
    for r in kv_refs:
        spans.append((r, start, r.shape[2]))
        start += r.shape[2]

    outs = []
    for first in range(0, len(_SCORE_HEADS), per_round):
        chunk = _SCORE_HEADS[first:first + per_round]
        for j, (q_off, k_off, width, _) in enumerate(chunk):
            for r, lo, size in spans:
                s_ref[j, :, lo:lo + size] = _dot_nt(q_ref[:, q_off:q_off + width], r[0, 0, :, k_off:k_off + width])
        s = s_ref[...]
        p = jnp.exp2(s - jnp.max(s, axis=-1, keepdims=True))
        inv = 1.0 / jnp.sum(p, axis=-1, keepdims=True)
        p_ref[...] = p.astype(bf16)
        for j, (_, _, _, v_off) in enumerate(chunk):
            o = sum(_dot(p_ref[j, :, lo:lo + size], r[0, 0, :, v_off:v_off + DIFF_V]) for r, lo, size in spans)
            outs.append(o * inv[j])

    lp = lam_ref[0]
    e1 = jnp.exp(jnp.sum(lp[0:1] * lp[1:2], axis=-1, keepdims=True))
    e2 = jnp.exp(jnp.sum(lp[2:3] * lp[3:4], axis=-1, keepdims=True))
    lam = e1 - e2 + lam_init
    heads = outs[:MLA_HEADS + GQA_HEADS]
    for h in range(DIFF_HEADS):
        o1, o2 = outs[MLA_HEADS + GQA_HEADS + 2 * h:MLA_HEADS + GQA_HEADS + 2 * h + 2]
        heads.append(_rms(o1 - lam * o2, DIFF_V) * dg_ref[0] * (1.0 - lam_init))
    for j in range(len(heads) // 2):
        o_ref[:, LANES * j:LANES * (j + 1)] = jnp.concatenate(heads[2 * j:2 * j + 2], axis=1).astype(bf16)


def _attention(layer, q, sources, w, tile, per_round):
    n = q.shape[0]
    lam_init = 0.8 - 0.6 * math.exp(-0.3 * layer)
    s_kv = sum(rows for _, rows, _ in sources)
    assert len(_SCORE_HEADS) % per_round == 0
    return pl.pallas_call(
        functools.partial(_attn_kernel, lam_init, per_round, len(sources)),
        out_shape=jax.ShapeDtypeStruct((n, D_MODEL), bf16),
        scratch_shapes=[pltpu.VMEM((per_round, tile, s_kv), f32), pltpu.VMEM((per_round, tile, s_kv), bf16)],
        grid=(n // tile,),
        in_specs=[pl.BlockSpec((tile, Q_COLS), lambda t: (t, 0))]
        + [pl.BlockSpec((1, 1, rows, KV_COLS), index) for _, rows, index in sources]
        + [pl.BlockSpec((1, 4, DIFF_QK), lambda t: (layer, 0, 0)), pl.BlockSpec((1, 1, DIFF_V), lambda t: (layer, 0, 0))],
        out_specs=pl.BlockSpec((tile, D_MODEL), lambda t: (t, 0)),
        compiler_params=_params(("arbitrary",)),
        name="attention",
    )(q, *[arr for arr, _, _ in sources], w["lam"], w["diff_g"])


def _post_kernel(merge, *refs):
    it = iter(refs)
    o_ref, x_ref, mod_ref, w_out_ref, g2_ref, wg_ref, bg_ref, we_ref, be_ref = (next(it) for _ in range(9))
    if merge:
        next(it), next(it), next(it)
    x1_ref, hx_ref, cls_ref, w_scr = (next(it) for _ in range(4))

    @pl.when(pl.program_id(0) == 0)
    def _():
        w_scr[...] = w_out_ref[0].astype(bf16)

    gate1 = mod_ref[0, 0, 2:3, :]
    shift2 = mod_ref[0, 0, 3:4, :]
    scale2 = mod_ref[0, 0, 4:5, :]
    x1 = x_ref[...] + gate1 * _dot(o_ref[...], w_scr[...])
    x1_ref[...] = x1
    h2 = ((_rms(x1, D_MODEL) * g2_ref[0]) * (1.0 + scale2) + shift2).astype(bf16)
    bits = pltpu.bitcast(h2.astype(f32), i32)
    hx_ref[:, 0:HX_HALF] = bits[:, 0:HX_HALF] | jax.lax.shift_right_logical(bits[:, HX_HALF:D_MODEL], 16)

    def first_lane(mask, lane_f):
        return jnp.min(jnp.where(mask, lane_f, float(LANES)), axis=-1, keepdims=True)

    gl = _dot(h2, wg_ref[0].astype(bf16)) + bg_ref[0]
    glane = jax.lax.broadcasted_iota(i32, gl.shape, 1).astype(f32)
    ge = jnp.exp(gl - jnp.max(gl, axis=-1, keepdims=True))
    gprob = ge / jnp.sum(ge, axis=-1, keepdims=True)
    g_top = jnp.max(gprob, axis=-1, keepdims=True)
    g_idx = first_lane(gprob == g_top, glane)

    el = _dot(h2, we_ref[0].astype(bf16)) + be_ref[0]
    lane = jax.lax.broadcasted_iota(i32, el.shape, 1)
    lane_f = lane.astype(f32)
    emask = (lane >> 2).astype(f32) == g_idx
    em = jnp.where(emask, el, -jnp.inf)
    ee = jnp.where(emask, jnp.exp(em - jnp.max(em, axis=-1, keepdims=True)), 0.0)
    ep = ee / jnp.sum(ee, axis=-1, keepdims=True)
    p1 = jnp.max(jnp.where(emask, ep, -1.0), axis=-1, keepdims=True)
    i1 = first_lane(emask & (ep == p1), lane_f)
    rest = emask & (lane_f != i1)
    p2 = jnp.max(jnp.where(rest, ep, -1.0), axis=-1, keepdims=True)
    i2 = first_lane(rest & (ep == p2), lane_f)
    tot = p1 + p2
    w1 = g_top * (p1 / tot)
    w2 = g_top * (p2 / tot)

    lo = jnp.minimum(i1, i2) - EXPERTS_PER_GROUP * g_idx
    hi = jnp.maximum(i1, i2) - EXPERTS_PER_GROUP * g_idx
    pair = jnp.where(lo == 0.0, hi - 1.0, jnp.where(lo == 1.0, jnp.where(hi == 3.0, 3.0, 4.0), 5.0))
    cls_ref[...] = (N_PAIRS * g_idx + pair).astype(i32)
    g_lo = jnp.where(i1 < i2, w1, w2)
    g_hi = jnp.where(i1 < i2, w2, w1)
    tail_lane = jax.lax.broadcasted_iota(i32, (h2.shape[0], LANES), 1)
    hx_ref[:, HX_HALF:HX_COLS] = pltpu.bitcast(
        jnp.where(tail_lane == 0, g_lo, jnp.where(tail_lane == 1, g_hi, 0.0)), i32)


def _post(layer, o, x, row0, total, mod, mod_row, w, tile, merged=None):
    n = o.shape[0]
    lay = lambda t: (layer, 0, 0)
    row = lambda t: (t, 0)
    off = row0 // tile
    out_row = lambda t: (off + t, 0)
    in_specs = [
        pl.BlockSpec((tile, D_MODEL), row),
        pl.BlockSpec((tile, D_MODEL), row),
        pl.BlockSpec((1, 1, N_MOD, D_MODEL), lambda t: (layer, mod_row(t * tile), 0, 0)),
        pl.BlockSpec((1, D_MODEL, D_MODEL), lay),
        pl.BlockSpec((1, 1, D_MODEL), lay),
        pl.BlockSpec((1, D_MODEL, N_GROUPS), lay),
        pl.BlockSpec((1, 1, N_GROUPS), lay),
        pl.BlockSpec((1, D_MODEL, N_EXPERTS), lay),
        pl.BlockSpec((1, 1, N_EXPERTS), lay),
    ]
    args = [o, x, mod, w["w_out"], w["g2"], w["w_grp"], w["b_grp"], w["w_rtr"], w["b_rtr"]]
    aliases = {}
    if merged is not None:
        aliases = {len(args) + j: j for j in range(3)}
        in_specs += [pl.BlockSpec(memory_space=pl.ANY)] * 3
        args += list(merged)
    return pl.pallas_call(
        functools.partial(_post_kernel, merged is not None),
        out_shape=[jax.ShapeDtypeStruct((total, D_MODEL), f32), jax.ShapeDtypeStruct((total, HX_COLS), i32),
                   jax.ShapeDtypeStruct((total, 1), i32)],
        grid=(n // tile,),
        in_specs=in_specs,
        out_specs=[pl.BlockSpec((tile, D_MODEL), out_row), pl.BlockSpec((tile, HX_COLS), out_row),
                   pl.BlockSpec((tile, 1), out_row)],
        scratch_shapes=[pltpu.VMEM((D_MODEL, D_MODEL), bf16)],
        input_output_aliases=aliases,
        compiler_params=_params(("arbitrary",)),
        name="post_attention",
    )(*args)


def _ctx_kernel(lam_init, n_prev, resid, *refs):
    it = iter(refs)
    pre_in = [next(it) for _ in range(9 + (2 if resid else 0) + n_prev)]
    lam_ref, dg_ref = next(it), next(it)
    post_w = [next(it) for _ in range(6)]
    cache_refs = [next(it) for _ in range(len(CACHE_WIDTHS))]
    x1_ref, hx_ref, cls_ref = (next(it) for _ in range(3))
    w_in_scr, q_scr, kv_scr, o_scr, s_scr, p_scr, w_out_scr, x2_scr = (next(it) for _ in range(8))
    x_ref, mod_ref = pre_in[0], pre_in[1]
    x2_ref = [x2_scr] if resid else []

    _pre_kernel(False, n_prev, resid, *pre_in, q_scr, kv_scr.at[0, 0], *x2_ref, *cache_refs, w_in_scr)
    _attn_kernel(lam_init, len(_SCORE_HEADS), 1, q_scr, kv_scr, lam_ref, dg_ref, o_scr, s_scr, p_scr)
    _post_kernel(False, o_scr, x2_scr if resid else x_ref, mod_ref, *post_w, x1_ref, hx_ref, cls_ref, w_out_scr)


def _ctx_layer(layer, x, n, seq, total, mod, w, prev_cache=(), resid=None):
    lay = lambda t: (layer, 0, 0)
    row = lambda t: (t, 0)
    mod_spec = lambda l: pl.BlockSpec((1, 1, N_MOD, D_MODEL), lambda t: (l, 0, 0, 0))
    in_specs = [
        pl.BlockSpec((seq, D_MODEL), row), mod_spec(layer),
        pl.BlockSpec((1, 1, D_MODEL), lay), pl.BlockSpec((1, IN_COLS, D_MODEL), lay),
        pl.BlockSpec((1, 1, KV_RANK), lay), pl.BlockSpec((1, 1, 512), lay), pl.BlockSpec((512, 512), lambda t: (0, 0)),
        pl.BlockSpec((1, 256, 384), lay), pl.BlockSpec((1, KV_RANK, 384), lay),
    ]
    args = [x, mod, w["g1"], w["w_in"], w["kv_g"], w["qk_g"], w["seg"], w["w_ka"], w["w_uv"]]
    if resid is not None:
        in_specs += [pl.BlockSpec((seq, D_MODEL), row), mod_spec(layer - 1)]
        args += [resid, mod]
    out_shape, out_specs, aliases = [], [], {}
    for j, width in enumerate(CACHE_WIDTHS):
        if prev_cache:
            aliases[len(args)] = len(out_shape)
            in_specs.append(pl.BlockSpec(memory_space=pl.ANY))
            args.append(prev_cache[j])
        out_shape.append(jax.ShapeDtypeStruct((n // seq, DEPTH, seq, width), f32))
        out_specs.append(pl.BlockSpec((1, 1, seq, width), lambda t: (t, layer, 0, 0)))
    in_specs += [
        pl.BlockSpec((1, 4, DIFF_QK), lay), pl.BlockSpec((1, 1, DIFF_V), lay),
        pl.BlockSpec((1, D_MODEL, D_MODEL), lay), pl.BlockSpec((1, 1, D_MODEL), lay),
        pl.BlockSpec((1, D_MODEL, N_GROUPS), lay), pl.BlockSpec((1, 1, N_GROUPS), lay),
        pl.BlockSpec((1, D_MODEL, N_EXPERTS), lay), pl.BlockSpec((1, 1, N_EXPERTS), lay),
    ]
    args += [w["lam"], w["diff_g"], w["w_out"], w["g2"], w["w_grp"], w["b_grp"], w["w_rtr"], w["b_rtr"]]
    out_shape += [jax.ShapeDtypeStruct((total, D_MODEL), f32), jax.ShapeDtypeStruct((total, HX_COLS), i32),
                  jax.ShapeDtypeStruct((total, 1), i32)]
    out_specs += [pl.BlockSpec((seq, D_MODEL), row), pl.BlockSpec((seq, HX_COLS), row), pl.BlockSpec((seq, 1), row)]
    heads = len(_SCORE_HEADS)
    return pl.pallas_call(
        functools.partial(_ctx_kernel, 0.8 - 0.6 * math.exp(-0.3 * layer), len(prev_cache), resid is not None),
        out_shape=out_shape,
        grid=(n // seq,),
        in_specs=in_specs,
        out_specs=out_specs,
        scratch_shapes=[pltpu.VMEM((Z_COLS, D_MODEL), bf16), pltpu.VMEM((seq, Q_COLS), bf16),
                        pltpu.VMEM((1, 1, seq, KV_COLS), bf16), pltpu.VMEM((seq, D_MODEL), bf16),
                        pltpu.VMEM((heads, seq, seq), f32), pltpu.VMEM((heads, seq, seq), bf16),
                        pltpu.VMEM((D_MODEL, D_MODEL), bf16), pltpu.VMEM((seq, D_MODEL), f32)],
        input_output_aliases=aliases,
        compiler_params=_params(("arbitrary",)),
        name="context_layer",
    )(*args)


PLAN_CHUNK = 1024
TAB_ROWS = LANES


def _plan_kernel(cls_ref, slot_ref, tab_ref, rank_scr):
    n = cls_ref.shape[0]
    lane = jax.lax.broadcasted_iota(i32, (PLAN_CHUNK, LANES), 1)
    r = jax.lax.broadcasted_iota(i32, (PLAN_CHUNK, PLAN_CHUNK), 0)
    c = jax.lax.broadcasted_iota(i32, (PLAN_CHUNK, PLAN_CHUNK), 1)
    before = (c < r).astype(bf16)

    def count(b, seen):
        rows = pl.ds(pl.multiple_of(b * PLAN_CHUNK, PLAN_CHUNK), PLAN_CHUNK)
        onehot = (cls_ref[rows, :] == lane).astype(f32)
        ahead = _dot(before, onehot.astype(bf16)) + seen
        rank_scr[rows, :] = jnp.sum(onehot * ahead, axis=-1, keepdims=True)
        return seen + jnp.sum(onehot, axis=0, keepdims=True)

    counts = jax.lax.fori_loop(0, n // PLAN_CHUNK, count, jnp.zeros((1, LANES), f32))
    tiles = jnp.floor((counts + (EXPERT_TILE - 1)) * (1.0 / EXPERT_TILE))
    rr = jax.lax.broadcasted_iota(i32, (LANES, LANES), 0)
    cc = jax.lax.broadcasted_iota(i32, (LANES, LANES), 1)
    ends = _dot(jnp.broadcast_to(tiles, (8, LANES)).astype(bf16), (rr <= cc).astype(bf16))[0:1]
    starts = ends - tiles

    def place(b, carry):
        rows = pl.ds(pl.multiple_of(b * PLAN_CHUNK, PLAN_CHUNK), PLAN_CHUNK)
        first = jnp.sum(jnp.where(cls_ref[rows, :] == lane, starts, 0.0), axis=-1, keepdims=True)
        slot_ref[rows, :] = (first * EXPERT_TILE + rank_scr[rows, :]).astype(i32)
        return carry

    jax.lax.fori_loop(0, n // PLAN_CHUNK, place, 0)

    tl = jax.lax.broadcasted_iota(i32, (TAB_ROWS, LANES), 1)
    n_tiles = jnp.sum(jnp.where(tl[0:1] == N_CLASSES - 1, ends, 0.0), axis=-1, keepdims=True)
    k = jnp.minimum(jax.lax.broadcasted_iota(i32, (TAB_ROWS, 1), 0).astype(f32), n_tiles - 1.0)
    cls_k = jnp.sum(jnp.where((tl < N_CLASSES) & (ends <= k), 1.0, 0.0), axis=-1, keepdims=True)
    cls_k = jnp.minimum(cls_k, N_CLASSES - 1.0)
    mine = tl.astype(f32) == cls_k
    used = jnp.sum(jnp.where(mine, counts, 0.0), axis=-1, keepdims=True)
    first = jnp.sum(jnp.where(mine, starts, 0.0), axis=-1, keepdims=True)
    valid = jnp.clip(used - (k - first) * EXPERT_TILE, 0.0, float(EXPERT_TILE))
    group = jnp.floor((cls_k + 0.5) * (1.0 / N_PAIRS))
    pair = cls_k - N_PAIRS * group
    lo = hi = jnp.zeros_like(pair)
    for p in range(N_PAIRS):
        lo = jnp.where(pair == p, float(PAIR_LO[p]), lo)
        hi = jnp.where(pair == p, float(PAIR_HI[p]), hi)
    e_lo = EXPERTS_PER_GROUP * group + lo
    e_hi = EXPERTS_PER_GROUP * group + hi
    tab = jnp.where(tl == 0, e_lo, jnp.where(tl == 1, e_hi, jnp.where(tl == 2, valid, jnp.where(tl == 3, n_tiles, 0.0))))
    tab_ref[...] = tab.astype(i32)


def _plan(cls, max_tiles):
    n = cls.shape[0]
    assert n % PLAN_CHUNK == 0 and max_tiles <= TAB_ROWS
    slot, tab = pl.pallas_call(
        _plan_kernel,
        out_shape=[jax.ShapeDtypeStruct((n, 1), i32), jax.ShapeDtypeStruct((TAB_ROWS, LANES), i32)],
        scratch_shapes=[pltpu.VMEM((n, 1), f32)],
        compiler_params=_params(None),
        name="dispatch_plan",
    )(cls)
    return slot.reshape(n), tab[:max_tiles, 0], tab[:max_tiles, 1], tab[:max_tiles, 2], tab[0, 3:4]


def _move_rows(src, idx, n_out, scatter):
    n = idx.shape[0]
    width = src.shape[1]
    per_worker = n // (SC_CORES * SC_SUBCORES)
    assert n % SC_ROWS == 0
    chunk = max(c for c in (64, 40, 32, 16, 8)
                if per_worker % c == 0 and 2 * c * width * src.dtype.itemsize <= SC_BUFFER_BYTES)
    n_chunks = per_worker // chunk
    mesh = plsc.VectorSubcoreMesh(core_axis_name="c", subcore_axis_name="s")

    @functools.partial(
        pl.kernel, mesh=mesh, out_type=jax.ShapeDtypeStruct((n_out, width), src.dtype),
        scratch_types=[pltpu.VMEM((chunk,), i32), pltpu.VMEM((chunk,), i32),
                       pltpu.VMEM((chunk, width), src.dtype), pltpu.VMEM((chunk, width), src.dtype),
                       pltpu.SemaphoreType.DMA, pltpu.SemaphoreType.DMA, pltpu.SemaphoreType.DMA,
                       pltpu.SemaphoreType.DMA])
    def move(src_hbm, idx_hbm, out_hbm, idx0, idx1, rows0, rows1, in0, in1, out0, out1):
        wid = jax.lax.axis_index("s") * SC_CORES + jax.lax.axis_index("c")
        base = wid * per_worker
        idx_v, rows_v, sem_in, sem_out = (idx0, idx1), (rows0, rows1), (in0, in1), (out0, out1)

        def fill(j):
            b = j % 2
            rows = pl.ds(base + j * chunk, chunk)
            pltpu.sync_copy(idx_hbm.at[rows], idx_v[b])
            src_rows = src_hbm.at[rows] if scatter else src_hbm.at[idx_v[b]]
            return pltpu.async_copy(src_rows, rows_v[b], sem_in[b])

        def drain(j):
            b = j % 2
            dst_rows = out_hbm.at[idx_v[b]] if scatter else out_hbm.at[pl.ds(base + j * chunk, chunk)]
            return pltpu.async_copy(rows_v[b], dst_rows, sem_out[b])

        fills, drains = {0: fill(0)}, {}
        for j in range(n_chunks):
            if j + 1 < n_chunks:
                if j >= 1:
                    drains[j - 1].wait()
                fills[j + 1] = fill(j + 1)
            fills[j].wait()
            drains[j] = drain(j)
        for j in range(max(n_chunks - 2, 0), n_chunks):
            drains[j].wait()

    return move(src, idx)


def _expert_kernel(lo_ref, hi_ref, valid_ref, nt_ref, xs_ref, wg_ref, wu_ref, wd_ref, ys_ref, wg_scr, wu_scr, wd_scr):
    k = pl.program_id(0)
    prev = jnp.maximum(k - 1, 0)

    @pl.when((k == 0) | (lo_ref[k] // EXPERTS_PER_GROUP != lo_ref[prev] // EXPERTS_PER_GROUP))
    def _():
        for j in range(EXPERTS_PER_GROUP):
            wg_scr[j] = wg_ref[0, 0, j].astype(bf16)
            wu_scr[j] = wu_ref[0, 0, j].astype(bf16)
            wd_scr[j] = wd_ref[0, 0, j].astype(bf16)

    @pl.when(k < nt_ref[0])
    def _():
        live = jax.lax.broadcasted_iota(i32, (EXPERT_TILE, 1), 0) < valid_ref[k]
        words = jnp.where(live, xs_ref[:, 0:HX_HALF], 0)
        x = jnp.concatenate([pltpu.bitcast(words & -65536, f32), pltpu.bitcast(words << 16, f32)], axis=1).astype(bf16)
        gates = pltpu.bitcast(jnp.where(live, xs_ref[:, HX_HALF:HX_COLS], 0), f32)
        y = None
        for lane, e_ref in enumerate((lo_ref, hi_ref)):
            j = e_ref[k] % EXPERTS_PER_GROUP
            hid = _silu(_dot(x, wg_scr[j])) * _dot(x, wu_scr[j]) * gates[:, lane:lane + 1]
            part = _dot(hid.astype(bf16), wd_scr[j])
            y = part if y is None else y + part
        ys_ref[...] = y


def _experts(layer, xs, e_lo, e_hi, valid, n_tiles, w):
    max_tiles = e_lo.shape[0]
    row = lambda k, lo, hi, valid, nt: (jnp.minimum(k, nt[0] - 1), 0)
    group = lambda k, lo, hi, valid, nt: (layer, lo[k] // EXPERTS_PER_GROUP, 0, 0, 0)
    by_group = lambda a: a.reshape(DEPTH, N_GROUPS, EXPERTS_PER_GROUP, *a.shape[2:])
    return pl.pallas_call(
        _expert_kernel,
        out_shape=jax.ShapeDtypeStruct((max_tiles * EXPERT_TILE, D_MODEL), f32),
        grid_spec=pltpu.PrefetchScalarGridSpec(
            num_scalar_prefetch=4, grid=(max_tiles,),
            in_specs=[pl.BlockSpec((EXPERT_TILE, HX_COLS), row),
                      pl.BlockSpec((1, 1, EXPERTS_PER_GROUP, D_MODEL, D_FF_EXPERT), group),
                      pl.BlockSpec((1, 1, EXPERTS_PER_GROUP, D_MODEL, D_FF_EXPERT), group),
                      pl.BlockSpec((1, 1, EXPERTS_PER_GROUP, D_FF_EXPERT, D_MODEL), group)],
            out_specs=pl.BlockSpec((EXPERT_TILE, D_MODEL), row),
            scratch_shapes=[pltpu.VMEM((EXPERTS_PER_GROUP, D_MODEL, D_FF_EXPERT), bf16),
                            pltpu.VMEM((EXPERTS_PER_GROUP, D_MODEL, D_FF_EXPERT), bf16),
                            pltpu.VMEM((EXPERTS_PER_GROUP, D_FF_EXPERT, D_MODEL), bf16)]),
        compiler_params=_params(("arbitrary",)),
        name="experts",
    )(e_lo, e_hi, valid, n_tiles, xs, by_group(w["w_gate"]), by_group(w["w_up"]), by_group(w["w_down"]))


def _final_kernel(x1_ref, y_ref, mod_ref, fg_ref, o_ref):
    o_ref[...] = _rms(x1_ref[...] + mod_ref[0, 0, 5:6, :] * y_ref[...], D_MODEL) * fg_ref[...]


def _final(x1, y, n, row0, mod, mod_row, w, tile):
    off = row0 // tile
    src_row = lambda t: (off + t, 0)
    return pl.pallas_call(
        _final_kernel,
        out_shape=jax.ShapeDtypeStruct((n, D_MODEL), f32),
        grid=(n // tile,),
        in_specs=[pl.BlockSpec((tile, D_MODEL), src_row), pl.BlockSpec((tile, D_MODEL), src_row),
                  pl.BlockSpec((1, 1, N_MOD, D_MODEL), lambda t: (DEPTH - 1, mod_row(t * tile), 0, 0)),
                  pl.BlockSpec((1, D_MODEL), lambda t: (0, 0))],
        out_specs=pl.BlockSpec((tile, D_MODEL), lambda t: (t, 0)),
        compiler_params=_params(("arbitrary",)),
        name="final_norm",
    )(x1, y, mod, w["final_g"])


def _rope_tables(n_tokens):
    pos = np.arange(n_tokens)
    row = (pos // GRID_W).astype(np.float64)
    col = (pos % GRID_W).astype(np.float64)

    def cs(rot_dim):
        quarter = rot_dim // 4
        inv = ROPE_THETA ** (-np.arange(quarter, dtype=np.float64) / quarter)
        ang = np.concatenate([row[:, None] * inv, col[:, None] * inv], axis=-1)
        return np.cos(ang), np.sin(ang)

    c32, s32 = cs(MLA_ROPE)
    c64, s64 = cs(HEAD_DIM)
    ones = np.ones((n_tokens, MLA_NOPE))
    zeros = np.zeros((n_tokens, MLA_NOPE))

    def rep(parts):
        period = np.concatenate(parts, axis=-1)
        return jnp.asarray(np.tile(period, (1, LANES // period.shape[-1])), f32)

    return (rep([ones, c32, c32]), rep([zeros, -s32, s32]), rep([c32, c32]), rep([-s32, s32]),
            rep([c64, c64]), rep([-s64, s64]))


def _layout_weights(norm1_g, norm2_g, w_in, mla_kv_norm_g, mla_w_uk, mla_w_uv, gqa_q_norm_g, gqa_k_norm_g,
                    diff_lambda, diff_norm_g, w_out, moe_w_group, moe_b_group, moe_w_router, moe_b_router,
                    moe_w_gate, moe_w_up, moe_w_down, final_norm_g):
    eye = jnp.eye(MLA_ROPE, dtype=f32)
    top = jnp.concatenate([mla_w_uk, jnp.zeros((DEPTH, KV_RANK, MLA_HEADS, MLA_ROPE), f32)], axis=-1)
    mid = jnp.concatenate([jnp.zeros((MLA_ROPE, MLA_HEADS, MLA_NOPE), f32),
                           jnp.broadcast_to(eye[:, None, :], (MLA_ROPE, MLA_HEADS, MLA_ROPE))], axis=-1)
    w_ka = jnp.concatenate([top.reshape(DEPTH, KV_RANK, 384),
                            jnp.broadcast_to(mid.reshape(1, MLA_ROPE, 384), (DEPTH, MLA_ROPE, 384)),
                            jnp.zeros((DEPTH, 256 - KV_RANK - MLA_ROPE, 384), f32)], axis=1).astype(bf16)
    seg_id = np.arange(512) // HEAD_DIM
    seg = jnp.asarray(seg_id[:, None] == seg_id[None, :], bf16)
    qk_g = jnp.concatenate([jnp.tile(gqa_q_norm_g, (1, GQA_HEADS)), jnp.tile(gqa_k_norm_g, (1, GQA_KV_HEADS))], axis=-1)
    return dict(
        g1=norm1_g.reshape(DEPTH, 1, D_MODEL), g2=norm2_g.reshape(DEPTH, 1, D_MODEL),
        w_in=jnp.swapaxes(w_in, 1, 2),
        kv_g=mla_kv_norm_g.reshape(DEPTH, 1, KV_RANK), qk_g=qk_g.reshape(DEPTH, 1, 512), seg=seg, w_ka=w_ka,
        w_uv=mla_w_uv.reshape(DEPTH, KV_RANK, 384), lam=diff_lambda, diff_g=diff_norm_g.reshape(DEPTH, 1, DIFF_V),
        w_out=w_out, w_grp=moe_w_group, b_grp=moe_b_group.reshape(DEPTH, 1, N_GROUPS), w_rtr=moe_w_router,
        b_rtr=moe_b_router.reshape(DEPTH, 1, N_EXPERTS), w_gate=moe_w_gate, w_up=moe_w_up, w_down=moe_w_down,
        final_g=final_norm_g.reshape(1, D_MODEL))


PRE_TILE = 512
CTX_ATTN_TILE = 256
LAT_ATTN_TILE = 256
CTX_HEADS_PER_ROUND = 20
LAT_HEADS_PER_ROUND = 5
POST_TILE = 512
FINAL_TILE = 512


def kernel(x_prompt, x_sample, c, cache_mla_ckv, cache_mla_krope, cache_gqa_k, cache_gqa_v, cache_diff_k, cache_diff_v, c_ctx, norm1_g, norm2_g, w_mod, b_mod, w_in, mla_kv_norm_g, mla_w_uk, mla_w_uv, gqa_q_norm_g, gqa_k_norm_g, diff_lambda, diff_norm_g, w_out, moe_w_group, moe_b_group, moe_w_router, moe_b_router, moe_w_gate, moe_w_up, moe_w_down, final_norm_g):
    B, S, _ = x_prompt.shape
    Bl, Sl, _ = x_sample.shape
    n_ctx, n_lat = B * S, Bl * Sl
    total = n_ctx + n_lat
    assert S == CTX_ATTN_TILE and Bl + 1 <= MOD_ROWS and DEPTH == 2 and total % SC_ROWS == 0
    slot_rows = -(-(total + N_CLASSES * EXPERT_TILE) // SC_ROWS) * SC_ROWS
    max_tiles = slot_rows // EXPERT_TILE
    w = _layout_weights(norm1_g, norm2_g, w_in, mla_kv_norm_g, mla_w_uk, mla_w_uv, gqa_q_norm_g, gqa_k_norm_g,
                        diff_lambda, diff_norm_g, w_out, moe_w_group, moe_b_group, moe_w_router, moe_b_router,
                        moe_w_gate, moe_w_up, moe_w_down, final_norm_g)
    cond = jnp.concatenate([c_ctx[None, :], c, jnp.zeros((MOD_ROWS - 1 - Bl, D_MODEL), f32)], axis=0)
    mod = _modulation(cond, w_mod, b_mod).reshape(DEPTH, MOD_ROWS, N_MOD, D_MODEL)
    ctx_row = lambda token: 0
    lat_row = lambda token: 1 + token // Sl
    tabs = _rope_tables(Sl)
    kv_past = _cache_rows((cache_mla_ckv, cache_mla_krope, cache_gqa_k, cache_gqa_v, cache_diff_k, cache_diff_v), w)
    per_b = Sl // LAT_ATTN_TILE

    x_ctx, x_lat = x_prompt.reshape(n_ctx, D_MODEL), x_sample.reshape(n_lat, D_MODEL)
    cache = ()
    x1 = y = None
    for i in range(DEPTH):
        if i == 0:
            *cache, x1_c, hx_c, cls_c = _ctx_layer(i, x_ctx, n_ctx, S, total, mod, w)
        else:
            *cache, x1_c, hx_c, cls_c = _ctx_layer(i, x1, n_ctx, S, total, mod, w, prev_cache=cache, resid=y)
        if i == 0:
            q_l, kv_l = _pre(i, x_lat, n_lat, 0, Sl, mod, lat_row, w, PRE_TILE, rope_tabs=tabs)
        else:
            q_l, kv_l, x_lat = _pre(i, x1, n_lat, n_ctx, Sl, mod, lat_row, w, PRE_TILE, rope_tabs=tabs, resid=y)
        past = (kv_past, PAST_LEN, lambda t, i=i: (i, t // per_b, 0, 0))
        own = (kv_l.reshape(1, Bl, Sl, KV_COLS), Sl, lambda t: (0, t // per_b, 0, 0))
        o_l = _attention(i, q_l, [past, own], w, LAT_ATTN_TILE, LAT_HEADS_PER_ROUND)
        x1, hx, cls = _post(i, o_l, x_lat, n_ctx, total, mod, lat_row, w, POST_TILE, merged=(x1_c, hx_c, cls_c))
        slot, e_lo, e_hi, valid, n_tiles = _plan(cls, max_tiles)
        xs = _move_rows(hx, slot, slot_rows, scatter=True)
        ys = _experts(i, xs, e_lo, e_hi, valid, n_tiles, w)
        y = _move_rows(ys, slot, total, scatter=False)

    y_prompt = _final(x1, y, n_ctx, 0, mod, ctx_row, w, FINAL_TILE).reshape(B, S, D_MODEL)
    y_sample = _final(x1, y, n_lat, n_ctx, mod, lat_row, w, FINAL_TILE).reshape(Bl, Sl, D_MODEL)
    new_mla_ckv, new_mla_krope = cache[0], cache[1]
    new_gqa_k = cache[2].reshape(B, DEPTH, S, GQA_KV_HEADS, HEAD_DIM)
    new_gqa_v = cache[3].reshape(B, DEPTH, S, GQA_KV_HEADS, HEAD_DIM)
    new_diff_k = cache[4].reshape(B, DEPTH, S, DIFF_HEADS, 2, DIFF_QK)
    new_diff_v = cache[5].reshape(B, DEPTH, S, DIFF_HEADS, DIFF_V)
    return (y_prompt, y_sample, new_mla_ckv, new_mla_krope, new_gqa_k, new_gqa_v, new_diff_k, new_diff_v)
```

```python
import functools
import math

import jax
import jax.numpy as jnp
import numpy as np
from jax.experimental import pallas as pl
from jax.experimental.pallas import tpu as pltpu
from jax.experimental.pallas import tpu_sc as plsc

D_MODEL = 1024
DEPTH = 2
PAST_LEN = 512
GRID_W = 64
ROPE_THETA = 10000.0
EPS = 1e-6
LOG2E = 1.4426950408889634
N_MOD = 6
HEAD_DIM = 64
MLA_HEADS = 6
MLA_NOPE = 32
MLA_ROPE = 32
MLA_V = 64
KV_RANK = 128
GQA_HEADS = 6
GQA_KV_HEADS = 2
GQA_GROUP = GQA_HEADS // GQA_KV_HEADS
DIFF_HEADS = 4
DIFF_QK = 32
DIFF_V = 64
N_GROUPS = 4
EXPERTS_PER_GROUP = 4
N_EXPERTS = N_GROUPS * EXPERTS_PER_GROUP
D_FF_EXPERT = 256

LANES = 128
MOD_ROWS = 8

IN_COLS = 1952
IN_KR = 512
Z_QA, Z_CKV, Z_QG, Z_KG, Z_VG, Z_QD, Z_KD, Z_VD, Z_KR = 0, 384, 512, 896, 1024, 1152, 1408, 1664, 1920
Z_COLS = 2048
Q_A, Q_G, Q_D, Q_COLS = 0, 384, 768, 1024
KV_KA, KV_VA, KV_KG, KV_VG, KV_KD, KV_VD, KV_COLS = 0, 384, 768, 896, 1024, 1280, 1536
CACHE_WIDTHS = (128, 32, 128, 128, 256, 256)

PAIR_LO = (0, 0, 0, 1, 1, 2)
PAIR_HI = (1, 2, 3, 3, 2, 3)
N_PAIRS = len(PAIR_LO)
N_CLASSES = N_GROUPS * N_PAIRS
HX_HALF = D_MODEL // 2
HX_COLS = HX_HALF + LANES
EXPERT_TILE = 256

SC_CORES, SC_SUBCORES = 2, 16
SC_BUFFER_BYTES = 400 * 1024
SC_ROWS = SC_CORES * SC_SUBCORES * 8

VMEM_LIMIT = 56 * 1024 * 1024

bf16 = jnp.bfloat16
f32 = jnp.float32
i32 = jnp.int32


def _dot(a, b):
    return jnp.dot(a, b, preferred_element_type=f32)


def _dot_nt(a, b):
    return jax.lax.dot_general(a, b, (((1,), (1,)), ((), ())), preferred_element_type=f32)


def _rms(x, width):
    return x * jax.lax.rsqrt(jnp.sum(x * x, axis=-1, keepdims=True) * (1.0 / width) + EPS)


def _silu(x):
    return x * (1.0 / (1.0 + jnp.exp(-x)))


def _params(sem):
    return pltpu.CompilerParams(dimension_semantics=sem, vmem_limit_bytes=VMEM_LIMIT)


def _mod_kernel(cond_ref, w_ref, b_ref, o_ref):
    o_ref[0] = _dot(_silu(cond_ref[...]).astype(bf16), w_ref[0].astype(bf16)) + b_ref[0]


def _modulation(cond, w_mod, b_mod):
    return pl.pallas_call(
        _mod_kernel,
        out_shape=jax.ShapeDtypeStruct((DEPTH, MOD_ROWS, N_MOD * D_MODEL), f32),
        grid=(DEPTH, N_MOD),
        in_specs=[
            pl.BlockSpec((MOD_ROWS, D_MODEL), lambda i, j: (0, 0)),
            pl.BlockSpec((1, D_MODEL, D_MODEL), lambda i, j: (i, 0, j)),
            pl.BlockSpec((1, 1, D_MODEL), lambda i, j: (i, 0, j)),
        ],
        out_specs=pl.BlockSpec((1, MOD_ROWS, D_MODEL), lambda i, j: (i, 0, j)),
        compiler_params=_params(("arbitrary", "arbitrary")),
        name="modulation",
    )(cond, w_mod, b_mod.reshape(DEPTH, 1, N_MOD * D_MODEL))


def _swap_halves(x, half):
    lane = jax.lax.broadcasted_iota(i32, x.shape, 1)
    fwd = pltpu.roll(x, LANES - half, 1)
    bwd = pltpu.roll(x, half, 1)
    return jnp.where((lane & (2 * half - 1)) < half, fwd, bwd)


def _rope_block(x, cos, sin, half):
    return x * cos + _swap_halves(x, half) * sin


def _pre_kernel(rope, n_prev, resid, *refs):
    it = iter(refs)
    x_ref, mod_ref, g1_ref, w_in_ref, kvg_ref, qkg_ref, seg_ref, wka_ref, wuv_ref = (next(it) for _ in range(9))
    if resid:
        y_ref, pmod_ref = next(it), next(it)
    if rope:
        ca_ref, sa_ref, c32_ref, s32_ref, c64_ref, s64_ref = (next(it) for _ in range(6))
    prev_refs = [next(it) for _ in range(n_prev)]
    q_ref, kv_ref = next(it), next(it)
    if resid:
        x2_ref = next(it)
    cache_refs = [] if rope else [next(it) for _ in range(len(CACHE_WIDTHS))]
    w_scr = next(it)

    @pl.when(pl.program_id(0) == 0)
    def _():
        w_scr[0:IN_KR] = w_in_ref[0, 0:IN_KR].astype(bf16)
        w_scr[IN_KR:Z_KR] = w_in_ref[0, IN_KR + MLA_ROPE:IN_COLS].astype(bf16)
        w_scr[Z_KR:Z_KR + MLA_ROPE] = w_in_ref[0, IN_KR:IN_KR + MLA_ROPE].astype(bf16)
        w_scr[Z_KR + MLA_ROPE:Z_COLS] = jnp.zeros((Z_COLS - Z_KR - MLA_ROPE, D_MODEL), bf16)

    x = x_ref[...]
    if resid:
        x = x + pmod_ref[0, 0, 5:6, :] * y_ref[...]
        x2_ref[...] = x
    shift1 = mod_ref[0, 0, 0:1, :]
    scale1 = mod_ref[0, 0, 1:2, :]
    h = (_rms(x, D_MODEL) * g1_ref[0]) * (1.0 + scale1) + shift1
    z = _dot_nt(h.astype(bf16), w_scr[...])

    ckv = _rms(z[:, Z_CKV:Z_CKV + KV_RANK], KV_RANK) * kvg_ref[0]

    qk = z[:, Z_QG:Z_VG]
    sq = qk * qk
    sq_hi = sq.astype(bf16)
    sq_lo = (sq - sq_hi.astype(f32)).astype(bf16)
    seg = seg_ref[...]
    ms = (_dot(sq_hi, seg) + _dot(sq_lo, seg)) * (1.0 / HEAD_DIM)
    qk = qk * jax.lax.rsqrt(ms + EPS) * qkg_ref[0]

    def blocks(arr, n):
        return [arr[:, LANES * j:LANES * (j + 1)] for j in range(n)]

    qa = blocks(z[:, Z_QA:Z_QA + 384], 3)
    qkb = blocks(qk, 4)
    qd = blocks(z[:, Z_QD:Z_QD + 256], 2)
    kd = blocks(z[:, Z_KD:Z_KD + 256], 2)
    kr = z[:, Z_KR:Z_KR + LANES]
    if rope:
        ca, sa, c32, s32, c64, s64 = (r[...] for r in (ca_ref, sa_ref, c32_ref, s32_ref, c64_ref, s64_ref))
        qa = [_rope_block(b, ca, sa, MLA_ROPE // 2) for b in qa]
        qkb = [_rope_block(b, c64, s64, HEAD_DIM // 2) for b in qkb]
        qd = [_rope_block(b, c32, s32, DIFF_QK // 2) for b in qd]
        kd = [_rope_block(b, c32, s32, DIFF_QK // 2) for b in kd]
        kr = _rope_block(kr, c32, s32, MLA_ROPE // 2)

    vg = z[:, Z_VG:Z_VG + 128]
    vd = z[:, Z_VD:Z_VD + 256]
    ckv_b = ckv.astype(bf16)
    k_a = _dot(jnp.concatenate([ckv_b, kr.astype(bf16)], axis=1), wka_ref[0])
    v_a = _dot(ckv_b, wuv_ref[0].astype(bf16))

    for j in range(3):
        q_ref[:, Q_A + LANES * j:Q_A + LANES * (j + 1)] = (qa[j] * (HEAD_DIM ** -0.5 * LOG2E)).astype(bf16)
        q_ref[:, Q_G + LANES * j:Q_G + LANES * (j + 1)] = (qkb[j] * (HEAD_DIM ** -0.5 * LOG2E)).astype(bf16)
    for j in range(2):
        q_ref[:, Q_D + LANES * j:Q_D + LANES * (j + 1)] = (qd[j] * (DIFF_QK ** -0.5 * LOG2E)).astype(bf16)
        kv_ref[:, KV_KD + LANES * j:KV_KD + LANES * (j + 1)] = kd[j].astype(bf16)
    kv_ref[:, KV_KA:KV_KA + 384] = k_a.astype(bf16)
    kv_ref[:, KV_VA:KV_VA + 384] = v_a.astype(bf16)
    kv_ref[:, KV_KG:KV_KG + 128] = qkb[3].astype(bf16)
    kv_ref[:, KV_VG:KV_VG + 128] = vg.astype(bf16)
    kv_ref[:, KV_VD:KV_VD + 256] = vd.astype(bf16)
    if not rope:
        rows = [ckv, kr[:, :MLA_ROPE], qkb[3], vg, jnp.concatenate(kd, axis=1), vd]
        for out, new in zip(cache_refs, rows):
            reqs, _, seq, width = out.shape
            out[:, 0] = new.reshape(reqs, seq, width)


def _pre(layer, x, n, row0, seq, mod, mod_row, w, tile, rope_tabs=None, prev_cache=(), resid=None):
    rope = rope_tabs is not None
    lay = lambda t: (layer, 0, 0)
    row = lambda t: (t, 0)
    off = row0 // tile
    src_row = lambda t: (off + t, 0)
    in_specs = [
        pl.BlockSpec((tile, D_MODEL), src_row),
        pl.BlockSpec((1, 1, N_MOD, D_MODEL), lambda t: (layer, mod_row(t * tile), 0, 0)),
        pl.BlockSpec((1, 1, D_MODEL), lay),
        pl.BlockSpec((1, IN_COLS, D_MODEL), lay),
        pl.BlockSpec((1, 1, KV_RANK), lay),
        pl.BlockSpec((1, 1, 512), lay),
        pl.BlockSpec((512, 512), lambda t: (0, 0)),
        pl.BlockSpec((1, 256, 384), lay),
        pl.BlockSpec((1, KV_RANK, 384), lay),
    ]
    args = [x, mod, w["g1"], w["w_in"], w["kv_g"], w["qk_g"], w["seg"], w["w_ka"], w["w_uv"]]
    if resid is not None:
        in_specs += [pl.BlockSpec((tile, D_MODEL), src_row),
                     pl.BlockSpec((1, 1, N_MOD, D_MODEL), lambda t: (layer - 1, mod_row(t * tile), 0, 0))]
        args += [resid, mod]
    if rope:
        per_b = seq // tile
        in_specs += [pl.BlockSpec((tile, LANES), lambda t: (t % per_b, 0))] * 6
        args += list(rope_tabs)
    out_shape = [jax.ShapeDtypeStruct((n, Q_COLS), bf16), jax.ShapeDtypeStruct((n, KV_COLS), bf16)]
    out_specs = [pl.BlockSpec((tile, Q_COLS), row), pl.BlockSpec((tile, KV_COLS), row)]
    if resid is not None:
        out_shape.append(jax.ShapeDtypeStruct((n, D_MODEL), f32))
        out_specs.append(pl.BlockSpec((tile, D_MODEL), row))
    aliases = {}
    if not rope:
        reqs = tile // seq
        for j, width in enumerate(CACHE_WIDTHS):
            if prev_cache:
                aliases[len(args)] = len(out_shape)
                in_specs.append(pl.BlockSpec(memory_space=pl.ANY))
                args.append(prev_cache[j])
            out_shape.append(jax.ShapeDtypeStruct((n // seq, DEPTH, seq, width), f32))
            out_specs.append(pl.BlockSpec((reqs, 1, seq, width), lambda t: (t, layer, 0, 0)))
    return pl.pallas_call(
        functools.partial(_pre_kernel, rope, len(prev_cache), resid is not None),
        out_shape=out_shape,
        grid=(n // tile,),
        in_specs=in_specs,
        out_specs=out_specs,
        input_output_aliases=aliases,
        scratch_shapes=[pltpu.VMEM((Z_COLS, D_MODEL), bf16)],
        compiler_params=_params(("arbitrary",)),
        name="pre_latent" if rope else "pre_context",
    )(*args)


def _cache_kernel(ckv_ref, kr_ref, kg_ref, vg_ref, kd_ref, vd_ref, wka_ref, wuv_ref, kv_ref):
    ckv_b = ckv_ref[0, 0].astype(bf16)
    wka = wka_ref[0]
    k_a = _dot(ckv_b, wka[:KV_RANK]) + _dot(kr_ref[0, 0].astype(bf16), wka[KV_RANK:KV_RANK + MLA_ROPE])
    kv_ref[0, 0, :, KV_KA:KV_KA + 384] = k_a.astype(bf16)
    kv_ref[0, 0, :, KV_VA:KV_VA + 384] = _dot(ckv_b, wuv_ref[0].astype(bf16)).astype(bf16)
    kv_ref[0, 0, :, KV_KG:KV_KG + 128] = kg_ref[0, 0].astype(bf16)
    kv_ref[0, 0, :, KV_VG:KV_VG + 128] = vg_ref[0, 0].astype(bf16)
    kv_ref[0, 0, :, KV_KD:KV_KD + 256] = kd_ref[0, 0].astype(bf16)
    kv_ref[0, 0, :, KV_VD:KV_VD + 256] = vd_ref[0, 0].astype(bf16)


def _cache_rows(caches, w):
    B = caches[0].shape[0]
    spec = lambda width: pl.BlockSpec((1, 1, PAST_LEN, width), lambda i, b: (b, i, 0, 0))
    return pl.pallas_call(
        _cache_kernel,
        out_shape=jax.ShapeDtypeStruct((DEPTH, B, PAST_LEN, KV_COLS), bf16),
        grid=(DEPTH, B),
        in_specs=[spec(width) for width in CACHE_WIDTHS]
        + [pl.BlockSpec((1, 256, 384), lambda i, b: (i, 0, 0)), pl.BlockSpec((1, KV_RANK, 384), lambda i, b: (i, 0, 0))],
        out_specs=pl.BlockSpec((1, 1, PAST_LEN, KV_COLS), lambda i, b: (i, b, 0, 0)),
        compiler_params=_params(("arbitrary", "arbitrary")),
        name="cache_rows",
    )(*[c.reshape(B, DEPTH, PAST_LEN, width) for c, width in zip(caches, CACHE_WIDTHS)], w["w_ka"], w["w_uv"])


_SCORE_HEADS = (
    [(Q_A + 64 * h, KV_KA + 64 * h, 64, KV_VA + MLA_V * h) for h in range(MLA_HEADS)]
    + [(Q_G + 64 * h, KV_KG + 64 * (h // GQA_GROUP), 64, KV_VG + 64 * (h // GQA_GROUP)) for h in range(GQA_HEADS)]
    + [(Q_D + 64 * h + DIFF_QK * c, KV_KD + 64 * h + DIFF_QK * c, DIFF_QK, KV_VD + DIFF_V * h)
       for h in range(DIFF_HEADS) for c in range(2)])


def _attn_kernel(lam_init, per_round, n_src, q_ref, *refs):
    kv_refs = refs[:n_src]
    lam_ref, dg_ref, o_ref, s_ref, p_ref = refs[n_src:]
    spans, start = [], 0
    for r in kv_refs:
        spans.append((r, start, r.shape[2]))
        start += r.shape[2]
    mxu_sum = start >= MXU_SUM_MIN_KEYS

    outs = []
    for first in range(0, len(_SCORE_HEADS), per_round):
        chunk = _SCORE_HEADS[first:first + per_round]
        for j, (q_off, k_off, width, _) in enumerate(chunk):
            for r, lo, size in spans:
                s_ref[j, :, lo:lo + size] = _dot_nt(q_ref[:, q_off:q_off + width], r[0, 0, :, k_off:k_off + width])
        s = s_ref[...]
        p = jnp.exp2(s - jnp.max(s, axis=-1, keepdims=True))
        if mxu_sum:
            p_ref[...] = p.astype(bf16)
            for j, (_, _, _, v_off) in enumerate(chunk):
                o = sum(_dot(p_ref[j, :, lo:lo + size],
                             jnp.concatenate([r[0, 0, :, v_off:v_off + DIFF_V],
                                              jnp.ones((size, LANES - DIFF_V), bf16)], axis=1))
                        for r, lo, size in spans)
                outs.append((o * pltpu.roll(1.0 / o, DIFF_V, 1))[:, :DIFF_V])
        else:
            inv = 1.0 / jnp.sum(p, axis=-1, keepdims=True)
            p_ref[...] = p.astype(bf16)
            for j, (_, _, _, v_off) in enumerate(chunk):
                o = sum(_dot(p_ref[j, :, lo:lo + size], r[0, 0, :, v_off:v_off + DIFF_V]) for r, lo, size in spans)
                outs.append(o * inv[j])

    lp = lam_ref[0]
    e1 = jnp.exp(jnp.sum(lp[0:1] * lp[1:2], axis=-1, keepdims=True))
    e2 = jnp.exp(jnp.sum(lp[2:3] * lp[3:4], axis=-1, keepdims=True))
    lam = e1 - e2 + lam_init
    heads = outs[:MLA_HEADS + GQA_HEADS]
    for h in range(DIFF_HEADS):
        o1, o2 = outs[MLA_HEADS + GQA_HEADS + 2 * h:MLA_HEADS + GQA_HEADS + 2 * h + 2]
        heads.append(_rms(o1 - lam * o2, DIFF_V) * dg_ref[0] * (1.0 - lam_init))
    for j in range(len(heads) // 2):
        o_ref[:, LANES * j:LANES * (j + 1)] = jnp.concatenate(heads[2 * j:2 * j + 2], axis=1).astype(bf16)


def _attention(layer, q, sources, w, tile, per_round):
    n = q.shape[0]
    lam_init = 0.8 - 0.6 * math.exp(-0.3 * layer)
    s_kv = sum(rows for _, rows, _ in sources)
    assert len(_SCORE_HEADS) % per_round == 0
    return pl.pallas_call(
        functools.partial(_attn_kernel, lam_init, per_round, len(sources)),
        out_shape=jax.ShapeDtypeStruct((n, D_MODEL), bf16),
        scratch_shapes=[pltpu.VMEM((per_round, tile, s_kv), f32), pltpu.VMEM((per_round, tile, s_kv), bf16)],
        grid=(n // tile,),
        in_specs=[pl.BlockSpec((tile, Q_COLS), lambda t: (t, 0))]
        + [pl.BlockSpec((1, 1, rows, KV_COLS), index) for _, rows, index in sources]
        + [pl.BlockSpec((1, 4, DIFF_QK), lambda t: (layer, 0, 0)), pl.BlockSpec((1, 1, DIFF_V), lambda t: (layer, 0, 0))],
        out_specs=pl.BlockSpec((tile, D_MODEL), lambda t: (t, 0)),
        compiler_params=_params(("arbitrary",)),
        name="attention",
    )(q, *[arr for arr, _, _ in sources], w["lam"], w["diff_g"])


def _post_kernel(merge, *refs):
    it = iter(refs)
    o_ref, x_ref, mod_ref, w_out_ref, g2_ref, wg_ref, bg_ref, we_ref, be_ref = (next(it) for _ in range(9))
    if merge:
        next(it), next(it), next(it)
    x1_ref, hx_ref, cls_ref, w_scr = (next(it) for _ in range(4))

    @pl.when(pl.program_id(0) == 0)
    def _():
        w_scr[...] = w_out_ref[0].astype(bf16)

    gate1 = mod_ref[0, 0, 2:3, :]
    shift2 = mod_ref[0, 0, 3:4, :]
    scale2 = mod_ref[0, 0, 4:5, :]
    x1 = x_ref[...] + gate1 * _dot(o_ref[...], w_scr[...])
    x1_ref[...] = x1
    h2 = ((_rms(x1, D_MODEL) * g2_ref[0]) * (1.0 + scale2) + shift2).astype(bf16)
    bits = pltpu.bitcast(h2.astype(f32), i32)
    hx_ref[:, 0:HX_HALF] = bits[:, 0:HX_HALF] | jax.lax.shift_right_logical(bits[:, HX_HALF:D_MODEL], 16)

    def first_lane(mask, lane_f):
        return jnp.min(jnp.where(mask, lane_f, float(LANES)), axis=-1, keepdims=True)

    gl = _dot(h2, wg_ref[0].astype(bf16)) + bg_ref[0]
    glane = jax.lax.broadcasted_iota(i32, gl.shape, 1).astype(f32)
    ge = jnp.exp(gl - jnp.max(gl, axis=-1, keepdims=True))
    gprob = ge / jnp.sum(ge, axis=-1, keepdims=True)
    g_top = jnp.max(gprob, axis=-1, keepdims=True)
    g_idx = first_lane(gprob == g_top, glane)

    el = _dot(h2, we_ref[0].astype(bf16)) + be_ref[0]
    lane = jax.lax.broadcasted_iota(i32, el.shape, 1)
    lane_f = lane.astype(f32)
    emask = (lane >> 2).astype(f32) == g_idx
    em = jnp.where(emask, el, -jnp.inf)
    ee = jnp.where(emask, jnp.exp(em - jnp.max(em, axis=-1, keepdims=True)), 0.0)
    ep = ee / jnp.sum(ee, axis=-1, keepdims=True)
    p1 = jnp.max(jnp.where(emask, ep, -1.0), axis=-1, keepdims=True)
    i1 = first_lane(emask & (ep == p1), lane_f)
    rest = emask & (lane_f != i1)
    p2 = jnp.max(jnp.where(rest, ep, -1.0), axis=-1, keepdims=True)
    i2 = first_lane(rest & (ep == p2), lane_f)
    tot = p1 + p2
    w1 = g_top * (p1 / tot)
    w2 = g_top * (p2 / tot)

    lo = jnp.minimum(i1, i2) - EXPERTS_PER_GROUP * g_idx
    hi = jnp.maximum(i1, i2) - EXPERTS_PER_GROUP * g_idx
    pair = jnp.where(lo == 0.0, hi - 1.0, jnp.where(lo == 1.0, jnp.where(hi == 3.0, 3.0, 4.0), 5.0))
    cls_ref[...] = (N_PAIRS * g_idx + pair).astype(i32)
    g_lo = jnp.where(i1 < i2, w1, w2)
    g_hi = jnp.where(i1 < i2, w2, w1)
    tail_lane = jax.lax.broadcasted_iota(i32, (h2.shape[0], LANES), 1)
    hx_ref[:, HX_HALF:HX_COLS] = pltpu.bitcast(
        jnp.where(tail_lane == 0, g_lo, jnp.where(tail_lane == 1, g_hi, 0.0)), i32)


def _post(layer, o, x, row0, total, mod, mod_row, w, tile, merged=None):
    n = o.shape[0]
    lay = lambda t: (layer, 0, 0)
    row = lambda t: (t, 0)
    off = row0 // tile
    out_row = lambda t: (off + t, 0)
    in_specs = [
        pl.BlockSpec((tile, D_MODEL), row),
        pl.BlockSpec((tile, D_MODEL), row),
        pl.BlockSpec((1, 1, N_MOD, D_MODEL), lambda t: (layer, mod_row(t * tile), 0, 0)),
        pl.BlockSpec((1, D_MODEL, D_MODEL), lay),
        pl.BlockSpec((1, 1, D_MODEL), lay),
        pl.BlockSpec((1, D_MODEL, N_GROUPS), lay),
        pl.BlockSpec((1, 1, N_GROUPS), lay),
        pl.BlockSpec((1, D_MODEL, N_EXPERTS), lay),
        pl.BlockSpec((1, 1, N_EXPERTS), lay),
    ]
    args = [o, x, mod, w["w_out"], w["g2"], w["w_grp"], w["b_grp"], w["w_rtr"], w["b_rtr"]]
    aliases = {}
    if merged is not None:
        aliases = {len(args) + j: j for j in range(3)}
        in_specs += [pl.BlockSpec(memory_space=pl.ANY)] * 3
        args += list(merged)
    return pl.pallas_call(
        functools.partial(_post_kernel, merged is not None),
        out_shape=[jax.ShapeDtypeStruct((total, D_MODEL), f32), jax.ShapeDtypeStruct((total, HX_COLS), i32),
                   jax.ShapeDtypeStruct((total, 1), i32)],
        grid=(n // tile,),
        in_specs=in_specs,
        out_specs=[pl.BlockSpec((tile, D_MODEL), out_row), pl.BlockSpec((tile, HX_COLS), out_row),
                   pl.BlockSpec((tile, 1), out_row)],
        scratch_shapes=[pltpu.VMEM((D_MODEL, D_MODEL), bf16)],
        input_output_aliases=aliases,
        compiler_params=_params(("arbitrary",)),
        name="post_attention",
    )(*args)


def _ctx_kernel(lam_init, n_prev, resid, *refs):
    it = iter(refs)
    pre_in = [next(it) for _ in range(9 + (2 if resid else 0) + n_prev)]
    lam_ref, dg_ref = next(it), next(it)
    post_w = [next(it) for _ in range(6)]
    cache_refs = [next(it) for _ in range(len(CACHE_WIDTHS))]
    x1_ref, hx_ref, cls_ref = (next(it) for _ in range(3))
    w_in_scr, q_scr, kv_scr, o_scr, s_scr, p_scr, w_out_scr, x2_scr = (next(it) for _ in range(8))
    x_ref, mod_ref = pre_in[0], pre_in[1]
    x2_ref = [x2_scr] if resid else []

    _pre_kernel(False, n_prev, resid, *pre_in, q_scr, kv_scr.at[0, 0], *x2_ref, *cache_refs, w_in_scr)
    _attn_kernel(lam_init, len(_SCORE_HEADS), 1, q_scr, kv_scr, lam_ref, dg_ref, o_scr, s_scr, p_scr)
    _post_kernel(False, o_scr, x2_scr if resid else x_ref, mod_ref, *post_w, x1_ref, hx_ref, cls_ref, w_out_scr)


def _ctx_layer(layer, x, n, seq, total, mod, w, prev_cache=(), resid=None):
    lay = lambda t: (layer, 0, 0)
    row = lambda t: (t, 0)
    mod_spec = lambda l: pl.BlockSpec((1, 1, N_MOD, D_MODEL), lambda t: (l, 0, 0, 0))
    in_specs = [
        pl.BlockSpec((seq, D_MODEL), row), mod_spec(layer),
        pl.BlockSpec((1, 1, D_MODEL), lay), pl.BlockSpec((1, IN_COLS, D_MODEL), lay),
        pl.BlockSpec((1, 1, KV_RANK), lay), pl.BlockSpec((1, 1, 512), lay), pl.BlockSpec((512, 512), lambda t: (0, 0)),
        pl.BlockSpec((1, 256, 384), lay), pl.BlockSpec((1, KV_RANK, 384), lay),
    ]
    args = [x, mod, w["g1"], w["w_in"], w["kv_g"], w["qk_g"], w["seg"], w["w_ka"], w["w_uv"]]
    if resid is not None:
        in_specs += [pl.BlockSpec((seq, D_MODEL), row), mod_spec(layer - 1)]
        args += [resid, mod]
    out_shape, out_specs, aliases = [], [], {}
    for j, width in enumerate(CACHE_WIDTHS):
        if prev_cache:
            aliases[len(args)] = len(out_shape)
            in_specs.append(pl.BlockSpec(memory_space=pl.ANY))
            args.append(prev_cache[j])
        out_shape.append(jax.ShapeDtypeStruct((n // seq, DEPTH, seq, width), f32))
        out_specs.append(pl.BlockSpec((1, 1, seq, width), lambda t: (t, layer, 0, 0)))
    in_specs += [
        pl.BlockSpec((1, 4, DIFF_QK), lay), pl.BlockSpec((1, 1, DIFF_V), lay),
        pl.BlockSpec((1, D_MODEL, D_MODEL), lay), pl.BlockSpec((1, 1, D_MODEL), lay),
        pl.BlockSpec((1, D_MODEL, N_GROUPS), lay), pl.BlockSpec((1, 1, N_GROUPS), lay),
        pl.BlockSpec((1, D_MODEL, N_EXPERTS), lay), pl.BlockSpec((1, 1, N_EXPERTS), lay),
    ]
    args += [w["lam"], w["diff_g"], w["w_out"], w["g2"], w["w_grp"], w["b_grp"], w["w_rtr"], w["b_rtr"]]
    out_shape += [jax.ShapeDtypeStruct((total, D_MODEL), f32), jax.ShapeDtypeStruct((total, HX_COLS), i32),
                  jax.ShapeDtypeStruct((total, 1), i32)]
    out_specs += [pl.BlockSpec((seq, D_MODEL), row), pl.BlockSpec((seq, HX_COLS), row), pl.BlockSpec((seq, 1), row)]
    heads = len(_SCORE_HEADS)
    return pl.pallas_call(
        functools.partial(_ctx_kernel, 0.8 - 0.6 * math.exp(-0.3 * layer), len(prev_cache), resid is not None),
        out_shape=out_shape,
        grid=(n // seq,),
        in_specs=in_specs,
        out_specs=out_specs,
        scratch_shapes=[pltpu.VMEM((Z_COLS, D_MODEL), bf16), pltpu.VMEM((seq, Q_COLS), bf16),
                        pltpu.VMEM((1, 1, seq, KV_COLS), bf16), pltpu.VMEM((seq, D_MODEL), bf16),
                        pltpu.VMEM((heads, seq, seq), f32), pltpu.VMEM((heads, seq, seq), bf16),
                        pltpu.VMEM((D_MODEL, D_MODEL), bf16), pltpu.VMEM((seq, D_MODEL), f32)],
        input_output_aliases=aliases,
        compiler_params=_params(("arbitrary",)),
        name="context_layer",
    )(*args)


PLAN_CHUNK = 1024
TAB_ROWS = LANES


def _plan_kernel(cls_ref, slot_ref, tab_ref, rank_scr):
    n = cls_ref.shape[0]
    lane = jax.lax.broadcasted_iota(i32, (PLAN_CHUNK, LANES), 1)
    r = jax.lax.broadcasted_iota(i32, (PLAN_CHUNK, PLAN_CHUNK), 0)
    c = jax.lax.broadcasted_iota(i32, (PLAN_CHUNK, PLAN_CHUNK), 1)
    before = (c < r).astype(bf16)

    def count(b, seen):
        rows = pl.ds(pl.multiple_of(b * PLAN_CHUNK, PLAN_CHUNK), PLAN_CHUNK)
        onehot = (cls_ref[rows, :] == lane).astype(f32)
        ahead = _dot(before, onehot.astype(bf16)) + seen
        rank_scr[rows, :] = jnp.sum(onehot * ahead, axis=-1, keepdims=True)
        return seen + jnp.sum(onehot, axis=0, keepdims=True)

    counts = jax.lax.fori_loop(0, n // PLAN_CHUNK, count, jnp.zeros((1, LANES), f32))
    tiles = jnp.floor((counts + (EXPERT_TILE - 1)) * (1.0 / EXPERT_TILE))
    rr = jax.lax.broadcasted_iota(i32, (LANES, LANES), 0)
    cc = jax.lax.broadcasted_iota(i32, (LANES, LANES), 1)
    ends = _dot(jnp.broadcast_to(tiles, (8, LANES)).astype(bf16), (rr <= cc).astype(bf16))[0:1]
    starts = ends - tiles

    def place(b, carry):
        rows = pl.ds(pl.multiple_of(b * PLAN_CHUNK, PLAN_CHUNK), PLAN_CHUNK)
        first = jnp.sum(jnp.where(cls_ref[rows, :] == lane, starts, 0.0), axis=-1, keepdims=True)
        slot_ref[rows, :] = (first * EXPERT_TILE + rank_scr[rows, :]).astype(i32)
        return carry

    jax.lax.fori_loop(0, n // PLAN_CHUNK, place, 0)

    tl = jax.lax.broadcasted_iota(i32, (TAB_ROWS, LANES), 1)
    n_tiles = jnp.sum(jnp.where(tl[0:1] == N_CLASSES - 1, ends, 0.0), axis=-1, keepdims=True)
    k = jnp.minimum(jax.lax.broadcasted_iota(i32, (TAB_ROWS, 1), 0).astype(f32), n_tiles - 1.0)
    cls_k = jnp.sum(jnp.where((tl < N_CLASSES) & (ends <= k), 1.0, 0.0), axis=-1, keepdims=True)
    cls_k = jnp.minimum(cls_k, N_CLASSES - 1.0)
    mine = tl.astype(f32) == cls_k
    used = jnp.sum(jnp.where(mine, counts, 0.0), axis=-1, keepdims=True)
    first = jnp.sum(jnp.where(mine, starts, 0.0), axis=-1, keepdims=True)
    valid = jnp.clip(used - (k - first) * EXPERT_TILE, 0.0, float(EXPERT_TILE))
    group = jnp.floor((cls_k + 0.5) * (1.0 / N_PAIRS))
    pair = cls_k - N_PAIRS * group
    lo = hi = jnp.zeros_like(pair)
    for p in range(N_PAIRS):
        lo = jnp.where(pair == p, float(PAIR_LO[p]), lo)
        hi = jnp.where(pair == p, float(PAIR_HI[p]), hi)
    e_lo = EXPERTS_PER_GROUP * group + lo
    e_hi = EXPERTS_PER_GROUP * group + hi
    tab = jnp.where(tl == 0, e_lo, jnp.where(tl == 1, e_hi, jnp.where(tl == 2, valid, jnp.where(tl == 3, n_tiles, 0.0))))
    tab_ref[...] = tab.astype(i32)


def _plan(cls, max_tiles):
    n = cls.shape[0]
    assert n % PLAN_CHUNK == 0 and max_tiles <= TAB_ROWS
    slot, tab = pl.pallas_call(
        _plan_kernel,
        out_shape=[jax.ShapeDtypeStruct((n, 1), i32), jax.ShapeDtypeStruct((TAB_ROWS, LANES), i32)],
        scratch_shapes=[pltpu.VMEM((n, 1), f32)],
        compiler_params=_params(None),
        name="dispatch_plan",
    )(cls)
    return slot.reshape(n), tab[:max_tiles, 0], tab[:max_tiles, 1], tab[:max_tiles, 2], tab[0, 3:4]


def _move_rows(src, idx, n_out, scatter):
    n = idx.shape[0]
    width = src.shape[1]
    per_worker = n // (SC_CORES * SC_SUBCORES)
    assert n % SC_ROWS == 0
    chunk = max(c for c in (64, 40, 32, 16, 8)
                if per_worker % c == 0 and 2 * c * width * src.dtype.itemsize <= SC_BUFFER_BYTES)
    n_chunks = per_worker // chunk
    mesh = plsc.VectorSubcoreMesh(core_axis_name="c", subcore_axis_name="s")

    @functools.partial(
        pl.kernel, mesh=mesh, out_type=jax.ShapeDtypeStruct((n_out, width), src.dtype),
        scratch_types=[pltpu.VMEM((chunk,), i32), pltpu.VMEM((chunk,), i32),
                       pltpu.VMEM((chunk, width), src.dtype), pltpu.VMEM((chunk, width), src.dtype),
                       pltpu.SemaphoreType.DMA, pltpu.SemaphoreType.DMA, pltpu.SemaphoreType.DMA,
                       pltpu.SemaphoreType.DMA])
    def move(src_hbm, idx_hbm, out_hbm, idx0, idx1, rows0, rows1, in0, in1, out0, out1):
        wid = jax.lax.axis_index("s") * SC_CORES + jax.lax.axis_index("c")
        base = wid * per_worker
        idx_v, rows_v, sem_in, sem_out = (idx0, idx1), (rows0, rows1), (in0, in1), (out0, out1)

        def fill(j):
            b = j % 2
            rows = pl.ds(base + j * chunk, chunk)
            pltpu.sync_copy(idx_hbm.at[rows], idx_v[b])
            src_rows = src_hbm.at[rows] if scatter else src_hbm.at[idx_v[b]]
            return pltpu.async_copy(src_rows, rows_v[b], sem_in[b])

        def drain(j):
            b = j % 2
            dst_rows = out_hbm.at[idx_v[b]] if scatter else out_hbm.at[pl.ds(base + j * chunk, chunk)]
            return pltpu.async_copy(rows_v[b], dst_rows, sem_out[b])

        fills, drains = {0: fill(0)}, {}
        for j in range(n_chunks):
            if j + 1 < n_chunks:
                if j >= 1:
                    drains[j - 1].wait()
                fills[j + 1] = fill(j + 1)
            fills[j].wait()
            drains[j] = drain(j)
        for j in range(max(n_chunks - 2, 0), n_chunks):
            drains[j].wait()

    return move(src, idx)


def _expert_kernel(lo_ref, hi_ref, valid_ref, nt_ref, xs_ref, wg_ref, wu_ref, wd_ref, ys_ref, wg_scr, wu_scr, wd_scr):
    k = pl.program_id(0)
    prev = jnp.maximum(k - 1, 0)

    @pl.when((k == 0) | (lo_ref[k] // EXPERTS_PER_GROUP != lo_ref[prev] // EXPERTS_PER_GROUP))
    def _():
        for j in range(EXPERTS_PER_GROUP):
            wg_scr[j] = wg_ref[0, 0, j].astype(bf16)
            wu_scr[j] = wu_ref[0, 0, j].astype(bf16)
            wd_scr[j] = wd_ref[0, 0, j].astype(bf16)

    @pl.when(k < nt_ref[0])
    def _():
        live = jax.lax.broadcasted_iota(i32, (EXPERT_TILE, 1), 0) < valid_ref[k]
        words = jnp.where(live, xs_ref[:, 0:HX_HALF], 0)
        x = jnp.concatenate([pltpu.bitcast(words & -65536, f32), pltpu.bitcast(words << 16, f32)], axis=1).astype(bf16)
        gates = pltpu.bitcast(jnp.where(live, xs_ref[:, HX_HALF:HX_COLS], 0), f32)
        y = None
        for lane, e_ref in enumerate((lo_ref, hi_ref)):
            j = e_ref[k] % EXPERTS_PER_GROUP
            hid = _silu(_dot(x, wg_scr[j])) * _dot(x, wu_scr[j]) * gates[:, lane:lane + 1]
            part = _dot(hid.astype(bf16), wd_scr[j])
            y = part if y is None else y + part
        ys_ref[...] = y


def _experts(layer, xs, e_lo, e_hi, valid, n_tiles, w):
    max_tiles = e_lo.shape[0]
    row = lambda k, lo, hi, valid, nt: (jnp.minimum(k, nt[0] - 1), 0)
    group = lambda k, lo, hi, valid, nt: (layer, lo[k] // EXPERTS_PER_GROUP, 0, 0, 0)
    by_group = lambda a: a.reshape(DEPTH, N_GROUPS, EXPERTS_PER_GROUP, *a.shape[2:])
    return pl.pallas_call(
        _expert_kernel,
        out_shape=jax.ShapeDtypeStruct((max_tiles * EXPERT_TILE, D_MODEL), f32),
        grid_spec=pltpu.PrefetchScalarGridSpec(
            num_scalar_prefetch=4, grid=(max_tiles,),
            in_specs=[pl.BlockSpec((EXPERT_TILE, HX_COLS), row),
                      pl.BlockSpec((1, 1, EXPERTS_PER_GROUP, D_MODEL, D_FF_EXPERT), group),
                      pl.BlockSpec((1, 1, EXPERTS_PER_GROUP, D_MODEL, D_FF_EXPERT), group),
                      pl.BlockSpec((1, 1, EXPERTS_PER_GROUP, D_FF_EXPERT, D_MODEL), group)],
            out_specs=pl.BlockSpec((EXPERT_TILE, D_MODEL), row),
            scratch_shapes=[pltpu.VMEM((EXPERTS_PER_GROUP, D_MODEL, D_FF_EXPERT), bf16),
                            pltpu.VMEM((EXPERTS_PER_GROUP, D_MODEL, D_FF_EXPERT), bf16),
                            pltpu.VMEM((EXPERTS_PER_GROUP, D_FF_EXPERT, D_MODEL), bf16)]),
        compiler_params=_params(("arbitrary",)),
        name="experts",
    )(e_lo, e_hi, valid, n_tiles, xs, by_group(w["w_gate"]), by_group(w["w_up"]), by_group(w["w_down"]))


def _final_kernel(x1_ref, y_ref, mod_ref, fg_ref, o_ref):
    o_ref[...] = _rms(x1_ref[...] + mod_ref[0, 0, 5:6, :] * y_ref[...], D_MODEL) * fg_ref[...]


def _final(x1, y, n, row0, mod, mod_row, w, tile):
    off = row0 // tile
    src_row = lambda t: (off + t, 0)
    return pl.pallas_call(
        _final_kernel,
        out_shape=jax.ShapeDtypeStruct((n, D_MODEL), f32),
        grid=(n // tile,),
        in_specs=[pl.BlockSpec((tile, D_MODEL), src_row), pl.BlockSpec((tile, D_MODEL), src_row),
                  pl.BlockSpec((1, 1, N_MOD, D_MODEL), lambda t: (DEPTH - 1, mod_row(t * tile), 0, 0)),
                  pl.BlockSpec((1, D_MODEL), lambda t: (0, 0))],
        out_specs=pl.BlockSpec((tile, D_MODEL), lambda t: (t, 0)),
        compiler_params=_params(("arbitrary",)),
        name="final_norm",
    )(x1, y, mod, w["final_g"])


def _rope_tables(n_tokens):
    pos = np.arange(n_tokens)
    row = (pos // GRID_W).astype(np.float64)
    col = (pos % GRID_W).astype(np.float64)

    def cs(rot_dim):
        quarter = rot_dim // 4
        inv = ROPE_THETA ** (-np.arange(quarter, dtype=np.float64) / quarter)
        ang = np.concatenate([row[:, None] * inv, col[:, None] * inv], axis=-1)
        return np.cos(ang), np.sin(ang)

    c32, s32 = cs(MLA_ROPE)
    c64, s64 = cs(HEAD_DIM)
    ones = np.ones((n_tokens, MLA_NOPE))
    zeros = np.zeros((n_tokens, MLA_NOPE))

    def rep(parts):
        period = np.concatenate(parts, axis=-1)
        return jnp.asarray(np.tile(period, (1, LANES // period.shape[-1])), f32)

    return (rep([ones, c32, c32]), rep([zeros, -s32, s32]), rep([c32, c32]), rep([-s32, s32]),
            rep([c64, c64]), rep([-s64, s64]))


def _layout_weights(norm1_g, norm2_g, w_in, mla_kv_norm_g, mla_w_uk, mla_w_uv, gqa_q_norm_g, gqa_k_norm_g,
                    diff_lambda, diff_norm_g, w_out, moe_w_group, moe_b_group, moe_w_router, moe_b_router,
                    moe_w_gate, moe_w_up, moe_w_down, final_norm_g):
    eye = jnp.eye(MLA_ROPE, dtype=f32)
    top = jnp.concatenate([mla_w_uk, jnp.zeros((DEPTH, KV_RANK, MLA_HEADS, MLA_ROPE), f32)], axis=-1)
    mid = jnp.concatenate([jnp.zeros((MLA_ROPE, MLA_HEADS, MLA_NOPE), f32),
                           jnp.broadcast_to(eye[:, None, :], (MLA_ROPE, MLA_HEADS, MLA_ROPE))], axis=-1)
    w_ka = jnp.concatenate([top.reshape(DEPTH, KV_RANK, 384),
                            jnp.broadcast_to(mid.reshape(1, MLA_ROPE, 384), (DEPTH, MLA_ROPE, 384)),
                            jnp.zeros((DEPTH, 256 - KV_RANK - MLA_ROPE, 384), f32)], axis=1).astype(bf16)
    seg_id = np.arange(512) // HEAD_DIM
    seg = jnp.asarray(seg_id[:, None] == seg_id[None, :], bf16)
    qk_g = jnp.concatenate([jnp.tile(gqa_q_norm_g, (1, GQA_HEADS)), jnp.tile(gqa_k_norm_g, (1, GQA_KV_HEADS))], axis=-1)
    return dict(
        g1=norm1_g.reshape(DEPTH, 1, D_MODEL), g2=norm2_g.reshape(DEPTH, 1, D_MODEL),
        w_in=jnp.swapaxes(w_in, 1, 2),
        kv_g=mla_kv_norm_g.reshape(DEPTH, 1, KV_RANK), qk_g=qk_g.reshape(DEPTH, 1, 512), seg=seg, w_ka=w_ka,
        w_uv=mla_w_uv.reshape(DEPTH, KV_RANK, 384), lam=diff_lambda, diff_g=diff_norm_g.reshape(DEPTH, 1, DIFF_V),
        w_out=w_out, w_grp=moe_w_group, b_grp=moe_b_group.reshape(DEPTH, 1, N_GROUPS), w_rtr=moe_w_router,
        b_rtr=moe_b_router.reshape(DEPTH, 1, N_EXPERTS), w_gate=moe_w_gate, w_up=moe_w_up, w_down=moe_w_down,
        final_g=final_norm_g.reshape(1, D_MODEL))


PRE_TILE = 512
CTX_ATTN_TILE = 256
LAT_ATTN_TILE = 256
CTX_HEADS_PER_ROUND = 20
LAT_HEADS_PER_ROUND = 5
MXU_SUM_MIN_KEYS = 1024
POST_TILE = 512
FINAL_TILE = 1024


def kernel(x_prompt, x_sample, c, cache_mla_ckv, cache_mla_krope, cache_gqa_k, cache_gqa_v, cache_diff_k, cache_diff_v, c_ctx, norm1_g, norm2_g, w_mod, b_mod, w_in, mla_kv_norm_g, mla_w_uk, mla_w_uv, gqa_q_norm_g, gqa_k_norm_g, diff_lambda, diff_norm_g, w_out, moe_w_group, moe_b_group, moe_w_router, moe_b_router, moe_w_gate, moe_w_up, moe_w_down, final_norm_g):
    B, S, _ = x_prompt.shape
    Bl, Sl, _ = x_sample.shape
    n_ctx, n_lat = B * S, Bl * Sl
    total = n_ctx + n_lat
    assert S == CTX_ATTN_TILE and Bl + 1 <= MOD_ROWS and DEPTH == 2 and total % SC_ROWS == 0
    slot_rows = -(-(total + N_CLASSES * EXPERT_TILE) // SC_ROWS) * SC_ROWS
    max_tiles = slot_rows // EXPERT_TILE
    w = _layout_weights(norm1_g, norm2_g, w_in, mla_kv_norm_g, mla_w_uk, mla_w_uv, gqa_q_norm_g, gqa_k_norm_g,
                        diff_lambda, diff_norm_g, w_out, moe_w_group, moe_b_group, moe_w_router, moe_b_router,
                        moe_w_gate, moe_w_up, moe_w_down, final_norm_g)
    cond = jnp.concatenate([c_ctx[None, :], c, jnp.zeros((MOD_ROWS - 1 - Bl, D_MODEL), f32)], axis=0)
    mod = _modulation(cond, w_mod, b_mod).reshape(DEPTH, MOD_ROWS, N_MOD, D_MODEL)
    ctx_row = lambda token: 0
    lat_row = lambda token: 1 + token // Sl
    tabs = _rope_tables(Sl)
    kv_past = _cache_rows((cache_mla_ckv, cache_mla_krope, cache_gqa_k, cache_gqa_v, cache_diff_k, cache_diff_v), w)
    per_b = Sl // LAT_ATTN_TILE

    x_ctx, x_lat = x_prompt.reshape(n_ctx, D_MODEL), x_sample.reshape(n_lat, D_MODEL)
    cache = ()
    x1 = y = None
    for i in range(DEPTH):
        if i == 0:
            *cache, x1_c, hx_c, cls_c = _ctx_layer(i, x_ctx, n_ctx, S, total, mod, w)
        else:
            *cache, x1_c, hx_c, cls_c = _ctx_layer(i, x1, n_ctx, S, total, mod, w, prev_cache=cache, resid=y)
        if i == 0:
            q_l, kv_l = _pre(i, x_lat, n_lat, 0, Sl, mod, lat_row, w, PRE_TILE, rope_tabs=tabs)
        else:
            q_l, kv_l, x_lat = _pre(i, x1, n_lat, n_ctx, Sl, mod, lat_row, w, PRE_TILE, rope_tabs=tabs, resid=y)
        past = (kv_past, PAST_LEN, lambda t, i=i: (i, t // per_b, 0, 0))
        own = (kv_l.reshape(1, Bl, Sl, KV_COLS), Sl, lambda t: (0, t // per_b, 0, 0))
        o_l = _attention(i, q_l, [past, own], w, LAT_ATTN_TILE, LAT_HEADS_PER_ROUND)
        x1, hx, cls = _post(i, o_l, x_lat, n_ctx, total, mod, lat_row, w, POST_TILE, merged=(x1_c, hx_c, cls_c))
        slot, e_lo, e_hi, valid, n_tiles = _plan(cls, max_tiles)
        xs = _move_rows(hx, slot, slot_rows, scatter=True)
        ys = _experts(i, xs, e_lo, e_hi, valid, n_tiles, w)
        y = _move_rows(ys, slot, total, scatter=False)

    y_prompt = _final(x1, y, n_ctx, 0, mod, ctx_row, w, FINAL_TILE).reshape(B, S, D_MODEL)
    y_sample = _final(x1, y, n_lat, n_ctx, mod, lat_row, w, FINAL_TILE).reshape(Bl, Sl, D_MODEL)
    new_mla_ckv, new_mla_krope = cache[0], cache[1]
    new_gqa_k = cache[2].reshape(B, DEPTH, S, GQA_KV_HEADS, HEAD_DIM)
    new_gqa_v = cache[3].reshape(B, DEPTH, S, GQA_KV_HEADS, HEAD_DIM)
    new_diff_k = cache[4].reshape(B, DEPTH, S, DIFF_HEADS, 2, DIFF_QK)
    new_diff_v = cache[5].reshape(B, DEPTH, S, DIFF_HEADS, DIFF_V)
    return (y_prompt, y_sample, new_mla_ckv, new_mla_krope, new_gqa_k, new_gqa_v, new_diff_k, new_diff_v)
```

```python
import functools
import math

import jax
import jax.numpy as jnp
import numpy as np
from jax.experimental import pallas as pl
from jax.experimental.pallas import tpu as pltpu
from jax.experimental.pallas import tpu_sc as plsc

D_MODEL = 1024
DEPTH = 2
PAST_LEN = 512
GRID_W = 64
ROPE_THETA = 10000.0
EPS = 1e-6
LOG2E = 1.4426950408889634
N_MOD = 6
HEAD_DIM = 64
MLA_HEADS = 6
MLA_NOPE = 32
MLA_ROPE = 32
MLA_V = 64
KV_RANK = 128
GQA_HEADS = 6
GQA_KV_HEADS = 2
GQA_GROUP = GQA_HEADS // GQA_KV_HEADS
DIFF_HEADS = 4
DIFF_QK = 32
DIFF_V = 64
N_GROUPS = 4
EXPERTS_PER_GROUP = 4
N_EXPERTS = N_GROUPS * EXPERTS_PER_GROUP
D_FF_EXPERT = 256

LANES = 128
MOD_ROWS = 8

IN_COLS = 1952
IN_KR = 512
Z_QA, Z_CKV, Z_QG, Z_KG, Z_VG, Z_QD, Z_KD, Z_VD, Z_KR = 0, 384, 512, 896, 1024, 1152, 1408, 1664, 1920
Z_COLS = 2048
Q_A, Q_G, Q_D, Q_COLS = 0, 384, 768, 1024
KV_KA, KV_VA, KV_KG, KV_VG, KV_KD, KV_VD, KV_COLS = 0, 384, 768, 896, 1024, 1280, 1536
CACHE_WIDTHS = (128, 32, 128, 128, 256, 256)

PAIR_LO = (0, 0, 0, 1, 1, 2)
PAIR_HI = (1, 2, 3, 3, 2, 3)
N_PAIRS = len(PAIR_LO)
N_CLASSES = N_GROUPS * N_PAIRS
HX_HALF = D_MODEL // 2
HX_COLS = HX_HALF + LANES

SC_CORES, SC_SUBCORES = 2, 16
SC_BUFFER_BYTES = 400 * 1024
SC_ROWS = SC_CORES * SC_SUBCORES * 8

VMEM_LIMIT = 56 * 1024 * 1024

bf16 = jnp.bfloat16
f32 = jnp.float32
i32 = jnp.int32


def _dot(a, b):
    return jnp.dot(a, b, preferred_element_type=f32)


def _dot_nt(a, b):
    return jax.lax.dot_general(a, b, (((1,), (1,)), ((), ())), preferred_element_type=f32)


def _rms(x, width):
    return x * jax.lax.rsqrt(jnp.sum(x * x, axis=-1, keepdims=True) * (1.0 / width) + EPS)


def _silu(x):
    return x * (1.0 / (1.0 + jnp.exp(-x)))


def _params(sem):
    return pltpu.CompilerParams(dimension_semantics=sem, vmem_limit_bytes=VMEM_LIMIT)


def _mod_kernel(cond_ref, w_ref, b_ref, o_ref):
    o_ref[0] = _dot(_silu(cond_ref[...]).astype(bf16), w_ref[0].astype(bf16)) + b_ref[0]


def _modulation(cond, w_mod, b_mod):
    return pl.pallas_call(
        _mod_kernel,
        out_shape=jax.ShapeDtypeStruct((DEPTH, MOD_ROWS, N_MOD * D_MODEL), f32),
        grid=(DEPTH, N_MOD),
        in_specs=[
            pl.BlockSpec((MOD_ROWS, D_MODEL), lambda i, j: (0, 0)),
            pl.BlockSpec((1, D_MODEL, D_MODEL), lambda i, j: (i, 0, j)),
            pl.BlockSpec((1, 1, D_MODEL), lambda i, j: (i, 0, j)),
        ],
        out_specs=pl.BlockSpec((1, MOD_ROWS, D_MODEL), lambda i, j: (i, 0, j)),
        compiler_params=_params(("arbitrary", "arbitrary")),
        name="modulation",
    )(cond, w_mod, b_mod.reshape(DEPTH, 1, N_MOD * D_MODEL))


def _swap_halves(x, half):
    lane = jax.lax.broadcasted_iota(i32, x.shape, 1)
    fwd = pltpu.roll(x, LANES - half, 1)
    bwd = pltpu.roll(x, half, 1)
    return jnp.where((lane & (2 * half - 1)) < half, fwd, bwd)


def _rope_block(x, cos, sin, half):
    return x * cos + _swap_halves(x, half) * sin


def _pre_kernel(rope, n_prev, resid, *refs):
    it = iter(refs)
    x_ref, mod_ref, g1_ref, w_in_ref, kvg_ref, qkg_ref, seg_ref, wka_ref, wuv_ref = (next(it) for _ in range(9))
    if resid:
        y_ref, pmod_ref = next(it), next(it)
    if rope:
        ca_ref, sa_ref, c32_ref, s32_ref, c64_ref, s64_ref = (next(it) for _ in range(6))
    prev_refs = [next(it) for _ in range(n_prev)]
    q_ref, kv_ref = next(it), next(it)
    if resid:
        x2_ref = next(it)
    cache_refs = [] if rope else [next(it) for _ in range(len(CACHE_WIDTHS))]
    w_scr = next(it)

    @pl.when(pl.program_id(0) == 0)
    def _():
        w_scr[0:IN_KR] = w_in_ref[0, 0:IN_KR].astype(bf16)
        w_scr[IN_KR:Z_KR] = w_in_ref[0, IN_KR + MLA_ROPE:IN_COLS].astype(bf16)
        w_scr[Z_KR:Z_KR + MLA_ROPE] = w_in_ref[0, IN_KR:IN_KR + MLA_ROPE].astype(bf16)
        w_scr[Z_KR + MLA_ROPE:Z_COLS] = jnp.zeros((Z_COLS - Z_KR - MLA_ROPE, D_MODEL), bf16)

    x = x_ref[...]
    if resid:
        x = x + pmod_ref[0, 0, 5:6, :] * y_ref[...]
        x2_ref[...] = x
    shift1 = mod_ref[0, 0, 0:1, :]
    scale1 = mod_ref[0, 0, 1:2, :]
    h = (_rms(x, D_MODEL) * g1_ref[0]) * (1.0 + scale1) + shift1
    z = _dot_nt(h.astype(bf16), w_scr[...])

    ckv = _rms(z[:, Z_CKV:Z_CKV + KV_RANK], KV_RANK) * kvg_ref[0]

    qk = z[:, Z_QG:Z_VG]
    sq = qk * qk
    sq_hi = sq.astype(bf16)
    sq_lo = (sq - sq_hi.astype(f32)).astype(bf16)
    seg = seg_ref[...]
    ms = (_dot(sq_hi, seg) + _dot(sq_lo, seg)) * (1.0 / HEAD_DIM)
    qk = qk * jax.lax.rsqrt(ms + EPS) * qkg_ref[0]

    def blocks(arr, n):
        return [arr[:, LANES * j:LANES * (j + 1)] for j in range(n)]

    qa = blocks(z[:, Z_QA:Z_QA + 384], 3)
    qkb = blocks(qk, 4)
    qd = blocks(z[:, Z_QD:Z_QD + 256], 2)
    kd = blocks(z[:, Z_KD:Z_KD + 256], 2)
    kr = z[:, Z_KR:Z_KR + LANES]
    if rope:
        ca, sa, c32, s32, c64, s64 = (r[...] for r in (ca_ref, sa_ref, c32_ref, s32_ref, c64_ref, s64_ref))
        qa = [_rope_block(b, ca, sa, MLA_ROPE // 2) for b in qa]
        qkb = [_rope_block(b, c64, s64, HEAD_DIM // 2) for b in qkb]
        qd = [_rope_block(b, c32, s32, DIFF_QK // 2) for b in qd]
        kd = [_rope_block(b, c32, s32, DIFF_QK // 2) for b in kd]
        kr = _rope_block(kr, c32, s32, MLA_ROPE // 2)

    vg = z[:, Z_VG:Z_VG + 128]
    vd = z[:, Z_VD:Z_VD + 256]
    ckv_b = ckv.astype(bf16)
    k_a = _dot(jnp.concatenate([ckv_b, kr.astype(bf16)], axis=1), wka_ref[0])
    v_a = _dot(ckv_b, wuv_ref[0].astype(bf16))

    for j in range(3):
        q_ref[:, Q_A + LANES * j:Q_A + LANES * (j + 1)] = (qa[j] * (HEAD_DIM ** -0.5 * LOG2E)).astype(bf16)
        q_ref[:, Q_G + LANES * j:Q_G + LANES * (j + 1)] = (qkb[j] * (HEAD_DIM ** -0.5 * LOG2E)).astype(bf16)
    for j in range(2):
        q_ref[:, Q_D + LANES * j:Q_D + LANES * (j + 1)] = (qd[j] * (DIFF_QK ** -0.5 * LOG2E)).astype(bf16)
        kv_ref[:, KV_KD + LANES * j:KV_KD + LANES * (j + 1)] = kd[j].astype(bf16)
    kv_ref[:, KV_KA:KV_KA + 384] = k_a.astype(bf16)
    kv_ref[:, KV_VA:KV_VA + 384] = v_a.astype(bf16)
    kv_ref[:, KV_KG:KV_KG + 128] = qkb[3].astype(bf16)
    kv_ref[:, KV_VG:KV_VG + 128] = vg.astype(bf16)
    kv_ref[:, KV_VD:KV_VD + 256] = vd.astype(bf16)
    if not rope:
        rows = [ckv, kr[:, :MLA_ROPE], qkb[3], vg, jnp.concatenate(kd, axis=1), vd]
        for out, new in zip(cache_refs, rows):
            reqs, _, seq, width = out.shape
            out[:, 0] = new.reshape(reqs, seq, width)


def _pre(layer, x, n, row0, seq, mod, mod_row, w, tile, rope_tabs=None, prev_cache=(), resid=None):
    rope = rope_tabs is not None
    lay = lambda t: (layer, 0, 0)
    row = lambda t: (t, 0)
    off = row0 // tile
    src_row = lambda t: (off + t, 0)
    in_specs = [
        pl.BlockSpec((tile, D_MODEL), src_row),
        pl.BlockSpec((1, 1, N_MOD, D_MODEL), lambda t: (layer, mod_row(t * tile), 0, 0)),
        pl.BlockSpec((1, 1, D_MODEL), lay),
        pl.BlockSpec((1, IN_COLS, D_MODEL), lay),
        pl.BlockSpec((1, 1, KV_RANK), lay),
        pl.BlockSpec((1, 1, 512), lay),
        pl.BlockSpec((512, 512), lambda t: (0, 0)),
        pl.BlockSpec((1, 256, 384), lay),
        pl.BlockSpec((1, KV_RANK, 384), lay),
    ]
    args = [x, mod, w["g1"], w["w_in"], w["kv_g"], w["qk_g"], w["seg"], w["w_ka"], w["w_uv"]]
    if resid is not None:
        in_specs += [pl.BlockSpec((tile, D_MODEL), src_row),
                     pl.BlockSpec((1, 1, N_MOD, D_MODEL), lambda t: (layer - 1, mod_row(t * tile), 0, 0))]
        args += [resid, mod]
    if rope:
        per_b = seq // tile
        in_specs += [pl.BlockSpec((tile, LANES), lambda t: (t % per_b, 0))] * 6
        args += list(rope_tabs)
    out_shape = [jax.ShapeDtypeStruct((n, Q_COLS), bf16), jax.ShapeDtypeStruct((n, KV_COLS), bf16)]
    out_specs = [pl.BlockSpec((tile, Q_COLS), row), pl.BlockSpec((tile, KV_COLS), row)]
    if resid is not None:
        out_shape.append(jax.ShapeDtypeStruct((n, D_MODEL), f32))
        out_specs.append(pl.BlockSpec((tile, D_MODEL), row))
    aliases = {}
    if not rope:
        reqs = tile // seq
        for j, width in enumerate(CACHE_WIDTHS):
            if prev_cache:
                aliases[len(args)] = len(out_shape)
                in_specs.append(pl.BlockSpec(memory_space=pl.ANY))
                args.append(prev_cache[j])
            out_shape.append(jax.ShapeDtypeStruct((n // seq, DEPTH, seq, width), f32))
            out_specs.append(pl.BlockSpec((reqs, 1, seq, width), lambda t: (t, layer, 0, 0)))
    return pl.pallas_call(
        functools.partial(_pre_kernel, rope, len(prev_cache), resid is not None),
        out_shape=out_shape,
        grid=(n // tile,),
        in_specs=in_specs,
        out_specs=out_specs,
        input_output_aliases=aliases,
        scratch_shapes=[pltpu.VMEM((Z_COLS, D_MODEL), bf16)],
        compiler_params=_params(("arbitrary",)),
        name="pre_latent" if rope else "pre_context",
    )(*args)


def _cache_kernel(ckv_ref, kr_ref, kg_ref, vg_ref, kd_ref, vd_ref, wka_ref, wuv_ref, kv_ref):
    ckv_b = ckv_ref[0, 0].astype(bf16)
    wka = wka_ref[0]
    k_a = _dot(ckv_b, wka[:KV_RANK]) + _dot(kr_ref[0, 0].astype(bf16), wka[KV_RANK:KV_RANK + MLA_ROPE])
    kv_ref[0, 0, :, KV_KA:KV_KA + 384] = k_a.astype(bf16)
    kv_ref[0, 0, :, KV_VA:KV_VA + 384] = _dot(ckv_b, wuv_ref[0].astype(bf16)).astype(bf16)
    kv_ref[0, 0, :, KV_KG:KV_KG + 128] = kg_ref[0, 0].astype(bf16)
    kv_ref[0, 0, :, KV_VG:KV_VG + 128] = vg_ref[0, 0].astype(bf16)
    kv_ref[0, 0, :, KV_KD:KV_KD + 256] = kd_ref[0, 0].astype(bf16)
    kv_ref[0, 0, :, KV_VD:KV_VD + 256] = vd_ref[0, 0].astype(bf16)


def _cache_rows(caches, w):
    B = caches[0].shape[0]
    spec = lambda width: pl.BlockSpec((1, 1, PAST_LEN, width), lambda i, b: (b, i, 0, 0))
    return pl.pallas_call(
        _cache_kernel,
        out_shape=jax.ShapeDtypeStruct((DEPTH, B, PAST_LEN, KV_COLS), bf16),
        grid=(DEPTH, B),
        in_specs=[spec(width) for width in CACHE_WIDTHS]
        + [pl.BlockSpec((1, 256, 384), lambda i, b: (i, 0, 0)), pl.BlockSpec((1, KV_RANK, 384), lambda i, b: (i, 0, 0))],
        out_specs=pl.BlockSpec((1, 1, PAST_LEN, KV_COLS), lambda i, b: (i, b, 0, 0)),
        compiler_params=_params(("arbitrary", "arbitrary")),
        name="cache_rows",
    )(*[c.reshape(B, DEPTH, PAST_LEN, width) for c, width in zip(caches, CACHE_WIDTHS)], w["w_ka"], w["w_uv"])


_SCORE_HEADS = (
    [(Q_A + 64 * h, KV_KA + 64 * h, 64, KV_VA + MLA_V * h) for h in range(MLA_HEADS)]
    + [(Q_G + 64 * h, KV_KG + 64 * (h // GQA_GROUP), 64, KV_VG + 64 * (h // GQA_GROUP)) for h in range(GQA_HEADS)]
    + [(Q_D + 64 * h + DIFF_QK * c, KV_KD + 64 * h + DIFF_QK * c, DIFF_QK, KV_VD + DIFF_V * h)
       for h in range(DIFF_HEADS) for c in range(2)])


def _attn_kernel(lam_init, per_round, n_src, q_ref, *refs):
    kv_refs = refs[:n_src]
    lam_ref, dg_ref, o_ref, s_ref, p_ref = refs[n_src:]
    spans, start = [], 0
    for r in kv_refs:
        spans.append((r, start, r.shape[2]))
        start += r.shape[2]
    mxu_sum = start >= MXU_SUM_MIN_KEYS

    outs = []
    for first in range(0, len(_SCORE_HEADS), per_round):
        chunk = _SCORE_HEADS[first:first + per_round]
        for j, (q_off, k_off, width, _) in enumerate(chunk):
            for r, lo, size in spans:
                s_ref[j, :, lo:lo + size] = _dot_nt(q_ref[:, q_off:q_off + width], r[0, 0, :, k_off:k_off + width])
        s = s_ref[...]
        p = jnp.exp2(s - jnp.max(s, axis=-1, keepdims=True))
        if mxu_sum:
            p_ref[...] = p.astype(bf16)
            for j, (_, _, _, v_off) in enumerate(chunk):
                o = sum(_dot(p_ref[j, :, lo:lo + size],
                             jnp.concatenate([r[0, 0, :, v_off:v_off + DIFF_V],
                                              jnp.ones((size, LANES - DIFF_V), bf16)], axis=1))
                        for r, lo, size in spans)
                outs.append((o * pltpu.roll(1.0 / o, DIFF_V, 1))[:, :DIFF_V])
        else:
            inv = 1.0 / jnp.sum(p, axis=-1, keepdims=True)
            p_ref[...] = p.astype(bf16)
            for j, (_, _, _, v_off) in enumerate(chunk):
                o = sum(_dot(p_ref[j, :, lo:lo + size], r[0, 0, :, v_off:v_off + DIFF_V]) for r, lo, size in spans)
                outs.append(o * inv[j])

    lp = lam_ref[0]
    e1 = jnp.exp(jnp.sum(lp[0:1] * lp[1:2], axis=-1, keepdims=True))
    e2 = jnp.exp(jnp.sum(lp[2:3] * lp[3:4], axis=-1, keepdims=True))
    lam = e1 - e2 + lam_init
    heads = outs[:MLA_HEADS + GQA_HEADS]
    for h in range(DIFF_HEADS):
        o1, o2 = outs[MLA_HEADS + GQA_HEADS + 2 * h:MLA_HEADS + GQA_HEADS + 2 * h + 2]
        heads.append(_rms(o1 - lam * o2, DIFF_V) * dg_ref[0] * (1.0 - lam_init))
    for j in range(len(heads) // 2):
        o_ref[:, LANES * j:LANES * (j + 1)] = jnp.concatenate(heads[2 * j:2 * j + 2], axis=1).astype(bf16)


def _attention(layer, q, sources, w, tile, per_round):
    n = q.shape[0]
    lam_init = 0.8 - 0.6 * math.exp(-0.3 * layer)
    s_kv = sum(rows for _, rows, _ in sources)
    assert len(_SCORE_HEADS) % per_round == 0
    return pl.pallas_call(
        functools.partial(_attn_kernel, lam_init, per_round, len(sources)),
        out_shape=jax.ShapeDtypeStruct((n, D_MODEL), bf16),
        scratch_shapes=[pltpu.VMEM((per_round, tile, s_kv), f32), pltpu.VMEM((per_round, tile, s_kv), bf16)],
        grid=(n // tile,),
        in_specs=[pl.BlockSpec((tile, Q_COLS), lambda t: (t, 0))]
        + [pl.BlockSpec((1, 1, rows, KV_COLS), index) for _, rows, index in sources]
        + [pl.BlockSpec((1, 4, DIFF_QK), lambda t: (layer, 0, 0)), pl.BlockSpec((1, 1, DIFF_V), lambda t: (layer, 0, 0))],
        out_specs=pl.BlockSpec((tile, D_MODEL), lambda t: (t, 0)),
        compiler_params=_params(("arbitrary",)),
        name="attention",
    )(q, *[arr for arr, _, _ in sources], w["lam"], w["diff_g"])


def _post_kernel(o_ref, x_ref, mod_ref, w_out_ref, g2_ref, wg_ref, bg_ref, we_ref, be_ref,
                 x1_ref, hx_ref, cls_ref, w_scr):

    @pl.when(pl.program_id(0) == 0)
    def _():
        w_scr[...] = w_out_ref[0].astype(bf16)

    gate1 = mod_ref[0, 0, 2:3, :]
    shift2 = mod_ref[0, 0, 3:4, :]
    scale2 = mod_ref[0, 0, 4:5, :]
    x1 = x_ref[...] + gate1 * _dot(o_ref[...], w_scr[...])
    x1_ref[...] = x1
    h2 = ((_rms(x1, D_MODEL) * g2_ref[0]) * (1.0 + scale2) + shift2).astype(bf16)
    bits = pltpu.bitcast(h2.astype(f32), i32)
    hx_ref[:, 0:HX_HALF] = bits[:, 0:HX_HALF] | jax.lax.shift_right_logical(bits[:, HX_HALF:D_MODEL], 16)

    def first_lane(mask, lane_f):
        return jnp.min(jnp.where(mask, lane_f, float(LANES)), axis=-1, keepdims=True)

    gl = _dot(h2, wg_ref[0].astype(bf16)) + bg_ref[0]
    glane = jax.lax.broadcasted_iota(i32, gl.shape, 1).astype(f32)
    ge = jnp.exp(gl - jnp.max(gl, axis=-1, keepdims=True))
    gprob = ge / jnp.sum(ge, axis=-1, keepdims=True)
    g_top = jnp.max(gprob, axis=-1, keepdims=True)
    g_idx = first_lane(gprob == g_top, glane)

    el = _dot(h2, we_ref[0].astype(bf16)) + be_ref[0]
    lane = jax.lax.broadcasted_iota(i32, el.shape, 1)
    lane_f = lane.astype(f32)
    emask = (lane >> 2).astype(f32) == g_idx
    em = jnp.where(emask, el, -jnp.inf)
    ee = jnp.where(emask, jnp.exp(em - jnp.max(em, axis=-1, keepdims=True)), 0.0)
    ep = ee / jnp.sum(ee, axis=-1, keepdims=True)
    p1 = jnp.max(jnp.where(emask, ep, -1.0), axis=-1, keepdims=True)
    i1 = first_lane(emask & (ep == p1), lane_f)
    rest = emask & (lane_f != i1)
    p2 = jnp.max(jnp.where(rest, ep, -1.0), axis=-1, keepdims=True)
    i2 = first_lane(rest & (ep == p2), lane_f)
    tot = p1 + p2
    w1 = g_top * (p1 / tot)
    w2 = g_top * (p2 / tot)

    lo = jnp.minimum(i1, i2) - EXPERTS_PER_GROUP * g_idx
    hi = jnp.maximum(i1, i2) - EXPERTS_PER_GROUP * g_idx
    pair = jnp.where(lo == 0.0, hi - 1.0, jnp.where(lo == 1.0, jnp.where(hi == 3.0, 3.0, 4.0), 5.0))
    cls_ref[...] = (N_PAIRS * g_idx + pair).astype(i32)
    g_lo = jnp.where(i1 < i2, w1, w2)
    g_hi = jnp.where(i1 < i2, w2, w1)
    tail_lane = jax.lax.broadcasted_iota(i32, (h2.shape[0], LANES), 1)
    hx_ref[:, HX_HALF:HX_COLS] = pltpu.bitcast(
        jnp.where(tail_lane == 0, g_lo, jnp.where(tail_lane == 1, g_hi, 0.0)), i32)


def _post(layer, o, x, mod, mod_row, w, tile):
    n = o.shape[0]
    lay = lambda t: (layer, 0, 0)
    row = lambda t: (t, 0)
    in_specs = [
        pl.BlockSpec((tile, D_MODEL), row),
        pl.BlockSpec((tile, D_MODEL), row),
        pl.BlockSpec((1, 1, N_MOD, D_MODEL), lambda t: (layer, mod_row(t * tile), 0, 0)),
        pl.BlockSpec((1, D_MODEL, D_MODEL), lay),
        pl.BlockSpec((1, 1, D_MODEL), lay),
        pl.BlockSpec((1, D_MODEL, N_GROUPS), lay),
        pl.BlockSpec((1, 1, N_GROUPS), lay),
        pl.BlockSpec((1, D_MODEL, N_EXPERTS), lay),
        pl.BlockSpec((1, 1, N_EXPERTS), lay),
    ]
    return pl.pallas_call(
        _post_kernel,
        out_shape=[jax.ShapeDtypeStruct((n, D_MODEL), f32), jax.ShapeDtypeStruct((n, HX_COLS), i32),
                   jax.ShapeDtypeStruct((n, 1), i32)],
        grid=(n // tile,),
        in_specs=in_specs,
        out_specs=[pl.BlockSpec((tile, D_MODEL), row), pl.BlockSpec((tile, HX_COLS), row), pl.BlockSpec((tile, 1), row)],
        scratch_shapes=[pltpu.VMEM((D_MODEL, D_MODEL), bf16)],
        compiler_params=_params(("arbitrary",)),
        name="post_attention",
    )(o, x, mod, w["w_out"], w["g2"], w["w_grp"], w["b_grp"], w["w_rtr"], w["b_rtr"])


def _ctx_kernel(lam_init, n_prev, resid, *refs):
    it = iter(refs)
    pre_in = [next(it) for _ in range(9 + (2 if resid else 0) + n_prev)]
    lam_ref, dg_ref = next(it), next(it)
    post_w = [next(it) for _ in range(6)]
    cache_refs = [next(it) for _ in range(len(CACHE_WIDTHS))]
    x1_ref, hx_ref, cls_ref = (next(it) for _ in range(3))
    w_in_scr, q_scr, kv_scr, o_scr, s_scr, p_scr, w_out_scr, x2_scr = (next(it) for _ in range(8))
    x_ref, mod_ref = pre_in[0], pre_in[1]
    x2_ref = [x2_scr] if resid else []

    _pre_kernel(False, n_prev, resid, *pre_in, q_scr, kv_scr.at[0, 0], *x2_ref, *cache_refs, w_in_scr)
    _attn_kernel(lam_init, len(_SCORE_HEADS), 1, q_scr, kv_scr, lam_ref, dg_ref, o_scr, s_scr, p_scr)
    _post_kernel(o_scr, x2_scr if resid else x_ref, mod_ref, *post_w, x1_ref, hx_ref, cls_ref, w_out_scr)


def _ctx_layer(layer, x, n, seq, mod, w, prev_cache=(), resid=None):
    lay = lambda t: (layer, 0, 0)
    row = lambda t: (t, 0)
    mod_spec = lambda l: pl.BlockSpec((1, 1, N_MOD, D_MODEL), lambda t: (l, 0, 0, 0))
    in_specs = [
        pl.BlockSpec((seq, D_MODEL), row), mod_spec(layer),
        pl.BlockSpec((1, 1, D_MODEL), lay), pl.BlockSpec((1, IN_COLS, D_MODEL), lay),
        pl.BlockSpec((1, 1, KV_RANK), lay), pl.BlockSpec((1, 1, 512), lay), pl.BlockSpec((512, 512), lambda t: (0, 0)),
        pl.BlockSpec((1, 256, 384), lay), pl.BlockSpec((1, KV_RANK, 384), lay),
    ]
    args = [x, mod, w["g1"], w["w_in"], w["kv_g"], w["qk_g"], w["seg"], w["w_ka"], w["w_uv"]]
    if resid is not None:
        in_specs += [pl.BlockSpec((seq, D_MODEL), row), mod_spec(layer - 1)]
        args += [resid, mod]
    out_shape, out_specs, aliases = [], [], {}
    for j, width in enumerate(CACHE_WIDTHS):
        if prev_cache:
            aliases[len(args)] = len(out_shape)
            in_specs.append(pl.BlockSpec(memory_space=pl.ANY))
            args.append(prev_cache[j])
        out_shape.append(jax.ShapeDtypeStruct((n // seq, DEPTH, seq, width), f32))
        out_specs.append(pl.BlockSpec((1, 1, seq, width), lambda t: (t, layer, 0, 0)))
    in_specs += [
        pl.BlockSpec((1, 4, DIFF_QK), lay), pl.BlockSpec((1, 1, DIFF_V), lay),
        pl.BlockSpec((1, D_MODEL, D_MODEL), lay), pl.BlockSpec((1, 1, D_MODEL), lay),
        pl.BlockSpec((1, D_MODEL, N_GROUPS), lay), pl.BlockSpec((1, 1, N_GROUPS), lay),
        pl.BlockSpec((1, D_MODEL, N_EXPERTS), lay), pl.BlockSpec((1, 1, N_EXPERTS), lay),
    ]
    args += [w["lam"], w["diff_g"], w["w_out"], w["g2"], w["w_grp"], w["b_grp"], w["w_rtr"], w["b_rtr"]]
    out_shape += [jax.ShapeDtypeStruct((n, D_MODEL), f32), jax.ShapeDtypeStruct((n, HX_COLS), i32),
                  jax.ShapeDtypeStruct((n, 1), i32)]
    out_specs += [pl.BlockSpec((seq, D_MODEL), row), pl.BlockSpec((seq, HX_COLS), row), pl.BlockSpec((seq, 1), row)]
    heads = len(_SCORE_HEADS)
    return pl.pallas_call(
        functools.partial(_ctx_kernel, 0.8 - 0.6 * math.exp(-0.3 * layer), len(prev_cache), resid is not None),
        out_shape=out_shape,
        grid=(n // seq,),
        in_specs=in_specs,
        out_specs=out_specs,
        scratch_shapes=[pltpu.VMEM((Z_COLS, D_MODEL), bf16), pltpu.VMEM((seq, Q_COLS), bf16),
                        pltpu.VMEM((1, 1, seq, KV_COLS), bf16), pltpu.VMEM((seq, D_MODEL), bf16),
                        pltpu.VMEM((heads, seq, seq), f32), pltpu.VMEM((heads, seq, seq), bf16),
                        pltpu.VMEM((D_MODEL, D_MODEL), bf16), pltpu.VMEM((seq, D_MODEL), f32)],
        input_output_aliases=aliases,
        compiler_params=_params(("arbitrary",)),
        name="context_layer",
    )(*args)


PLAN_CHUNK = 1024
TAB_ROWS = LANES


def _plan_kernel(tile, cls_ref, slot_ref, tab_ref, rank_scr):
    n = cls_ref.shape[0]
    lane = jax.lax.broadcasted_iota(i32, (PLAN_CHUNK, LANES), 1)
    r = jax.lax.broadcasted_iota(i32, (PLAN_CHUNK, PLAN_CHUNK), 0)
    c = jax.lax.broadcasted_iota(i32, (PLAN_CHUNK, PLAN_CHUNK), 1)
    before = (c < r).astype(bf16)

    def count(b, seen):
        rows = pl.ds(pl.multiple_of(b * PLAN_CHUNK, PLAN_CHUNK), PLAN_CHUNK)
        onehot = (cls_ref[rows, :] == lane).astype(f32)
        ahead = _dot(before, onehot.astype(bf16)) + seen
        rank_scr[rows, :] = jnp.sum(onehot * ahead, axis=-1, keepdims=True)
        return seen + jnp.sum(onehot, axis=0, keepdims=True)

    counts = jax.lax.fori_loop(0, n // PLAN_CHUNK, count, jnp.zeros((1, LANES), f32))
    tiles = jnp.floor((counts + (tile - 1)) * (1.0 / tile))
    rr = jax.lax.broadcasted_iota(i32, (LANES, LANES), 0)
    cc = jax.lax.broadcasted_iota(i32, (LANES, LANES), 1)
    ends = _dot(jnp.broadcast_to(tiles, (8, LANES)).astype(bf16), (rr <= cc).astype(bf16))[0:1]
    starts = ends - tiles

    def place(b, carry):
        rows = pl.ds(pl.multiple_of(b * PLAN_CHUNK, PLAN_CHUNK), PLAN_CHUNK)
        first = jnp.sum(jnp.where(cls_ref[rows, :] == lane, starts, 0.0), axis=-1, keepdims=True)
        slot_ref[rows, :] = (first * tile + rank_scr[rows, :]).astype(i32)
        return carry

    jax.lax.fori_loop(0, n // PLAN_CHUNK, place, 0)

    tl = jax.lax.broadcasted_iota(i32, (TAB_ROWS, LANES), 1)
    n_tiles = jnp.sum(jnp.where(tl[0:1] == N_CLASSES - 1, ends, 0.0), axis=-1, keepdims=True)
    k = jnp.minimum(jax.lax.broadcasted_iota(i32, (TAB_ROWS, 1), 0).astype(f32), n_tiles - 1.0)
    cls_k = jnp.sum(jnp.where((tl < N_CLASSES) & (ends <= k), 1.0, 0.0), axis=-1, keepdims=True)
    cls_k = jnp.minimum(cls_k, N_CLASSES - 1.0)
    mine = tl.astype(f32) == cls_k
    used = jnp.sum(jnp.where(mine, counts, 0.0), axis=-1, keepdims=True)
    first = jnp.sum(jnp.where(mine, starts, 0.0), axis=-1, keepdims=True)
    valid = jnp.clip(used - (k - first) * tile, 0.0, float(tile))
    group = jnp.floor((cls_k + 0.5) * (1.0 / N_PAIRS))
    pair = cls_k - N_PAIRS * group
    lo = hi = jnp.zeros_like(pair)
    for p in range(N_PAIRS):
        lo = jnp.where(pair == p, float(PAIR_LO[p]), lo)
        hi = jnp.where(pair == p, float(PAIR_HI[p]), hi)
    e_lo = EXPERTS_PER_GROUP * group + lo
    e_hi = EXPERTS_PER_GROUP * group + hi
    tab = jnp.where(tl == 0, e_lo, jnp.where(tl == 1, e_hi, jnp.where(tl == 2, valid, jnp.where(tl == 3, n_tiles, 0.0))))
    tab_ref[...] = tab.astype(i32)


def _plan(cls, max_tiles, tile):
    n = cls.shape[0]
    assert n % PLAN_CHUNK == 0 and max_tiles <= TAB_ROWS
    slot, tab = pl.pallas_call(
        functools.partial(_plan_kernel, tile),
        out_shape=[jax.ShapeDtypeStruct((n, 1), i32), jax.ShapeDtypeStruct((TAB_ROWS, LANES), i32)],
        scratch_shapes=[pltpu.VMEM((n, 1), f32)],
        compiler_params=_params(None),
        name="dispatch_plan",
    )(cls)
    return slot.reshape(n), tab[:max_tiles, 0], tab[:max_tiles, 1], tab[:max_tiles, 2], tab[0, 3:4]


def _move_rows(src, idx, n_out, scatter):
    n = idx.shape[0]
    width = src.shape[1]
    per_worker = n // (SC_CORES * SC_SUBCORES)
    assert n % SC_ROWS == 0
    chunk = max(c for c in (64, 40, 32, 16, 8)
                if per_worker % c == 0 and 2 * c * width * src.dtype.itemsize <= SC_BUFFER_BYTES)
    n_chunks = per_worker // chunk
    mesh = plsc.VectorSubcoreMesh(core_axis_name="c", subcore_axis_name="s")

    @functools.partial(
        pl.kernel, mesh=mesh, out_type=jax.ShapeDtypeStruct((n_out, width), src.dtype),
        scratch_types=[pltpu.VMEM((chunk,), i32), pltpu.VMEM((chunk,), i32),
                       pltpu.VMEM((chunk, width), src.dtype), pltpu.VMEM((chunk, width), src.dtype),
                       pltpu.SemaphoreType.DMA, pltpu.SemaphoreType.DMA, pltpu.SemaphoreType.DMA,
                       pltpu.SemaphoreType.DMA])
    def move(src_hbm, idx_hbm, out_hbm, idx0, idx1, rows0, rows1, in0, in1, out0, out1):
        wid = jax.lax.axis_index("s") * SC_CORES + jax.lax.axis_index("c")
        base = wid * per_worker
        idx_v, rows_v, sem_in, sem_out = (idx0, idx1), (rows0, rows1), (in0, in1), (out0, out1)

        def fill(j):
            b = j % 2
            rows = pl.ds(base + j * chunk, chunk)
            pltpu.sync_copy(idx_hbm.at[rows], idx_v[b])
            src_rows = src_hbm.at[rows] if scatter else src_hbm.at[idx_v[b]]
            return pltpu.async_copy(src_rows, rows_v[b], sem_in[b])

        def drain(j):
            b = j % 2
            dst_rows = out_hbm.at[idx_v[b]] if scatter else out_hbm.at[pl.ds(base + j * chunk, chunk)]
            return pltpu.async_copy(rows_v[b], dst_rows, sem_out[b])

        fills, drains = {0: fill(0)}, {}
        for j in range(n_chunks):
            if j + 1 < n_chunks:
                if j >= 1:
                    drains[j - 1].wait()
                fills[j + 1] = fill(j + 1)
            fills[j].wait()
            drains[j] = drain(j)
        for j in range(max(n_chunks - 2, 0), n_chunks):
            drains[j].wait()

    return move(src, idx)


def _expert_kernel(lo_ref, hi_ref, valid_ref, nt_ref, xs_ref, wg_ref, wu_ref, wd_ref, ys_ref, wg_scr, wu_scr, wd_scr):
    k = pl.program_id(0)
    prev = jnp.maximum(k - 1, 0)

    @pl.when((k == 0) | (lo_ref[k] // EXPERTS_PER_GROUP != lo_ref[prev] // EXPERTS_PER_GROUP))
    def _():
        for j in range(EXPERTS_PER_GROUP):
            wg_scr[j] = wg_ref[0, 0, j].astype(bf16)
            wu_scr[j] = wu_ref[0, 0, j].astype(bf16)
            wd_scr[j] = wd_ref[0, 0, j].astype(bf16)

    @pl.when(k < nt_ref[0])
    def _():
        live = jax.lax.broadcasted_iota(i32, (xs_ref.shape[0], 1), 0) < valid_ref[k]
        words = jnp.where(live, xs_ref[:, 0:HX_HALF], 0)
        x = jnp.concatenate([pltpu.bitcast(words & -65536, f32), pltpu.bitcast(words << 16, f32)], axis=1).astype(bf16)
        gates = pltpu.bitcast(jnp.where(live, xs_ref[:, HX_HALF:HX_COLS], 0), f32)
        y = None
        for lane, e_ref in enumerate((lo_ref, hi_ref)):
            j = e_ref[k] % EXPERTS_PER_GROUP
            hid = _silu(_dot(x, wg_scr[j])) * _dot(x, wu_scr[j]) * gates[:, lane:lane + 1]
            part = _dot(hid.astype(bf16), wd_scr[j])
            y = part if y is None else y + part
        ys_ref[...] = y


def _experts(layer, xs, e_lo, e_hi, valid, n_tiles, w, tile):
    max_tiles = e_lo.shape[0]
    row = lambda k, lo, hi, valid, nt: (jnp.minimum(k, nt[0] - 1), 0)
    group = lambda k, lo, hi, valid, nt: (layer, lo[k] // EXPERTS_PER_GROUP, 0, 0, 0)
    by_group = lambda a: a.reshape(DEPTH, N_GROUPS, EXPERTS_PER_GROUP, *a.shape[2:])
    return pl.pallas_call(
        _expert_kernel,
        out_shape=jax.ShapeDtypeStruct((max_tiles * tile, D_MODEL), f32),
        grid_spec=pltpu.PrefetchScalarGridSpec(
            num_scalar_prefetch=4, grid=(max_tiles,),
            in_specs=[pl.BlockSpec((tile, HX_COLS), row),
                      pl.BlockSpec((1, 1, EXPERTS_PER_GROUP, D_MODEL, D_FF_EXPERT), group),
                      pl.BlockSpec((1, 1, EXPERTS_PER_GROUP, D_MODEL, D_FF_EXPERT), group),
                      pl.BlockSpec((1, 1, EXPERTS_PER_GROUP, D_FF_EXPERT, D_MODEL), group)],
            out_specs=pl.BlockSpec((tile, D_MODEL), row),
            scratch_shapes=[pltpu.VMEM((EXPERTS_PER_GROUP, D_MODEL, D_FF_EXPERT), bf16),
                            pltpu.VMEM((EXPERTS_PER_GROUP, D_MODEL, D_FF_EXPERT), bf16),
                            pltpu.VMEM((EXPERTS_PER_GROUP, D_FF_EXPERT, D_MODEL), bf16)]),
        compiler_params=_params(("arbitrary",)),
        name="experts",
    )(e_lo, e_hi, valid, n_tiles, xs, by_group(w["w_gate"]), by_group(w["w_up"]), by_group(w["w_down"]))


def _mixer(layer, hx, cls, w, tile):
    n = cls.shape[0]
    slot_rows = -(-(n + N_CLASSES * tile) // SC_ROWS) * SC_ROWS
    slot, e_lo, e_hi, valid, n_tiles = _plan(cls, slot_rows // tile, tile)
    xs = _move_rows(hx, slot, slot_rows, scatter=True)
    ys = _experts(layer, xs, e_lo, e_hi, valid, n_tiles, w, tile)
    return _move_rows(ys, slot, n, scatter=False)


def _final_kernel(x1_ref, y_ref, mod_ref, fg_ref, o_ref):
    o_ref[...] = _rms(x1_ref[...] + mod_ref[0, 0, 5:6, :] * y_ref[...], D_MODEL) * fg_ref[...]


def _final(x1, y, n, row0, mod, mod_row, w, tile):
    off = row0 // tile
    src_row = lambda t: (off + t, 0)
    return pl.pallas_call(
        _final_kernel,
        out_shape=jax.ShapeDtypeStruct((n, D_MODEL), f32),
        grid=(n // tile,),
        in_specs=[pl.BlockSpec((tile, D_MODEL), src_row), pl.BlockSpec((tile, D_MODEL), src_row),
                  pl.BlockSpec((1, 1, N_MOD, D_MODEL), lambda t: (DEPTH - 1, mod_row(t * tile), 0, 0)),
                  pl.BlockSpec((1, D_MODEL), lambda t: (0, 0))],
        out_specs=pl.BlockSpec((tile, D_MODEL), lambda t: (t, 0)),
        compiler_params=_params(("arbitrary",)),
        name="final_norm",
    )(x1, y, mod, w["final_g"])


def _rope_tables(n_tokens):
    pos = np.arange(n_tokens)
    row = (pos // GRID_W).astype(np.float64)
    col = (pos % GRID_W).astype(np.float64)

    def cs(rot_dim):
        quarter = rot_dim // 4
        inv = ROPE_THETA ** (-np.arange(quarter, dtype=np.float64) / quarter)
        ang = np.concatenate([row[:, None] * inv, col[:, None] * inv], axis=-1)
        return np.cos(ang), np.sin(ang)

    c32, s32 = cs(MLA_ROPE)
    c64, s64 = cs(HEAD_DIM)
    ones = np.ones((n_tokens, MLA_NOPE))
    zeros = np.zeros((n_tokens, MLA_NOPE))

    def rep(parts):
        period = np.concatenate(parts, axis=-1)
        return jnp.asarray(np.tile(period, (1, LANES // period.shape[-1])), f32)

    return (rep([ones, c32, c32]), rep([zeros, -s32, s32]), rep([c32, c32]), rep([-s32, s32]),
            rep([c64, c64]), rep([-s64, s64]))


def _layout_weights(norm1_g, norm2_g, w_in, mla_kv_norm_g, mla_w_uk, mla_w_uv, gqa_q_norm_g, gqa_k_norm_g,
                    diff_lambda, diff_norm_g, w_out, moe_w_group, moe_b_group, moe_w_router, moe_b_router,
                    moe_w_gate, moe_w_up, moe_w_down, final_norm_g):
    eye = jnp.eye(MLA_ROPE, dtype=f32)
    top = jnp.concatenate([mla_w_uk, jnp.zeros((DEPTH, KV_RANK, MLA_HEADS, MLA_ROPE), f32)], axis=-1)
    mid = jnp.concatenate([jnp.zeros((MLA_ROPE, MLA_HEADS, MLA_NOPE), f32),
                           jnp.broadcast_to(eye[:, None, :], (MLA_ROPE, MLA_HEADS, MLA_ROPE))], axis=-1)
    w_ka = jnp.concatenate([top.reshape(DEPTH, KV_RANK, 384),
                            jnp.broadcast_to(mid.reshape(1, MLA_ROPE, 384), (DEPTH, MLA_ROPE, 384)),
                            jnp.zeros((DEPTH, 256 - KV_RANK - MLA_ROPE, 384), f32)], axis=1).astype(bf16)
    seg_id = np.arange(512) // HEAD_DIM
    seg = jnp.asarray(seg_id[:, None] == seg_id[None, :], bf16)
    qk_g = jnp.concatenate([jnp.tile(gqa_q_norm_g, (1, GQA_HEADS)), jnp.tile(gqa_k_norm_g, (1, GQA_KV_HEADS))], axis=-1)
    return dict(
        g1=norm1_g.reshape(DEPTH, 1, D_MODEL), g2=norm2_g.reshape(DEPTH, 1, D_MODEL),
        w_in=jnp.swapaxes(w_in, 1, 2),
        kv_g=mla_kv_norm_g.reshape(DEPTH, 1, KV_RANK), qk_g=qk_g.reshape(DEPTH, 1, 512), seg=seg, w_ka=w_ka,
        w_uv=mla_w_uv.reshape(DEPTH, KV_RANK, 384), lam=diff_lambda, diff_g=diff_norm_g.reshape(DEPTH, 1, DIFF_V),
        w_out=w_out, w_grp=moe_w_group, b_grp=moe_b_group.reshape(DEPTH, 1, N_GROUPS), w_rtr=moe_w_router,
        b_rtr=moe_b_router.reshape(DEPTH, 1, N_EXPERTS), w_gate=moe_w_gate, w_up=moe_w_up, w_down=moe_w_down,
        final_g=final_norm_g.reshape(1, D_MODEL))


PRE_TILE = 512
CTX_ATTN_TILE = 256
LAT_ATTN_TILE = 256
CTX_HEADS_PER_ROUND = 20
LAT_HEADS_PER_ROUND = 5
MXU_SUM_MIN_KEYS = 1024
POST_TILE = 512
FINAL_TILE = 1024
CTX_EXPERT_TILE = 256
LAT_EXPERT_TILE = 128


def kernel(x_prompt, x_sample, c, cache_mla_ckv, cache_mla_krope, cache_gqa_k, cache_gqa_v, cache_diff_k, cache_diff_v, c_ctx, norm1_g, norm2_g, w_mod, b_mod, w_in, mla_kv_norm_g, mla_w_uk, mla_w_uv, gqa_q_norm_g, gqa_k_norm_g, diff_lambda, diff_norm_g, w_out, moe_w_group, moe_b_group, moe_w_router, moe_b_router, moe_w_gate, moe_w_up, moe_w_down, final_norm_g):
    B, S, _ = x_prompt.shape
    Bl, Sl, _ = x_sample.shape
    n_ctx, n_lat = B * S, Bl * Sl
    assert S == CTX_ATTN_TILE and Bl + 1 <= MOD_ROWS and DEPTH == 2
    w = _layout_weights(norm1_g, norm2_g, w_in, mla_kv_norm_g, mla_w_uk, mla_w_uv, gqa_q_norm_g, gqa_k_norm_g,
                        diff_lambda, diff_norm_g, w_out, moe_w_group, moe_b_group, moe_w_router, moe_b_router,
                        moe_w_gate, moe_w_up, moe_w_down, final_norm_g)
    cond = jnp.concatenate([c_ctx[None, :], c, jnp.zeros((MOD_ROWS - 1 - Bl, D_MODEL), f32)], axis=0)
    mod = _modulation(cond, w_mod, b_mod).reshape(DEPTH, MOD_ROWS, N_MOD, D_MODEL)
    ctx_row = lambda token: 0
    lat_row = lambda token: 1 + token // Sl
    tabs = _rope_tables(Sl)
    kv_past = _cache_rows((cache_mla_ckv, cache_mla_krope, cache_gqa_k, cache_gqa_v, cache_diff_k, cache_diff_v), w)
    per_b = Sl // LAT_ATTN_TILE

    x_ctx, x_lat = x_prompt.reshape(n_ctx, D_MODEL), x_sample.reshape(n_lat, D_MODEL)
    cache = ()
    x1_c = y_c = x1_l = y_l = None
    for i in range(DEPTH):
        if i == 0:
            *cache, x1_c, hx_c, cls_c = _ctx_layer(i, x_ctx, n_ctx, S, mod, w)
        else:
            *cache, x1_c, hx_c, cls_c = _ctx_layer(i, x1_c, n_ctx, S, mod, w, prev_cache=cache, resid=y_c)
        y_c = _mixer(i, hx_c, cls_c, w, CTX_EXPERT_TILE)
        if i == 0:
            q_l, kv_l = _pre(i, x_lat, n_lat, 0, Sl, mod, lat_row, w, PRE_TILE, rope_tabs=tabs)
        else:
            q_l, kv_l, x_lat = _pre(i, x1_l, n_lat, 0, Sl, mod, lat_row, w, PRE_TILE, rope_tabs=tabs, resid=y_l)
        past = (kv_past, PAST_LEN, lambda t, i=i: (i, t // per_b, 0, 0))
        own = (kv_l.reshape(1, Bl, Sl, KV_COLS), Sl, lambda t: (0, t // per_b, 0, 0))
        o_l = _attention(i, q_l, [past, own], w, LAT_ATTN_TILE, LAT_HEADS_PER_ROUND)
        x1_l, hx_l, cls_l = _post(i, o_l, x_lat, mod, lat_row, w, POST_TILE)
        if i == DEPTH - 1:
            outs = (cache[0], cache[1], cache[2].reshape(B, DEPTH, S, GQA_KV_HEADS, HEAD_DIM),
                    cache[3].reshape(B, DEPTH, S, GQA_KV_HEADS, HEAD_DIM),
                    cache[4].reshape(B, DEPTH, S, DIFF_HEADS, 2, DIFF_QK),
                    cache[5].reshape(B, DEPTH, S, DIFF_HEADS, DIFF_V))
            cls_l, outs = jax.lax.optimization_barrier((cls_l, outs))
        y_l = _mixer(i, hx_l, cls_l, w, LAT_EXPERT_TILE)

    y_prompt = _final(x1_c, y_c, n_ctx, 0, mod, ctx_row, w, FINAL_TILE).reshape(B, S, D_MODEL)
    y_sample = _final(x1_l, y_l, n_lat, 0, mod, lat_row, w, FINAL_TILE).reshape(Bl, Sl, D_MODEL)
    return (y_prompt, y_sample, *outs)
```

```python
import functools
import math

import jax
import jax.numpy as jnp
import numpy as np
from jax.experimental import pallas as pl
from jax.experimental.pallas import tpu as pltpu
from jax.experimental.pallas import tpu_sc as plsc

D_MODEL = 1024
DEPTH = 2
PAST_LEN = 512
GRID_W = 64
ROPE_THETA = 10000.0
EPS = 1e-6
LOG2E = 1.4426950408889634
N_MOD = 6
HEAD_DIM = 64
MLA_HEADS = 6
MLA_NOPE = 32
MLA_ROPE = 32
MLA_V = 64
KV_RANK = 128
GQA_HEADS = 6
GQA_KV_HEADS = 2
GQA_GROUP = GQA_HEADS // GQA_KV_HEADS
DIFF_HEADS = 4
DIFF_QK = 32
DIFF_V = 64
N_GROUPS = 4
EXPERTS_PER_GROUP = 4
N_EXPERTS = N_GROUPS * EXPERTS_PER_GROUP
D_FF_EXPERT = 256

LANES = 128
MOD_ROWS = 8

IN_COLS = 1952
IN_KR = 512
Z_QA, Z_CKV, Z_QG, Z_KG, Z_VG, Z_QD, Z_KD, Z_VD, Z_KR = 0, 384, 512, 896, 1024, 1152, 1408, 1664, 1920
Z_COLS = 2048
Q_A, Q_G, Q_D, Q_COLS = 0, 384, 768, 1024
KV_KA, KV_VA, KV_KG, KV_VG, KV_KD, KV_VD, KV_COLS = 0, 384, 768, 896, 1024, 1280, 1536
CACHE_WIDTHS = (128, 32, 128, 128, 256, 256)

PAIR_LO = (0, 0, 0, 1, 1, 2)
PAIR_HI = (1, 2, 3, 3, 2, 3)
N_PAIRS = len(PAIR_LO)
N_CLASSES = N_GROUPS * N_PAIRS
HX_HALF = D_MODEL // 2
HX_COLS = HX_HALF + LANES
EXPERT_TILE = 256

SC_CORES, SC_SUBCORES = 2, 16
SC_BUFFER_BYTES = 400 * 1024
SC_ROWS = SC_CORES * SC_SUBCORES * 8

VMEM_LIMIT = 56 * 1024 * 1024

bf16 = jnp.bfloat16
f32 = jnp.float32
i32 = jnp.int32


def _dot(a, b):
    return jnp.dot(a, b, preferred_element_type=f32)


def _dot_nt(a, b):
    return jax.lax.dot_general(a, b, (((1,), (1,)), ((), ())), preferred_element_type=f32)


def _rms(x, width):
    return x * jax.lax.rsqrt(jnp.sum(x * x, axis=-1, keepdims=True) * (1.0 / width) + EPS)


def _silu(x):
    return x * (1.0 / (1.0 + jnp.exp(-x)))


def _params(sem):
    return pltpu.CompilerParams(dimension_semantics=sem, vmem_limit_bytes=VMEM_LIMIT)


def _mod_kernel(cond_ref, w_ref, b_ref, o_ref):
    o_ref[0] = _dot(_silu(cond_ref[...]).astype(bf16), w_ref[0].astype(bf16)) + b_ref[0]


def _modulation(cond, w_mod, b_mod):
    return pl.pallas_call(
        _mod_kernel,
        out_shape=jax.ShapeDtypeStruct((DEPTH, MOD_ROWS, N_MOD * D_MODEL), f32),
        grid=(DEPTH, N_MOD),
        in_specs=[
            pl.BlockSpec((MOD_ROWS, D_MODEL), lambda i, j: (0, 0)),
            pl.BlockSpec((1, D_MODEL, D_MODEL), lambda i, j: (i, 0, j)),
            pl.BlockSpec((1, 1, D_MODEL), lambda i, j: (i, 0, j)),
        ],
        out_specs=pl.BlockSpec((1, MOD_ROWS, D_MODEL), lambda i, j: (i, 0, j)),
        compiler_params=_params(("arbitrary", "arbitrary")),
        name="modulation",
    )(cond, w_mod, b_mod.reshape(DEPTH, 1, N_MOD * D_MODEL))


def _swap_halves(x, half):
    lane = jax.lax.broadcasted_iota(i32, x.shape, 1)
    fwd = pltpu.roll(x, LANES - half, 1)
    bwd = pltpu.roll(x, half, 1)
    return jnp.where((lane & (2 * half - 1)) < half, fwd, bwd)


def _rope_block(x, cos, sin, half):
    return x * cos + _swap_halves(x, half) * sin


def _pre_kernel(rope, n_prev, resid, *refs):
    it = iter(refs)
    x_ref, mod_ref, g1_ref, w_in_ref, kvg_ref, qkg_ref, seg_ref, wka_ref, wuv_ref = (next(it) for _ in range(9))
    if resid:
        y_ref, pmod_ref = next(it), next(it)
    if rope:
        ca_ref, sa_ref, c32_ref, s32_ref, c64_ref, s64_ref = (next(it) for _ in range(6))
    prev_refs = [next(it) for _ in range(n_prev)]
    q_ref, kv_ref = next(it), next(it)
    if resid:
        x2_ref = next(it)
    cache_refs = [] if rope else [next(it) for _ in range(len(CACHE_WIDTHS))]
    w_scr = next(it)

    @pl.when(pl.program_id(0) == 0)
    def _():
        w_scr[0:IN_KR] = w_in_ref[0, 0:IN_KR].astype(bf16)
        w_scr[IN_KR:Z_KR] = w_in_ref[0, IN_KR + MLA_ROPE:IN_COLS].astype(bf16)
        w_scr[Z_KR:Z_KR + MLA_ROPE] = w_in_ref[0, IN_KR:IN_KR + MLA_ROPE].astype(bf16)
        w_scr[Z_KR + MLA_ROPE:Z_COLS] = jnp.zeros((Z_COLS - Z_KR - MLA_ROPE, D_MODEL), bf16)

    x = x_ref[...]
    if resid:
        x = x + pmod_ref[0, 0, 5:6, :] * y_ref[...]
        x2_ref[...] = x
    shift1 = mod_ref[0, 0, 0:1, :]
    scale1 = mod_ref[0, 0, 1:2, :]
    h = (_rms(x, D_MODEL) * g1_ref[0]) * (1.0 + scale1) + shift1
    z = _dot_nt(h.astype(bf16), w_scr[...])

    ckv = _rms(z[:, Z_CKV:Z_CKV + KV_RANK], KV_RANK) * kvg_ref[0]

    qk = z[:, Z_QG:Z_VG]
    sq = qk * qk
    sq_hi = sq.astype(bf16)
    sq_lo = (sq - sq_hi.astype(f32)).astype(bf16)
    seg = seg_ref[...]
    ms = (_dot(sq_hi, seg) + _dot(sq_lo, seg)) * (1.0 / HEAD_DIM)
    qk = qk * jax.lax.rsqrt(ms + EPS) * qkg_ref[0]

    def blocks(arr, n):
        return [arr[:, LANES * j:LANES * (j + 1)] for j in range(n)]

    qa = blocks(z[:, Z_QA:Z_QA + 384], 3)
    qkb = blocks(qk, 4)
    qd = blocks(z[:, Z_QD:Z_QD + 256], 2)
    kd = blocks(z[:, Z_KD:Z_KD + 256], 2)
    kr = z[:, Z_KR:Z_KR + LANES]
    if rope:
        ca, sa, c32, s32, c64, s64 = (r[...] for r in (ca_ref, sa_ref, c32_ref, s32_ref, c64_ref, s64_ref))
        qa = [_rope_block(b, ca, sa, MLA_ROPE // 2) for b in qa]
        qkb = [_rope_block(b, c64, s64, HEAD_DIM // 2) for b in qkb]
        qd = [_rope_block(b, c32, s32, DIFF_QK // 2) for b in qd]
        kd = [_rope_block(b, c32, s32, DIFF_QK // 2) for b in kd]
        kr = _rope_block(kr, c32, s32, MLA_ROPE // 2)

    vg = z[:, Z_VG:Z_VG + 128]
    vd = z[:, Z_VD:Z_VD + 256]
    ckv_b = ckv.astype(bf16)
    k_a = _dot(jnp.concatenate([ckv_b, kr.astype(bf16)], axis=1), wka_ref[0])
    v_a = _dot(ckv_b, wuv_ref[0].astype(bf16))

    for j in range(3):
        q_ref[:, Q_A + LANES * j:Q_A + LANES * (j + 1)] = (qa[j] * (HEAD_DIM ** -0.5 * LOG2E)).astype(bf16)
        q_ref[:, Q_G + LANES * j:Q_G + LANES * (j + 1)] = (qkb[j] * (HEAD_DIM ** -0.5 * LOG2E)).astype(bf16)
    for j in range(2):
        q_ref[:, Q_D + LANES * j:Q_D + LANES * (j + 1)] = (qd[j] * (DIFF_QK ** -0.5 * LOG2E)).astype(bf16)
        kv_ref[:, KV_KD + LANES * j:KV_KD + LANES * (j + 1)] = kd[j].astype(bf16)
    kv_ref[:, KV_KA:KV_KA + 384] = k_a.astype(bf16)
    kv_ref[:, KV_VA:KV_VA + 384] = v_a.astype(bf16)
    kv_ref[:, KV_KG:KV_KG + 128] = qkb[3].astype(bf16)
    kv_ref[:, KV_VG:KV_VG + 128] = vg.astype(bf16)
    kv_ref[:, KV_VD:KV_VD + 256] = vd.astype(bf16)
    if not rope:
        rows = [ckv, kr[:, :MLA_ROPE], qkb[3], vg, jnp.concatenate(kd, axis=1), vd]
        for out, new in zip(cache_refs, rows):
            reqs, _, seq, width = out.shape
            out[:, 0] = new.reshape(reqs, seq, width)


def _pre_latent(layer, x, n, row0, seq, mod, mod_row, w, tile, rope_tabs, resid=None):
    lay = lambda t: (layer, 0, 0)
    row = lambda t: (t, 0)
    off = row0 // tile
    src_row = lambda t: (off + t, 0)
    in_specs = [
        pl.BlockSpec((tile, D_MODEL), src_row),
        pl.BlockSpec((1, 1, N_MOD, D_MODEL), lambda t: (layer, mod_row(t * tile), 0, 0)),
        pl.BlockSpec((1, 1, D_MODEL), lay),
        pl.BlockSpec((1, IN_COLS, D_MODEL), lay),
        pl.BlockSpec((1, 1, KV_RANK), lay),
        pl.BlockSpec((1, 1, 512), lay),
        pl.BlockSpec((512, 512), lambda t: (0, 0)),
        pl.BlockSpec((1, 256, 384), lay),
        pl.BlockSpec((1, KV_RANK, 384), lay),
    ]
    args = [x, mod, w["g1"], w["w_in"], w["kv_g"], w["qk_g"], w["seg"], w["w_ka"], w["w_uv"]]
    if resid is not None:
        in_specs += [pl.BlockSpec((tile, D_MODEL), src_row),
                     pl.BlockSpec((1, 1, N_MOD, D_MODEL), lambda t: (layer - 1, mod_row(t * tile), 0, 0))]
        args += [resid, mod]
    per_b = seq // tile
    in_specs += [pl.BlockSpec((tile, LANES), lambda t: (t % per_b, 0))] * 6
    args += list(rope_tabs)
    out_shape = [jax.ShapeDtypeStruct((n, Q_COLS), bf16), jax.ShapeDtypeStruct((n, KV_COLS), bf16)]
    out_specs = [pl.BlockSpec((tile, Q_COLS), row), pl.BlockSpec((tile, KV_COLS), row)]
    if resid is not None:
        out_shape.append(jax.ShapeDtypeStruct((n, D_MODEL), f32))
        out_specs.append(pl.BlockSpec((tile, D_MODEL), row))
    return pl.pallas_call(
        functools.partial(_pre_kernel, True, 0, resid is not None),
        out_shape=out_shape,
        grid=(n // tile,),
        in_specs=in_specs,
        out_specs=out_specs,
        scratch_shapes=[pltpu.VMEM((Z_COLS, D_MODEL), bf16)],
        compiler_params=_params(("arbitrary",)),
        name="pre_latent",
    )(*args)


PAST_CKV, PAST_KG, PAST_VG, PAST_KD, PAST_VD, PAST_KR, PAST_COLS = 0, 128, 256, 384, 640, 896, 928


def _cache_kernel(past_ref, wka_ref, wuv_ref, kv_ref):
    ckv_b = past_ref[0, 0, :, PAST_CKV:PAST_CKV + KV_RANK].astype(bf16)
    kr_b = past_ref[0, 0, :, PAST_KR:PAST_KR + MLA_ROPE].astype(bf16)
    wka = wka_ref[0]
    k_a = _dot(ckv_b, wka[:KV_RANK]) + _dot(kr_b, wka[KV_RANK:KV_RANK + MLA_ROPE])
    kv_ref[0, 0, :, KV_KA:KV_KA + 384] = k_a.astype(bf16)
    kv_ref[0, 0, :, KV_VA:KV_VA + 384] = _dot(ckv_b, wuv_ref[0].astype(bf16)).astype(bf16)
    kv_ref[0, 0, :, KV_KG:KV_KG + 128] = past_ref[0, 0, :, PAST_KG:PAST_KG + 128].astype(bf16)
    kv_ref[0, 0, :, KV_VG:KV_VG + 128] = past_ref[0, 0, :, PAST_VG:PAST_VG + 128].astype(bf16)
    kv_ref[0, 0, :, KV_KD:KV_KD + 256] = past_ref[0, 0, :, PAST_KD:PAST_KD + 256].astype(bf16)
    kv_ref[0, 0, :, KV_VD:KV_VD + 256] = past_ref[0, 0, :, PAST_VD:PAST_VD + 256].astype(bf16)


def _cache_rows(caches, w):
    ckv, kr, kg, vg, kd, vd = caches
    B = ckv.shape[0]
    flat = lambda a: a.reshape(B, DEPTH, PAST_LEN, -1)
    past = jnp.concatenate([flat(ckv), flat(kg), flat(vg), flat(kd), flat(vd), flat(kr)], axis=-1)
    return pl.pallas_call(
        _cache_kernel,
        out_shape=jax.ShapeDtypeStruct((DEPTH, B, PAST_LEN, KV_COLS), bf16),
        grid=(DEPTH, B),
        in_specs=[pl.BlockSpec((1, 1, PAST_LEN, PAST_COLS), lambda i, b: (b, i, 0, 0)),
                  pl.BlockSpec((1, 256, 384), lambda i, b: (i, 0, 0)), pl.BlockSpec((1, KV_RANK, 384), lambda i, b: (i, 0, 0))],
        out_specs=pl.BlockSpec((1, 1, PAST_LEN, KV_COLS), lambda i, b: (i, b, 0, 0)),
        compiler_params=_params(("arbitrary", "arbitrary")),
        name="cache_rows",
    )(past, w["w_ka"], w["w_uv"])


_SCORE_HEADS = (
    [(Q_A + 64 * h, KV_KA + 64 * h, 64, KV_VA + MLA_V * h) for h in range(MLA_HEADS)]
    + [(Q_G + 64 * h, KV_KG + 64 * (h // GQA_GROUP), 64, KV_VG + 64 * (h // GQA_GROUP)) for h in range(GQA_HEADS)]
    + [(Q_D + 64 * h + DIFF_QK * c, KV_KD + 64 * h + DIFF_QK * c, DIFF_QK, KV_VD + DIFF_V * h)
       for h in range(DIFF_HEADS) for c in range(2)])


def _attn_kernel(lam_init, per_round, n_src, q_ref, *refs):
    kv_refs = refs[:n_src]
    lam_ref, dg_ref, o_ref, s_ref, p_ref = refs[n_src:]
    spans, start = [], 0
    for r in kv_refs:
        spans.append((r, start, r.shape[2]))
        start += r.shape[2]
    mxu_sum = start >= MXU_SUM_MIN_KEYS

    outs = []
    for first in range(0, len(_SCORE_HEADS), per_round):
        chunk = _SCORE_HEADS[first:first + per_round]
        for j, (q_off, k_off, width, _) in enumerate(chunk):
            for r, lo, size in spans:
                s_ref[j, :, lo:lo + size] = _dot_nt(q_ref[:, q_off:q_off + width], r[0, 0, :, k_off:k_off + width])
        s = s_ref[...]
        p = jnp.exp2(s - jnp.max(s, axis=-1, keepdims=True))
        if mxu_sum:
            p_ref[...] = p.astype(bf16)
            for j, (_, _, _, v_off) in enumerate(chunk):
                o = sum(_dot(p_ref[j, :, lo:lo + size],
                             jnp.concatenate([r[0, 0, :, v_off:v_off + DIFF_V],
                                              jnp.ones((size, LANES - DIFF_V), bf16)], axis=1))
                        for r, lo, size in spans)
                outs.append((o * pltpu.roll(1.0 / o, DIFF_V, 1))[:, :DIFF_V])
        else:
            inv = 1.0 / jnp.sum(p, axis=-1, keepdims=True)
            p_ref[...] = p.astype(bf16)
            for j, (_, _, _, v_off) in enumerate(chunk):
                o = sum(_dot(p_ref[j, :, lo:lo + size], r[0, 0, :, v_off:v_off + DIFF_V]) for r, lo, size in spans)
                outs.append(o * inv[j])

    lp = lam_ref[0]
    e1 = jnp.exp(jnp.sum(lp[0:1] * lp[1:2], axis=-1, keepdims=True))
    e2 = jnp.exp(jnp.sum(lp[2:3] * lp[3:4], axis=-1, keepdims=True))
    lam = e1 - e2 + lam_init
    heads = outs[:MLA_HEADS + GQA_HEADS]
    for h in range(DIFF_HEADS):
        o1, o2 = outs[MLA_HEADS + GQA_HEADS + 2 * h:MLA_HEADS + GQA_HEADS + 2 * h + 2]
        heads.append(_rms(o1 - lam * o2, DIFF_V) * dg_ref[0] * (1.0 - lam_init))
    for j in range(len(heads) // 2):
        o_ref[:, LANES * j:LANES * (j + 1)] = jnp.concatenate(heads[2 * j:2 * j + 2], axis=1).astype(bf16)


def _attention(layer, q, sources, w, tile, per_round):
    n = q.shape[0]
    lam_init = 0.8 - 0.6 * math.exp(-0.3 * layer)
    s_kv = sum(rows for _, rows, _ in sources)
    assert len(_SCORE_HEADS) % per_round == 0
    return pl.pallas_call(
        functools.partial(_attn_kernel, lam_init, per_round, len(sources)),
        out_shape=jax.ShapeDtypeStruct((n, D_MODEL), bf16),
        scratch_shapes=[pltpu.VMEM((per_round, tile, s_kv), f32), pltpu.VMEM((per_round, tile, s_kv), bf16)],
        grid=(n // tile,),
        in_specs=[pl.BlockSpec((tile, Q_COLS), lambda t: (t, 0))]
        + [pl.BlockSpec((1, 1, rows, KV_COLS), index) for _, rows, index in sources]
        + [pl.BlockSpec((1, 4, DIFF_QK), lambda t: (layer, 0, 0)), pl.BlockSpec((1, 1, DIFF_V), lambda t: (layer, 0, 0))],
        out_specs=pl.BlockSpec((tile, D_MODEL), lambda t: (t, 0)),
        compiler_params=_params(("arbitrary",)),
        name="attention",
    )(q, *[arr for arr, _, _ in sources], w["lam"], w["diff_g"])


def _post_kernel(merge, *refs):
    it = iter(refs)
    o_ref, x_ref, mod_ref, w_out_ref, g2_ref, wg_ref, bg_ref, we_ref, be_ref = (next(it) for _ in range(9))
    if merge:
        next(it), next(it), next(it)
    x1_ref, hx_ref, cls_ref, w_scr = (next(it) for _ in range(4))

    @pl.when(pl.program_id(0) == 0)
    def _():
        w_scr[...] = w_out_ref[0].astype(bf16)

    gate1 = mod_ref[0, 0, 2:3, :]
    shift2 = mod_ref[0, 0, 3:4, :]
    scale2 = mod_ref[0, 0, 4:5, :]
    x1 = x_ref[...] + gate1 * _dot(o_ref[...], w_scr[...])
    x1_ref[...] = x1
    h2 = ((_rms(x1, D_MODEL) * g2_ref[0]) * (1.0 + scale2) + shift2).astype(bf16)
    bits = pltpu.bitcast(h2.astype(f32), i32)
    hx_ref[:, 0:HX_HALF] = bits[:, 0:HX_HALF] | jax.lax.shift_right_logical(bits[:, HX_HALF:D_MODEL], 16)

    def first_lane(mask, lane_f):
        return jnp.min(jnp.where(mask, lane_f, float(LANES)), axis=-1, keepdims=True)

    gl = _dot(h2, wg_ref[0].astype(bf16)) + bg_ref[0]
    glane = jax.lax.broadcasted_iota(i32, gl.shape, 1).astype(f32)
    ge = jnp.exp(gl - jnp.max(gl, axis=-1, keepdims=True))
    gprob = ge / jnp.sum(ge, axis=-1, keepdims=True)
    g_top = jnp.max(gprob, axis=-1, keepdims=True)
    g_idx = first_lane(gprob == g_top, glane)

    el = _dot(h2, we_ref[0].astype(bf16)) + be_ref[0]
    lane = jax.lax.broadcasted_iota(i32, el.shape, 1)
    lane_f = lane.astype(f32)
    emask = (lane >> 2).astype(f32) == g_idx
    em = jnp.where(emask, el, -jnp.inf)
    ee = jnp.where(emask, jnp.exp(em - jnp.max(em, axis=-1, keepdims=True)), 0.0)
    ep = ee / jnp.sum(ee, axis=-1, keepdims=True)
    p1 = jnp.max(jnp.where(emask, ep, -1.0), axis=-1, keepdims=True)
    i1 = first_lane(emask & (ep == p1), lane_f)
    rest = emask & (lane_f != i1)
    p2 = jnp.max(jnp.where(rest, ep, -1.0), axis=-1, keepdims=True)
    i2 = first_lane(rest & (ep == p2), lane_f)
    tot = p1 + p2
    w1 = g_top * (p1 / tot)
    w2 = g_top * (p2 / tot)

    lo = jnp.minimum(i1, i2) - EXPERTS_PER_GROUP * g_idx
    hi = jnp.maximum(i1, i2) - EXPERTS_PER_GROUP * g_idx
    pair = jnp.where(lo == 0.0, hi - 1.0, jnp.where(lo == 1.0, jnp.where(hi == 3.0, 3.0, 4.0), 5.0))
    cls_ref[...] = (N_PAIRS * g_idx + pair).astype(i32)
    g_lo = jnp.where(i1 < i2, w1, w2)
    g_hi = jnp.where(i1 < i2, w2, w1)
    tail_lane = jax.lax.broadcasted_iota(i32, (h2.shape[0], LANES), 1)
    hx_ref[:, HX_HALF:HX_COLS] = pltpu.bitcast(
        jnp.where(tail_lane == 0, g_lo, jnp.where(tail_lane == 1, g_hi, 0.0)), i32)


def _post(layer, o, x, row0, total, mod, mod_row, w, tile, merged=None):
    n = o.shape[0]
    lay = lambda t: (layer, 0, 0)
    row = lambda t: (t, 0)
    off = row0 // tile
    out_row = lambda t: (off + t, 0)
    in_specs = [
        pl.BlockSpec((tile, D_MODEL), row),
        pl.BlockSpec((tile, D_MODEL), row),
        pl.BlockSpec((1, 1, N_MOD, D_MODEL), lambda t: (layer, mod_row(t * tile), 0, 0)),
        pl.BlockSpec((1, D_MODEL, D_MODEL), lay),
        pl.BlockSpec((1, 1, D_MODEL), lay),
        pl.BlockSpec((1, D_MODEL, N_GROUPS), lay),
        pl.BlockSpec((1, 1, N_GROUPS), lay),
        pl.BlockSpec((1, D_MODEL, N_EXPERTS), lay),
        pl.BlockSpec((1, 1, N_EXPERTS), lay),
    ]
    args = [o, x, mod, w["w_out"], w["g2"], w["w_grp"], w["b_grp"], w["w_rtr"], w["b_rtr"]]
    aliases = {}
    if merged is not None:
        aliases = {len(args) + j: j for j in range(3)}
        in_specs += [pl.BlockSpec(memory_space=pl.ANY)] * 3
        args += list(merged)
    return pl.pallas_call(
        functools.partial(_post_kernel, merged is not None),
        out_shape=[jax.ShapeDtypeStruct((total, D_MODEL), f32), jax.ShapeDtypeStruct((total, HX_COLS), i32),
                   jax.ShapeDtypeStruct((total, 1), i32)],
        grid=(n // tile,),
        in_specs=in_specs,
        out_specs=[pl.BlockSpec((tile, D_MODEL), out_row), pl.BlockSpec((tile, HX_COLS), out_row),
                   pl.BlockSpec((tile, 1), out_row)],
        scratch_shapes=[pltpu.VMEM((D_MODEL, D_MODEL), bf16)],
        input_output_aliases=aliases,
        compiler_params=_params(("arbitrary",)),
        name="post_attention",
    )(*args)


def _ctx_kernel(lam_init, n_prev, resid, *refs):
    it = iter(refs)
    pre_in = [next(it) for _ in range(9 + (2 if resid else 0) + n_prev)]
    lam_ref, dg_ref = next(it), next(it)
    post_w = [next(it) for _ in range(6)]
    cache_refs = [next(it) for _ in range(len(CACHE_WIDTHS))]
    x1_ref, hx_ref, cls_ref = (next(it) for _ in range(3))
    w_in_scr, q_scr, kv_scr, o_scr, s_scr, p_scr, w_out_scr, x2_scr = (next(it) for _ in range(8))
    x_ref, mod_ref = pre_in[0], pre_in[1]
    x2_ref = [x2_scr] if resid else []

    _pre_kernel(False, n_prev, resid, *pre_in, q_scr, kv_scr.at[0, 0], *x2_ref, *cache_refs, w_in_scr)
    _attn_kernel(lam_init, len(_SCORE_HEADS), 1, q_scr, kv_scr, lam_ref, dg_ref, o_scr, s_scr, p_scr)
    _post_kernel(False, o_scr, x2_scr if resid else x_ref, mod_ref, *post_w, x1_ref, hx_ref, cls_ref, w_out_scr)


def _ctx_layer(layer, x, n, seq, total, mod, w, prev_cache=(), resid=None):
    lay = lambda t: (layer, 0, 0)
    row = lambda t: (t, 0)
    mod_spec = lambda l: pl.BlockSpec((1, 1, N_MOD, D_MODEL), lambda t: (l, 0, 0, 0))
    in_specs = [
        pl.BlockSpec((seq, D_MODEL), row), mod_spec(layer),
        pl.BlockSpec((1, 1, D_MODEL), lay), pl.BlockSpec((1, IN_COLS, D_MODEL), lay),
        pl.BlockSpec((1, 1, KV_RANK), lay), pl.BlockSpec((1, 1, 512), lay), pl.BlockSpec((512, 512), lambda t: (0, 0)),
        pl.BlockSpec((1, 256, 384), lay), pl.BlockSpec((1, KV_RANK, 384), lay),
    ]
    args = [x, mod, w["g1"], w["w_in"], w["kv_g"], w["qk_g"], w["seg"], w["w_ka"], w["w_uv"]]
    if resid is not None:
        in_specs += [pl.BlockSpec((seq, D_MODEL), row), mod_spec(layer - 1)]
        args += [resid, mod]
    out_shape, out_specs, aliases = [], [], {}
    for j, width in enumerate(CACHE_WIDTHS):
        if prev_cache:
            aliases[len(args)] = len(out_shape)
            in_specs.append(pl.BlockSpec(memory_space=pl.ANY))
            args.append(prev_cache[j])
        out_shape.append(jax.ShapeDtypeStruct((n // seq, DEPTH, seq, width), f32))
        out_specs.append(pl.BlockSpec((1, 1, seq, width), lambda t: (t, layer, 0, 0)))
    in_specs += [
        pl.BlockSpec((1, 4, DIFF_QK), lay), pl.BlockSpec((1, 1, DIFF_V), lay),
        pl.BlockSpec((1, D_MODEL, D_MODEL), lay), pl.BlockSpec((1, 1, D_MODEL), lay),
        pl.BlockSpec((1, D_MODEL, N_GROUPS), lay), pl.BlockSpec((1, 1, N_GROUPS), lay),
        pl.BlockSpec((1, D_MODEL, N_EXPERTS), lay), pl.BlockSpec((1, 1, N_EXPERTS), lay),
    ]
    args += [w["lam"], w["diff_g"], w["w_out"], w["g2"], w["w_grp"], w["b_grp"], w["w_rtr"], w["b_rtr"]]
    out_shape += [jax.ShapeDtypeStruct((total, D_MODEL), f32), jax.ShapeDtypeStruct((total, HX_COLS), i32),
                  jax.ShapeDtypeStruct((total, 1), i32)]
    out_specs += [pl.BlockSpec((seq, D_MODEL), row), pl.BlockSpec((seq, HX_COLS), row), pl.BlockSpec((seq, 1), row)]
    heads = len(_SCORE_HEADS)
    return pl.pallas_call(
        functools.partial(_ctx_kernel, 0.8 - 0.6 * math.exp(-0.3 * layer), len(prev_cache), resid is not None),
        out_shape=out_shape,
        grid=(n // seq,),
        in_specs=in_specs,
        out_specs=out_specs,
        scratch_shapes=[pltpu.VMEM((Z_COLS, D_MODEL), bf16), pltpu.VMEM((seq, Q_COLS), bf16),
                        pltpu.VMEM((1, 1, seq, KV_COLS), bf16), pltpu.VMEM((seq, D_MODEL), bf16),
                        pltpu.VMEM((heads, seq, seq), f32), pltpu.VMEM((heads, seq, seq), bf16),
                        pltpu.VMEM((D_MODEL, D_MODEL), bf16), pltpu.VMEM((seq, D_MODEL), f32)],
        input_output_aliases=aliases,
        compiler_params=_params(("arbitrary",)),
        name="context_layer",
    )(*args)


PLAN_CHUNK = 1024
TAB_ROWS = LANES


def _plan_kernel(cls_ref, slot_ref, tab_ref, rank_scr):
    n = cls_ref.shape[0]
    lane = jax.lax.broadcasted_iota(i32, (PLAN_CHUNK, LANES), 1)
    r = jax.lax.broadcasted_iota(i32, (PLAN_CHUNK, PLAN_CHUNK), 0)
    c = jax.lax.broadcasted_iota(i32, (PLAN_CHUNK, PLAN_CHUNK), 1)
    before = (c < r).astype(bf16)

    def count(b, seen):
        rows = pl.ds(pl.multiple_of(b * PLAN_CHUNK, PLAN_CHUNK), PLAN_CHUNK)
        onehot = (cls_ref[rows, :] == lane).astype(f32)
        ahead = _dot(before, onehot.astype(bf16)) + seen
        rank_scr[rows, :] = jnp.sum(onehot * ahead, axis=-1, keepdims=True)
        return seen + jnp.sum(onehot, axis=0, keepdims=True)

    counts = jax.lax.fori_loop(0, n // PLAN_CHUNK, count, jnp.zeros((1, LANES), f32))
    tiles = jnp.floor((counts + (EXPERT_TILE - 1)) * (1.0 / EXPERT_TILE))
    rr = jax.lax.broadcasted_iota(i32, (LANES, LANES), 0)
    cc = jax.lax.broadcasted_iota(i32, (LANES, LANES), 1)
    ends = _dot(jnp.broadcast_to(tiles, (8, LANES)).astype(bf16), (rr <= cc).astype(bf16))[0:1]
    starts = ends - tiles

    def place(b, carry):
        rows = pl.ds(pl.multiple_of(b * PLAN_CHUNK, PLAN_CHUNK), PLAN_CHUNK)
        first = jnp.sum(jnp.where(cls_ref[rows, :] == lane, starts, 0.0), axis=-1, keepdims=True)
        slot_ref[rows, :] = (first * EXPERT_TILE + rank_scr[rows, :]).astype(i32)
        return carry

    jax.lax.fori_loop(0, n // PLAN_CHUNK, place, 0)

    tl = jax.lax.broadcasted_iota(i32, (TAB_ROWS, LANES), 1)
    n_tiles = jnp.sum(jnp.where(tl[0:1] == N_CLASSES - 1, ends, 0.0), axis=-1, keepdims=True)
    k = jnp.minimum(jax.lax.broadcasted_iota(i32, (TAB_ROWS, 1), 0).astype(f32), n_tiles - 1.0)
    cls_k = jnp.sum(jnp.where((tl < N_CLASSES) & (ends <= k), 1.0, 0.0), axis=-1, keepdims=True)
    cls_k = jnp.minimum(cls_k, N_CLASSES - 1.0)
    mine = tl.astype(f32) == cls_k
    used = jnp.sum(jnp.where(mine, counts, 0.0), axis=-1, keepdims=True)
    first = jnp.sum(jnp.where(mine, starts, 0.0), axis=-1, keepdims=True)
    valid = jnp.clip(used - (k - first) * EXPERT_TILE, 0.0, float(EXPERT_TILE))
    group = jnp.floor((cls_k + 0.5) * (1.0 / N_PAIRS))
    pair = cls_k - N_PAIRS * group
    lo = hi = jnp.zeros_like(pair)
    for p in range(N_PAIRS):
        lo = jnp.where(pair == p, float(PAIR_LO[p]), lo)
        hi = jnp.where(pair == p, float(PAIR_HI[p]), hi)
    e_lo = EXPERTS_PER_GROUP * group + lo
    e_hi = EXPERTS_PER_GROUP * group + hi
    tab = jnp.where(tl == 0, e_lo, jnp.where(tl == 1, e_hi, jnp.where(tl == 2, valid, jnp.where(tl == 3, n_tiles, 0.0))))
    tab_ref[...] = tab.astype(i32)


def _plan(cls, max_tiles):
    n = cls.shape[0]
    assert n % PLAN_CHUNK == 0 and max_tiles <= TAB_ROWS
    slot, tab = pl.pallas_call(
        _plan_kernel,
        out_shape=[jax.ShapeDtypeStruct((n, 1), i32), jax.ShapeDtypeStruct((TAB_ROWS, LANES), i32)],
        scratch_shapes=[pltpu.VMEM((n, 1), f32)],
        compiler_params=_params(None),
        name="dispatch_plan",
    )(cls)
    return slot.reshape(n), tab[:max_tiles, 0], tab[:max_tiles, 1], tab[:max_tiles, 2], tab[0, 3:4]


def _move_rows(src, idx, n_out, scatter):
    n = idx.shape[0]
    width = src.shape[1]
    per_worker = n // (SC_CORES * SC_SUBCORES)
    assert n % SC_ROWS == 0
    chunk = max(c for c in (64, 40, 32, 16, 8)
                if per_worker % c == 0 and 2 * c * width * src.dtype.itemsize <= SC_BUFFER_BYTES)
    n_chunks = per_worker // chunk
    mesh = plsc.VectorSubcoreMesh(core_axis_name="c", subcore_axis_name="s")

    @functools.partial(
        pl.kernel, mesh=mesh, out_type=jax.ShapeDtypeStruct((n_out, width), src.dtype),
        scratch_types=[pltpu.VMEM((chunk,), i32), pltpu.VMEM((chunk,), i32),
                       pltpu.VMEM((chunk, width), src.dtype), pltpu.VMEM((chunk, width), src.dtype),
                       pltpu.SemaphoreType.DMA, pltpu.SemaphoreType.DMA, pltpu.SemaphoreType.DMA,
                       pltpu.SemaphoreType.DMA])
    def move(src_hbm, idx_hbm, out_hbm, idx0, idx1, rows0, rows1, in0, in1, out0, out1):
        wid = jax.lax.axis_index("s") * SC_CORES + jax.lax.axis_index("c")
        base = wid * per_worker
        idx_v, rows_v, sem_in, sem_out = (idx0, idx1), (rows0, rows1), (in0, in1), (out0, out1)

        def fill(j):
            b = j % 2
            rows = pl.ds(base + j * chunk, chunk)
            pltpu.sync_copy(idx_hbm.at[rows], idx_v[b])
            src_rows = src_hbm.at[rows] if scatter else src_hbm.at[idx_v[b]]
            return pltpu.async_copy(src_rows, rows_v[b], sem_in[b])

        def drain(j):
            b = j % 2
            dst_rows = out_hbm.at[idx_v[b]] if scatter else out_hbm.at[pl.ds(base + j * chunk, chunk)]
            return pltpu.async_copy(rows_v[b], dst_rows, sem_out[b])

        fills, drains = {0: fill(0)}, {}
        for j in range(n_chunks):
            if j + 1 < n_chunks:
                if j >= 1:
                    drains[j - 1].wait()
                fills[j + 1] = fill(j + 1)
            fills[j].wait()
            drains[j] = drain(j)
        for j in range(max(n_chunks - 2, 0), n_chunks):
            drains[j].wait()

    return move(src, idx)


def _expert_kernel(lo_ref, hi_ref, valid_ref, nt_ref, xs_ref, wg_ref, wu_ref, wd_ref, ys_ref, wg_scr, wu_scr, wd_scr):
    k = pl.program_id(0)
    prev = jnp.maximum(k - 1, 0)

    @pl.when((k == 0) | (lo_ref[k] // EXPERTS_PER_GROUP != lo_ref[prev] // EXPERTS_PER_GROUP))
    def _():
        for j in range(EXPERTS_PER_GROUP):
            wg_scr[j] = wg_ref[0, 0, j].astype(bf16)
            wu_scr[j] = wu_ref[0, 0, j].astype(bf16)
            wd_scr[j] = wd_ref[0, 0, j].astype(bf16)

    @pl.when(k < nt_ref[0])
    def _():
        live = jax.lax.broadcasted_iota(i32, (EXPERT_TILE, 1), 0) < valid_ref[k]
        words = jnp.where(live, xs_ref[:, 0:HX_HALF], 0)
        x = jnp.concatenate([pltpu.bitcast(words & -65536, f32), pltpu.bitcast(words << 16, f32)], axis=1).astype(bf16)
        gates = pltpu.bitcast(jnp.where(live, xs_ref[:, HX_HALF:HX_COLS], 0), f32)
        y = None
        for lane, e_ref in enumerate((lo_ref, hi_ref)):
            j = e_ref[k] % EXPERTS_PER_GROUP
            hid = _silu(_dot(x, wg_scr[j])) * _dot(x, wu_scr[j]) * gates[:, lane:lane + 1]
            part = _dot(hid.astype(bf16), wd_scr[j])
            y = part if y is None else y + part
        ys_ref[...] = y


def _experts(layer, xs, e_lo, e_hi, valid, n_tiles, w):
    max_tiles = e_lo.shape[0]
    row = lambda k, lo, hi, valid, nt: (jnp.minimum(k, nt[0] - 1), 0)
    group = lambda k, lo, hi, valid, nt: (layer, lo[k] // EXPERTS_PER_GROUP, 0, 0, 0)
    by_group = lambda a: a.reshape(DEPTH, N_GROUPS, EXPERTS_PER_GROUP, *a.shape[2:])
    return pl.pallas_call(
        _expert_kernel,
        out_shape=jax.ShapeDtypeStruct((max_tiles * EXPERT_TILE, D_MODEL), f32),
        grid_spec=pltpu.PrefetchScalarGridSpec(
            num_scalar_prefetch=4, grid=(max_tiles,),
            in_specs=[pl.BlockSpec((EXPERT_TILE, HX_COLS), row),
                      pl.BlockSpec((1, 1, EXPERTS_PER_GROUP, D_MODEL, D_FF_EXPERT), group),
                      pl.BlockSpec((1, 1, EXPERTS_PER_GROUP, D_MODEL, D_FF_EXPERT), group),
                      pl.BlockSpec((1, 1, EXPERTS_PER_GROUP, D_FF_EXPERT, D_MODEL), group)],
            out_specs=pl.BlockSpec((EXPERT_TILE, D_MODEL), row),
            scratch_shapes=[pltpu.VMEM((EXPERTS_PER_GROUP, D_MODEL, D_FF_EXPERT), bf16),
                            pltpu.VMEM((EXPERTS_PER_GROUP, D_MODEL, D_FF_EXPERT), bf16),
                            pltpu.VMEM((EXPERTS_PER_GROUP, D_FF_EXPERT, D_MODEL), bf16)]),
        compiler_params=_params(("arbitrary",)),
        name="experts",
    )(e_lo, e_hi, valid, n_tiles, xs, by_group(w["w_gate"]), by_group(w["w_up"]), by_group(w["w_down"]))


def _final_kernel(x1_ref, y_ref, mod_ref, fg_ref, o_ref):
    o_ref[...] = _rms(x1_ref[...] + mod_ref[0, 0, 5:6, :] * y_ref[...], D_MODEL) * fg_ref[...]


def _final(x1, y, n, row0, mod, mod_row, w, tile):
    off = row0 // tile
    src_row = lambda t: (off + t, 0)
    return pl.pallas_call(
        _final_kernel,
        out_shape=jax.ShapeDtypeStruct((n, D_MODEL), f32),
        grid=(n // tile,),
        in_specs=[pl.BlockSpec((tile, D_MODEL), src_row), pl.BlockSpec((tile, D_MODEL), src_row),
                  pl.BlockSpec((1, 1, N_MOD, D_MODEL), lambda t: (DEPTH - 1, mod_row(t * tile), 0, 0)),
                  pl.BlockSpec((1, D_MODEL), lambda t: (0, 0))],
        out_specs=pl.BlockSpec((tile, D_MODEL), lambda t: (t, 0)),
        compiler_params=_params(("arbitrary",)),
        name="final_norm",
    )(x1, y, mod, w["final_g"])


def _rope_tables(n_tokens):
    pos = np.arange(n_tokens)
    row = (pos // GRID_W).astype(np.float64)
    col = (pos % GRID_W).astype(np.float64)

    def cs(rot_dim):
        quarter = rot_dim // 4
        inv = ROPE_THETA ** (-np.arange(quarter, dtype=np.float64) / quarter)
        ang = np.concatenate([row[:, None] * inv, col[:, None] * inv], axis=-1)
        return np.cos(ang), np.sin(ang)

    c32, s32 = cs(MLA_ROPE)
    c64, s64 = cs(HEAD_DIM)
    ones = np.ones((n_tokens, MLA_NOPE))
    zeros = np.zeros((n_tokens, MLA_NOPE))

    def rep(parts):
        period = np.concatenate(parts, axis=-1)
        return jnp.asarray(np.tile(period, (1, LANES // period.shape[-1])), f32)

    return (rep([ones, c32, c32]), rep([zeros, -s32, s32]), rep([c32, c32]), rep([-s32, s32]),
            rep([c64, c64]), rep([-s64, s64]))


def _layout_weights(norm1_g, norm2_g, w_in, mla_kv_norm_g, mla_w_uk, mla_w_uv, gqa_q_norm_g, gqa_k_norm_g,
                    diff_lambda, diff_norm_g, w_out, moe_w_group, moe_b_group, moe_w_router, moe_b_router,
                    moe_w_gate, moe_w_up, moe_w_down, final_norm_g):
    eye = jnp.eye(MLA_ROPE, dtype=f32)
    top = jnp.concatenate([mla_w_uk, jnp.zeros((DEPTH, KV_RANK, MLA_HEADS, MLA_ROPE), f32)], axis=-1)
    mid = jnp.concatenate([jnp.zeros((MLA_ROPE, MLA_HEADS, MLA_NOPE), f32),
                           jnp.broadcast_to(eye[:, None, :], (MLA_ROPE, MLA_HEADS, MLA_ROPE))], axis=-1)
    w_ka = jnp.concatenate([top.reshape(DEPTH, KV_RANK, 384),
                            jnp.broadcast_to(mid.reshape(1, MLA_ROPE, 384), (DEPTH, MLA_ROPE, 384)),
                            jnp.zeros((DEPTH, 256 - KV_RANK - MLA_ROPE, 384), f32)], axis=1).astype(bf16)
    seg_id = np.arange(512) // HEAD_DIM
    seg = jnp.asarray(seg_id[:, None] == seg_id[None, :], bf16)
    qk_g = jnp.concatenate([jnp.tile(gqa_q_norm_g, (1, GQA_HEADS)), jnp.tile(gqa_k_norm_g, (1, GQA_KV_HEADS))], axis=-1)
    return dict(
        g1=norm1_g.reshape(DEPTH, 1, D_MODEL), g2=norm2_g.reshape(DEPTH, 1, D_MODEL),
        w_in=jnp.swapaxes(w_in, 1, 2),
        kv_g=mla_kv_norm_g.reshape(DEPTH, 1, KV_RANK), qk_g=qk_g.reshape(DEPTH, 1, 512), seg=seg, w_ka=w_ka,
        w_uv=mla_w_uv.reshape(DEPTH, KV_RANK, 384), lam=diff_lambda, diff_g=diff_norm_g.reshape(DEPTH, 1, DIFF_V),
        w_out=w_out, w_grp=moe_w_group, b_grp=moe_b_group.reshape(DEPTH, 1, N_GROUPS), w_rtr=moe_w_router,
        b_rtr=moe_b_router.reshape(DEPTH, 1, N_EXPERTS), w_gate=moe_w_gate, w_up=moe_w_up, w_down=moe_w_down,
        final_g=final_norm_g.reshape(1, D_MODEL))


PRE_TILE = 512
LAT_ATTN_TILE = 256
LAT_HEADS_PER_ROUND = 5
MXU_SUM_MIN_KEYS = 1024
POST_TILE = 512
FINAL_TILE = 1024


def kernel(x_prompt, x_sample, c, cache_mla_ckv, cache_mla_krope, cache_gqa_k, cache_gqa_v, cache_diff_k, cache_diff_v, c_ctx, norm1_g, norm2_g, w_mod, b_mod, w_in, mla_kv_norm_g, mla_w_uk, mla_w_uv, gqa_q_norm_g, gqa_k_norm_g, diff_lambda, diff_norm_g, w_out, moe_w_group, moe_b_group, moe_w_router, moe_b_router, moe_w_gate, moe_w_up, moe_w_down, final_norm_g):
    B, S, _ = x_prompt.shape
    Bl, Sl, _ = x_sample.shape
    n_ctx, n_lat = B * S, Bl * Sl
    total = n_ctx + n_lat
    assert Bl + 1 <= MOD_ROWS and DEPTH == 2 and total % SC_ROWS == 0
    slot_rows = -(-(total + N_CLASSES * EXPERT_TILE) // SC_ROWS) * SC_ROWS
    max_tiles = slot_rows // EXPERT_TILE
    w = _layout_weights(norm1_g, norm2_g, w_in, mla_kv_norm_g, mla_w_uk, mla_w_uv, gqa_q_norm_g, gqa_k_norm_g,
                        diff_lambda, diff_norm_g, w_out, moe_w_group, moe_b_group, moe_w_router, moe_b_router,
                        moe_w_gate, moe_w_up, moe_w_down, final_norm_g)
    cond = jnp.concatenate([c_ctx[None, :], c, jnp.zeros((MOD_ROWS - 1 - Bl, D_MODEL), f32)], axis=0)
    mod = _modulation(cond, w_mod, b_mod).reshape(DEPTH, MOD_ROWS, N_MOD, D_MODEL)
    ctx_row = lambda token: 0
    lat_row = lambda token: 1 + token // Sl
    tabs = _rope_tables(Sl)
    kv_past = _cache_rows((cache_mla_ckv, cache_mla_krope, cache_gqa_k, cache_gqa_v, cache_diff_k, cache_diff_v), w)
    per_b = Sl // LAT_ATTN_TILE

    x_ctx, x_lat = x_prompt.reshape(n_ctx, D_MODEL), x_sample.reshape(n_lat, D_MODEL)
    cache = ()
    x1 = y = None
    for i in range(DEPTH):
        if i == 0:
            *cache, x1_c, hx_c, cls_c = _ctx_layer(i, x_ctx, n_ctx, S, total, mod, w)
        else:
            *cache, x1_c, hx_c, cls_c = _ctx_layer(i, x1, n_ctx, S, total, mod, w, prev_cache=cache, resid=y)
        if i == 0:
            q_l, kv_l = _pre_latent(i, x_lat, n_lat, 0, Sl, mod, lat_row, w, PRE_TILE, tabs)
        else:
            q_l, kv_l, x_lat = _pre_latent(i, x1, n_lat, n_ctx, Sl, mod, lat_row, w, PRE_TILE, tabs, resid=y)
        past = (kv_past, PAST_LEN, lambda t, i=i: (i, t // per_b, 0, 0))
        own = (kv_l.reshape(1, Bl, Sl, KV_COLS), Sl, lambda t: (0, t // per_b, 0, 0))
        o_l = _attention(i, q_l, [past, own], w, LAT_ATTN_TILE, LAT_HEADS_PER_ROUND)
        x1, hx, cls = _post(i, o_l, x_lat, n_ctx, total, mod, lat_row, w, POST_TILE, merged=(x1_c, hx_c, cls_c))
        if i == DEPTH - 1:
            outs = (cache[0], cache[1], cache[2].reshape(B, DEPTH, S, GQA_KV_HEADS, HEAD_DIM),
                    cache[3].reshape(B, DEPTH, S, GQA_KV_HEADS, HEAD_DIM),
                    cache[4].reshape(B, DEPTH, S, DIFF_HEADS, 2, DIFF_QK),
                    cache[5].reshape(B, DEPTH, S, DIFF_HEADS, DIFF_V))
            cls, outs = jax.lax.optimization_barrier((cls, outs))
        slot, e_lo, e_hi, valid, n_tiles = _plan(cls, max_tiles)
        xs = _move_rows(hx, slot, slot_rows, scatter=True)
        ys = _experts(i, xs, e_lo, e_hi, valid, n_tiles, w)
        y = _move_rows(ys, slot, total, scatter=False)

    y_prompt = _final(x1, y, n_ctx, 0, mod, ctx_row, w, FINAL_TILE).reshape(B, S, D_MODEL)
    y_sample = _final(x1, y, n_lat, n_ctx, mod, lat_row, w, FINAL_TILE).reshape(Bl, Sl, D_MODEL)
    return (y_prompt, y_sample, *outs)
```

```python
import functools
import math

import jax
import jax.numpy as jnp
import numpy as np
from jax.experimental import pallas as pl
from jax.experimental.pallas import tpu as pltpu
from jax.experimental.pallas import tpu_sc as plsc

D_MODEL = 1024
DEPTH = 2
PAST_LEN = 512
GRID_W = 64
ROPE_THETA = 10000.0
EPS = 1e-6
LOG2E = 1.4426950408889634
N_MOD = 6
HEAD_DIM = 64
MLA_HEADS = 6
MLA_NOPE = 32
MLA_ROPE = 32
MLA_V = 64
KV_RANK = 128
GQA_HEADS = 6
GQA_KV_HEADS = 2
GQA_GROUP = GQA_HEADS // GQA_KV_HEADS
DIFF_HEADS = 4
DIFF_QK = 32
DIFF_V = 64
N_GROUPS = 4
EXPERTS_PER_GROUP = 4
N_EXPERTS = N_GROUPS * EXPERTS_PER_GROUP
D_FF_EXPERT = 256

LANES = 128
MOD_ROWS = 8

IN_COLS = 1952
IN_KR = 512
Z_QA, Z_CKV, Z_QG, Z_KG, Z_VG, Z_QD, Z_KD, Z_VD, Z_KR = 0, 384, 512, 896, 1024, 1152, 1408, 1664, 1920
Z_COLS = 2048
Q_A, Q_G, Q_D, Q_COLS = 0, 384, 768, 1024
KV_KA, KV_VA, KV_KG, KV_VG, KV_KD, KV_VD, KV_COLS = 0, 384, 768, 896, 1024, 1280, 1536
CACHE_WIDTHS = (128, 32, 128, 128, 256, 256)

PAIR_LO = (0, 0, 0, 1, 1, 2)
PAIR_HI = (1, 2, 3, 3, 2, 3)
N_PAIRS = len(PAIR_LO)
N_CLASSES = N_GROUPS * N_PAIRS
HX_HALF = D_MODEL // 2
HX_COLS = HX_HALF + LANES
EXPERT_TILE = 512

SC_CORES, SC_SUBCORES = 2, 16
SC_BUFFER_BYTES = 400 * 1024
SC_ROWS = SC_CORES * SC_SUBCORES * 8

VMEM_LIMIT = 56 * 1024 * 1024

bf16 = jnp.bfloat16
f32 = jnp.float32
i32 = jnp.int32


def _dot(a, b):
    return jnp.dot(a, b, preferred_element_type=f32)


def _dot_nt(a, b):
    return jax.lax.dot_general(a, b, (((1,), (1,)), ((), ())), preferred_element_type=f32)


def _rms(x, width):
    return x * jax.lax.rsqrt(jnp.sum(x * x, axis=-1, keepdims=True) * (1.0 / width) + EPS)


def _silu(x):
    return x * (1.0 / (1.0 + jnp.exp(-x)))


def _params(sem):
    return pltpu.CompilerParams(dimension_semantics=sem, vmem_limit_bytes=VMEM_LIMIT)


def _mod_kernel(cond_ref, w_ref, b_ref, o_ref):
    o_ref[0] = _dot(_silu(cond_ref[...]).astype(bf16), w_ref[0].astype(bf16)) + b_ref[0]


def _modulation(cond, w_mod, b_mod):
    return pl.pallas_call(
        _mod_kernel,
        out_shape=jax.ShapeDtypeStruct((DEPTH, MOD_ROWS, N_MOD * D_MODEL), f32),
        grid=(DEPTH, N_MOD),
        in_specs=[
            pl.BlockSpec((MOD_ROWS, D_MODEL), lambda i, j: (0, 0)),
            pl.BlockSpec((1, D_MODEL, D_MODEL), lambda i, j: (i, 0, j)),
            pl.BlockSpec((1, 1, D_MODEL), lambda i, j: (i, 0, j)),
        ],
        out_specs=pl.BlockSpec((1, MOD_ROWS, D_MODEL), lambda i, j: (i, 0, j)),
        compiler_params=_params(("arbitrary", "arbitrary")),
        name="modulation",
    )(cond, w_mod, b_mod.reshape(DEPTH, 1, N_MOD * D_MODEL))


def _swap_halves(x, half):
    lane = jax.lax.broadcasted_iota(i32, x.shape, 1)
    fwd = pltpu.roll(x, LANES - half, 1)
    bwd = pltpu.roll(x, half, 1)
    return jnp.where((lane & (2 * half - 1)) < half, fwd, bwd)


def _rope_block(x, cos, sin, half):
    return x * cos + _swap_halves(x, half) * sin


def _pre_kernel(rope, n_prev, resid, *refs):
    it = iter(refs)
    x_ref, mod_ref, g1_ref, w_in_ref, kvg_ref, qkg_ref, seg_ref, wka_ref, wuv_ref = (next(it) for _ in range(9))
    if resid:
        y_ref, pmod_ref = next(it), next(it)
    if rope:
        ca_ref, sa_ref, c32_ref, s32_ref, c64_ref, s64_ref = (next(it) for _ in range(6))
    prev_refs = [next(it) for _ in range(n_prev)]
    q_ref, kv_ref = next(it), next(it)
    if resid:
        x2_ref = next(it)
    cache_refs = [] if rope else [next(it) for _ in range(len(CACHE_WIDTHS))]
    w_scr = next(it)

    @pl.when(pl.program_id(0) == 0)
    def _():
        w_scr[0:IN_KR] = w_in_ref[0, 0:IN_KR].astype(bf16)
        w_scr[IN_KR:Z_KR] = w_in_ref[0, IN_KR + MLA_ROPE:IN_COLS].astype(bf16)
        w_scr[Z_KR:Z_KR + MLA_ROPE] = w_in_ref[0, IN_KR:IN_KR + MLA_ROPE].astype(bf16)
        w_scr[Z_KR + MLA_ROPE:Z_COLS] = jnp.zeros((Z_COLS - Z_KR - MLA_ROPE, D_MODEL), bf16)

    x = x_ref[...]
    if resid:
        x = x + pmod_ref[0, 0, 5:6, :] * y_ref[...]
        x2_ref[...] = x
    shift1 = mod_ref[0, 0, 0:1, :]
    scale1 = mod_ref[0, 0, 1:2, :]
    h = (_rms(x, D_MODEL) * g1_ref[0]) * (1.0 + scale1) + shift1
    z = _dot_nt(h.astype(bf16), w_scr[...])

    ckv = _rms(z[:, Z_CKV:Z_CKV + KV_RANK], KV_RANK) * kvg_ref[0]

    qk = z[:, Z_QG:Z_VG]
    sq = qk * qk
    sq_hi = sq.astype(bf16)
    sq_lo = (sq - sq_hi.astype(f32)).astype(bf16)
    seg = seg_ref[...]
    ms = (_dot(sq_hi, seg) + _dot(sq_lo, seg)) * (1.0 / HEAD_DIM)
    qk = qk * jax.lax.rsqrt(ms + EPS) * qkg_ref[0]

    def blocks(arr, n):
        return [arr[:, LANES * j:LANES * (j + 1)] for j in range(n)]

    qa = blocks(z[:, Z_QA:Z_QA + 384], 3)
    qkb = blocks(qk, 4)
    qd = blocks(z[:, Z_QD:Z_QD + 256], 2)
    kd = blocks(z[:, Z_KD:Z_KD + 256], 2)
    kr = z[:, Z_KR:Z_KR + LANES]
    if rope:
        ca, sa, c32, s32, c64, s64 = (r[...] for r in (ca_ref, sa_ref, c32_ref, s32_ref, c64_ref, s64_ref))
        qa = [_rope_block(b, ca, sa, MLA_ROPE // 2) for b in qa]
        qkb = [_rope_block(b, c64, s64, HEAD_DIM // 2) for b in qkb]
        qd = [_rope_block(b, c32, s32, DIFF_QK // 2) for b in qd]
        kd = [_rope_block(b, c32, s32, DIFF_QK // 2) for b in kd]
        kr = _rope_block(kr, c32, s32, MLA_ROPE // 2)

    vg = z[:, Z_VG:Z_VG + 128]
    vd = z[:, Z_VD:Z_VD + 256]
    ckv_b = ckv.astype(bf16)
    k_a = _dot(jnp.concatenate([ckv_b, kr.astype(bf16)], axis=1), wka_ref[0])
    v_a = _dot(ckv_b, wuv_ref[0].astype(bf16))

    for j in range(3):
        q_ref[:, Q_A + LANES * j:Q_A + LANES * (j + 1)] = (qa[j] * (HEAD_DIM ** -0.5 * LOG2E)).astype(bf16)
        q_ref[:, Q_G + LANES * j:Q_G + LANES * (j + 1)] = (qkb[j] * (HEAD_DIM ** -0.5 * LOG2E)).astype(bf16)
    for j in range(2):
        q_ref[:, Q_D + LANES * j:Q_D + LANES * (j + 1)] = (qd[j] * (DIFF_QK ** -0.5 * LOG2E)).astype(bf16)
        kv_ref[:, KV_KD + LANES * j:KV_KD + LANES * (j + 1)] = kd[j].astype(bf16)
    kv_ref[:, KV_KA:KV_KA + 384] = k_a.astype(bf16)
    kv_ref[:, KV_VA:KV_VA + 384] = v_a.astype(bf16)
    kv_ref[:, KV_KG:KV_KG + 128] = qkb[3].astype(bf16)
    kv_ref[:, KV_VG:KV_VG + 128] = vg.astype(bf16)
    kv_ref[:, KV_VD:KV_VD + 256] = vd.astype(bf16)
    if not rope:
        rows = [ckv, kr[:, :MLA_ROPE], qkb[3], vg, jnp.concatenate(kd, axis=1), vd]
        for out, new in zip(cache_refs, rows):
            reqs, _, seq, width = out.shape
            out[:, 0] = new.reshape(reqs, seq, width)


def _pre_latent(layer, x, n, row0, seq, mod, mod_row, w, tile, rope_tabs, resid=None):
    lay = lambda t: (layer, 0, 0)
    row = lambda t: (t, 0)
    off = row0 // tile
    src_row = lambda t: (off + t, 0)
    in_specs = [
        pl.BlockSpec((tile, D_MODEL), src_row),
        pl.BlockSpec((1, 1, N_MOD, D_MODEL), lambda t: (layer, mod_row(t * tile), 0, 0)),
        pl.BlockSpec((1, 1, D_MODEL), lay),
        pl.BlockSpec((1, IN_COLS, D_MODEL), lay),
        pl.BlockSpec((1, 1, KV_RANK), lay),
        pl.BlockSpec((1, 1, 512), lay),
        pl.BlockSpec((512, 512), lambda t: (0, 0)),
        pl.BlockSpec((1, 256, 384), lay),
        pl.BlockSpec((1, KV_RANK, 384), lay),
    ]
    args = [x, mod, w["g1"], w["w_in"], w["kv_g"], w["qk_g"], w["seg"], w["w_ka"], w["w_uv"]]
    if resid is not None:
        in_specs += [pl.BlockSpec((tile, D_MODEL), src_row),
                     pl.BlockSpec((1, 1, N_MOD, D_MODEL), lambda t: (layer - 1, mod_row(t * tile), 0, 0))]
        args += [resid, mod]
    per_b = seq // tile
    in_specs += [pl.BlockSpec((tile, LANES), lambda t: (t % per_b, 0))] * 6
    args += list(rope_tabs)
    out_shape = [jax.ShapeDtypeStruct((n, Q_COLS), bf16), jax.ShapeDtypeStruct((n, KV_COLS), bf16)]
    out_specs = [pl.BlockSpec((tile, Q_COLS), row), pl.BlockSpec((tile, KV_COLS), row)]
    if resid is not None:
        out_shape.append(jax.ShapeDtypeStruct((n, D_MODEL), f32))
        out_specs.append(pl.BlockSpec((tile, D_MODEL), row))
    return pl.pallas_call(
        functools.partial(_pre_kernel, True, 0, resid is not None),
        out_shape=out_shape,
        grid=(n // tile,),
        in_specs=in_specs,
        out_specs=out_specs,
        scratch_shapes=[pltpu.VMEM((Z_COLS, D_MODEL), bf16)],
        compiler_params=_params(("arbitrary",)),
        name="pre_latent",
    )(*args)


PAST_CKV, PAST_KG, PAST_VG, PAST_KD, PAST_VD, PAST_KR, PAST_COLS = 0, 128, 256, 384, 640, 896, 928


def _cache_kernel(past_ref, wka_ref, wuv_ref, kv_ref):
    ckv_b = past_ref[0, 0, :, PAST_CKV:PAST_CKV + KV_RANK].astype(bf16)
    kr_b = past_ref[0, 0, :, PAST_KR:PAST_KR + MLA_ROPE].astype(bf16)
    wka = wka_ref[0]
    k_a = _dot(ckv_b, wka[:KV_RANK]) + _dot(kr_b, wka[KV_RANK:KV_RANK + MLA_ROPE])
    kv_ref[0, 0, :, KV_KA:KV_KA + 384] = k_a.astype(bf16)
    kv_ref[0, 0, :, KV_VA:KV_VA + 384] = _dot(ckv_b, wuv_ref[0].astype(bf16)).astype(bf16)
    kv_ref[0, 0, :, KV_KG:KV_KG + 128] = past_ref[0, 0, :, PAST_KG:PAST_KG + 128].astype(bf16)
    kv_ref[0, 0, :, KV_VG:KV_VG + 128] = past_ref[0, 0, :, PAST_VG:PAST_VG + 128].astype(bf16)
    kv_ref[0, 0, :, KV_KD:KV_KD + 256] = past_ref[0, 0, :, PAST_KD:PAST_KD + 256].astype(bf16)
    kv_ref[0, 0, :, KV_VD:KV_VD + 256] = past_ref[0, 0, :, PAST_VD:PAST_VD + 256].astype(bf16)


def _cache_rows(caches, w):
    ckv, kr, kg, vg, kd, vd = caches
    B = ckv.shape[0]
    flat = lambda a: a.reshape(B, DEPTH, PAST_LEN, -1)
    past = jnp.concatenate([flat(ckv), flat(kg), flat(vg), flat(kd), flat(vd), flat(kr)], axis=-1)
    return pl.pallas_call(
        _cache_kernel,
        out_shape=jax.ShapeDtypeStruct((DEPTH, B, PAST_LEN, KV_COLS), bf16),
        grid=(DEPTH, B),
        in_specs=[pl.BlockSpec((1, 1, PAST_LEN, PAST_COLS), lambda i, b: (b, i, 0, 0)),
                  pl.BlockSpec((1, 256, 384), lambda i, b: (i, 0, 0)), pl.BlockSpec((1, KV_RANK, 384), lambda i, b: (i, 0, 0))],
        out_specs=pl.BlockSpec((1, 1, PAST_LEN, KV_COLS), lambda i, b: (i, b, 0, 0)),
        compiler_params=_params(("arbitrary", "arbitrary")),
        name="cache_rows",
    )(past, w["w_ka"], w["w_uv"])


_SCORE_HEADS = (
    [(Q_A + 64 * h, KV_KA + 64 * h, 64, KV_VA + MLA_V * h) for h in range(MLA_HEADS)]
    + [(Q_G + 64 * h, KV_KG + 64 * (h // GQA_GROUP), 64, KV_VG + 64 * (h // GQA_GROUP)) for h in range(GQA_HEADS)]
    + [(Q_D + 64 * h + DIFF_QK * c, KV_KD + 64 * h + DIFF_QK * c, DIFF_QK, KV_VD + DIFF_V * h)
       for h in range(DIFF_HEADS) for c in range(2)])


def _attn_kernel(lam_init, per_round, n_src, q_ref, *refs):
    kv_refs = refs[:n_src]
    lam_ref, dg_ref, o_ref, s_ref, p_ref = refs[n_src:]
    spans, start = [], 0
    for r in kv_refs:
        spans.append((r, start, r.shape[2]))
        start += r.shape[2]
    mxu_sum = start >= MXU_SUM_MIN_KEYS

    outs = []
    for first in range(0, len(_SCORE_HEADS), per_round):
        chunk = _SCORE_HEADS[first:first + per_round]
        for j, (q_off, k_off, width, _) in enumerate(chunk):
            for r, lo, size in spans:
                s_ref[j, :, lo:lo + size] = _dot_nt(q_ref[:, q_off:q_off + width], r[0, 0, :, k_off:k_off + width])
        s = s_ref[...]
        p = jnp.exp2(s - jnp.max(s, axis=-1, keepdims=True))
        if mxu_sum:
            p_ref[...] = p.astype(bf16)
            for j, (_, _, _, v_off) in enumerate(chunk):
                o = sum(_dot(p_ref[j, :, lo:lo + size],
                             jnp.concatenate([r[0, 0, :, v_off:v_off + DIFF_V],
                                              jnp.ones((size, LANES - DIFF_V), bf16)], axis=1))
                        for r, lo, size in spans)
                outs.append((o * pltpu.roll(1.0 / o, DIFF_V, 1))[:, :DIFF_V])
        else:
            inv = 1.0 / jnp.sum(p, axis=-1, keepdims=True)
            p_ref[...] = p.astype(bf16)
            for j, (_, _, _, v_off) in enumerate(chunk):
                o = sum(_dot(p_ref[j, :, lo:lo + size], r[0, 0, :, v_off:v_off + DIFF_V]) for r, lo, size in spans)
                outs.append(o * inv[j])

    lp = lam_ref[0]
    e1 = jnp.exp(jnp.sum(lp[0:1] * lp[1:2], axis=-1, keepdims=True))
    e2 = jnp.exp(jnp.sum(lp[2:3] * lp[3:4], axis=-1, keepdims=True))
    lam = e1 - e2 + lam_init
    heads = outs[:MLA_HEADS + GQA_HEADS]
    for h in range(DIFF_HEADS):
        o1, o2 = outs[MLA_HEADS + GQA_HEADS + 2 * h:MLA_HEADS + GQA_HEADS + 2 * h + 2]
        heads.append(_rms(o1 - lam * o2, DIFF_V) * dg_ref[0] * (1.0 - lam_init))
    for j in range(len(heads) // 2):
        o_ref[:, LANES * j:LANES * (j + 1)] = jnp.concatenate(heads[2 * j:2 * j + 2], axis=1).astype(bf16)


def _attention(layer, q, sources, w, tile, per_round):
    n = q.shape[0]
    lam_init = 0.8 - 0.6 * math.exp(-0.3 * layer)
    s_kv = sum(rows for _, rows, _ in sources)
    assert len(_SCORE_HEADS) % per_round == 0
    return pl.pallas_call(
        functools.partial(_attn_kernel, lam_init, per_round, len(sources)),
        out_shape=jax.ShapeDtypeStruct((n, D_MODEL), bf16),
        scratch_shapes=[pltpu.VMEM((per_round, tile, s_kv), f32), pltpu.VMEM((per_round, tile, s_kv), bf16)],
        grid=(n // tile,),
        in_specs=[pl.BlockSpec((tile, Q_COLS), lambda t: (t, 0))]
        + [pl.BlockSpec((1, 1, rows, KV_COLS), index) for _, rows, index in sources]
        + [pl.BlockSpec((1, 4, DIFF_QK), lambda t: (layer, 0, 0)), pl.BlockSpec((1, 1, DIFF_V), lambda t: (layer, 0, 0))],
        out_specs=pl.BlockSpec((tile, D_MODEL), lambda t: (t, 0)),
        compiler_params=_params(("arbitrary",)),
        name="attention",
    )(q, *[arr for arr, _, _ in sources], w["lam"], w["diff_g"])


def _post_kernel(merge, *refs):
    it = iter(refs)
    o_ref, x_ref, mod_ref, w_out_ref, g2_ref, wg_ref, bg_ref, we_ref, be_ref = (next(it) for _ in range(9))
    if merge:
        next(it), next(it), next(it)
    x1_ref, hx_ref, cls_ref, w_scr = (next(it) for _ in range(4))

    @pl.when(pl.program_id(0) == 0)
    def _():
        w_scr[...] = w_out_ref[0].astype(bf16)

    gate1 = mod_ref[0, 0, 2:3, :]
    shift2 = mod_ref[0, 0, 3:4, :]
    scale2 = mod_ref[0, 0, 4:5, :]
    x1 = x_ref[...] + gate1 * _dot(o_ref[...], w_scr[...])
    x1_ref[...] = x1
    h2 = ((_rms(x1, D_MODEL) * g2_ref[0]) * (1.0 + scale2) + shift2).astype(bf16)
    bits = pltpu.bitcast(h2.astype(f32), i32)
    hx_ref[:, 0:HX_HALF] = bits[:, 0:HX_HALF] | jax.lax.shift_right_logical(bits[:, HX_HALF:D_MODEL], 16)

    def first_lane(mask, lane_f):
        return jnp.min(jnp.where(mask, lane_f, float(LANES)), axis=-1, keepdims=True)

    gl = _dot(h2, wg_ref[0].astype(bf16)) + bg_ref[0]
    glane = jax.lax.broadcasted_iota(i32, gl.shape, 1).astype(f32)
    ge = jnp.exp(gl - jnp.max(gl, axis=-1, keepdims=True))
    gprob = ge / jnp.sum(ge, axis=-1, keepdims=True)
    g_top = jnp.max(gprob, axis=-1, keepdims=True)
    g_idx = first_lane(gprob == g_top, glane)

    el = _dot(h2, we_ref[0].astype(bf16)) + be_ref[0]
    lane = jax.lax.broadcasted_iota(i32, el.shape, 1)
    lane_f = lane.astype(f32)
    emask = (lane >> 2).astype(f32) == g_idx
    em = jnp.where(emask, el, -jnp.inf)
    ee = jnp.where(emask, jnp.exp(em - jnp.max(em, axis=-1, keepdims=True)), 0.0)
    ep = ee / jnp.sum(ee, axis=-1, keepdims=True)
    p1 = jnp.max(jnp.where(emask, ep, -1.0), axis=-1, keepdims=True)
    i1 = first_lane(emask & (ep == p1), lane_f)
    rest = emask & (lane_f != i1)
    p2 = jnp.max(jnp.where(rest, ep, -1.0), axis=-1, keepdims=True)
    i2 = first_lane(rest & (ep == p2), lane_f)
    tot = p1 + p2
    w1 = g_top * (p1 / tot)
    w2 = g_top * (p2 / tot)

    lo = jnp.minimum(i1, i2) - EXPERTS_PER_GROUP * g_idx
    hi = jnp.maximum(i1, i2) - EXPERTS_PER_GROUP * g_idx
    pair = jnp.where(lo == 0.0, hi - 1.0, jnp.where(lo == 1.0, jnp.where(hi == 3.0, 3.0, 4.0), 5.0))
    cls_ref[...] = (N_PAIRS * g_idx + pair).astype(i32)
    g_lo = jnp.where(i1 < i2, w1, w2)
    g_hi = jnp.where(i1 < i2, w2, w1)
    tail_lane = jax.lax.broadcasted_iota(i32, (h2.shape[0], LANES), 1)
    hx_ref[:, HX_HALF:HX_COLS] = pltpu.bitcast(
        jnp.where(tail_lane == 0, g_lo, jnp.where(tail_lane == 1, g_hi, 0.0)), i32)


def _post(layer, o, x, row0, total, mod, mod_row, w, tile, merged=None):
    n = o.shape[0]
    lay = lambda t: (layer, 0, 0)
    row = lambda t: (t, 0)
    off = row0 // tile
    out_row = lambda t: (off + t, 0)
    in_specs = [
        pl.BlockSpec((tile, D_MODEL), row),
        pl.BlockSpec((tile, D_MODEL), row),
        pl.BlockSpec((1, 1, N_MOD, D_MODEL), lambda t: (layer, mod_row(t * tile), 0, 0)),
        pl.BlockSpec((1, D_MODEL, D_MODEL), lay),
        pl.BlockSpec((1, 1, D_MODEL), lay),
        pl.BlockSpec((1, D_MODEL, N_GROUPS), lay),
        pl.BlockSpec((1, 1, N_GROUPS), lay),
        pl.BlockSpec((1, D_MODEL, N_EXPERTS), lay),
        pl.BlockSpec((1, 1, N_EXPERTS), lay),
    ]
    args = [o, x, mod, w["w_out"], w["g2"], w["w_grp"], w["b_grp"], w["w_rtr"], w["b_rtr"]]
    aliases = {}
    if merged is not None:
        aliases = {len(args) + j: j for j in range(3)}
        in_specs += [pl.BlockSpec(memory_space=pl.ANY)] * 3
        args += list(merged)
    return pl.pallas_call(
        functools.partial(_post_kernel, merged is not None),
        out_shape=[jax.ShapeDtypeStruct((total, D_MODEL), f32), jax.ShapeDtypeStruct((total, HX_COLS), i32),
                   jax.ShapeDtypeStruct((total, 1), i32)],
        grid=(n // tile,),
        in_specs=in_specs,
        out_specs=[pl.BlockSpec((tile, D_MODEL), out_row), pl.BlockSpec((tile, HX_COLS), out_row),
                   pl.BlockSpec((tile, 1), out_row)],
        scratch_shapes=[pltpu.VMEM((D_MODEL, D_MODEL), bf16)],
        input_output_aliases=aliases,
        compiler_params=_params(("arbitrary",)),
        name="post_attention",
    )(*args)


def _ctx_kernel(lam_init, n_prev, resid, *refs):
    it = iter(refs)
    pre_in = [next(it) for _ in range(9 + (2 if resid else 0) + n_prev)]
    lam_ref, dg_ref = next(it), next(it)
    post_w = [next(it) for _ in range(6)]
    cache_refs = [next(it) for _ in range(len(CACHE_WIDTHS))]
    x1_ref, hx_ref, cls_ref = (next(it) for _ in range(3))
    w_in_scr, q_scr, kv_scr, o_scr, s_scr, p_scr, w_out_scr, x2_scr = (next(it) for _ in range(8))
    x_ref, mod_ref = pre_in[0], pre_in[1]
    x2_ref = [x2_scr] if resid else []

    _pre_kernel(False, n_prev, resid, *pre_in, q_scr, kv_scr.at[0, 0], *x2_ref, *cache_refs, w_in_scr)
    _attn_kernel(lam_init, len(_SCORE_HEADS), 1, q_scr, kv_scr, lam_ref, dg_ref, o_scr, s_scr, p_scr)
    _post_kernel(False, o_scr, x2_scr if resid else x_ref, mod_ref, *post_w, x1_ref, hx_ref, cls_ref, w_out_scr)


def _ctx_layer(layer, x, n, seq, total, mod, w, prev_cache=(), resid=None):
    lay = lambda t: (layer, 0, 0)
    row = lambda t: (t, 0)
    mod_spec = lambda l: pl.BlockSpec((1, 1, N_MOD, D_MODEL), lambda t: (l, 0, 0, 0))
    in_specs = [
        pl.BlockSpec((seq, D_MODEL), row), mod_spec(layer),
        pl.BlockSpec((1, 1, D_MODEL), lay), pl.BlockSpec((1, IN_COLS, D_MODEL), lay),
        pl.BlockSpec((1, 1, KV_RANK), lay), pl.BlockSpec((1, 1, 512), lay), pl.BlockSpec((512, 512), lambda t: (0, 0)),
        pl.BlockSpec((1, 256, 384), lay), pl.BlockSpec((1, KV_RANK, 384), lay),
    ]
    args = [x, mod, w["g1"], w["w_in"], w["kv_g"], w["qk_g"], w["seg"], w["w_ka"], w["w_uv"]]
    if resid is not None:
        in_specs += [pl.BlockSpec((seq, D_MODEL), row), mod_spec(layer - 1)]
        args += [resid, mod]
    out_shape, out_specs, aliases = [], [], {}
    for j, width in enumerate(CACHE_WIDTHS):
        if prev_cache:
            aliases[len(args)] = len(out_shape)
            in_specs.append(pl.BlockSpec(memory_space=pl.ANY))
            args.append(prev_cache[j])
        out_shape.append(jax.ShapeDtypeStruct((n // seq, DEPTH, seq, width), f32))
        out_specs.append(pl.BlockSpec((1, 1, seq, width), lambda t: (t, layer, 0, 0)))
    in_specs += [
        pl.BlockSpec((1, 4, DIFF_QK), lay), pl.BlockSpec((1, 1, DIFF_V), lay),
        pl.BlockSpec((1, D_MODEL, D_MODEL), lay), pl.BlockSpec((1, 1, D_MODEL), lay),
        pl.BlockSpec((1, D_MODEL, N_GROUPS), lay), pl.BlockSpec((1, 1, N_GROUPS), lay),
        pl.BlockSpec((1, D_MODEL, N_EXPERTS), lay), pl.BlockSpec((1, 1, N_EXPERTS), lay),
    ]
    args += [w["lam"], w["diff_g"], w["w_out"], w["g2"], w["w_grp"], w["b_grp"], w["w_rtr"], w["b_rtr"]]
    out_shape += [jax.ShapeDtypeStruct((total, D_MODEL), f32), jax.ShapeDtypeStruct((total, HX_COLS), i32),
                  jax.ShapeDtypeStruct((total, 1), i32)]
    out_specs += [pl.BlockSpec((seq, D_MODEL), row), pl.BlockSpec((seq, HX_COLS), row), pl.BlockSpec((seq, 1), row)]
    heads = len(_SCORE_HEADS)
    return pl.pallas_call(
        functools.partial(_ctx_kernel, 0.8 - 0.6 * math.exp(-0.3 * layer), len(prev_cache), resid is not None),
        out_shape=out_shape,
        grid=(n // seq,),
        in_specs=in_specs,
        out_specs=out_specs,
        scratch_shapes=[pltpu.VMEM((Z_COLS, D_MODEL), bf16), pltpu.VMEM((seq, Q_COLS), bf16),
                        pltpu.VMEM((1, 1, seq, KV_COLS), bf16), pltpu.VMEM((seq, D_MODEL), bf16),
                        pltpu.VMEM((heads, seq, seq), f32), pltpu.VMEM((heads, seq, seq), bf16),
                        pltpu.VMEM((D_MODEL, D_MODEL), bf16), pltpu.VMEM((seq, D_MODEL), f32)],
        input_output_aliases=aliases,
        compiler_params=_params(("arbitrary",)),
        name="context_layer",
    )(*args)


PLAN_CHUNK = 1024
TAB_ROWS = LANES


def _plan_kernel(cls_ref, slot_ref, tab_ref, rank_scr):
    n = cls_ref.shape[0]
    lane = jax.lax.broadcasted_iota(i32, (PLAN_CHUNK, LANES), 1)
    r = jax.lax.broadcasted_iota(i32, (PLAN_CHUNK, PLAN_CHUNK), 0)
    c = jax.lax.broadcasted_iota(i32, (PLAN_CHUNK, PLAN_CHUNK), 1)
    before = (c < r).astype(bf16)

    def count(b, seen):
        rows = pl.ds(pl.multiple_of(b * PLAN_CHUNK, PLAN_CHUNK), PLAN_CHUNK)
        onehot = (cls_ref[rows, :] == lane).astype(f32)
        ahead = _dot(before, onehot.astype(bf16)) + seen
        rank_scr[rows, :] = jnp.sum(onehot * ahead, axis=-1, keepdims=True)
        return seen + jnp.sum(onehot, axis=0, keepdims=True)

    counts = jax.lax.fori_loop(0, n // PLAN_CHUNK, count, jnp.zeros((1, LANES), f32))
    tiles = jnp.floor((counts + (EXPERT_TILE - 1)) * (1.0 / EXPERT_TILE))
    rr = jax.lax.broadcasted_iota(i32, (LANES, LANES), 0)
    cc = jax.lax.broadcasted_iota(i32, (LANES, LANES), 1)
    ends = _dot(jnp.broadcast_to(tiles, (8, LANES)).astype(bf16), (rr <= cc).astype(bf16))[0:1]
    starts = ends - tiles

    def place(b, carry):
        rows = pl.ds(pl.multiple_of(b * PLAN_CHUNK, PLAN_CHUNK), PLAN_CHUNK)
        first = jnp.sum(jnp.where(cls_ref[rows, :] == lane, starts, 0.0), axis=-1, keepdims=True)
        slot_ref[rows, :] = (first * EXPERT_TILE + rank_scr[rows, :]).astype(i32)
        return carry

    jax.lax.fori_loop(0, n // PLAN_CHUNK, place, 0)

    tl = jax.lax.broadcasted_iota(i32, (TAB_ROWS, LANES), 1)
    n_tiles = jnp.sum(jnp.where(tl[0:1] == N_CLASSES - 1, ends, 0.0), axis=-1, keepdims=True)
    k = jnp.minimum(jax.lax.broadcasted_iota(i32, (TAB_ROWS, 1), 0).astype(f32), n_tiles - 1.0)
    cls_k = jnp.sum(jnp.where((tl < N_CLASSES) & (ends <= k), 1.0, 0.0), axis=-1, keepdims=True)
    cls_k = jnp.minimum(cls_k, N_CLASSES - 1.0)
    mine = tl.astype(f32) == cls_k
    used = jnp.sum(jnp.where(mine, counts, 0.0), axis=-1, keepdims=True)
    first = jnp.sum(jnp.where(mine, starts, 0.0), axis=-1, keepdims=True)
    valid = jnp.clip(used - (k - first) * EXPERT_TILE, 0.0, float(EXPERT_TILE))
    group = jnp.floor((cls_k + 0.5) * (1.0 / N_PAIRS))
    pair = cls_k - N_PAIRS * group
    lo = hi = jnp.zeros_like(pair)
    for p in range(N_PAIRS):
        lo = jnp.where(pair == p, float(PAIR_LO[p]), lo)
        hi = jnp.where(pair == p, float(PAIR_HI[p]), hi)
    e_lo = EXPERTS_PER_GROUP * group + lo
    e_hi = EXPERTS_PER_GROUP * group + hi
    tab = jnp.where(tl == 0, e_lo, jnp.where(tl == 1, e_hi, jnp.where(tl == 2, valid, jnp.where(tl == 3, n_tiles, 0.0))))
    tab_ref[...] = tab.astype(i32)


def _plan(cls, max_tiles):
    n = cls.shape[0]
    assert n % PLAN_CHUNK == 0 and max_tiles <= TAB_ROWS
    slot, tab = pl.pallas_call(
        _plan_kernel,
        out_shape=[jax.ShapeDtypeStruct((n, 1), i32), jax.ShapeDtypeStruct((TAB_ROWS, LANES), i32)],
        scratch_shapes=[pltpu.VMEM((n, 1), f32)],
        compiler_params=_params(None),
        name="dispatch_plan",
    )(cls)
    return slot.reshape(n), tab[:max_tiles, 0], tab[:max_tiles, 1], tab[:max_tiles, 2], tab[0, 3:4]


def _move_rows(src, idx, n_out, scatter):
    n = idx.shape[0]
    width = src.shape[1]
    per_worker = n // (SC_CORES * SC_SUBCORES)
    assert n % SC_ROWS == 0
    chunk = max(c for c in (64, 40, 32, 16, 8)
                if per_worker % c == 0 and 2 * c * width * src.dtype.itemsize <= SC_BUFFER_BYTES)
    n_chunks = per_worker // chunk
    mesh = plsc.VectorSubcoreMesh(core_axis_name="c", subcore_axis_name="s")

    @functools.partial(
        pl.kernel, mesh=mesh, out_type=jax.ShapeDtypeStruct((n_out, width), src.dtype),
        scratch_types=[pltpu.VMEM((chunk,), i32), pltpu.VMEM((chunk,), i32),
                       pltpu.VMEM((chunk, width), src.dtype), pltpu.VMEM((chunk, width), src.dtype),
                       pltpu.SemaphoreType.DMA, pltpu.SemaphoreType.DMA, pltpu.SemaphoreType.DMA,
                       pltpu.SemaphoreType.DMA])
    def move(src_hbm, idx_hbm, out_hbm, idx0, idx1, rows0, rows1, in0, in1, out0, out1):
        wid = jax.lax.axis_index("s") * SC_CORES + jax.lax.axis_index("c")
        base = wid * per_worker
        idx_v, rows_v, sem_in, sem_out = (idx0, idx1), (rows0, rows1), (in0, in1), (out0, out1)

        def fill(j):
            b = j % 2
            rows = pl.ds(base + j * chunk, chunk)
            pltpu.sync_copy(idx_hbm.at[rows], idx_v[b])
            src_rows = src_hbm.at[rows] if scatter else src_hbm.at[idx_v[b]]
            return pltpu.async_copy(src_rows, rows_v[b], sem_in[b])

        def drain(j):
            b = j % 2
            dst_rows = out_hbm.at[idx_v[b]] if scatter else out_hbm.at[pl.ds(base + j * chunk, chunk)]
            return pltpu.async_copy(rows_v[b], dst_rows, sem_out[b])

        fills, drains = {0: fill(0)}, {}
        for j in range(n_chunks):
            if j + 1 < n_chunks:
                if j >= 1:
                    drains[j - 1].wait()
                fills[j + 1] = fill(j + 1)
            fills[j].wait()
            drains[j] = drain(j)
        for j in range(max(n_chunks - 2, 0), n_chunks):
            drains[j].wait()

    return move(src, idx)


def _expert_kernel(lo_ref, hi_ref, valid_ref, nt_ref, xs_ref, wg_ref, wu_ref, wd_ref, ys_ref, wg_scr, wu_scr, wd_scr):
    k = pl.program_id(0)
    prev = jnp.maximum(k - 1, 0)

    @pl.when((k == 0) | (lo_ref[k] // EXPERTS_PER_GROUP != lo_ref[prev] // EXPERTS_PER_GROUP))
    def _():
        for j in range(EXPERTS_PER_GROUP):
            wg_scr[j] = wg_ref[0, 0, j].astype(bf16)
            wu_scr[j] = wu_ref[0, 0, j].astype(bf16)
            wd_scr[j] = wd_ref[0, 0, j].astype(bf16)

    @pl.when(k < nt_ref[0])
    def _():
        live = jax.lax.broadcasted_iota(i32, (EXPERT_TILE, 1), 0) < valid_ref[k]
        words = jnp.where(live, xs_ref[:, 0:HX_HALF], 0)
        x = jnp.concatenate([pltpu.bitcast(words & -65536, f32), pltpu.bitcast(words << 16, f32)], axis=1).astype(bf16)
        gates = pltpu.bitcast(jnp.where(live, xs_ref[:, HX_HALF:HX_COLS], 0), f32)
        y = None
        for lane, e_ref in enumerate((lo_ref, hi_ref)):
            j = e_ref[k] % EXPERTS_PER_GROUP
            hid = _silu(_dot(x, wg_scr[j])) * _dot(x, wu_scr[j]) * gates[:, lane:lane + 1]
            part = _dot(hid.astype(bf16), wd_scr[j])
            y = part if y is None else y + part
        ys_ref[...] = y


def _experts(layer, xs, e_lo, e_hi, valid, n_tiles, w):
    max_tiles = e_lo.shape[0]
    row = lambda k, lo, hi, valid, nt: (jnp.minimum(k, nt[0] - 1), 0)
    group = lambda k, lo, hi, valid, nt: (layer, lo[k] // EXPERTS_PER_GROUP, 0, 0, 0)
    by_group = lambda a: a.reshape(DEPTH, N_GROUPS, EXPERTS_PER_GROUP, *a.shape[2:])
    return pl.pallas_call(
        _expert_kernel,
        out_shape=jax.ShapeDtypeStruct((max_tiles * EXPERT_TILE, D_MODEL), f32),
        grid_spec=pltpu.PrefetchScalarGridSpec(
            num_scalar_prefetch=4, grid=(max_tiles,),
            in_specs=[pl.BlockSpec((EXPERT_TILE, HX_COLS), row),
                      pl.BlockSpec((1, 1, EXPERTS_PER_GROUP, D_MODEL, D_FF_EXPERT), group),
                      pl.BlockSpec((1, 1, EXPERTS_PER_GROUP, D_MODEL, D_FF_EXPERT), group),
                      pl.BlockSpec((1, 1, EXPERTS_PER_GROUP, D_FF_EXPERT, D_MODEL), group)],
            out_specs=pl.BlockSpec((EXPERT_TILE, D_MODEL), row),
            scratch_shapes=[pltpu.VMEM((EXPERTS_PER_GROUP, D_MODEL, D_FF_EXPERT), bf16),
                            pltpu.VMEM((EXPERTS_PER_GROUP, D_MODEL, D_FF_EXPERT), bf16),
                            pltpu.VMEM((EXPERTS_PER_GROUP, D_FF_EXPERT, D_MODEL), bf16)]),
        compiler_params=_params(("arbitrary",)),
        name="experts",
    )(e_lo, e_hi, valid, n_tiles, xs, by_group(w["w_gate"]), by_group(w["w_up"]), by_group(w["w_down"]))


def _final_kernel(x1_ref, y_ref, mod_ref, fg_ref, o_ref):
    o_ref[...] = _rms(x1_ref[...] + mod_ref[0, 0, 5:6, :] * y_ref[...], D_MODEL) * fg_ref[...]


def _final(x1, y, n, row0, mod, mod_row, w, tile):
    off = row0 // tile
    src_row = lambda t: (off + t, 0)
    return pl.pallas_call(
        _final_kernel,
        out_shape=jax.ShapeDtypeStruct((n, D_MODEL), f32),
        grid=(n // tile,),
        in_specs=[pl.BlockSpec((tile, D_MODEL), src_row), pl.BlockSpec((tile, D_MODEL), src_row),
                  pl.BlockSpec((1, 1, N_MOD, D_MODEL), lambda t: (DEPTH - 1, mod_row(t * tile), 0, 0)),
                  pl.BlockSpec((1, D_MODEL), lambda t: (0, 0))],
        out_specs=pl.BlockSpec((tile, D_MODEL), lambda t: (t, 0)),
        compiler_params=_params(("arbitrary",)),
        name="final_norm",
    )(x1, y, mod, w["final_g"])


def _rope_tables(n_tokens):
    pos = np.arange(n_tokens)
    row = (pos // GRID_W).astype(np.float64)
    col = (pos % GRID_W).astype(np.float64)

    def cs(rot_dim):
        quarter = rot_dim // 4
        inv = ROPE_THETA ** (-np.arange(quarter, dtype=np.float64) / quarter)
        ang = np.concatenate([row[:, None] * inv, col[:, None] * inv], axis=-1)
        return np.cos(ang), np.sin(ang)

    c32, s32 = cs(MLA_ROPE)
    c64, s64 = cs(HEAD_DIM)
    ones = np.ones((n_tokens, MLA_NOPE))
    zeros = np.zeros((n_tokens, MLA_NOPE))

    def rep(parts):
        period = np.concatenate(parts, axis=-1)
        return jnp.asarray(np.tile(period, (1, LANES // period.shape[-1])), f32)

    return (rep([ones, c32, c32]), rep([zeros, -s32, s32]), rep([c32, c32]), rep([-s32, s32]),
            rep([c64, c64]), rep([-s64, s64]))


def _layout_weights(norm1_g, norm2_g, w_in, mla_kv_norm_g, mla_w_uk, mla_w_uv, gqa_q_norm_g, gqa_k_norm_g,
                    diff_lambda, diff_norm_g, w_out, moe_w_group, moe_b_group, moe_w_router, moe_b_router,
                    moe_w_gate, moe_w_up, moe_w_down, final_norm_g):
    eye = jnp.eye(MLA_ROPE, dtype=f32)
    top = jnp.concatenate([mla_w_uk, jnp.zeros((DEPTH, KV_RANK, MLA_HEADS, MLA_ROPE), f32)], axis=-1)
    mid = jnp.concatenate([jnp.zeros((MLA_ROPE, MLA_HEADS, MLA_NOPE), f32),
                           jnp.broadcast_to(eye[:, None, :], (MLA_ROPE, MLA_HEADS, MLA_ROPE))], axis=-1)
    w_ka = jnp.concatenate([top.reshape(DEPTH, KV_RANK, 384),
                            jnp.broadcast_to(mid.reshape(1, MLA_ROPE, 384), (DEPTH, MLA_ROPE, 384)),
                            jnp.zeros((DEPTH, 256 - KV_RANK - MLA_ROPE, 384), f32)], axis=1).astype(bf16)
    seg_id = np.arange(512) // HEAD_DIM
    seg = jnp.asarray(seg_id[:, None] == seg_id[None, :], bf16)
    qk_g = jnp.concatenate([jnp.tile(gqa_q_norm_g, (1, GQA_HEADS)), jnp.tile(gqa_k_norm_g, (1, GQA_KV_HEADS))], axis=-1)
    return dict(
        g1=norm1_g.reshape(DEPTH, 1, D_MODEL), g2=norm2_g.reshape(DEPTH, 1, D_MODEL),
        w_in=jnp.swapaxes(w_in, 1, 2),
        kv_g=mla_kv_norm_g.reshape(DEPTH, 1, KV_RANK), qk_g=qk_g.reshape(DEPTH, 1, 512), seg=seg, w_ka=w_ka,
        w_uv=mla_w_uv.reshape(DEPTH, KV_RANK, 384), lam=diff_lambda, diff_g=diff_norm_g.reshape(DEPTH, 1, DIFF_V),
        w_out=w_out, w_grp=moe_w_group, b_grp=moe_b_group.reshape(DEPTH, 1, N_GROUPS), w_rtr=moe_w_router,
        b_rtr=moe_b_router.reshape(DEPTH, 1, N_EXPERTS), w_gate=moe_w_gate, w_up=moe_w_up, w_down=moe_w_down,
        final_g=final_norm_g.reshape(1, D_MODEL))


PRE_TILE = 512
LAT_ATTN_TILE = 256
LAT_HEADS_PER_ROUND = 5
MXU_SUM_MIN_KEYS = 1024
POST_TILE = 512
FINAL_TILE = 1024


def kernel(x_prompt, x_sample, c, cache_mla_ckv, cache_mla_krope, cache_gqa_k, cache_gqa_v, cache_diff_k, cache_diff_v, c_ctx, norm1_g, norm2_g, w_mod, b_mod, w_in, mla_kv_norm_g, mla_w_uk, mla_w_uv, gqa_q_norm_g, gqa_k_norm_g, diff_lambda, diff_norm_g, w_out, moe_w_group, moe_b_group, moe_w_router, moe_b_router, moe_w_gate, moe_w_up, moe_w_down, final_norm_g):
    B, S, _ = x_prompt.shape
    Bl, Sl, _ = x_sample.shape
    n_ctx, n_lat = B * S, Bl * Sl
    total = n_ctx + n_lat
    assert Bl + 1 <= MOD_ROWS and DEPTH == 2 and total % SC_ROWS == 0
    slot_rows = -(-(total + N_CLASSES * EXPERT_TILE) // SC_ROWS) * SC_ROWS
    max_tiles = slot_rows // EXPERT_TILE
    w = _layout_weights(norm1_g, norm2_g, w_in, mla_kv_norm_g, mla_w_uk, mla_w_uv, gqa_q_norm_g, gqa_k_norm_g,
                        diff_lambda, diff_norm_g, w_out, moe_w_group, moe_b_group, moe_w_router, moe_b_router,
                        moe_w_gate, moe_w_up, moe_w_down, final_norm_g)
    cond = jnp.concatenate([c_ctx[None, :], c, jnp.zeros((MOD_ROWS - 1 - Bl, D_MODEL), f32)], axis=0)
    mod = _modulation(cond, w_mod, b_mod).reshape(DEPTH, MOD_ROWS, N_MOD, D_MODEL)
    ctx_row = lambda token: 0
    lat_row = lambda token: 1 + token // Sl
    tabs = _rope_tables(Sl)
    kv_past = _cache_rows((cache_mla_ckv, cache_mla_krope, cache_gqa_k, cache_gqa_v, cache_diff_k, cache_diff_v), w)
    per_b = Sl // LAT_ATTN_TILE

    x_ctx, x_lat = x_prompt.reshape(n_ctx, D_MODEL), x_sample.reshape(n_lat, D_MODEL)
    cache = ()
    x1 = y = None
    for i in range(DEPTH):
        if i == 0:
            *cache, x1_c, hx_c, cls_c = _ctx_layer(i, x_ctx, n_ctx, S, total, mod, w)
        else:
            *cache, x1_c, hx_c, cls_c = _ctx_layer(i, x1, n_ctx, S, total, mod, w, prev_cache=cache, resid=y)
        if i == 0:
            q_l, kv_l = _pre_latent(i, x_lat, n_lat, 0, Sl, mod, lat_row, w, PRE_TILE, tabs)
        else:
            q_l, kv_l, x_lat = _pre_latent(i, x1, n_lat, n_ctx, Sl, mod, lat_row, w, PRE_TILE, tabs, resid=y)
        past = (kv_past, PAST_LEN, lambda t, i=i: (i, t // per_b, 0, 0))
        own = (kv_l.reshape(1, Bl, Sl, KV_COLS), Sl, lambda t: (0, t // per_b, 0, 0))
        o_l = _attention(i, q_l, [past, own], w, LAT_ATTN_TILE, LAT_HEADS_PER_ROUND)
        x1, hx, cls = _post(i, o_l, x_lat, n_ctx, total, mod, lat_row, w, POST_TILE, merged=(x1_c, hx_c, cls_c))
        if i == DEPTH - 1:
            outs = (cache[0], cache[1], cache[2].reshape(B, DEPTH, S, GQA_KV_HEADS, HEAD_DIM),
                    cache[3].reshape(B, DEPTH, S, GQA_KV_HEADS, HEAD_DIM),
                    cache[4].reshape(B, DEPTH, S, DIFF_HEADS, 2, DIFF_QK),
                    cache[5].reshape(B, DEPTH, S, DIFF_HEADS, DIFF_V))
            cls, outs = jax.lax.optimization_barrier((cls, outs))
        slot, e_lo, e_hi, valid, n_tiles = _plan(cls, max_tiles)
        xs = _move_rows(hx, slot, slot_rows, scatter=True)
        ys = _experts(i, xs, e_lo, e_hi, valid, n_tiles, w)
        y = _move_rows(ys, slot, total, scatter=False)

    y_prompt = _final(x1, y, n_ctx, 0, mod, ctx_row, w, FINAL_TILE).reshape(B, S, D_MODEL)
    y_sample = _final(x1, y, n_lat, n_ctx, mod, lat_row, w, FINAL_TILE).reshape(Bl, Sl, D_MODEL)
    return (y_prompt, y_sample, *outs)
```

```python
import functools
import math

import jax
import jax.numpy as jnp
import numpy as np
from jax.experimental import pallas as pl
from jax.experimental.pallas import tpu as pltpu
from jax.experimental.pallas import tpu_sc as plsc

D_MODEL = 1024
DEPTH = 2
PAST_LEN = 512
GRID_W = 64
ROPE_THETA = 10000.0
EPS = 1e-6
LOG2E = 1.4426950408889634
N_MOD = 6
HEAD_DIM = 64
MLA_HEADS = 6
MLA_NOPE = 32
MLA_ROPE = 32
MLA_V = 64
KV_RANK = 128
GQA_HEADS = 6
GQA_KV_HEADS = 2
GQA_GROUP = GQA_HEADS // GQA_KV_HEADS
DIFF_HEADS = 4
DIFF_QK = 32
DIFF_V = 64
N_GROUPS = 4
EXPERTS_PER_GROUP = 4
N_EXPERTS = N_GROUPS * EXPERTS_PER_GROUP
D_FF_EXPERT = 256

LANES = 128
MOD_ROWS = 8

IN_COLS = 1952
IN_KR = 512
Z_QA, Z_CKV, Z_QG, Z_KG, Z_VG, Z_QD, Z_KD, Z_VD, Z_KR = 0, 384, 512, 896, 1024, 1152, 1408, 1664, 1920
Z_COLS = 2048
Q_A, Q_G, Q_D, Q_COLS = 0, 384, 768, 1024
KV_KA, KV_VA, KV_KG, KV_VG, KV_KD, KV_VD, KV_COLS = 0, 384, 768, 896, 1024, 1280, 1536
CACHE_WIDTHS = (128, 32, 128, 128, 256, 256)

PAIR_LO = (0, 0, 0, 1, 1, 2)
PAIR_HI = (1, 2, 3, 3, 2, 3)
N_PAIRS = len(PAIR_LO)
N_CLASSES = N_GROUPS * N_PAIRS
HX_HALF = D_MODEL // 2
HX_COLS = HX_HALF + LANES
EXPERT_TILE = 256

SC_CORES, SC_SUBCORES = 2, 16
SC_BUFFER_BYTES = 400 * 1024
SC_ROWS = SC_CORES * SC_SUBCORES * 8

VMEM_LIMIT = 56 * 1024 * 1024

bf16 = jnp.bfloat16
f32 = jnp.float32
i32 = jnp.int32


def _dot(a, b):
    return jnp.dot(a, b, preferred_element_type=f32)


def _dot_nt(a, b):
    return jax.lax.dot_general(a, b, (((1,), (1,)), ((), ())), preferred_element_type=f32)


def _rms(x, width):
    return x * jax.lax.rsqrt(jnp.sum(x * x, axis=-1, keepdims=True) * (1.0 / width) + EPS)


def _silu(x):
    return x * (1.0 / (1.0 + jnp.exp(-x)))


def _params(sem):
    return pltpu.CompilerParams(dimension_semantics=sem, vmem_limit_bytes=VMEM_LIMIT)


def _mod_kernel(cond_ref, w_ref, b_ref, o_ref):
    o_ref[0] = _dot(_silu(cond_ref[...]).astype(bf16), w_ref[0].astype(bf16)) + b_ref[0]


def _modulation(cond, w_mod, b_mod):
    return pl.pallas_call(
        _mod_kernel,
        out_shape=jax.ShapeDtypeStruct((DEPTH, MOD_ROWS, N_MOD * D_MODEL), f32),
        grid=(DEPTH, N_MOD),
        in_specs=[
            pl.BlockSpec((MOD_ROWS, D_MODEL), lambda i, j: (0, 0)),
            pl.BlockSpec((1, D_MODEL, D_MODEL), lambda i, j: (i, 0, j)),
            pl.BlockSpec((1, 1, D_MODEL), lambda i, j: (i, 0, j)),
        ],
        out_specs=pl.BlockSpec((1, MOD_ROWS, D_MODEL), lambda i, j: (i, 0, j)),
        compiler_params=_params(("arbitrary", "arbitrary")),
        name="modulation",
    )(cond, w_mod, b_mod.reshape(DEPTH, 1, N_MOD * D_MODEL))


def _swap_halves(x, half):
    lane = jax.lax.broadcasted_iota(i32, x.shape, 1)
    fwd = pltpu.roll(x, LANES - half, 1)
    bwd = pltpu.roll(x, half, 1)
    return jnp.where((lane & (2 * half - 1)) < half, fwd, bwd)


def _rope_block(x, cos, sin, half):
    return x * cos + _swap_halves(x, half) * sin


def _pre_kernel(rope, n_prev, resid, *refs):
    it = iter(refs)
    x_ref, mod_ref, g1_ref, w_in_ref, kvg_ref, qkg_ref, seg_ref, wka_ref, wuv_ref = (next(it) for _ in range(9))
    if resid:
        y_ref, pmod_ref = next(it), next(it)
    if rope:
        ca_ref, sa_ref, c32_ref, s32_ref, c64_ref, s64_ref = (next(it) for _ in range(6))
    prev_refs = [next(it) for _ in range(n_prev)]
    q_ref, kv_ref = next(it), next(it)
    if resid:
        x2_ref = next(it)
    cache_refs = [] if rope else [next(it) for _ in range(len(CACHE_WIDTHS))]
    w_scr = next(it)

    @pl.when(pl.program_id(0) == 0)
    def _():
        w_scr[0:IN_KR] = w_in_ref[0, 0:IN_KR].astype(bf16)
        w_scr[IN_KR:Z_KR] = w_in_ref[0, IN_KR + MLA_ROPE:IN_COLS].astype(bf16)
        w_scr[Z_KR:Z_KR + MLA_ROPE] = w_in_ref[0, IN_KR:IN_KR + MLA_ROPE].astype(bf16)
        w_scr[Z_KR + MLA_ROPE:Z_COLS] = jnp.zeros((Z_COLS - Z_KR - MLA_ROPE, D_MODEL), bf16)

    x = x_ref[...]
    if resid:
        x = x + pmod_ref[0, 0, 5:6, :] * y_ref[...]
        x2_ref[...] = x
    shift1 = mod_ref[0, 0, 0:1, :]
    scale1 = mod_ref[0, 0, 1:2, :]
    h = (_rms(x, D_MODEL) * g1_ref[0]) * (1.0 + scale1) + shift1
    z = _dot_nt(h.astype(bf16), w_scr[...])

    ckv = _rms(z[:, Z_CKV:Z_CKV + KV_RANK], KV_RANK) * kvg_ref[0]

    qk = z[:, Z_QG:Z_VG]
    sq = qk * qk
    sq_hi = sq.astype(bf16)
    sq_lo = (sq - sq_hi.astype(f32)).astype(bf16)
    seg = seg_ref[...]
    ms = (_dot(sq_hi, seg) + _dot(sq_lo, seg)) * (1.0 / HEAD_DIM)
    qk = qk * jax.lax.rsqrt(ms + EPS) * qkg_ref[0]

    def blocks(arr, n):
        return [arr[:, LANES * j:LANES * (j + 1)] for j in range(n)]

    qa = blocks(z[:, Z_QA:Z_QA + 384], 3)
    qkb = blocks(qk, 4)
    qd = blocks(z[:, Z_QD:Z_QD + 256], 2)
    kd = blocks(z[:, Z_KD:Z_KD + 256], 2)
    kr = z[:, Z_KR:Z_KR + LANES]
    if rope:
        ca, sa, c32, s32, c64, s64 = (r[...] for r in (ca_ref, sa_ref, c32_ref, s32_ref, c64_ref, s64_ref))
        qa = [_rope_block(b, ca, sa, MLA_ROPE // 2) for b in qa]
        qkb = [_rope_block(b, c64, s64, HEAD_DIM // 2) for b in qkb]
        qd = [_rope_block(b, c32, s32, DIFF_QK // 2) for b in qd]
        kd = [_rope_block(b, c32, s32, DIFF_QK // 2) for b in kd]
        kr = _rope_block(kr, c32, s32, MLA_ROPE // 2)

    vg = z[:, Z_VG:Z_VG + 128]
    vd = z[:, Z_VD:Z_VD + 256]
    ckv_b = ckv.astype(bf16)
    k_a = _dot(jnp.concatenate([ckv_b, kr.astype(bf16)], axis=1), wka_ref[0])
    v_a = _dot(ckv_b, wuv_ref[0].astype(bf16))

    for j in range(3):
        q_ref[:, Q_A + LANES * j:Q_A + LANES * (j + 1)] = (qa[j] * (HEAD_DIM ** -0.5 * LOG2E)).astype(bf16)
        q_ref[:, Q_G + LANES * j:Q_G + LANES * (j + 1)] = (qkb[j] * (HEAD_DIM ** -0.5 * LOG2E)).astype(bf16)
    for j in range(2):
        q_ref[:, Q_D + LANES * j:Q_D + LANES * (j + 1)] = (qd[j] * (DIFF_QK ** -0.5 * LOG2E)).astype(bf16)
        kv_ref[:, KV_KD + LANES * j:KV_KD + LANES * (j + 1)] = kd[j].astype(bf16)
    kv_ref[:, KV_KA:KV_KA + 384] = k_a.astype(bf16)
    kv_ref[:, KV_VA:KV_VA + 384] = v_a.astype(bf16)
    kv_ref[:, KV_KG:KV_KG + 128] = qkb[3].astype(bf16)
    kv_ref[:, KV_VG:KV_VG + 128] = vg.astype(bf16)
    kv_ref[:, KV_VD:KV_VD + 256] = vd.astype(bf16)
    if not rope:
        rows = [ckv, None, qkb[3], vg, jnp.concatenate(kd, axis=1), vd]
        for out, new in zip(cache_refs, rows):
            if new is None:
                assert out.shape[0] == 1
                out[0, 0] = kr.T[:MLA_ROPE, :]
            else:
                reqs, _, seq, width = out.shape
                out[:, 0] = new.reshape(reqs, seq, width)


def _pre_latent(layer, x, n, row0, seq, mod, mod_row, w, tile, rope_tabs, resid=None):
    lay = lambda t: (layer, 0, 0)
    row = lambda t: (t, 0)
    off = row0 // tile
    src_row = lambda t: (off + t, 0)
    in_specs = [
        pl.BlockSpec((tile, D_MODEL), src_row),
        pl.BlockSpec((1, 1, N_MOD, D_MODEL), lambda t: (layer, mod_row(t * tile), 0, 0)),
        pl.BlockSpec((1, 1, D_MODEL), lay),
        pl.BlockSpec((1, IN_COLS, D_MODEL), lay),
        pl.BlockSpec((1, 1, KV_RANK), lay),
        pl.BlockSpec((1, 1, 512), lay),
        pl.BlockSpec((512, 512), lambda t: (0, 0)),
        pl.BlockSpec((1, 256, 384), lay),
        pl.BlockSpec((1, KV_RANK, 384), lay),
    ]
    args = [x, mod, w["g1"], w["w_in"], w["kv_g"], w["qk_g"], w["seg"], w["w_ka"], w["w_uv"]]
    if resid is not None:
        in_specs += [pl.BlockSpec((tile, D_MODEL), src_row),
                     pl.BlockSpec((1, 1, N_MOD, D_MODEL), lambda t: (layer - 1, mod_row(t * tile), 0, 0))]
        args += [resid, mod]
    per_b = seq // tile
    in_specs += [pl.BlockSpec((tile, LANES), lambda t: (t % per_b, 0))] * 6
    args += list(rope_tabs)
    out_shape = [jax.ShapeDtypeStruct((n, Q_COLS), bf16), jax.ShapeDtypeStruct((n, KV_COLS), bf16)]
    out_specs = [pl.BlockSpec((tile, Q_COLS), row), pl.BlockSpec((tile, KV_COLS), row)]
    if resid is not None:
        out_shape.append(jax.ShapeDtypeStruct((n, D_MODEL), f32))
        out_specs.append(pl.BlockSpec((tile, D_MODEL), row))
    return pl.pallas_call(
        functools.partial(_pre_kernel, True, 0, resid is not None),
        out_shape=out_shape,
        grid=(n // tile,),
        in_specs=in_specs,
        out_specs=out_specs,
        scratch_shapes=[pltpu.VMEM((Z_COLS, D_MODEL), bf16)],
        compiler_params=_params(("arbitrary",)),
        name="pre_latent",
    )(*args)


PAST_CKV, PAST_KG, PAST_VG, PAST_KD, PAST_VD, PAST_KR, PAST_COLS = 0, 128, 256, 384, 640, 896, 928


def _cache_kernel(past_ref, wka_ref, wuv_ref, kv_ref):
    ckv_b = past_ref[0, 0, :, PAST_CKV:PAST_CKV + KV_RANK].astype(bf16)
    kr_b = past_ref[0, 0, :, PAST_KR:PAST_KR + MLA_ROPE].astype(bf16)
    wka = wka_ref[0]
    k_a = _dot(ckv_b, wka[:KV_RANK]) + _dot(kr_b, wka[KV_RANK:KV_RANK + MLA_ROPE])
    kv_ref[0, 0, :, KV_KA:KV_KA + 384] = k_a.astype(bf16)
    kv_ref[0, 0, :, KV_VA:KV_VA + 384] = _dot(ckv_b, wuv_ref[0].astype(bf16)).astype(bf16)
    kv_ref[0, 0, :, KV_KG:KV_KG + 128] = past_ref[0, 0, :, PAST_KG:PAST_KG + 128].astype(bf16)
    kv_ref[0, 0, :, KV_VG:KV_VG + 128] = past_ref[0, 0, :, PAST_VG:PAST_VG + 128].astype(bf16)
    kv_ref[0, 0, :, KV_KD:KV_KD + 256] = past_ref[0, 0, :, PAST_KD:PAST_KD + 256].astype(bf16)
    kv_ref[0, 0, :, KV_VD:KV_VD + 256] = past_ref[0, 0, :, PAST_VD:PAST_VD + 256].astype(bf16)


def _cache_rows(caches, w):
    ckv, kr, kg, vg, kd, vd = caches
    B = ckv.shape[0]
    flat = lambda a: a.reshape(B, DEPTH, PAST_LEN, -1)
    past = jnp.concatenate([flat(ckv), flat(kg), flat(vg), flat(kd), flat(vd), flat(kr)], axis=-1)
    return pl.pallas_call(
        _cache_kernel,
        out_shape=jax.ShapeDtypeStruct((DEPTH, B, PAST_LEN, KV_COLS), bf16),
        grid=(DEPTH, B),
        in_specs=[pl.BlockSpec((1, 1, PAST_LEN, PAST_COLS), lambda i, b: (b, i, 0, 0)),
                  pl.BlockSpec((1, 256, 384), lambda i, b: (i, 0, 0)), pl.BlockSpec((1, KV_RANK, 384), lambda i, b: (i, 0, 0))],
        out_specs=pl.BlockSpec((1, 1, PAST_LEN, KV_COLS), lambda i, b: (i, b, 0, 0)),
        compiler_params=_params(("arbitrary", "arbitrary")),
        name="cache_rows",
    )(past, w["w_ka"], w["w_uv"])


_SCORE_HEADS = (
    [(Q_A + 64 * h, KV_KA + 64 * h, 64, KV_VA + MLA_V * h) for h in range(MLA_HEADS)]
    + [(Q_G + 64 * h, KV_KG + 64 * (h // GQA_GROUP), 64, KV_VG + 64 * (h // GQA_GROUP)) for h in range(GQA_HEADS)]
    + [(Q_D + 64 * h + DIFF_QK * c, KV_KD + 64 * h + DIFF_QK * c, DIFF_QK, KV_VD + DIFF_V * h)
       for h in range(DIFF_HEADS) for c in range(2)])


def _attn_kernel(lam_init, per_round, n_src, q_ref, *refs):
    kv_refs = refs[:n_src]
    lam_ref, dg_ref, o_ref, s_ref, p_ref = refs[n_src:]
    spans, start = [], 0
    for r in kv_refs:
        spans.append((r, start, r.shape[2]))
        start += r.shape[2]
    mxu_sum = start >= MXU_SUM_MIN_KEYS

    outs = []
    for first in range(0, len(_SCORE_HEADS), per_round):
        chunk = _SCORE_HEADS[first:first + per_round]
        for j, (q_off, k_off, width, _) in enumerate(chunk):
            for r, lo, size in spans:
                s_ref[j, :, lo:lo + size] = _dot_nt(q_ref[:, q_off:q_off + width], r[0, 0, :, k_off:k_off + width])
        s = s_ref[...]
        p = jnp.exp2(s - jnp.max(s, axis=-1, keepdims=True))
        if mxu_sum:
            p_ref[...] = p.astype(bf16)
            for j, (_, _, _, v_off) in enumerate(chunk):
                o = sum(_dot(p_ref[j, :, lo:lo + size],
                             jnp.concatenate([r[0, 0, :, v_off:v_off + DIFF_V],
                                              jnp.ones((size, LANES - DIFF_V), bf16)], axis=1))
                        for r, lo, size in spans)
                outs.append((o * pltpu.roll(1.0 / o, DIFF_V, 1))[:, :DIFF_V])
        else:
            inv = 1.0 / jnp.sum(p, axis=-1, keepdims=True)
            p_ref[...] = p.astype(bf16)
            for j, (_, _, _, v_off) in enumerate(chunk):
                o = sum(_dot(p_ref[j, :, lo:lo + size], r[0, 0, :, v_off:v_off + DIFF_V]) for r, lo, size in spans)
                outs.append(o * inv[j])

    lp = lam_ref[0]
    e1 = jnp.exp(jnp.sum(lp[0:1] * lp[1:2], axis=-1, keepdims=True))
    e2 = jnp.exp(jnp.sum(lp[2:3] * lp[3:4], axis=-1, keepdims=True))
    lam = e1 - e2 + lam_init
    heads = outs[:MLA_HEADS + GQA_HEADS]
    for h in range(DIFF_HEADS):
        o1, o2 = outs[MLA_HEADS + GQA_HEADS + 2 * h:MLA_HEADS + GQA_HEADS + 2 * h + 2]
        heads.append(_rms(o1 - lam * o2, DIFF_V) * dg_ref[0] * (1.0 - lam_init))
    for j in range(len(heads) // 2):
        o_ref[:, LANES * j:LANES * (j + 1)] = jnp.concatenate(heads[2 * j:2 * j + 2], axis=1).astype(bf16)


def _attention(layer, q, sources, w, tile, per_round):
    n = q.shape[0]
    lam_init = 0.8 - 0.6 * math.exp(-0.3 * layer)
    s_kv = sum(rows for _, rows, _ in sources)
    assert len(_SCORE_HEADS) % per_round == 0
    return pl.pallas_call(
        functools.partial(_attn_kernel, lam_init, per_round, len(sources)),
        out_shape=jax.ShapeDtypeStruct((n, D_MODEL), bf16),
        scratch_shapes=[pltpu.VMEM((per_round, tile, s_kv), f32), pltpu.VMEM((per_round, tile, s_kv), bf16)],
        grid=(n // tile,),
        in_specs=[pl.BlockSpec((tile, Q_COLS), lambda t: (t, 0))]
        + [pl.BlockSpec((1, 1, rows, KV_COLS), index) for _, rows, index in sources]
        + [pl.BlockSpec((1, 4, DIFF_QK), lambda t: (layer, 0, 0)), pl.BlockSpec((1, 1, DIFF_V), lambda t: (layer, 0, 0))],
        out_specs=pl.BlockSpec((tile, D_MODEL), lambda t: (t, 0)),
        compiler_params=_params(("arbitrary",)),
        name="attention",
    )(q, *[arr for arr, _, _ in sources], w["lam"], w["diff_g"])


def _post_kernel(merge, *refs):
    it = iter(refs)
    o_ref, x_ref, mod_ref, w_out_ref, g2_ref, wg_ref, bg_ref, we_ref, be_ref = (next(it) for _ in range(9))
    if merge:
        next(it), next(it), next(it)
    x1_ref, hx_ref, cls_ref, w_scr = (next(it) for _ in range(4))

    @pl.when(pl.program_id(0) == 0)
    def _():
        w_scr[...] = w_out_ref[0].astype(bf16)

    gate1 = mod_ref[0, 0, 2:3, :]
    shift2 = mod_ref[0, 0, 3:4, :]
    scale2 = mod_ref[0, 0, 4:5, :]
    x1 = x_ref[...] + gate1 * _dot(o_ref[...], w_scr[...])
    x1_ref[...] = x1
    h2 = ((_rms(x1, D_MODEL) * g2_ref[0]) * (1.0 + scale2) + shift2).astype(bf16)
    bits = pltpu.bitcast(h2.astype(f32), i32)
    hx_ref[:, 0:HX_HALF] = bits[:, 0:HX_HALF] | jax.lax.shift_right_logical(bits[:, HX_HALF:D_MODEL], 16)

    def first_lane(mask, lane_f):
        return jnp.min(jnp.where(mask, lane_f, float(LANES)), axis=-1, keepdims=True)

    gl = _dot(h2, wg_ref[0].astype(bf16)) + bg_ref[0]
    glane = jax.lax.broadcasted_iota(i32, gl.shape, 1).astype(f32)
    ge = jnp.exp(gl - jnp.max(gl, axis=-1, keepdims=True))
    gprob = ge / jnp.sum(ge, axis=-1, keepdims=True)
    g_top = jnp.max(gprob, axis=-1, keepdims=True)
    g_idx = first_lane(gprob == g_top, glane)

    el = _dot(h2, we_ref[0].astype(bf16)) + be_ref[0]
    lane = jax.lax.broadcasted_iota(i32, el.shape, 1)
    lane_f = lane.astype(f32)
    emask = (lane >> 2).astype(f32) == g_idx
    em = jnp.where(emask, el, -jnp.inf)
    ee = jnp.where(emask, jnp.exp(em - jnp.max(em, axis=-1, keepdims=True)), 0.0)
    ep = ee / jnp.sum(ee, axis=-1, keepdims=True)
    p1 = jnp.max(jnp.where(emask, ep, -1.0), axis=-1, keepdims=True)
    i1 = first_lane(emask & (ep == p1), lane_f)
    rest = emask & (lane_f != i1)
    p2 = jnp.max(jnp.where(rest, ep, -1.0), axis=-1, keepdims=True)
    i2 = first_lane(rest & (ep == p2), lane_f)
    tot = p1 + p2
    w1 = g_top * (p1 / tot)
    w2 = g_top * (p2 / tot)

    lo = jnp.minimum(i1, i2) - EXPERTS_PER_GROUP * g_idx
    hi = jnp.maximum(i1, i2) - EXPERTS_PER_GROUP * g_idx
    pair = jnp.where(lo == 0.0, hi - 1.0, jnp.where(lo == 1.0, jnp.where(hi == 3.0, 3.0, 4.0), 5.0))
    cls_ref[...] = (N_PAIRS * g_idx + pair).astype(i32)
    g_lo = jnp.where(i1 < i2, w1, w2)
    g_hi = jnp.where(i1 < i2, w2, w1)
    tail_lane = jax.lax.broadcasted_iota(i32, (h2.shape[0], LANES), 1)
    hx_ref[:, HX_HALF:HX_COLS] = pltpu.bitcast(
        jnp.where(tail_lane == 0, g_lo, jnp.where(tail_lane == 1, g_hi, 0.0)), i32)


def _post(layer, o, x, row0, total, mod, mod_row, w, tile, merged=None):
    n = o.shape[0]
    lay = lambda t: (layer, 0, 0)
    row = lambda t: (t, 0)
    off = row0 // tile
    out_row = lambda t: (off + t, 0)
    in_specs = [
        pl.BlockSpec((tile, D_MODEL), row),
        pl.BlockSpec((tile, D_MODEL), row),
        pl.BlockSpec((1, 1, N_MOD, D_MODEL), lambda t: (layer, mod_row(t * tile), 0, 0)),
        pl.BlockSpec((1, D_MODEL, D_MODEL), lay),
        pl.BlockSpec((1, 1, D_MODEL), lay),
        pl.BlockSpec((1, D_MODEL, N_GROUPS), lay),
        pl.BlockSpec((1, 1, N_GROUPS), lay),
        pl.BlockSpec((1, D_MODEL, N_EXPERTS), lay),
        pl.BlockSpec((1, 1, N_EXPERTS), lay),
    ]
    args = [o, x, mod, w["w_out"], w["g2"], w["w_grp"], w["b_grp"], w["w_rtr"], w["b_rtr"]]
    aliases = {}
    if merged is not None:
        aliases = {len(args) + j: j for j in range(3)}
        in_specs += [pl.BlockSpec(memory_space=pl.ANY)] * 3
        args += list(merged)
    return pl.pallas_call(
        functools.partial(_post_kernel, merged is not None),
        out_shape=[jax.ShapeDtypeStruct((total, D_MODEL), f32), jax.ShapeDtypeStruct((total, HX_COLS), i32),
                   jax.ShapeDtypeStruct((total, 1), i32)],
        grid=(n // tile,),
        in_specs=in_specs,
        out_specs=[pl.BlockSpec((tile, D_MODEL), out_row), pl.BlockSpec((tile, HX_COLS), out_row),
                   pl.BlockSpec((tile, 1), out_row)],
        scratch_shapes=[pltpu.VMEM((D_MODEL, D_MODEL), bf16)],
        input_output_aliases=aliases,
        compiler_params=_params(("arbitrary",)),
        name="post_attention",
    )(*args)


def _ctx_kernel(lam_init, n_prev, resid, *refs):
    it = iter(refs)
    pre_in = [next(it) for _ in range(9 + (2 if resid else 0) + n_prev)]
    lam_ref, dg_ref = next(it), next(it)
    post_w = [next(it) for _ in range(6)]
    cache_refs = [next(it) for _ in range(len(CACHE_WIDTHS))]
    x1_ref, hx_ref, cls_ref = (next(it) for _ in range(3))
    w_in_scr, q_scr, kv_scr, o_scr, s_scr, p_scr, w_out_scr, x2_scr = (next(it) for _ in range(8))
    x_ref, mod_ref = pre_in[0], pre_in[1]
    x2_ref = [x2_scr] if resid else []

    _pre_kernel(False, n_prev, resid, *pre_in, q_scr, kv_scr.at[0, 0], *x2_ref, *cache_refs, w_in_scr)
    _attn_kernel(lam_init, len(_SCORE_HEADS), 1, q_scr, kv_scr, lam_ref, dg_ref, o_scr, s_scr, p_scr)
    _post_kernel(False, o_scr, x2_scr if resid else x_ref, mod_ref, *post_w, x1_ref, hx_ref, cls_ref, w_out_scr)


def _ctx_layer(layer, x, n, seq, total, mod, w, prev_cache=(), resid=None):
    lay = lambda t: (layer, 0, 0)
    row = lambda t: (t, 0)
    mod_spec = lambda l: pl.BlockSpec((1, 1, N_MOD, D_MODEL), lambda t: (l, 0, 0, 0))
    in_specs = [
        pl.BlockSpec((seq, D_MODEL), row), mod_spec(layer),
        pl.BlockSpec((1, 1, D_MODEL), lay), pl.BlockSpec((1, IN_COLS, D_MODEL), lay),
        pl.BlockSpec((1, 1, KV_RANK), lay), pl.BlockSpec((1, 1, 512), lay), pl.BlockSpec((512, 512), lambda t: (0, 0)),
        pl.BlockSpec((1, 256, 384), lay), pl.BlockSpec((1, KV_RANK, 384), lay),
    ]
    args = [x, mod, w["g1"], w["w_in"], w["kv_g"], w["qk_g"], w["seg"], w["w_ka"], w["w_uv"]]
    if resid is not None:
        in_specs += [pl.BlockSpec((seq, D_MODEL), row), mod_spec(layer - 1)]
        args += [resid, mod]
    out_shape, out_specs, aliases = [], [], {}
    for j, width in enumerate(CACHE_WIDTHS):
        if prev_cache:
            aliases[len(args)] = len(out_shape)
            in_specs.append(pl.BlockSpec(memory_space=pl.ANY))
            args.append(prev_cache[j])
        shape = (MLA_ROPE, seq) if j == 1 else (seq, width)
        out_shape.append(jax.ShapeDtypeStruct((n // seq, DEPTH) + shape, f32))
        out_specs.append(pl.BlockSpec((1, 1) + shape, lambda t: (t, layer, 0, 0)))
    in_specs += [
        pl.BlockSpec((1, 4, DIFF_QK), lay), pl.BlockSpec((1, 1, DIFF_V), lay),
        pl.BlockSpec((1, D_MODEL, D_MODEL), lay), pl.BlockSpec((1, 1, D_MODEL), lay),
        pl.BlockSpec((1, D_MODEL, N_GROUPS), lay), pl.BlockSpec((1, 1, N_GROUPS), lay),
        pl.BlockSpec((1, D_MODEL, N_EXPERTS), lay), pl.BlockSpec((1, 1, N_EXPERTS), lay),
    ]
    args += [w["lam"], w["diff_g"], w["w_out"], w["g2"], w["w_grp"], w["b_grp"], w["w_rtr"], w["b_rtr"]]
    out_shape += [jax.ShapeDtypeStruct((total, D_MODEL), f32), jax.ShapeDtypeStruct((total, HX_COLS), i32),
                  jax.ShapeDtypeStruct((total, 1), i32)]
    out_specs += [pl.BlockSpec((seq, D_MODEL), row), pl.BlockSpec((seq, HX_COLS), row), pl.BlockSpec((seq, 1), row)]
    heads = len(_SCORE_HEADS)
    return pl.pallas_call(
        functools.partial(_ctx_kernel, 0.8 - 0.6 * math.exp(-0.3 * layer), len(prev_cache), resid is not None),
        out_shape=out_shape,
        grid=(n // seq,),
        in_specs=in_specs,
        out_specs=out_specs,
        scratch_shapes=[pltpu.VMEM((Z_COLS, D_MODEL), bf16), pltpu.VMEM((seq, Q_COLS), bf16),
                        pltpu.VMEM((1, 1, seq, KV_COLS), bf16), pltpu.VMEM((seq, D_MODEL), bf16),
                        pltpu.VMEM((heads, seq, seq), f32), pltpu.VMEM((heads, seq, seq), bf16),
                        pltpu.VMEM((D_MODEL, D_MODEL), bf16), pltpu.VMEM((seq, D_MODEL), f32)],
        input_output_aliases=aliases,
        compiler_params=_params(("arbitrary",)),
        name="context_layer",
    )(*args)


PLAN_CHUNK = 1024
TAB_ROWS = LANES


def _plan_kernel(cls_ref, slot_ref, tab_ref, rank_scr):
    n = cls_ref.shape[0]
    lane = jax.lax.broadcasted_iota(i32, (PLAN_CHUNK, LANES), 1)
    r = jax.lax.broadcasted_iota(i32, (PLAN_CHUNK, PLAN_CHUNK), 0)
    c = jax.lax.broadcasted_iota(i32, (PLAN_CHUNK, PLAN_CHUNK), 1)
    before = (c < r).astype(bf16)

    def count(b, seen):
        rows = pl.ds(pl.multiple_of(b * PLAN_CHUNK, PLAN_CHUNK), PLAN_CHUNK)
        onehot = (cls_ref[rows, :] == lane).astype(f32)
        ahead = _dot(before, onehot.astype(bf16)) + seen
        rank_scr[rows, :] = jnp.sum(onehot * ahead, axis=-1, keepdims=True)
        return seen + jnp.sum(onehot, axis=0, keepdims=True)

    counts = jax.lax.fori_loop(0, n // PLAN_CHUNK, count, jnp.zeros((1, LANES), f32))
    tiles = jnp.floor((counts + (EXPERT_TILE - 1)) * (1.0 / EXPERT_TILE))
    rr = jax.lax.broadcasted_iota(i32, (LANES, LANES), 0)
    cc = jax.lax.broadcasted_iota(i32, (LANES, LANES), 1)
    ends = _dot(jnp.broadcast_to(tiles, (8, LANES)).astype(bf16), (rr <= cc).astype(bf16))[0:1]
    starts = ends - tiles

    def place(b, carry):
        rows = pl.ds(pl.multiple_of(b * PLAN_CHUNK, PLAN_CHUNK), PLAN_CHUNK)
        first = jnp.sum(jnp.where(cls_ref[rows, :] == lane, starts, 0.0), axis=-1, keepdims=True)
        slot_ref[rows, :] = (first * EXPERT_TILE + rank_scr[rows, :]).astype(i32)
        return carry

    jax.lax.fori_loop(0, n // PLAN_CHUNK, place, 0)

    tl = jax.lax.broadcasted_iota(i32, (TAB_ROWS, LANES), 1)
    n_tiles = jnp.sum(jnp.where(tl[0:1] == N_CLASSES - 1, ends, 0.0), axis=-1, keepdims=True)
    k = jnp.minimum(jax.lax.broadcasted_iota(i32, (TAB_ROWS, 1), 0).astype(f32), n_tiles - 1.0)
    cls_k = jnp.sum(jnp.where((tl < N_CLASSES) & (ends <= k), 1.0, 0.0), axis=-1, keepdims=True)
    cls_k = jnp.minimum(cls_k, N_CLASSES - 1.0)
    mine = tl.astype(f32) == cls_k
    used = jnp.sum(jnp.where(mine, counts, 0.0), axis=-1, keepdims=True)
    first = jnp.sum(jnp.where(mine, starts, 0.0), axis=-1, keepdims=True)
    valid = jnp.clip(used - (k - first) * EXPERT_TILE, 0.0, float(EXPERT_TILE))
    group = jnp.floor((cls_k + 0.5) * (1.0 / N_PAIRS))
    pair = cls_k - N_PAIRS * group
    lo = hi = jnp.zeros_like(pair)
    for p in range(N_PAIRS):
        lo = jnp.where(pair == p, float(PAIR_LO[p]), lo)
        hi = jnp.where(pair == p, float(PAIR_HI[p]), hi)
    e_lo = EXPERTS_PER_GROUP * group + lo
    e_hi = EXPERTS_PER_GROUP * group + hi
    tab = jnp.where(tl == 0, e_lo, jnp.where(tl == 1, e_hi, jnp.where(tl == 2, valid, jnp.where(tl == 3, n_tiles, 0.0))))
    tab_ref[...] = tab.astype(i32)


def _plan(cls, max_tiles):
    n = cls.shape[0]
    assert n % PLAN_CHUNK == 0 and max_tiles <= TAB_ROWS
    slot, tab = pl.pallas_call(
        _plan_kernel,
        out_shape=[jax.ShapeDtypeStruct((n, 1), i32), jax.ShapeDtypeStruct((TAB_ROWS, LANES), i32)],
        scratch_shapes=[pltpu.VMEM((n, 1), f32)],
        compiler_params=_params(None),
        name="dispatch_plan",
    )(cls)
    return slot.reshape(n), tab[:max_tiles, 0], tab[:max_tiles, 1], tab[:max_tiles, 2], tab[0, 3:4]


def _move_rows(src, idx, n_out, scatter):
    n = idx.shape[0]
    width = src.shape[1]
    per_worker = n // (SC_CORES * SC_SUBCORES)
    assert n % SC_ROWS == 0
    chunk = max(c for c in (64, 40, 32, 16, 8)
                if per_worker % c == 0 and 2 * c * width * src.dtype.itemsize <= SC_BUFFER_BYTES)
    n_chunks = per_worker // chunk
    mesh = plsc.VectorSubcoreMesh(core_axis_name="c", subcore_axis_name="s")

    @functools.partial(
        pl.kernel, mesh=mesh, out_type=jax.ShapeDtypeStruct((n_out, width), src.dtype),
        scratch_types=[pltpu.VMEM((chunk,), i32), pltpu.VMEM((chunk,), i32),
                       pltpu.VMEM((chunk, width), src.dtype), pltpu.VMEM((chunk, width), src.dtype),
                       pltpu.SemaphoreType.DMA, pltpu.SemaphoreType.DMA, pltpu.SemaphoreType.DMA,
                       pltpu.SemaphoreType.DMA])
    def move(src_hbm, idx_hbm, out_hbm, idx0, idx1, rows0, rows1, in0, in1, out0, out1):
        wid = jax.lax.axis_index("s") * SC_CORES + jax.lax.axis_index("c")
        base = wid * per_worker
        idx_v, rows_v, sem_in, sem_out = (idx0, idx1), (rows0, rows1), (in0, in1), (out0, out1)

        def fill(j):
            b = j % 2
            rows = pl.ds(base + j * chunk, chunk)
            pltpu.sync_copy(idx_hbm.at[rows], idx_v[b])
            src_rows = src_hbm.at[rows] if scatter else src_hbm.at[idx_v[b]]
            return pltpu.async_copy(src_rows, rows_v[b], sem_in[b])

        def drain(j):
            b = j % 2
            dst_rows = out_hbm.at[idx_v[b]] if scatter else out_hbm.at[pl.ds(base + j * chunk, chunk)]
            return pltpu.async_copy(rows_v[b], dst_rows, sem_out[b])

        fills, drains = {0: fill(0)}, {}
        for j in range(n_chunks):
            if j + 1 < n_chunks:
                if j >= 1:
                    drains[j - 1].wait()
                fills[j + 1] = fill(j + 1)
            fills[j].wait()
            drains[j] = drain(j)
        for j in range(max(n_chunks - 2, 0), n_chunks):
            drains[j].wait()

    return move(src, idx)


def _expert_kernel(lo_ref, hi_ref, valid_ref, nt_ref, xs_ref, wg_ref, wu_ref, wd_ref, ys_ref, wg_scr, wu_scr, wd_scr):
    k = pl.program_id(0)
    prev = jnp.maximum(k - 1, 0)

    @pl.when((k == 0) | (lo_ref[k] // EXPERTS_PER_GROUP != lo_ref[prev] // EXPERTS_PER_GROUP))
    def _():
        for j in range(EXPERTS_PER_GROUP):
            wg_scr[j] = wg_ref[0, 0, j].astype(bf16)
            wu_scr[j] = wu_ref[0, 0, j].astype(bf16)
            wd_scr[j] = wd_ref[0, 0, j].astype(bf16)

    @pl.when(k < nt_ref[0])
    def _():
        live = jax.lax.broadcasted_iota(i32, (EXPERT_TILE, 1), 0) < valid_ref[k]
        words = jnp.where(live, xs_ref[:, 0:HX_HALF], 0)
        x = jnp.concatenate([pltpu.bitcast(words & -65536, f32), pltpu.bitcast(words << 16, f32)], axis=1).astype(bf16)
        gates = pltpu.bitcast(jnp.where(live, xs_ref[:, HX_HALF:HX_COLS], 0), f32)
        y = None
        for lane, e_ref in enumerate((lo_ref, hi_ref)):
            j = e_ref[k] % EXPERTS_PER_GROUP
            hid = _silu(_dot(x, wg_scr[j])) * _dot(x, wu_scr[j]) * gates[:, lane:lane + 1]
            part = _dot(hid.astype(bf16), wd_scr[j])
            y = part if y is None else y + part
        ys_ref[...] = y


def _experts(layer, xs, e_lo, e_hi, valid, n_tiles, w):
    max_tiles = e_lo.shape[0]
    row = lambda k, lo, hi, valid, nt: (jnp.minimum(k, nt[0] - 1), 0)
    group = lambda k, lo, hi, valid, nt: (layer, lo[k] // EXPERTS_PER_GROUP, 0, 0, 0)
    by_group = lambda a: a.reshape(DEPTH, N_GROUPS, EXPERTS_PER_GROUP, *a.shape[2:])
    return pl.pallas_call(
        _expert_kernel,
        out_shape=jax.ShapeDtypeStruct((max_tiles * EXPERT_TILE, D_MODEL), f32),
        grid_spec=pltpu.PrefetchScalarGridSpec(
            num_scalar_prefetch=4, grid=(max_tiles,),
            in_specs=[pl.BlockSpec((EXPERT_TILE, HX_COLS), row),
                      pl.BlockSpec((1, 1, EXPERTS_PER_GROUP, D_MODEL, D_FF_EXPERT), group),
                      pl.BlockSpec((1, 1, EXPERTS_PER_GROUP, D_MODEL, D_FF_EXPERT), group),
                      pl.BlockSpec((1, 1, EXPERTS_PER_GROUP, D_FF_EXPERT, D_MODEL), group)],
            out_specs=pl.BlockSpec((EXPERT_TILE, D_MODEL), row),
            scratch_shapes=[pltpu.VMEM((EXPERTS_PER_GROUP, D_MODEL, D_FF_EXPERT), bf16),
                            pltpu.VMEM((EXPERTS_PER_GROUP, D_MODEL, D_FF_EXPERT), bf16),
                            pltpu.VMEM((EXPERTS_PER_GROUP, D_FF_EXPERT, D_MODEL), bf16)]),
        compiler_params=_params(("arbitrary",)),
        name="experts",
    )(e_lo, e_hi, valid, n_tiles, xs, by_group(w["w_gate"]), by_group(w["w_up"]), by_group(w["w_down"]))


def _final_kernel(x1_ref, y_ref, mod_ref, fg_ref, o_ref):
    o_ref[...] = _rms(x1_ref[...] + mod_ref[0, 0, 5:6, :] * y_ref[...], D_MODEL) * fg_ref[...]


def _final(x1, y, n, row0, mod, mod_row, w, tile):
    off = row0 // tile
    src_row = lambda t: (off + t, 0)
    return pl.pallas_call(
        _final_kernel,
        out_shape=jax.ShapeDtypeStruct((n, D_MODEL), f32),
        grid=(n // tile,),
        in_specs=[pl.BlockSpec((tile, D_MODEL), src_row), pl.BlockSpec((tile, D_MODEL), src_row),
                  pl.BlockSpec((1, 1, N_MOD, D_MODEL), lambda t: (DEPTH - 1, mod_row(t * tile), 0, 0)),
                  pl.BlockSpec((1, D_MODEL), lambda t: (0, 0))],
        out_specs=pl.BlockSpec((tile, D_MODEL), lambda t: (t, 0)),
        compiler_params=_params(("arbitrary",)),
        name="final_norm",
    )(x1, y, mod, w["final_g"])


def _rope_tables(n_tokens):
    pos = np.arange(n_tokens)
    row = (pos // GRID_W).astype(np.float64)
    col = (pos % GRID_W).astype(np.float64)

    def cs(rot_dim):
        quarter = rot_dim // 4
        inv = ROPE_THETA ** (-np.arange(quarter, dtype=np.float64) / quarter)
        ang = np.concatenate([row[:, None] * inv, col[:, None] * inv], axis=-1)
        return np.cos(ang), np.sin(ang)

    c32, s32 = cs(MLA_ROPE)
    c64, s64 = cs(HEAD_DIM)
    ones = np.ones((n_tokens, MLA_NOPE))
    zeros = np.zeros((n_tokens, MLA_NOPE))

    def rep(parts):
        period = np.concatenate(parts, axis=-1)
        return jnp.asarray(np.tile(period, (1, LANES // period.shape[-1])), f32)

    return (rep([ones, c32, c32]), rep([zeros, -s32, s32]), rep([c32, c32]), rep([-s32, s32]),
            rep([c64, c64]), rep([-s64, s64]))


def _layout_weights(norm1_g, norm2_g, w_in, mla_kv_norm_g, mla_w_uk, mla_w_uv, gqa_q_norm_g, gqa_k_norm_g,
                    diff_lambda, diff_norm_g, w_out, moe_w_group, moe_b_group, moe_w_router, moe_b_router,
                    moe_w_gate, moe_w_up, moe_w_down, final_norm_g):
    eye = jnp.eye(MLA_ROPE, dtype=f32)
    top = jnp.concatenate([mla_w_uk, jnp.zeros((DEPTH, KV_RANK, MLA_HEADS, MLA_ROPE), f32)], axis=-1)
    mid = jnp.concatenate([jnp.zeros((MLA_ROPE, MLA_HEADS, MLA_NOPE), f32),
                           jnp.broadcast_to(eye[:, None, :], (MLA_ROPE, MLA_HEADS, MLA_ROPE))], axis=-1)
    w_ka = jnp.concatenate([top.reshape(DEPTH, KV_RANK, 384),
                            jnp.broadcast_to(mid.reshape(1, MLA_ROPE, 384), (DEPTH, MLA_ROPE, 384)),
                            jnp.zeros((DEPTH, 256 - KV_RANK - MLA_ROPE, 384), f32)], axis=1).astype(bf16)
    seg_id = np.arange(512) // HEAD_DIM
    seg = jnp.asarray(seg_id[:, None] == seg_id[None, :], bf16)
    qk_g = jnp.concatenate([jnp.tile(gqa_q_norm_g, (1, GQA_HEADS)), jnp.tile(gqa_k_norm_g, (1, GQA_KV_HEADS))], axis=-1)
    return dict(
        g1=norm1_g.reshape(DEPTH, 1, D_MODEL), g2=norm2_g.reshape(DEPTH, 1, D_MODEL),
        w_in=jnp.swapaxes(w_in, 1, 2),
        kv_g=mla_kv_norm_g.reshape(DEPTH, 1, KV_RANK), qk_g=qk_g.reshape(DEPTH, 1, 512), seg=seg, w_ka=w_ka,
        w_uv=mla_w_uv.reshape(DEPTH, KV_RANK, 384), lam=diff_lambda, diff_g=diff_norm_g.reshape(DEPTH, 1, DIFF_V),
        w_out=w_out, w_grp=moe_w_group, b_grp=moe_b_group.reshape(DEPTH, 1, N_GROUPS), w_rtr=moe_w_router,
        b_rtr=moe_b_router.reshape(DEPTH, 1, N_EXPERTS), w_gate=moe_w_gate, w_up=moe_w_up, w_down=moe_w_down,
        final_g=final_norm_g.reshape(1, D_MODEL))


PRE_TILE = 512
LAT_ATTN_TILE = 256
LAT_HEADS_PER_ROUND = 5
MXU_SUM_MIN_KEYS = 1024
POST_TILE = 512
FINAL_TILE = 1024


def kernel(x_prompt, x_sample, c, cache_mla_ckv, cache_mla_krope, cache_gqa_k, cache_gqa_v, cache_diff_k, cache_diff_v, c_ctx, norm1_g, norm2_g, w_mod, b_mod, w_in, mla_kv_norm_g, mla_w_uk, mla_w_uv, gqa_q_norm_g, gqa_k_norm_g, diff_lambda, diff_norm_g, w_out, moe_w_group, moe_b_group, moe_w_router, moe_b_router, moe_w_gate, moe_w_up, moe_w_down, final_norm_g):
    B, S, _ = x_prompt.shape
    Bl, Sl, _ = x_sample.shape
    n_ctx, n_lat = B * S, Bl * Sl
    total = n_ctx + n_lat
    assert Bl + 1 <= MOD_ROWS and DEPTH == 2 and total % SC_ROWS == 0
    slot_rows = -(-(total + N_CLASSES * EXPERT_TILE) // SC_ROWS) * SC_ROWS
    max_tiles = slot_rows // EXPERT_TILE
    w = _layout_weights(norm1_g, norm2_g, w_in, mla_kv_norm_g, mla_w_uk, mla_w_uv, gqa_q_norm_g, gqa_k_norm_g,
                        diff_lambda, diff_norm_g, w_out, moe_w_group, moe_b_group, moe_w_router, moe_b_router,
                        moe_w_gate, moe_w_up, moe_w_down, final_norm_g)
    cond = jnp.concatenate([c_ctx[None, :], c, jnp.zeros((MOD_ROWS - 1 - Bl, D_MODEL), f32)], axis=0)
    mod = _modulation(cond, w_mod, b_mod).reshape(DEPTH, MOD_ROWS, N_MOD, D_MODEL)
    ctx_row = lambda token: 0
    lat_row = lambda token: 1 + token // Sl
    tabs = _rope_tables(Sl)
    kv_past = _cache_rows((cache_mla_ckv, cache_mla_krope, cache_gqa_k, cache_gqa_v, cache_diff_k, cache_diff_v), w)
    per_b = Sl // LAT_ATTN_TILE

    x_ctx, x_lat = x_prompt.reshape(n_ctx, D_MODEL), x_sample.reshape(n_lat, D_MODEL)
    cache = ()
    x1 = y = None
    for i in range(DEPTH):
        if i == 0:
            *cache, x1_c, hx_c, cls_c = _ctx_layer(i, x_ctx, n_ctx, S, total, mod, w)
        else:
            *cache, x1_c, hx_c, cls_c = _ctx_layer(i, x1, n_ctx, S, total, mod, w, prev_cache=cache, resid=y)
        if i == 0:
            q_l, kv_l = _pre_latent(i, x_lat, n_lat, 0, Sl, mod, lat_row, w, PRE_TILE, tabs)
        else:
            q_l, kv_l, x_lat = _pre_latent(i, x1, n_lat, n_ctx, Sl, mod, lat_row, w, PRE_TILE, tabs, resid=y)
        past = (kv_past, PAST_LEN, lambda t, i=i: (i, t // per_b, 0, 0))
        own = (kv_l.reshape(1, Bl, Sl, KV_COLS), Sl, lambda t: (0, t // per_b, 0, 0))
        o_l = _attention(i, q_l, [past, own], w, LAT_ATTN_TILE, LAT_HEADS_PER_ROUND)
        x1, hx, cls = _post(i, o_l, x_lat, n_ctx, total, mod, lat_row, w, POST_TILE, merged=(x1_c, hx_c, cls_c))
        if i == DEPTH - 1:
            outs = (cache[0], jnp.swapaxes(cache[1], 2, 3), cache[2].reshape(B, DEPTH, S, GQA_KV_HEADS, HEAD_DIM),
                    cache[3].reshape(B, DEPTH, S, GQA_KV_HEADS, HEAD_DIM),
                    cache[4].reshape(B, DEPTH, S, DIFF_HEADS, 2, DIFF_QK),
                    cache[5].reshape(B, DEPTH, S, DIFF_HEADS, DIFF_V))
            cls, outs = jax.lax.optimization_barrier((cls, outs))
        slot, e_lo, e_hi, valid, n_tiles = _plan(cls, max_tiles)
        xs = _move_rows(hx, slot, slot_rows, scatter=True)
        ys = _experts(i, xs, e_lo, e_hi, valid, n_tiles, w)
        y = _move_rows(ys, slot, total, scatter=False)

    y_prompt = _final(x1, y, n_ctx, 0, mod, ctx_row, w, FINAL_TILE).reshape(B, S, D_MODEL)
    y_sample = _final(x1, y, n_lat, n_ctx, mod, lat_row, w, FINAL_TILE).reshape(Bl, Sl, D_MODEL)
    return (y_prompt, y_sample, *outs)
```

```python
import functools
import math

import jax
import jax.numpy as jnp
import numpy as np
from jax.experimental import pallas as pl
from jax.experimental.pallas import tpu as pltpu
from jax.experimental.pallas import tpu_sc as plsc

D_MODEL = 1024
DEPTH = 2
PAST_LEN = 512
GRID_W = 64
ROPE_THETA = 10000.0
EPS = 1e-6
LOG2E = 1.4426950408889634
N_MOD = 6
HEAD_DIM = 64
MLA_HEADS = 6
MLA_NOPE = 32
MLA_ROPE = 32
MLA_V = 64
KV_RANK = 128
GQA_HEADS = 6
GQA_KV_HEADS = 2
GQA_GROUP = GQA_HEADS // GQA_KV_HEADS
DIFF_HEADS = 4
DIFF_QK = 32
DIFF_V = 64
N_GROUPS = 4
EXPERTS_PER_GROUP = 4
N_EXPERTS = N_GROUPS * EXPERTS_PER_GROUP
D_FF_EXPERT = 256

LANES = 128
MOD_ROWS = 8

IN_COLS = 1952
IN_KR = 512
Z_QA, Z_CKV, Z_QG, Z_KG, Z_VG, Z_QD, Z_KD, Z_VD, Z_KR = 0, 384, 512, 896, 1024, 1152, 1408, 1664, 1920
Z_COLS = 2048
Q_A, Q_G, Q_D, Q_COLS = 0, 384, 768, 1024
KV_KA, KV_VA, KV_KG, KV_VG, KV_KD, KV_VD, KV_COLS = 0, 384, 768, 896, 1024, 1280, 1536
CACHE_WIDTHS = (128, 32, 128, 128, 256, 256)

PAIR_LO = (0, 0, 0, 1, 1, 2)
PAIR_HI = (1, 2, 3, 3, 2, 3)
N_PAIRS = len(PAIR_LO)
N_CLASSES = N_GROUPS * N_PAIRS
HX_HALF = D_MODEL // 2
HX_COLS = HX_HALF + LANES
EXPERT_TILE = 256

SC_CORES, SC_SUBCORES = 2, 16
SC_BUFFER_BYTES = 400 * 1024
SC_ROWS = SC_CORES * SC_SUBCORES * 8

VMEM_LIMIT = 56 * 1024 * 1024

bf16 = jnp.bfloat16
f32 = jnp.float32
i32 = jnp.int32


def _dot(a, b):
    return jnp.dot(a, b, preferred_element_type=f32)


def _dot_nt(a, b):
    return jax.lax.dot_general(a, b, (((1,), (1,)), ((), ())), preferred_element_type=f32)


def _rms(x, width):
    return x * jax.lax.rsqrt(jnp.sum(x * x, axis=-1, keepdims=True) * (1.0 / width) + EPS)


def _silu(x):
    return x * (1.0 / (1.0 + jnp.exp(-x)))


def _params(sem):
    return pltpu.CompilerParams(dimension_semantics=sem, vmem_limit_bytes=VMEM_LIMIT)


def _mod_kernel(cond_ref, w_ref, b_ref, o_ref):
    o_ref[0] = _dot(_silu(cond_ref[...]).astype(bf16), w_ref[0].astype(bf16)) + b_ref[0]


def _modulation(cond, w_mod, b_mod):
    return pl.pallas_call(
        _mod_kernel,
        out_shape=jax.ShapeDtypeStruct((DEPTH, MOD_ROWS, N_MOD * D_MODEL), f32),
        grid=(DEPTH, N_MOD),
        in_specs=[
            pl.BlockSpec((MOD_ROWS, D_MODEL), lambda i, j: (0, 0)),
            pl.BlockSpec((1, D_MODEL, D_MODEL), lambda i, j: (i, 0, j)),
            pl.BlockSpec((1, 1, D_MODEL), lambda i, j: (i, 0, j)),
        ],
        out_specs=pl.BlockSpec((1, MOD_ROWS, D_MODEL), lambda i, j: (i, 0, j)),
        compiler_params=_params(("arbitrary", "arbitrary")),
        name="modulation",
    )(cond, w_mod, b_mod.reshape(DEPTH, 1, N_MOD * D_MODEL))


def _swap_halves(x, half):
    lane = jax.lax.broadcasted_iota(i32, x.shape, 1)
    fwd = pltpu.roll(x, LANES - half, 1)
    bwd = pltpu.roll(x, half, 1)
    return jnp.where((lane & (2 * half - 1)) < half, fwd, bwd)


def _rope_block(x, cos, sin, half):
    return x * cos + _swap_halves(x, half) * sin


def _pre_kernel(rope, n_prev, resid, *refs):
    it = iter(refs)
    x_ref, mod_ref, g1_ref, w_in_ref, kvg_ref, qkg_ref, seg_ref, wka_ref, wuv_ref = (next(it) for _ in range(9))
    if resid:
        y_ref, pmod_ref = next(it), next(it)
    if rope:
        ca_ref, sa_ref, c32_ref, s32_ref, c64_ref, s64_ref = (next(it) for _ in range(6))
    prev_refs = [next(it) for _ in range(n_prev)]
    q_ref, kv_ref = next(it), next(it)
    if resid:
        x2_ref = next(it)
    cache_refs = [] if rope else [next(it) for _ in range(len(CACHE_WIDTHS))]
    w_scr = next(it)

    @pl.when(pl.program_id(0) == 0)
    def _():
        w_scr[0:IN_KR] = w_in_ref[0, 0:IN_KR].astype(bf16)
        w_scr[IN_KR:Z_KR] = w_in_ref[0, IN_KR + MLA_ROPE:IN_COLS].astype(bf16)
        w_scr[Z_KR:Z_KR + MLA_ROPE] = w_in_ref[0, IN_KR:IN_KR + MLA_ROPE].astype(bf16)
        w_scr[Z_KR + MLA_ROPE:Z_COLS] = jnp.zeros((Z_COLS - Z_KR - MLA_ROPE, D_MODEL), bf16)

    x = x_ref[...]
    if resid:
        x = x + pmod_ref[0, 0, 5:6, :] * y_ref[...]
        x2_ref[...] = x
    shift1 = mod_ref[0, 0, 0:1, :]
    scale1 = mod_ref[0, 0, 1:2, :]
    h = (_rms(x, D_MODEL) * g1_ref[0]) * (1.0 + scale1) + shift1
    z = _dot_nt(h.astype(bf16), w_scr[...])

    ckv = _rms(z[:, Z_CKV:Z_CKV + KV_RANK], KV_RANK) * kvg_ref[0]

    qk = z[:, Z_QG:Z_VG]
    sq = qk * qk
    sq_hi = sq.astype(bf16)
    sq_lo = (sq - sq_hi.astype(f32)).astype(bf16)
    seg = seg_ref[...]
    ms = (_dot(sq_hi, seg) + _dot(sq_lo, seg)) * (1.0 / HEAD_DIM)
    qk = qk * jax.lax.rsqrt(ms + EPS) * qkg_ref[0]

    def blocks(arr, n):
        return [arr[:, LANES * j:LANES * (j + 1)] for j in range(n)]

    qa = blocks(z[:, Z_QA:Z_QA + 384], 3)
    qkb = blocks(qk, 4)
    qd = blocks(z[:, Z_QD:Z_QD + 256], 2)
    kd = blocks(z[:, Z_KD:Z_KD + 256], 2)
    kr = z[:, Z_KR:Z_KR + LANES]
    if rope:
        ca, sa, c32, s32, c64, s64 = (r[...] for r in (ca_ref, sa_ref, c32_ref, s32_ref, c64_ref, s64_ref))
        qa = [_rope_block(b, ca, sa, MLA_ROPE // 2) for b in qa]
        qkb = [_rope_block(b, c64, s64, HEAD_DIM // 2) for b in qkb]
        qd = [_rope_block(b, c32, s32, DIFF_QK // 2) for b in qd]
        kd = [_rope_block(b, c32, s32, DIFF_QK // 2) for b in kd]
        kr = _rope_block(kr, c32, s32, MLA_ROPE // 2)

    vg = z[:, Z_VG:Z_VG + 128]
    vd = z[:, Z_VD:Z_VD + 256]
    ckv_b = ckv.astype(bf16)
    k_a = _dot(jnp.concatenate([ckv_b, kr.astype(bf16)], axis=1), wka_ref[0])
    v_a = _dot(ckv_b, wuv_ref[0].astype(bf16))

    for j in range(3):
        q_ref[:, Q_A + LANES * j:Q_A + LANES * (j + 1)] = (qa[j] * (HEAD_DIM ** -0.5 * LOG2E)).astype(bf16)
        q_ref[:, Q_G + LANES * j:Q_G + LANES * (j + 1)] = (qkb[j] * (HEAD_DIM ** -0.5 * LOG2E)).astype(bf16)
    for j in range(2):
        q_ref[:, Q_D + LANES * j:Q_D + LANES * (j + 1)] = (qd[j] * (DIFF_QK ** -0.5 * LOG2E)).astype(bf16)
        kv_ref[:, KV_KD + LANES * j:KV_KD + LANES * (j + 1)] = kd[j].astype(bf16)
    kv_ref[:, KV_KA:KV_KA + 384] = k_a.astype(bf16)
    kv_ref[:, KV_VA:KV_VA + 384] = v_a.astype(bf16)
    kv_ref[:, KV_KG:KV_KG + 128] = qkb[3].astype(bf16)
    kv_ref[:, KV_VG:KV_VG + 128] = vg.astype(bf16)
    kv_ref[:, KV_VD:KV_VD + 256] = vd.astype(bf16)
    if not rope:
        rows = [ckv, None, qkb[3], vg, jnp.concatenate(kd, axis=1), vd]
        for out, new in zip(cache_refs, rows):
            if new is None:
                seq = out.shape[3]
                for r in range(out.shape[0]):
                    out[r, 0] = kr[r * seq:(r + 1) * seq].T[:MLA_ROPE, :]
            else:
                reqs, _, seq, width = out.shape
                out[:, 0] = new.reshape(reqs, seq, width)


def _pre_latent(layer, x, n, row0, seq, mod, mod_row, w, tile, rope_tabs, resid=None):
    lay = lambda t: (layer, 0, 0)
    row = lambda t: (t, 0)
    off = row0 // tile
    src_row = lambda t: (off + t, 0)
    in_specs = [
        pl.BlockSpec((tile, D_MODEL), src_row),
        pl.BlockSpec((1, 1, N_MOD, D_MODEL), lambda t: (layer, mod_row(t * tile), 0, 0)),
        pl.BlockSpec((1, 1, D_MODEL), lay),
        pl.BlockSpec((1, IN_COLS, D_MODEL), lay),
        pl.BlockSpec((1, 1, KV_RANK), lay),
        pl.BlockSpec((1, 1, 512), lay),
        pl.BlockSpec((512, 512), lambda t: (0, 0)),
        pl.BlockSpec((1, 256, 384), lay),
        pl.BlockSpec((1, KV_RANK, 384), lay),
    ]
    args = [x, mod, w["g1"], w["w_in"], w["kv_g"], w["qk_g"], w["seg"], w["w_ka"], w["w_uv"]]
    if resid is not None:
        in_specs += [pl.BlockSpec((tile, D_MODEL), row),
                     pl.BlockSpec((1, 1, N_MOD, D_MODEL), lambda t: (layer - 1, mod_row(t * tile), 0, 0))]
        args += [resid, mod]
    per_b = seq // tile
    in_specs += [pl.BlockSpec((tile, LANES), lambda t: (t % per_b, 0))] * 6
    args += list(rope_tabs)
    out_shape = [jax.ShapeDtypeStruct((n, Q_COLS), bf16), jax.ShapeDtypeStruct((n, KV_COLS), bf16)]
    out_specs = [pl.BlockSpec((tile, Q_COLS), row), pl.BlockSpec((tile, KV_COLS), row)]
    if resid is not None:
        out_shape.append(jax.ShapeDtypeStruct((n, D_MODEL), f32))
        out_specs.append(pl.BlockSpec((tile, D_MODEL), row))
    return pl.pallas_call(
        functools.partial(_pre_kernel, True, 0, resid is not None),
        out_shape=out_shape,
        grid=(n // tile,),
        in_specs=in_specs,
        out_specs=out_specs,
        scratch_shapes=[pltpu.VMEM((Z_COLS, D_MODEL), bf16)],
        compiler_params=_params(("arbitrary",)),
        name="pre_latent",
    )(*args)


PAST_CKV, PAST_KG, PAST_VG, PAST_KD, PAST_VD, PAST_KR, PAST_COLS = 0, 128, 256, 384, 640, 896, 928


def _cache_kernel(past_ref, wka_ref, wuv_ref, kv_ref):
    ckv_b = past_ref[0, 0, :, PAST_CKV:PAST_CKV + KV_RANK].astype(bf16)
    kr_b = past_ref[0, 0, :, PAST_KR:PAST_KR + MLA_ROPE].astype(bf16)
    wka = wka_ref[0]
    k_a = _dot(ckv_b, wka[:KV_RANK]) + _dot(kr_b, wka[KV_RANK:KV_RANK + MLA_ROPE])
    kv_ref[0, 0, :, KV_KA:KV_KA + 384] = k_a.astype(bf16)
    kv_ref[0, 0, :, KV_VA:KV_VA + 384] = _dot(ckv_b, wuv_ref[0].astype(bf16)).astype(bf16)
    kv_ref[0, 0, :, KV_KG:KV_KG + 128] = past_ref[0, 0, :, PAST_KG:PAST_KG + 128].astype(bf16)
    kv_ref[0, 0, :, KV_VG:KV_VG + 128] = past_ref[0, 0, :, PAST_VG:PAST_VG + 128].astype(bf16)
    kv_ref[0, 0, :, KV_KD:KV_KD + 256] = past_ref[0, 0, :, PAST_KD:PAST_KD + 256].astype(bf16)
    kv_ref[0, 0, :, KV_VD:KV_VD + 256] = past_ref[0, 0, :, PAST_VD:PAST_VD + 256].astype(bf16)


def _cache_rows(caches, w):
    ckv, kr, kg, vg, kd, vd = caches
    B = ckv.shape[0]
    flat = lambda a: a.reshape(B, DEPTH, PAST_LEN, -1)
    past = jnp.concatenate([flat(ckv), flat(kg), flat(vg), flat(kd), flat(vd), flat(kr)], axis=-1)
    return pl.pallas_call(
        _cache_kernel,
        out_shape=jax.ShapeDtypeStruct((DEPTH, B, PAST_LEN, KV_COLS), bf16),
        grid=(DEPTH, B),
        in_specs=[pl.BlockSpec((1, 1, PAST_LEN, PAST_COLS), lambda i, b: (b, i, 0, 0)),
                  pl.BlockSpec((1, 256, 384), lambda i, b: (i, 0, 0)), pl.BlockSpec((1, KV_RANK, 384), lambda i, b: (i, 0, 0))],
        out_specs=pl.BlockSpec((1, 1, PAST_LEN, KV_COLS), lambda i, b: (i, b, 0, 0)),
        compiler_params=_params(("arbitrary", "arbitrary")),
        name="cache_rows",
    )(past, w["w_ka"], w["w_uv"])


_SCORE_HEADS = (
    [(Q_A + 64 * h, KV_KA + 64 * h, 64, KV_VA + MLA_V * h) for h in range(MLA_HEADS)]
    + [(Q_G + 64 * h, KV_KG + 64 * (h // GQA_GROUP), 64, KV_VG + 64 * (h // GQA_GROUP)) for h in range(GQA_HEADS)]
    + [(Q_D + 64 * h + DIFF_QK * c, KV_KD + 64 * h + DIFF_QK * c, DIFF_QK, KV_VD + DIFF_V * h)
       for h in range(DIFF_HEADS) for c in range(2)])


def _attn_kernel(lam_init, per_round, n_src, q_ref, *refs):
    kv_refs = refs[:n_src]
    lam_ref, dg_ref, o_ref, s_ref, p_ref = refs[n_src:]
    spans, start = [], 0
    for r in kv_refs:
        spans.append((r, start, r.shape[2]))
        start += r.shape[2]
    mxu_sum = start >= MXU_SUM_MIN_KEYS

    outs = []
    for first in range(0, len(_SCORE_HEADS), per_round):
        chunk = _SCORE_HEADS[first:first + per_round]
        for j, (q_off, k_off, width, _) in enumerate(chunk):
            for r, lo, size in spans:
                s_ref[j, :, lo:lo + size] = _dot_nt(q_ref[:, q_off:q_off + width], r[0, 0, :, k_off:k_off + width])
        s = s_ref[...]
        p = jnp.exp2(s - jnp.max(s, axis=-1, keepdims=True))
        if mxu_sum:
            p_ref[...] = p.astype(bf16)
            for j, (_, _, _, v_off) in enumerate(chunk):
                o = sum(_dot(p_ref[j, :, lo:lo + size],
                             jnp.concatenate([r[0, 0, :, v_off:v_off + DIFF_V],
                                              jnp.ones((size, LANES - DIFF_V), bf16)], axis=1))
                        for r, lo, size in spans)
                outs.append((o * pltpu.roll(1.0 / o, DIFF_V, 1))[:, :DIFF_V])
        else:
            inv = 1.0 / jnp.sum(p, axis=-1, keepdims=True)
            p_ref[...] = p.astype(bf16)
            for j, (_, _, _, v_off) in enumerate(chunk):
                o = sum(_dot(p_ref[j, :, lo:lo + size], r[0, 0, :, v_off:v_off + DIFF_V]) for r, lo, size in spans)
                outs.append(o * inv[j])

    lp = lam_ref[0]
    e1 = jnp.exp(jnp.sum(lp[0:1] * lp[1:2], axis=-1, keepdims=True))
    e2 = jnp.exp(jnp.sum(lp[2:3] * lp[3:4], axis=-1, keepdims=True))
    lam = e1 - e2 + lam_init
    heads = outs[:MLA_HEADS + GQA_HEADS]
    for h in range(DIFF_HEADS):
        o1, o2 = outs[MLA_HEADS + GQA_HEADS + 2 * h:MLA_HEADS + GQA_HEADS + 2 * h + 2]
        heads.append(_rms(o1 - lam * o2, DIFF_V) * dg_ref[0] * (1.0 - lam_init))
    for j in range(len(heads) // 2):
        o_ref[:, LANES * j:LANES * (j + 1)] = jnp.concatenate(heads[2 * j:2 * j + 2], axis=1).astype(bf16)


def _attention(layer, q, sources, w, tile, per_round):
    n = q.shape[0]
    lam_init = 0.8 - 0.6 * math.exp(-0.3 * layer)
    s_kv = sum(rows for _, rows, _ in sources)
    assert len(_SCORE_HEADS) % per_round == 0
    return pl.pallas_call(
        functools.partial(_attn_kernel, lam_init, per_round, len(sources)),
        out_shape=jax.ShapeDtypeStruct((n, D_MODEL), bf16),
        scratch_shapes=[pltpu.VMEM((per_round, tile, s_kv), f32), pltpu.VMEM((per_round, tile, s_kv), bf16)],
        grid=(n // tile,),
        in_specs=[pl.BlockSpec((tile, Q_COLS), lambda t: (t, 0))]
        + [pl.BlockSpec((1, 1, rows, KV_COLS), index) for _, rows, index in sources]
        + [pl.BlockSpec((1, 4, DIFF_QK), lambda t: (layer, 0, 0)), pl.BlockSpec((1, 1, DIFF_V), lambda t: (layer, 0, 0))],
        out_specs=pl.BlockSpec((tile, D_MODEL), lambda t: (t, 0)),
        compiler_params=_params(("arbitrary",)),
        name="attention",
    )(q, *[arr for arr, _, _ in sources], w["lam"], w["diff_g"])


def _post_kernel(merge, *refs):
    it = iter(refs)
    o_ref, x_ref, mod_ref, w_out_ref, g2_ref, wg_ref, bg_ref, we_ref, be_ref = (next(it) for _ in range(9))
    if merge:
        next(it), next(it), next(it)
    x1_ref, hx_ref, cls_ref, w_scr = (next(it) for _ in range(4))

    @pl.when(pl.program_id(0) == 0)
    def _():
        w_scr[...] = w_out_ref[0].astype(bf16)

    gate1 = mod_ref[0, 0, 2:3, :]
    shift2 = mod_ref[0, 0, 3:4, :]
    scale2 = mod_ref[0, 0, 4:5, :]
    x1 = x_ref[...] + gate1 * _dot(o_ref[...], w_scr[...])
    x1_ref[...] = x1
    h2 = ((_rms(x1, D_MODEL) * g2_ref[0]) * (1.0 + scale2) + shift2).astype(bf16)
    bits = pltpu.bitcast(h2.astype(f32), i32)
    hx_ref[:, 0:HX_HALF] = bits[:, 0:HX_HALF] | jax.lax.shift_right_logical(bits[:, HX_HALF:D_MODEL], 16)

    def first_lane(mask, lane_f):
        return jnp.min(jnp.where(mask, lane_f, float(LANES)), axis=-1, keepdims=True)

    gl = _dot(h2, wg_ref[0].astype(bf16)) + bg_ref[0]
    glane = jax.lax.broadcasted_iota(i32, gl.shape, 1).astype(f32)
    ge = jnp.exp(gl - jnp.max(gl, axis=-1, keepdims=True))
    gprob = ge / jnp.sum(ge, axis=-1, keepdims=True)
    g_top = jnp.max(gprob, axis=-1, keepdims=True)
    g_idx = first_lane(gprob == g_top, glane)

    el = _dot(h2, we_ref[0].astype(bf16)) + be_ref[0]
    lane = jax.lax.broadcasted_iota(i32, el.shape, 1)
    lane_f = lane.astype(f32)
    emask = (lane >> 2).astype(f32) == g_idx
    em = jnp.where(emask, el, -jnp.inf)
    ee = jnp.where(emask, jnp.exp(em - jnp.max(em, axis=-1, keepdims=True)), 0.0)
    ep = ee / jnp.sum(ee, axis=-1, keepdims=True)
    p1 = jnp.max(jnp.where(emask, ep, -1.0), axis=-1, keepdims=True)
    i1 = first_lane(emask & (ep == p1), lane_f)
    rest = emask & (lane_f != i1)
    p2 = jnp.max(jnp.where(rest, ep, -1.0), axis=-1, keepdims=True)
    i2 = first_lane(rest & (ep == p2), lane_f)
    tot = p1 + p2
    w1 = g_top * (p1 / tot)
    w2 = g_top * (p2 / tot)

    lo = jnp.minimum(i1, i2) - EXPERTS_PER_GROUP * g_idx
    hi = jnp.maximum(i1, i2) - EXPERTS_PER_GROUP * g_idx
    pair = jnp.where(lo == 0.0, hi - 1.0, jnp.where(lo == 1.0, jnp.where(hi == 3.0, 3.0, 4.0), 5.0))
    cls_ref[...] = (N_PAIRS * g_idx + pair).astype(i32)
    g_lo = jnp.where(i1 < i2, w1, w2)
    g_hi = jnp.where(i1 < i2, w2, w1)
    tail_lane = jax.lax.broadcasted_iota(i32, (h2.shape[0], LANES), 1)
    hx_ref[:, HX_HALF:HX_COLS] = pltpu.bitcast(
        jnp.where(tail_lane == 0, g_lo, jnp.where(tail_lane == 1, g_hi, 0.0)), i32)


def _post(layer, o, x, row0, total, mod, mod_row, w, tile, merged=None):
    n = o.shape[0]
    lay = lambda t: (layer, 0, 0)
    row = lambda t: (t, 0)
    off = row0 // tile
    out_row = lambda t: (off + t, 0)
    in_specs = [
        pl.BlockSpec((tile, D_MODEL), row),
        pl.BlockSpec((tile, D_MODEL), row),
        pl.BlockSpec((1, 1, N_MOD, D_MODEL), lambda t: (layer, mod_row(t * tile), 0, 0)),
        pl.BlockSpec((1, D_MODEL, D_MODEL), lay),
        pl.BlockSpec((1, 1, D_MODEL), lay),
        pl.BlockSpec((1, D_MODEL, N_GROUPS), lay),
        pl.BlockSpec((1, 1, N_GROUPS), lay),
        pl.BlockSpec((1, D_MODEL, N_EXPERTS), lay),
        pl.BlockSpec((1, 1, N_EXPERTS), lay),
    ]
    args = [o, x, mod, w["w_out"], w["g2"], w["w_grp"], w["b_grp"], w["w_rtr"], w["b_rtr"]]
    aliases = {}
    if merged is not None:
        aliases = {len(args) + j: j for j in range(3)}
        in_specs += [pl.BlockSpec(memory_space=pl.ANY)] * 3
        args += list(merged)
    return pl.pallas_call(
        functools.partial(_post_kernel, merged is not None),
        out_shape=[jax.ShapeDtypeStruct((total, D_MODEL), f32), jax.ShapeDtypeStruct((total, HX_COLS), i32),
                   jax.ShapeDtypeStruct((total, 1), i32)],
        grid=(n // tile,),
        in_specs=in_specs,
        out_specs=[pl.BlockSpec((tile, D_MODEL), out_row), pl.BlockSpec((tile, HX_COLS), out_row),
                   pl.BlockSpec((tile, 1), out_row)],
        scratch_shapes=[pltpu.VMEM((D_MODEL, D_MODEL), bf16)],
        input_output_aliases=aliases,
        compiler_params=_params(("arbitrary",)),
        name="post_attention",
    )(*args)


def _ctx_kernel(lam_init, n_prev, resid, *refs):
    it = iter(refs)
    pre_in = [next(it) for _ in range(9 + (2 if resid else 0) + n_prev)]
    lam_ref, dg_ref = next(it), next(it)
    post_w = [next(it) for _ in range(6)]
    cache_refs = [next(it) for _ in range(len(CACHE_WIDTHS))]
    x1_ref, hx_ref, cls_ref = (next(it) for _ in range(3))
    w_in_scr, q_scr, kv_scr, o_scr, s_scr, p_scr, w_out_scr, x2_scr = (next(it) for _ in range(8))
    x_ref, mod_ref = pre_in[0], pre_in[1]
    x2_ref = [x2_scr] if resid else []

    _pre_kernel(False, n_prev, resid, *pre_in, q_scr, kv_scr.at[0, 0], *x2_ref, *cache_refs, w_in_scr)
    seq = s_scr.shape[1]
    for r in range(q_scr.shape[0] // seq):
        rows = pl.ds(r * seq, seq)
        _attn_kernel(lam_init, len(_SCORE_HEADS), 1, q_scr.at[rows], kv_scr.at[:, :, rows], lam_ref, dg_ref,
                     o_scr.at[rows], s_scr, p_scr)
    _post_kernel(False, o_scr, x2_scr if resid else x_ref, mod_ref, *post_w, x1_ref, hx_ref, cls_ref, w_out_scr)


def _ctx_layer(layer, x, n, seq, total, mod, w, prev_cache=(), resid=None):
    lay = lambda t: (layer, 0, 0)
    row = lambda t: (t, 0)
    reqs = CTX_REQUESTS_PER_STEP
    tile = reqs * seq
    mod_spec = lambda l: pl.BlockSpec((1, 1, N_MOD, D_MODEL), lambda t: (l, 0, 0, 0))
    in_specs = [
        pl.BlockSpec((tile, D_MODEL), row), mod_spec(layer),
        pl.BlockSpec((1, 1, D_MODEL), lay), pl.BlockSpec((1, IN_COLS, D_MODEL), lay),
        pl.BlockSpec((1, 1, KV_RANK), lay), pl.BlockSpec((1, 1, 512), lay), pl.BlockSpec((512, 512), lambda t: (0, 0)),
        pl.BlockSpec((1, 256, 384), lay), pl.BlockSpec((1, KV_RANK, 384), lay),
    ]
    args = [x, mod, w["g1"], w["w_in"], w["kv_g"], w["qk_g"], w["seg"], w["w_ka"], w["w_uv"]]
    if resid is not None:
        in_specs += [pl.BlockSpec((tile, D_MODEL), row), mod_spec(layer - 1)]
        args += [resid, mod]
    out_shape, out_specs, aliases = [], [], {}
    for j, width in enumerate(CACHE_WIDTHS):
        if prev_cache:
            aliases[len(args)] = len(out_shape)
            in_specs.append(pl.BlockSpec(memory_space=pl.ANY))
            args.append(prev_cache[j])
        shape = (MLA_ROPE, seq) if j == 1 else (seq, width)
        out_shape.append(jax.ShapeDtypeStruct((n // seq, DEPTH) + shape, f32))
        out_specs.append(pl.BlockSpec((reqs, 1) + shape, lambda t: (t, layer, 0, 0)))
    in_specs += [
        pl.BlockSpec((1, 4, DIFF_QK), lay), pl.BlockSpec((1, 1, DIFF_V), lay),
        pl.BlockSpec((1, D_MODEL, D_MODEL), lay), pl.BlockSpec((1, 1, D_MODEL), lay),
        pl.BlockSpec((1, D_MODEL, N_GROUPS), lay), pl.BlockSpec((1, 1, N_GROUPS), lay),
        pl.BlockSpec((1, D_MODEL, N_EXPERTS), lay), pl.BlockSpec((1, 1, N_EXPERTS), lay),
    ]
    args += [w["lam"], w["diff_g"], w["w_out"], w["g2"], w["w_grp"], w["b_grp"], w["w_rtr"], w["b_rtr"]]
    out_shape += [jax.ShapeDtypeStruct((total, D_MODEL), f32), jax.ShapeDtypeStruct((total, HX_COLS), i32),
                  jax.ShapeDtypeStruct((total, 1), i32)]
    out_specs += [pl.BlockSpec((tile, D_MODEL), row), pl.BlockSpec((tile, HX_COLS), row), pl.BlockSpec((tile, 1), row)]
    heads = len(_SCORE_HEADS)
    return pl.pallas_call(
        functools.partial(_ctx_kernel, 0.8 - 0.6 * math.exp(-0.3 * layer), len(prev_cache), resid is not None),
        out_shape=out_shape,
        grid=(n // tile,),
        in_specs=in_specs,
        out_specs=out_specs,
        scratch_shapes=[pltpu.VMEM((Z_COLS, D_MODEL), bf16), pltpu.VMEM((tile, Q_COLS), bf16),
                        pltpu.VMEM((1, 1, tile, KV_COLS), bf16), pltpu.VMEM((tile, D_MODEL), bf16),
                        pltpu.VMEM((heads, seq, seq), f32), pltpu.VMEM((heads, seq, seq), bf16),
                        pltpu.VMEM((D_MODEL, D_MODEL), bf16), pltpu.VMEM((tile, D_MODEL), f32)],
        input_output_aliases=aliases,
        compiler_params=_params(("arbitrary",)),
        name="context_layer",
    )(*args)


PLAN_CHUNK = 1024
TAB_ROWS = LANES


def _plan_kernel(cls_ref, slot_ref, tab_ref, rank_scr):
    n = cls_ref.shape[0]
    lane = jax.lax.broadcasted_iota(i32, (PLAN_CHUNK, LANES), 1)
    r = jax.lax.broadcasted_iota(i32, (PLAN_CHUNK, PLAN_CHUNK), 0)
    c = jax.lax.broadcasted_iota(i32, (PLAN_CHUNK, PLAN_CHUNK), 1)
    before = (c < r).astype(bf16)

    def count(b, seen):
        rows = pl.ds(pl.multiple_of(b * PLAN_CHUNK, PLAN_CHUNK), PLAN_CHUNK)
        onehot = (cls_ref[rows, :] == lane).astype(f32)
        ahead = _dot(before, onehot.astype(bf16)) + seen
        rank_scr[rows, :] = jnp.sum(onehot * ahead, axis=-1, keepdims=True)
        return seen + jnp.sum(onehot, axis=0, keepdims=True)

    counts = jax.lax.fori_loop(0, n // PLAN_CHUNK, count, jnp.zeros((1, LANES), f32))
    tiles = jnp.floor((counts + (EXPERT_TILE - 1)) * (1.0 / EXPERT_TILE))
    rr = jax.lax.broadcasted_iota(i32, (LANES, LANES), 0)
    cc = jax.lax.broadcasted_iota(i32, (LANES, LANES), 1)
    ends = _dot(jnp.broadcast_to(tiles, (8, LANES)).astype(bf16), (rr <= cc).astype(bf16))[0:1]
    starts = ends - tiles

    def place(b, carry):
        rows = pl.ds(pl.multiple_of(b * PLAN_CHUNK, PLAN_CHUNK), PLAN_CHUNK)
        first = jnp.sum(jnp.where(cls_ref[rows, :] == lane, starts, 0.0), axis=-1, keepdims=True)
        slot_ref[rows, :] = (first * EXPERT_TILE + rank_scr[rows, :]).astype(i32)
        return carry

    jax.lax.fori_loop(0, n // PLAN_CHUNK, place, 0)

    tl = jax.lax.broadcasted_iota(i32, (TAB_ROWS, LANES), 1)
    n_tiles = jnp.sum(jnp.where(tl[0:1] == N_CLASSES - 1, ends, 0.0), axis=-1, keepdims=True)
    k = jnp.minimum(jax.lax.broadcasted_iota(i32, (TAB_ROWS, 1), 0).astype(f32), n_tiles - 1.0)
    cls_k = jnp.sum(jnp.where((tl < N_CLASSES) & (ends <= k), 1.0, 0.0), axis=-1, keepdims=True)
    cls_k = jnp.minimum(cls_k, N_CLASSES - 1.0)
    mine = tl.astype(f32) == cls_k
    used = jnp.sum(jnp.where(mine, counts, 0.0), axis=-1, keepdims=True)
    first = jnp.sum(jnp.where(mine, starts, 0.0), axis=-1, keepdims=True)
    valid = jnp.clip(used - (k - first) * EXPERT_TILE, 0.0, float(EXPERT_TILE))
    group = jnp.floor((cls_k + 0.5) * (1.0 / N_PAIRS))
    pair = cls_k - N_PAIRS * group
    lo = hi = jnp.zeros_like(pair)
    for p in range(N_PAIRS):
        lo = jnp.where(pair == p, float(PAIR_LO[p]), lo)
        hi = jnp.where(pair == p, float(PAIR_HI[p]), hi)
    e_lo = EXPERTS_PER_GROUP * group + lo
    e_hi = EXPERTS_PER_GROUP * group + hi
    tab = jnp.where(tl == 0, e_lo, jnp.where(tl == 1, e_hi, jnp.where(tl == 2, valid, jnp.where(tl == 3, n_tiles, 0.0))))
    tab_ref[...] = tab.astype(i32)


def _plan(cls, max_tiles):
    n = cls.shape[0]
    assert n % PLAN_CHUNK == 0 and max_tiles <= TAB_ROWS
    slot, tab = pl.pallas_call(
        _plan_kernel,
        out_shape=[jax.ShapeDtypeStruct((n, 1), i32), jax.ShapeDtypeStruct((TAB_ROWS, LANES), i32)],
        scratch_shapes=[pltpu.VMEM((n, 1), f32)],
        compiler_params=_params(None),
        name="dispatch_plan",
    )(cls)
    return slot.reshape(n), tab[:max_tiles, 0], tab[:max_tiles, 1], tab[:max_tiles, 2], tab[0, 3:4]


def _move_rows(src, idx, n_out, scatter):
    n = idx.shape[0]
    width = src.shape[1]
    per_worker = n // (SC_CORES * SC_SUBCORES)
    assert n % SC_ROWS == 0
    chunk = max(c for c in (64, 40, 32, 16, 8)
                if per_worker % c == 0 and 2 * c * width * src.dtype.itemsize <= SC_BUFFER_BYTES)
    n_chunks = per_worker // chunk
    mesh = plsc.VectorSubcoreMesh(core_axis_name="c", subcore_axis_name="s")

    @functools.partial(
        pl.kernel, mesh=mesh, out_type=jax.ShapeDtypeStruct((n_out, width), src.dtype),
        scratch_types=[pltpu.VMEM((chunk,), i32), pltpu.VMEM((chunk,), i32),
                       pltpu.VMEM((chunk, width), src.dtype), pltpu.VMEM((chunk, width), src.dtype),
                       pltpu.SemaphoreType.DMA, pltpu.SemaphoreType.DMA, pltpu.SemaphoreType.DMA,
                       pltpu.SemaphoreType.DMA])
    def move(src_hbm, idx_hbm, out_hbm, idx0, idx1, rows0, rows1, in0, in1, out0, out1):
        wid = jax.lax.axis_index("s") * SC_CORES + jax.lax.axis_index("c")
        base = wid * per_worker
        idx_v, rows_v, sem_in, sem_out = (idx0, idx1), (rows0, rows1), (in0, in1), (out0, out1)

        def fill(j):
            b = j % 2
            rows = pl.ds(base + j * chunk, chunk)
            pltpu.sync_copy(idx_hbm.at[rows], idx_v[b])
            src_rows = src_hbm.at[rows] if scatter else src_hbm.at[idx_v[b]]
            return pltpu.async_copy(src_rows, rows_v[b], sem_in[b])

        def drain(j):
            b = j % 2
            dst_rows = out_hbm.at[idx_v[b]] if scatter else out_hbm.at[pl.ds(base + j * chunk, chunk)]
            return pltpu.async_copy(rows_v[b], dst_rows, sem_out[b])

        fills, drains = {0: fill(0)}, {}
        for j in range(n_chunks):
            if j + 1 < n_chunks:
                if j >= 1:
                    drains[j - 1].wait()
                fills[j + 1] = fill(j + 1)
            fills[j].wait()
            drains[j] = drain(j)
        for j in range(max(n_chunks - 2, 0), n_chunks):
            drains[j].wait()

    return move(src, idx)


def _expert_kernel(lo_ref, hi_ref, valid_ref, nt_ref, xs_ref, wg_ref, wu_ref, wd_ref, ys_ref, wg_scr, wu_scr, wd_scr):
    k = pl.program_id(0)
    prev = jnp.maximum(k - 1, 0)

    @pl.when((k == 0) | (lo_ref[k] // EXPERTS_PER_GROUP != lo_ref[prev] // EXPERTS_PER_GROUP))
    def _():
        for j in range(EXPERTS_PER_GROUP):
            wg_scr[j] = wg_ref[0, 0, j].astype(bf16)
            wu_scr[j] = wu_ref[0, 0, j].astype(bf16)
            wd_scr[j] = wd_ref[0, 0, j].astype(bf16)

    @pl.when(k < nt_ref[0])
    def _():
        live = jax.lax.broadcasted_iota(i32, (EXPERT_TILE, 1), 0) < valid_ref[k]
        words = jnp.where(live, xs_ref[:, 0:HX_HALF], 0)
        x = jnp.concatenate([pltpu.bitcast(words & -65536, f32), pltpu.bitcast(words << 16, f32)], axis=1).astype(bf16)
        gates = pltpu.bitcast(jnp.where(live, xs_ref[:, HX_HALF:HX_COLS], 0), f32)
        y = None
        for lane, e_ref in enumerate((lo_ref, hi_ref)):
            j = e_ref[k] % EXPERTS_PER_GROUP
            hid = _silu(_dot(x, wg_scr[j])) * _dot(x, wu_scr[j]) * gates[:, lane:lane + 1]
            part = _dot(hid.astype(bf16), wd_scr[j])
            y = part if y is None else y + part
        ys_ref[...] = y


def _experts(layer, xs, e_lo, e_hi, valid, n_tiles, w):
    max_tiles = e_lo.shape[0]
    row = lambda k, lo, hi, valid, nt: (jnp.minimum(k, nt[0] - 1), 0)
    group = lambda k, lo, hi, valid, nt: (layer, lo[k] // EXPERTS_PER_GROUP, 0, 0, 0)
    by_group = lambda a: a.reshape(DEPTH, N_GROUPS, EXPERTS_PER_GROUP, *a.shape[2:])
    return pl.pallas_call(
        _expert_kernel,
        out_shape=jax.ShapeDtypeStruct((max_tiles * EXPERT_TILE, D_MODEL), f32),
        grid_spec=pltpu.PrefetchScalarGridSpec(
            num_scalar_prefetch=4, grid=(max_tiles,),
            in_specs=[pl.BlockSpec((EXPERT_TILE, HX_COLS), row),
                      pl.BlockSpec((1, 1, EXPERTS_PER_GROUP, D_MODEL, D_FF_EXPERT), group),
                      pl.BlockSpec((1, 1, EXPERTS_PER_GROUP, D_MODEL, D_FF_EXPERT), group),
                      pl.BlockSpec((1, 1, EXPERTS_PER_GROUP, D_FF_EXPERT, D_MODEL), group)],
            out_specs=pl.BlockSpec((EXPERT_TILE, D_MODEL), row),
            scratch_shapes=[pltpu.VMEM((EXPERTS_PER_GROUP, D_MODEL, D_FF_EXPERT), bf16),
                            pltpu.VMEM((EXPERTS_PER_GROUP, D_MODEL, D_FF_EXPERT), bf16),
                            pltpu.VMEM((EXPERTS_PER_GROUP, D_FF_EXPERT, D_MODEL), bf16)]),
        compiler_params=_params(("arbitrary",)),
        name="experts",
    )(e_lo, e_hi, valid, n_tiles, xs, by_group(w["w_gate"]), by_group(w["w_up"]), by_group(w["w_down"]))


def _final_kernel(x1_ref, y_ref, mod_ref, fg_ref, o_ref):
    o_ref[...] = _rms(x1_ref[...] + mod_ref[0, 0, 5:6, :] * y_ref[...], D_MODEL) * fg_ref[...]


def _final(x1, y, n, row0, mod, mod_row, w, tile):
    off = row0 // tile
    src_row = lambda t: (off + t, 0)
    return pl.pallas_call(
        _final_kernel,
        out_shape=jax.ShapeDtypeStruct((n, D_MODEL), f32),
        grid=(n // tile,),
        in_specs=[pl.BlockSpec((tile, D_MODEL), src_row), pl.BlockSpec((tile, D_MODEL), lambda t: (t, 0)),
                  pl.BlockSpec((1, 1, N_MOD, D_MODEL), lambda t: (DEPTH - 1, mod_row(t * tile), 0, 0)),
                  pl.BlockSpec((1, D_MODEL), lambda t: (0, 0))],
        out_specs=pl.BlockSpec((tile, D_MODEL), lambda t: (t, 0)),
        compiler_params=_params(("arbitrary",)),
        name="final_norm",
    )(x1, y, mod, w["final_g"])


def _rope_tables(n_tokens):
    pos = np.arange(n_tokens)
    row = (pos // GRID_W).astype(np.float64)
    col = (pos % GRID_W).astype(np.float64)

    def cs(rot_dim):
        quarter = rot_dim // 4
        inv = ROPE_THETA ** (-np.arange(quarter, dtype=np.float64) / quarter)
        ang = np.concatenate([row[:, None] * inv, col[:, None] * inv], axis=-1)
        return np.cos(ang), np.sin(ang)

    c32, s32 = cs(MLA_ROPE)
    c64, s64 = cs(HEAD_DIM)
    ones = np.ones((n_tokens, MLA_NOPE))
    zeros = np.zeros((n_tokens, MLA_NOPE))

    def rep(parts):
        period = np.concatenate(parts, axis=-1)
        return jnp.asarray(np.tile(period, (1, LANES // period.shape[-1])), f32)

    return (rep([ones, c32, c32]), rep([zeros, -s32, s32]), rep([c32, c32]), rep([-s32, s32]),
            rep([c64, c64]), rep([-s64, s64]))


def _layout_weights(norm1_g, norm2_g, w_in, mla_kv_norm_g, mla_w_uk, mla_w_uv, gqa_q_norm_g, gqa_k_norm_g,
                    diff_lambda, diff_norm_g, w_out, moe_w_group, moe_b_group, moe_w_router, moe_b_router,
                    moe_w_gate, moe_w_up, moe_w_down, final_norm_g):
    eye = jnp.eye(MLA_ROPE, dtype=f32)
    top = jnp.concatenate([mla_w_uk, jnp.zeros((DEPTH, KV_RANK, MLA_HEADS, MLA_ROPE), f32)], axis=-1)
    mid = jnp.concatenate([jnp.zeros((MLA_ROPE, MLA_HEADS, MLA_NOPE), f32),
                           jnp.broadcast_to(eye[:, None, :], (MLA_ROPE, MLA_HEADS, MLA_ROPE))], axis=-1)
    w_ka = jnp.concatenate([top.reshape(DEPTH, KV_RANK, 384),
                            jnp.broadcast_to(mid.reshape(1, MLA_ROPE, 384), (DEPTH, MLA_ROPE, 384)),
                            jnp.zeros((DEPTH, 256 - KV_RANK - MLA_ROPE, 384), f32)], axis=1).astype(bf16)
    seg_id = np.arange(512) // HEAD_DIM
    seg = jnp.asarray(seg_id[:, None] == seg_id[None, :], bf16)
    qk_g = jnp.concatenate([jnp.tile(gqa_q_norm_g, (1, GQA_HEADS)), jnp.tile(gqa_k_norm_g, (1, GQA_KV_HEADS))], axis=-1)
    return dict(
        g1=norm1_g.reshape(DEPTH, 1, D_MODEL), g2=norm2_g.reshape(DEPTH, 1, D_MODEL),
        w_in=jnp.swapaxes(w_in, 1, 2),
        kv_g=mla_kv_norm_g.reshape(DEPTH, 1, KV_RANK), qk_g=qk_g.reshape(DEPTH, 1, 512), seg=seg, w_ka=w_ka,
        w_uv=mla_w_uv.reshape(DEPTH, KV_RANK, 384), lam=diff_lambda, diff_g=diff_norm_g.reshape(DEPTH, 1, DIFF_V),
        w_out=w_out, w_grp=moe_w_group, b_grp=moe_b_group.reshape(DEPTH, 1, N_GROUPS), w_rtr=moe_w_router,
        b_rtr=moe_b_router.reshape(DEPTH, 1, N_EXPERTS), w_gate=moe_w_gate, w_up=moe_w_up, w_down=moe_w_down,
        final_g=final_norm_g.reshape(1, D_MODEL))


PRE_TILE = 512
CTX_REQUESTS_PER_STEP = 2
LAT_ATTN_TILE = 256
LAT_HEADS_PER_ROUND = 10
MXU_SUM_MIN_KEYS = 1024
POST_TILE = 512
FINAL_TILE = 1024


def kernel(x_prompt, x_sample, c, cache_mla_ckv, cache_mla_krope, cache_gqa_k, cache_gqa_v, cache_diff_k, cache_diff_v, c_ctx, norm1_g, norm2_g, w_mod, b_mod, w_in, mla_kv_norm_g, mla_w_uk, mla_w_uv, gqa_q_norm_g, gqa_k_norm_g, diff_lambda, diff_norm_g, w_out, moe_w_group, moe_b_group, moe_w_router, moe_b_router, moe_w_gate, moe_w_up, moe_w_down, final_norm_g):
    B, S, _ = x_prompt.shape
    Bl, Sl, _ = x_sample.shape
    n_ctx, n_lat = B * S, Bl * Sl
    total = n_ctx + n_lat
    assert Bl + 1 <= MOD_ROWS and DEPTH == 2 and total % SC_ROWS == 0
    slot_rows = -(-(total + N_CLASSES * EXPERT_TILE) // SC_ROWS) * SC_ROWS
    max_tiles = slot_rows // EXPERT_TILE
    w = _layout_weights(norm1_g, norm2_g, w_in, mla_kv_norm_g, mla_w_uk, mla_w_uv, gqa_q_norm_g, gqa_k_norm_g,
                        diff_lambda, diff_norm_g, w_out, moe_w_group, moe_b_group, moe_w_router, moe_b_router,
                        moe_w_gate, moe_w_up, moe_w_down, final_norm_g)
    cond = jnp.concatenate([c_ctx[None, :], c, jnp.zeros((MOD_ROWS - 1 - Bl, D_MODEL), f32)], axis=0)
    mod = _modulation(cond, w_mod, b_mod).reshape(DEPTH, MOD_ROWS, N_MOD, D_MODEL)
    ctx_row = lambda token: 0
    lat_row = lambda token: 1 + token // Sl
    tabs = _rope_tables(Sl)
    kv_past = _cache_rows((cache_mla_ckv, cache_mla_krope, cache_gqa_k, cache_gqa_v, cache_diff_k, cache_diff_v), w)
    per_b = Sl // LAT_ATTN_TILE

    x_ctx, x_lat = x_prompt.reshape(n_ctx, D_MODEL), x_sample.reshape(n_lat, D_MODEL)
    cache = ()
    x1 = y_c = y_l = None
    for i in range(DEPTH):
        if i == 0:
            q_l, kv_l = _pre_latent(i, x_lat, n_lat, 0, Sl, mod, lat_row, w, PRE_TILE, tabs)
        else:
            q_l, kv_l, x_lat = _pre_latent(i, x1, n_lat, n_ctx, Sl, mod, lat_row, w, PRE_TILE, tabs, resid=y_l)
        past = (kv_past, PAST_LEN, lambda t, i=i: (i, t // per_b, 0, 0))
        own = (kv_l.reshape(1, Bl, Sl, KV_COLS), Sl, lambda t: (0, t // per_b, 0, 0))
        o_l = _attention(i, q_l, [past, own], w, LAT_ATTN_TILE, LAT_HEADS_PER_ROUND)
        if i == 0:
            *cache, x1_c, hx_c, cls_c = _ctx_layer(i, x_ctx, n_ctx, S, total, mod, w)
        else:
            y_c, o_l = jax.lax.optimization_barrier((y_c, o_l))
            *cache, x1_c, hx_c, cls_c = _ctx_layer(i, x1, n_ctx, S, total, mod, w, prev_cache=cache, resid=y_c)
        x1, hx, cls = _post(i, o_l, x_lat, n_ctx, total, mod, lat_row, w, POST_TILE, merged=(x1_c, hx_c, cls_c))
        if i == DEPTH - 1:
            outs = (cache[0], jnp.swapaxes(cache[1], 2, 3), cache[2].reshape(B, DEPTH, S, GQA_KV_HEADS, HEAD_DIM),
                    cache[3].reshape(B, DEPTH, S, GQA_KV_HEADS, HEAD_DIM),
                    cache[4].reshape(B, DEPTH, S, DIFF_HEADS, 2, DIFF_QK),
                    cache[5].reshape(B, DEPTH, S, DIFF_HEADS, DIFF_V))
            cls, outs = jax.lax.optimization_barrier((cls, outs))
        slot, e_lo, e_hi, valid, n_tiles = _plan(cls, max_tiles)
        xs = _move_rows(hx, slot, slot_rows, scatter=True)
        ys = _experts(i, xs, e_lo, e_hi, valid, n_tiles, w)
        y_l = _move_rows(ys, slot[n_ctx:], n_lat, scatter=False)
        y_c = _move_rows(ys, slot[:n_ctx], n_ctx, scatter=False)

    y_sample = _final(x1, y_l, n_lat, n_ctx, mod, lat_row, w, FINAL_TILE).reshape(Bl, Sl, D_MODEL)
    y_prompt = _final(x1, y_c, n_ctx, 0, mod, ctx_row, w, FINAL_TILE).reshape(B, S, D_MODEL)
    return (y_prompt, y_sample, *outs)
```

```python
import functools
import math

import jax
import jax.numpy as jnp
import numpy as np
from jax.experimental import pallas as pl
from jax.experimental.pallas import tpu as pltpu
from jax.experimental.pallas import tpu_sc as plsc

D_MODEL = 1024
DEPTH = 2
PAST_LEN = 512
GRID_W = 64
ROPE_THETA = 10000.0
EPS = 1e-6
LOG2E = 1.4426950408889634
N_MOD = 6
HEAD_DIM = 64
MLA_HEADS = 6
MLA_NOPE = 32
MLA_ROPE = 32
MLA_V = 64
KV_RANK = 128
GQA_HEADS = 6
GQA_KV_HEADS = 2
GQA_GROUP = GQA_HEADS // GQA_KV_HEADS
DIFF_HEADS = 4
DIFF_QK = 32
DIFF_V = 64
N_GROUPS = 4
EXPERTS_PER_GROUP = 4
N_EXPERTS = N_GROUPS * EXPERTS_PER_GROUP
D_FF_EXPERT = 256

LANES = 128
MOD_ROWS = 8

IN_COLS = 1952
IN_KR = 512
Z_QA, Z_CKV, Z_QG, Z_KG, Z_VG, Z_QD, Z_KD, Z_VD, Z_KR = 0, 384, 512, 896, 1024, 1152, 1408, 1664, 1920
Z_COLS = 2048
Q_A, Q_G, Q_D, Q_COLS = 0, 384, 768, 1024
KV_KA, KV_VA, KV_KG, KV_VG, KV_KD, KV_VD, KV_COLS = 0, 384, 768, 896, 1024, 1280, 1536
CACHE_WIDTHS = (128, 32, 128, 128, 256, 256)

PAIR_LO = (0, 0, 0, 1, 1, 2)
PAIR_HI = (1, 2, 3, 3, 2, 3)
N_PAIRS = len(PAIR_LO)
N_CLASSES = N_GROUPS * N_PAIRS
HX_HALF = D_MODEL // 2
HX_COLS = HX_HALF + LANES
EXPERT_TILE = 256

SC_CORES, SC_SUBCORES = 2, 16
SC_BUFFER_BYTES = 400 * 1024
SC_ROWS = SC_CORES * SC_SUBCORES * 8

VMEM_LIMIT = 56 * 1024 * 1024

bf16 = jnp.bfloat16
f32 = jnp.float32
i32 = jnp.int32


def _dot(a, b):
    return jnp.dot(a, b, preferred_element_type=f32)


def _dot_nt(a, b):
    return jax.lax.dot_general(a, b, (((1,), (1,)), ((), ())), preferred_element_type=f32)


def _rms(x, width):
    return x * jax.lax.rsqrt(jnp.sum(x * x, axis=-1, keepdims=True) * (1.0 / width) + EPS)


def _silu(x):
    return x * (1.0 / (1.0 + jnp.exp(-x)))


def _params(sem):
    return pltpu.CompilerParams(dimension_semantics=sem, vmem_limit_bytes=VMEM_LIMIT)


def _mod_kernel(cond_ref, w_ref, b_ref, o_ref):
    o_ref[0] = _dot(_silu(cond_ref[...]).astype(bf16), w_ref[0].astype(bf16)) + b_ref[0]


def _modulation(cond, w_mod, b_mod):
    return pl.pallas_call(
        _mod_kernel,
        out_shape=jax.ShapeDtypeStruct((DEPTH, MOD_ROWS, N_MOD * D_MODEL), f32),
        grid=(DEPTH, N_MOD),
        in_specs=[
            pl.BlockSpec((MOD_ROWS, D_MODEL), lambda i, j: (0, 0)),
            pl.BlockSpec((1, D_MODEL, D_MODEL), lambda i, j: (i, 0, j)),
            pl.BlockSpec((1, 1, D_MODEL), lambda i, j: (i, 0, j)),
        ],
        out_specs=pl.BlockSpec((1, MOD_ROWS, D_MODEL), lambda i, j: (i, 0, j)),
        compiler_params=_params(("arbitrary", "arbitrary")),
        name="modulation",
    )(cond, w_mod, b_mod.reshape(DEPTH, 1, N_MOD * D_MODEL))


def _swap_halves(x, half):
    lane = jax.lax.broadcasted_iota(i32, x.shape, 1)
    fwd = pltpu.roll(x, LANES - half, 1)
    bwd = pltpu.roll(x, half, 1)
    return jnp.where((lane & (2 * half - 1)) < half, fwd, bwd)


def _rope_block(x, cos, sin, half):
    return x * cos + _swap_halves(x, half) * sin


def _pre_kernel(rope, n_prev, resid, *refs):
    it = iter(refs)
    x_ref, mod_ref, g1_ref, w_in_ref, kvg_ref, qkg_ref, seg_ref, wka_ref, wuv_ref = (next(it) for _ in range(9))
    if resid:
        y_ref, pmod_ref = next(it), next(it)
    if rope:
        ca_ref, sa_ref, c32_ref, s32_ref, c64_ref, s64_ref = (next(it) for _ in range(6))
    prev_refs = [next(it) for _ in range(n_prev)]
    q_ref, kv_ref = next(it), next(it)
    if resid:
        x2_ref = next(it)
    cache_refs = [] if rope else [next(it) for _ in range(len(CACHE_WIDTHS))]
    w_scr = next(it)

    @pl.when(pl.program_id(0) == 0)
    def _():
        w_scr[0:IN_KR] = w_in_ref[0, 0:IN_KR].astype(bf16)
        w_scr[IN_KR:Z_KR] = w_in_ref[0, IN_KR + MLA_ROPE:IN_COLS].astype(bf16)
        w_scr[Z_KR:Z_KR + MLA_ROPE] = w_in_ref[0, IN_KR:IN_KR + MLA_ROPE].astype(bf16)
        w_scr[Z_KR + MLA_ROPE:Z_COLS] = jnp.zeros((Z_COLS - Z_KR - MLA_ROPE, D_MODEL), bf16)

    x = x_ref[...]
    if resid:
        x = x + pmod_ref[0, 0, 5:6, :] * y_ref[...]
        x2_ref[...] = x
    shift1 = mod_ref[0, 0, 0:1, :]
    scale1 = mod_ref[0, 0, 1:2, :]
    h = (_rms(x, D_MODEL) * g1_ref[0]) * (1.0 + scale1) + shift1
    z = _dot_nt(h.astype(bf16), w_scr[...])

    ckv = _rms(z[:, Z_CKV:Z_CKV + KV_RANK], KV_RANK) * kvg_ref[0]

    qk = z[:, Z_QG:Z_VG]
    sq = qk * qk
    sq_hi = sq.astype(bf16)
    sq_lo = (sq - sq_hi.astype(f32)).astype(bf16)
    seg = seg_ref[...]
    ms = (_dot(sq_hi, seg) + _dot(sq_lo, seg)) * (1.0 / HEAD_DIM)
    qk = qk * jax.lax.rsqrt(ms + EPS) * qkg_ref[0]

    def blocks(arr, n):
        return [arr[:, LANES * j:LANES * (j + 1)] for j in range(n)]

    qa = blocks(z[:, Z_QA:Z_QA + 384], 3)
    qkb = blocks(qk, 4)
    qd = blocks(z[:, Z_QD:Z_QD + 256], 2)
    kd = blocks(z[:, Z_KD:Z_KD + 256], 2)
    kr = z[:, Z_KR:Z_KR + LANES]
    if rope:
        ca, sa, c32, s32, c64, s64 = (r[...] for r in (ca_ref, sa_ref, c32_ref, s32_ref, c64_ref, s64_ref))
        qa = [_rope_block(b, ca, sa, MLA_ROPE // 2) for b in qa]
        qkb = [_rope_block(b, c64, s64, HEAD_DIM // 2) for b in qkb]
        qd = [_rope_block(b, c32, s32, DIFF_QK // 2) for b in qd]
        kd = [_rope_block(b, c32, s32, DIFF_QK // 2) for b in kd]
        kr = _rope_block(kr, c32, s32, MLA_ROPE // 2)

    vg = z[:, Z_VG:Z_VG + 128]
    vd = z[:, Z_VD:Z_VD + 256]
    ckv_b = ckv.astype(bf16)
    k_a = _dot(jnp.concatenate([ckv_b, kr.astype(bf16)], axis=1), wka_ref[0])
    v_a = _dot(ckv_b, wuv_ref[0].astype(bf16))

    for j in range(3):
        q_ref[:, Q_A + LANES * j:Q_A + LANES * (j + 1)] = (qa[j] * (HEAD_DIM ** -0.5 * LOG2E)).astype(bf16)
        q_ref[:, Q_G + LANES * j:Q_G + LANES * (j + 1)] = (qkb[j] * (HEAD_DIM ** -0.5 * LOG2E)).astype(bf16)
    for j in range(2):
        q_ref[:, Q_D + LANES * j:Q_D + LANES * (j + 1)] = (qd[j] * (DIFF_QK ** -0.5 * LOG2E)).astype(bf16)
        kv_ref[:, KV_KD + LANES * j:KV_KD + LANES * (j + 1)] = kd[j].astype(bf16)
    kv_ref[:, KV_KA:KV_KA + 384] = k_a.astype(bf16)
    kv_ref[:, KV_VA:KV_VA + 384] = v_a.astype(bf16)
    kv_ref[:, KV_KG:KV_KG + 128] = qkb[3].astype(bf16)
    kv_ref[:, KV_VG:KV_VG + 128] = vg.astype(bf16)
    kv_ref[:, KV_VD:KV_VD + 256] = vd.astype(bf16)
    if not rope:
        rows = [ckv, None, qkb[3], vg, jnp.concatenate(kd, axis=1), vd]
        for out, new in zip(cache_refs, rows):
            if new is None:
                seq = out.shape[3]
                for r in range(out.shape[0]):
                    out[r, 0] = kr[r * seq:(r + 1) * seq].T[:MLA_ROPE, :]
            else:
                reqs, _, seq, width = out.shape
                out[:, 0] = new.reshape(reqs, seq, width)


def _pre_latent(layer, x, n, row0, seq, mod, mod_row, w, tile, rope_tabs, resid=None):
    lay = lambda t: (layer, 0, 0)
    row = lambda t: (t, 0)
    off = row0 // tile
    src_row = lambda t: (off + t, 0)
    in_specs = [
        pl.BlockSpec((tile, D_MODEL), src_row),
        pl.BlockSpec((1, 1, N_MOD, D_MODEL), lambda t: (layer, mod_row(t * tile), 0, 0)),
        pl.BlockSpec((1, 1, D_MODEL), lay),
        pl.BlockSpec((1, IN_COLS, D_MODEL), lay),
        pl.BlockSpec((1, 1, KV_RANK), lay),
        pl.BlockSpec((1, 1, 512), lay),
        pl.BlockSpec((512, 512), lambda t: (0, 0)),
        pl.BlockSpec((1, 256, 384), lay),
        pl.BlockSpec((1, KV_RANK, 384), lay),
    ]
    args = [x, mod, w["g1"], w["w_in"], w["kv_g"], w["qk_g"], w["seg"], w["w_ka"], w["w_uv"]]
    if resid is not None:
        in_specs += [pl.BlockSpec((tile, D_MODEL), row),
                     pl.BlockSpec((1, 1, N_MOD, D_MODEL), lambda t: (layer - 1, mod_row(t * tile), 0, 0))]
        args += [resid, mod]
    per_b = seq // tile
    in_specs += [pl.BlockSpec((tile, LANES), lambda t: (t % per_b, 0))] * 6
    args += list(rope_tabs)
    out_shape = [jax.ShapeDtypeStruct((n, Q_COLS), bf16), jax.ShapeDtypeStruct((n, KV_COLS), bf16)]
    out_specs = [pl.BlockSpec((tile, Q_COLS), row), pl.BlockSpec((tile, KV_COLS), row)]
    if resid is not None:
        out_shape.append(jax.ShapeDtypeStruct((n, D_MODEL), f32))
        out_specs.append(pl.BlockSpec((tile, D_MODEL), row))
    return pl.pallas_call(
        functools.partial(_pre_kernel, True, 0, resid is not None),
        out_shape=out_shape,
        grid=(n // tile,),
        in_specs=in_specs,
        out_specs=out_specs,
        scratch_shapes=[pltpu.VMEM((Z_COLS, D_MODEL), bf16)],
        compiler_params=_params(("arbitrary",)),
        name="pre_latent",
    )(*args)


PAST_CKV, PAST_KG, PAST_VG, PAST_KD, PAST_VD, PAST_KR, PAST_COLS = 0, 128, 256, 384, 640, 896, 928


def _cache_kernel(past_ref, wka_ref, wuv_ref, kv_ref):
    ckv_b = past_ref[0, 0, :, PAST_CKV:PAST_CKV + KV_RANK].astype(bf16)
    kr_b = past_ref[0, 0, :, PAST_KR:PAST_KR + MLA_ROPE].astype(bf16)
    wka = wka_ref[0]
    k_a = _dot(ckv_b, wka[:KV_RANK]) + _dot(kr_b, wka[KV_RANK:KV_RANK + MLA_ROPE])
    kv_ref[0, 0, :, KV_KA:KV_KA + 384] = k_a.astype(bf16)
    kv_ref[0, 0, :, KV_VA:KV_VA + 384] = _dot(ckv_b, wuv_ref[0].astype(bf16)).astype(bf16)
    kv_ref[0, 0, :, KV_KG:KV_KG + 128] = past_ref[0, 0, :, PAST_KG:PAST_KG + 128].astype(bf16)
    kv_ref[0, 0, :, KV_VG:KV_VG + 128] = past_ref[0, 0, :, PAST_VG:PAST_VG + 128].astype(bf16)
    kv_ref[0, 0, :, KV_KD:KV_KD + 256] = past_ref[0, 0, :, PAST_KD:PAST_KD + 256].astype(bf16)
    kv_ref[0, 0, :, KV_VD:KV_VD + 256] = past_ref[0, 0, :, PAST_VD:PAST_VD + 256].astype(bf16)


def _cache_rows(caches, w):
    ckv, kr, kg, vg, kd, vd = caches
    B = ckv.shape[0]
    flat = lambda a: a.reshape(B, DEPTH, PAST_LEN, -1)
    past = jnp.concatenate([flat(ckv), flat(kg), flat(vg), flat(kd), flat(vd), flat(kr)], axis=-1)
    return pl.pallas_call(
        _cache_kernel,
        out_shape=jax.ShapeDtypeStruct((DEPTH, B, PAST_LEN, KV_COLS), bf16),
        grid=(DEPTH, B),
        in_specs=[pl.BlockSpec((1, 1, PAST_LEN, PAST_COLS), lambda i, b: (b, i, 0, 0)),
                  pl.BlockSpec((1, 256, 384), lambda i, b: (i, 0, 0)), pl.BlockSpec((1, KV_RANK, 384), lambda i, b: (i, 0, 0))],
        out_specs=pl.BlockSpec((1, 1, PAST_LEN, KV_COLS), lambda i, b: (i, b, 0, 0)),
        compiler_params=_params(("arbitrary", "arbitrary")),
        name="cache_rows",
    )(past, w["w_ka"], w["w_uv"])


_SCORE_HEADS = (
    [(Q_A + 64 * h, KV_KA + 64 * h, 64, KV_VA + MLA_V * h) for h in range(MLA_HEADS)]
    + [(Q_G + 64 * h, KV_KG + 64 * (h // GQA_GROUP), 64, KV_VG + 64 * (h // GQA_GROUP)) for h in range(GQA_HEADS)]
    + [(Q_D + 64 * h + DIFF_QK * c, KV_KD + 64 * h + DIFF_QK * c, DIFF_QK, KV_VD + DIFF_V * h)
       for h in range(DIFF_HEADS) for c in range(2)])


def _attn_kernel(lam_init, per_round, n_src, q_ref, *refs):
    kv_refs = refs[:n_src]
    lam_ref, dg_ref, o_ref, s_ref, p_ref = refs[n_src:]
    spans, start = [], 0
    for r in kv_refs:
        spans.append((r, start, r.shape[2]))
        start += r.shape[2]
    mxu_sum = start >= MXU_SUM_MIN_KEYS

    outs = []
    for first in range(0, len(_SCORE_HEADS), per_round):
        chunk = _SCORE_HEADS[first:first + per_round]
        for j, (q_off, k_off, width, _) in enumerate(chunk):
            for r, lo, size in spans:
                s_ref[j, :, lo:lo + size] = _dot_nt(q_ref[:, q_off:q_off + width], r[0, 0, :, k_off:k_off + width])
        s = s_ref[...]
        p = jnp.exp2(s - jnp.max(s, axis=-1, keepdims=True))
        if mxu_sum:
            p_ref[...] = p.astype(bf16)
            for j, (_, _, _, v_off) in enumerate(chunk):
                o = sum(_dot(p_ref[j, :, lo:lo + size],
                             jnp.concatenate([r[0, 0, :, v_off:v_off + DIFF_V],
                                              jnp.ones((size, LANES - DIFF_V), bf16)], axis=1))
                        for r, lo, size in spans)
                outs.append((o * pltpu.roll(1.0 / o, DIFF_V, 1))[:, :DIFF_V])
        else:
            inv = 1.0 / jnp.sum(p, axis=-1, keepdims=True)
            p_ref[...] = p.astype(bf16)
            for j, (_, _, _, v_off) in enumerate(chunk):
                o = sum(_dot(p_ref[j, :, lo:lo + size], r[0, 0, :, v_off:v_off + DIFF_V]) for r, lo, size in spans)
                outs.append(o * inv[j])

    lp = lam_ref[0]
    e1 = jnp.exp(jnp.sum(lp[0:1] * lp[1:2], axis=-1, keepdims=True))
    e2 = jnp.exp(jnp.sum(lp[2:3] * lp[3:4], axis=-1, keepdims=True))
    lam = e1 - e2 + lam_init
    heads = outs[:MLA_HEADS + GQA_HEADS]
    for h in range(DIFF_HEADS):
        o1, o2 = outs[MLA_HEADS + GQA_HEADS + 2 * h:MLA_HEADS + GQA_HEADS + 2 * h + 2]
        heads.append(_rms(o1 - lam * o2, DIFF_V) * dg_ref[0] * (1.0 - lam_init))
    for j in range(len(heads) // 2):
        o_ref[:, LANES * j:LANES * (j + 1)] = jnp.concatenate(heads[2 * j:2 * j + 2], axis=1).astype(bf16)


def _attention(layer, q, sources, w, tile, per_round):
    n = q.shape[0]
    lam_init = 0.8 - 0.6 * math.exp(-0.3 * layer)
    s_kv = sum(rows for _, rows, _ in sources)
    assert len(_SCORE_HEADS) % per_round == 0
    return pl.pallas_call(
        functools.partial(_attn_kernel, lam_init, per_round, len(sources)),
        out_shape=jax.ShapeDtypeStruct((n, D_MODEL), bf16),
        scratch_shapes=[pltpu.VMEM((per_round, tile, s_kv), f32), pltpu.VMEM((per_round, tile, s_kv), bf16)],
        grid=(n // tile,),
        in_specs=[pl.BlockSpec((tile, Q_COLS), lambda t: (t, 0))]
        + [pl.BlockSpec((1, 1, rows, KV_COLS), index) for _, rows, index in sources]
        + [pl.BlockSpec((1, 4, DIFF_QK), lambda t: (layer, 0, 0)), pl.BlockSpec((1, 1, DIFF_V), lambda t: (layer, 0, 0))],
        out_specs=pl.BlockSpec((tile, D_MODEL), lambda t: (t, 0)),
        compiler_params=_params(("arbitrary",)),
        name="attention",
    )(q, *[arr for arr, _, _ in sources], w["lam"], w["diff_g"])


def _post_kernel(merge, *refs):
    it = iter(refs)
    o_ref, x_ref, mod_ref, w_out_ref, g2_ref, wg_ref, bg_ref, we_ref, be_ref = (next(it) for _ in range(9))
    if merge:
        next(it), next(it), next(it)
    x1_ref, hx_ref, cls_ref, w_scr = (next(it) for _ in range(4))

    @pl.when(pl.program_id(0) == 0)
    def _():
        w_scr[...] = w_out_ref[0].astype(bf16)

    gate1 = mod_ref[0, 0, 2:3, :]
    shift2 = mod_ref[0, 0, 3:4, :]
    scale2 = mod_ref[0, 0, 4:5, :]
    x1 = x_ref[...] + gate1 * _dot(o_ref[...], w_scr[...])
    x1_ref[...] = x1
    h2 = ((_rms(x1, D_MODEL) * g2_ref[0]) * (1.0 + scale2) + shift2).astype(bf16)
    bits = pltpu.bitcast(h2.astype(f32), i32)
    hx_ref[:, 0:HX_HALF] = bits[:, 0:HX_HALF] | jax.lax.shift_right_logical(bits[:, HX_HALF:D_MODEL], 16)

    def first_lane(mask, lane_f):
        return jnp.min(jnp.where(mask, lane_f, float(LANES)), axis=-1, keepdims=True)

    gl = _dot(h2, wg_ref[0].astype(bf16)) + bg_ref[0]
    glane = jax.lax.broadcasted_iota(i32, gl.shape, 1).astype(f32)
    ge = jnp.exp(gl - jnp.max(gl, axis=-1, keepdims=True))
    gprob = ge / jnp.sum(ge, axis=-1, keepdims=True)
    g_top = jnp.max(gprob, axis=-1, keepdims=True)
    g_idx = first_lane(gprob == g_top, glane)

    el = _dot(h2, we_ref[0].astype(bf16)) + be_ref[0]
    lane = jax.lax.broadcasted_iota(i32, el.shape, 1)
    lane_f = lane.astype(f32)
    emask = (lane >> 2).astype(f32) == g_idx
    em = jnp.where(emask, el, -jnp.inf)
    ee = jnp.where(emask, jnp.exp(em - jnp.max(em, axis=-1, keepdims=True)), 0.0)
    ep = ee / jnp.sum(ee, axis=-1, keepdims=True)
    p1 = jnp.max(jnp.where(emask, ep, -1.0), axis=-1, keepdims=True)
    i1 = first_lane(emask & (ep == p1), lane_f)
    rest = emask & (lane_f != i1)
    p2 = jnp.max(jnp.where(rest, ep, -1.0), axis=-1, keepdims=True)
    i2 = first_lane(rest & (ep == p2), lane_f)
    tot = p1 + p2
    w1 = g_top * (p1 / tot)
    w2 = g_top * (p2 / tot)

    lo = jnp.minimum(i1, i2) - EXPERTS_PER_GROUP * g_idx
    hi = jnp.maximum(i1, i2) - EXPERTS_PER_GROUP * g_idx
    pair = jnp.where(lo == 0.0, hi - 1.0, jnp.where(lo == 1.0, jnp.where(hi == 3.0, 3.0, 4.0), 5.0))
    cls_ref[...] = (N_PAIRS * g_idx + pair).astype(i32)
    g_lo = jnp.where(i1 < i2, w1, w2)
    g_hi = jnp.where(i1 < i2, w2, w1)
    tail_lane = jax.lax.broadcasted_iota(i32, (h2.shape[0], LANES), 1)
    hx_ref[:, HX_HALF:HX_COLS] = pltpu.bitcast(
        jnp.where(tail_lane == 0, g_lo, jnp.where(tail_lane == 1, g_hi, 0.0)), i32)


def _post(layer, o, x, row0, total, mod, mod_row, w, tile, merged=None):
    n = o.shape[0]
    lay = lambda t: (layer, 0, 0)
    row = lambda t: (t, 0)
    off = row0 // tile
    out_row = lambda t: (off + t, 0)
    in_specs = [
        pl.BlockSpec((tile, D_MODEL), row),
        pl.BlockSpec((tile, D_MODEL), row),
        pl.BlockSpec((1, 1, N_MOD, D_MODEL), lambda t: (layer, mod_row(t * tile), 0, 0)),
        pl.BlockSpec((1, D_MODEL, D_MODEL), lay),
        pl.BlockSpec((1, 1, D_MODEL), lay),
        pl.BlockSpec((1, D_MODEL, N_GROUPS), lay),
        pl.BlockSpec((1, 1, N_GROUPS), lay),
        pl.BlockSpec((1, D_MODEL, N_EXPERTS), lay),
        pl.BlockSpec((1, 1, N_EXPERTS), lay),
    ]
    args = [o, x, mod, w["w_out"], w["g2"], w["w_grp"], w["b_grp"], w["w_rtr"], w["b_rtr"]]
    aliases = {}
    if merged is not None:
        aliases = {len(args) + j: j for j in range(3)}
        in_specs += [pl.BlockSpec(memory_space=pl.ANY)] * 3
        args += list(merged)
    return pl.pallas_call(
        functools.partial(_post_kernel, merged is not None),
        out_shape=[jax.ShapeDtypeStruct((total, D_MODEL), f32), jax.ShapeDtypeStruct((total, HX_COLS), i32),
                   jax.ShapeDtypeStruct((total, 1), i32)],
        grid=(n // tile,),
        in_specs=in_specs,
        out_specs=[pl.BlockSpec((tile, D_MODEL), out_row), pl.BlockSpec((tile, HX_COLS), out_row),
                   pl.BlockSpec((tile, 1), out_row)],
        scratch_shapes=[pltpu.VMEM((D_MODEL, D_MODEL), bf16)],
        input_output_aliases=aliases,
        compiler_params=_params(("arbitrary",)),
        name="post_attention",
    )(*args)


def _ctx_kernel(lam_init, n_prev, resid, *refs):
    it = iter(refs)
    pre_in = [next(it) for _ in range(9 + (2 if resid else 0) + n_prev)]
    lam_ref, dg_ref = next(it), next(it)
    post_w = [next(it) for _ in range(6)]
    cache_refs = [next(it) for _ in range(len(CACHE_WIDTHS))]
    x1_ref, hx_ref, cls_ref = (next(it) for _ in range(3))
    w_in_scr, q_scr, kv_scr, o_scr, s_scr, p_scr, w_out_scr, x2_scr = (next(it) for _ in range(8))
    x_ref, mod_ref = pre_in[0], pre_in[1]
    x2_ref = [x2_scr] if resid else []

    _pre_kernel(False, n_prev, resid, *pre_in, q_scr, kv_scr.at[0, 0], *x2_ref, *cache_refs, w_in_scr)
    seq = s_scr.shape[1]
    for r in range(q_scr.shape[0] // seq):
        rows = pl.ds(r * seq, seq)
        _attn_kernel(lam_init, len(_SCORE_HEADS), 1, q_scr.at[rows], kv_scr.at[:, :, rows], lam_ref, dg_ref,
                     o_scr.at[rows], s_scr, p_scr)
    _post_kernel(False, o_scr, x2_scr if resid else x_ref, mod_ref, *post_w, x1_ref, hx_ref, cls_ref, w_out_scr)


def _ctx_layer(layer, x, n, seq, total, mod, w, prev_cache=(), resid=None):
    lay = lambda t: (layer, 0, 0)
    row = lambda t: (t, 0)
    reqs = CTX_REQUESTS_PER_STEP
    tile = reqs * seq
    mod_spec = lambda l: pl.BlockSpec((1, 1, N_MOD, D_MODEL), lambda t: (l, 0, 0, 0))
    in_specs = [
        pl.BlockSpec((tile, D_MODEL), row), mod_spec(layer),
        pl.BlockSpec((1, 1, D_MODEL), lay), pl.BlockSpec((1, IN_COLS, D_MODEL), lay),
        pl.BlockSpec((1, 1, KV_RANK), lay), pl.BlockSpec((1, 1, 512), lay), pl.BlockSpec((512, 512), lambda t: (0, 0)),
        pl.BlockSpec((1, 256, 384), lay), pl.BlockSpec((1, KV_RANK, 384), lay),
    ]
    args = [x, mod, w["g1"], w["w_in"], w["kv_g"], w["qk_g"], w["seg"], w["w_ka"], w["w_uv"]]
    if resid is not None:
        in_specs += [pl.BlockSpec((tile, D_MODEL), row), mod_spec(layer - 1)]
        args += [resid, mod]
    out_shape, out_specs, aliases = [], [], {}
    for j, width in enumerate(CACHE_WIDTHS):
        if prev_cache:
            aliases[len(args)] = len(out_shape)
            in_specs.append(pl.BlockSpec(memory_space=pl.ANY))
            args.append(prev_cache[j])
        shape = (MLA_ROPE, seq) if j == 1 else (seq, width)
        out_shape.append(jax.ShapeDtypeStruct((n // seq, DEPTH) + shape, f32))
        out_specs.append(pl.BlockSpec((reqs, 1) + shape, lambda t: (t, layer, 0, 0)))
    in_specs += [
        pl.BlockSpec((1, 4, DIFF_QK), lay), pl.BlockSpec((1, 1, DIFF_V), lay),
        pl.BlockSpec((1, D_MODEL, D_MODEL), lay), pl.BlockSpec((1, 1, D_MODEL), lay),
        pl.BlockSpec((1, D_MODEL, N_GROUPS), lay), pl.BlockSpec((1, 1, N_GROUPS), lay),
        pl.BlockSpec((1, D_MODEL, N_EXPERTS), lay), pl.BlockSpec((1, 1, N_EXPERTS), lay),
    ]
    args += [w["lam"], w["diff_g"], w["w_out"], w["g2"], w["w_grp"], w["b_grp"], w["w_rtr"], w["b_rtr"]]
    out_shape += [jax.ShapeDtypeStruct((total, D_MODEL), f32), jax.ShapeDtypeStruct((total, HX_COLS), i32),
                  jax.ShapeDtypeStruct((total, 1), i32)]
    out_specs += [pl.BlockSpec((tile, D_MODEL), row), pl.BlockSpec((tile, HX_COLS), row), pl.BlockSpec((tile, 1), row)]
    heads = len(_SCORE_HEADS)
    return pl.pallas_call(
        functools.partial(_ctx_kernel, 0.8 - 0.6 * math.exp(-0.3 * layer), len(prev_cache), resid is not None),
        out_shape=out_shape,
        grid=(n // tile,),
        in_specs=in_specs,
        out_specs=out_specs,
        scratch_shapes=[pltpu.VMEM((Z_COLS, D_MODEL), bf16), pltpu.VMEM((tile, Q_COLS), bf16),
                        pltpu.VMEM((1, 1, tile, KV_COLS), bf16), pltpu.VMEM((tile, D_MODEL), bf16),
                        pltpu.VMEM((heads, seq, seq), f32), pltpu.VMEM((heads, seq, seq), bf16),
                        pltpu.VMEM((D_MODEL, D_MODEL), bf16), pltpu.VMEM((tile, D_MODEL), f32)],
        input_output_aliases=aliases,
        compiler_params=_params(("arbitrary",)),
        name="context_layer",
    )(*args)


PLAN_CHUNK = 1024
TAB_ROWS = LANES


def _plan_kernel(cls_ref, slot_ref, tab_ref, rank_scr):
    n = cls_ref.shape[0]
    lane = jax.lax.broadcasted_iota(i32, (PLAN_CHUNK, LANES), 1)
    r = jax.lax.broadcasted_iota(i32, (PLAN_CHUNK, PLAN_CHUNK), 0)
    c = jax.lax.broadcasted_iota(i32, (PLAN_CHUNK, PLAN_CHUNK), 1)
    before = (c < r).astype(bf16)

    def count(b, seen):
        rows = pl.ds(pl.multiple_of(b * PLAN_CHUNK, PLAN_CHUNK), PLAN_CHUNK)
        onehot = (cls_ref[rows, :] == lane).astype(f32)
        ahead = _dot(before, onehot.astype(bf16)) + seen
        rank_scr[rows, :] = jnp.sum(onehot * ahead, axis=-1, keepdims=True)
        return seen + jnp.sum(onehot, axis=0, keepdims=True)

    counts = jax.lax.fori_loop(0, n // PLAN_CHUNK, count, jnp.zeros((1, LANES), f32))
    tiles = jnp.floor((counts + (EXPERT_TILE - 1)) * (1.0 / EXPERT_TILE))
    rr = jax.lax.broadcasted_iota(i32, (LANES, LANES), 0)
    cc = jax.lax.broadcasted_iota(i32, (LANES, LANES), 1)
    ends = _dot(jnp.broadcast_to(tiles, (8, LANES)).astype(bf16), (rr <= cc).astype(bf16))[0:1]
    starts = ends - tiles

    def place(b, carry):
        rows = pl.ds(pl.multiple_of(b * PLAN_CHUNK, PLAN_CHUNK), PLAN_CHUNK)
        first = jnp.sum(jnp.where(cls_ref[rows, :] == lane, starts, 0.0), axis=-1, keepdims=True)
        slot_ref[rows, :] = (first * EXPERT_TILE + rank_scr[rows, :]).astype(i32)
        return carry

    jax.lax.fori_loop(0, n // PLAN_CHUNK, place, 0)

    tl = jax.lax.broadcasted_iota(i32, (TAB_ROWS, LANES), 1)
    n_tiles = jnp.sum(jnp.where(tl[0:1] == N_CLASSES - 1, ends, 0.0), axis=-1, keepdims=True)
    k = jnp.minimum(jax.lax.broadcasted_iota(i32, (TAB_ROWS, 1), 0).astype(f32), n_tiles - 1.0)
    cls_k = jnp.sum(jnp.where((tl < N_CLASSES) & (ends <= k), 1.0, 0.0), axis=-1, keepdims=True)
    cls_k = jnp.minimum(cls_k, N_CLASSES - 1.0)
    mine = tl.astype(f32) == cls_k
    used = jnp.sum(jnp.where(mine, counts, 0.0), axis=-1, keepdims=True)
    first = jnp.sum(jnp.where(mine, starts, 0.0), axis=-1, keepdims=True)
    valid = jnp.clip(used - (k - first) * EXPERT_TILE, 0.0, float(EXPERT_TILE))
    group = jnp.floor((cls_k + 0.5) * (1.0 / N_PAIRS))
    pair = cls_k - N_PAIRS * group
    lo = hi = jnp.zeros_like(pair)
    for p in range(N_PAIRS):
        lo = jnp.where(pair == p, float(PAIR_LO[p]), lo)
        hi = jnp.where(pair == p, float(PAIR_HI[p]), hi)
    e_lo = EXPERTS_PER_GROUP * group + lo
    e_hi = EXPERTS_PER_GROUP * group + hi
    tab = jnp.where(tl == 0, e_lo, jnp.where(tl == 1, e_hi, jnp.where(tl == 2, valid, jnp.where(tl == 3, n_tiles, 0.0))))
    tab_ref[...] = tab.astype(i32)


def _plan(cls, max_tiles):
    n = cls.shape[0]
    assert n % PLAN_CHUNK == 0 and max_tiles <= TAB_ROWS
    slot, tab = pl.pallas_call(
        _plan_kernel,
        out_shape=[jax.ShapeDtypeStruct((n, 1), i32), jax.ShapeDtypeStruct((TAB_ROWS, LANES), i32)],
        scratch_shapes=[pltpu.VMEM((n, 1), f32)],
        compiler_params=_params(None),
        name="dispatch_plan",
    )(cls)
    return slot.reshape(n), tab[:max_tiles, 0], tab[:max_tiles, 1], tab[:max_tiles, 2], tab[0, 3:4]


def _move_rows(src, idx, n_out, scatter):
    n = idx.shape[0]
    width = src.shape[1]
    per_worker = n // (SC_CORES * SC_SUBCORES)
    assert n % SC_ROWS == 0
    chunk = max(c for c in (64, 40, 32, 16, 8)
                if per_worker % c == 0 and 2 * c * width * src.dtype.itemsize <= SC_BUFFER_BYTES)
    n_chunks = per_worker // chunk
    mesh = plsc.VectorSubcoreMesh(core_axis_name="c", subcore_axis_name="s")

    @functools.partial(
        pl.kernel, mesh=mesh, out_type=jax.ShapeDtypeStruct((n_out, width), src.dtype),
        scratch_types=[pltpu.VMEM((chunk,), i32), pltpu.VMEM((chunk,), i32),
                       pltpu.VMEM((chunk, width), src.dtype), pltpu.VMEM((chunk, width), src.dtype),
                       pltpu.SemaphoreType.DMA, pltpu.SemaphoreType.DMA, pltpu.SemaphoreType.DMA,
                       pltpu.SemaphoreType.DMA])
    def move(src_hbm, idx_hbm, out_hbm, idx0, idx1, rows0, rows1, in0, in1, out0, out1):
        wid = jax.lax.axis_index("s") * SC_CORES + jax.lax.axis_index("c")
        base = wid * per_worker
        idx_v, rows_v, sem_in, sem_out = (idx0, idx1), (rows0, rows1), (in0, in1), (out0, out1)

        def fill(j):
            b = j % 2
            rows = pl.ds(base + j * chunk, chunk)
            pltpu.sync_copy(idx_hbm.at[rows], idx_v[b])
            src_rows = src_hbm.at[rows] if scatter else src_hbm.at[idx_v[b]]
            return pltpu.async_copy(src_rows, rows_v[b], sem_in[b])

        def drain(j):
            b = j % 2
            dst_rows = out_hbm.at[idx_v[b]] if scatter else out_hbm.at[pl.ds(base + j * chunk, chunk)]
            return pltpu.async_copy(rows_v[b], dst_rows, sem_out[b])

        fills, drains = {0: fill(0)}, {}
        for j in range(n_chunks):
            if j + 1 < n_chunks:
                if j >= 1:
                    drains[j - 1].wait()
                fills[j + 1] = fill(j + 1)
            fills[j].wait()
            drains[j] = drain(j)
        for j in range(max(n_chunks - 2, 0), n_chunks):
            drains[j].wait()

    return move(src, idx)


def _expert_kernel(lo_ref, hi_ref, valid_ref, nt_ref, xs_ref, wg_ref, wu_ref, wd_ref, ys_ref, wg_scr, wu_scr, wd_scr):
    k = pl.program_id(0)
    prev = jnp.maximum(k - 1, 0)

    @pl.when((k == 0) | (lo_ref[k] // EXPERTS_PER_GROUP != lo_ref[prev] // EXPERTS_PER_GROUP))
    def _():
        for j in range(EXPERTS_PER_GROUP):
            wg_scr[j] = wg_ref[0, 0, j].astype(bf16)
            wu_scr[j] = wu_ref[0, 0, j].astype(bf16)
            wd_scr[j] = wd_ref[0, 0, j].astype(bf16)

    @pl.when(k < nt_ref[0])
    def _():
        live = jax.lax.broadcasted_iota(i32, (EXPERT_TILE, 1), 0) < valid_ref[k]
        words = jnp.where(live, xs_ref[:, 0:HX_HALF], 0)
        x = jnp.concatenate([pltpu.bitcast(words & -65536, f32), pltpu.bitcast(words << 16, f32)], axis=1).astype(bf16)
        gates = pltpu.bitcast(jnp.where(live, xs_ref[:, HX_HALF:HX_COLS], 0), f32)
        y = None
        for lane, e_ref in enumerate((lo_ref, hi_ref)):
            j = e_ref[k] % EXPERTS_PER_GROUP
            hid = _silu(_dot(x, wg_scr[j])) * _dot(x, wu_scr[j]) * gates[:, lane:lane + 1]
            part = _dot(hid.astype(bf16), wd_scr[j])
            y = part if y is None else y + part
        ys_ref[...] = y


def _experts(layer, xs, e_lo, e_hi, valid, n_tiles, w):
    max_tiles = e_lo.shape[0]
    row = lambda k, lo, hi, valid, nt: (jnp.minimum(k, nt[0] - 1), 0)
    group = lambda k, lo, hi, valid, nt: (layer, lo[k] // EXPERTS_PER_GROUP, 0, 0, 0)
    by_group = lambda a: a.reshape(DEPTH, N_GROUPS, EXPERTS_PER_GROUP, *a.shape[2:])
    return pl.pallas_call(
        _expert_kernel,
        out_shape=jax.ShapeDtypeStruct((max_tiles * EXPERT_TILE, D_MODEL), f32),
        grid_spec=pltpu.PrefetchScalarGridSpec(
            num_scalar_prefetch=4, grid=(max_tiles,),
            in_specs=[pl.BlockSpec((EXPERT_TILE, HX_COLS), row),
                      pl.BlockSpec((1, 1, EXPERTS_PER_GROUP, D_MODEL, D_FF_EXPERT), group),
                      pl.BlockSpec((1, 1, EXPERTS_PER_GROUP, D_MODEL, D_FF_EXPERT), group),
                      pl.BlockSpec((1, 1, EXPERTS_PER_GROUP, D_FF_EXPERT, D_MODEL), group)],
            out_specs=pl.BlockSpec((EXPERT_TILE, D_MODEL), row),
            scratch_shapes=[pltpu.VMEM((EXPERTS_PER_GROUP, D_MODEL, D_FF_EXPERT), bf16),
                            pltpu.VMEM((EXPERTS_PER_GROUP, D_MODEL, D_FF_EXPERT), bf16),
                            pltpu.VMEM((EXPERTS_PER_GROUP, D_FF_EXPERT, D_MODEL), bf16)]),
        compiler_params=_params(("arbitrary",)),
        name="experts",
    )(e_lo, e_hi, valid, n_tiles, xs, by_group(w["w_gate"]), by_group(w["w_up"]), by_group(w["w_down"]))


def _final_kernel(x1_ref, y_ref, mod_ref, fg_ref, o_ref):
    o_ref[...] = _rms(x1_ref[...] + mod_ref[0, 0, 5:6, :] * y_ref[...], D_MODEL) * fg_ref[...]


def _final(x1, y, n, row0, mod, mod_row, w, tile):
    off = row0 // tile
    src_row = lambda t: (off + t, 0)
    return pl.pallas_call(
        _final_kernel,
        out_shape=jax.ShapeDtypeStruct((n, D_MODEL), f32),
        grid=(n // tile,),
        in_specs=[pl.BlockSpec((tile, D_MODEL), src_row), pl.BlockSpec((tile, D_MODEL), lambda t: (t, 0)),
                  pl.BlockSpec((1, 1, N_MOD, D_MODEL), lambda t: (DEPTH - 1, mod_row(t * tile), 0, 0)),
                  pl.BlockSpec((1, D_MODEL), lambda t: (0, 0))],
        out_specs=pl.BlockSpec((tile, D_MODEL), lambda t: (t, 0)),
        compiler_params=_params(("arbitrary",)),
        name="final_norm",
    )(x1, y, mod, w["final_g"])


def _rope_tables(n_tokens):
    pos = np.arange(n_tokens)
    row = (pos // GRID_W).astype(np.float64)
    col = (pos % GRID_W).astype(np.float64)

    def cs(rot_dim):
        quarter = rot_dim // 4
        inv = ROPE_THETA ** (-np.arange(quarter, dtype=np.float64) / quarter)
        ang = np.concatenate([row[:, None] * inv, col[:, None] * inv], axis=-1)
        return np.cos(ang), np.sin(ang)

    c32, s32 = cs(MLA_ROPE)
    c64, s64 = cs(HEAD_DIM)
    ones = np.ones((n_tokens, MLA_NOPE))
    zeros = np.zeros((n_tokens, MLA_NOPE))

    def rep(parts):
        period = np.concatenate(parts, axis=-1)
        return jnp.asarray(np.tile(period, (1, LANES // period.shape[-1])), f32)

    return (rep([ones, c32, c32]), rep([zeros, -s32, s32]), rep([c32, c32]), rep([-s32, s32]),
            rep([c64, c64]), rep([-s64, s64]))


def _layout_weights(norm1_g, norm2_g, w_in, mla_kv_norm_g, mla_w_uk, mla_w_uv, gqa_q_norm_g, gqa_k_norm_g,
                    diff_lambda, diff_norm_g, w_out, moe_w_group, moe_b_group, moe_w_router, moe_b_router,
                    moe_w_gate, moe_w_up, moe_w_down, final_norm_g):
    eye = jnp.eye(MLA_ROPE, dtype=f32)
    top = jnp.concatenate([mla_w_uk, jnp.zeros((DEPTH, KV_RANK, MLA_HEADS, MLA_ROPE), f32)], axis=-1)
    mid = jnp.concatenate([jnp.zeros((MLA_ROPE, MLA_HEADS, MLA_NOPE), f32),
                           jnp.broadcast_to(eye[:, None, :], (MLA_ROPE, MLA_HEADS, MLA_ROPE))], axis=-1)
    w_ka = jnp.concatenate([top.reshape(DEPTH, KV_RANK, 384),
                            jnp.broadcast_to(mid.reshape(1, MLA_ROPE, 384), (DEPTH, MLA_ROPE, 384)),
                            jnp.zeros((DEPTH, 256 - KV_RANK - MLA_ROPE, 384), f32)], axis=1).astype(bf16)
    seg_id = np.arange(512) // HEAD_DIM
    seg = jnp.asarray(seg_id[:, None] == seg_id[None, :], bf16)
    qk_g = jnp.concatenate([jnp.tile(gqa_q_norm_g, (1, GQA_HEADS)), jnp.tile(gqa_k_norm_g, (1, GQA_KV_HEADS))], axis=-1)
    return dict(
        g1=norm1_g.reshape(DEPTH, 1, D_MODEL), g2=norm2_g.reshape(DEPTH, 1, D_MODEL),
        w_in=jnp.swapaxes(w_in, 1, 2),
        kv_g=mla_kv_norm_g.reshape(DEPTH, 1, KV_RANK), qk_g=qk_g.reshape(DEPTH, 1, 512), seg=seg, w_ka=w_ka,
        w_uv=mla_w_uv.reshape(DEPTH, KV_RANK, 384), lam=diff_lambda, diff_g=diff_norm_g.reshape(DEPTH, 1, DIFF_V),
        w_out=w_out, w_grp=moe_w_group, b_grp=moe_b_group.reshape(DEPTH, 1, N_GROUPS), w_rtr=moe_w_router,
        b_rtr=moe_b_router.reshape(DEPTH, 1, N_EXPERTS), w_gate=moe_w_gate, w_up=moe_w_up, w_down=moe_w_down,
        final_g=final_norm_g.reshape(1, D_MODEL))


PRE_TILE = 512
CTX_REQUESTS_PER_STEP = 2
LAT_ATTN_TILE = 256
LAT_HEADS_PER_ROUND = 10
MXU_SUM_MIN_KEYS = 1024
POST_TILE = 512
FINAL_TILE = 1024


def kernel(x_prompt, x_sample, c, cache_mla_ckv, cache_mla_krope, cache_gqa_k, cache_gqa_v, cache_diff_k, cache_diff_v, c_ctx, norm1_g, norm2_g, w_mod, b_mod, w_in, mla_kv_norm_g, mla_w_uk, mla_w_uv, gqa_q_norm_g, gqa_k_norm_g, diff_lambda, diff_norm_g, w_out, moe_w_group, moe_b_group, moe_w_router, moe_b_router, moe_w_gate, moe_w_up, moe_w_down, final_norm_g):
    B, S, _ = x_prompt.shape
    Bl, Sl, _ = x_sample.shape
    n_ctx, n_lat = B * S, Bl * Sl
    total = n_ctx + n_lat
    assert Bl + 1 <= MOD_ROWS and DEPTH == 2 and total % SC_ROWS == 0
    slot_rows = -(-(total + N_CLASSES * EXPERT_TILE) // SC_ROWS) * SC_ROWS
    max_tiles = slot_rows // EXPERT_TILE
    w = _layout_weights(norm1_g, norm2_g, w_in, mla_kv_norm_g, mla_w_uk, mla_w_uv, gqa_q_norm_g, gqa_k_norm_g,
                        diff_lambda, diff_norm_g, w_out, moe_w_group, moe_b_group, moe_w_router, moe_b_router,
                        moe_w_gate, moe_w_up, moe_w_down, final_norm_g)
    cond = jnp.concatenate([c_ctx[None, :], c, jnp.zeros((MOD_ROWS - 1 - Bl, D_MODEL), f32)], axis=0)
    mod = _modulation(cond, w_mod, b_mod).reshape(DEPTH, MOD_ROWS, N_MOD, D_MODEL)
    ctx_row = lambda token: 0
    lat_row = lambda token: 1 + token // Sl
    tabs = _rope_tables(Sl)
    kv_past = _cache_rows((cache_mla_ckv, cache_mla_krope, cache_gqa_k, cache_gqa_v, cache_diff_k, cache_diff_v), w)
    per_b = Sl // LAT_ATTN_TILE

    x_ctx, x_lat = x_prompt.reshape(n_ctx, D_MODEL), x_sample.reshape(n_lat, D_MODEL)
    cache = ()
    x1 = y_c = y_l = None
    for i in range(DEPTH):
        if i == 0:
            q_l, kv_l = _pre_latent(i, x_lat, n_lat, 0, Sl, mod, lat_row, w, PRE_TILE, tabs)
        else:
            q_l, kv_l, x_lat = _pre_latent(i, x1, n_lat, n_ctx, Sl, mod, lat_row, w, PRE_TILE, tabs, resid=y_l)
        past = (kv_past, PAST_LEN, lambda t, i=i: (i, t // per_b, 0, 0))
        own = (kv_l.reshape(1, Bl, Sl, KV_COLS), Sl, lambda t: (0, t // per_b, 0, 0))
        o_l = _attention(i, q_l, [past, own], w, LAT_ATTN_TILE, LAT_HEADS_PER_ROUND)
        if i == 0:
            *cache, x1_c, hx_c, cls_c = _ctx_layer(i, x_ctx, n_ctx, S, total, mod, w)
        else:
            y_c, o_l = jax.lax.optimization_barrier((y_c, o_l))
            *cache, x1_c, hx_c, cls_c = _ctx_layer(i, x1, n_ctx, S, total, mod, w, prev_cache=cache, resid=y_c)
        x1, hx, cls = _post(i, o_l, x_lat, n_ctx, total, mod, lat_row, w, POST_TILE, merged=(x1_c, hx_c, cls_c))
        slot, e_lo, e_hi, valid, n_tiles = _plan(cls, max_tiles)
        xs = _move_rows(hx, slot, slot_rows, scatter=True)
        ys = _experts(i, xs, e_lo, e_hi, valid, n_tiles, w)
        if i == DEPTH - 1:
            outs = (cache[0], jnp.swapaxes(cache[1], 2, 3), cache[2].reshape(B, DEPTH, S, GQA_KV_HEADS, HEAD_DIM),
                    cache[3].reshape(B, DEPTH, S, GQA_KV_HEADS, HEAD_DIM),
                    cache[4].reshape(B, DEPTH, S, DIFF_HEADS, 2, DIFF_QK),
                    cache[5].reshape(B, DEPTH, S, DIFF_HEADS, DIFF_V))
            ys, outs = jax.lax.optimization_barrier((ys, outs))
        y_l = _move_rows(ys, slot[n_ctx:], n_lat, scatter=False)
        y_c = _move_rows(ys, slot[:n_ctx], n_ctx, scatter=False)

    y_sample = _final(x1, y_l, n_lat, n_ctx, mod, lat_row, w, FINAL_TILE).reshape(Bl, Sl, D_MODEL)
    y_prompt = _final(x1, y_c, n_ctx, 0, mod, ctx_row, w, FINAL_TILE).reshape(B, S, D_MODEL)
    return (y_prompt, y_sample, *outs)
```

```python
import functools
import math

import jax
import jax.numpy as jnp
import numpy as np
from jax.experimental import pallas as pl
from jax.experimental.pallas import tpu as pltpu
from jax.experimental.pallas import tpu_sc as plsc

D_MODEL = 1024
DEPTH = 2
PAST_LEN = 512
GRID_W = 64
ROPE_THETA = 10000.0
EPS = 1e-6
LOG2E = 1.4426950408889634
N_MOD = 6
HEAD_DIM = 64
MLA_HEADS = 6
MLA_NOPE = 32
MLA_ROPE = 32
MLA_V = 64
KV_RANK = 128
GQA_HEADS = 6
GQA_KV_HEADS = 2
GQA_GROUP = GQA_HEADS // GQA_KV_HEADS
DIFF_HEADS = 4
DIFF_QK = 32
DIFF_V = 64
N_GROUPS = 4
EXPERTS_PER_GROUP = 4
N_EXPERTS = N_GROUPS * EXPERTS_PER_GROUP
D_FF_EXPERT = 256

LANES = 128
MOD_ROWS = 8

IN_COLS = 1952
IN_KR = 512
Z_QA, Z_CKV, Z_QG, Z_KG, Z_VG, Z_QD, Z_KD, Z_VD, Z_KR = 0, 384, 512, 896, 1024, 1152, 1408, 1664, 1920
Z_COLS = 2048
Q_A, Q_G, Q_D, Q_COLS = 0, 384, 768, 1024
KV_KA, KV_VA, KV_KG, KV_VG, KV_KD, KV_VD, KV_COLS = 0, 384, 768, 896, 1024, 1280, 1536
CACHE_WIDTHS = (128, 32, 128, 128, 256, 256)

PAIR_LO = (0, 0, 0, 1, 1, 2)
PAIR_HI = (1, 2, 3, 3, 2, 3)
N_PAIRS = len(PAIR_LO)
N_CLASSES = N_GROUPS * N_PAIRS
HX_HALF = D_MODEL // 2
HX_COLS = HX_HALF + LANES
EXPERT_TILE = 256

SC_CORES, SC_SUBCORES = 2, 16
SC_BUFFER_BYTES = 400 * 1024
SC_ROWS = SC_CORES * SC_SUBCORES * 8

VMEM_LIMIT = 56 * 1024 * 1024

bf16 = jnp.bfloat16
f32 = jnp.float32
i32 = jnp.int32


def _dot(a, b):
    return jnp.dot(a, b, preferred_element_type=f32)


def _dot_nt(a, b):
    return jax.lax.dot_general(a, b, (((1,), (1,)), ((), ())), preferred_element_type=f32)


def _rms(x, width):
    return x * jax.lax.rsqrt(jnp.sum(x * x, axis=-1, keepdims=True) * (1.0 / width) + EPS)


def _silu(x):
    return x * (1.0 / (1.0 + jnp.exp(-x)))


def _params(sem):
    return pltpu.CompilerParams(dimension_semantics=sem, vmem_limit_bytes=VMEM_LIMIT)


def _mod_kernel(cond_ref, w_ref, b_ref, o_ref):
    o_ref[0] = _dot(_silu(cond_ref[...]).astype(bf16), w_ref[0].astype(bf16)) + b_ref[0]


def _modulation(cond, w_mod, b_mod):
    return pl.pallas_call(
        _mod_kernel,
        out_shape=jax.ShapeDtypeStruct((DEPTH, MOD_ROWS, N_MOD * D_MODEL), f32),
        grid=(DEPTH, N_MOD),
        in_specs=[
            pl.BlockSpec((MOD_ROWS, D_MODEL), lambda i, j: (0, 0)),
            pl.BlockSpec((1, D_MODEL, D_MODEL), lambda i, j: (i, 0, j)),
            pl.BlockSpec((1, 1, D_MODEL), lambda i, j: (i, 0, j)),
        ],
        out_specs=pl.BlockSpec((1, MOD_ROWS, D_MODEL), lambda i, j: (i, 0, j)),
        compiler_params=_params(("arbitrary", "arbitrary")),
        name="modulation",
    )(cond, w_mod, b_mod.reshape(DEPTH, 1, N_MOD * D_MODEL))


def _swap_halves(x, half):
    lane = jax.lax.broadcasted_iota(i32, x.shape, 1)
    fwd = pltpu.roll(x, LANES - half, 1)
    bwd = pltpu.roll(x, half, 1)
    return jnp.where((lane & (2 * half - 1)) < half, fwd, bwd)


def _rope_block(x, cos, sin, half):
    return x * cos + _swap_halves(x, half) * sin


def _pre_kernel(rope, n_prev, resid, *refs):
    it = iter(refs)
    x_ref, mod_ref, g1_ref, w_in_ref, kvg_ref, qkg_ref, seg_ref, wka_ref, wuv_ref = (next(it) for _ in range(9))
    if resid:
        y_ref, pmod_ref = next(it), next(it)
    if rope:
        ca_ref, sa_ref, c32_ref, s32_ref, c64_ref, s64_ref = (next(it) for _ in range(6))
    prev_refs = [next(it) for _ in range(n_prev)]
    q_ref, kv_ref = next(it), next(it)
    if resid:
        x2_ref = next(it)
    cache_refs = [] if rope else [next(it) for _ in range(len(CACHE_WIDTHS))]
    w_scr = next(it)

    @pl.when(pl.program_id(0) == 0)
    def _():
        w_scr[0:IN_KR] = w_in_ref[0, 0:IN_KR].astype(bf16)
        w_scr[IN_KR:Z_KR] = w_in_ref[0, IN_KR + MLA_ROPE:IN_COLS].astype(bf16)
        w_scr[Z_KR:Z_KR + MLA_ROPE] = w_in_ref[0, IN_KR:IN_KR + MLA_ROPE].astype(bf16)
        w_scr[Z_KR + MLA_ROPE:Z_COLS] = jnp.zeros((Z_COLS - Z_KR - MLA_ROPE, D_MODEL), bf16)

    x = x_ref[...]
    if resid:
        x = x + pmod_ref[0, 0, 5:6, :] * y_ref[...]
        x2_ref[...] = x
    shift1 = mod_ref[0, 0, 0:1, :]
    scale1 = mod_ref[0, 0, 1:2, :]
    h = (_rms(x, D_MODEL) * g1_ref[0]) * (1.0 + scale1) + shift1
    z = _dot_nt(h.astype(bf16), w_scr[...])

    ckv = _rms(z[:, Z_CKV:Z_CKV + KV_RANK], KV_RANK) * kvg_ref[0]

    qk = z[:, Z_QG:Z_VG]
    sq = qk * qk
    sq_hi = sq.astype(bf16)
    sq_lo = (sq - sq_hi.astype(f32)).astype(bf16)
    seg = seg_ref[...]
    ms = (_dot(sq_hi, seg) + _dot(sq_lo, seg)) * (1.0 / HEAD_DIM)
    qk = qk * jax.lax.rsqrt(ms + EPS) * qkg_ref[0]

    def blocks(arr, n):
        return [arr[:, LANES * j:LANES * (j + 1)] for j in range(n)]

    qa = blocks(z[:, Z_QA:Z_QA + 384], 3)
    qkb = blocks(qk, 4)
    qd = blocks(z[:, Z_QD:Z_QD + 256], 2)
    kd = blocks(z[:, Z_KD:Z_KD + 256], 2)
    kr = z[:, Z_KR:Z_KR + LANES]
    if rope:
        ca, sa, c32, s32, c64, s64 = (r[...] for r in (ca_ref, sa_ref, c32_ref, s32_ref, c64_ref, s64_ref))
        qa = [_rope_block(b, ca, sa, MLA_ROPE // 2) for b in qa]
        qkb = [_rope_block(b, c64, s64, HEAD_DIM // 2) for b in qkb]
        qd = [_rope_block(b, c32, s32, DIFF_QK // 2) for b in qd]
        kd = [_rope_block(b, c32, s32, DIFF_QK // 2) for b in kd]
        kr = _rope_block(kr, c32, s32, MLA_ROPE // 2)

    vg = z[:, Z_VG:Z_VG + 128]
    vd = z[:, Z_VD:Z_VD + 256]
    ckv_b = ckv.astype(bf16)
    k_a = _dot(jnp.concatenate([ckv_b, kr.astype(bf16)], axis=1), wka_ref[0])
    v_a = _dot(ckv_b, wuv_ref[0].astype(bf16))

    for j in range(3):
        q_ref[:, Q_A + LANES * j:Q_A + LANES * (j + 1)] = (qa[j] * (HEAD_DIM ** -0.5 * LOG2E)).astype(bf16)
        q_ref[:, Q_G + LANES * j:Q_G + LANES * (j + 1)] = (qkb[j] * (HEAD_DIM ** -0.5 * LOG2E)).astype(bf16)
    for j in range(2):
        q_ref[:, Q_D + LANES * j:Q_D + LANES * (j + 1)] = (qd[j] * (DIFF_QK ** -0.5 * LOG2E)).astype(bf16)
        kv_ref[:, KV_KD + LANES * j:KV_KD + LANES * (j + 1)] = kd[j].astype(bf16)
    kv_ref[:, KV_KA:KV_KA + 384] = k_a.astype(bf16)
    kv_ref[:, KV_VA:KV_VA + 384] = v_a.astype(bf16)
    kv_ref[:, KV_KG:KV_KG + 128] = qkb[3].astype(bf16)
    kv_ref[:, KV_VG:KV_VG + 128] = vg.astype(bf16)
    kv_ref[:, KV_VD:KV_VD + 256] = vd.astype(bf16)
    if not rope:
        rows = [ckv, None, qkb[3], vg, jnp.concatenate(kd, axis=1), vd]
        for out, new in zip(cache_refs, rows):
            if new is None:
                seq = out.shape[3]
                for r in range(out.shape[0]):
                    out[r, 0] = kr[r * seq:(r + 1) * seq].T[:MLA_ROPE, :]
            else:
                reqs, _, seq, width = out.shape
                out[:, 0] = new.reshape(reqs, seq, width)


def _pre_latent(layer, x, n, row0, seq, mod, mod_row, w, tile, rope_tabs, resid=None):
    lay = lambda t: (layer, 0, 0)
    row = lambda t: (t, 0)
    off = row0 // tile
    src_row = lambda t: (off + t, 0)
    in_specs = [
        pl.BlockSpec((tile, D_MODEL), src_row),
        pl.BlockSpec((1, 1, N_MOD, D_MODEL), lambda t: (layer, mod_row(t * tile), 0, 0)),
        pl.BlockSpec((1, 1, D_MODEL), lay),
        pl.BlockSpec((1, IN_COLS, D_MODEL), lay),
        pl.BlockSpec((1, 1, KV_RANK), lay),
        pl.BlockSpec((1, 1, 512), lay),
        pl.BlockSpec((512, 512), lambda t: (0, 0)),
        pl.BlockSpec((1, 256, 384), lay),
        pl.BlockSpec((1, KV_RANK, 384), lay),
    ]
    args = [x, mod, w["g1"], w["w_in"], w["kv_g"], w["qk_g"], w["seg"], w["w_ka"], w["w_uv"]]
    if resid is not None:
        in_specs += [pl.BlockSpec((tile, D_MODEL), row),
                     pl.BlockSpec((1, 1, N_MOD, D_MODEL), lambda t: (layer - 1, mod_row(t * tile), 0, 0))]
        args += [resid, mod]
    per_b = seq // tile
    in_specs += [pl.BlockSpec((tile, LANES), lambda t: (t % per_b, 0))] * 6
    args += list(rope_tabs)
    out_shape = [jax.ShapeDtypeStruct((n, Q_COLS), bf16), jax.ShapeDtypeStruct((n, KV_COLS), bf16)]
    out_specs = [pl.BlockSpec((tile, Q_COLS), row), pl.BlockSpec((tile, KV_COLS), row)]
    if resid is not None:
        out_shape.append(jax.ShapeDtypeStruct((n, D_MODEL), f32))
        out_specs.append(pl.BlockSpec((tile, D_MODEL), row))
    return pl.pallas_call(
        functools.partial(_pre_kernel, True, 0, resid is not None),
        out_shape=out_shape,
        grid=(n // tile,),
        in_specs=in_specs,
        out_specs=out_specs,
        scratch_shapes=[pltpu.VMEM((Z_COLS, D_MODEL), bf16)],
        compiler_params=_params(("arbitrary",)),
        name="pre_latent",
    )(*args)


PAST_CKV, PAST_KG, PAST_VG, PAST_KD, PAST_VD, PAST_KR, PAST_COLS = 0, 128, 256, 384, 640, 896, 928


def _cache_kernel(past_ref, wka_ref, wuv_ref, kv_ref):
    ckv_b = past_ref[0, 0, :, PAST_CKV:PAST_CKV + KV_RANK].astype(bf16)
    kr_b = past_ref[0, 0, :, PAST_KR:PAST_KR + MLA_ROPE].astype(bf16)
    wka = wka_ref[0]
    k_a = _dot(ckv_b, wka[:KV_RANK]) + _dot(kr_b, wka[KV_RANK:KV_RANK + MLA_ROPE])
    kv_ref[0, 0, :, KV_KA:KV_KA + 384] = k_a.astype(bf16)
    kv_ref[0, 0, :, KV_VA:KV_VA + 384] = _dot(ckv_b, wuv_ref[0].astype(bf16)).astype(bf16)
    kv_ref[0, 0, :, KV_KG:KV_KG + 128] = past_ref[0, 0, :, PAST_KG:PAST_KG + 128].astype(bf16)
    kv_ref[0, 0, :, KV_VG:KV_VG + 128] = past_ref[0, 0, :, PAST_VG:PAST_VG + 128].astype(bf16)
    kv_ref[0, 0, :, KV_KD:KV_KD + 256] = past_ref[0, 0, :, PAST_KD:PAST_KD + 256].astype(bf16)
    kv_ref[0, 0, :, KV_VD:KV_VD + 256] = past_ref[0, 0, :, PAST_VD:PAST_VD + 256].astype(bf16)


def _cache_rows(caches, w):
    ckv, kr, kg, vg, kd, vd = caches
    B = ckv.shape[0]
    flat = lambda a: a.reshape(B, DEPTH, PAST_LEN, -1)
    past = jnp.concatenate([flat(ckv), flat(kg), flat(vg), flat(kd), flat(vd), flat(kr)], axis=-1)
    return pl.pallas_call(
        _cache_kernel,
        out_shape=jax.ShapeDtypeStruct((DEPTH, B, PAST_LEN, KV_COLS), bf16),
        grid=(DEPTH, B),
        in_specs=[pl.BlockSpec((1, 1, PAST_LEN, PAST_COLS), lambda i, b: (b, i, 0, 0)),
                  pl.BlockSpec((1, 256, 384), lambda i, b: (i, 0, 0)), pl.BlockSpec((1, KV_RANK, 384), lambda i, b: (i, 0, 0))],
        out_specs=pl.BlockSpec((1, 1, PAST_LEN, KV_COLS), lambda i, b: (i, b, 0, 0)),
        compiler_params=_params(("arbitrary", "arbitrary")),
        name="cache_rows",
    )(past, w["w_ka"], w["w_uv"])


_SCORE_HEADS = (
    [(Q_A + 64 * h, KV_KA + 64 * h, 64, KV_VA + MLA_V * h) for h in range(MLA_HEADS)]
    + [(Q_G + 64 * h, KV_KG + 64 * (h // GQA_GROUP), 64, KV_VG + 64 * (h // GQA_GROUP)) for h in range(GQA_HEADS)]
    + [(Q_D + 64 * h + DIFF_QK * c, KV_KD + 64 * h + DIFF_QK * c, DIFF_QK, KV_VD + DIFF_V * h)
       for h in range(DIFF_HEADS) for c in range(2)])


def _attn_kernel(lam_init, per_round, n_src, q_ref, *refs):
    kv_refs = refs[:n_src]
    lam_ref, dg_ref, o_ref, s_ref, p_ref = refs[n_src:]
    spans, start = [], 0
    for r in kv_refs:
        spans.append((r, start, r.shape[2]))
        start += r.shape[2]
    mxu_sum = start >= MXU_SUM_MIN_KEYS

    outs = []
    for first in range(0, len(_SCORE_HEADS), per_round):
        chunk = _SCORE_HEADS[first:first + per_round]
        for j, (q_off, k_off, width, _) in enumerate(chunk):
            for r, lo, size in spans:
                s_ref[j, :, lo:lo + size] = _dot_nt(q_ref[:, q_off:q_off + width], r[0, 0, :, k_off:k_off + width])
        s = s_ref[...]
        p = jnp.exp2(s - jnp.max(s, axis=-1, keepdims=True))
        if mxu_sum:
            p_ref[...] = p.astype(bf16)
            for j, (_, _, _, v_off) in enumerate(chunk):
                o = sum(_dot(p_ref[j, :, lo:lo + size],
                             jnp.concatenate([r[0, 0, :, v_off:v_off + DIFF_V],
                                              jnp.ones((size, LANES - DIFF_V), bf16)], axis=1))
                        for r, lo, size in spans)
                outs.append((o * pltpu.roll(1.0 / o, DIFF_V, 1))[:, :DIFF_V])
        else:
            inv = 1.0 / jnp.sum(p, axis=-1, keepdims=True)
            p_ref[...] = p.astype(bf16)
            for j, (_, _, _, v_off) in enumerate(chunk):
                o = sum(_dot(p_ref[j, :, lo:lo + size], r[0, 0, :, v_off:v_off + DIFF_V]) for r, lo, size in spans)
                outs.append(o * inv[j])

    lp = lam_ref[0]
    e1 = jnp.exp(jnp.sum(lp[0:1] * lp[1:2], axis=-1, keepdims=True))
    e2 = jnp.exp(jnp.sum(lp[2:3] * lp[3:4], axis=-1, keepdims=True))
    lam = e1 - e2 + lam_init
    heads = outs[:MLA_HEADS + GQA_HEADS]
    for h in range(DIFF_HEADS):
        o1, o2 = outs[MLA_HEADS + GQA_HEADS + 2 * h:MLA_HEADS + GQA_HEADS + 2 * h + 2]
        heads.append(_rms(o1 - lam * o2, DIFF_V) * dg_ref[0] * (1.0 - lam_init))
    for j in range(len(heads) // 2):
        o_ref[:, LANES * j:LANES * (j + 1)] = jnp.concatenate(heads[2 * j:2 * j + 2], axis=1).astype(bf16)


def _attention(layer, q, sources, w, tile, per_round):
    n = q.shape[0]
    lam_init = 0.8 - 0.6 * math.exp(-0.3 * layer)
    s_kv = sum(rows for _, rows, _ in sources)
    assert len(_SCORE_HEADS) % per_round == 0
    return pl.pallas_call(
        functools.partial(_attn_kernel, lam_init, per_round, len(sources)),
        out_shape=jax.ShapeDtypeStruct((n, D_MODEL), bf16),
        scratch_shapes=[pltpu.VMEM((per_round, tile, s_kv), f32), pltpu.VMEM((per_round, tile, s_kv), bf16)],
        grid=(n // tile,),
        in_specs=[pl.BlockSpec((tile, Q_COLS), lambda t: (t, 0))]
        + [pl.BlockSpec((1, 1, rows, KV_COLS), index) for _, rows, index in sources]
        + [pl.BlockSpec((1, 4, DIFF_QK), lambda t: (layer, 0, 0)), pl.BlockSpec((1, 1, DIFF_V), lambda t: (layer, 0, 0))],
        out_specs=pl.BlockSpec((tile, D_MODEL), lambda t: (t, 0)),
        compiler_params=_params(("arbitrary",)),
        name="attention",
    )(q, *[arr for arr, _, _ in sources], w["lam"], w["diff_g"])


def _post_kernel(merge, *refs):
    it = iter(refs)
    o_ref, x_ref, mod_ref, w_out_ref, g2_ref, wg_ref, bg_ref, we_ref, be_ref = (next(it) for _ in range(9))
    if merge:
        next(it), next(it), next(it)
    x1_ref, hx_ref, cls_ref, w_scr = (next(it) for _ in range(4))

    @pl.when(pl.program_id(0) == 0)
    def _():
        w_scr[...] = w_out_ref[0].astype(bf16)

    gate1 = mod_ref[0, 0, 2:3, :]
    shift2 = mod_ref[0, 0, 3:4, :]
    scale2 = mod_ref[0, 0, 4:5, :]
    x1 = x_ref[...] + gate1 * _dot(o_ref[...], w_scr[...])
    x1_ref[...] = x1
    h2 = ((_rms(x1, D_MODEL) * g2_ref[0]) * (1.0 + scale2) + shift2).astype(bf16)
    bits = pltpu.bitcast(h2.astype(f32), i32)
    hx_ref[:, 0:HX_HALF] = bits[:, 0:HX_HALF] | jax.lax.shift_right_logical(bits[:, HX_HALF:D_MODEL], 16)

    def first_lane(mask, lane_f):
        return jnp.min(jnp.where(mask, lane_f, float(LANES)), axis=-1, keepdims=True)

    gl = _dot(h2, wg_ref[0].astype(bf16)) + bg_ref[0]
    glane = jax.lax.broadcasted_iota(i32, gl.shape, 1).astype(f32)
    ge = jnp.exp(gl - jnp.max(gl, axis=-1, keepdims=True))
    gprob = ge / jnp.sum(ge, axis=-1, keepdims=True)
    g_top = jnp.max(gprob, axis=-1, keepdims=True)
    g_idx = first_lane(gprob == g_top, glane)

    el = _dot(h2, we_ref[0].astype(bf16)) + be_ref[0]
    lane = jax.lax.broadcasted_iota(i32, el.shape, 1)
    lane_f = lane.astype(f32)
    emask = (lane >> 2).astype(f32) == g_idx
    em = jnp.where(emask, el, -jnp.inf)
    ee = jnp.where(emask, jnp.exp(em - jnp.max(em, axis=-1, keepdims=True)), 0.0)
    ep = ee / jnp.sum(ee, axis=-1, keepdims=True)
    p1 = jnp.max(jnp.where(emask, ep, -1.0), axis=-1, keepdims=True)
    i1 = first_lane(emask & (ep == p1), lane_f)
    rest = emask & (lane_f != i1)
    p2 = jnp.max(jnp.where(rest, ep, -1.0), axis=-1, keepdims=True)
    i2 = first_lane(rest & (ep == p2), lane_f)
    tot = p1 + p2
    w1 = g_top * (p1 / tot)
    w2 = g_top * (p2 / tot)

    lo = jnp.minimum(i1, i2) - EXPERTS_PER_GROUP * g_idx
    hi = jnp.maximum(i1, i2) - EXPERTS_PER_GROUP * g_idx
    pair = jnp.where(lo == 0.0, hi - 1.0, jnp.where(lo == 1.0, jnp.where(hi == 3.0, 3.0, 4.0), 5.0))
    cls_ref[...] = (N_PAIRS * g_idx + pair).astype(i32)
    g_lo = jnp.where(i1 < i2, w1, w2)
    g_hi = jnp.where(i1 < i2, w2, w1)
    tail_lane = jax.lax.broadcasted_iota(i32, (h2.shape[0], LANES), 1)
    hx_ref[:, HX_HALF:HX_COLS] = pltpu.bitcast(
        jnp.where(tail_lane == 0, g_lo, jnp.where(tail_lane == 1, g_hi, 0.0)), i32)


def _post(layer, o, x, row0, total, mod, mod_row, w, tile, merged=None):
    n = o.shape[0]
    lay = lambda t: (layer, 0, 0)
    row = lambda t: (t, 0)
    off = row0 // tile
    out_row = lambda t: (off + t, 0)
    in_specs = [
        pl.BlockSpec((tile, D_MODEL), row),
        pl.BlockSpec((tile, D_MODEL), row),
        pl.BlockSpec((1, 1, N_MOD, D_MODEL), lambda t: (layer, mod_row(t * tile), 0, 0)),
        pl.BlockSpec((1, D_MODEL, D_MODEL), lay),
        pl.BlockSpec((1, 1, D_MODEL), lay),
        pl.BlockSpec((1, D_MODEL, N_GROUPS), lay),
        pl.BlockSpec((1, 1, N_GROUPS), lay),
        pl.BlockSpec((1, D_MODEL, N_EXPERTS), lay),
        pl.BlockSpec((1, 1, N_EXPERTS), lay),
    ]
    args = [o, x, mod, w["w_out"], w["g2"], w["w_grp"], w["b_grp"], w["w_rtr"], w["b_rtr"]]
    aliases = {}
    if merged is not None:
        aliases = {len(args) + j: j for j in range(3)}
        in_specs += [pl.BlockSpec(memory_space=pl.ANY)] * 3
        args += list(merged)
    return pl.pallas_call(
        functools.partial(_post_kernel, merged is not None),
        out_shape=[jax.ShapeDtypeStruct((total, D_MODEL), f32), jax.ShapeDtypeStruct((total, HX_COLS), i32),
                   jax.ShapeDtypeStruct((total, 1), i32)],
        grid=(n // tile,),
        in_specs=in_specs,
        out_specs=[pl.BlockSpec((tile, D_MODEL), out_row), pl.BlockSpec((tile, HX_COLS), out_row),
                   pl.BlockSpec((tile, 1), out_row)],
        scratch_shapes=[pltpu.VMEM((D_MODEL, D_MODEL), bf16)],
        input_output_aliases=aliases,
        compiler_params=_params(("arbitrary",)),
        name="post_attention",
    )(*args)


def _ctx_kernel(lam_init, n_prev, resid, *refs):
    it = iter(refs)
    pre_in = [next(it) for _ in range(9 + (2 if resid else 0) + n_prev)]
    lam_ref, dg_ref = next(it), next(it)
    post_w = [next(it) for _ in range(6)]
    cache_refs = [next(it) for _ in range(len(CACHE_WIDTHS))]
    x1_ref, hx_ref, cls_ref = (next(it) for _ in range(3))
    w_in_scr, q_scr, kv_scr, o_scr, s_scr, p_scr, w_out_scr, x2_scr = (next(it) for _ in range(8))
    x_ref, mod_ref = pre_in[0], pre_in[1]
    x2_ref = [x2_scr] if resid else []

    _pre_kernel(False, n_prev, resid, *pre_in, q_scr, kv_scr.at[0, 0], *x2_ref, *cache_refs, w_in_scr)
    seq = s_scr.shape[1]
    for r in range(q_scr.shape[0] // seq):
        rows = pl.ds(r * seq, seq)
        _attn_kernel(lam_init, len(_SCORE_HEADS), 1, q_scr.at[rows], kv_scr.at[:, :, rows], lam_ref, dg_ref,
                     o_scr.at[rows], s_scr, p_scr)
    _post_kernel(False, o_scr, x2_scr if resid else x_ref, mod_ref, *post_w, x1_ref, hx_ref, cls_ref, w_out_scr)


def _ctx_layer(layer, x, n, seq, total, mod, w, prev_cache=(), resid=None):
    lay = lambda t: (layer, 0, 0)
    row = lambda t: (t, 0)
    reqs = CTX_REQUESTS_PER_STEP
    tile = reqs * seq
    mod_spec = lambda l: pl.BlockSpec((1, 1, N_MOD, D_MODEL), lambda t: (l, 0, 0, 0))
    in_specs = [
        pl.BlockSpec((tile, D_MODEL), row), mod_spec(layer),
        pl.BlockSpec((1, 1, D_MODEL), lay), pl.BlockSpec((1, IN_COLS, D_MODEL), lay),
        pl.BlockSpec((1, 1, KV_RANK), lay), pl.BlockSpec((1, 1, 512), lay), pl.BlockSpec((512, 512), lambda t: (0, 0)),
        pl.BlockSpec((1, 256, 384), lay), pl.BlockSpec((1, KV_RANK, 384), lay),
    ]
    args = [x, mod, w["g1"], w["w_in"], w["kv_g"], w["qk_g"], w["seg"], w["w_ka"], w["w_uv"]]
    if resid is not None:
        in_specs += [pl.BlockSpec((tile, D_MODEL), row), mod_spec(layer - 1)]
        args += [resid, mod]
    out_shape, out_specs, aliases = [], [], {}
    for j, width in enumerate(CACHE_WIDTHS):
        if prev_cache:
            aliases[len(args)] = len(out_shape)
            in_specs.append(pl.BlockSpec(memory_space=pl.ANY))
            args.append(prev_cache[j])
        shape = (MLA_ROPE, seq) if j == 1 else (seq, width)
        out_shape.append(jax.ShapeDtypeStruct((n // seq, DEPTH) + shape, f32))
        out_specs.append(pl.BlockSpec((reqs, 1) + shape, lambda t: (t, layer, 0, 0)))
    in_specs += [
        pl.BlockSpec((1, 4, DIFF_QK), lay), pl.BlockSpec((1, 1, DIFF_V), lay),
        pl.BlockSpec((1, D_MODEL, D_MODEL), lay), pl.BlockSpec((1, 1, D_MODEL), lay),
        pl.BlockSpec((1, D_MODEL, N_GROUPS), lay), pl.BlockSpec((1, 1, N_GROUPS), lay),
        pl.BlockSpec((1, D_MODEL, N_EXPERTS), lay), pl.BlockSpec((1, 1, N_EXPERTS), lay),
    ]
    args += [w["lam"], w["diff_g"], w["w_out"], w["g2"], w["w_grp"], w["b_grp"], w["w_rtr"], w["b_rtr"]]
    out_shape += [jax.ShapeDtypeStruct((total, D_MODEL), f32), jax.ShapeDtypeStruct((total, HX_COLS), i32),
                  jax.ShapeDtypeStruct((total, 1), i32)]
    out_specs += [pl.BlockSpec((tile, D_MODEL), row), pl.BlockSpec((tile, HX_COLS), row), pl.BlockSpec((tile, 1), row)]
    heads = len(_SCORE_HEADS)
    return pl.pallas_call(
        functools.partial(_ctx_kernel, 0.8 - 0.6 * math.exp(-0.3 * layer), len(prev_cache), resid is not None),
        out_shape=out_shape,
        grid=(n // tile,),
        in_specs=in_specs,
        out_specs=out_specs,
        scratch_shapes=[pltpu.VMEM((Z_COLS, D_MODEL), bf16), pltpu.VMEM((tile, Q_COLS), bf16),
                        pltpu.VMEM((1, 1, tile, KV_COLS), bf16), pltpu.VMEM((tile, D_MODEL), bf16),
                        pltpu.VMEM((heads, seq, seq), f32), pltpu.VMEM((heads, seq, seq), bf16),
                        pltpu.VMEM((D_MODEL, D_MODEL), bf16), pltpu.VMEM((tile, D_MODEL), f32)],
        input_output_aliases=aliases,
        compiler_params=_params(("arbitrary",)),
        name="context_layer",
    )(*args)


PLAN_CHUNK = 1024
TAB_ROWS = LANES


def _plan_kernel(cls_ref, slot_ref, tab_ref, rank_scr):
    n = cls_ref.shape[0]
    lane = jax.lax.broadcasted_iota(i32, (PLAN_CHUNK, LANES), 1)
    r = jax.lax.broadcasted_iota(i32, (PLAN_CHUNK, PLAN_CHUNK), 0)
    c = jax.lax.broadcasted_iota(i32, (PLAN_CHUNK, PLAN_CHUNK), 1)
    before = (c < r).astype(bf16)

    def count(b, seen):
        rows = pl.ds(pl.multiple_of(b * PLAN_CHUNK, PLAN_CHUNK), PLAN_CHUNK)
        onehot = (cls_ref[rows, :] == lane).astype(f32)
        ahead = _dot(before, onehot.astype(bf16)) + seen
        rank_scr[rows, :] = jnp.sum(onehot * ahead, axis=-1, keepdims=True)
        return seen + jnp.sum(onehot, axis=0, keepdims=True)

    counts = jax.lax.fori_loop(0, n // PLAN_CHUNK, count, jnp.zeros((1, LANES), f32))
    tiles = jnp.floor((counts + (EXPERT_TILE - 1)) * (1.0 / EXPERT_TILE))
    rr = jax.lax.broadcasted_iota(i32, (LANES, LANES), 0)
    cc = jax.lax.broadcasted_iota(i32, (LANES, LANES), 1)
    ends = _dot(jnp.broadcast_to(tiles, (8, LANES)).astype(bf16), (rr <= cc).astype(bf16))[0:1]
    starts = ends - tiles

    def place(b, carry):
        rows = pl.ds(pl.multiple_of(b * PLAN_CHUNK, PLAN_CHUNK), PLAN_CHUNK)
        first = jnp.sum(jnp.where(cls_ref[rows, :] == lane, starts, 0.0), axis=-1, keepdims=True)
        slot_ref[rows, :] = (first * EXPERT_TILE + rank_scr[rows, :]).astype(i32)
        return carry

    jax.lax.fori_loop(0, n // PLAN_CHUNK, place, 0)

    tl = jax.lax.broadcasted_iota(i32, (TAB_ROWS, LANES), 1)
    n_tiles = jnp.sum(jnp.where(tl[0:1] == N_CLASSES - 1, ends, 0.0), axis=-1, keepdims=True)
    k = jnp.minimum(jax.lax.broadcasted_iota(i32, (TAB_ROWS, 1), 0).astype(f32), n_tiles - 1.0)
    cls_k = jnp.sum(jnp.where((tl < N_CLASSES) & (ends <= k), 1.0, 0.0), axis=-1, keepdims=True)
    cls_k = jnp.minimum(cls_k, N_CLASSES - 1.0)
    mine = tl.astype(f32) == cls_k
    used = jnp.sum(jnp.where(mine, counts, 0.0), axis=-1, keepdims=True)
    first = jnp.sum(jnp.where(mine, starts, 0.0), axis=-1, keepdims=True)
    valid = jnp.clip(used - (k - first) * EXPERT_TILE, 0.0, float(EXPERT_TILE))
    group = jnp.floor((cls_k + 0.5) * (1.0 / N_PAIRS))
    pair = cls_k - N_PAIRS * group
    lo = hi = jnp.zeros_like(pair)
    for p in range(N_PAIRS):
        lo = jnp.where(pair == p, float(PAIR_LO[p]), lo)
        hi = jnp.where(pair == p, float(PAIR_HI[p]), hi)
    e_lo = EXPERTS_PER_GROUP * group + lo
    e_hi = EXPERTS_PER_GROUP * group + hi
    tab = jnp.where(tl == 0, e_lo, jnp.where(tl == 1, e_hi, jnp.where(tl == 2, valid, jnp.where(tl == 3, n_tiles, 0.0))))
    tab_ref[...] = tab.astype(i32)


def _plan(cls, max_tiles):
    n = cls.shape[0]
    assert n % PLAN_CHUNK == 0 and max_tiles <= TAB_ROWS
    slot, tab = pl.pallas_call(
        _plan_kernel,
        out_shape=[jax.ShapeDtypeStruct((n, 1), i32), jax.ShapeDtypeStruct((TAB_ROWS, LANES), i32)],
        scratch_shapes=[pltpu.VMEM((n, 1), f32)],
        compiler_params=_params(None),
        name="dispatch_plan",
    )(cls)
    return slot.reshape(n), tab[:max_tiles, 0], tab[:max_tiles, 1], tab[:max_tiles, 2], tab[0, 3:4]


def _move_rows(src, idx, n_out, scatter):
    n = idx.shape[0]
    width = src.shape[1]
    per_worker = n // (SC_CORES * SC_SUBCORES)
    assert n % SC_ROWS == 0
    chunk = max(c for c in (64, 40, 32, 16, 8)
                if per_worker % c == 0 and 2 * c * width * src.dtype.itemsize <= SC_BUFFER_BYTES)
    n_chunks = per_worker // chunk
    mesh = plsc.VectorSubcoreMesh(core_axis_name="c", subcore_axis_name="s")

    @functools.partial(
        pl.kernel, mesh=mesh, out_type=jax.ShapeDtypeStruct((n_out, width), src.dtype),
        scratch_types=[pltpu.VMEM((chunk,), i32), pltpu.VMEM((chunk,), i32),
                       pltpu.VMEM((chunk, width), src.dtype), pltpu.VMEM((chunk, width), src.dtype),
                       pltpu.SemaphoreType.DMA, pltpu.SemaphoreType.DMA, pltpu.SemaphoreType.DMA,
                       pltpu.SemaphoreType.DMA])
    def move(src_hbm, idx_hbm, out_hbm, idx0, idx1, rows0, rows1, in0, in1, out0, out1):
        wid = jax.lax.axis_index("s") * SC_CORES + jax.lax.axis_index("c")
        base = wid * per_worker
        idx_v, rows_v, sem_in, sem_out = (idx0, idx1), (rows0, rows1), (in0, in1), (out0, out1)

        def fill(j):
            b = j % 2
            rows = pl.ds(base + j * chunk, chunk)
            pltpu.sync_copy(idx_hbm.at[rows], idx_v[b])
            src_rows = src_hbm.at[rows] if scatter else src_hbm.at[idx_v[b]]
            return pltpu.async_copy(src_rows, rows_v[b], sem_in[b])

        def drain(j):
            b = j % 2
            dst_rows = out_hbm.at[idx_v[b]] if scatter else out_hbm.at[pl.ds(base + j * chunk, chunk)]
            return pltpu.async_copy(rows_v[b], dst_rows, sem_out[b])

        fills, drains = {0: fill(0)}, {}
        for j in range(n_chunks):
            if j + 1 < n_chunks:
                if j >= 1:
                    drains[j - 1].wait()
                fills[j + 1] = fill(j + 1)
            fills[j].wait()
            drains[j] = drain(j)
        for j in range(max(n_chunks - 2, 0), n_chunks):
            drains[j].wait()

    return move(src, idx)


def _expert_kernel(lo_ref, hi_ref, valid_ref, nt_ref, xs_ref, wg_ref, wu_ref, wd_ref, ys_ref, wg_scr, wu_scr, wd_scr):
    k = pl.program_id(0)
    prev = jnp.maximum(k - 1, 0)

    @pl.when((k == 0) | (lo_ref[k] // EXPERTS_PER_GROUP != lo_ref[prev] // EXPERTS_PER_GROUP))
    def _():
        for j in range(EXPERTS_PER_GROUP):
            wg_scr[j] = wg_ref[0, 0, j].astype(bf16)
            wu_scr[j] = wu_ref[0, 0, j].astype(bf16)
            wd_scr[j] = wd_ref[0, 0, j].astype(bf16)

    @pl.when(k < nt_ref[0])
    def _():
        live = jax.lax.broadcasted_iota(i32, (EXPERT_TILE, 1), 0) < valid_ref[k]
        words = jnp.where(live, xs_ref[:, 0:HX_HALF], 0)
        x = jnp.concatenate([pltpu.bitcast(words & -65536, f32), pltpu.bitcast(words << 16, f32)], axis=1).astype(bf16)
        gates = pltpu.bitcast(jnp.where(live, xs_ref[:, HX_HALF:HX_COLS], 0), f32)
        y = None
        for lane, e_ref in enumerate((lo_ref, hi_ref)):
            j = e_ref[k] % EXPERTS_PER_GROUP
            hid = _silu(_dot(x, wg_scr[j])) * _dot(x, wu_scr[j]) * gates[:, lane:lane + 1]
            part = _dot(hid.astype(bf16), wd_scr[j])
            y = part if y is None else y + part
        ys_ref[...] = y


def _experts(layer, xs, e_lo, e_hi, valid, n_tiles, w):
    max_tiles = e_lo.shape[0]
    row = lambda k, lo, hi, valid, nt: (jnp.minimum(k, nt[0] - 1), 0)
    group = lambda k, lo, hi, valid, nt: (layer, lo[k] // EXPERTS_PER_GROUP, 0, 0, 0)
    by_group = lambda a: a.reshape(DEPTH, N_GROUPS, EXPERTS_PER_GROUP, *a.shape[2:])
    return pl.pallas_call(
        _expert_kernel,
        out_shape=jax.ShapeDtypeStruct((max_tiles * EXPERT_TILE, D_MODEL), f32),
        grid_spec=pltpu.PrefetchScalarGridSpec(
            num_scalar_prefetch=4, grid=(max_tiles,),
            in_specs=[pl.BlockSpec((EXPERT_TILE, HX_COLS), row),
                      pl.BlockSpec((1, 1, EXPERTS_PER_GROUP, D_MODEL, D_FF_EXPERT), group),
                      pl.BlockSpec((1, 1, EXPERTS_PER_GROUP, D_MODEL, D_FF_EXPERT), group),
                      pl.BlockSpec((1, 1, EXPERTS_PER_GROUP, D_FF_EXPERT, D_MODEL), group)],
            out_specs=pl.BlockSpec((EXPERT_TILE, D_MODEL), row),
            scratch_shapes=[pltpu.VMEM((EXPERTS_PER_GROUP, D_MODEL, D_FF_EXPERT), bf16),
                            pltpu.VMEM((EXPERTS_PER_GROUP, D_MODEL, D_FF_EXPERT), bf16),
                            pltpu.VMEM((EXPERTS_PER_GROUP, D_FF_EXPERT, D_MODEL), bf16)]),
        compiler_params=_params(("arbitrary",)),
        name="experts",
    )(e_lo, e_hi, valid, n_tiles, xs, by_group(w["w_gate"]), by_group(w["w_up"]), by_group(w["w_down"]))


def _final_kernel(x1_ref, y_ref, mod_ref, fg_ref, o_ref):
    o_ref[...] = _rms(x1_ref[...] + mod_ref[0, 0, 5:6, :] * y_ref[...], D_MODEL) * fg_ref[...]


def _final(x1, y, n, row0, mod, mod_row, w, tile):
    off = row0 // tile
    src_row = lambda t: (off + t, 0)
    return pl.pallas_call(
        _final_kernel,
        out_shape=jax.ShapeDtypeStruct((n, D_MODEL), f32),
        grid=(n // tile,),
        in_specs=[pl.BlockSpec((tile, D_MODEL), src_row), pl.BlockSpec((tile, D_MODEL), lambda t: (t, 0)),
                  pl.BlockSpec((1, 1, N_MOD, D_MODEL), lambda t: (DEPTH - 1, mod_row(t * tile), 0, 0)),
                  pl.BlockSpec((1, D_MODEL), lambda t: (0, 0))],
        out_specs=pl.BlockSpec((tile, D_MODEL), lambda t: (t, 0)),
        compiler_params=_params(("arbitrary",)),
        name="final_norm",
    )(x1, y, mod, w["final_g"])


def _rope_tables(n_tokens):
    pos = np.arange(n_tokens)
    row = (pos // GRID_W).astype(np.float64)
    col = (pos % GRID_W).astype(np.float64)

    def cs(rot_dim):
        quarter = rot_dim // 4
        inv = ROPE_THETA ** (-np.arange(quarter, dtype=np.float64) / quarter)
        ang = np.concatenate([row[:, None] * inv, col[:, None] * inv], axis=-1)
        return np.cos(ang), np.sin(ang)

    c32, s32 = cs(MLA_ROPE)
    c64, s64 = cs(HEAD_DIM)
    ones = np.ones((n_tokens, MLA_NOPE))
    zeros = np.zeros((n_tokens, MLA_NOPE))

    def rep(parts):
        period = np.concatenate(parts, axis=-1)
        return jnp.asarray(np.tile(period, (1, LANES // period.shape[-1])), f32)

    return (rep([ones, c32, c32]), rep([zeros, -s32, s32]), rep([c32, c32]), rep([-s32, s32]),
            rep([c64, c64]), rep([-s64, s64]))


def _layout_weights(norm1_g, norm2_g, w_in, mla_kv_norm_g, mla_w_uk, mla_w_uv, gqa_q_norm_g, gqa_k_norm_g,
                    diff_lambda, diff_norm_g, w_out, moe_w_group, moe_b_group, moe_w_router, moe_b_router,
                    moe_w_gate, moe_w_up, moe_w_down, final_norm_g):
    eye = jnp.eye(MLA_ROPE, dtype=f32)
    top = jnp.concatenate([mla_w_uk, jnp.zeros((DEPTH, KV_RANK, MLA_HEADS, MLA_ROPE), f32)], axis=-1)
    mid = jnp.concatenate([jnp.zeros((MLA_ROPE, MLA_HEADS, MLA_NOPE), f32),
                           jnp.broadcast_to(eye[:, None, :], (MLA_ROPE, MLA_HEADS, MLA_ROPE))], axis=-1)
    w_ka = jnp.concatenate([top.reshape(DEPTH, KV_RANK, 384),
                            jnp.broadcast_to(mid.reshape(1, MLA_ROPE, 384), (DEPTH, MLA_ROPE, 384)),
                            jnp.zeros((DEPTH, 256 - KV_RANK - MLA_ROPE, 384), f32)], axis=1).astype(bf16)
    seg_id = np.arange(512) // HEAD_DIM
    seg = jnp.asarray(seg_id[:, None] == seg_id[None, :], bf16)
    qk_g = jnp.concatenate([jnp.tile(gqa_q_norm_g, (1, GQA_HEADS)), jnp.tile(gqa_k_norm_g, (1, GQA_KV_HEADS))], axis=-1)
    return dict(
        g1=norm1_g.reshape(DEPTH, 1, D_MODEL), g2=norm2_g.reshape(DEPTH, 1, D_MODEL),
        w_in=jnp.swapaxes(w_in, 1, 2),
        kv_g=mla_kv_norm_g.reshape(DEPTH, 1, KV_RANK), qk_g=qk_g.reshape(DEPTH, 1, 512), seg=seg, w_ka=w_ka,
        w_uv=mla_w_uv.reshape(DEPTH, KV_RANK, 384), lam=diff_lambda, diff_g=diff_norm_g.reshape(DEPTH, 1, DIFF_V),
        w_out=w_out, w_grp=moe_w_group, b_grp=moe_b_group.reshape(DEPTH, 1, N_GROUPS), w_rtr=moe_w_router,
        b_rtr=moe_b_router.reshape(DEPTH, 1, N_EXPERTS), w_gate=moe_w_gate, w_up=moe_w_up, w_down=moe_w_down,
        final_g=final_norm_g.reshape(1, D_MODEL))


PRE_TILE = 512
CTX_REQUESTS_PER_STEP = 2
LAT_ATTN_TILE = 256
LAT_HEADS_PER_ROUND = 10
MXU_SUM_MIN_KEYS = 1024
POST_TILE = 512
FINAL_TILE = 1024


def kernel(x_prompt, x_sample, c, cache_mla_ckv, cache_mla_krope, cache_gqa_k, cache_gqa_v, cache_diff_k, cache_diff_v, c_ctx, norm1_g, norm2_g, w_mod, b_mod, w_in, mla_kv_norm_g, mla_w_uk, mla_w_uv, gqa_q_norm_g, gqa_k_norm_g, diff_lambda, diff_norm_g, w_out, moe_w_group, moe_b_group, moe_w_router, moe_b_router, moe_w_gate, moe_w_up, moe_w_down, final_norm_g):
    B, S, _ = x_prompt.shape
    Bl, Sl, _ = x_sample.shape
    n_ctx, n_lat = B * S, Bl * Sl
    total = n_ctx + n_lat
    assert Bl + 1 <= MOD_ROWS and DEPTH == 2 and total % SC_ROWS == 0
    slot_rows = -(-(total + N_CLASSES * EXPERT_TILE) // SC_ROWS) * SC_ROWS
    max_tiles = slot_rows // EXPERT_TILE
    w = _layout_weights(norm1_g, norm2_g, w_in, mla_kv_norm_g, mla_w_uk, mla_w_uv, gqa_q_norm_g, gqa_k_norm_g,
                        diff_lambda, diff_norm_g, w_out, moe_w_group, moe_b_group, moe_w_router, moe_b_router,
                        moe_w_gate, moe_w_up, moe_w_down, final_norm_g)
    cond = jnp.concatenate([c_ctx[None, :], c, jnp.zeros((MOD_ROWS - 1 - Bl, D_MODEL), f32)], axis=0)
    mod = _modulation(cond, w_mod, b_mod).reshape(DEPTH, MOD_ROWS, N_MOD, D_MODEL)
    ctx_row = lambda token: 0
    lat_row = lambda token: 1 + token // Sl
    tabs = _rope_tables(Sl)
    kv_past = _cache_rows((cache_mla_ckv, cache_mla_krope, cache_gqa_k, cache_gqa_v, cache_diff_k, cache_diff_v), w)
    per_b = Sl // LAT_ATTN_TILE

    x_ctx, x_lat = x_prompt.reshape(n_ctx, D_MODEL), x_sample.reshape(n_lat, D_MODEL)
    cache = ()
    x1 = y_c = y_l = None
    for i in range(DEPTH):
        if i == 0:
            *cache, x1_c, hx_c, cls_c = _ctx_layer(i, x_ctx, n_ctx, S, total, mod, w)
        else:
            *cache, x1_c, hx_c, cls_c = _ctx_layer(i, x1, n_ctx, S, total, mod, w, prev_cache=cache, resid=y_c)
        if i == 0:
            q_l, kv_l = _pre_latent(i, x_lat, n_lat, 0, Sl, mod, lat_row, w, PRE_TILE, tabs)
        else:
            q_l, kv_l, x_lat = _pre_latent(i, x1, n_lat, n_ctx, Sl, mod, lat_row, w, PRE_TILE, tabs, resid=y_l)
        past = (kv_past, PAST_LEN, lambda t, i=i: (i, t // per_b, 0, 0))
        own = (kv_l.reshape(1, Bl, Sl, KV_COLS), Sl, lambda t: (0, t // per_b, 0, 0))
        o_l = _attention(i, q_l, [past, own], w, LAT_ATTN_TILE, LAT_HEADS_PER_ROUND)
        x1, hx, cls = _post(i, o_l, x_lat, n_ctx, total, mod, lat_row, w, POST_TILE, merged=(x1_c, hx_c, cls_c))
        if i == DEPTH - 1:
            outs = (cache[0], jnp.swapaxes(cache[1], 2, 3), cache[2].reshape(B, DEPTH, S, GQA_KV_HEADS, HEAD_DIM),
                    cache[3].reshape(B, DEPTH, S, GQA_KV_HEADS, HEAD_DIM),
                    cache[4].reshape(B, DEPTH, S, DIFF_HEADS, 2, DIFF_QK),
                    cache[5].reshape(B, DEPTH, S, DIFF_HEADS, DIFF_V))
            cls, outs = jax.lax.optimization_barrier((cls, outs))
        slot, e_lo, e_hi, valid, n_tiles = _plan(cls, max_tiles)
        xs = _move_rows(hx, slot, slot_rows, scatter=True)
        ys = _experts(i, xs, e_lo, e_hi, valid, n_tiles, w)
        if i < DEPTH - 1:
            y_c = _move_rows(ys, slot[:n_ctx], n_ctx, scatter=False)
            y_l = _move_rows(ys, slot[n_ctx:], n_lat, scatter=False)
        else:
            y_l = _move_rows(ys, slot[n_ctx:], n_lat, scatter=False)
            y_c = _move_rows(ys, slot[:n_ctx], n_ctx, scatter=False)

    y_sample = _final(x1, y_l, n_lat, n_ctx, mod, lat_row, w, FINAL_TILE).reshape(Bl, Sl, D_MODEL)
    y_prompt = _final(x1, y_c, n_ctx, 0, mod, ctx_row, w, FINAL_TILE).reshape(B, S, D_MODEL)
    return (y_prompt, y_sample, *outs)
```

```python
import functools
import math

import jax
import jax.numpy as jnp
import numpy as np
from jax.experimental import pallas as pl
from jax.experimental.pallas import tpu as pltpu
from jax.experimental.pallas import tpu_sc as plsc

D_MODEL = 1024
DEPTH = 2
PAST_LEN = 512
GRID_W = 64
ROPE_THETA = 10000.0
EPS = 1e-6
LOG2E = 1.4426950408889634
N_MOD = 6
HEAD_DIM = 64
MLA_HEADS = 6
MLA_NOPE = 32
MLA_ROPE = 32
MLA_V = 64
KV_RANK = 128
GQA_HEADS = 6
GQA_KV_HEADS = 2
GQA_GROUP = GQA_HEADS // GQA_KV_HEADS
DIFF_HEADS = 4
DIFF_QK = 32
DIFF_V = 64
N_GROUPS = 4
EXPERTS_PER_GROUP = 4
N_EXPERTS = N_GROUPS * EXPERTS_PER_GROUP
D_FF_EXPERT = 256

LANES = 128
MOD_ROWS = 8

IN_COLS = 1952
IN_KR = 512
Z_QA, Z_CKV, Z_QG, Z_KG, Z_VG, Z_QD, Z_KD, Z_VD, Z_KR = 0, 384, 512, 896, 1024, 1152, 1408, 1664, 1920
Z_COLS = 2048
Q_A, Q_G, Q_D, Q_COLS = 0, 384, 768, 1024
KV_KA, KV_VA, KV_KG, KV_VG, KV_KD, KV_VD, KV_COLS = 0, 384, 768, 896, 1024, 1280, 1536
CACHE_WIDTHS = (128, 32, 128, 128, 256, 256)

PAIR_LO = (0, 0, 0, 1, 1, 2)
PAIR_HI = (1, 2, 3, 3, 2, 3)
N_PAIRS = len(PAIR_LO)
N_CLASSES = N_GROUPS * N_PAIRS
HX_HALF = D_MODEL // 2
HX_COLS = HX_HALF + LANES
EXPERT_TILE = 256

SC_CORES, SC_SUBCORES = 2, 16
SC_BUFFER_BYTES = 400 * 1024
SC_ROWS = SC_CORES * SC_SUBCORES * 8

VMEM_LIMIT = 56 * 1024 * 1024

bf16 = jnp.bfloat16
f32 = jnp.float32
i32 = jnp.int32


def _dot(a, b):
    return jnp.dot(a, b, preferred_element_type=f32)


def _dot_nt(a, b):
    return jax.lax.dot_general(a, b, (((1,), (1,)), ((), ())), preferred_element_type=f32)


def _rms(x, width):
    return x * jax.lax.rsqrt(jnp.sum(x * x, axis=-1, keepdims=True) * (1.0 / width) + EPS)


def _silu(x):
    return x * (1.0 / (1.0 + jnp.exp(-x)))


def _params(sem):
    return pltpu.CompilerParams(dimension_semantics=sem, vmem_limit_bytes=VMEM_LIMIT)


def _mod_kernel(cond_ref, w_ref, b_ref, o_ref):
    o_ref[0] = _dot(_silu(cond_ref[...]).astype(bf16), w_ref[0].astype(bf16)) + b_ref[0]


def _modulation(cond, w_mod, b_mod):
    return pl.pallas_call(
        _mod_kernel,
        out_shape=jax.ShapeDtypeStruct((DEPTH, MOD_ROWS, N_MOD * D_MODEL), f32),
        grid=(DEPTH, N_MOD),
        in_specs=[
            pl.BlockSpec((MOD_ROWS, D_MODEL), lambda i, j: (0, 0)),
            pl.BlockSpec((1, D_MODEL, D_MODEL), lambda i, j: (i, 0, j)),
            pl.BlockSpec((1, 1, D_MODEL), lambda i, j: (i, 0, j)),
        ],
        out_specs=pl.BlockSpec((1, MOD_ROWS, D_MODEL), lambda i, j: (i, 0, j)),
        compiler_params=_params(("arbitrary", "arbitrary")),
        name="modulation",
    )(cond, w_mod, b_mod.reshape(DEPTH, 1, N_MOD * D_MODEL))


def _swap_halves(x, half):
    lane = jax.lax.broadcasted_iota(i32, x.shape, 1)
    fwd = pltpu.roll(x, LANES - half, 1)
    bwd = pltpu.roll(x, half, 1)
    return jnp.where((lane & (2 * half - 1)) < half, fwd, bwd)


def _rope_block(x, cos, sin, half):
    return x * cos + _swap_halves(x, half) * sin


def _pre_kernel(rope, n_prev, resid, *refs):
    it = iter(refs)
    x_ref, mod_ref, g1_ref, w_in_ref, kvg_ref, qkg_ref, seg_ref, wka_ref, wuv_ref = (next(it) for _ in range(9))
    if resid:
        y_ref, pmod_ref = next(it), next(it)
    if rope:
        ca_ref, sa_ref, c32_ref, s32_ref, c64_ref, s64_ref = (next(it) for _ in range(6))
    prev_refs = [next(it) for _ in range(n_prev)]
    q_ref, kv_ref = next(it), next(it)
    if resid:
        x2_ref = next(it)
    cache_refs = [] if rope else [next(it) for _ in range(len(CACHE_WIDTHS))]
    w_scr = next(it)

    @pl.when(pl.program_id(0) == 0)
    def _():
        w_scr[0:IN_KR] = w_in_ref[0, 0:IN_KR].astype(bf16)
        w_scr[IN_KR:Z_KR] = w_in_ref[0, IN_KR + MLA_ROPE:IN_COLS].astype(bf16)
        w_scr[Z_KR:Z_KR + MLA_ROPE] = w_in_ref[0, IN_KR:IN_KR + MLA_ROPE].astype(bf16)
        w_scr[Z_KR + MLA_ROPE:Z_COLS] = jnp.zeros((Z_COLS - Z_KR - MLA_ROPE, D_MODEL), bf16)

    x = x_ref[...]
    if resid:
        x = x + pmod_ref[0, 0, 5:6, :] * y_ref[...]
        x2_ref[...] = x
    shift1 = mod_ref[0, 0, 0:1, :]
    scale1 = mod_ref[0, 0, 1:2, :]
    h = (_rms(x, D_MODEL) * g1_ref[0]) * (1.0 + scale1) + shift1
    z = _dot_nt(h.astype(bf16), w_scr[...])

    ckv = _rms(z[:, Z_CKV:Z_CKV + KV_RANK], KV_RANK) * kvg_ref[0]

    qk = z[:, Z_QG:Z_VG]
    sq = qk * qk
    sq_hi = sq.astype(bf16)
    sq_lo = (sq - sq_hi.astype(f32)).astype(bf16)
    seg = seg_ref[...]
    ms = (_dot(sq_hi, seg) + _dot(sq_lo, seg)) * (1.0 / HEAD_DIM)
    qk = qk * jax.lax.rsqrt(ms + EPS) * qkg_ref[0]

    def blocks(arr, n):
        return [arr[:, LANES * j:LANES * (j + 1)] for j in range(n)]

    qa = blocks(z[:, Z_QA:Z_QA + 384], 3)
    qkb = blocks(qk, 4)
    qd = blocks(z[:, Z_QD:Z_QD + 256], 2)
    kd = blocks(z[:, Z_KD:Z_KD + 256], 2)
    kr = z[:, Z_KR:Z_KR + LANES]
    if rope:
        ca, sa, c32, s32, c64, s64 = (r[...] for r in (ca_ref, sa_ref, c32_ref, s32_ref, c64_ref, s64_ref))
        qa = [_rope_block(b, ca, sa, MLA_ROPE // 2) for b in qa]
        qkb = [_rope_block(b, c64, s64, HEAD_DIM // 2) for b in qkb]
        qd = [_rope_block(b, c32, s32, DIFF_QK // 2) for b in qd]
        kd = [_rope_block(b, c32, s32, DIFF_QK // 2) for b in kd]
        kr = _rope_block(kr, c32, s32, MLA_ROPE // 2)

    vg = z[:, Z_VG:Z_VG + 128]
    vd = z[:, Z_VD:Z_VD + 256]
    ckv_b = ckv.astype(bf16)
    k_a = _dot(jnp.concatenate([ckv_b, kr.astype(bf16)], axis=1), wka_ref[0])
    v_a = _dot(ckv_b, wuv_ref[0].astype(bf16))

    for j in range(3):
        q_ref[:, Q_A + LANES * j:Q_A + LANES * (j + 1)] = (qa[j] * (HEAD_DIM ** -0.5 * LOG2E)).astype(bf16)
        q_ref[:, Q_G + LANES * j:Q_G + LANES * (j + 1)] = (qkb[j] * (HEAD_DIM ** -0.5 * LOG2E)).astype(bf16)
    for j in range(2):
        q_ref[:, Q_D + LANES * j:Q_D + LANES * (j + 1)] = (qd[j] * (DIFF_QK ** -0.5 * LOG2E)).astype(bf16)
        kv_ref[:, KV_KD + LANES * j:KV_KD + LANES * (j + 1)] = kd[j].astype(bf16)
    kv_ref[:, KV_KA:KV_KA + 384] = k_a.astype(bf16)
    kv_ref[:, KV_VA:KV_VA + 384] = v_a.astype(bf16)
    kv_ref[:, KV_KG:KV_KG + 128] = qkb[3].astype(bf16)
    kv_ref[:, KV_VG:KV_VG + 128] = vg.astype(bf16)
    kv_ref[:, KV_VD:KV_VD + 256] = vd.astype(bf16)
    if not rope:
        rows = [ckv, None, qkb[3], vg, jnp.concatenate(kd, axis=1), vd]
        for out, new in zip(cache_refs, rows):
            if new is None:
                seq = out.shape[3]
                for r in range(out.shape[0]):
                    out[r, 0] = kr[r * seq:(r + 1) * seq].T[:MLA_ROPE, :]
            else:
                reqs, _, seq, width = out.shape
                out[:, 0] = new.reshape(reqs, seq, width)


def _pre_latent(layer, x, n, row0, seq, mod, mod_row, w, tile, rope_tabs, resid=None):
    lay = lambda t: (layer, 0, 0)
    row = lambda t: (t, 0)
    off = row0 // tile
    src_row = lambda t: (off + t, 0)
    in_specs = [
        pl.BlockSpec((tile, D_MODEL), src_row),
        pl.BlockSpec((1, 1, N_MOD, D_MODEL), lambda t: (layer, mod_row(t * tile), 0, 0)),
        pl.BlockSpec((1, 1, D_MODEL), lay),
        pl.BlockSpec((1, IN_COLS, D_MODEL), lay),
        pl.BlockSpec((1, 1, KV_RANK), lay),
        pl.BlockSpec((1, 1, 512), lay),
        pl.BlockSpec((512, 512), lambda t: (0, 0)),
        pl.BlockSpec((1, 256, 384), lay),
        pl.BlockSpec((1, KV_RANK, 384), lay),
    ]
    args = [x, mod, w["g1"], w["w_in"], w["kv_g"], w["qk_g"], w["seg"], w["w_ka"], w["w_uv"]]
    if resid is not None:
        in_specs += [pl.BlockSpec((tile, D_MODEL), src_row),
                     pl.BlockSpec((1, 1, N_MOD, D_MODEL), lambda t: (layer - 1, mod_row(t * tile), 0, 0))]
        args += [resid, mod]
    per_b = seq // tile
    in_specs += [pl.BlockSpec((tile, LANES), lambda t: (t % per_b, 0))] * 6
    args += list(rope_tabs)
    out_shape = [jax.ShapeDtypeStruct((n, Q_COLS), bf16), jax.ShapeDtypeStruct((n, KV_COLS), bf16)]
    out_specs = [pl.BlockSpec((tile, Q_COLS), row), pl.BlockSpec((tile, KV_COLS), row)]
    if resid is not None:
        out_shape.append(jax.ShapeDtypeStruct((n, D_MODEL), f32))
        out_specs.append(pl.BlockSpec((tile, D_MODEL), row))
    return pl.pallas_call(
        functools.partial(_pre_kernel, True, 0, resid is not None),
        out_shape=out_shape,
        grid=(n // tile,),
        in_specs=in_specs,
        out_specs=out_specs,
        scratch_shapes=[pltpu.VMEM((Z_COLS, D_MODEL), bf16)],
        compiler_params=_params(("arbitrary",)),
        name="pre_latent",
    )(*args)


PAST_CKV, PAST_KG, PAST_VG, PAST_KD, PAST_VD, PAST_KR, PAST_COLS = 0, 128, 256, 384, 640, 896, 928


def _cache_kernel(past_ref, wka_ref, wuv_ref, kv_ref):
    ckv_b = past_ref[0, 0, :, PAST_CKV:PAST_CKV + KV_RANK].astype(bf16)
    kr_b = past_ref[0, 0, :, PAST_KR:PAST_KR + MLA_ROPE].astype(bf16)
    wka = wka_ref[0]
    k_a = _dot(ckv_b, wka[:KV_RANK]) + _dot(kr_b, wka[KV_RANK:KV_RANK + MLA_ROPE])
    kv_ref[0, 0, :, KV_KA:KV_KA + 384] = k_a.astype(bf16)
    kv_ref[0, 0, :, KV_VA:KV_VA + 384] = _dot(ckv_b, wuv_ref[0].astype(bf16)).astype(bf16)
    kv_ref[0, 0, :, KV_KG:KV_KG + 128] = past_ref[0, 0, :, PAST_KG:PAST_KG + 128].astype(bf16)
    kv_ref[0, 0, :, KV_VG:KV_VG + 128] = past_ref[0, 0, :, PAST_VG:PAST_VG + 128].astype(bf16)
    kv_ref[0, 0, :, KV_KD:KV_KD + 256] = past_ref[0, 0, :, PAST_KD:PAST_KD + 256].astype(bf16)
    kv_ref[0, 0, :, KV_VD:KV_VD + 256] = past_ref[0, 0, :, PAST_VD:PAST_VD + 256].astype(bf16)


def _cache_rows(caches, w):
    ckv, kr, kg, vg, kd, vd = caches
    B = ckv.shape[0]
    flat = lambda a: a.reshape(B, DEPTH, PAST_LEN, -1)
    past = jnp.concatenate([flat(ckv), flat(kg), flat(vg), flat(kd), flat(vd), flat(kr)], axis=-1)
    return pl.pallas_call(
        _cache_kernel,
        out_shape=jax.ShapeDtypeStruct((DEPTH, B, PAST_LEN, KV_COLS), bf16),
        grid=(DEPTH, B),
        in_specs=[pl.BlockSpec((1, 1, PAST_LEN, PAST_COLS), lambda i, b: (b, i, 0, 0)),
                  pl.BlockSpec((1, 256, 384), lambda i, b: (i, 0, 0)), pl.BlockSpec((1, KV_RANK, 384), lambda i, b: (i, 0, 0))],
        out_specs=pl.BlockSpec((1, 1, PAST_LEN, KV_COLS), lambda i, b: (i, b, 0, 0)),
        compiler_params=_params(("arbitrary", "arbitrary")),
        name="cache_rows",
    )(past, w["w_ka"], w["w_uv"])


_SCORE_HEADS = (
    [(Q_A + 64 * h, KV_KA + 64 * h, 64, KV_VA + MLA_V * h) for h in range(MLA_HEADS)]
    + [(Q_G + 64 * h, KV_KG + 64 * (h // GQA_GROUP), 64, KV_VG + 64 * (h // GQA_GROUP)) for h in range(GQA_HEADS)]
    + [(Q_D + 64 * h + DIFF_QK * c, KV_KD + 64 * h + DIFF_QK * c, DIFF_QK, KV_VD + DIFF_V * h)
       for h in range(DIFF_HEADS) for c in range(2)])


def _attn_kernel(lam_init, per_round, n_src, q_ref, *refs):
    kv_refs = refs[:n_src]
    lam_ref, dg_ref, o_ref, s_ref, p_ref = refs[n_src:]
    spans, start = [], 0
    for r in kv_refs:
        spans.append((r, start, r.shape[2]))
        start += r.shape[2]
    mxu_sum = start >= MXU_SUM_MIN_KEYS

    outs = []
    for first in range(0, len(_SCORE_HEADS), per_round):
        chunk = _SCORE_HEADS[first:first + per_round]
        for j, (q_off, k_off, width, _) in enumerate(chunk):
            for r, lo, size in spans:
                s_ref[j, :, lo:lo + size] = _dot_nt(q_ref[:, q_off:q_off + width], r[0, 0, :, k_off:k_off + width])
        s = s_ref[...]
        p = jnp.exp2(s - jnp.max(s, axis=-1, keepdims=True))
        if mxu_sum:
            p_ref[...] = p.astype(bf16)
            for j, (_, _, _, v_off) in enumerate(chunk):
                o = sum(_dot(p_ref[j, :, lo:lo + size],
                             jnp.concatenate([r[0, 0, :, v_off:v_off + DIFF_V],
                                              jnp.ones((size, LANES - DIFF_V), bf16)], axis=1))
                        for r, lo, size in spans)
                outs.append((o * pltpu.roll(1.0 / o, DIFF_V, 1))[:, :DIFF_V])
        else:
            inv = 1.0 / jnp.sum(p, axis=-1, keepdims=True)
            p_ref[...] = p.astype(bf16)
            for j, (_, _, _, v_off) in enumerate(chunk):
                o = sum(_dot(p_ref[j, :, lo:lo + size], r[0, 0, :, v_off:v_off + DIFF_V]) for r, lo, size in spans)
                outs.append(o * inv[j])

    lp = lam_ref[0]
    e1 = jnp.exp(jnp.sum(lp[0:1] * lp[1:2], axis=-1, keepdims=True))
    e2 = jnp.exp(jnp.sum(lp[2:3] * lp[3:4], axis=-1, keepdims=True))
    lam = e1 - e2 + lam_init
    heads = outs[:MLA_HEADS + GQA_HEADS]
    for h in range(DIFF_HEADS):
        o1, o2 = outs[MLA_HEADS + GQA_HEADS + 2 * h:MLA_HEADS + GQA_HEADS + 2 * h + 2]
        heads.append(_rms(o1 - lam * o2, DIFF_V) * dg_ref[0] * (1.0 - lam_init))
    for j in range(len(heads) // 2):
        o_ref[:, LANES * j:LANES * (j + 1)] = jnp.concatenate(heads[2 * j:2 * j + 2], axis=1).astype(bf16)


def _attention(layer, q, sources, w, tile, per_round):
    n = q.shape[0]
    lam_init = 0.8 - 0.6 * math.exp(-0.3 * layer)
    s_kv = sum(rows for _, rows, _ in sources)
    assert len(_SCORE_HEADS) % per_round == 0
    return pl.pallas_call(
        functools.partial(_attn_kernel, lam_init, per_round, len(sources)),
        out_shape=jax.ShapeDtypeStruct((n, D_MODEL), bf16),
        scratch_shapes=[pltpu.VMEM((per_round, tile, s_kv), f32), pltpu.VMEM((per_round, tile, s_kv), bf16)],
        grid=(n // tile,),
        in_specs=[pl.BlockSpec((tile, Q_COLS), lambda t: (t, 0))]
        + [pl.BlockSpec((1, 1, rows, KV_COLS), index) for _, rows, index in sources]
        + [pl.BlockSpec((1, 4, DIFF_QK), lambda t: (layer, 0, 0)), pl.BlockSpec((1, 1, DIFF_V), lambda t: (layer, 0, 0))],
        out_specs=pl.BlockSpec((tile, D_MODEL), lambda t: (t, 0)),
        compiler_params=_params(("arbitrary",)),
        name="attention",
    )(q, *[arr for arr, _, _ in sources], w["lam"], w["diff_g"])


def _post_kernel(merge, *refs):
    it = iter(refs)
    o_ref, x_ref, mod_ref, w_out_ref, g2_ref, wg_ref, bg_ref, we_ref, be_ref = (next(it) for _ in range(9))
    if merge:
        next(it), next(it), next(it)
    x1_ref, hx_ref, cls_ref, w_scr = (next(it) for _ in range(4))

    @pl.when(pl.program_id(0) == 0)
    def _():
        w_scr[...] = w_out_ref[0].astype(bf16)

    gate1 = mod_ref[0, 0, 2:3, :]
    shift2 = mod_ref[0, 0, 3:4, :]
    scale2 = mod_ref[0, 0, 4:5, :]
    x1 = x_ref[...] + gate1 * _dot(o_ref[...], w_scr[...])
    x1_ref[...] = x1
    h2 = ((_rms(x1, D_MODEL) * g2_ref[0]) * (1.0 + scale2) + shift2).astype(bf16)
    bits = pltpu.bitcast(h2.astype(f32), i32)
    hx_ref[:, 0:HX_HALF] = bits[:, 0:HX_HALF] | jax.lax.shift_right_logical(bits[:, HX_HALF:D_MODEL], 16)

    def first_lane(mask, lane_f):
        return jnp.min(jnp.where(mask, lane_f, float(LANES)), axis=-1, keepdims=True)

    gl = _dot(h2, wg_ref[0].astype(bf16)) + bg_ref[0]
    glane = jax.lax.broadcasted_iota(i32, gl.shape, 1).astype(f32)
    ge = jnp.exp(gl - jnp.max(gl, axis=-1, keepdims=True))
    gprob = ge / jnp.sum(ge, axis=-1, keepdims=True)
    g_top = jnp.max(gprob, axis=-1, keepdims=True)
    g_idx = first_lane(gprob == g_top, glane)

    el = _dot(h2, we_ref[0].astype(bf16)) + be_ref[0]
    lane = jax.lax.broadcasted_iota(i32, el.shape, 1)
    lane_f = lane.astype(f32)
    emask = (lane >> 2).astype(f32) == g_idx
    em = jnp.where(emask, el, -jnp.inf)
    ee = jnp.where(emask, jnp.exp(em - jnp.max(em, axis=-1, keepdims=True)), 0.0)
    ep = ee / jnp.sum(ee, axis=-1, keepdims=True)
    p1 = jnp.max(jnp.where(emask, ep, -1.0), axis=-1, keepdims=True)
    i1 = first_lane(emask & (ep == p1), lane_f)
    rest = emask & (lane_f != i1)
    p2 = jnp.max(jnp.where(rest, ep, -1.0), axis=-1, keepdims=True)
    i2 = first_lane(rest & (ep == p2), lane_f)
    tot = p1 + p2
    w1 = g_top * (p1 / tot)
    w2 = g_top * (p2 / tot)

    lo = jnp.minimum(i1, i2) - EXPERTS_PER_GROUP * g_idx
    hi = jnp.maximum(i1, i2) - EXPERTS_PER_GROUP * g_idx
    pair = jnp.where(lo == 0.0, hi - 1.0, jnp.where(lo == 1.0, jnp.where(hi == 3.0, 3.0, 4.0), 5.0))
    cls_ref[...] = (N_PAIRS * g_idx + pair).astype(i32)
    g_lo = jnp.where(i1 < i2, w1, w2)
    g_hi = jnp.where(i1 < i2, w2, w1)
    tail_lane = jax.lax.broadcasted_iota(i32, (h2.shape[0], LANES), 1)
    hx_ref[:, HX_HALF:HX_COLS] = pltpu.bitcast(
        jnp.where(tail_lane == 0, g_lo, jnp.where(tail_lane == 1, g_hi, 0.0)), i32)


def _post(layer, o, x, row0, total, mod, mod_row, w, tile, merged=None):
    n = o.shape[0]
    lay = lambda t: (layer, 0, 0)
    row = lambda t: (t, 0)
    off = row0 // tile
    out_row = lambda t: (off + t, 0)
    in_specs = [
        pl.BlockSpec((tile, D_MODEL), row),
        pl.BlockSpec((tile, D_MODEL), row),
        pl.BlockSpec((1, 1, N_MOD, D_MODEL), lambda t: (layer, mod_row(t * tile), 0, 0)),
        pl.BlockSpec((1, D_MODEL, D_MODEL), lay),
        pl.BlockSpec((1, 1, D_MODEL), lay),
        pl.BlockSpec((1, D_MODEL, N_GROUPS), lay),
        pl.BlockSpec((1, 1, N_GROUPS), lay),
        pl.BlockSpec((1, D_MODEL, N_EXPERTS), lay),
        pl.BlockSpec((1, 1, N_EXPERTS), lay),
    ]
    args = [o, x, mod, w["w_out"], w["g2"], w["w_grp"], w["b_grp"], w["w_rtr"], w["b_rtr"]]
    aliases = {}
    if merged is not None:
        aliases = {len(args) + j: j for j in range(3)}
        in_specs += [pl.BlockSpec(memory_space=pl.ANY)] * 3
        args += list(merged)
    return pl.pallas_call(
        functools.partial(_post_kernel, merged is not None),
        out_shape=[jax.ShapeDtypeStruct((total, D_MODEL), f32), jax.ShapeDtypeStruct((total, HX_COLS), i32),
                   jax.ShapeDtypeStruct((total, 1), i32)],
        grid=(n // tile,),
        in_specs=in_specs,
        out_specs=[pl.BlockSpec((tile, D_MODEL), out_row), pl.BlockSpec((tile, HX_COLS), out_row),
                   pl.BlockSpec((tile, 1), out_row)],
        scratch_shapes=[pltpu.VMEM((D_MODEL, D_MODEL), bf16)],
        input_output_aliases=aliases,
        compiler_params=_params(("arbitrary",)),
        name="post_attention",
    )(*args)


def _ctx_kernel(lam_init, n_prev, resid, *refs):
    it = iter(refs)
    pre_in = [next(it) for _ in range(9 + (2 if resid else 0) + n_prev)]
    lam_ref, dg_ref = next(it), next(it)
    post_w = [next(it) for _ in range(6)]
    cache_refs = [next(it) for _ in range(len(CACHE_WIDTHS))]
    x1_ref, hx_ref, cls_ref = (next(it) for _ in range(3))
    w_in_scr, q_scr, kv_scr, o_scr, s_scr, p_scr, w_out_scr, x2_scr = (next(it) for _ in range(8))
    x_ref, mod_ref = pre_in[0], pre_in[1]
    x2_ref = [x2_scr] if resid else []

    _pre_kernel(False, n_prev, resid, *pre_in, q_scr, kv_scr.at[0, 0], *x2_ref, *cache_refs, w_in_scr)
    seq = s_scr.shape[1]
    for r in range(q_scr.shape[0] // seq):
        rows = pl.ds(r * seq, seq)
        _attn_kernel(lam_init, len(_SCORE_HEADS), 1, q_scr.at[rows], kv_scr.at[:, :, rows], lam_ref, dg_ref,
                     o_scr.at[rows], s_scr, p_scr)
    _post_kernel(False, o_scr, x2_scr if resid else x_ref, mod_ref, *post_w, x1_ref, hx_ref, cls_ref, w_out_scr)


def _ctx_layer(layer, x, n, seq, total, mod, w, prev_cache=(), resid=None):
    lay = lambda t: (layer, 0, 0)
    row = lambda t: (t, 0)
    reqs = CTX_REQUESTS_PER_STEP
    tile = reqs * seq
    mod_spec = lambda l: pl.BlockSpec((1, 1, N_MOD, D_MODEL), lambda t: (l, 0, 0, 0))
    in_specs = [
        pl.BlockSpec((tile, D_MODEL), row), mod_spec(layer),
        pl.BlockSpec((1, 1, D_MODEL), lay), pl.BlockSpec((1, IN_COLS, D_MODEL), lay),
        pl.BlockSpec((1, 1, KV_RANK), lay), pl.BlockSpec((1, 1, 512), lay), pl.BlockSpec((512, 512), lambda t: (0, 0)),
        pl.BlockSpec((1, 256, 384), lay), pl.BlockSpec((1, KV_RANK, 384), lay),
    ]
    args = [x, mod, w["g1"], w["w_in"], w["kv_g"], w["qk_g"], w["seg"], w["w_ka"], w["w_uv"]]
    if resid is not None:
        in_specs += [pl.BlockSpec((tile, D_MODEL), row), mod_spec(layer - 1)]
        args += [resid, mod]
    out_shape, out_specs, aliases = [], [], {}
    for j, width in enumerate(CACHE_WIDTHS):
        if prev_cache:
            aliases[len(args)] = len(out_shape)
            in_specs.append(pl.BlockSpec(memory_space=pl.ANY))
            args.append(prev_cache[j])
        shape = (MLA_ROPE, seq) if j == 1 else (seq, width)
        out_shape.append(jax.ShapeDtypeStruct((n // seq, DEPTH) + shape, f32))
        out_specs.append(pl.BlockSpec((reqs, 1) + shape, lambda t: (t, layer, 0, 0)))
    in_specs += [
        pl.BlockSpec((1, 4, DIFF_QK), lay), pl.BlockSpec((1, 1, DIFF_V), lay),
        pl.BlockSpec((1, D_MODEL, D_MODEL), lay), pl.BlockSpec((1, 1, D_MODEL), lay),
        pl.BlockSpec((1, D_MODEL, N_GROUPS), lay), pl.BlockSpec((1, 1, N_GROUPS), lay),
        pl.BlockSpec((1, D_MODEL, N_EXPERTS), lay), pl.BlockSpec((1, 1, N_EXPERTS), lay),
    ]
    args += [w["lam"], w["diff_g"], w["w_out"], w["g2"], w["w_grp"], w["b_grp"], w["w_rtr"], w["b_rtr"]]
    out_shape += [jax.ShapeDtypeStruct((total, D_MODEL), f32), jax.ShapeDtypeStruct((total, HX_COLS), i32),
                  jax.ShapeDtypeStruct((total, 1), i32)]
    out_specs += [pl.BlockSpec((tile, D_MODEL), row), pl.BlockSpec((tile, HX_COLS), row), pl.BlockSpec((tile, 1), row)]
    heads = len(_SCORE_HEADS)
    return pl.pallas_call(
        functools.partial(_ctx_kernel, 0.8 - 0.6 * math.exp(-0.3 * layer), len(prev_cache), resid is not None),
        out_shape=out_shape,
        grid=(n // tile,),
        in_specs=in_specs,
        out_specs=out_specs,
        scratch_shapes=[pltpu.VMEM((Z_COLS, D_MODEL), bf16), pltpu.VMEM((tile, Q_COLS), bf16),
                        pltpu.VMEM((1, 1, tile, KV_COLS), bf16), pltpu.VMEM((tile, D_MODEL), bf16),
                        pltpu.VMEM((heads, seq, seq), f32), pltpu.VMEM((heads, seq, seq), bf16),
                        pltpu.VMEM((D_MODEL, D_MODEL), bf16), pltpu.VMEM((tile, D_MODEL), f32)],
        input_output_aliases=aliases,
        compiler_params=_params(("arbitrary",)),
        name="context_layer",
    )(*args)


PLAN_CHUNK = 1024
TAB_ROWS = LANES


def _plan_kernel(cls_ref, slot_ref, tab_ref, rank_scr):
    n = cls_ref.shape[0]
    lane = jax.lax.broadcasted_iota(i32, (PLAN_CHUNK, LANES), 1)
    r = jax.lax.broadcasted_iota(i32, (LANES, LANES), 0)
    c = jax.lax.broadcasted_iota(i32, (LANES, LANES), 1)
    before = (c < r).astype(bf16)

    def count(b, seen):
        base = pl.multiple_of(b * PLAN_CHUNK, PLAN_CHUNK)
        onehot = (cls_ref[pl.ds(base, PLAN_CHUNK), :] == lane).astype(f32)
        for blk in range(PLAN_CHUNK // LANES):
            part = onehot[blk * LANES:(blk + 1) * LANES]
            ahead = _dot(before, part.astype(bf16)) + seen
            rank_scr[pl.ds(base + blk * LANES, LANES), :] = jnp.sum(part * ahead, axis=-1, keepdims=True)
            seen = seen + jnp.sum(part, axis=0, keepdims=True)
        return seen

    counts = jax.lax.fori_loop(0, n // PLAN_CHUNK, count, jnp.zeros((1, LANES), f32))
    tiles = jnp.floor((counts + (EXPERT_TILE - 1)) * (1.0 / EXPERT_TILE))
    rr = jax.lax.broadcasted_iota(i32, (LANES, LANES), 0)
    cc = jax.lax.broadcasted_iota(i32, (LANES, LANES), 1)
    ends = _dot(jnp.broadcast_to(tiles, (8, LANES)).astype(bf16), (rr <= cc).astype(bf16))[0:1]
    starts = ends - tiles

    def place(b, carry):
        rows = pl.ds(pl.multiple_of(b * PLAN_CHUNK, PLAN_CHUNK), PLAN_CHUNK)
        first = jnp.sum(jnp.where(cls_ref[rows, :] == lane, starts, 0.0), axis=-1, keepdims=True)
        slot_ref[rows, :] = (first * EXPERT_TILE + rank_scr[rows, :]).astype(i32)
        return carry

    jax.lax.fori_loop(0, n // PLAN_CHUNK, place, 0)

    tl = jax.lax.broadcasted_iota(i32, (TAB_ROWS, LANES), 1)
    n_tiles = jnp.sum(jnp.where(tl[0:1] == N_CLASSES - 1, ends, 0.0), axis=-1, keepdims=True)
    k = jnp.minimum(jax.lax.broadcasted_iota(i32, (TAB_ROWS, 1), 0).astype(f32), n_tiles - 1.0)
    cls_k = jnp.sum(jnp.where((tl < N_CLASSES) & (ends <= k), 1.0, 0.0), axis=-1, keepdims=True)
    cls_k = jnp.minimum(cls_k, N_CLASSES - 1.0)
    mine = tl.astype(f32) == cls_k
    used = jnp.sum(jnp.where(mine, counts, 0.0), axis=-1, keepdims=True)
    first = jnp.sum(jnp.where(mine, starts, 0.0), axis=-1, keepdims=True)
    valid = jnp.clip(used - (k - first) * EXPERT_TILE, 0.0, float(EXPERT_TILE))
    group = jnp.floor((cls_k + 0.5) * (1.0 / N_PAIRS))
    pair = cls_k - N_PAIRS * group
    lo = hi = jnp.zeros_like(pair)
    for p in range(N_PAIRS):
        lo = jnp.where(pair == p, float(PAIR_LO[p]), lo)
        hi = jnp.where(pair == p, float(PAIR_HI[p]), hi)
    e_lo = EXPERTS_PER_GROUP * group + lo
    e_hi = EXPERTS_PER_GROUP * group + hi
    tab = jnp.where(tl == 0, e_lo, jnp.where(tl == 1, e_hi, jnp.where(tl == 2, valid, jnp.where(tl == 3, n_tiles, 0.0))))
    tab_ref[...] = tab.astype(i32)


def _plan(cls, max_tiles):
    n = cls.shape[0]
    assert n % PLAN_CHUNK == 0 and max_tiles <= TAB_ROWS
    slot, tab = pl.pallas_call(
        _plan_kernel,
        out_shape=[jax.ShapeDtypeStruct((n, 1), i32), jax.ShapeDtypeStruct((TAB_ROWS, LANES), i32)],
        scratch_shapes=[pltpu.VMEM((n, 1), f32)],
        compiler_params=_params(None),
        name="dispatch_plan",
    )(cls)
    return slot.reshape(n), tab[:max_tiles, 0], tab[:max_tiles, 1], tab[:max_tiles, 2], tab[0, 3:4]


def _move_rows(src, idx, n_out, scatter):
    n = idx.shape[0]
    width = src.shape[1]
    per_worker = n // (SC_CORES * SC_SUBCORES)
    assert n % SC_ROWS == 0
    chunk = max(c for c in (64, 40, 32, 16, 8)
                if per_worker % c == 0 and 2 * c * width * src.dtype.itemsize <= SC_BUFFER_BYTES)
    n_chunks = per_worker // chunk
    mesh = plsc.VectorSubcoreMesh(core_axis_name="c", subcore_axis_name="s")

    @functools.partial(
        pl.kernel, mesh=mesh, out_type=jax.ShapeDtypeStruct((n_out, width), src.dtype),
        scratch_types=[pltpu.VMEM((chunk,), i32), pltpu.VMEM((chunk,), i32),
                       pltpu.VMEM((chunk, width), src.dtype), pltpu.VMEM((chunk, width), src.dtype),
                       pltpu.SemaphoreType.DMA, pltpu.SemaphoreType.DMA, pltpu.SemaphoreType.DMA,
                       pltpu.SemaphoreType.DMA])
    def move(src_hbm, idx_hbm, out_hbm, idx0, idx1, rows0, rows1, in0, in1, out0, out1):
        wid = jax.lax.axis_index("s") * SC_CORES + jax.lax.axis_index("c")
        base = wid * per_worker
        idx_v, rows_v, sem_in, sem_out = (idx0, idx1), (rows0, rows1), (in0, in1), (out0, out1)

        def fill(j):
            b = j % 2
            rows = pl.ds(base + j * chunk, chunk)
            pltpu.sync_copy(idx_hbm.at[rows], idx_v[b])
            src_rows = src_hbm.at[rows] if scatter else src_hbm.at[idx_v[b]]
            return pltpu.async_copy(src_rows, rows_v[b], sem_in[b])

        def drain(j):
            b = j % 2
            dst_rows = out_hbm.at[idx_v[b]] if scatter else out_hbm.at[pl.ds(base + j * chunk, chunk)]
            return pltpu.async_copy(rows_v[b], dst_rows, sem_out[b])

        fills, drains = {0: fill(0)}, {}
        for j in range(n_chunks):
            if j + 1 < n_chunks:
                if j >= 1:
                    drains[j - 1].wait()
                fills[j + 1] = fill(j + 1)
            fills[j].wait()
            drains[j] = drain(j)
        for j in range(max(n_chunks - 2, 0), n_chunks):
            drains[j].wait()

    return move(src, idx)


def _expert_kernel(lo_ref, hi_ref, valid_ref, nt_ref, xs_ref, wg_ref, wu_ref, wd_ref, ys_ref, wg_scr, wu_scr, wd_scr):
    k = pl.program_id(0)
    prev = jnp.maximum(k - 1, 0)

    @pl.when((k == 0) | (lo_ref[k] // EXPERTS_PER_GROUP != lo_ref[prev] // EXPERTS_PER_GROUP))
    def _():
        for j in range(EXPERTS_PER_GROUP):
            wg_scr[j] = wg_ref[0, 0, j].astype(bf16)
            wu_scr[j] = wu_ref[0, 0, j].astype(bf16)
            wd_scr[j] = wd_ref[0, 0, j].astype(bf16)

    @pl.when(k < nt_ref[0])
    def _():
        live = jax.lax.broadcasted_iota(i32, (EXPERT_TILE, 1), 0) < valid_ref[k]
        words = jnp.where(live, xs_ref[:, 0:HX_HALF], 0)
        x = jnp.concatenate([pltpu.bitcast(words & -65536, f32), pltpu.bitcast(words << 16, f32)], axis=1).astype(bf16)
        gates = pltpu.bitcast(jnp.where(live, xs_ref[:, HX_HALF:HX_COLS], 0), f32)
        y = None
        for lane, e_ref in enumerate((lo_ref, hi_ref)):
            j = e_ref[k] % EXPERTS_PER_GROUP
            hid = _silu(_dot(x, wg_scr[j])) * _dot(x, wu_scr[j]) * gates[:, lane:lane + 1]
            part = _dot(hid.astype(bf16), wd_scr[j])
            y = part if y is None else y + part
        ys_ref[...] = y


def _experts(layer, xs, e_lo, e_hi, valid, n_tiles, w):
    max_tiles = e_lo.shape[0]
    row = lambda k, lo, hi, valid, nt: (jnp.minimum(k, nt[0] - 1), 0)
    group = lambda k, lo, hi, valid, nt: (layer, lo[k] // EXPERTS_PER_GROUP, 0, 0, 0)
    by_group = lambda a: a.reshape(DEPTH, N_GROUPS, EXPERTS_PER_GROUP, *a.shape[2:])
    return pl.pallas_call(
        _expert_kernel,
        out_shape=jax.ShapeDtypeStruct((max_tiles * EXPERT_TILE, D_MODEL), f32),
        grid_spec=pltpu.PrefetchScalarGridSpec(
            num_scalar_prefetch=4, grid=(max_tiles,),
            in_specs=[pl.BlockSpec((EXPERT_TILE, HX_COLS), row),
                      pl.BlockSpec((1, 1, EXPERTS_PER_GROUP, D_MODEL, D_FF_EXPERT), group),
                      pl.BlockSpec((1, 1, EXPERTS_PER_GROUP, D_MODEL, D_FF_EXPERT), group),
                      pl.BlockSpec((1, 1, EXPERTS_PER_GROUP, D_FF_EXPERT, D_MODEL), group)],
            out_specs=pl.BlockSpec((EXPERT_TILE, D_MODEL), row),
            scratch_shapes=[pltpu.VMEM((EXPERTS_PER_GROUP, D_MODEL, D_FF_EXPERT), bf16),
                            pltpu.VMEM((EXPERTS_PER_GROUP, D_MODEL, D_FF_EXPERT), bf16),
                            pltpu.VMEM((EXPERTS_PER_GROUP, D_FF_EXPERT, D_MODEL), bf16)]),
        compiler_params=_params(("arbitrary",)),
        name="experts",
    )(e_lo, e_hi, valid, n_tiles, xs, by_group(w["w_gate"]), by_group(w["w_up"]), by_group(w["w_down"]))


def _final_kernel(x1_ref, y_ref, mod_ref, fg_ref, o_ref):
    o_ref[...] = _rms(x1_ref[...] + mod_ref[0, 0, 5:6, :] * y_ref[...], D_MODEL) * fg_ref[...]


def _final(x1, y, n, row0, mod, mod_row, w, tile):
    off = row0 // tile
    src_row = lambda t: (off + t, 0)
    return pl.pallas_call(
        _final_kernel,
        out_shape=jax.ShapeDtypeStruct((n, D_MODEL), f32),
        grid=(n // tile,),
        in_specs=[pl.BlockSpec((tile, D_MODEL), src_row), pl.BlockSpec((tile, D_MODEL), src_row),
                  pl.BlockSpec((1, 1, N_MOD, D_MODEL), lambda t: (DEPTH - 1, mod_row(t * tile), 0, 0)),
                  pl.BlockSpec((1, D_MODEL), lambda t: (0, 0))],
        out_specs=pl.BlockSpec((tile, D_MODEL), lambda t: (t, 0)),
        compiler_params=_params(("arbitrary",)),
        name="final_norm",
    )(x1, y, mod, w["final_g"])


def _rope_tables(n_tokens):
    pos = np.arange(n_tokens)
    row = (pos // GRID_W).astype(np.float64)
    col = (pos % GRID_W).astype(np.float64)

    def cs(rot_dim):
        quarter = rot_dim // 4
        inv = ROPE_THETA ** (-np.arange(quarter, dtype=np.float64) / quarter)
        ang = np.concatenate([row[:, None] * inv, col[:, None] * inv], axis=-1)
        return np.cos(ang), np.sin(ang)

    c32, s32 = cs(MLA_ROPE)
    c64, s64 = cs(HEAD_DIM)
    ones = np.ones((n_tokens, MLA_NOPE))
    zeros = np.zeros((n_tokens, MLA_NOPE))

    def rep(parts):
        period = np.concatenate(parts, axis=-1)
        return jnp.asarray(np.tile(period, (1, LANES // period.shape[-1])), f32)

    return (rep([ones, c32, c32]), rep([zeros, -s32, s32]), rep([c32, c32]), rep([-s32, s32]),
            rep([c64, c64]), rep([-s64, s64]))


def _layout_weights(norm1_g, norm2_g, w_in, mla_kv_norm_g, mla_w_uk, mla_w_uv, gqa_q_norm_g, gqa_k_norm_g,
                    diff_lambda, diff_norm_g, w_out, moe_w_group, moe_b_group, moe_w_router, moe_b_router,
                    moe_w_gate, moe_w_up, moe_w_down, final_norm_g):
    eye = jnp.eye(MLA_ROPE, dtype=f32)
    top = jnp.concatenate([mla_w_uk, jnp.zeros((DEPTH, KV_RANK, MLA_HEADS, MLA_ROPE), f32)], axis=-1)
    mid = jnp.concatenate([jnp.zeros((MLA_ROPE, MLA_HEADS, MLA_NOPE), f32),
                           jnp.broadcast_to(eye[:, None, :], (MLA_ROPE, MLA_HEADS, MLA_ROPE))], axis=-1)
    w_ka = jnp.concatenate([top.reshape(DEPTH, KV_RANK, 384),
                            jnp.broadcast_to(mid.reshape(1, MLA_ROPE, 384), (DEPTH, MLA_ROPE, 384)),
                            jnp.zeros((DEPTH, 256 - KV_RANK - MLA_ROPE, 384), f32)], axis=1).astype(bf16)
    seg_id = np.arange(512) // HEAD_DIM
    seg = jnp.asarray(seg_id[:, None] == seg_id[None, :], bf16)
    qk_g = jnp.concatenate([jnp.tile(gqa_q_norm_g, (1, GQA_HEADS)), jnp.tile(gqa_k_norm_g, (1, GQA_KV_HEADS))], axis=-1)
    return dict(
        g1=norm1_g.reshape(DEPTH, 1, D_MODEL), g2=norm2_g.reshape(DEPTH, 1, D_MODEL),
        w_in=jnp.swapaxes(w_in, 1, 2),
        kv_g=mla_kv_norm_g.reshape(DEPTH, 1, KV_RANK), qk_g=qk_g.reshape(DEPTH, 1, 512), seg=seg, w_ka=w_ka,
        w_uv=mla_w_uv.reshape(DEPTH, KV_RANK, 384), lam=diff_lambda, diff_g=diff_norm_g.reshape(DEPTH, 1, DIFF_V),
        w_out=w_out, w_grp=moe_w_group, b_grp=moe_b_group.reshape(DEPTH, 1, N_GROUPS), w_rtr=moe_w_router,
        b_rtr=moe_b_router.reshape(DEPTH, 1, N_EXPERTS), w_gate=moe_w_gate, w_up=moe_w_up, w_down=moe_w_down,
        final_g=final_norm_g.reshape(1, D_MODEL))


PRE_TILE = 512
CTX_REQUESTS_PER_STEP = 2
LAT_ATTN_TILE = 256
LAT_HEADS_PER_ROUND = 10
MXU_SUM_MIN_KEYS = 1024
POST_TILE = 512
FINAL_TILE = 1024


def kernel(x_prompt, x_sample, c, cache_mla_ckv, cache_mla_krope, cache_gqa_k, cache_gqa_v, cache_diff_k, cache_diff_v, c_ctx, norm1_g, norm2_g, w_mod, b_mod, w_in, mla_kv_norm_g, mla_w_uk, mla_w_uv, gqa_q_norm_g, gqa_k_norm_g, diff_lambda, diff_norm_g, w_out, moe_w_group, moe_b_group, moe_w_router, moe_b_router, moe_w_gate, moe_w_up, moe_w_down, final_norm_g):
    B, S, _ = x_prompt.shape
    Bl, Sl, _ = x_sample.shape
    n_ctx, n_lat = B * S, Bl * Sl
    total = n_ctx + n_lat
    assert Bl + 1 <= MOD_ROWS and DEPTH == 2 and total % SC_ROWS == 0
    slot_rows = -(-(total + N_CLASSES * EXPERT_TILE) // SC_ROWS) * SC_ROWS
    max_tiles = slot_rows // EXPERT_TILE
    w = _layout_weights(norm1_g, norm2_g, w_in, mla_kv_norm_g, mla_w_uk, mla_w_uv, gqa_q_norm_g, gqa_k_norm_g,
                        diff_lambda, diff_norm_g, w_out, moe_w_group, moe_b_group, moe_w_router, moe_b_router,
                        moe_w_gate, moe_w_up, moe_w_down, final_norm_g)
    cond = jnp.concatenate([c_ctx[None, :], c, jnp.zeros((MOD_ROWS - 1 - Bl, D_MODEL), f32)], axis=0)
    mod = _modulation(cond, w_mod, b_mod).reshape(DEPTH, MOD_ROWS, N_MOD, D_MODEL)
    ctx_row = lambda token: 0
    lat_row = lambda token: 1 + token // Sl
    tabs = _rope_tables(Sl)
    kv_past = _cache_rows((cache_mla_ckv, cache_mla_krope, cache_gqa_k, cache_gqa_v, cache_diff_k, cache_diff_v), w)
    per_b = Sl // LAT_ATTN_TILE

    x_ctx, x_lat = x_prompt.reshape(n_ctx, D_MODEL), x_sample.reshape(n_lat, D_MODEL)
    cache = ()
    x1 = y = None
    for i in range(DEPTH):
        if i == 0:
            *cache, x1_c, hx_c, cls_c = _ctx_layer(i, x_ctx, n_ctx, S, total, mod, w)
        else:
            *cache, x1_c, hx_c, cls_c = _ctx_layer(i, x1, n_ctx, S, total, mod, w, prev_cache=cache, resid=y)
        if i == 0:
            q_l, kv_l = _pre_latent(i, x_lat, n_lat, 0, Sl, mod, lat_row, w, PRE_TILE, tabs)
        else:
            q_l, kv_l, x_lat = _pre_latent(i, x1, n_lat, n_ctx, Sl, mod, lat_row, w, PRE_TILE, tabs, resid=y)
        past = (kv_past, PAST_LEN, lambda t, i=i: (i, t // per_b, 0, 0))
        own = (kv_l.reshape(1, Bl, Sl, KV_COLS), Sl, lambda t: (0, t // per_b, 0, 0))
        o_l = _attention(i, q_l, [past, own], w, LAT_ATTN_TILE, LAT_HEADS_PER_ROUND)
        x1, hx, cls = _post(i, o_l, x_lat, n_ctx, total, mod, lat_row, w, POST_TILE, merged=(x1_c, hx_c, cls_c))
        if i == DEPTH - 1:
            outs = (cache[0], jnp.swapaxes(cache[1], 2, 3), cache[2].reshape(B, DEPTH, S, GQA_KV_HEADS, HEAD_DIM),
                    cache[3].reshape(B, DEPTH, S, GQA_KV_HEADS, HEAD_DIM),
                    cache[4].reshape(B, DEPTH, S, DIFF_HEADS, 2, DIFF_QK),
                    cache[5].reshape(B, DEPTH, S, DIFF_HEADS, DIFF_V))
            cls, outs = jax.lax.optimization_barrier((cls, outs))
        slot, e_lo, e_hi, valid, n_tiles = _plan(cls, max_tiles)
        xs = _move_rows(hx, slot, slot_rows, scatter=True)
        ys = _experts(i, xs, e_lo, e_hi, valid, n_tiles, w)
        y = _move_rows(ys, slot, total, scatter=False)

    y_prompt = _final(x1, y, n_ctx, 0, mod, ctx_row, w, FINAL_TILE).reshape(B, S, D_MODEL)
    y_sample = _final(x1, y, n_lat, n_ctx, mod, lat_row, w, FINAL_TILE).reshape(Bl, Sl, D_MODEL)
    return (y_prompt, y_sample, *outs)
```

```python
import functools
import math

import jax
import jax.numpy as jnp
import numpy as np
from jax.experimental import pallas as pl
from jax.experimental.pallas import tpu as pltpu
from jax.experimental.pallas import tpu_sc as plsc

D_MODEL = 1024
DEPTH = 2
PAST_LEN = 512
GRID_W = 64
ROPE_THETA = 10000.0
EPS = 1e-6
LOG2E = 1.4426950408889634
N_MOD = 6
HEAD_DIM = 64
MLA_HEADS = 6
MLA_NOPE = 32
MLA_ROPE = 32
MLA_V = 64
KV_RANK = 128
GQA_HEADS = 6
GQA_KV_HEADS = 2
GQA_GROUP = GQA_HEADS // GQA_KV_HEADS
DIFF_HEADS = 4
DIFF_QK = 32
DIFF_V = 64
N_GROUPS = 4
EXPERTS_PER_GROUP = 4
N_EXPERTS = N_GROUPS * EXPERTS_PER_GROUP
D_FF_EXPERT = 256

LANES = 128
MOD_ROWS = 8

IN_COLS = 1952
IN_KR = 512
Z_QA, Z_CKV, Z_QG, Z_KG, Z_VG, Z_QD, Z_KD, Z_VD, Z_KR = 0, 384, 512, 896, 1024, 1152, 1408, 1664, 1920
Z_COLS = 2048
Q_A, Q_G, Q_D, Q_COLS = 0, 384, 768, 1024
KV_KA, KV_VA, KV_KG, KV_VG, KV_KD, KV_VD, KV_COLS = 0, 384, 768, 896, 1024, 1280, 1536
CACHE_WIDTHS = (128, 32, 128, 128, 256, 256)

PAIR_LO = (0, 0, 0, 1, 1, 2)
PAIR_HI = (1, 2, 3, 3, 2, 3)
N_PAIRS = len(PAIR_LO)
N_CLASSES = N_GROUPS * N_PAIRS
HX_HALF = D_MODEL // 2
HX_COLS = HX_HALF + LANES
EXPERT_TILE = 256

SC_CORES, SC_SUBCORES = 2, 16
SC_BUFFER_BYTES = 400 * 1024
SC_ROWS = SC_CORES * SC_SUBCORES * 8

VMEM_LIMIT = 56 * 1024 * 1024

bf16 = jnp.bfloat16
f32 = jnp.float32
i32 = jnp.int32


def _dot(a, b):
    return jnp.dot(a, b, preferred_element_type=f32)


def _dot_nt(a, b):
    return jax.lax.dot_general(a, b, (((1,), (1,)), ((), ())), preferred_element_type=f32)


def _rms(x, width):
    return x * jax.lax.rsqrt(jnp.sum(x * x, axis=-1, keepdims=True) * (1.0 / width) + EPS)


def _silu(x):
    return x * (1.0 / (1.0 + jnp.exp(-x)))


def _params(sem):
    return pltpu.CompilerParams(dimension_semantics=sem, vmem_limit_bytes=VMEM_LIMIT)


def _mod_kernel(cond_ref, w_ref, b_ref, o_ref):
    o_ref[0] = _dot(_silu(cond_ref[...]).astype(bf16), w_ref[0].astype(bf16)) + b_ref[0]


def _modulation(cond, w_mod, b_mod):
    return pl.pallas_call(
        _mod_kernel,
        out_shape=jax.ShapeDtypeStruct((DEPTH, MOD_ROWS, N_MOD * D_MODEL), f32),
        grid=(DEPTH, N_MOD),
        in_specs=[
            pl.BlockSpec((MOD_ROWS, D_MODEL), lambda i, j: (0, 0)),
            pl.BlockSpec((1, D_MODEL, D_MODEL), lambda i, j: (i, 0, j)),
            pl.BlockSpec((1, 1, D_MODEL), lambda i, j: (i, 0, j)),
        ],
        out_specs=pl.BlockSpec((1, MOD_ROWS, D_MODEL), lambda i, j: (i, 0, j)),
        compiler_params=_params(("arbitrary", "arbitrary")),
        name="modulation",
    )(cond, w_mod, b_mod.reshape(DEPTH, 1, N_MOD * D_MODEL))


def _swap_halves(x, half):
    lane = jax.lax.broadcasted_iota(i32, x.shape, 1)
    fwd = pltpu.roll(x, LANES - half, 1)
    bwd = pltpu.roll(x, half, 1)
    return jnp.where((lane & (2 * half - 1)) < half, fwd, bwd)


def _rope_block(x, cos, sin, half):
    return x * cos + _swap_halves(x, half) * sin


def _pre_kernel(rope, n_prev, resid, *refs):
    it = iter(refs)
    x_ref, mod_ref, g1_ref, w_in_ref, kvg_ref, qkg_ref, seg_ref, wka_ref, wuv_ref = (next(it) for _ in range(9))
    if resid:
        y_ref, pmod_ref = next(it), next(it)
    if rope:
        ca_ref, sa_ref, c32_ref, s32_ref, c64_ref, s64_ref = (next(it) for _ in range(6))
    prev_refs = [next(it) for _ in range(n_prev)]
    q_ref, kv_ref = next(it), next(it)
    if resid:
        x2_ref = next(it)
    cache_refs = [] if rope else [next(it) for _ in range(len(CACHE_WIDTHS))]
    w_scr = next(it)

    @pl.when(pl.program_id(0) == 0)
    def _():
        w_scr[0:IN_KR] = w_in_ref[0, 0:IN_KR].astype(bf16)
        w_scr[IN_KR:Z_KR] = w_in_ref[0, IN_KR + MLA_ROPE:IN_COLS].astype(bf16)
        w_scr[Z_KR:Z_KR + MLA_ROPE] = w_in_ref[0, IN_KR:IN_KR + MLA_ROPE].astype(bf16)
        w_scr[Z_KR + MLA_ROPE:Z_COLS] = jnp.zeros((Z_COLS - Z_KR - MLA_ROPE, D_MODEL), bf16)

    x = x_ref[...]
    if resid:
        x = x + pmod_ref[0, 0, 5:6, :] * y_ref[...]
        x2_ref[...] = x
    shift1 = mod_ref[0, 0, 0:1, :]
    scale1 = mod_ref[0, 0, 1:2, :]
    h = (_rms(x, D_MODEL) * g1_ref[0]) * (1.0 + scale1) + shift1
    z = _dot_nt(h.astype(bf16), w_scr[...])

    ckv = _rms(z[:, Z_CKV:Z_CKV + KV_RANK], KV_RANK) * kvg_ref[0]

    qk = z[:, Z_QG:Z_VG]
    sq = qk * qk
    sq_hi = sq.astype(bf16)
    sq_lo = (sq - sq_hi.astype(f32)).astype(bf16)
    seg = seg_ref[...]
    ms = jnp.concatenate(
        [_dot(sq_hi[:, LANES * j:LANES * (j + 1)], seg) + _dot(sq_lo[:, LANES * j:LANES * (j + 1)], seg)
         for j in range(qk.shape[1] // LANES)], axis=1) * (1.0 / HEAD_DIM)
    qk = qk * jax.lax.rsqrt(ms + EPS) * qkg_ref[0]

    def blocks(arr, n):
        return [arr[:, LANES * j:LANES * (j + 1)] for j in range(n)]

    qa = blocks(z[:, Z_QA:Z_QA + 384], 3)
    qkb = blocks(qk, 4)
    qd = blocks(z[:, Z_QD:Z_QD + 256], 2)
    kd = blocks(z[:, Z_KD:Z_KD + 256], 2)
    kr = z[:, Z_KR:Z_KR + LANES]
    if rope:
        ca, sa, c32, s32, c64, s64 = (r[...] for r in (ca_ref, sa_ref, c32_ref, s32_ref, c64_ref, s64_ref))
        qa = [_rope_block(b, ca, sa, MLA_ROPE // 2) for b in qa]
        qkb = [_rope_block(b, c64, s64, HEAD_DIM // 2) for b in qkb]
        qd = [_rope_block(b, c32, s32, DIFF_QK // 2) for b in qd]
        kd = [_rope_block(b, c32, s32, DIFF_QK // 2) for b in kd]
        kr = _rope_block(kr, c32, s32, MLA_ROPE // 2)

    vg = z[:, Z_VG:Z_VG + 128]
    vd = z[:, Z_VD:Z_VD + 256]
    ckv_b = ckv.astype(bf16)
    k_a = _dot(jnp.concatenate([ckv_b, kr.astype(bf16)], axis=1), wka_ref[0])
    v_a = _dot(ckv_b, wuv_ref[0].astype(bf16))

    for j in range(3):
        q_ref[:, Q_A + LANES * j:Q_A + LANES * (j + 1)] = (qa[j] * (HEAD_DIM ** -0.5 * LOG2E)).astype(bf16)
        q_ref[:, Q_G + LANES * j:Q_G + LANES * (j + 1)] = (qkb[j] * (HEAD_DIM ** -0.5 * LOG2E)).astype(bf16)
    for j in range(2):
        q_ref[:, Q_D + LANES * j:Q_D + LANES * (j + 1)] = (qd[j] * (DIFF_QK ** -0.5 * LOG2E)).astype(bf16)
        kv_ref[:, KV_KD + LANES * j:KV_KD + LANES * (j + 1)] = kd[j].astype(bf16)
    kv_ref[:, KV_KA:KV_KA + 384] = k_a.astype(bf16)
    kv_ref[:, KV_VA:KV_VA + 384] = v_a.astype(bf16)
    kv_ref[:, KV_KG:KV_KG + 128] = qkb[3].astype(bf16)
    kv_ref[:, KV_VG:KV_VG + 128] = vg.astype(bf16)
    kv_ref[:, KV_VD:KV_VD + 256] = vd.astype(bf16)
    if not rope:
        rows = [ckv, None, qkb[3], vg, jnp.concatenate(kd, axis=1), vd]
        for out, new in zip(cache_refs, rows):
            if new is None:
                seq = out.shape[3]
                for r in range(out.shape[0]):
                    out[r, 0] = kr[r * seq:(r + 1) * seq].T[:MLA_ROPE, :]
            else:
                reqs, _, seq, width = out.shape
                out[:, 0] = new.reshape(reqs, seq, width)


def _pre_latent(layer, x, n, row0, seq, mod, mod_row, w, tile, rope_tabs, resid=None):
    lay = lambda t: (layer, 0, 0)
    row = lambda t: (t, 0)
    off = row0 // tile
    src_row = lambda t: (off + t, 0)
    in_specs = [
        pl.BlockSpec((tile, D_MODEL), src_row),
        pl.BlockSpec((1, 1, N_MOD, D_MODEL), lambda t: (layer, mod_row(t * tile), 0, 0)),
        pl.BlockSpec((1, 1, D_MODEL), lay),
        pl.BlockSpec((1, IN_COLS, D_MODEL), lay),
        pl.BlockSpec((1, 1, KV_RANK), lay),
        pl.BlockSpec((1, 1, 512), lay),
        pl.BlockSpec((LANES, LANES), lambda t: (0, 0)),
        pl.BlockSpec((1, 256, 384), lay),
        pl.BlockSpec((1, KV_RANK, 384), lay),
    ]
    args = [x, mod, w["g1"], w["w_in"], w["kv_g"], w["qk_g"], w["seg"], w["w_ka"], w["w_uv"]]
    if resid is not None:
        in_specs += [pl.BlockSpec((tile, D_MODEL), src_row),
                     pl.BlockSpec((1, 1, N_MOD, D_MODEL), lambda t: (layer - 1, mod_row(t * tile), 0, 0))]
        args += [resid, mod]
    per_b = seq // tile
    in_specs += [pl.BlockSpec((tile, LANES), lambda t: (t % per_b, 0))] * 6
    args += list(rope_tabs)
    out_shape = [jax.ShapeDtypeStruct((n, Q_COLS), bf16), jax.ShapeDtypeStruct((n, KV_COLS), bf16)]
    out_specs = [pl.BlockSpec((tile, Q_COLS), row), pl.BlockSpec((tile, KV_COLS), row)]
    if resid is not None:
        out_shape.append(jax.ShapeDtypeStruct((n, D_MODEL), f32))
        out_specs.append(pl.BlockSpec((tile, D_MODEL), row))
    return pl.pallas_call(
        functools.partial(_pre_kernel, True, 0, resid is not None),
        out_shape=out_shape,
        grid=(n // tile,),
        in_specs=in_specs,
        out_specs=out_specs,
        scratch_shapes=[pltpu.VMEM((Z_COLS, D_MODEL), bf16)],
        compiler_params=_params(("arbitrary",)),
        name="pre_latent",
    )(*args)


PAST_CKV, PAST_KG, PAST_VG, PAST_KD, PAST_VD, PAST_KR, PAST_COLS = 0, 128, 256, 384, 640, 896, 928


def _cache_kernel(past_ref, wka_ref, wuv_ref, kv_ref):
    ckv_b = past_ref[0, 0, :, PAST_CKV:PAST_CKV + KV_RANK].astype(bf16)
    kr_b = past_ref[0, 0, :, PAST_KR:PAST_KR + MLA_ROPE].astype(bf16)
    wka = wka_ref[0]
    k_a = _dot(ckv_b, wka[:KV_RANK]) + _dot(kr_b, wka[KV_RANK:KV_RANK + MLA_ROPE])
    kv_ref[0, 0, :, KV_KA:KV_KA + 384] = k_a.astype(bf16)
    kv_ref[0, 0, :, KV_VA:KV_VA + 384] = _dot(ckv_b, wuv_ref[0].astype(bf16)).astype(bf16)
    kv_ref[0, 0, :, KV_KG:KV_KG + 128] = past_ref[0, 0, :, PAST_KG:PAST_KG + 128].astype(bf16)
    kv_ref[0, 0, :, KV_VG:KV_VG + 128] = past_ref[0, 0, :, PAST_VG:PAST_VG + 128].astype(bf16)
    kv_ref[0, 0, :, KV_KD:KV_KD + 256] = past_ref[0, 0, :, PAST_KD:PAST_KD + 256].astype(bf16)
    kv_ref[0, 0, :, KV_VD:KV_VD + 256] = past_ref[0, 0, :, PAST_VD:PAST_VD + 256].astype(bf16)


def _cache_rows(caches, w):
    ckv, kr, kg, vg, kd, vd = caches
    B = ckv.shape[0]
    flat = lambda a: a.reshape(B, DEPTH, PAST_LEN, -1)
    past = jnp.concatenate([flat(ckv), flat(kg), flat(vg), flat(kd), flat(vd), flat(kr)], axis=-1)
    return pl.pallas_call(
        _cache_kernel,
        out_shape=jax.ShapeDtypeStruct((DEPTH, B, PAST_LEN, KV_COLS), bf16),
        grid=(DEPTH, B),
        in_specs=[pl.BlockSpec((1, 1, PAST_LEN, PAST_COLS), lambda i, b: (b, i, 0, 0)),
                  pl.BlockSpec((1, 256, 384), lambda i, b: (i, 0, 0)), pl.BlockSpec((1, KV_RANK, 384), lambda i, b: (i, 0, 0))],
        out_specs=pl.BlockSpec((1, 1, PAST_LEN, KV_COLS), lambda i, b: (i, b, 0, 0)),
        compiler_params=_params(("arbitrary", "arbitrary")),
        name="cache_rows",
    )(past, w["w_ka"], w["w_uv"])


_SCORE_HEADS = (
    [(Q_A + 64 * h, KV_KA + 64 * h, 64, KV_VA + MLA_V * h) for h in range(MLA_HEADS)]
    + [(Q_G + 64 * h, KV_KG + 64 * (h // GQA_GROUP), 64, KV_VG + 64 * (h // GQA_GROUP)) for h in range(GQA_HEADS)]
    + [(Q_D + 64 * h + DIFF_QK * c, KV_KD + 64 * h + DIFF_QK * c, DIFF_QK, KV_VD + DIFF_V * h)
       for h in range(DIFF_HEADS) for c in range(2)])


def _attn_kernel(lam_init, per_round, n_src, q_ref, *refs):
    kv_refs = refs[:n_src]
    lam_ref, dg_ref, o_ref, s_ref, p_ref = refs[n_src:]
    spans, start = [], 0
    for r in kv_refs:
        spans.append((r, start, r.shape[2]))
        start += r.shape[2]
    mxu_sum = start >= MXU_SUM_MIN_KEYS

    outs = []
    for first in range(0, len(_SCORE_HEADS), per_round):
        chunk = _SCORE_HEADS[first:first + per_round]
        for j, (q_off, k_off, width, _) in enumerate(chunk):
            for r, lo, size in spans:
                s_ref[j, :, lo:lo + size] = _dot_nt(q_ref[:, q_off:q_off + width], r[0, 0, :, k_off:k_off + width])
        s = s_ref[...]
        p = jnp.exp2(s - jnp.max(s, axis=-1, keepdims=True))
        if mxu_sum:
            p_ref[...] = p.astype(bf16)
            for j, (_, _, _, v_off) in enumerate(chunk):
                o = sum(_dot(p_ref[j, :, lo:lo + size],
                             jnp.concatenate([r[0, 0, :, v_off:v_off + DIFF_V],
                                              jnp.ones((size, LANES - DIFF_V), bf16)], axis=1))
                        for r, lo, size in spans)
                outs.append((o * pltpu.roll(1.0 / o, DIFF_V, 1))[:, :DIFF_V])
        else:
            inv = 1.0 / jnp.sum(p, axis=-1, keepdims=True)
            p_ref[...] = p.astype(bf16)
            for j, (_, _, _, v_off) in enumerate(chunk):
                o = sum(_dot(p_ref[j, :, lo:lo + size], r[0, 0, :, v_off:v_off + DIFF_V]) for r, lo, size in spans)
                outs.append(o * inv[j])

    lp = lam_ref[0]
    e1 = jnp.exp(jnp.sum(lp[0:1] * lp[1:2], axis=-1, keepdims=True))
    e2 = jnp.exp(jnp.sum(lp[2:3] * lp[3:4], axis=-1, keepdims=True))
    lam = e1 - e2 + lam_init
    heads = outs[:MLA_HEADS + GQA_HEADS]
    for h in range(DIFF_HEADS):
        o1, o2 = outs[MLA_HEADS + GQA_HEADS + 2 * h:MLA_HEADS + GQA_HEADS + 2 * h + 2]
        heads.append(_rms(o1 - lam * o2, DIFF_V) * dg_ref[0] * (1.0 - lam_init))
    for j in range(len(heads) // 2):
        o_ref[:, LANES * j:LANES * (j + 1)] = jnp.concatenate(heads[2 * j:2 * j + 2], axis=1).astype(bf16)


def _attention(layer, q, sources, w, tile, per_round):
    n = q.shape[0]
    lam_init = 0.8 - 0.6 * math.exp(-0.3 * layer)
    s_kv = sum(rows for _, rows, _ in sources)
    assert len(_SCORE_HEADS) % per_round == 0
    return pl.pallas_call(
        functools.partial(_attn_kernel, lam_init, per_round, len(sources)),
        out_shape=jax.ShapeDtypeStruct((n, D_MODEL), bf16),
        scratch_shapes=[pltpu.VMEM((per_round, tile, s_kv), f32), pltpu.VMEM((per_round, tile, s_kv), bf16)],
        grid=(n // tile,),
        in_specs=[pl.BlockSpec((tile, Q_COLS), lambda t: (t, 0))]
        + [pl.BlockSpec((1, 1, rows, KV_COLS), index) for _, rows, index in sources]
        + [pl.BlockSpec((1, 4, DIFF_QK), lambda t: (layer, 0, 0)), pl.BlockSpec((1, 1, DIFF_V), lambda t: (layer, 0, 0))],
        out_specs=pl.BlockSpec((tile, D_MODEL), lambda t: (t, 0)),
        compiler_params=_params(("arbitrary",)),
        name="attention",
    )(q, *[arr for arr, _, _ in sources], w["lam"], w["diff_g"])


def _post_kernel(merge, *refs):
    it = iter(refs)
    o_ref, x_ref, mod_ref, w_out_ref, g2_ref, wg_ref, bg_ref, we_ref, be_ref = (next(it) for _ in range(9))
    if merge:
        next(it), next(it), next(it)
    x1_ref, hx_ref, cls_ref, w_scr, r_scr = (next(it) for _ in range(5))

    @pl.when(pl.program_id(0) == 0)
    def _():
        w_scr[...] = w_out_ref[0].astype(bf16)
        r_scr[...] = jnp.zeros(r_scr.shape, bf16)
        r_scr[:, 0:N_EXPERTS] = we_ref[0].astype(bf16)
        r_scr[:, N_EXPERTS:N_EXPERTS + N_GROUPS] = wg_ref[0].astype(bf16)

    gate1 = mod_ref[0, 0, 2:3, :]
    shift2 = mod_ref[0, 0, 3:4, :]
    scale2 = mod_ref[0, 0, 4:5, :]
    x1 = x_ref[...] + gate1 * _dot(o_ref[...], w_scr[...])
    x1_ref[...] = x1
    h2 = ((_rms(x1, D_MODEL) * g2_ref[0]) * (1.0 + scale2) + shift2).astype(bf16)
    bits = pltpu.bitcast(h2.astype(f32), i32)
    hx_ref[:, 0:HX_HALF] = bits[:, 0:HX_HALF] | jax.lax.shift_right_logical(bits[:, HX_HALF:D_MODEL], 16)

    def first_lane(mask, lane_f):
        return jnp.min(jnp.where(mask, lane_f, float(LANES)), axis=-1, keepdims=True)

    logits = _dot(h2, r_scr[...])
    gl = logits[:, N_EXPERTS:N_EXPERTS + N_GROUPS] + bg_ref[0]
    glane = jax.lax.broadcasted_iota(i32, gl.shape, 1).astype(f32)
    ge = jnp.exp(gl - jnp.max(gl, axis=-1, keepdims=True))
    gprob = ge / jnp.sum(ge, axis=-1, keepdims=True)
    g_top = jnp.max(gprob, axis=-1, keepdims=True)
    g_idx = first_lane(gprob == g_top, glane)

    el = logits[:, 0:N_EXPERTS] + be_ref[0]
    lane = jax.lax.broadcasted_iota(i32, el.shape, 1)
    lane_f = lane.astype(f32)
    emask = (lane >> 2).astype(f32) == g_idx
    em = jnp.where(emask, el, -jnp.inf)
    ee = jnp.where(emask, jnp.exp(em - jnp.max(em, axis=-1, keepdims=True)), 0.0)
    ep = ee / jnp.sum(ee, axis=-1, keepdims=True)
    p1 = jnp.max(jnp.where(emask, ep, -1.0), axis=-1, keepdims=True)
    i1 = first_lane(emask & (ep == p1), lane_f)
    rest = emask & (lane_f != i1)
    p2 = jnp.max(jnp.where(rest, ep, -1.0), axis=-1, keepdims=True)
    i2 = first_lane(rest & (ep == p2), lane_f)
    tot = p1 + p2
    w1 = g_top * (p1 / tot)
    w2 = g_top * (p2 / tot)

    lo = jnp.minimum(i1, i2) - EXPERTS_PER_GROUP * g_idx
    hi = jnp.maximum(i1, i2) - EXPERTS_PER_GROUP * g_idx
    pair = jnp.where(lo == 0.0, hi - 1.0, jnp.where(lo == 1.0, jnp.where(hi == 3.0, 3.0, 4.0), 5.0))
    cls_ref[...] = (N_PAIRS * g_idx + pair).astype(i32)
    g_lo = jnp.where(i1 < i2, w1, w2)
    g_hi = jnp.where(i1 < i2, w2, w1)
    tail_lane = jax.lax.broadcasted_iota(i32, (h2.shape[0], LANES), 1)
    hx_ref[:, HX_HALF:HX_COLS] = pltpu.bitcast(
        jnp.where(tail_lane == 0, g_lo, jnp.where(tail_lane == 1, g_hi, 0.0)), i32)


def _post(layer, o, x, row0, total, mod, mod_row, w, tile, merged=None):
    n = o.shape[0]
    lay = lambda t: (layer, 0, 0)
    row = lambda t: (t, 0)
    off = row0 // tile
    out_row = lambda t: (off + t, 0)
    in_specs = [
        pl.BlockSpec((tile, D_MODEL), row),
        pl.BlockSpec((tile, D_MODEL), row),
        pl.BlockSpec((1, 1, N_MOD, D_MODEL), lambda t: (layer, mod_row(t * tile), 0, 0)),
        pl.BlockSpec((1, D_MODEL, D_MODEL), lay),
        pl.BlockSpec((1, 1, D_MODEL), lay),
        pl.BlockSpec((1, D_MODEL, N_GROUPS), lay),
        pl.BlockSpec((1, 1, N_GROUPS), lay),
        pl.BlockSpec((1, D_MODEL, N_EXPERTS), lay),
        pl.BlockSpec((1, 1, N_EXPERTS), lay),
    ]
    args = [o, x, mod, w["w_out"], w["g2"], w["w_grp"], w["b_grp"], w["w_rtr"], w["b_rtr"]]
    aliases = {}
    if merged is not None:
        aliases = {len(args) + j: j for j in range(3)}
        in_specs += [pl.BlockSpec(memory_space=pl.ANY)] * 3
        args += list(merged)
    return pl.pallas_call(
        functools.partial(_post_kernel, merged is not None),
        out_shape=[jax.ShapeDtypeStruct((total, D_MODEL), f32), jax.ShapeDtypeStruct((total, HX_COLS), i32),
                   jax.ShapeDtypeStruct((total, 1), i32)],
        grid=(n // tile,),
        in_specs=in_specs,
        out_specs=[pl.BlockSpec((tile, D_MODEL), out_row), pl.BlockSpec((tile, HX_COLS), out_row),
                   pl.BlockSpec((tile, 1), out_row)],
        scratch_shapes=[pltpu.VMEM((D_MODEL, D_MODEL), bf16), pltpu.VMEM((D_MODEL, LANES), bf16)],
        input_output_aliases=aliases,
        compiler_params=_params(("arbitrary",)),
        name="post_attention",
    )(*args)


def _ctx_kernel(lam_init, n_prev, resid, *refs):
    it = iter(refs)
    pre_in = [next(it) for _ in range(9 + (2 if resid else 0) + n_prev)]
    lam_ref, dg_ref = next(it), next(it)
    post_w = [next(it) for _ in range(6)]
    cache_refs = [next(it) for _ in range(len(CACHE_WIDTHS))]
    x1_ref, hx_ref, cls_ref = (next(it) for _ in range(3))
    w_in_scr, q_scr, kv_scr, o_scr, s_scr, p_scr, w_out_scr, x2_scr, r_scr = (next(it) for _ in range(9))
    x_ref, mod_ref = pre_in[0], pre_in[1]
    x2_ref = [x2_scr] if resid else []

    _pre_kernel(False, n_prev, resid, *pre_in, q_scr, kv_scr.at[0, 0], *x2_ref, *cache_refs, w_in_scr)
    seq = s_scr.shape[1]
    for r in range(q_scr.shape[0] // seq):
        rows = pl.ds(r * seq, seq)
        _attn_kernel(lam_init, len(_SCORE_HEADS), 1, q_scr.at[rows], kv_scr.at[:, :, rows], lam_ref, dg_ref,
                     o_scr.at[rows], s_scr, p_scr)
    _post_kernel(False, o_scr, x2_scr if resid else x_ref, mod_ref, *post_w, x1_ref, hx_ref, cls_ref, w_out_scr,
                 r_scr)


def _ctx_layer(layer, x, n, seq, total, mod, w, prev_cache=(), resid=None):
    lay = lambda t: (layer, 0, 0)
    row = lambda t: (t, 0)
    reqs = CTX_REQUESTS_PER_STEP
    tile = reqs * seq
    mod_spec = lambda l: pl.BlockSpec((1, 1, N_MOD, D_MODEL), lambda t: (l, 0, 0, 0))
    in_specs = [
        pl.BlockSpec((tile, D_MODEL), row), mod_spec(layer),
        pl.BlockSpec((1, 1, D_MODEL), lay), pl.BlockSpec((1, IN_COLS, D_MODEL), lay),
        pl.BlockSpec((1, 1, KV_RANK), lay), pl.BlockSpec((1, 1, 512), lay), pl.BlockSpec((LANES, LANES), lambda t: (0, 0)),
        pl.BlockSpec((1, 256, 384), lay), pl.BlockSpec((1, KV_RANK, 384), lay),
    ]
    args = [x, mod, w["g1"], w["w_in"], w["kv_g"], w["qk_g"], w["seg"], w["w_ka"], w["w_uv"]]
    if resid is not None:
        in_specs += [pl.BlockSpec((tile, D_MODEL), row), mod_spec(layer - 1)]
        args += [resid, mod]
    out_shape, out_specs, aliases = [], [], {}
    for j, width in enumerate(CACHE_WIDTHS):
        if prev_cache:
            aliases[len(args)] = len(out_shape)
            in_specs.append(pl.BlockSpec(memory_space=pl.ANY))
            args.append(prev_cache[j])
        shape = (MLA_ROPE, seq) if j == 1 else (seq, width)
        out_shape.append(jax.ShapeDtypeStruct((n // seq, DEPTH) + shape, f32))
        out_specs.append(pl.BlockSpec((reqs, 1) + shape, lambda t: (t, layer, 0, 0)))
    in_specs += [
        pl.BlockSpec((1, 4, DIFF_QK), lay), pl.BlockSpec((1, 1, DIFF_V), lay),
        pl.BlockSpec((1, D_MODEL, D_MODEL), lay), pl.BlockSpec((1, 1, D_MODEL), lay),
        pl.BlockSpec((1, D_MODEL, N_GROUPS), lay), pl.BlockSpec((1, 1, N_GROUPS), lay),
        pl.BlockSpec((1, D_MODEL, N_EXPERTS), lay), pl.BlockSpec((1, 1, N_EXPERTS), lay),
    ]
    args += [w["lam"], w["diff_g"], w["w_out"], w["g2"], w["w_grp"], w["b_grp"], w["w_rtr"], w["b_rtr"]]
    out_shape += [jax.ShapeDtypeStruct((total, D_MODEL), f32), jax.ShapeDtypeStruct((total, HX_COLS), i32),
                  jax.ShapeDtypeStruct((total, 1), i32)]
    out_specs += [pl.BlockSpec((tile, D_MODEL), row), pl.BlockSpec((tile, HX_COLS), row), pl.BlockSpec((tile, 1), row)]
    heads = len(_SCORE_HEADS)
    return pl.pallas_call(
        functools.partial(_ctx_kernel, 0.8 - 0.6 * math.exp(-0.3 * layer), len(prev_cache), resid is not None),
        out_shape=out_shape,
        grid=(n // tile,),
        in_specs=in_specs,
        out_specs=out_specs,
        scratch_shapes=[pltpu.VMEM((Z_COLS, D_MODEL), bf16), pltpu.VMEM((tile, Q_COLS), bf16),
                        pltpu.VMEM((1, 1, tile, KV_COLS), bf16), pltpu.VMEM((tile, D_MODEL), bf16),
                        pltpu.VMEM((heads, seq, seq), f32), pltpu.VMEM((heads, seq, seq), bf16),
                        pltpu.VMEM((D_MODEL, D_MODEL), bf16), pltpu.VMEM((tile, D_MODEL), f32),
                        pltpu.VMEM((D_MODEL, LANES), bf16)],
        input_output_aliases=aliases,
        compiler_params=_params(("arbitrary",)),
        name="context_layer",
    )(*args)


PLAN_CHUNK = 1024
TAB_ROWS = LANES


def _plan_kernel(cls_ref, slot_ref, tab_ref, rank_scr):
    n = cls_ref.shape[0]
    lane = jax.lax.broadcasted_iota(i32, (PLAN_CHUNK, LANES), 1)
    r = jax.lax.broadcasted_iota(i32, (LANES, LANES), 0)
    c = jax.lax.broadcasted_iota(i32, (LANES, LANES), 1)
    before = (c < r).astype(bf16)

    def count(b, seen):
        base = pl.multiple_of(b * PLAN_CHUNK, PLAN_CHUNK)
        onehot = (cls_ref[pl.ds(base, PLAN_CHUNK), :] == lane).astype(f32)
        for blk in range(PLAN_CHUNK // LANES):
            part = onehot[blk * LANES:(blk + 1) * LANES]
            ahead = _dot(before, part.astype(bf16)) + seen
            rank_scr[pl.ds(base + blk * LANES, LANES), :] = jnp.sum(part * ahead, axis=-1, keepdims=True)
            seen = seen + jnp.sum(part, axis=0, keepdims=True)
        return seen

    counts = jax.lax.fori_loop(0, n // PLAN_CHUNK, count, jnp.zeros((1, LANES), f32))
    tiles = jnp.floor((counts + (EXPERT_TILE - 1)) * (1.0 / EXPERT_TILE))
    rr = jax.lax.broadcasted_iota(i32, (LANES, LANES), 0)
    cc = jax.lax.broadcasted_iota(i32, (LANES, LANES), 1)
    ends = _dot(jnp.broadcast_to(tiles, (8, LANES)).astype(bf16), (rr <= cc).astype(bf16))[0:1]
    starts = ends - tiles

    def place(b, carry):
        rows = pl.ds(pl.multiple_of(b * PLAN_CHUNK, PLAN_CHUNK), PLAN_CHUNK)
        first = jnp.sum(jnp.where(cls_ref[rows, :] == lane, starts, 0.0), axis=-1, keepdims=True)
        slot_ref[rows, :] = (first * EXPERT_TILE + rank_scr[rows, :]).astype(i32)
        return carry

    jax.lax.fori_loop(0, n // PLAN_CHUNK, place, 0)

    tl = jax.lax.broadcasted_iota(i32, (TAB_ROWS, LANES), 1)
    n_tiles = jnp.sum(jnp.where(tl[0:1] == N_CLASSES - 1, ends, 0.0), axis=-1, keepdims=True)
    k = jnp.minimum(jax.lax.broadcasted_iota(i32, (TAB_ROWS, 1), 0).astype(f32), n_tiles - 1.0)
    cls_k = jnp.sum(jnp.where((tl < N_CLASSES) & (ends <= k), 1.0, 0.0), axis=-1, keepdims=True)
    cls_k = jnp.minimum(cls_k, N_CLASSES - 1.0)
    mine = tl.astype(f32) == cls_k
    used = jnp.sum(jnp.where(mine, counts, 0.0), axis=-1, keepdims=True)
    first = jnp.sum(jnp.where(mine, starts, 0.0), axis=-1, keepdims=True)
    valid = jnp.clip(used - (k - first) * EXPERT_TILE, 0.0, float(EXPERT_TILE))
    group = jnp.floor((cls_k + 0.5) * (1.0 / N_PAIRS))
    pair = cls_k - N_PAIRS * group
    lo = hi = jnp.zeros_like(pair)
    for p in range(N_PAIRS):
        lo = jnp.where(pair == p, float(PAIR_LO[p]), lo)
        hi = jnp.where(pair == p, float(PAIR_HI[p]), hi)
    e_lo = EXPERTS_PER_GROUP * group + lo
    e_hi = EXPERTS_PER_GROUP * group + hi
    tab = jnp.where(tl == 0, e_lo, jnp.where(tl == 1, e_hi, jnp.where(tl == 2, valid, jnp.where(tl == 3, n_tiles, 0.0))))
    tab_ref[...] = tab.astype(i32)


def _plan(cls, max_tiles):
    n = cls.shape[0]
    assert n % PLAN_CHUNK == 0 and max_tiles <= TAB_ROWS
    slot, tab = pl.pallas_call(
        _plan_kernel,
        out_shape=[jax.ShapeDtypeStruct((n, 1), i32), jax.ShapeDtypeStruct((TAB_ROWS, LANES), i32)],
        scratch_shapes=[pltpu.VMEM((n, 1), f32)],
        compiler_params=_params(None),
        name="dispatch_plan",
    )(cls)
    return slot.reshape(n), tab[:max_tiles, 0], tab[:max_tiles, 1], tab[:max_tiles, 2], tab[0, 3:4]


def _move_rows(src, idx, n_out, scatter):
    n = idx.shape[0]
    width = src.shape[1]
    per_worker = n // (SC_CORES * SC_SUBCORES)
    assert n % SC_ROWS == 0
    chunk = max(c for c in (64, 40, 32, 16, 8)
                if per_worker % c == 0 and 2 * c * width * src.dtype.itemsize <= SC_BUFFER_BYTES)
    n_chunks = per_worker // chunk
    mesh = plsc.VectorSubcoreMesh(core_axis_name="c", subcore_axis_name="s")

    @functools.partial(
        pl.kernel, mesh=mesh, out_type=jax.ShapeDtypeStruct((n_out, width), src.dtype),
        scratch_types=[pltpu.VMEM((chunk,), i32), pltpu.VMEM((chunk,), i32),
                       pltpu.VMEM((chunk, width), src.dtype), pltpu.VMEM((chunk, width), src.dtype),
                       pltpu.SemaphoreType.DMA, pltpu.SemaphoreType.DMA, pltpu.SemaphoreType.DMA,
                       pltpu.SemaphoreType.DMA])
    def move(src_hbm, idx_hbm, out_hbm, idx0, idx1, rows0, rows1, in0, in1, out0, out1):
        wid = jax.lax.axis_index("s") * SC_CORES + jax.lax.axis_index("c")
        base = wid * per_worker
        idx_v, rows_v, sem_in, sem_out = (idx0, idx1), (rows0, rows1), (in0, in1), (out0, out1)

        def fill(j):
            b = j % 2
            rows = pl.ds(base + j * chunk, chunk)
            pltpu.sync_copy(idx_hbm.at[rows], idx_v[b])
            src_rows = src_hbm.at[rows] if scatter else src_hbm.at[idx_v[b]]
            return pltpu.async_copy(src_rows, rows_v[b], sem_in[b])

        def drain(j):
            b = j % 2
            dst_rows = out_hbm.at[idx_v[b]] if scatter else out_hbm.at[pl.ds(base + j * chunk, chunk)]
            return pltpu.async_copy(rows_v[b], dst_rows, sem_out[b])

        fills, drains = {0: fill(0)}, {}
        for j in range(n_chunks):
            if j + 1 < n_chunks:
                if j >= 1:
                    drains[j - 1].wait()
                fills[j + 1] = fill(j + 1)
            fills[j].wait()
            drains[j] = drain(j)
        for j in range(max(n_chunks - 2, 0), n_chunks):
            drains[j].wait()

    return move(src, idx)


def _expert_kernel(lo_ref, hi_ref, valid_ref, nt_ref, xs_ref, wg_ref, wu_ref, wd_ref, ys_ref, wg_scr, wu_scr, wd_scr):
    k = pl.program_id(0)
    prev = jnp.maximum(k - 1, 0)

    @pl.when((k == 0) | (lo_ref[k] // EXPERTS_PER_GROUP != lo_ref[prev] // EXPERTS_PER_GROUP))
    def _():
        for j in range(EXPERTS_PER_GROUP):
            wg_scr[j] = wg_ref[0, 0, j].astype(bf16)
            wu_scr[j] = wu_ref[0, 0, j].astype(bf16)
            wd_scr[j] = wd_ref[0, 0, j].astype(bf16)

    @pl.when(k < nt_ref[0])
    def _():
        live = jax.lax.broadcasted_iota(i32, (EXPERT_TILE, 1), 0) < valid_ref[k]
        words = jnp.where(live, xs_ref[:, 0:HX_HALF], 0)
        x = jnp.concatenate([pltpu.bitcast(words & -65536, f32), pltpu.bitcast(words << 16, f32)], axis=1).astype(bf16)
        gates = pltpu.bitcast(jnp.where(live, xs_ref[:, HX_HALF:HX_COLS], 0), f32)
        y = None
        for lane, e_ref in enumerate((lo_ref, hi_ref)):
            j = e_ref[k] % EXPERTS_PER_GROUP
            hid = _silu(_dot(x, wg_scr[j])) * _dot(x, wu_scr[j]) * gates[:, lane:lane + 1]
            part = _dot(hid.astype(bf16), wd_scr[j])
            y = part if y is None else y + part
        ys_ref[...] = y


def _experts(layer, xs, e_lo, e_hi, valid, n_tiles, w):
    max_tiles = e_lo.shape[0]
    row = lambda k, lo, hi, valid, nt: (jnp.minimum(k, nt[0] - 1), 0)
    group = lambda k, lo, hi, valid, nt: (layer, lo[k] // EXPERTS_PER_GROUP, 0, 0, 0)
    by_group = lambda a: a.reshape(DEPTH, N_GROUPS, EXPERTS_PER_GROUP, *a.shape[2:])
    return pl.pallas_call(
        _expert_kernel,
        out_shape=jax.ShapeDtypeStruct((max_tiles * EXPERT_TILE, D_MODEL), f32),
        grid_spec=pltpu.PrefetchScalarGridSpec(
            num_scalar_prefetch=4, grid=(max_tiles,),
            in_specs=[pl.BlockSpec((EXPERT_TILE, HX_COLS), row),
                      pl.BlockSpec((1, 1, EXPERTS_PER_GROUP, D_MODEL, D_FF_EXPERT), group),
                      pl.BlockSpec((1, 1, EXPERTS_PER_GROUP, D_MODEL, D_FF_EXPERT), group),
                      pl.BlockSpec((1, 1, EXPERTS_PER_GROUP, D_FF_EXPERT, D_MODEL), group)],
            out_specs=pl.BlockSpec((EXPERT_TILE, D_MODEL), row),
            scratch_shapes=[pltpu.VMEM((EXPERTS_PER_GROUP, D_MODEL, D_FF_EXPERT), bf16),
                            pltpu.VMEM((EXPERTS_PER_GROUP, D_MODEL, D_FF_EXPERT), bf16),
                            pltpu.VMEM((EXPERTS_PER_GROUP, D_FF_EXPERT, D_MODEL), bf16)]),
        compiler_params=_params(("arbitrary",)),
        name="experts",
    )(e_lo, e_hi, valid, n_tiles, xs, by_group(w["w_gate"]), by_group(w["w_up"]), by_group(w["w_down"]))


def _final_kernel(x1_ref, y_ref, mod_ref, fg_ref, o_ref):
    o_ref[...] = _rms(x1_ref[...] + mod_ref[0, 0, 5:6, :] * y_ref[...], D_MODEL) * fg_ref[...]


def _final(x1, y, n, row0, mod, mod_row, w, tile):
    off = row0 // tile
    src_row = lambda t: (off + t, 0)
    return pl.pallas_call(
        _final_kernel,
        out_shape=jax.ShapeDtypeStruct((n, D_MODEL), f32),
        grid=(n // tile,),
        in_specs=[pl.BlockSpec((tile, D_MODEL), src_row), pl.BlockSpec((tile, D_MODEL), src_row),
                  pl.BlockSpec((1, 1, N_MOD, D_MODEL), lambda t: (DEPTH - 1, mod_row(t * tile), 0, 0)),
                  pl.BlockSpec((1, D_MODEL), lambda t: (0, 0))],
        out_specs=pl.BlockSpec((tile, D_MODEL), lambda t: (t, 0)),
        compiler_params=_params(("arbitrary",)),
        name="final_norm",
    )(x1, y, mod, w["final_g"])


def _rope_tables(n_tokens):
    pos = np.arange(n_tokens)
    row = (pos // GRID_W).astype(np.float64)
    col = (pos % GRID_W).astype(np.float64)

    def cs(rot_dim):
        quarter = rot_dim // 4
        inv = ROPE_THETA ** (-np.arange(quarter, dtype=np.float64) / quarter)
        ang = np.concatenate([row[:, None] * inv, col[:, None] * inv], axis=-1)
        return np.cos(ang), np.sin(ang)

    c32, s32 = cs(MLA_ROPE)
    c64, s64 = cs(HEAD_DIM)
    ones = np.ones((n_tokens, MLA_NOPE))
    zeros = np.zeros((n_tokens, MLA_NOPE))

    def rep(parts):
        period = np.concatenate(parts, axis=-1)
        return jnp.asarray(np.tile(period, (1, LANES // period.shape[-1])), f32)

    return (rep([ones, c32, c32]), rep([zeros, -s32, s32]), rep([c32, c32]), rep([-s32, s32]),
            rep([c64, c64]), rep([-s64, s64]))


def _layout_weights(norm1_g, norm2_g, w_in, mla_kv_norm_g, mla_w_uk, mla_w_uv, gqa_q_norm_g, gqa_k_norm_g,
                    diff_lambda, diff_norm_g, w_out, moe_w_group, moe_b_group, moe_w_router, moe_b_router,
                    moe_w_gate, moe_w_up, moe_w_down, final_norm_g):
    eye = jnp.eye(MLA_ROPE, dtype=f32)
    top = jnp.concatenate([mla_w_uk, jnp.zeros((DEPTH, KV_RANK, MLA_HEADS, MLA_ROPE), f32)], axis=-1)
    mid = jnp.concatenate([jnp.zeros((MLA_ROPE, MLA_HEADS, MLA_NOPE), f32),
                           jnp.broadcast_to(eye[:, None, :], (MLA_ROPE, MLA_HEADS, MLA_ROPE))], axis=-1)
    w_ka = jnp.concatenate([top.reshape(DEPTH, KV_RANK, 384),
                            jnp.broadcast_to(mid.reshape(1, MLA_ROPE, 384), (DEPTH, MLA_ROPE, 384)),
                            jnp.zeros((DEPTH, 256 - KV_RANK - MLA_ROPE, 384), f32)], axis=1).astype(bf16)
    seg_id = np.arange(LANES) // HEAD_DIM
    seg = jnp.asarray(seg_id[:, None] == seg_id[None, :], bf16)
    qk_g = jnp.concatenate([jnp.tile(gqa_q_norm_g, (1, GQA_HEADS)), jnp.tile(gqa_k_norm_g, (1, GQA_KV_HEADS))], axis=-1)
    return dict(
        g1=norm1_g.reshape(DEPTH, 1, D_MODEL), g2=norm2_g.reshape(DEPTH, 1, D_MODEL),
        w_in=jnp.swapaxes(w_in, 1, 2),
        kv_g=mla_kv_norm_g.reshape(DEPTH, 1, KV_RANK), qk_g=qk_g.reshape(DEPTH, 1, 512), seg=seg, w_ka=w_ka,
        w_uv=mla_w_uv.reshape(DEPTH, KV_RANK, 384), lam=diff_lambda, diff_g=diff_norm_g.reshape(DEPTH, 1, DIFF_V),
        w_out=w_out, w_grp=moe_w_group, b_grp=moe_b_group.reshape(DEPTH, 1, N_GROUPS), w_rtr=moe_w_router,
        b_rtr=moe_b_router.reshape(DEPTH, 1, N_EXPERTS), w_gate=moe_w_gate, w_up=moe_w_up, w_down=moe_w_down,
        final_g=final_norm_g.reshape(1, D_MODEL))


PRE_TILE = 512
CTX_REQUESTS_PER_STEP = 2
LAT_ATTN_TILE = 256
LAT_HEADS_PER_ROUND = 10
MXU_SUM_MIN_KEYS = 1024
POST_TILE = 512
FINAL_TILE = 1024


def kernel(x_prompt, x_sample, c, cache_mla_ckv, cache_mla_krope, cache_gqa_k, cache_gqa_v, cache_diff_k, cache_diff_v, c_ctx, norm1_g, norm2_g, w_mod, b_mod, w_in, mla_kv_norm_g, mla_w_uk, mla_w_uv, gqa_q_norm_g, gqa_k_norm_g, diff_lambda, diff_norm_g, w_out, moe_w_group, moe_b_group, moe_w_router, moe_b_router, moe_w_gate, moe_w_up, moe_w_down, final_norm_g):
    B, S, _ = x_prompt.shape
    Bl, Sl, _ = x_sample.shape
    n_ctx, n_lat = B * S, Bl * Sl
    total = n_ctx + n_lat
    assert Bl + 1 <= MOD_ROWS and DEPTH == 2 and total % SC_ROWS == 0
    slot_rows = -(-(total + N_CLASSES * EXPERT_TILE) // SC_ROWS) * SC_ROWS
    max_tiles = slot_rows // EXPERT_TILE
    w = _layout_weights(norm1_g, norm2_g, w_in, mla_kv_norm_g, mla_w_uk, mla_w_uv, gqa_q_norm_g, gqa_k_norm_g,
                        diff_lambda, diff_norm_g, w_out, moe_w_group, moe_b_group, moe_w_router, moe_b_router,
                        moe_w_gate, moe_w_up, moe_w_down, final_norm_g)
    cond = jnp.concatenate([c_ctx[None, :], c, jnp.zeros((MOD_ROWS - 1 - Bl, D_MODEL), f32)], axis=0)
    mod = _modulation(cond, w_mod, b_mod).reshape(DEPTH, MOD_ROWS, N_MOD, D_MODEL)
    ctx_row = lambda token: 0
    lat_row = lambda token: 1 + token // Sl
    tabs = _rope_tables(Sl)
    kv_past = _cache_rows((cache_mla_ckv, cache_mla_krope, cache_gqa_k, cache_gqa_v, cache_diff_k, cache_diff_v), w)
    per_b = Sl // LAT_ATTN_TILE

    x_ctx, x_lat = x_prompt.reshape(n_ctx, D_MODEL), x_sample.reshape(n_lat, D_MODEL)
    cache = ()
    x1 = y = None
    for i in range(DEPTH):
        if i == 0:
            *cache, x1_c, hx_c, cls_c = _ctx_layer(i, x_ctx, n_ctx, S, total, mod, w)
        else:
            *cache, x1_c, hx_c, cls_c = _ctx_layer(i, x1, n_ctx, S, total, mod, w, prev_cache=cache, resid=y)
        if i == 0:
            q_l, kv_l = _pre_latent(i, x_lat, n_lat, 0, Sl, mod, lat_row, w, PRE_TILE, tabs)
        else:
            q_l, kv_l, x_lat = _pre_latent(i, x1, n_lat, n_ctx, Sl, mod, lat_row, w, PRE_TILE, tabs, resid=y)
        past = (kv_past, PAST_LEN, lambda t, i=i: (i, t // per_b, 0, 0))
        own = (kv_l.reshape(1, Bl, Sl, KV_COLS), Sl, lambda t: (0, t // per_b, 0, 0))
        o_l = _attention(i, q_l, [past, own], w, LAT_ATTN_TILE, LAT_HEADS_PER_ROUND)
        x1, hx, cls = _post(i, o_l, x_lat, n_ctx, total, mod, lat_row, w, POST_TILE, merged=(x1_c, hx_c, cls_c))
        if i == DEPTH - 1:
            outs = (cache[0], jnp.swapaxes(cache[1], 2, 3), cache[2].reshape(B, DEPTH, S, GQA_KV_HEADS, HEAD_DIM),
                    cache[3].reshape(B, DEPTH, S, GQA_KV_HEADS, HEAD_DIM),
                    cache[4].reshape(B, DEPTH, S, DIFF_HEADS, 2, DIFF_QK),
                    cache[5].reshape(B, DEPTH, S, DIFF_HEADS, DIFF_V))
            cls, outs = jax.lax.optimization_barrier((cls, outs))
        slot, e_lo, e_hi, valid, n_tiles = _plan(cls, max_tiles)
        xs = _move_rows(hx, slot, slot_rows, scatter=True)
        ys = _experts(i, xs, e_lo, e_hi, valid, n_tiles, w)
        y = _move_rows(ys, slot, total, scatter=False)

    y_prompt = _final(x1, y, n_ctx, 0, mod, ctx_row, w, FINAL_TILE).reshape(B, S, D_MODEL)
    y_sample = _final(x1, y, n_lat, n_ctx, mod, lat_row, w, FINAL_TILE).reshape(Bl, Sl, D_MODEL)
    return (y_prompt, y_sample, *outs)
```

```python
import functools
import math

import jax
import jax.numpy as jnp
import numpy as np
from jax.experimental import pallas as pl
from jax.experimental.pallas import tpu as pltpu
from jax.experimental.pallas import tpu_sc as plsc

D_MODEL = 1024
DEPTH = 2
PAST_LEN = 512
GRID_W = 64
ROPE_THETA = 10000.0
EPS = 1e-6
LOG2E = 1.4426950408889634
N_MOD = 6
HEAD_DIM = 64
MLA_HEADS = 6
MLA_NOPE = 32
MLA_ROPE = 32
MLA_V = 64
KV_RANK = 128
GQA_HEADS = 6
GQA_KV_HEADS = 2
GQA_GROUP = GQA_HEADS // GQA_KV_HEADS
DIFF_HEADS = 4
DIFF_QK = 32
DIFF_V = 64
N_GROUPS = 4
EXPERTS_PER_GROUP = 4
N_EXPERTS = N_GROUPS * EXPERTS_PER_GROUP
D_FF_EXPERT = 256

LANES = 128
MOD_ROWS = 8

IN_COLS = 1952
IN_KR = 512
Z_QA, Z_CKV, Z_QG, Z_KG, Z_VG, Z_QD, Z_KD, Z_VD, Z_KR = 0, 384, 512, 896, 1024, 1152, 1408, 1664, 1920
Z_COLS = 2048
Q_A, Q_G, Q_D, Q_COLS = 0, 384, 768, 1024
KV_KA, KV_VA, KV_KG, KV_VG, KV_KD, KV_VD, KV_COLS = 0, 384, 768, 896, 1024, 1280, 1536
CACHE_WIDTHS = (128, 32, 128, 128, 256, 256)

PAIR_LO = (0, 0, 0, 1, 1, 2)
PAIR_HI = (1, 2, 3, 3, 2, 3)
N_PAIRS = len(PAIR_LO)
N_CLASSES = N_GROUPS * N_PAIRS
HX_HALF = D_MODEL // 2
HX_COLS = HX_HALF + LANES
EXPERT_TILE = 256

SC_CORES, SC_SUBCORES = 2, 16
SC_BUFFER_BYTES = 400 * 1024
SC_ROWS = SC_CORES * SC_SUBCORES * 8

VMEM_LIMIT = 56 * 1024 * 1024

bf16 = jnp.bfloat16
f32 = jnp.float32
i32 = jnp.int32


def _dot(a, b):
    return jnp.dot(a, b, preferred_element_type=f32)


def _dot_nt(a, b):
    return jax.lax.dot_general(a, b, (((1,), (1,)), ((), ())), preferred_element_type=f32)


def _rms(x, width):
    return x * jax.lax.rsqrt(jnp.sum(x * x, axis=-1, keepdims=True) * (1.0 / width) + EPS)


def _silu(x):
    return x * (1.0 / (1.0 + jnp.exp(-x)))


def _params(sem):
    return pltpu.CompilerParams(dimension_semantics=sem, vmem_limit_bytes=VMEM_LIMIT)


def _mod_kernel(cond_ref, w_ref, b_ref, o_ref):
    o_ref[0] = _dot(_silu(cond_ref[...]).astype(bf16), w_ref[0].astype(bf16)) + b_ref[0]


def _modulation(cond, w_mod, b_mod):
    return pl.pallas_call(
        _mod_kernel,
        out_shape=jax.ShapeDtypeStruct((DEPTH, MOD_ROWS, N_MOD * D_MODEL), f32),
        grid=(DEPTH, N_MOD),
        in_specs=[
            pl.BlockSpec((MOD_ROWS, D_MODEL), lambda i, j: (0, 0)),
            pl.BlockSpec((1, D_MODEL, D_MODEL), lambda i, j: (i, 0, j)),
            pl.BlockSpec((1, 1, D_MODEL), lambda i, j: (i, 0, j)),
        ],
        out_specs=pl.BlockSpec((1, MOD_ROWS, D_MODEL), lambda i, j: (i, 0, j)),
        compiler_params=_params(("arbitrary", "arbitrary")),
        name="modulation",
    )(cond, w_mod, b_mod.reshape(DEPTH, 1, N_MOD * D_MODEL))


def _swap_halves(x, half):
    lane = jax.lax.broadcasted_iota(i32, x.shape, 1)
    fwd = pltpu.roll(x, LANES - half, 1)
    bwd = pltpu.roll(x, half, 1)
    return jnp.where((lane & (2 * half - 1)) < half, fwd, bwd)


def _rope_block(x, cos, sin, half):
    return x * cos + _swap_halves(x, half) * sin


def _pre_kernel(rope, n_prev, resid, *refs):
    it = iter(refs)
    x_ref, mod_ref, g1_ref, w_in_ref, kvg_ref, qkg_ref, seg_ref, wka_ref, wuv_ref = (next(it) for _ in range(9))
    if resid:
        y_ref, pmod_ref = next(it), next(it)
    if rope:
        ca_ref, sa_ref, c32_ref, s32_ref, c64_ref, s64_ref = (next(it) for _ in range(6))
    prev_refs = [next(it) for _ in range(n_prev)]
    q_ref, kv_ref = next(it), next(it)
    if resid:
        x2_ref = next(it)
    cache_refs = [] if rope else [next(it) for _ in range(len(CACHE_WIDTHS))]
    w_scr = next(it)

    @pl.when(pl.program_id(0) == 0)
    def _():
        w_scr[0:IN_KR] = w_in_ref[0, 0:IN_KR].astype(bf16)
        w_scr[IN_KR:Z_KR] = w_in_ref[0, IN_KR + MLA_ROPE:IN_COLS].astype(bf16)
        w_scr[Z_KR:Z_KR + MLA_ROPE] = w_in_ref[0, IN_KR:IN_KR + MLA_ROPE].astype(bf16)
        w_scr[Z_KR + MLA_ROPE:Z_COLS] = jnp.zeros((Z_COLS - Z_KR - MLA_ROPE, D_MODEL), bf16)

    x = x_ref[...]
    if resid:
        x = x + pmod_ref[0, 0, 5:6, :] * y_ref[...]
        x2_ref[...] = x
    shift1 = mod_ref[0, 0, 0:1, :]
    scale1 = mod_ref[0, 0, 1:2, :]
    h = (_rms(x, D_MODEL) * g1_ref[0]) * (1.0 + scale1) + shift1
    z = _dot_nt(h.astype(bf16), w_scr[...])

    ckv = _rms(z[:, Z_CKV:Z_CKV + KV_RANK], KV_RANK) * kvg_ref[0]

    qk = z[:, Z_QG:Z_VG]
    sq = qk * qk
    sq_hi = sq.astype(bf16)
    sq_lo = (sq - sq_hi.astype(f32)).astype(bf16)
    seg = seg_ref[...]
    ms = jnp.concatenate(
        [_dot(jnp.concatenate([sq_hi[:, LANES * j:LANES * (j + 1)], sq_lo[:, LANES * j:LANES * (j + 1)]], axis=1), seg)
         for j in range(qk.shape[1] // LANES)], axis=1) * (1.0 / HEAD_DIM)
    qk = qk * jax.lax.rsqrt(ms + EPS) * qkg_ref[0]

    def blocks(arr, n):
        return [arr[:, LANES * j:LANES * (j + 1)] for j in range(n)]

    qa = blocks(z[:, Z_QA:Z_QA + 384], 3)
    qkb = blocks(qk, 4)
    qd = blocks(z[:, Z_QD:Z_QD + 256], 2)
    kd = blocks(z[:, Z_KD:Z_KD + 256], 2)
    kr = z[:, Z_KR:Z_KR + LANES]
    if rope:
        ca, sa, c32, s32, c64, s64 = (r[...] for r in (ca_ref, sa_ref, c32_ref, s32_ref, c64_ref, s64_ref))
        qa = [_rope_block(b, ca, sa, MLA_ROPE // 2) for b in qa]
        qkb = [_rope_block(b, c64, s64, HEAD_DIM // 2) for b in qkb]
        qd = [_rope_block(b, c32, s32, DIFF_QK // 2) for b in qd]
        kd = [_rope_block(b, c32, s32, DIFF_QK // 2) for b in kd]
        kr = _rope_block(kr, c32, s32, MLA_ROPE // 2)

    vg = z[:, Z_VG:Z_VG + 128]
    vd = z[:, Z_VD:Z_VD + 256]
    ckv_b = ckv.astype(bf16)
    k_a = _dot(jnp.concatenate([ckv_b, kr.astype(bf16)], axis=1), wka_ref[0])
    v_a = _dot(ckv_b, wuv_ref[0].astype(bf16))

    for j in range(3):
        q_ref[:, Q_A + LANES * j:Q_A + LANES * (j + 1)] = (qa[j] * (HEAD_DIM ** -0.5 * LOG2E)).astype(bf16)
        q_ref[:, Q_G + LANES * j:Q_G + LANES * (j + 1)] = (qkb[j] * (HEAD_DIM ** -0.5 * LOG2E)).astype(bf16)
    for j in range(2):
        q_ref[:, Q_D + LANES * j:Q_D + LANES * (j + 1)] = (qd[j] * (DIFF_QK ** -0.5 * LOG2E)).astype(bf16)
        kv_ref[:, KV_KD + LANES * j:KV_KD + LANES * (j + 1)] = kd[j].astype(bf16)
    kv_ref[:, KV_KA:KV_KA + 384] = k_a.astype(bf16)
    kv_ref[:, KV_VA:KV_VA + 384] = v_a.astype(bf16)
    kv_ref[:, KV_KG:KV_KG + 128] = qkb[3].astype(bf16)
    kv_ref[:, KV_VG:KV_VG + 128] = vg.astype(bf16)
    kv_ref[:, KV_VD:KV_VD + 256] = vd.astype(bf16)
    if not rope:
        rows = [ckv, None, qkb[3], vg, jnp.concatenate(kd, axis=1), vd]
        for out, new in zip(cache_refs, rows):
            if new is None:
                seq = out.shape[3]
                for r in range(out.shape[0]):
                    out[r, 0] = kr[r * seq:(r + 1) * seq].T[:MLA_ROPE, :]
            else:
                reqs, _, seq, width = out.shape
                out[:, 0] = new.reshape(reqs, seq, width)


def _pre_latent(layer, x, n, row0, seq, mod, mod_row, w, tile, rope_tabs, resid=None):
    lay = lambda t: (layer, 0, 0)
    row = lambda t: (t, 0)
    off = row0 // tile
    src_row = lambda t: (off + t, 0)
    in_specs = [
        pl.BlockSpec((tile, D_MODEL), src_row),
        pl.BlockSpec((1, 1, N_MOD, D_MODEL), lambda t: (layer, mod_row(t * tile), 0, 0)),
        pl.BlockSpec((1, 1, D_MODEL), lay),
        pl.BlockSpec((1, IN_COLS, D_MODEL), lay),
        pl.BlockSpec((1, 1, KV_RANK), lay),
        pl.BlockSpec((1, 1, 512), lay),
        pl.BlockSpec((2 * LANES, LANES), lambda t: (0, 0)),
        pl.BlockSpec((1, 256, 384), lay),
        pl.BlockSpec((1, KV_RANK, 384), lay),
    ]
    args = [x, mod, w["g1"], w["w_in"], w["kv_g"], w["qk_g"], w["seg"], w["w_ka"], w["w_uv"]]
    if resid is not None:
        in_specs += [pl.BlockSpec((tile, D_MODEL), src_row),
                     pl.BlockSpec((1, 1, N_MOD, D_MODEL), lambda t: (layer - 1, mod_row(t * tile), 0, 0))]
        args += [resid, mod]
    per_b = seq // tile
    in_specs += [pl.BlockSpec((tile, LANES), lambda t: (t % per_b, 0))] * 6
    args += list(rope_tabs)
    out_shape = [jax.ShapeDtypeStruct((n, Q_COLS), bf16), jax.ShapeDtypeStruct((n, KV_COLS), bf16)]
    out_specs = [pl.BlockSpec((tile, Q_COLS), row), pl.BlockSpec((tile, KV_COLS), row)]
    if resid is not None:
        out_shape.append(jax.ShapeDtypeStruct((n, D_MODEL), f32))
        out_specs.append(pl.BlockSpec((tile, D_MODEL), row))
    return pl.pallas_call(
        functools.partial(_pre_kernel, True, 0, resid is not None),
        out_shape=out_shape,
        grid=(n // tile,),
        in_specs=in_specs,
        out_specs=out_specs,
        scratch_shapes=[pltpu.VMEM((Z_COLS, D_MODEL), bf16)],
        compiler_params=_params(("arbitrary",)),
        name="pre_latent",
    )(*args)


PAST_CKV, PAST_KG, PAST_VG, PAST_KD, PAST_VD, PAST_KR, PAST_COLS = 0, 128, 256, 384, 640, 896, 928


def _cache_kernel(past_ref, wka_ref, wuv_ref, kv_ref):
    ckv_b = past_ref[0, 0, :, PAST_CKV:PAST_CKV + KV_RANK].astype(bf16)
    kr_b = past_ref[0, 0, :, PAST_KR:PAST_KR + MLA_ROPE].astype(bf16)
    wka = wka_ref[0]
    k_a = _dot(ckv_b, wka[:KV_RANK]) + _dot(kr_b, wka[KV_RANK:KV_RANK + MLA_ROPE])
    kv_ref[0, 0, :, KV_KA:KV_KA + 384] = k_a.astype(bf16)
    kv_ref[0, 0, :, KV_VA:KV_VA + 384] = _dot(ckv_b, wuv_ref[0].astype(bf16)).astype(bf16)
    kv_ref[0, 0, :, KV_KG:KV_KG + 128] = past_ref[0, 0, :, PAST_KG:PAST_KG + 128].astype(bf16)
    kv_ref[0, 0, :, KV_VG:KV_VG + 128] = past_ref[0, 0, :, PAST_VG:PAST_VG + 128].astype(bf16)
    kv_ref[0, 0, :, KV_KD:KV_KD + 256] = past_ref[0, 0, :, PAST_KD:PAST_KD + 256].astype(bf16)
    kv_ref[0, 0, :, KV_VD:KV_VD + 256] = past_ref[0, 0, :, PAST_VD:PAST_VD + 256].astype(bf16)


def _cache_rows(caches, w):
    ckv, kr, kg, vg, kd, vd = caches
    B = ckv.shape[0]
    flat = lambda a: a.reshape(B, DEPTH, PAST_LEN, -1)
    past = jnp.concatenate([flat(ckv), flat(kg), flat(vg), flat(kd), flat(vd), flat(kr)], axis=-1)
    return pl.pallas_call(
        _cache_kernel,
        out_shape=jax.ShapeDtypeStruct((DEPTH, B, PAST_LEN, KV_COLS), bf16),
        grid=(DEPTH, B),
        in_specs=[pl.BlockSpec((1, 1, PAST_LEN, PAST_COLS), lambda i, b: (b, i, 0, 0)),
                  pl.BlockSpec((1, 256, 384), lambda i, b: (i, 0, 0)), pl.BlockSpec((1, KV_RANK, 384), lambda i, b: (i, 0, 0))],
        out_specs=pl.BlockSpec((1, 1, PAST_LEN, KV_COLS), lambda i, b: (i, b, 0, 0)),
        compiler_params=_params(("arbitrary", "arbitrary")),
        name="cache_rows",
    )(past, w["w_ka"], w["w_uv"])


_SCORE_HEADS = (
    [(Q_A + 64 * h, KV_KA + 64 * h, 64, KV_VA + MLA_V * h) for h in range(MLA_HEADS)]
    + [(Q_G + 64 * h, KV_KG + 64 * (h // GQA_GROUP), 64, KV_VG + 64 * (h // GQA_GROUP)) for h in range(GQA_HEADS)]
    + [(Q_D + 64 * h + DIFF_QK * c, KV_KD + 64 * h + DIFF_QK * c, DIFF_QK, KV_VD + DIFF_V * h)
       for h in range(DIFF_HEADS) for c in range(2)])


def _attn_kernel(lam_init, per_round, n_src, q_ref, *refs):
    kv_refs = refs[:n_src]
    lam_ref, dg_ref, o_ref, s_ref, p_ref = refs[n_src:]
    spans, start = [], 0
    for r in kv_refs:
        spans.append((r, start, r.shape[2]))
        start += r.shape[2]
    mxu_sum = start >= MXU_SUM_MIN_KEYS

    outs = []
    for first in range(0, len(_SCORE_HEADS), per_round):
        chunk = _SCORE_HEADS[first:first + per_round]
        for j, (q_off, k_off, width, _) in enumerate(chunk):
            for r, lo, size in spans:
                s_ref[j, :, lo:lo + size] = _dot_nt(q_ref[:, q_off:q_off + width], r[0, 0, :, k_off:k_off + width])
        s = s_ref[...]
        p = jnp.exp2(s - jnp.max(s, axis=-1, keepdims=True))
        if mxu_sum:
            p_ref[...] = p.astype(bf16)
            for j, (_, _, _, v_off) in enumerate(chunk):
                o = sum(_dot(p_ref[j, :, lo:lo + size],
                             jnp.concatenate([r[0, 0, :, v_off:v_off + DIFF_V],
                                              jnp.ones((size, LANES - DIFF_V), bf16)], axis=1))
                        for r, lo, size in spans)
                outs.append((o * pltpu.roll(1.0 / o, DIFF_V, 1))[:, :DIFF_V])
        else:
            inv = 1.0 / jnp.sum(p, axis=-1, keepdims=True)
            p_ref[...] = p.astype(bf16)
            for j, (_, _, _, v_off) in enumerate(chunk):
                o = sum(_dot(p_ref[j, :, lo:lo + size], r[0, 0, :, v_off:v_off + DIFF_V]) for r, lo, size in spans)
                outs.append(o * inv[j])

    lp = lam_ref[0]
    e1 = jnp.exp(jnp.sum(lp[0:1] * lp[1:2], axis=-1, keepdims=True))
    e2 = jnp.exp(jnp.sum(lp[2:3] * lp[3:4], axis=-1, keepdims=True))
    lam = e1 - e2 + lam_init
    heads = outs[:MLA_HEADS + GQA_HEADS]
    for h in range(DIFF_HEADS):
        o1, o2 = outs[MLA_HEADS + GQA_HEADS + 2 * h:MLA_HEADS + GQA_HEADS + 2 * h + 2]
        heads.append(_rms(o1 - lam * o2, DIFF_V) * dg_ref[0] * (1.0 - lam_init))
    for j in range(len(heads) // 2):
        o_ref[:, LANES * j:LANES * (j + 1)] = jnp.concatenate(heads[2 * j:2 * j + 2], axis=1).astype(bf16)


def _attention(layer, q, sources, w, tile, per_round):
    n = q.shape[0]
    lam_init = 0.8 - 0.6 * math.exp(-0.3 * layer)
    s_kv = sum(rows for _, rows, _ in sources)
    assert len(_SCORE_HEADS) % per_round == 0
    return pl.pallas_call(
        functools.partial(_attn_kernel, lam_init, per_round, len(sources)),
        out_shape=jax.ShapeDtypeStruct((n, D_MODEL), bf16),
        scratch_shapes=[pltpu.VMEM((per_round, tile, s_kv), f32), pltpu.VMEM((per_round, tile, s_kv), bf16)],
        grid=(n // tile,),
        in_specs=[pl.BlockSpec((tile, Q_COLS), lambda t: (t, 0))]
        + [pl.BlockSpec((1, 1, rows, KV_COLS), index) for _, rows, index in sources]
        + [pl.BlockSpec((1, 4, DIFF_QK), lambda t: (layer, 0, 0)), pl.BlockSpec((1, 1, DIFF_V), lambda t: (layer, 0, 0))],
        out_specs=pl.BlockSpec((tile, D_MODEL), lambda t: (t, 0)),
        compiler_params=_params(("arbitrary",)),
        name="attention",
    )(q, *[arr for arr, _, _ in sources], w["lam"], w["diff_g"])


def _post_kernel(merge, *refs):
    it = iter(refs)
    o_ref, x_ref, mod_ref, w_out_ref, g2_ref, wg_ref, bg_ref, we_ref, be_ref = (next(it) for _ in range(9))
    if merge:
        next(it), next(it), next(it)
    x1_ref, hx_ref, cls_ref, w_scr = (next(it) for _ in range(4))

    @pl.when(pl.program_id(0) == 0)
    def _():
        w_scr[...] = w_out_ref[0].astype(bf16)

    gate1 = mod_ref[0, 0, 2:3, :]
    shift2 = mod_ref[0, 0, 3:4, :]
    scale2 = mod_ref[0, 0, 4:5, :]
    x1 = x_ref[...] + gate1 * _dot(o_ref[...], w_scr[...])
    x1_ref[...] = x1
    h2 = ((_rms(x1, D_MODEL) * g2_ref[0]) * (1.0 + scale2) + shift2).astype(bf16)
    bits = pltpu.bitcast(h2.astype(f32), i32)
    hx_ref[:, 0:HX_HALF] = bits[:, 0:HX_HALF] | jax.lax.shift_right_logical(bits[:, HX_HALF:D_MODEL], 16)

    def first_lane(mask, lane_f):
        return jnp.min(jnp.where(mask, lane_f, float(LANES)), axis=-1, keepdims=True)

    gl = _dot(h2, wg_ref[0].astype(bf16)) + bg_ref[0]
    glane = jax.lax.broadcasted_iota(i32, gl.shape, 1).astype(f32)
    ge = jnp.exp(gl - jnp.max(gl, axis=-1, keepdims=True))
    gprob = ge / jnp.sum(ge, axis=-1, keepdims=True)
    g_top = jnp.max(gprob, axis=-1, keepdims=True)
    g_idx = first_lane(gprob == g_top, glane)

    el = _dot(h2, we_ref[0].astype(bf16)) + be_ref[0]
    lane = jax.lax.broadcasted_iota(i32, el.shape, 1)
    lane_f = lane.astype(f32)
    emask = (lane >> 2).astype(f32) == g_idx
    em = jnp.where(emask, el, -jnp.inf)
    ee = jnp.where(emask, jnp.exp(em - jnp.max(em, axis=-1, keepdims=True)), 0.0)
    ep = ee / jnp.sum(ee, axis=-1, keepdims=True)
    p1 = jnp.max(jnp.where(emask, ep, -1.0), axis=-1, keepdims=True)
    i1 = first_lane(emask & (ep == p1), lane_f)
    rest = emask & (lane_f != i1)
    p2 = jnp.max(jnp.where(rest, ep, -1.0), axis=-1, keepdims=True)
    i2 = first_lane(rest & (ep == p2), lane_f)
    tot = p1 + p2
    w1 = g_top * (p1 / tot)
    w2 = g_top * (p2 / tot)

    lo = jnp.minimum(i1, i2) - EXPERTS_PER_GROUP * g_idx
    hi = jnp.maximum(i1, i2) - EXPERTS_PER_GROUP * g_idx
    pair = jnp.where(lo == 0.0, hi - 1.0, jnp.where(lo == 1.0, jnp.where(hi == 3.0, 3.0, 4.0), 5.0))
    cls_ref[...] = (N_PAIRS * g_idx + pair).astype(i32)
    g_lo = jnp.where(i1 < i2, w1, w2)
    g_hi = jnp.where(i1 < i2, w2, w1)
    tail_lane = jax.lax.broadcasted_iota(i32, (h2.shape[0], LANES), 1)
    hx_ref[:, HX_HALF:HX_COLS] = pltpu.bitcast(
        jnp.where(tail_lane == 0, g_lo, jnp.where(tail_lane == 1, g_hi, 0.0)), i32)


def _post(layer, o, x, row0, total, mod, mod_row, w, tile, merged=None):
    n = o.shape[0]
    lay = lambda t: (layer, 0, 0)
    row = lambda t: (t, 0)
    off = row0 // tile
    out_row = lambda t: (off + t, 0)
    in_specs = [
        pl.BlockSpec((tile, D_MODEL), row),
        pl.BlockSpec((tile, D_MODEL), row),
        pl.BlockSpec((1, 1, N_MOD, D_MODEL), lambda t: (layer, mod_row(t * tile), 0, 0)),
        pl.BlockSpec((1, D_MODEL, D_MODEL), lay),
        pl.BlockSpec((1, 1, D_MODEL), lay),
        pl.BlockSpec((1, D_MODEL, N_GROUPS), lay),
        pl.BlockSpec((1, 1, N_GROUPS), lay),
        pl.BlockSpec((1, D_MODEL, N_EXPERTS), lay),
        pl.BlockSpec((1, 1, N_EXPERTS), lay),
    ]
    args = [o, x, mod, w["w_out"], w["g2"], w["w_grp"], w["b_grp"], w["w_rtr"], w["b_rtr"]]
    aliases = {}
    if merged is not None:
        aliases = {len(args) + j: j for j in range(3)}
        in_specs += [pl.BlockSpec(memory_space=pl.ANY)] * 3
        args += list(merged)
    return pl.pallas_call(
        functools.partial(_post_kernel, merged is not None),
        out_shape=[jax.ShapeDtypeStruct((total, D_MODEL), f32), jax.ShapeDtypeStruct((total, HX_COLS), i32),
                   jax.ShapeDtypeStruct((total, 1), i32)],
        grid=(n // tile,),
        in_specs=in_specs,
        out_specs=[pl.BlockSpec((tile, D_MODEL), out_row), pl.BlockSpec((tile, HX_COLS), out_row),
                   pl.BlockSpec((tile, 1), out_row)],
        scratch_shapes=[pltpu.VMEM((D_MODEL, D_MODEL), bf16)],
        input_output_aliases=aliases,
        compiler_params=_params(("arbitrary",)),
        name="post_attention",
    )(*args)


def _ctx_kernel(lam_init, n_prev, resid, *refs):
    it = iter(refs)
    pre_in = [next(it) for _ in range(9 + (2 if resid else 0) + n_prev)]
    lam_ref, dg_ref = next(it), next(it)
    post_w = [next(it) for _ in range(6)]
    cache_refs = [next(it) for _ in range(len(CACHE_WIDTHS))]
    x1_ref, hx_ref, cls_ref = (next(it) for _ in range(3))
    w_in_scr, q_scr, kv_scr, o_scr, s_scr, p_scr, w_out_scr, x2_scr = (next(it) for _ in range(8))
    x_ref, mod_ref = pre_in[0], pre_in[1]
    x2_ref = [x2_scr] if resid else []

    _pre_kernel(False, n_prev, resid, *pre_in, q_scr, kv_scr.at[0, 0], *x2_ref, *cache_refs, w_in_scr)
    seq = s_scr.shape[1]
    for r in range(q_scr.shape[0] // seq):
        rows = pl.ds(r * seq, seq)
        _attn_kernel(lam_init, len(_SCORE_HEADS), 1, q_scr.at[rows], kv_scr.at[:, :, rows], lam_ref, dg_ref,
                     o_scr.at[rows], s_scr, p_scr)
    _post_kernel(False, o_scr, x2_scr if resid else x_ref, mod_ref, *post_w, x1_ref, hx_ref, cls_ref, w_out_scr)


def _ctx_layer(layer, x, n, seq, total, mod, w, prev_cache=(), resid=None):
    lay = lambda t: (layer, 0, 0)
    row = lambda t: (t, 0)
    reqs = CTX_REQUESTS_PER_STEP
    tile = reqs * seq
    mod_spec = lambda l: pl.BlockSpec((1, 1, N_MOD, D_MODEL), lambda t: (l, 0, 0, 0))
    in_specs = [
        pl.BlockSpec((tile, D_MODEL), row), mod_spec(layer),
        pl.BlockSpec((1, 1, D_MODEL), lay), pl.BlockSpec((1, IN_COLS, D_MODEL), lay),
        pl.BlockSpec((1, 1, KV_RANK), lay), pl.BlockSpec((1, 1, 512), lay), pl.BlockSpec((2 * LANES, LANES), lambda t: (0, 0)),
        pl.BlockSpec((1, 256, 384), lay), pl.BlockSpec((1, KV_RANK, 384), lay),
    ]
    args = [x, mod, w["g1"], w["w_in"], w["kv_g"], w["qk_g"], w["seg"], w["w_ka"], w["w_uv"]]
    if resid is not None:
        in_specs += [pl.BlockSpec((tile, D_MODEL), row), mod_spec(layer - 1)]
        args += [resid, mod]
    out_shape, out_specs, aliases = [], [], {}
    for j, width in enumerate(CACHE_WIDTHS):
        if prev_cache:
            aliases[len(args)] = len(out_shape)
            in_specs.append(pl.BlockSpec(memory_space=pl.ANY))
            args.append(prev_cache[j])
        shape = (MLA_ROPE, seq) if j == 1 else (seq, width)
        out_shape.append(jax.ShapeDtypeStruct((n // seq, DEPTH) + shape, f32))
        out_specs.append(pl.BlockSpec((reqs, 1) + shape, lambda t: (t, layer, 0, 0)))
    in_specs += [
        pl.BlockSpec((1, 4, DIFF_QK), lay), pl.BlockSpec((1, 1, DIFF_V), lay),
        pl.BlockSpec((1, D_MODEL, D_MODEL), lay), pl.BlockSpec((1, 1, D_MODEL), lay),
        pl.BlockSpec((1, D_MODEL, N_GROUPS), lay), pl.BlockSpec((1, 1, N_GROUPS), lay),
        pl.BlockSpec((1, D_MODEL, N_EXPERTS), lay), pl.BlockSpec((1, 1, N_EXPERTS), lay),
    ]
    args += [w["lam"], w["diff_g"], w["w_out"], w["g2"], w["w_grp"], w["b_grp"], w["w_rtr"], w["b_rtr"]]
    out_shape += [jax.ShapeDtypeStruct((total, D_MODEL), f32), jax.ShapeDtypeStruct((total, HX_COLS), i32),
                  jax.ShapeDtypeStruct((total, 1), i32)]
    out_specs += [pl.BlockSpec((tile, D_MODEL), row), pl.BlockSpec((tile, HX_COLS), row), pl.BlockSpec((tile, 1), row)]
    heads = len(_SCORE_HEADS)
    return pl.pallas_call(
        functools.partial(_ctx_kernel, 0.8 - 0.6 * math.exp(-0.3 * layer), len(prev_cache), resid is not None),
        out_shape=out_shape,
        grid=(n // tile,),
        in_specs=in_specs,
        out_specs=out_specs,
        scratch_shapes=[pltpu.VMEM((Z_COLS, D_MODEL), bf16), pltpu.VMEM((tile, Q_COLS), bf16),
                        pltpu.VMEM((1, 1, tile, KV_COLS), bf16), pltpu.VMEM((tile, D_MODEL), bf16),
                        pltpu.VMEM((heads, seq, seq), f32), pltpu.VMEM((heads, seq, seq), bf16),
                        pltpu.VMEM((D_MODEL, D_MODEL), bf16), pltpu.VMEM((tile, D_MODEL), f32)],
        input_output_aliases=aliases,
        compiler_params=_params(("arbitrary",)),
        name="context_layer",
    )(*args)


PLAN_CHUNK = 1024
TAB_ROWS = LANES


def _plan_kernel(cls_ref, slot_ref, tab_ref, rank_scr):
    n = cls_ref.shape[0]
    lane = jax.lax.broadcasted_iota(i32, (PLAN_CHUNK, LANES), 1)
    r = jax.lax.broadcasted_iota(i32, (LANES, LANES), 0)
    c = jax.lax.broadcasted_iota(i32, (LANES, LANES), 1)
    before = (c < r).astype(bf16)

    def count(b, seen):
        base = pl.multiple_of(b * PLAN_CHUNK, PLAN_CHUNK)
        onehot = (cls_ref[pl.ds(base, PLAN_CHUNK), :] == lane).astype(f32)
        for blk in range(PLAN_CHUNK // LANES):
            part = onehot[blk * LANES:(blk + 1) * LANES]
            ahead = _dot(before, part.astype(bf16)) + seen
            rank_scr[pl.ds(base + blk * LANES, LANES), :] = jnp.sum(part * ahead, axis=-1, keepdims=True)
            seen = seen + jnp.sum(part, axis=0, keepdims=True)
        return seen

    counts = jax.lax.fori_loop(0, n // PLAN_CHUNK, count, jnp.zeros((1, LANES), f32))
    tiles = jnp.floor((counts + (EXPERT_TILE - 1)) * (1.0 / EXPERT_TILE))
    rr = jax.lax.broadcasted_iota(i32, (LANES, LANES), 0)
    cc = jax.lax.broadcasted_iota(i32, (LANES, LANES), 1)
    ends = _dot(jnp.broadcast_to(tiles, (8, LANES)).astype(bf16), (rr <= cc).astype(bf16))[0:1]
    starts = ends - tiles

    def place(b, carry):
        rows = pl.ds(pl.multiple_of(b * PLAN_CHUNK, PLAN_CHUNK), PLAN_CHUNK)
        first = jnp.sum(jnp.where(cls_ref[rows, :] == lane, starts, 0.0), axis=-1, keepdims=True)
        slot_ref[rows, :] = (first * EXPERT_TILE + rank_scr[rows, :]).astype(i32)
        return carry

    jax.lax.fori_loop(0, n // PLAN_CHUNK, place, 0)

    tl = jax.lax.broadcasted_iota(i32, (TAB_ROWS, LANES), 1)
    n_tiles = jnp.sum(jnp.where(tl[0:1] == N_CLASSES - 1, ends, 0.0), axis=-1, keepdims=True)
    k = jnp.minimum(jax.lax.broadcasted_iota(i32, (TAB_ROWS, 1), 0).astype(f32), n_tiles - 1.0)
    cls_k = jnp.sum(jnp.where((tl < N_CLASSES) & (ends <= k), 1.0, 0.0), axis=-1, keepdims=True)
    cls_k = jnp.minimum(cls_k, N_CLASSES - 1.0)
    mine = tl.astype(f32) == cls_k
    used = jnp.sum(jnp.where(mine, counts, 0.0), axis=-1, keepdims=True)
    first = jnp.sum(jnp.where(mine, starts, 0.0), axis=-1, keepdims=True)
    valid = jnp.clip(used - (k - first) * EXPERT_TILE, 0.0, float(EXPERT_TILE))
    group = jnp.floor((cls_k + 0.5) * (1.0 / N_PAIRS))
    pair = cls_k - N_PAIRS * group
    lo = hi = jnp.zeros_like(pair)
    for p in range(N_PAIRS):
        lo = jnp.where(pair == p, float(PAIR_LO[p]), lo)
        hi = jnp.where(pair == p, float(PAIR_HI[p]), hi)
    e_lo = EXPERTS_PER_GROUP * group + lo
    e_hi = EXPERTS_PER_GROUP * group + hi
    tab = jnp.where(tl == 0, e_lo, jnp.where(tl == 1, e_hi, jnp.where(tl == 2, valid, jnp.where(tl == 3, n_tiles, 0.0))))
    tab_ref[...] = tab.astype(i32)


def _plan(cls, max_tiles):
    n = cls.shape[0]
    assert n % PLAN_CHUNK == 0 and max_tiles <= TAB_ROWS
    slot, tab = pl.pallas_call(
        _plan_kernel,
        out_shape=[jax.ShapeDtypeStruct((n, 1), i32), jax.ShapeDtypeStruct((TAB_ROWS, LANES), i32)],
        scratch_shapes=[pltpu.VMEM((n, 1), f32)],
        compiler_params=_params(None),
        name="dispatch_plan",
    )(cls)
    return slot.reshape(n), tab[:max_tiles, 0], tab[:max_tiles, 1], tab[:max_tiles, 2], tab[0, 3:4]


def _move_rows(src, idx, n_out, scatter):
    n = idx.shape[0]
    width = src.shape[1]
    per_worker = n // (SC_CORES * SC_SUBCORES)
    assert n % SC_ROWS == 0
    chunk = max(c for c in (64, 40, 32, 16, 8)
                if per_worker % c == 0 and 2 * c * width * src.dtype.itemsize <= SC_BUFFER_BYTES)
    n_chunks = per_worker // chunk
    mesh = plsc.VectorSubcoreMesh(core_axis_name="c", subcore_axis_name="s")

    @functools.partial(
        pl.kernel, mesh=mesh, out_type=jax.ShapeDtypeStruct((n_out, width), src.dtype),
        scratch_types=[pltpu.VMEM((chunk,), i32), pltpu.VMEM((chunk,), i32),
                       pltpu.VMEM((chunk, width), src.dtype), pltpu.VMEM((chunk, width), src.dtype),
                       pltpu.SemaphoreType.DMA, pltpu.SemaphoreType.DMA, pltpu.SemaphoreType.DMA,
                       pltpu.SemaphoreType.DMA])
    def move(src_hbm, idx_hbm, out_hbm, idx0, idx1, rows0, rows1, in0, in1, out0, out1):
        wid = jax.lax.axis_index("s") * SC_CORES + jax.lax.axis_index("c")
        base = wid * per_worker
        idx_v, rows_v, sem_in, sem_out = (idx0, idx1), (rows0, rows1), (in0, in1), (out0, out1)

        def fill(j):
            b = j % 2
            rows = pl.ds(base + j * chunk, chunk)
            pltpu.sync_copy(idx_hbm.at[rows], idx_v[b])
            src_rows = src_hbm.at[rows] if scatter else src_hbm.at[idx_v[b]]
            return pltpu.async_copy(src_rows, rows_v[b], sem_in[b])

        def drain(j):
            b = j % 2
            dst_rows = out_hbm.at[idx_v[b]] if scatter else out_hbm.at[pl.ds(base + j * chunk, chunk)]
            return pltpu.async_copy(rows_v[b], dst_rows, sem_out[b])

        fills, drains = {0: fill(0)}, {}
        for j in range(n_chunks):
            if j + 1 < n_chunks:
                if j >= 1:
                    drains[j - 1].wait()
                fills[j + 1] = fill(j + 1)
            fills[j].wait()
            drains[j] = drain(j)
        for j in range(max(n_chunks - 2, 0), n_chunks):
            drains[j].wait()

    return move(src, idx)


def _expert_kernel(lo_ref, hi_ref, valid_ref, nt_ref, xs_ref, wg_ref, wu_ref, wd_ref, ys_ref, wg_scr, wu_scr, wd_scr):
    k = pl.program_id(0)
    prev = jnp.maximum(k - 1, 0)

    @pl.when((k == 0) | (lo_ref[k] // EXPERTS_PER_GROUP != lo_ref[prev] // EXPERTS_PER_GROUP))
    def _():
        for j in range(EXPERTS_PER_GROUP):
            wg_scr[j] = wg_ref[0, 0, j].astype(bf16)
            wu_scr[j] = wu_ref[0, 0, j].astype(bf16)
            wd_scr[j] = wd_ref[0, 0, j].astype(bf16)

    @pl.when(k < nt_ref[0])
    def _():
        live = jax.lax.broadcasted_iota(i32, (EXPERT_TILE, 1), 0) < valid_ref[k]
        words = jnp.where(live, xs_ref[:, 0:HX_HALF], 0)
        x = jnp.concatenate([pltpu.bitcast(words & -65536, f32), pltpu.bitcast(words << 16, f32)], axis=1).astype(bf16)
        gates = pltpu.bitcast(jnp.where(live, xs_ref[:, HX_HALF:HX_COLS], 0), f32)
        y = None
        for lane, e_ref in enumerate((lo_ref, hi_ref)):
            j = e_ref[k] % EXPERTS_PER_GROUP
            hid = _silu(_dot(x, wg_scr[j])) * _dot(x, wu_scr[j]) * gates[:, lane:lane + 1]
            part = _dot(hid.astype(bf16), wd_scr[j])
            y = part if y is None else y + part
        ys_ref[...] = y


def _experts(layer, xs, e_lo, e_hi, valid, n_tiles, w):
    max_tiles = e_lo.shape[0]
    row = lambda k, lo, hi, valid, nt: (jnp.minimum(k, nt[0] - 1), 0)
    group = lambda k, lo, hi, valid, nt: (layer, lo[k] // EXPERTS_PER_GROUP, 0, 0, 0)
    by_group = lambda a: a.reshape(DEPTH, N_GROUPS, EXPERTS_PER_GROUP, *a.shape[2:])
    return pl.pallas_call(
        _expert_kernel,
        out_shape=jax.ShapeDtypeStruct((max_tiles * EXPERT_TILE, D_MODEL), f32),
        grid_spec=pltpu.PrefetchScalarGridSpec(
            num_scalar_prefetch=4, grid=(max_tiles,),
            in_specs=[pl.BlockSpec((EXPERT_TILE, HX_COLS), row),
                      pl.BlockSpec((1, 1, EXPERTS_PER_GROUP, D_MODEL, D_FF_EXPERT), group),
                      pl.BlockSpec((1, 1, EXPERTS_PER_GROUP, D_MODEL, D_FF_EXPERT), group),
                      pl.BlockSpec((1, 1, EXPERTS_PER_GROUP, D_FF_EXPERT, D_MODEL), group)],
            out_specs=pl.BlockSpec((EXPERT_TILE, D_MODEL), row),
            scratch_shapes=[pltpu.VMEM((EXPERTS_PER_GROUP, D_MODEL, D_FF_EXPERT), bf16),
                            pltpu.VMEM((EXPERTS_PER_GROUP, D_MODEL, D_FF_EXPERT), bf16),
                            pltpu.VMEM((EXPERTS_PER_GROUP, D_FF_EXPERT, D_MODEL), bf16)]),
        compiler_params=_params(("arbitrary",)),
        name="experts",
    )(e_lo, e_hi, valid, n_tiles, xs, by_group(w["w_gate"]), by_group(w["w_up"]), by_group(w["w_down"]))


def _final_kernel(x1_ref, y_ref, mod_ref, fg_ref, o_ref):
    o_ref[...] = _rms(x1_ref[...] + mod_ref[0, 0, 5:6, :] * y_ref[...], D_MODEL) * fg_ref[...]


def _final(x1, y, n, row0, mod, mod_row, w, tile):
    off = row0 // tile
    src_row = lambda t: (off + t, 0)
    return pl.pallas_call(
        _final_kernel,
        out_shape=jax.ShapeDtypeStruct((n, D_MODEL), f32),
        grid=(n // tile,),
        in_specs=[pl.BlockSpec((tile, D_MODEL), src_row), pl.BlockSpec((tile, D_MODEL), src_row),
                  pl.BlockSpec((1, 1, N_MOD, D_MODEL), lambda t: (DEPTH - 1, mod_row(t * tile), 0, 0)),
                  pl.BlockSpec((1, D_MODEL), lambda t: (0, 0))],
        out_specs=pl.BlockSpec((tile, D_MODEL), lambda t: (t, 0)),
        compiler_params=_params(("arbitrary",)),
        name="final_norm",
    )(x1, y, mod, w["final_g"])


def _rope_tables(n_tokens):
    pos = np.arange(n_tokens)
    row = (pos // GRID_W).astype(np.float64)
    col = (pos % GRID_W).astype(np.float64)

    def cs(rot_dim):
        quarter = rot_dim // 4
        inv = ROPE_THETA ** (-np.arange(quarter, dtype=np.float64) / quarter)
        ang = np.concatenate([row[:, None] * inv, col[:, None] * inv], axis=-1)
        return np.cos(ang), np.sin(ang)

    c32, s32 = cs(MLA_ROPE)
    c64, s64 = cs(HEAD_DIM)
    ones = np.ones((n_tokens, MLA_NOPE))
    zeros = np.zeros((n_tokens, MLA_NOPE))

    def rep(parts):
        period = np.concatenate(parts, axis=-1)
        return jnp.asarray(np.tile(period, (1, LANES // period.shape[-1])), f32)

    return (rep([ones, c32, c32]), rep([zeros, -s32, s32]), rep([c32, c32]), rep([-s32, s32]),
            rep([c64, c64]), rep([-s64, s64]))


def _layout_weights(norm1_g, norm2_g, w_in, mla_kv_norm_g, mla_w_uk, mla_w_uv, gqa_q_norm_g, gqa_k_norm_g,
                    diff_lambda, diff_norm_g, w_out, moe_w_group, moe_b_group, moe_w_router, moe_b_router,
                    moe_w_gate, moe_w_up, moe_w_down, final_norm_g):
    eye = jnp.eye(MLA_ROPE, dtype=f32)
    top = jnp.concatenate([mla_w_uk, jnp.zeros((DEPTH, KV_RANK, MLA_HEADS, MLA_ROPE), f32)], axis=-1)
    mid = jnp.concatenate([jnp.zeros((MLA_ROPE, MLA_HEADS, MLA_NOPE), f32),
                           jnp.broadcast_to(eye[:, None, :], (MLA_ROPE, MLA_HEADS, MLA_ROPE))], axis=-1)
    w_ka = jnp.concatenate([top.reshape(DEPTH, KV_RANK, 384),
                            jnp.broadcast_to(mid.reshape(1, MLA_ROPE, 384), (DEPTH, MLA_ROPE, 384)),
                            jnp.zeros((DEPTH, 256 - KV_RANK - MLA_ROPE, 384), f32)], axis=1).astype(bf16)
    seg_id = np.arange(LANES) // HEAD_DIM
    seg = jnp.asarray(np.tile(seg_id[:, None] == seg_id[None, :], (2, 1)), bf16)
    qk_g = jnp.concatenate([jnp.tile(gqa_q_norm_g, (1, GQA_HEADS)), jnp.tile(gqa_k_norm_g, (1, GQA_KV_HEADS))], axis=-1)
    return dict(
        g1=norm1_g.reshape(DEPTH, 1, D_MODEL), g2=norm2_g.reshape(DEPTH, 1, D_MODEL),
        w_in=jnp.swapaxes(w_in, 1, 2),
        kv_g=mla_kv_norm_g.reshape(DEPTH, 1, KV_RANK), qk_g=qk_g.reshape(DEPTH, 1, 512), seg=seg, w_ka=w_ka,
        w_uv=mla_w_uv.reshape(DEPTH, KV_RANK, 384), lam=diff_lambda, diff_g=diff_norm_g.reshape(DEPTH, 1, DIFF_V),
        w_out=w_out, w_grp=moe_w_group, b_grp=moe_b_group.reshape(DEPTH, 1, N_GROUPS), w_rtr=moe_w_router,
        b_rtr=moe_b_router.reshape(DEPTH, 1, N_EXPERTS), w_gate=moe_w_gate, w_up=moe_w_up, w_down=moe_w_down,
        final_g=final_norm_g.reshape(1, D_MODEL))


PRE_TILE = 512
CTX_REQUESTS_PER_STEP = 2
LAT_ATTN_TILE = 256
LAT_HEADS_PER_ROUND = 10
MXU_SUM_MIN_KEYS = 1024
POST_TILE = 512
FINAL_TILE = 1024


def kernel(x_prompt, x_sample, c, cache_mla_ckv, cache_mla_krope, cache_gqa_k, cache_gqa_v, cache_diff_k, cache_diff_v, c_ctx, norm1_g, norm2_g, w_mod, b_mod, w_in, mla_kv_norm_g, mla_w_uk, mla_w_uv, gqa_q_norm_g, gqa_k_norm_g, diff_lambda, diff_norm_g, w_out, moe_w_group, moe_b_group, moe_w_router, moe_b_router, moe_w_gate, moe_w_up, moe_w_down, final_norm_g):
    B, S, _ = x_prompt.shape
    Bl, Sl, _ = x_sample.shape
    n_ctx, n_lat = B * S, Bl * Sl
    total = n_ctx + n_lat
    assert Bl + 1 <= MOD_ROWS and DEPTH == 2 and total % SC_ROWS == 0
    slot_rows = -(-(total + N_CLASSES * EXPERT_TILE) // SC_ROWS) * SC_ROWS
    max_tiles = slot_rows // EXPERT_TILE
    w = _layout_weights(norm1_g, norm2_g, w_in, mla_kv_norm_g, mla_w_uk, mla_w_uv, gqa_q_norm_g, gqa_k_norm_g,
                        diff_lambda, diff_norm_g, w_out, moe_w_group, moe_b_group, moe_w_router, moe_b_router,
                        moe_w_gate, moe_w_up, moe_w_down, final_norm_g)
    cond = jnp.concatenate([c_ctx[None, :], c, jnp.zeros((MOD_ROWS - 1 - Bl, D_MODEL), f32)], axis=0)
    mod = _modulation(cond, w_mod, b_mod).reshape(DEPTH, MOD_ROWS, N_MOD, D_MODEL)
    ctx_row = lambda token: 0
    lat_row = lambda token: 1 + token // Sl
    tabs = _rope_tables(Sl)
    kv_past = _cache_rows((cache_mla_ckv, cache_mla_krope, cache_gqa_k, cache_gqa_v, cache_diff_k, cache_diff_v), w)
    per_b = Sl // LAT_ATTN_TILE

    x_ctx, x_lat = x_prompt.reshape(n_ctx, D_MODEL), x_sample.reshape(n_lat, D_MODEL)
    cache = ()
    x1 = y = None
    for i in range(DEPTH):
        if i == 0:
            *cache, x1_c, hx_c, cls_c = _ctx_layer(i, x_ctx, n_ctx, S, total, mod, w)
        else:
            *cache, x1_c, hx_c, cls_c = _ctx_layer(i, x1, n_ctx, S, total, mod, w, prev_cache=cache, resid=y)
        if i == 0:
            q_l, kv_l = _pre_latent(i, x_lat, n_lat, 0, Sl, mod, lat_row, w, PRE_TILE, tabs)
        else:
            q_l, kv_l, x_lat = _pre_latent(i, x1, n_lat, n_ctx, Sl, mod, lat_row, w, PRE_TILE, tabs, resid=y)
        past = (kv_past, PAST_LEN, lambda t, i=i: (i, t // per_b, 0, 0))
        own = (kv_l.reshape(1, Bl, Sl, KV_COLS), Sl, lambda t: (0, t // per_b, 0, 0))
        o_l = _attention(i, q_l, [past, own], w, LAT_ATTN_TILE, LAT_HEADS_PER_ROUND)
        x1, hx, cls = _post(i, o_l, x_lat, n_ctx, total, mod, lat_row, w, POST_TILE, merged=(x1_c, hx_c, cls_c))
        if i == DEPTH - 1:
            outs = (cache[0], jnp.swapaxes(cache[1], 2, 3), cache[2].reshape(B, DEPTH, S, GQA_KV_HEADS, HEAD_DIM),
                    cache[3].reshape(B, DEPTH, S, GQA_KV_HEADS, HEAD_DIM),
                    cache[4].reshape(B, DEPTH, S, DIFF_HEADS, 2, DIFF_QK),
                    cache[5].reshape(B, DEPTH, S, DIFF_HEADS, DIFF_V))
            cls, outs = jax.lax.optimization_barrier((cls, outs))
        slot, e_lo, e_hi, valid, n_tiles = _plan(cls, max_tiles)
        xs = _move_rows(hx, slot, slot_rows, scatter=True)
        ys = _experts(i, xs, e_lo, e_hi, valid, n_tiles, w)
        y = _move_rows(ys, slot, total, scatter=False)

    y_prompt = _final(x1, y, n_ctx, 0, mod, ctx_row, w, FINAL_TILE).reshape(B, S, D_MODEL)
    y_sample = _final(x1, y, n_lat, n_ctx, mod, lat_row, w, FINAL_TILE).reshape(Bl, Sl, D_MODEL)
    return (y_prompt, y_sample, *outs)
```

```python
import functools
import math

import jax
import jax.numpy as jnp
import numpy as np
from jax.experimental import pallas as pl
from jax.experimental.pallas import tpu as pltpu
from jax.experimental.pallas import tpu_sc as plsc

D_MODEL = 1024
DEPTH = 2
PAST_LEN = 512
GRID_W = 64
ROPE_THETA = 10000.0
EPS = 1e-6
LOG2E = 1.4426950408889634
N_MOD = 6
HEAD_DIM = 64
MLA_HEADS = 6
MLA_NOPE = 32
MLA_ROPE = 32
MLA_V = 64
KV_RANK = 128
GQA_HEADS = 6
GQA_KV_HEADS = 2
GQA_GROUP = GQA_HEADS // GQA_KV_HEADS
DIFF_HEADS = 4
DIFF_QK = 32
DIFF_V = 64
N_GROUPS = 4
EXPERTS_PER_GROUP = 4
N_EXPERTS = N_GROUPS * EXPERTS_PER_GROUP
D_FF_EXPERT = 256

LANES = 128
MOD_ROWS = 8

IN_COLS = 1952
IN_KR = 512
Z_QA, Z_CKV, Z_QG, Z_KG, Z_VG, Z_QD, Z_KD, Z_VD, Z_KR = 0, 384, 512, 896, 1024, 1152, 1408, 1664, 1920
Z_COLS = 2048
Q_A, Q_G, Q_D, Q_COLS = 0, 384, 768, 1024
KV_KA, KV_VA, KV_KG, KV_VG, KV_KD, KV_VD, KV_COLS = 0, 384, 768, 896, 1024, 1280, 1536
CACHE_WIDTHS = (128, 32, 128, 128, 256, 256)

PAIR_LO = (0, 0, 0, 1, 1, 2)
PAIR_HI = (1, 2, 3, 3, 2, 3)
N_PAIRS = len(PAIR_LO)
N_CLASSES = N_GROUPS * N_PAIRS
HX_HALF = D_MODEL // 2
HX_COLS = HX_HALF + LANES
EXPERT_TILE = 256

SC_CORES, SC_SUBCORES = 2, 16
SC_BUFFER_BYTES = 400 * 1024
SC_ROWS = SC_CORES * SC_SUBCORES * 8

VMEM_LIMIT = 56 * 1024 * 1024

bf16 = jnp.bfloat16
f32 = jnp.float32
i32 = jnp.int32


def _dot(a, b):
    return jnp.dot(a, b, preferred_element_type=f32)


def _dot_nt(a, b):
    return jax.lax.dot_general(a, b, (((1,), (1,)), ((), ())), preferred_element_type=f32)


def _rms(x, width):
    return x * jax.lax.rsqrt(jnp.sum(x * x, axis=-1, keepdims=True) * (1.0 / width) + EPS)


def _silu(x):
    return x * (1.0 / (1.0 + jnp.exp(-x)))


def _params(sem):
    return pltpu.CompilerParams(dimension_semantics=sem, vmem_limit_bytes=VMEM_LIMIT)


def _mod_kernel(cond_ref, w_ref, b_ref, o_ref):
    o_ref[0] = _dot(_silu(cond_ref[...]).astype(bf16), w_ref[0].astype(bf16)) + b_ref[0]


def _modulation(cond, w_mod, b_mod):
    return pl.pallas_call(
        _mod_kernel,
        out_shape=jax.ShapeDtypeStruct((DEPTH, MOD_ROWS, N_MOD * D_MODEL), f32),
        grid=(DEPTH, N_MOD),
        in_specs=[
            pl.BlockSpec((MOD_ROWS, D_MODEL), lambda i, j: (0, 0)),
            pl.BlockSpec((1, D_MODEL, D_MODEL), lambda i, j: (i, 0, j)),
            pl.BlockSpec((1, 1, D_MODEL), lambda i, j: (i, 0, j)),
        ],
        out_specs=pl.BlockSpec((1, MOD_ROWS, D_MODEL), lambda i, j: (i, 0, j)),
        compiler_params=_params(("arbitrary", "arbitrary")),
        name="modulation",
    )(cond, w_mod, b_mod.reshape(DEPTH, 1, N_MOD * D_MODEL))


def _swap_halves(x, half):
    lane = jax.lax.broadcasted_iota(i32, x.shape, 1)
    fwd = pltpu.roll(x, LANES - half, 1)
    bwd = pltpu.roll(x, half, 1)
    return jnp.where((lane & (2 * half - 1)) < half, fwd, bwd)


def _rope_block(x, cos, sin, half):
    return x * cos + _swap_halves(x, half) * sin


def _pre_kernel(rope, n_prev, resid, *refs):
    it = iter(refs)
    x_ref, mod_ref, g1_ref, w_in_ref, kvg_ref, qkg_ref, seg_ref, wka_ref, wuv_ref = (next(it) for _ in range(9))
    if resid:
        y_ref, pmod_ref = next(it), next(it)
    if rope:
        ca_ref, sa_ref, c32_ref, s32_ref, c64_ref, s64_ref = (next(it) for _ in range(6))
    prev_refs = [next(it) for _ in range(n_prev)]
    q_ref, kv_ref = next(it), next(it)
    if resid:
        x2_ref = next(it)
    cache_refs = [] if rope else [next(it) for _ in range(len(CACHE_WIDTHS))]
    w_scr = next(it)

    @pl.when(pl.program_id(0) == 0)
    def _():
        w_scr[0:IN_KR] = w_in_ref[0, 0:IN_KR].astype(bf16)
        w_scr[IN_KR:Z_KR] = w_in_ref[0, IN_KR + MLA_ROPE:IN_COLS].astype(bf16)
        w_scr[Z_KR:Z_KR + MLA_ROPE] = w_in_ref[0, IN_KR:IN_KR + MLA_ROPE].astype(bf16)
        w_scr[Z_KR + MLA_ROPE:Z_COLS] = jnp.zeros((Z_COLS - Z_KR - MLA_ROPE, D_MODEL), bf16)

    x = x_ref[...]
    if resid:
        x = x + pmod_ref[0, 0, 5:6, :] * y_ref[...]
        x2_ref[...] = x
    shift1 = mod_ref[0, 0, 0:1, :]
    scale1 = mod_ref[0, 0, 1:2, :]
    h = (_rms(x, D_MODEL) * g1_ref[0]) * (1.0 + scale1) + shift1
    z = _dot_nt(h.astype(bf16), w_scr[...])

    ckv = _rms(z[:, Z_CKV:Z_CKV + KV_RANK], KV_RANK) * kvg_ref[0]

    qk = z[:, Z_QG:Z_VG]
    sq = qk * qk
    sq_hi = sq.astype(bf16)
    sq_lo = (sq - sq_hi.astype(f32)).astype(bf16)
    seg = seg_ref[...]
    ms = jnp.concatenate(
        [_dot(jnp.concatenate([sq_hi[:, LANES * j:LANES * (j + 1)], sq_lo[:, LANES * j:LANES * (j + 1)]], axis=1), seg)
         for j in range(qk.shape[1] // LANES)], axis=1) * (1.0 / HEAD_DIM)
    qk = qk * jax.lax.rsqrt(ms + EPS) * qkg_ref[0]

    def blocks(arr, n):
        return [arr[:, LANES * j:LANES * (j + 1)] for j in range(n)]

    qa = blocks(z[:, Z_QA:Z_QA + 384], 3)
    qkb = blocks(qk, 4)
    qd = blocks(z[:, Z_QD:Z_QD + 256], 2)
    kd = blocks(z[:, Z_KD:Z_KD + 256], 2)
    kr = z[:, Z_KR:Z_KR + LANES]
    if rope:
        ca, sa, c32, s32, c64, s64 = (r[...] for r in (ca_ref, sa_ref, c32_ref, s32_ref, c64_ref, s64_ref))
        qa = [_rope_block(b, ca, sa, MLA_ROPE // 2) for b in qa]
        qkb = [_rope_block(b, c64, s64, HEAD_DIM // 2) for b in qkb]
        qd = [_rope_block(b, c32, s32, DIFF_QK // 2) for b in qd]
        kd = [_rope_block(b, c32, s32, DIFF_QK // 2) for b in kd]
        kr = _rope_block(kr, c32, s32, MLA_ROPE // 2)

    vg = z[:, Z_VG:Z_VG + 128]
    vd = z[:, Z_VD:Z_VD + 256]
    ckv_b = ckv.astype(bf16)
    k_a = _dot(jnp.concatenate([ckv_b, kr.astype(bf16)], axis=1), wka_ref[0])
    v_a = _dot(ckv_b, wuv_ref[0].astype(bf16))

    for j in range(3):
        q_ref[:, Q_A + LANES * j:Q_A + LANES * (j + 1)] = (qa[j] * (HEAD_DIM ** -0.5 * LOG2E)).astype(bf16)
        q_ref[:, Q_G + LANES * j:Q_G + LANES * (j + 1)] = (qkb[j] * (HEAD_DIM ** -0.5 * LOG2E)).astype(bf16)
    for j in range(2):
        q_ref[:, Q_D + LANES * j:Q_D + LANES * (j + 1)] = (qd[j] * (DIFF_QK ** -0.5 * LOG2E)).astype(bf16)
        kv_ref[:, KV_KD + LANES * j:KV_KD + LANES * (j + 1)] = kd[j].astype(bf16)
    kv_ref[:, KV_KA:KV_KA + 384] = k_a.astype(bf16)
    kv_ref[:, KV_VA:KV_VA + 384] = v_a.astype(bf16)
    kv_ref[:, KV_KG:KV_KG + 128] = qkb[3].astype(bf16)
    kv_ref[:, KV_VG:KV_VG + 128] = vg.astype(bf16)
    kv_ref[:, KV_VD:KV_VD + 256] = vd.astype(bf16)
    if not rope:
        rows = [ckv, None, qkb[3], vg, jnp.concatenate(kd, axis=1), vd]
        for out, new in zip(cache_refs, rows):
            if new is None:
                seq = out.shape[3]
                for r in range(out.shape[0]):
                    out[r, 0] = kr[r * seq:(r + 1) * seq].T[:MLA_ROPE, :]
            else:
                reqs, _, seq, width = out.shape
                out[:, 0] = new.reshape(reqs, seq, width)


def _pre_latent(layer, x, n, row0, seq, mod, mod_row, w, tile, rope_tabs, resid=None):
    lay = lambda t: (layer, 0, 0)
    row = lambda t: (t, 0)
    off = row0 // tile
    src_row = lambda t: (off + t, 0)
    in_specs = [
        pl.BlockSpec((tile, D_MODEL), src_row),
        pl.BlockSpec((1, 1, N_MOD, D_MODEL), lambda t: (layer, mod_row(t * tile), 0, 0)),
        pl.BlockSpec((1, 1, D_MODEL), lay),
        pl.BlockSpec((1, IN_COLS, D_MODEL), lay),
        pl.BlockSpec((1, 1, KV_RANK), lay),
        pl.BlockSpec((1, 1, 512), lay),
        pl.BlockSpec((2 * LANES, LANES), lambda t: (0, 0)),
        pl.BlockSpec((1, 256, 384), lay),
        pl.BlockSpec((1, KV_RANK, 384), lay),
    ]
    args = [x, mod, w["g1"], w["w_in"], w["kv_g"], w["qk_g"], w["seg"], w["w_ka"], w["w_uv"]]
    if resid is not None:
        in_specs += [pl.BlockSpec((tile, D_MODEL), src_row),
                     pl.BlockSpec((1, 1, N_MOD, D_MODEL), lambda t: (layer - 1, mod_row(t * tile), 0, 0))]
        args += [resid, mod]
    per_b = seq // tile
    in_specs += [pl.BlockSpec((tile, LANES), lambda t: (t % per_b, 0))] * 6
    args += list(rope_tabs)
    out_shape = [jax.ShapeDtypeStruct((n, Q_COLS), bf16), jax.ShapeDtypeStruct((n, KV_COLS), bf16)]
    out_specs = [pl.BlockSpec((tile, Q_COLS), row), pl.BlockSpec((tile, KV_COLS), row)]
    if resid is not None:
        out_shape.append(jax.ShapeDtypeStruct((n, D_MODEL), f32))
        out_specs.append(pl.BlockSpec((tile, D_MODEL), row))
    return pl.pallas_call(
        functools.partial(_pre_kernel, True, 0, resid is not None),
        out_shape=out_shape,
        grid=(n // tile,),
        in_specs=in_specs,
        out_specs=out_specs,
        scratch_shapes=[pltpu.VMEM((Z_COLS, D_MODEL), bf16)],
        compiler_params=_params(("arbitrary",)),
        name="pre_latent",
    )(*args)


PAST_CKV, PAST_KG, PAST_VG, PAST_KD, PAST_VD, PAST_KR, PAST_COLS = 0, 128, 256, 384, 640, 896, 928


def _cache_kernel(past_ref, wka_ref, wuv_ref, kv_ref):
    ckv_b = past_ref[0, 0, :, PAST_CKV:PAST_CKV + KV_RANK].astype(bf16)
    kr_b = past_ref[0, 0, :, PAST_KR:PAST_KR + MLA_ROPE].astype(bf16)
    wka = wka_ref[0]
    k_a = _dot(ckv_b, wka[:KV_RANK]) + _dot(kr_b, wka[KV_RANK:KV_RANK + MLA_ROPE])
    kv_ref[0, 0, :, KV_KA:KV_KA + 384] = k_a.astype(bf16)
    kv_ref[0, 0, :, KV_VA:KV_VA + 384] = _dot(ckv_b, wuv_ref[0].astype(bf16)).astype(bf16)
    kv_ref[0, 0, :, KV_KG:KV_KG + 128] = past_ref[0, 0, :, PAST_KG:PAST_KG + 128].astype(bf16)
    kv_ref[0, 0, :, KV_VG:KV_VG + 128] = past_ref[0, 0, :, PAST_VG:PAST_VG + 128].astype(bf16)
    kv_ref[0, 0, :, KV_KD:KV_KD + 256] = past_ref[0, 0, :, PAST_KD:PAST_KD + 256].astype(bf16)
    kv_ref[0, 0, :, KV_VD:KV_VD + 256] = past_ref[0, 0, :, PAST_VD:PAST_VD + 256].astype(bf16)


def _cache_rows(caches, w):
    ckv, kr, kg, vg, kd, vd = caches
    B = ckv.shape[0]
    flat = lambda a: a.reshape(B, DEPTH, PAST_LEN, -1)
    past = jnp.concatenate([flat(ckv), flat(kg), flat(vg), flat(kd), flat(vd), flat(kr)], axis=-1)
    return pl.pallas_call(
        _cache_kernel,
        out_shape=jax.ShapeDtypeStruct((DEPTH, B, PAST_LEN, KV_COLS), bf16),
        grid=(DEPTH, B),
        in_specs=[pl.BlockSpec((1, 1, PAST_LEN, PAST_COLS), lambda i, b: (b, i, 0, 0)),
                  pl.BlockSpec((1, 256, 384), lambda i, b: (i, 0, 0)), pl.BlockSpec((1, KV_RANK, 384), lambda i, b: (i, 0, 0))],
        out_specs=pl.BlockSpec((1, 1, PAST_LEN, KV_COLS), lambda i, b: (i, b, 0, 0)),
        compiler_params=_params(("arbitrary", "arbitrary")),
        name="cache_rows",
    )(past, w["w_ka"], w["w_uv"])


_SCORE_HEADS = (
    [(Q_A + 64 * h, KV_KA + 64 * h, 64, KV_VA + MLA_V * h) for h in range(MLA_HEADS)]
    + [(Q_G + 64 * h, KV_KG + 64 * (h // GQA_GROUP), 64, KV_VG + 64 * (h // GQA_GROUP)) for h in range(GQA_HEADS)]
    + [(Q_D + 64 * h + DIFF_QK * c, KV_KD + 64 * h + DIFF_QK * c, DIFF_QK, KV_VD + DIFF_V * h)
       for h in range(DIFF_HEADS) for c in range(2)])


def _attn_kernel(lam_init, per_round, n_src, q_ref, *refs):
    kv_refs = refs[:n_src]
    lam_ref, dg_ref, o_ref, s_ref, p_ref = refs[n_src:]
    spans, start = [], 0
    for r in kv_refs:
        spans.append((r, start, r.shape[2]))
        start += r.shape[2]
    mxu_sum = start >= MXU_SUM_MIN_KEYS

    outs = []
    for first in range(0, len(_SCORE_HEADS), per_round):
        chunk = _SCORE_HEADS[first:first + per_round]
        for j, (q_off, k_off, width, _) in enumerate(chunk):
            for r, lo, size in spans:
                s_ref[j, :, lo:lo + size] = _dot_nt(q_ref[:, q_off:q_off + width], r[0, 0, :, k_off:k_off + width])
        s = s_ref[...]
        p = jnp.exp2(s - jnp.max(s, axis=-1, keepdims=True))
        if mxu_sum:
            p_ref[...] = p.astype(bf16)
            for j, (_, _, _, v_off) in enumerate(chunk):
                o = sum(_dot(p_ref[j, :, lo:lo + size],
                             jnp.concatenate([r[0, 0, :, v_off:v_off + DIFF_V],
                                              jnp.ones((size, LANES - DIFF_V), bf16)], axis=1))
                        for r, lo, size in spans)
                outs.append((o * pltpu.roll(1.0 / o, DIFF_V, 1))[:, :DIFF_V])
        else:
            inv = 1.0 / jnp.sum(p, axis=-1, keepdims=True)
            p_ref[...] = p.astype(bf16)
            for j, (_, _, _, v_off) in enumerate(chunk):
                o = sum(_dot(p_ref[j, :, lo:lo + size], r[0, 0, :, v_off:v_off + DIFF_V]) for r, lo, size in spans)
                outs.append(o * inv[j])

    lp = lam_ref[0]
    e1 = jnp.exp(jnp.sum(lp[0:1] * lp[1:2], axis=-1, keepdims=True))
    e2 = jnp.exp(jnp.sum(lp[2:3] * lp[3:4], axis=-1, keepdims=True))
    lam = e1 - e2 + lam_init
    heads = outs[:MLA_HEADS + GQA_HEADS]
    for h in range(DIFF_HEADS):
        o1, o2 = outs[MLA_HEADS + GQA_HEADS + 2 * h:MLA_HEADS + GQA_HEADS + 2 * h + 2]
        heads.append(_rms(o1 - lam * o2, DIFF_V) * dg_ref[0] * (1.0 - lam_init))
    for j in range(len(heads) // 2):
        o_ref[:, LANES * j:LANES * (j + 1)] = jnp.concatenate(heads[2 * j:2 * j + 2], axis=1).astype(bf16)


def _attention(layer, q, sources, w, tile, per_round):
    n = q.shape[0]
    lam_init = 0.8 - 0.6 * math.exp(-0.3 * layer)
    s_kv = sum(rows for _, rows, _ in sources)
    assert len(_SCORE_HEADS) % per_round == 0
    return pl.pallas_call(
        functools.partial(_attn_kernel, lam_init, per_round, len(sources)),
        out_shape=jax.ShapeDtypeStruct((n, D_MODEL), bf16),
        scratch_shapes=[pltpu.VMEM((per_round, tile, s_kv), f32), pltpu.VMEM((per_round, tile, s_kv), bf16)],
        grid=(n // tile,),
        in_specs=[pl.BlockSpec((tile, Q_COLS), lambda t: (t, 0))]
        + [pl.BlockSpec((1, 1, rows, KV_COLS), index) for _, rows, index in sources]
        + [pl.BlockSpec((1, 4, DIFF_QK), lambda t: (layer, 0, 0)), pl.BlockSpec((1, 1, DIFF_V), lambda t: (layer, 0, 0))],
        out_specs=pl.BlockSpec((tile, D_MODEL), lambda t: (t, 0)),
        compiler_params=_params(("arbitrary",)),
        name="attention",
    )(q, *[arr for arr, _, _ in sources], w["lam"], w["diff_g"])


def _post_kernel(merge, *refs):
    it = iter(refs)
    o_ref, x_ref, mod_ref, w_out_ref, g2_ref, wg_ref, bg_ref, we_ref, be_ref = (next(it) for _ in range(9))
    if merge:
        next(it), next(it), next(it)
    x1_ref, hx_ref, cls_ref, w_scr = (next(it) for _ in range(4))

    @pl.when(pl.program_id(0) == 0)
    def _():
        w_scr[...] = w_out_ref[0].astype(bf16)

    gate1 = mod_ref[0, 0, 2:3, :]
    shift2 = mod_ref[0, 0, 3:4, :]
    scale2 = mod_ref[0, 0, 4:5, :]
    x1 = x_ref[...] + gate1 * _dot(o_ref[...], w_scr[...])
    x1_ref[...] = x1
    h2 = ((_rms(x1, D_MODEL) * g2_ref[0]) * (1.0 + scale2) + shift2).astype(bf16)
    bits = pltpu.bitcast(h2.astype(f32), i32)
    hx_ref[:, 0:HX_HALF] = bits[:, 0:HX_HALF] | jax.lax.shift_right_logical(bits[:, HX_HALF:D_MODEL], 16)

    def first_lane(mask, lane_f):
        return jnp.min(jnp.where(mask, lane_f, float(LANES)), axis=-1, keepdims=True)

    gl = _dot(h2, wg_ref[0].astype(bf16)) + bg_ref[0]
    glane = jax.lax.broadcasted_iota(i32, gl.shape, 1).astype(f32)
    ge = jnp.exp(gl - jnp.max(gl, axis=-1, keepdims=True))
    gprob = ge / jnp.sum(ge, axis=-1, keepdims=True)
    g_top = jnp.max(gprob, axis=-1, keepdims=True)
    g_idx = first_lane(gprob == g_top, glane)

    el = _dot(h2, we_ref[0].astype(bf16)) + be_ref[0]
    lane = jax.lax.broadcasted_iota(i32, el.shape, 1)
    lane_f = lane.astype(f32)
    emask = (lane >> 2).astype(f32) == g_idx
    em = jnp.where(emask, el, -jnp.inf)
    ee = jnp.where(emask, jnp.exp(em - jnp.max(em, axis=-1, keepdims=True)), 0.0)
    ep = ee / jnp.sum(ee, axis=-1, keepdims=True)
    p1 = jnp.max(jnp.where(emask, ep, -1.0), axis=-1, keepdims=True)
    i1 = first_lane(emask & (ep == p1), lane_f)
    rest = emask & (lane_f != i1)
    p2 = jnp.max(jnp.where(rest, ep, -1.0), axis=-1, keepdims=True)
    i2 = first_lane(rest & (ep == p2), lane_f)
    tot = p1 + p2
    w1 = g_top * (p1 / tot)
    w2 = g_top * (p2 / tot)

    lo = jnp.minimum(i1, i2) - EXPERTS_PER_GROUP * g_idx
    hi = jnp.maximum(i1, i2) - EXPERTS_PER_GROUP * g_idx
    pair = jnp.where(lo == 0.0, hi - 1.0, jnp.where(lo == 1.0, jnp.where(hi == 3.0, 3.0, 4.0), 5.0))
    cls_ref[...] = (N_PAIRS * g_idx + pair).astype(i32)
    g_lo = jnp.where(i1 < i2, w1, w2)
    g_hi = jnp.where(i1 < i2, w2, w1)
    tail_lane = jax.lax.broadcasted_iota(i32, (h2.shape[0], LANES), 1)
    hx_ref[:, HX_HALF:HX_COLS] = pltpu.bitcast(
        jnp.where(tail_lane == 0, g_lo, jnp.where(tail_lane == 1, g_hi, 0.0)), i32)


def _post(layer, o, x, row0, total, mod, mod_row, w, tile, merged=None):
    n = o.shape[0]
    lay = lambda t: (layer, 0, 0)
    row = lambda t: (t, 0)
    off = row0 // tile
    out_row = lambda t: (off + t, 0)
    in_specs = [
        pl.BlockSpec((tile, D_MODEL), row),
        pl.BlockSpec((tile, D_MODEL), row),
        pl.BlockSpec((1, 1, N_MOD, D_MODEL), lambda t: (layer, mod_row(t * tile), 0, 0)),
        pl.BlockSpec((1, D_MODEL, D_MODEL), lay),
        pl.BlockSpec((1, 1, D_MODEL), lay),
        pl.BlockSpec((1, D_MODEL, N_GROUPS), lay),
        pl.BlockSpec((1, 1, N_GROUPS), lay),
        pl.BlockSpec((1, D_MODEL, N_EXPERTS), lay),
        pl.BlockSpec((1, 1, N_EXPERTS), lay),
    ]
    args = [o, x, mod, w["w_out"], w["g2"], w["w_grp"], w["b_grp"], w["w_rtr"], w["b_rtr"]]
    aliases = {}
    if merged is not None:
        aliases = {len(args) + j: j for j in range(3)}
        in_specs += [pl.BlockSpec(memory_space=pl.ANY)] * 3
        args += list(merged)
    return pl.pallas_call(
        functools.partial(_post_kernel, merged is not None),
        out_shape=[jax.ShapeDtypeStruct((total, D_MODEL), f32), jax.ShapeDtypeStruct((total, HX_COLS), i32),
                   jax.ShapeDtypeStruct((total, 1), i32)],
        grid=(n // tile,),
        in_specs=in_specs,
        out_specs=[pl.BlockSpec((tile, D_MODEL), out_row), pl.BlockSpec((tile, HX_COLS), out_row),
                   pl.BlockSpec((tile, 1), out_row)],
        scratch_shapes=[pltpu.VMEM((D_MODEL, D_MODEL), bf16)],
        input_output_aliases=aliases,
        compiler_params=_params(("arbitrary",)),
        name="post_attention",
    )(*args)


def _ctx_kernel(lam_init, n_prev, resid, *refs):
    it = iter(refs)
    pre_in = [next(it) for _ in range(9 + (2 if resid else 0) + n_prev)]
    lam_ref, dg_ref = next(it), next(it)
    post_w = [next(it) for _ in range(6)]
    cache_refs = [next(it) for _ in range(len(CACHE_WIDTHS))]
    x1_ref, hx_ref, cls_ref = (next(it) for _ in range(3))
    w_in_scr, q_scr, kv_scr, o_scr, s_scr, p_scr, w_out_scr, x2_scr = (next(it) for _ in range(8))
    x_ref, mod_ref = pre_in[0], pre_in[1]
    x2_ref = [x2_scr] if resid else []

    _pre_kernel(False, n_prev, resid, *pre_in, q_scr, kv_scr.at[0, 0], *x2_ref, *cache_refs, w_in_scr)
    seq = s_scr.shape[1]
    for r in range(q_scr.shape[0] // seq):
        rows = pl.ds(r * seq, seq)
        _attn_kernel(lam_init, len(_SCORE_HEADS), 1, q_scr.at[rows], kv_scr.at[:, :, rows], lam_ref, dg_ref,
                     o_scr.at[rows], s_scr, p_scr)
    _post_kernel(False, o_scr, x2_scr if resid else x_ref, mod_ref, *post_w, x1_ref, hx_ref, cls_ref, w_out_scr)


def _ctx_layer(layer, x, n, seq, total, mod, w, prev_cache=(), resid=None):
    lay = lambda t: (layer, 0, 0)
    row = lambda t: (t, 0)
    reqs = CTX_REQUESTS_PER_STEP
    tile = reqs * seq
    mod_spec = lambda l: pl.BlockSpec((1, 1, N_MOD, D_MODEL), lambda t: (l, 0, 0, 0))
    in_specs = [
        pl.BlockSpec((tile, D_MODEL), row), mod_spec(layer),
        pl.BlockSpec((1, 1, D_MODEL), lay), pl.BlockSpec((1, IN_COLS, D_MODEL), lay),
        pl.BlockSpec((1, 1, KV_RANK), lay), pl.BlockSpec((1, 1, 512), lay), pl.BlockSpec((2 * LANES, LANES), lambda t: (0, 0)),
        pl.BlockSpec((1, 256, 384), lay), pl.BlockSpec((1, KV_RANK, 384), lay),
    ]
    args = [x, mod, w["g1"], w["w_in"], w["kv_g"], w["qk_g"], w["seg"], w["w_ka"], w["w_uv"]]
    if resid is not None:
        in_specs += [pl.BlockSpec((tile, D_MODEL), row), mod_spec(layer - 1)]
        args += [resid, mod]
    out_shape, out_specs, aliases = [], [], {}
    for j, width in enumerate(CACHE_WIDTHS):
        if prev_cache:
            aliases[len(args)] = len(out_shape)
            in_specs.append(pl.BlockSpec(memory_space=pl.ANY))
            args.append(prev_cache[j])
        shape = (MLA_ROPE, seq) if j == 1 else (seq, width)
        out_shape.append(jax.ShapeDtypeStruct((n // seq, DEPTH) + shape, f32))
        out_specs.append(pl.BlockSpec((reqs, 1) + shape, lambda t: (t, layer, 0, 0)))
    in_specs += [
        pl.BlockSpec((1, 4, DIFF_QK), lay), pl.BlockSpec((1, 1, DIFF_V), lay),
        pl.BlockSpec((1, D_MODEL, D_MODEL), lay), pl.BlockSpec((1, 1, D_MODEL), lay),
        pl.BlockSpec((1, D_MODEL, N_GROUPS), lay), pl.BlockSpec((1, 1, N_GROUPS), lay),
        pl.BlockSpec((1, D_MODEL, N_EXPERTS), lay), pl.BlockSpec((1, 1, N_EXPERTS), lay),
    ]
    args += [w["lam"], w["diff_g"], w["w_out"], w["g2"], w["w_grp"], w["b_grp"], w["w_rtr"], w["b_rtr"]]
    out_shape += [jax.ShapeDtypeStruct((total, D_MODEL), f32), jax.ShapeDtypeStruct((total, HX_COLS), i32),
                  jax.ShapeDtypeStruct((total, 1), i32)]
    out_specs += [pl.BlockSpec((tile, D_MODEL), row), pl.BlockSpec((tile, HX_COLS), row), pl.BlockSpec((tile, 1), row)]
    heads = len(_SCORE_HEADS)
    return pl.pallas_call(
        functools.partial(_ctx_kernel, 0.8 - 0.6 * math.exp(-0.3 * layer), len(prev_cache), resid is not None),
        out_shape=out_shape,
        grid=(n // tile,),
        in_specs=in_specs,
        out_specs=out_specs,
        scratch_shapes=[pltpu.VMEM((Z_COLS, D_MODEL), bf16), pltpu.VMEM((tile, Q_COLS), bf16),
                        pltpu.VMEM((1, 1, tile, KV_COLS), bf16), pltpu.VMEM((tile, D_MODEL), bf16),
                        pltpu.VMEM((heads, seq, seq), f32), pltpu.VMEM((heads, seq, seq), bf16),
                        pltpu.VMEM((D_MODEL, D_MODEL), bf16), pltpu.VMEM((tile, D_MODEL), f32)],
        input_output_aliases=aliases,
        compiler_params=_params(("arbitrary",)),
        name="context_layer",
    )(*args)


PLAN_CHUNK = 1024
TAB_ROWS = LANES


def _plan_kernel(cls_ref, slot_ref, tab_ref, rank_scr):
    n = cls_ref.shape[0]
    lane = jax.lax.broadcasted_iota(i32, (PLAN_CHUNK, LANES), 1)
    r = jax.lax.broadcasted_iota(i32, (LANES, LANES), 0)
    c = jax.lax.broadcasted_iota(i32, (LANES, LANES), 1)
    before = (c < r).astype(bf16)

    def count(b, seen):
        base = pl.multiple_of(b * PLAN_CHUNK, PLAN_CHUNK)
        onehot = (cls_ref[pl.ds(base, PLAN_CHUNK), :] == lane).astype(f32)
        for blk in range(PLAN_CHUNK // LANES):
            part = onehot[blk * LANES:(blk + 1) * LANES]
            ahead = _dot(before, part.astype(bf16)) + seen
            rank_scr[pl.ds(base + blk * LANES, LANES), :] = jnp.sum(part * ahead, axis=-1, keepdims=True)
            seen = seen + jnp.sum(part, axis=0, keepdims=True)
        return seen

    counts = jax.lax.fori_loop(0, n // PLAN_CHUNK, count, jnp.zeros((1, LANES), f32))
    tiles = jnp.floor((counts + (EXPERT_TILE - 1)) * (1.0 / EXPERT_TILE))
    rr = jax.lax.broadcasted_iota(i32, (LANES, LANES), 0)
    cc = jax.lax.broadcasted_iota(i32, (LANES, LANES), 1)
    ends = _dot(jnp.broadcast_to(tiles, (8, LANES)).astype(bf16), (rr <= cc).astype(bf16))[0:1]
    starts = ends - tiles

    def place(b, carry):
        rows = pl.ds(pl.multiple_of(b * PLAN_CHUNK, PLAN_CHUNK), PLAN_CHUNK)
        first = jnp.sum(jnp.where(cls_ref[rows, :] == lane, starts, 0.0), axis=-1, keepdims=True)
        slot_ref[rows, :] = (first * EXPERT_TILE + rank_scr[rows, :]).astype(i32)
        return carry

    jax.lax.fori_loop(0, n // PLAN_CHUNK, place, 0)

    tl = jax.lax.broadcasted_iota(i32, (TAB_ROWS, LANES), 1)
    n_tiles = jnp.sum(jnp.where(tl[0:1] == N_CLASSES - 1, ends, 0.0), axis=-1, keepdims=True)
    k = jnp.minimum(jax.lax.broadcasted_iota(i32, (TAB_ROWS, 1), 0).astype(f32), n_tiles - 1.0)
    cls_k = jnp.sum(jnp.where((tl < N_CLASSES) & (ends <= k), 1.0, 0.0), axis=-1, keepdims=True)
    cls_k = jnp.minimum(cls_k, N_CLASSES - 1.0)
    mine = tl.astype(f32) == cls_k
    used = jnp.sum(jnp.where(mine, counts, 0.0), axis=-1, keepdims=True)
    first = jnp.sum(jnp.where(mine, starts, 0.0), axis=-1, keepdims=True)
    valid = jnp.clip(used - (k - first) * EXPERT_TILE, 0.0, float(EXPERT_TILE))
    group = jnp.floor((cls_k + 0.5) * (1.0 / N_PAIRS))
    pair = cls_k - N_PAIRS * group
    lo = hi = jnp.zeros_like(pair)
    for p in range(N_PAIRS):
        lo = jnp.where(pair == p, float(PAIR_LO[p]), lo)
        hi = jnp.where(pair == p, float(PAIR_HI[p]), hi)
    e_lo = EXPERTS_PER_GROUP * group + lo
    e_hi = EXPERTS_PER_GROUP * group + hi
    tab = jnp.where(tl == 0, e_lo, jnp.where(tl == 1, e_hi, jnp.where(tl == 2, valid, jnp.where(tl == 3, n_tiles, 0.0))))
    tab_ref[...] = tab.astype(i32)


def _plan(cls, max_tiles):
    n = cls.shape[0]
    assert n % PLAN_CHUNK == 0 and max_tiles <= TAB_ROWS
    slot, tab = pl.pallas_call(
        _plan_kernel,
        out_shape=[jax.ShapeDtypeStruct((n, 1), i32), jax.ShapeDtypeStruct((TAB_ROWS, LANES), i32)],
        scratch_shapes=[pltpu.VMEM((n, 1), f32)],
        compiler_params=_params(None),
        name="dispatch_plan",
    )(cls)
    return slot.reshape(n), tab[:max_tiles, 0], tab[:max_tiles, 1], tab[:max_tiles, 2], tab[0, 3:4]


def _move_rows(src, idx, n_out, scatter):
    n = idx.shape[0]
    width = src.shape[1]
    per_worker = n // (SC_CORES * SC_SUBCORES)
    assert n % SC_ROWS == 0
    chunk = max(c for c in (64, 40, 32, 16, 8)
                if per_worker % c == 0 and 2 * c * width * src.dtype.itemsize <= SC_BUFFER_BYTES)
    n_chunks = per_worker // chunk
    mesh = plsc.VectorSubcoreMesh(core_axis_name="c", subcore_axis_name="s")

    @functools.partial(
        pl.kernel, mesh=mesh, out_type=jax.ShapeDtypeStruct((n_out, width), src.dtype),
        scratch_types=[pltpu.VMEM((chunk,), i32), pltpu.VMEM((chunk,), i32),
                       pltpu.VMEM((chunk, width), src.dtype), pltpu.VMEM((chunk, width), src.dtype),
                       pltpu.SemaphoreType.DMA, pltpu.SemaphoreType.DMA, pltpu.SemaphoreType.DMA,
                       pltpu.SemaphoreType.DMA])
    def move(src_hbm, idx_hbm, out_hbm, idx0, idx1, rows0, rows1, in0, in1, out0, out1):
        wid = jax.lax.axis_index("s") * SC_CORES + jax.lax.axis_index("c")
        base = wid * per_worker
        idx_v, rows_v, sem_in, sem_out = (idx0, idx1), (rows0, rows1), (in0, in1), (out0, out1)

        def fill(j):
            b = j % 2
            rows = pl.ds(base + j * chunk, chunk)
            pltpu.sync_copy(idx_hbm.at[rows], idx_v[b])
            src_rows = src_hbm.at[rows] if scatter else src_hbm.at[idx_v[b]]
            return pltpu.async_copy(src_rows, rows_v[b], sem_in[b])

        def drain(j):
            b = j % 2
            dst_rows = out_hbm.at[idx_v[b]] if scatter else out_hbm.at[pl.ds(base + j * chunk, chunk)]
            return pltpu.async_copy(rows_v[b], dst_rows, sem_out[b])

        fills, drains = {0: fill(0)}, {}
        for j in range(n_chunks):
            if j + 1 < n_chunks:
                if j >= 1:
                    drains[j - 1].wait()
                fills[j + 1] = fill(j + 1)
            fills[j].wait()
            drains[j] = drain(j)
        for j in range(max(n_chunks - 2, 0), n_chunks):
            drains[j].wait()

    return move(src, idx)


def _expert_kernel(lo_ref, hi_ref, valid_ref, nt_ref, xs_ref, wg_ref, wu_ref, wd_ref, ys_ref, wg_scr, wu_scr, wd_scr):
    k = pl.program_id(0)
    prev = jnp.maximum(k - 1, 0)

    @pl.when((k == 0) | (lo_ref[k] // EXPERTS_PER_GROUP != lo_ref[prev] // EXPERTS_PER_GROUP))
    def _():
        for j in range(EXPERTS_PER_GROUP):
            wg_scr[j] = wg_ref[0, 0, j].astype(bf16)
            wu_scr[j] = wu_ref[0, 0, j].astype(bf16)
            wd_scr[j] = wd_ref[0, 0, j].astype(bf16)

    @pl.when(k < nt_ref[0])
    def _():
        live = jax.lax.broadcasted_iota(i32, (EXPERT_TILE, 1), 0) < valid_ref[k]
        words = jnp.where(live, xs_ref[:, 0:HX_HALF], 0)
        x = jnp.concatenate([pltpu.bitcast(words & -65536, f32), pltpu.bitcast(words << 16, f32)], axis=1).astype(bf16)
        gates = pltpu.bitcast(jnp.where(live, xs_ref[:, HX_HALF:HX_COLS], 0), f32)
        y = None
        for lane, e_ref in enumerate((lo_ref, hi_ref)):
            j = e_ref[k] % EXPERTS_PER_GROUP
            hid = _silu(_dot(x, wg_scr[j])) * _dot(x, wu_scr[j]) * gates[:, lane:lane + 1]
            part = _dot(hid.astype(bf16), wd_scr[j])
            y = part if y is None else y + part
        ys_ref[...] = y


def _experts(layer, xs, e_lo, e_hi, valid, n_tiles, w):
    max_tiles = e_lo.shape[0]
    row = lambda k, lo, hi, valid, nt: (jnp.minimum(k, nt[0] - 1), 0)
    group = lambda k, lo, hi, valid, nt: (layer, lo[k] // EXPERTS_PER_GROUP, 0, 0, 0)
    by_group = lambda a: a.reshape(DEPTH, N_GROUPS, EXPERTS_PER_GROUP, *a.shape[2:])
    return pl.pallas_call(
        _expert_kernel,
        out_shape=jax.ShapeDtypeStruct((max_tiles * EXPERT_TILE, D_MODEL), f32),
        grid_spec=pltpu.PrefetchScalarGridSpec(
            num_scalar_prefetch=4, grid=(max_tiles,),
            in_specs=[pl.BlockSpec((EXPERT_TILE, HX_COLS), row),
                      pl.BlockSpec((1, 1, EXPERTS_PER_GROUP, D_MODEL, D_FF_EXPERT), group),
                      pl.BlockSpec((1, 1, EXPERTS_PER_GROUP, D_MODEL, D_FF_EXPERT), group),
                      pl.BlockSpec((1, 1, EXPERTS_PER_GROUP, D_FF_EXPERT, D_MODEL), group)],
            out_specs=pl.BlockSpec((EXPERT_TILE, D_MODEL), row),
            scratch_shapes=[pltpu.VMEM((EXPERTS_PER_GROUP, D_MODEL, D_FF_EXPERT), bf16),
                            pltpu.VMEM((EXPERTS_PER_GROUP, D_MODEL, D_FF_EXPERT), bf16),
                            pltpu.VMEM((EXPERTS_PER_GROUP, D_FF_EXPERT, D_MODEL), bf16)]),
        compiler_params=_params(("arbitrary",)),
        name="experts",
    )(e_lo, e_hi, valid, n_tiles, xs, by_group(w["w_gate"]), by_group(w["w_up"]), by_group(w["w_down"]))


def _final_kernel(x1_ref, y_ref, mod_ref, fg_ref, o_ref):
    o_ref[...] = _rms(x1_ref[...] + mod_ref[0, 0, 5:6, :] * y_ref[...], D_MODEL) * fg_ref[...]


def _final(x1, y, n, row0, mod, mod_row, w, tile):
    off = row0 // tile
    src_row = lambda t: (off + t, 0)
    return pl.pallas_call(
        _final_kernel,
        out_shape=jax.ShapeDtypeStruct((n, D_MODEL), f32),
        grid=(n // tile,),
        in_specs=[pl.BlockSpec((tile, D_MODEL), src_row), pl.BlockSpec((tile, D_MODEL), src_row),
                  pl.BlockSpec((1, 1, N_MOD, D_MODEL), lambda t: (DEPTH - 1, mod_row(t * tile), 0, 0)),
                  pl.BlockSpec((1, D_MODEL), lambda t: (0, 0))],
        out_specs=pl.BlockSpec((tile, D_MODEL), lambda t: (t, 0)),
        compiler_params=_params(("arbitrary",)),
        name="final_norm",
    )(x1, y, mod, w["final_g"])


def _rope_tables(n_tokens):
    pos = np.arange(n_tokens)
    row = (pos // GRID_W).astype(np.float64)
    col = (pos % GRID_W).astype(np.float64)

    def cs(rot_dim):
        quarter = rot_dim // 4
        inv = ROPE_THETA ** (-np.arange(quarter, dtype=np.float64) / quarter)
        ang = np.concatenate([row[:, None] * inv, col[:, None] * inv], axis=-1)
        return np.cos(ang), np.sin(ang)

    c32, s32 = cs(MLA_ROPE)
    c64, s64 = cs(HEAD_DIM)
    ones = np.ones((n_tokens, MLA_NOPE))
    zeros = np.zeros((n_tokens, MLA_NOPE))

    def rep(parts):
        period = np.concatenate(parts, axis=-1)
        return jnp.asarray(np.tile(period, (1, LANES // period.shape[-1])), f32)

    return (rep([ones, c32, c32]), rep([zeros, -s32, s32]), rep([c32, c32]), rep([-s32, s32]),
            rep([c64, c64]), rep([-s64, s64]))


def _layout_weights(norm1_g, norm2_g, w_in, mla_kv_norm_g, mla_w_uk, mla_w_uv, gqa_q_norm_g, gqa_k_norm_g,
                    diff_lambda, diff_norm_g, w_out, moe_w_group, moe_b_group, moe_w_router, moe_b_router,
                    moe_w_gate, moe_w_up, moe_w_down, final_norm_g):
    eye = jnp.eye(MLA_ROPE, dtype=f32)
    top = jnp.concatenate([mla_w_uk, jnp.zeros((DEPTH, KV_RANK, MLA_HEADS, MLA_ROPE), f32)], axis=-1)
    mid = jnp.concatenate([jnp.zeros((MLA_ROPE, MLA_HEADS, MLA_NOPE), f32),
                           jnp.broadcast_to(eye[:, None, :], (MLA_ROPE, MLA_HEADS, MLA_ROPE))], axis=-1)
    w_ka = jnp.concatenate([top.reshape(DEPTH, KV_RANK, 384),
                            jnp.broadcast_to(mid.reshape(1, MLA_ROPE, 384), (DEPTH, MLA_ROPE, 384)),
                            jnp.zeros((DEPTH, 256 - KV_RANK - MLA_ROPE, 384), f32)], axis=1).astype(bf16)
    seg_id = np.arange(LANES) // HEAD_DIM
    seg = jnp.asarray(np.tile(seg_id[:, None] == seg_id[None, :], (2, 1)), bf16)
    qk_g = jnp.concatenate([jnp.tile(gqa_q_norm_g, (1, GQA_HEADS)), jnp.tile(gqa_k_norm_g, (1, GQA_KV_HEADS))], axis=-1)
    return dict(
        g1=norm1_g.reshape(DEPTH, 1, D_MODEL), g2=norm2_g.reshape(DEPTH, 1, D_MODEL),
        w_in=jnp.swapaxes(w_in, 1, 2),
        kv_g=mla_kv_norm_g.reshape(DEPTH, 1, KV_RANK), qk_g=qk_g.reshape(DEPTH, 1, 512), seg=seg, w_ka=w_ka,
        w_uv=mla_w_uv.reshape(DEPTH, KV_RANK, 384), lam=diff_lambda, diff_g=diff_norm_g.reshape(DEPTH, 1, DIFF_V),
        w_out=w_out, w_grp=moe_w_group, b_grp=moe_b_group.reshape(DEPTH, 1, N_GROUPS), w_rtr=moe_w_router,
        b_rtr=moe_b_router.reshape(DEPTH, 1, N_EXPERTS), w_gate=moe_w_gate, w_up=moe_w_up, w_down=moe_w_down,
        final_g=final_norm_g.reshape(1, D_MODEL))


PRE_TILE = 512
CTX_REQUESTS_PER_STEP = 2
LAT_ATTN_TILE = 256
LAT_HEADS_PER_ROUND = 10
MXU_SUM_MIN_KEYS = 1024
POST_TILE = 512
FINAL_TILE = 1024


def kernel(x_prompt, x_sample, c, cache_mla_ckv, cache_mla_krope, cache_gqa_k, cache_gqa_v, cache_diff_k, cache_diff_v, c_ctx, norm1_g, norm2_g, w_mod, b_mod, w_in, mla_kv_norm_g, mla_w_uk, mla_w_uv, gqa_q_norm_g, gqa_k_norm_g, diff_lambda, diff_norm_g, w_out, moe_w_group, moe_b_group, moe_w_router, moe_b_router, moe_w_gate, moe_w_up, moe_w_down, final_norm_g):
    B, S, _ = x_prompt.shape
    Bl, Sl, _ = x_sample.shape
    n_ctx, n_lat = B * S, Bl * Sl
    total = n_ctx + n_lat
    assert Bl + 1 <= MOD_ROWS and DEPTH == 2 and total % SC_ROWS == 0
    assert all(Sl % tile == 0 for tile in (PRE_TILE, LAT_ATTN_TILE, POST_TILE, FINAL_TILE))
    assert n_ctx % FINAL_TILE == 0 and n_ctx % POST_TILE == 0 and B % CTX_REQUESTS_PER_STEP == 0
    slot_rows = -(-(total + N_CLASSES * EXPERT_TILE) // SC_ROWS) * SC_ROWS
    max_tiles = slot_rows // EXPERT_TILE
    w = _layout_weights(norm1_g, norm2_g, w_in, mla_kv_norm_g, mla_w_uk, mla_w_uv, gqa_q_norm_g, gqa_k_norm_g,
                        diff_lambda, diff_norm_g, w_out, moe_w_group, moe_b_group, moe_w_router, moe_b_router,
                        moe_w_gate, moe_w_up, moe_w_down, final_norm_g)
    cond = jnp.concatenate([c_ctx[None, :], c, jnp.zeros((MOD_ROWS - 1 - Bl, D_MODEL), f32)], axis=0)
    mod = _modulation(cond, w_mod, b_mod).reshape(DEPTH, MOD_ROWS, N_MOD, D_MODEL)
    ctx_row = lambda token: 0
    lat_row = lambda token: 1 + token // Sl
    tabs = _rope_tables(Sl)
    kv_past = _cache_rows((cache_mla_ckv, cache_mla_krope, cache_gqa_k, cache_gqa_v, cache_diff_k, cache_diff_v), w)
    per_b = Sl // LAT_ATTN_TILE

    x_ctx, x_lat = x_prompt.reshape(n_ctx, D_MODEL), x_sample.reshape(n_lat, D_MODEL)
    cache = ()
    x1 = y = None
    for i in range(DEPTH):
        if i == 0:
            *cache, x1_c, hx_c, cls_c = _ctx_layer(i, x_ctx, n_ctx, S, total, mod, w)
        else:
            *cache, x1_c, hx_c, cls_c = _ctx_layer(i, x1, n_ctx, S, total, mod, w, prev_cache=cache, resid=y)
        if i == 0:
            q_l, kv_l = _pre_latent(i, x_lat, n_lat, 0, Sl, mod, lat_row, w, PRE_TILE, tabs)
        else:
            q_l, kv_l, x_lat = _pre_latent(i, x1, n_lat, n_ctx, Sl, mod, lat_row, w, PRE_TILE, tabs, resid=y)
        past = (kv_past, PAST_LEN, lambda t, i=i: (i, t // per_b, 0, 0))
        own = (kv_l.reshape(1, Bl, Sl, KV_COLS), Sl, lambda t: (0, t // per_b, 0, 0))
        o_l = _attention(i, q_l, [past, own], w, LAT_ATTN_TILE, LAT_HEADS_PER_ROUND)
        x1, hx, cls = _post(i, o_l, x_lat, n_ctx, total, mod, lat_row, w, POST_TILE, merged=(x1_c, hx_c, cls_c))
        if i == DEPTH - 1:
            outs = (cache[0], jnp.swapaxes(cache[1], 2, 3), cache[2].reshape(B, DEPTH, S, GQA_KV_HEADS, HEAD_DIM),
                    cache[3].reshape(B, DEPTH, S, GQA_KV_HEADS, HEAD_DIM),
                    cache[4].reshape(B, DEPTH, S, DIFF_HEADS, 2, DIFF_QK),
                    cache[5].reshape(B, DEPTH, S, DIFF_HEADS, DIFF_V))
            cls, outs = jax.lax.optimization_barrier((cls, outs))
        slot, e_lo, e_hi, valid, n_tiles = _plan(cls, max_tiles)
        xs = _move_rows(hx, slot, slot_rows, scatter=True)
        ys = _experts(i, xs, e_lo, e_hi, valid, n_tiles, w)
        y = _move_rows(ys, slot, total, scatter=False)

    y_prompt = _final(x1, y, n_ctx, 0, mod, ctx_row, w, FINAL_TILE).reshape(B, S, D_MODEL)
    y_sample = _final(x1, y, n_lat, n_ctx, mod, lat_row, w, FINAL_TILE).reshape(Bl, Sl, D_MODEL)
    return (y_prompt, y_sample, *outs)
```

```python
import functools
import math

import jax
import jax.numpy as jnp
import numpy as np
from jax.experimental import pallas as pl
from jax.experimental.pallas import tpu as pltpu
from jax.experimental.pallas import tpu_sc as plsc

D_MODEL = 1024
DEPTH = 2
PAST_LEN = 512
GRID_W = 64
ROPE_THETA = 10000.0
EPS = 1e-6
LOG2E = 1.4426950408889634
N_MOD = 6
HEAD_DIM = 64
MLA_HEADS = 6
MLA_NOPE = 32
MLA_ROPE = 32
MLA_V = 64
KV_RANK = 128
GQA_HEADS = 6
GQA_KV_HEADS = 2
GQA_GROUP = GQA_HEADS // GQA_KV_HEADS
DIFF_HEADS = 4
DIFF_QK = 32
DIFF_V = 64
N_GROUPS = 4
EXPERTS_PER_GROUP = 4
N_EXPERTS = N_GROUPS * EXPERTS_PER_GROUP
D_FF_EXPERT = 256

LANES = 128
MOD_ROWS = 8

IN_COLS = 1952
IN_KR = 512
Z_QA, Z_CKV, Z_QG, Z_KG, Z_VG, Z_QD, Z_KD, Z_VD, Z_KR = 0, 384, 512, 896, 1024, 1152, 1408, 1664, 1920
Z_COLS = 2048
Q_A, Q_G, Q_D, Q_COLS = 0, 384, 768, 1024
KV_KA, KV_VA, KV_KG, KV_VG, KV_KD, KV_VD, KV_COLS = 0, 384, 768, 896, 1024, 1280, 1536
CACHE_WIDTHS = (128, 32, 128, 128, 256, 256)

PAIR_LO = (0, 0, 0, 1, 1, 2)
PAIR_HI = (1, 2, 3, 3, 2, 3)
N_PAIRS = len(PAIR_LO)
N_CLASSES = N_GROUPS * N_PAIRS
HX_HALF = D_MODEL // 2
HX_COLS = HX_HALF + LANES
EXPERT_TILE = 256

SC_CORES, SC_SUBCORES = 2, 16
SC_BUFFER_BYTES = 400 * 1024
SC_ROWS = SC_CORES * SC_SUBCORES * 8

VMEM_LIMIT = 56 * 1024 * 1024

bf16 = jnp.bfloat16
f32 = jnp.float32
i32 = jnp.int32


def _dot(a, b):
    return jnp.dot(a, b, preferred_element_type=f32)


def _dot_nt(a, b):
    return jax.lax.dot_general(a, b, (((1,), (1,)), ((), ())), preferred_element_type=f32)


def _rms(x, width):
    return x * jax.lax.rsqrt(jnp.sum(x * x, axis=-1, keepdims=True) * (1.0 / width) + EPS)


def _silu(x):
    return x * (1.0 / (1.0 + jnp.exp(-x)))


def _params(sem):
    return pltpu.CompilerParams(dimension_semantics=sem, vmem_limit_bytes=VMEM_LIMIT)


def _mod_kernel(cond_ref, w_ref, b_ref, o_ref):
    o_ref[0] = _dot(_silu(cond_ref[...]).astype(bf16), w_ref[0].astype(bf16)) + b_ref[0]


def _modulation(cond, w_mod, b_mod):
    return pl.pallas_call(
        _mod_kernel,
        out_shape=jax.ShapeDtypeStruct((DEPTH, MOD_ROWS, N_MOD * D_MODEL), f32),
        grid=(DEPTH, N_MOD),
        in_specs=[
            pl.BlockSpec((MOD_ROWS, D_MODEL), lambda i, j: (0, 0)),
            pl.BlockSpec((1, D_MODEL, D_MODEL), lambda i, j: (i, 0, j)),
            pl.BlockSpec((1, 1, D_MODEL), lambda i, j: (i, 0, j)),
        ],
        out_specs=pl.BlockSpec((1, MOD_ROWS, D_MODEL), lambda i, j: (i, 0, j)),
        compiler_params=_params(("arbitrary", "arbitrary")),
        name="modulation",
    )(cond, w_mod, b_mod.reshape(DEPTH, 1, N_MOD * D_MODEL))


def _swap_halves(x, half):
    lane = jax.lax.broadcasted_iota(i32, x.shape, 1)
    fwd = pltpu.roll(x, LANES - half, 1)
    bwd = pltpu.roll(x, half, 1)
    return jnp.where((lane & (2 * half - 1)) < half, fwd, bwd)


def _rope_block(x, cos, sin, half):
    return x * cos + _swap_halves(x, half) * sin


def _pre_kernel(rope, n_prev, resid, *refs):
    it = iter(refs)
    x_ref, mod_ref, g1_ref, w_in_ref, kvg_ref, qkg_ref, seg_ref, wka_ref, wuv_ref = (next(it) for _ in range(9))
    if resid:
        y_ref, pmod_ref = next(it), next(it)
    if rope:
        ca_ref, sa_ref, c32_ref, s32_ref, c64_ref, s64_ref = (next(it) for _ in range(6))
    prev_refs = [next(it) for _ in range(n_prev)]
    q_ref, kv_ref = next(it), next(it)
    if resid:
        x2_ref = next(it)
    cache_refs = [] if rope else [next(it) for _ in range(len(CACHE_WIDTHS))]
    w_scr = next(it)

    @pl.when(pl.program_id(0) == 0)
    def _():
        w_scr[0:IN_KR] = w_in_ref[0, 0:IN_KR].astype(bf16)
        w_scr[IN_KR:Z_KR] = w_in_ref[0, IN_KR + MLA_ROPE:IN_COLS].astype(bf16)
        w_scr[Z_KR:Z_KR + MLA_ROPE] = w_in_ref[0, IN_KR:IN_KR + MLA_ROPE].astype(bf16)
        w_scr[Z_KR + MLA_ROPE:Z_COLS] = jnp.zeros((Z_COLS - Z_KR - MLA_ROPE, D_MODEL), bf16)

    x = x_ref[...]
    if resid:
        x = x + pmod_ref[0, 0, 5:6, :] * y_ref[...]
        x2_ref[...] = x
    shift1 = mod_ref[0, 0, 0:1, :]
    scale1 = mod_ref[0, 0, 1:2, :]
    h = (_rms(x, D_MODEL) * g1_ref[0]) * (1.0 + scale1) + shift1
    z = _dot_nt(h.astype(bf16), w_scr[...])

    ckv = _rms(z[:, Z_CKV:Z_CKV + KV_RANK], KV_RANK) * kvg_ref[0]

    qk = z[:, Z_QG:Z_VG]
    sq = qk * qk
    sq_hi = sq.astype(bf16)
    sq_lo = (sq - sq_hi.astype(f32)).astype(bf16)
    seg = seg_ref[...]
    ms = jnp.concatenate(
        [_dot(jnp.concatenate([sq_hi[:, LANES * j:LANES * (j + 1)], sq_lo[:, LANES * j:LANES * (j + 1)]], axis=1), seg)
         for j in range(qk.shape[1] // LANES)], axis=1) * (1.0 / HEAD_DIM)
    qk = qk * jax.lax.rsqrt(ms + EPS) * qkg_ref[0]

    def blocks(arr, n):
        return [arr[:, LANES * j:LANES * (j + 1)] for j in range(n)]

    qa = blocks(z[:, Z_QA:Z_QA + 384], 3)
    qkb = blocks(qk, 4)
    qd = blocks(z[:, Z_QD:Z_QD + 256], 2)
    kd = blocks(z[:, Z_KD:Z_KD + 256], 2)
    kr = z[:, Z_KR:Z_KR + LANES]
    if rope:
        ca, sa, c32, s32, c64, s64 = (r[...] for r in (ca_ref, sa_ref, c32_ref, s32_ref, c64_ref, s64_ref))
        qa = [_rope_block(b, ca, sa, MLA_ROPE // 2) for b in qa]
        qkb = [_rope_block(b, c64, s64, HEAD_DIM // 2) for b in qkb]
        qd = [_rope_block(b, c32, s32, DIFF_QK // 2) for b in qd]
        kd = [_rope_block(b, c32, s32, DIFF_QK // 2) for b in kd]
        kr = _rope_block(kr, c32, s32, MLA_ROPE // 2)

    vg = z[:, Z_VG:Z_VG + 128]
    vd = z[:, Z_VD:Z_VD + 256]
    ckv_b = ckv.astype(bf16)
    k_a = _dot(jnp.concatenate([ckv_b, kr.astype(bf16)], axis=1), wka_ref[0])
    v_a = _dot(ckv_b, wuv_ref[0].astype(bf16))

    for j in range(3):
        q_ref[:, Q_A + LANES * j:Q_A + LANES * (j + 1)] = (qa[j] * (HEAD_DIM ** -0.5 * LOG2E)).astype(bf16)
        q_ref[:, Q_G + LANES * j:Q_G + LANES * (j + 1)] = (qkb[j] * (HEAD_DIM ** -0.5 * LOG2E)).astype(bf16)
    for j in range(2):
        q_ref[:, Q_D + LANES * j:Q_D + LANES * (j + 1)] = (qd[j] * (DIFF_QK ** -0.5 * LOG2E)).astype(bf16)
        kv_ref[:, KV_KD + LANES * j:KV_KD + LANES * (j + 1)] = kd[j].astype(bf16)
    kv_ref[:, KV_KA:KV_KA + 384] = k_a.astype(bf16)
    kv_ref[:, KV_VA:KV_VA + 384] = v_a.astype(bf16)
    kv_ref[:, KV_KG:KV_KG + 128] = qkb[3].astype(bf16)
    kv_ref[:, KV_VG:KV_VG + 128] = vg.astype(bf16)
    kv_ref[:, KV_VD:KV_VD + 256] = vd.astype(bf16)
    if not rope:
        rows = [ckv, None, qkb[3], vg, jnp.concatenate(kd, axis=1), vd]
        for out, new in zip(cache_refs, rows):
            if new is None:
                seq = out.shape[3]
                for r in range(out.shape[0]):
                    out[r, 0] = kr[r * seq:(r + 1) * seq].T[:MLA_ROPE, :]
            else:
                reqs, _, seq, width = out.shape
                out[:, 0] = new.reshape(reqs, seq, width)


def _pre_latent(layer, x, n, row0, seq, mod, mod_row, w, tile, rope_tabs, resid=None):
    lay = lambda t: (layer, 0, 0)
    row = lambda t: (t, 0)
    off = row0 // tile
    src_row = lambda t: (off + t, 0)
    in_specs = [
        pl.BlockSpec((tile, D_MODEL), src_row),
        pl.BlockSpec((1, 1, N_MOD, D_MODEL), lambda t: (layer, mod_row(t * tile), 0, 0)),
        pl.BlockSpec((1, 1, D_MODEL), lay),
        pl.BlockSpec((1, IN_COLS, D_MODEL), lay),
        pl.BlockSpec((1, 1, KV_RANK), lay),
        pl.BlockSpec((1, 1, 512), lay),
        pl.BlockSpec((2 * LANES, LANES), lambda t: (0, 0)),
        pl.BlockSpec((1, 256, 384), lay),
        pl.BlockSpec((1, KV_RANK, 384), lay),
    ]
    args = [x, mod, w["g1"], w["w_in"], w["kv_g"], w["qk_g"], w["seg"], w["w_ka"], w["w_uv"]]
    if resid is not None:
        in_specs += [pl.BlockSpec((tile, D_MODEL), src_row),
                     pl.BlockSpec((1, 1, N_MOD, D_MODEL), lambda t: (layer - 1, mod_row(t * tile), 0, 0))]
        args += [resid, mod]
    per_b = seq // tile
    in_specs += [pl.BlockSpec((tile, LANES), lambda t: (t % per_b, 0))] * 6
    args += list(rope_tabs)
    out_shape = [jax.ShapeDtypeStruct((n, Q_COLS), bf16), jax.ShapeDtypeStruct((n, KV_COLS), bf16)]
    out_specs = [pl.BlockSpec((tile, Q_COLS), row), pl.BlockSpec((tile, KV_COLS), row)]
    if resid is not None:
        out_shape.append(jax.ShapeDtypeStruct((n, D_MODEL), f32))
        out_specs.append(pl.BlockSpec((tile, D_MODEL), row))
    return pl.pallas_call(
        functools.partial(_pre_kernel, True, 0, resid is not None),
        out_shape=out_shape,
        grid=(n // tile,),
        in_specs=in_specs,
        out_specs=out_specs,
        scratch_shapes=[pltpu.VMEM((Z_COLS, D_MODEL), bf16)],
        compiler_params=_params(("arbitrary",)),
        name="pre_latent",
    )(*args)


PAST_CKV, PAST_KG, PAST_VG, PAST_KD, PAST_VD, PAST_KR, PAST_COLS = 0, 128, 256, 384, 640, 896, 928


def _cache_kernel(past_ref, wka_ref, wuv_ref, kv_ref):
    ckv_b = past_ref[0, 0, :, PAST_CKV:PAST_CKV + KV_RANK].astype(bf16)
    kr_b = past_ref[0, 0, :, PAST_KR:PAST_KR + MLA_ROPE].astype(bf16)
    wka = wka_ref[0]
    k_a = _dot(ckv_b, wka[:KV_RANK]) + _dot(kr_b, wka[KV_RANK:KV_RANK + MLA_ROPE])
    kv_ref[0, 0, :, KV_KA:KV_KA + 384] = k_a.astype(bf16)
    kv_ref[0, 0, :, KV_VA:KV_VA + 384] = _dot(ckv_b, wuv_ref[0].astype(bf16)).astype(bf16)
    kv_ref[0, 0, :, KV_KG:KV_KG + 128] = past_ref[0, 0, :, PAST_KG:PAST_KG + 128].astype(bf16)
    kv_ref[0, 0, :, KV_VG:KV_VG + 128] = past_ref[0, 0, :, PAST_VG:PAST_VG + 128].astype(bf16)
    kv_ref[0, 0, :, KV_KD:KV_KD + 256] = past_ref[0, 0, :, PAST_KD:PAST_KD + 256].astype(bf16)
    kv_ref[0, 0, :, KV_VD:KV_VD + 256] = past_ref[0, 0, :, PAST_VD:PAST_VD + 256].astype(bf16)


def _cache_rows(caches, w):
    ckv, kr, kg, vg, kd, vd = caches
    B = ckv.shape[0]
    flat = lambda a: a.reshape(B, DEPTH, PAST_LEN, -1)
    past = jnp.concatenate([flat(ckv), flat(kg), flat(vg), flat(kd), flat(vd), flat(kr)], axis=-1)
    return pl.pallas_call(
        _cache_kernel,
        out_shape=jax.ShapeDtypeStruct((DEPTH, B, PAST_LEN, KV_COLS), bf16),
        grid=(DEPTH, B),
        in_specs=[pl.BlockSpec((1, 1, PAST_LEN, PAST_COLS), lambda i, b: (b, i, 0, 0)),
                  pl.BlockSpec((1, 256, 384), lambda i, b: (i, 0, 0)), pl.BlockSpec((1, KV_RANK, 384), lambda i, b: (i, 0, 0))],
        out_specs=pl.BlockSpec((1, 1, PAST_LEN, KV_COLS), lambda i, b: (i, b, 0, 0)),
        compiler_params=_params(("arbitrary", "arbitrary")),
        name="cache_rows",
    )(past, w["w_ka"], w["w_uv"])


_SCORE_HEADS = (
    [(Q_A + 64 * h, KV_KA + 64 * h, 64, KV_VA + MLA_V * h) for h in range(MLA_HEADS)]
    + [(Q_G + 64 * h, KV_KG + 64 * (h // GQA_GROUP), 64, KV_VG + 64 * (h // GQA_GROUP)) for h in range(GQA_HEADS)]
    + [(Q_D + 64 * h + DIFF_QK * c, KV_KD + 64 * h + DIFF_QK * c, DIFF_QK, KV_VD + DIFF_V * h)
       for h in range(DIFF_HEADS) for c in range(2)])


def _attn_kernel(lam_init, per_round, n_src, q_ref, *refs):
    kv_refs = refs[:n_src]
    lam_ref, dg_ref, o_ref, s_ref, p_ref = refs[n_src:]
    spans, start = [], 0
    for r in kv_refs:
        spans.append((r, start, r.shape[2]))
        start += r.shape[2]
    mxu_sum = start >= MXU_SUM_MIN_KEYS

    outs = []
    for first in range(0, len(_SCORE_HEADS), per_round):
        chunk = _SCORE_HEADS[first:first + per_round]
        for j, (q_off, k_off, width, _) in enumerate(chunk):
            for r, lo, size in spans:
                s_ref[j, :, lo:lo + size] = _dot_nt(q_ref[:, q_off:q_off + width], r[0, 0, :, k_off:k_off + width])
        s = s_ref[...]
        p = jnp.exp2(s - jnp.max(s, axis=-1, keepdims=True))
        if mxu_sum:
            p_ref[...] = p.astype(bf16)
            for j, (_, _, _, v_off) in enumerate(chunk):
                o = sum(_dot(p_ref[j, :, lo:lo + size],
                             jnp.concatenate([r[0, 0, :, v_off:v_off + DIFF_V],
                                              jnp.ones((size, LANES - DIFF_V), bf16)], axis=1))
                        for r, lo, size in spans)
                outs.append((o * pltpu.roll(1.0 / o, DIFF_V, 1))[:, :DIFF_V])
        else:
            inv = 1.0 / jnp.sum(p, axis=-1, keepdims=True)
            p_ref[...] = p.astype(bf16)
            for j, (_, _, _, v_off) in enumerate(chunk):
                o = sum(_dot(p_ref[j, :, lo:lo + size], r[0, 0, :, v_off:v_off + DIFF_V]) for r, lo, size in spans)
                outs.append(o * inv[j])

    lp = lam_ref[0]
    e1 = jnp.exp(jnp.sum(lp[0:1] * lp[1:2], axis=-1, keepdims=True))
    e2 = jnp.exp(jnp.sum(lp[2:3] * lp[3:4], axis=-1, keepdims=True))
    lam = e1 - e2 + lam_init
    heads = outs[:MLA_HEADS + GQA_HEADS]
    for h in range(DIFF_HEADS):
        o1, o2 = outs[MLA_HEADS + GQA_HEADS + 2 * h:MLA_HEADS + GQA_HEADS + 2 * h + 2]
        heads.append(_rms(o1 - lam * o2, DIFF_V) * dg_ref[0] * (1.0 - lam_init))
    for j in range(len(heads) // 2):
        o_ref[:, LANES * j:LANES * (j + 1)] = jnp.concatenate(heads[2 * j:2 * j + 2], axis=1).astype(bf16)


def _attention(layer, q, sources, w, tile, per_round):
    n = q.shape[0]
    lam_init = 0.8 - 0.6 * math.exp(-0.3 * layer)
    s_kv = sum(rows for _, rows, _ in sources)
    assert len(_SCORE_HEADS) % per_round == 0
    return pl.pallas_call(
        functools.partial(_attn_kernel, lam_init, per_round, len(sources)),
        out_shape=jax.ShapeDtypeStruct((n, D_MODEL), bf16),
        scratch_shapes=[pltpu.VMEM((per_round, tile, s_kv), f32), pltpu.VMEM((per_round, tile, s_kv), bf16)],
        grid=(n // tile,),
        in_specs=[pl.BlockSpec((tile, Q_COLS), lambda t: (t, 0))]
        + [pl.BlockSpec((1, 1, rows, KV_COLS), index) for _, rows, index in sources]
        + [pl.BlockSpec((1, 4, DIFF_QK), lambda t: (layer, 0, 0)), pl.BlockSpec((1, 1, DIFF_V), lambda t: (layer, 0, 0))],
        out_specs=pl.BlockSpec((tile, D_MODEL), lambda t: (t, 0)),
        compiler_params=_params(("arbitrary",)),
        name="attention",
    )(q, *[arr for arr, _, _ in sources], w["lam"], w["diff_g"])


def _post_kernel(merge, *refs):
    it = iter(refs)
    o_ref, x_ref, mod_ref, w_out_ref, g2_ref, wg_ref, bg_ref, we_ref, be_ref = (next(it) for _ in range(9))
    if merge:
        next(it), next(it), next(it)
    x1_ref, hx_ref, cls_ref, w_scr = (next(it) for _ in range(4))

    @pl.when(pl.program_id(0) == 0)
    def _():
        w_scr[...] = w_out_ref[0].astype(bf16)

    gate1 = mod_ref[0, 0, 2:3, :]
    shift2 = mod_ref[0, 0, 3:4, :]
    scale2 = mod_ref[0, 0, 4:5, :]
    x1 = x_ref[...] + gate1 * _dot(o_ref[...], w_scr[...])
    x1_ref[...] = x1
    h2 = ((_rms(x1, D_MODEL) * g2_ref[0]) * (1.0 + scale2) + shift2).astype(bf16)
    bits = pltpu.bitcast(h2.astype(f32), i32)
    hx_ref[:, 0:HX_HALF] = bits[:, 0:HX_HALF] | jax.lax.shift_right_logical(bits[:, HX_HALF:D_MODEL], 16)

    def first_lane(mask, lane_f):
        return jnp.min(jnp.where(mask, lane_f, float(LANES)), axis=-1, keepdims=True)

    gl = _dot(h2, wg_ref[0].astype(bf16)) + bg_ref[0]
    glane = jax.lax.broadcasted_iota(i32, gl.shape, 1).astype(f32)
    ge = jnp.exp(gl - jnp.max(gl, axis=-1, keepdims=True))
    gprob = ge / jnp.sum(ge, axis=-1, keepdims=True)
    g_top = jnp.max(gprob, axis=-1, keepdims=True)
    g_idx = first_lane(gprob == g_top, glane)

    el = _dot(h2, we_ref[0].astype(bf16)) + be_ref[0]
    lane = jax.lax.broadcasted_iota(i32, el.shape, 1)
    lane_f = lane.astype(f32)
    emask = (lane >> 2).astype(f32) == g_idx
    em = jnp.where(emask, el, -jnp.inf)
    ee = jnp.where(emask, jnp.exp(em - jnp.max(em, axis=-1, keepdims=True)), 0.0)
    ep = ee / jnp.sum(ee, axis=-1, keepdims=True)
    p1 = jnp.max(jnp.where(emask, ep, -1.0), axis=-1, keepdims=True)
    i1 = first_lane(emask & (ep == p1), lane_f)
    rest = emask & (lane_f != i1)
    p2 = jnp.max(jnp.where(rest, ep, -1.0), axis=-1, keepdims=True)
    i2 = first_lane(rest & (ep == p2), lane_f)
    tot = p1 + p2
    w1 = g_top * (p1 / tot)
    w2 = g_top * (p2 / tot)

    lo = jnp.minimum(i1, i2) - EXPERTS_PER_GROUP * g_idx
    hi = jnp.maximum(i1, i2) - EXPERTS_PER_GROUP * g_idx
    pair = jnp.where(lo == 0.0, hi - 1.0, jnp.where(lo == 1.0, jnp.where(hi == 3.0, 3.0, 4.0), 5.0))
    cls_ref[...] = (N_PAIRS * g_idx + pair).astype(i32)
    g_lo = jnp.where(i1 < i2, w1, w2)
    g_hi = jnp.where(i1 < i2, w2, w1)
    tail_lane = jax.lax.broadcasted_iota(i32, (h2.shape[0], LANES), 1)
    hx_ref[:, HX_HALF:HX_COLS] = pltpu.bitcast(
        jnp.where(tail_lane == 0, g_lo, jnp.where(tail_lane == 1, g_hi, 0.0)), i32)


def _post(layer, o, x, row0, total, mod, mod_row, w, tile, merged=None):
    n = o.shape[0]
    lay = lambda t: (layer, 0, 0)
    row = lambda t: (t, 0)
    off = row0 // tile
    out_row = lambda t: (off + t, 0)
    in_specs = [
        pl.BlockSpec((tile, D_MODEL), row),
        pl.BlockSpec((tile, D_MODEL), row),
        pl.BlockSpec((1, 1, N_MOD, D_MODEL), lambda t: (layer, mod_row(t * tile), 0, 0)),
        pl.BlockSpec((1, D_MODEL, D_MODEL), lay),
        pl.BlockSpec((1, 1, D_MODEL), lay),
        pl.BlockSpec((1, D_MODEL, N_GROUPS), lay),
        pl.BlockSpec((1, 1, N_GROUPS), lay),
        pl.BlockSpec((1, D_MODEL, N_EXPERTS), lay),
        pl.BlockSpec((1, 1, N_EXPERTS), lay),
    ]
    args = [o, x, mod, w["w_out"], w["g2"], w["w_grp"], w["b_grp"], w["w_rtr"], w["b_rtr"]]
    aliases = {}
    if merged is not None:
        aliases = {len(args) + j: j for j in range(3)}
        in_specs += [pl.BlockSpec(memory_space=pl.ANY)] * 3
        args += list(merged)
    return pl.pallas_call(
        functools.partial(_post_kernel, merged is not None),
        out_shape=[jax.ShapeDtypeStruct((total, D_MODEL), f32), jax.ShapeDtypeStruct((total, HX_COLS), i32),
                   jax.ShapeDtypeStruct((total, 1), i32)],
        grid=(n // tile,),
        in_specs=in_specs,
        out_specs=[pl.BlockSpec((tile, D_MODEL), out_row), pl.BlockSpec((tile, HX_COLS), out_row),
                   pl.BlockSpec((tile, 1), out_row)],
        scratch_shapes=[pltpu.VMEM((D_MODEL, D_MODEL), bf16)],
        input_output_aliases=aliases,
        compiler_params=_params(("arbitrary",)),
        name="post_attention",
    )(*args)


def _ctx_kernel(lam_init, n_prev, resid, *refs):
    it = iter(refs)
    pre_in = [next(it) for _ in range(9 + (2 if resid else 0) + n_prev)]
    lam_ref, dg_ref = next(it), next(it)
    post_w = [next(it) for _ in range(6)]
    cache_refs = [next(it) for _ in range(len(CACHE_WIDTHS))]
    x1_ref, hx_ref, cls_ref = (next(it) for _ in range(3))
    w_in_scr, q_scr, kv_scr, o_scr, s_scr, p_scr, w_out_scr, x2_scr = (next(it) for _ in range(8))
    x_ref, mod_ref = pre_in[0], pre_in[1]
    x2_ref = [x2_scr] if resid else []

    _pre_kernel(False, n_prev, resid, *pre_in, q_scr, kv_scr.at[0, 0], *x2_ref, *cache_refs, w_in_scr)
    seq = s_scr.shape[1]
    for r in range(q_scr.shape[0] // seq):
        rows = pl.ds(r * seq, seq)
        _attn_kernel(lam_init, len(_SCORE_HEADS), 1, q_scr.at[rows], kv_scr.at[:, :, rows], lam_ref, dg_ref,
                     o_scr.at[rows], s_scr, p_scr)
    _post_kernel(False, o_scr, x2_scr if resid else x_ref, mod_ref, *post_w, x1_ref, hx_ref, cls_ref, w_out_scr)


def _ctx_layer(layer, x, n, seq, total, mod, w, prev_cache=(), resid=None):
    lay = lambda t: (layer, 0, 0)
    row = lambda t: (t, 0)
    reqs = CTX_REQUESTS_PER_STEP
    tile = reqs * seq
    mod_spec = lambda l: pl.BlockSpec((1, 1, N_MOD, D_MODEL), lambda t: (l, 0, 0, 0))
    in_specs = [
        pl.BlockSpec((tile, D_MODEL), row), mod_spec(layer),
        pl.BlockSpec((1, 1, D_MODEL), lay), pl.BlockSpec((1, IN_COLS, D_MODEL), lay),
        pl.BlockSpec((1, 1, KV_RANK), lay), pl.BlockSpec((1, 1, 512), lay), pl.BlockSpec((2 * LANES, LANES), lambda t: (0, 0)),
        pl.BlockSpec((1, 256, 384), lay), pl.BlockSpec((1, KV_RANK, 384), lay),
    ]
    args = [x, mod, w["g1"], w["w_in"], w["kv_g"], w["qk_g"], w["seg"], w["w_ka"], w["w_uv"]]
    if resid is not None:
        in_specs += [pl.BlockSpec((tile, D_MODEL), row), mod_spec(layer - 1)]
        args += [resid, mod]
    out_shape, out_specs, aliases = [], [], {}
    for j, width in enumerate(CACHE_WIDTHS):
        if prev_cache:
            aliases[len(args)] = len(out_shape)
            in_specs.append(pl.BlockSpec(memory_space=pl.ANY))
            args.append(prev_cache[j])
        shape = (MLA_ROPE, seq) if j == 1 else (seq, width)
        out_shape.append(jax.ShapeDtypeStruct((n // seq, DEPTH) + shape, f32))
        out_specs.append(pl.BlockSpec((reqs, 1) + shape, lambda t: (t, layer, 0, 0)))
    in_specs += [
        pl.BlockSpec((1, 4, DIFF_QK), lay), pl.BlockSpec((1, 1, DIFF_V), lay),
        pl.BlockSpec((1, D_MODEL, D_MODEL), lay), pl.BlockSpec((1, 1, D_MODEL), lay),
        pl.BlockSpec((1, D_MODEL, N_GROUPS), lay), pl.BlockSpec((1, 1, N_GROUPS), lay),
        pl.BlockSpec((1, D_MODEL, N_EXPERTS), lay), pl.BlockSpec((1, 1, N_EXPERTS), lay),
    ]
    args += [w["lam"], w["diff_g"], w["w_out"], w["g2"], w["w_grp"], w["b_grp"], w["w_rtr"], w["b_rtr"]]
    out_shape += [jax.ShapeDtypeStruct((total, D_MODEL), f32), jax.ShapeDtypeStruct((total, HX_COLS), i32),
                  jax.ShapeDtypeStruct((total, 1), i32)]
    out_specs += [pl.BlockSpec((tile, D_MODEL), row), pl.BlockSpec((tile, HX_COLS), row), pl.BlockSpec((tile, 1), row)]
    heads = len(_SCORE_HEADS)
    return pl.pallas_call(
        functools.partial(_ctx_kernel, 0.8 - 0.6 * math.exp(-0.3 * layer), len(prev_cache), resid is not None),
        out_shape=out_shape,
        grid=(n // tile,),
        in_specs=in_specs,
        out_specs=out_specs,
        scratch_shapes=[pltpu.VMEM((Z_COLS, D_MODEL), bf16), pltpu.VMEM((tile, Q_COLS), bf16),
                        pltpu.VMEM((1, 1, tile, KV_COLS), bf16), pltpu.VMEM((tile, D_MODEL), bf16),
                        pltpu.VMEM((heads, seq, seq), f32), pltpu.VMEM((heads, seq, seq), bf16),
                        pltpu.VMEM((D_MODEL, D_MODEL), bf16), pltpu.VMEM((tile, D_MODEL), f32)],
        input_output_aliases=aliases,
        compiler_params=_params(("arbitrary",)),
        name="context_layer",
    )(*args)


PLAN_CHUNK = 1024
TAB_ROWS = LANES


def _plan_kernel(cls_ref, slot_ref, tab_ref, rank_scr):
    n = cls_ref.shape[0]
    lane = jax.lax.broadcasted_iota(i32, (PLAN_CHUNK, LANES), 1)
    r = jax.lax.broadcasted_iota(i32, (LANES, LANES), 0)
    c = jax.lax.broadcasted_iota(i32, (LANES, LANES), 1)
    before = (c < r).astype(bf16)

    def count(b, seen):
        base = pl.multiple_of(b * PLAN_CHUNK, PLAN_CHUNK)
        onehot = (cls_ref[pl.ds(base, PLAN_CHUNK), :] == lane).astype(f32)
        for blk in range(PLAN_CHUNK // LANES):
            part = onehot[blk * LANES:(blk + 1) * LANES]
            ahead = _dot(before, part.astype(bf16)) + seen
            rank_scr[pl.ds(base + blk * LANES, LANES), :] = jnp.sum(part * ahead, axis=-1, keepdims=True)
            seen = seen + jnp.sum(part, axis=0, keepdims=True)
        return seen

    counts = jax.lax.fori_loop(0, n // PLAN_CHUNK, count, jnp.zeros((1, LANES), f32))
    tiles = jnp.floor((counts + (EXPERT_TILE - 1)) * (1.0 / EXPERT_TILE))
    rr = jax.lax.broadcasted_iota(i32, (LANES, LANES), 0)
    cc = jax.lax.broadcasted_iota(i32, (LANES, LANES), 1)
    ends = _dot(jnp.broadcast_to(tiles, (8, LANES)).astype(bf16), (rr <= cc).astype(bf16))[0:1]
    starts = ends - tiles

    def place(b, carry):
        rows = pl.ds(pl.multiple_of(b * PLAN_CHUNK, PLAN_CHUNK), PLAN_CHUNK)
        first = jnp.sum(jnp.where(cls_ref[rows, :] == lane, starts, 0.0), axis=-1, keepdims=True)
        slot_ref[rows, :] = (first * EXPERT_TILE + rank_scr[rows, :]).astype(i32)
        return carry

    jax.lax.fori_loop(0, n // PLAN_CHUNK, place, 0)

    tl = jax.lax.broadcasted_iota(i32, (TAB_ROWS, LANES), 1)
    n_tiles = jnp.sum(jnp.where(tl[0:1] == N_CLASSES - 1, ends, 0.0), axis=-1, keepdims=True)
    k = jnp.minimum(jax.lax.broadcasted_iota(i32, (TAB_ROWS, 1), 0).astype(f32), n_tiles - 1.0)
    cls_k = jnp.sum(jnp.where((tl < N_CLASSES) & (ends <= k), 1.0, 0.0), axis=-1, keepdims=True)
    cls_k = jnp.minimum(cls_k, N_CLASSES - 1.0)
    mine = tl.astype(f32) == cls_k
    used = jnp.sum(jnp.where(mine, counts, 0.0), axis=-1, keepdims=True)
    first = jnp.sum(jnp.where(mine, starts, 0.0), axis=-1, keepdims=True)
    valid = jnp.clip(used - (k - first) * EXPERT_TILE, 0.0, float(EXPERT_TILE))
    group = jnp.floor((cls_k + 0.5) * (1.0 / N_PAIRS))
    pair = cls_k - N_PAIRS * group
    lo = hi = jnp.zeros_like(pair)
    for p in range(N_PAIRS):
        lo = jnp.where(pair == p, float(PAIR_LO[p]), lo)
        hi = jnp.where(pair == p, float(PAIR_HI[p]), hi)
    e_lo = EXPERTS_PER_GROUP * group + lo
    e_hi = EXPERTS_PER_GROUP * group + hi
    tab = jnp.where(tl == 0, e_lo, jnp.where(tl == 1, e_hi, jnp.where(tl == 2, valid, jnp.where(tl == 3, n_tiles, 0.0))))
    tab_ref[...] = tab.astype(i32)


def _plan(cls, max_tiles):
    n = cls.shape[0]
    assert n % PLAN_CHUNK == 0 and max_tiles <= TAB_ROWS
    slot, tab = pl.pallas_call(
        _plan_kernel,
        out_shape=[jax.ShapeDtypeStruct((n, 1), i32), jax.ShapeDtypeStruct((TAB_ROWS, LANES), i32)],
        scratch_shapes=[pltpu.VMEM((n, 1), f32)],
        compiler_params=_params(None),
        name="dispatch_plan",
    )(cls)
    return slot.reshape(n), tab[:max_tiles, 0], tab[:max_tiles, 1], tab[:max_tiles, 2], tab[0, 3:4]


def _move_rows(src, idx, n_out, scatter):
    n = idx.shape[0]
    width = src.shape[1]
    per_worker = n // (SC_CORES * SC_SUBCORES)
    assert n % SC_ROWS == 0
    chunk = max(c for c in (64, 40, 32, 16, 8)
                if per_worker % c == 0 and 2 * c * width * src.dtype.itemsize <= SC_BUFFER_BYTES)
    n_chunks = per_worker // chunk
    mesh = plsc.VectorSubcoreMesh(core_axis_name="c", subcore_axis_name="s")

    @functools.partial(
        pl.kernel, mesh=mesh, out_type=jax.ShapeDtypeStruct((n_out, width), src.dtype),
        scratch_types=[pltpu.VMEM((chunk,), i32), pltpu.VMEM((chunk,), i32),
                       pltpu.VMEM((chunk, width), src.dtype), pltpu.VMEM((chunk, width), src.dtype),
                       pltpu.SemaphoreType.DMA, pltpu.SemaphoreType.DMA, pltpu.SemaphoreType.DMA,
                       pltpu.SemaphoreType.DMA])
    def move(src_hbm, idx_hbm, out_hbm, idx0, idx1, rows0, rows1, in0, in1, out0, out1):
        wid = jax.lax.axis_index("s") * SC_CORES + jax.lax.axis_index("c")
        base = wid * per_worker
        idx_v, rows_v, sem_in, sem_out = (idx0, idx1), (rows0, rows1), (in0, in1), (out0, out1)

        def fill(j):
            b = j % 2
            rows = pl.ds(base + j * chunk, chunk)
            pltpu.sync_copy(idx_hbm.at[rows], idx_v[b])
            src_rows = src_hbm.at[rows] if scatter else src_hbm.at[idx_v[b]]
            return pltpu.async_copy(src_rows, rows_v[b], sem_in[b])

        def drain(j):
            b = j % 2
            dst_rows = out_hbm.at[idx_v[b]] if scatter else out_hbm.at[pl.ds(base + j * chunk, chunk)]
            return pltpu.async_copy(rows_v[b], dst_rows, sem_out[b])

        fills, drains = {0: fill(0)}, {}
        for j in range(n_chunks):
            if j + 1 < n_chunks:
                if j >= 1:
                    drains[j - 1].wait()
                fills[j + 1] = fill(j + 1)
            fills[j].wait()
            drains[j] = drain(j)
        for j in range(max(n_chunks - 2, 0), n_chunks):
            drains[j].wait()

    return move(src, idx)


def _expert_kernel(lo_ref, hi_ref, valid_ref, nt_ref, xs_ref, wg_ref, wu_ref, wd_ref, ys_ref, wg_scr, wu_scr, wd_scr):
    k = pl.program_id(0)
    prev = jnp.maximum(k - 1, 0)

    @pl.when((k == 0) | (lo_ref[k] // EXPERTS_PER_GROUP != lo_ref[prev] // EXPERTS_PER_GROUP))
    def _():
        for j in range(EXPERTS_PER_GROUP):
            wg_scr[j] = wg_ref[0, 0, j].astype(bf16)
            wu_scr[j] = wu_ref[0, 0, j].astype(bf16)
            wd_scr[j] = wd_ref[0, 0, j].astype(bf16)

    @pl.when(k < nt_ref[0])
    def _():
        live = jax.lax.broadcasted_iota(i32, (EXPERT_TILE, 1), 0) < valid_ref[k]
        words = jnp.where(live, xs_ref[:, 0:HX_HALF], 0)
        x = jnp.concatenate([pltpu.bitcast(words & -65536, f32), pltpu.bitcast(words << 16, f32)], axis=1).astype(bf16)
        gates = pltpu.bitcast(jnp.where(live, xs_ref[:, HX_HALF:HX_COLS], 0), f32)
        y = None
        for lane, e_ref in enumerate((lo_ref, hi_ref)):
            j = e_ref[k] % EXPERTS_PER_GROUP
            hid = _silu(_dot(x, wg_scr[j])) * _dot(x, wu_scr[j]) * gates[:, lane:lane + 1]
            part = _dot(hid.astype(bf16), wd_scr[j])
            y = part if y is None else y + part
        ys_ref[...] = y


def _experts(layer, xs, e_lo, e_hi, valid, n_tiles, w):
    max_tiles = e_lo.shape[0]
    row = lambda k, lo, hi, valid, nt: (jnp.minimum(k, nt[0] - 1), 0)
    group = lambda k, lo, hi, valid, nt: (layer, lo[k] // EXPERTS_PER_GROUP, 0, 0, 0)
    by_group = lambda a: a.reshape(DEPTH, N_GROUPS, EXPERTS_PER_GROUP, *a.shape[2:])
    return pl.pallas_call(
        _expert_kernel,
        out_shape=jax.ShapeDtypeStruct((max_tiles * EXPERT_TILE, D_MODEL), f32),
        grid_spec=pltpu.PrefetchScalarGridSpec(
            num_scalar_prefetch=4, grid=(max_tiles,),
            in_specs=[pl.BlockSpec((EXPERT_TILE, HX_COLS), row),
                      pl.BlockSpec((1, 1, EXPERTS_PER_GROUP, D_MODEL, D_FF_EXPERT), group),
                      pl.BlockSpec((1, 1, EXPERTS_PER_GROUP, D_MODEL, D_FF_EXPERT), group),
                      pl.BlockSpec((1, 1, EXPERTS_PER_GROUP, D_FF_EXPERT, D_MODEL), group)],
            out_specs=pl.BlockSpec((EXPERT_TILE, D_MODEL), row),
            scratch_shapes=[pltpu.VMEM((EXPERTS_PER_GROUP, D_MODEL, D_FF_EXPERT), bf16),
                            pltpu.VMEM((EXPERTS_PER_GROUP, D_MODEL, D_FF_EXPERT), bf16),
                            pltpu.VMEM((EXPERTS_PER_GROUP, D_FF_EXPERT, D_MODEL), bf16)]),
        compiler_params=_params(("arbitrary",)),
        name="experts",
    )(e_lo, e_hi, valid, n_tiles, xs, by_group(w["w_gate"]), by_group(w["w_up"]), by_group(w["w_down"]))


def _final_kernel(x1_ref, y_ref, mod_ref, fg_ref, o_ref):
    o_ref[...] = _rms(x1_ref[...] + mod_ref[0, 0, 5:6, :] * y_ref[...], D_MODEL) * fg_ref[...]


def _final(x1, y, n, row0, mod, mod_row, w, tile):
    off = row0 // tile
    src_row = lambda t: (off + t, 0)
    return pl.pallas_call(
        _final_kernel,
        out_shape=jax.ShapeDtypeStruct((n, D_MODEL), f32),
        grid=(n // tile,),
        in_specs=[pl.BlockSpec((tile, D_MODEL), src_row), pl.BlockSpec((tile, D_MODEL), src_row),
                  pl.BlockSpec((1, 1, N_MOD, D_MODEL), lambda t: (DEPTH - 1, mod_row(t * tile), 0, 0)),
                  pl.BlockSpec((1, D_MODEL), lambda t: (0, 0))],
        out_specs=pl.BlockSpec((tile, D_MODEL), lambda t: (t, 0)),
        compiler_params=_params(("arbitrary",)),
        name="final_norm",
    )(x1, y, mod, w["final_g"])


def _rope_tables(n_tokens):
    pos = np.arange(n_tokens)
    row = (pos // GRID_W).astype(np.float64)
    col = (pos % GRID_W).astype(np.float64)

    def cs(rot_dim):
        quarter = rot_dim // 4
        inv = ROPE_THETA ** (-np.arange(quarter, dtype=np.float64) / quarter)
        ang = np.concatenate([row[:, None] * inv, col[:, None] * inv], axis=-1)
        return np.cos(ang), np.sin(ang)

    c32, s32 = cs(MLA_ROPE)
    c64, s64 = cs(HEAD_DIM)
    ones = np.ones((n_tokens, MLA_NOPE))
    zeros = np.zeros((n_tokens, MLA_NOPE))

    def rep(parts):
        period = np.concatenate(parts, axis=-1)
        return jnp.asarray(np.tile(period, (1, LANES // period.shape[-1])), f32)

    return (rep([ones, c32, c32]), rep([zeros, -s32, s32]), rep([c32, c32]), rep([-s32, s32]),
            rep([c64, c64]), rep([-s64, s64]))


def _layout_weights(norm1_g, norm2_g, w_in, mla_kv_norm_g, mla_w_uk, mla_w_uv, gqa_q_norm_g, gqa_k_norm_g,
                    diff_lambda, diff_norm_g, w_out, moe_w_group, moe_b_group, moe_w_router, moe_b_router,
                    moe_w_gate, moe_w_up, moe_w_down, final_norm_g):
    eye = jnp.eye(MLA_ROPE, dtype=f32)
    top = jnp.concatenate([mla_w_uk, jnp.zeros((DEPTH, KV_RANK, MLA_HEADS, MLA_ROPE), f32)], axis=-1)
    mid = jnp.concatenate([jnp.zeros((MLA_ROPE, MLA_HEADS, MLA_NOPE), f32),
                           jnp.broadcast_to(eye[:, None, :], (MLA_ROPE, MLA_HEADS, MLA_ROPE))], axis=-1)
    w_ka = jnp.concatenate([top.reshape(DEPTH, KV_RANK, 384),
                            jnp.broadcast_to(mid.reshape(1, MLA_ROPE, 384), (DEPTH, MLA_ROPE, 384)),
                            jnp.zeros((DEPTH, 256 - KV_RANK - MLA_ROPE, 384), f32)], axis=1).astype(bf16)
    seg_id = np.arange(LANES) // HEAD_DIM
    seg = jnp.asarray(np.tile(seg_id[:, None] == seg_id[None, :], (2, 1)), bf16)
    qk_g = jnp.concatenate([jnp.tile(gqa_q_norm_g, (1, GQA_HEADS)), jnp.tile(gqa_k_norm_g, (1, GQA_KV_HEADS))], axis=-1)
    return dict(
        g1=norm1_g.reshape(DEPTH, 1, D_MODEL), g2=norm2_g.reshape(DEPTH, 1, D_MODEL),
        w_in=jnp.swapaxes(w_in, 1, 2),
        kv_g=mla_kv_norm_g.reshape(DEPTH, 1, KV_RANK), qk_g=qk_g.reshape(DEPTH, 1, 512), seg=seg, w_ka=w_ka,
        w_uv=mla_w_uv.reshape(DEPTH, KV_RANK, 384), lam=diff_lambda, diff_g=diff_norm_g.reshape(DEPTH, 1, DIFF_V),
        w_out=w_out, w_grp=moe_w_group, b_grp=moe_b_group.reshape(DEPTH, 1, N_GROUPS), w_rtr=moe_w_router,
        b_rtr=moe_b_router.reshape(DEPTH, 1, N_EXPERTS), w_gate=moe_w_gate, w_up=moe_w_up, w_down=moe_w_down,
        final_g=final_norm_g.reshape(1, D_MODEL))


PRE_TILE = 512
CTX_REQUESTS_PER_STEP = 2
LAT_ATTN_TILE = 256
LAT_HEADS_PER_ROUND = 10
MXU_SUM_MIN_KEYS = 1024
POST_TILE = 1024
FINAL_TILE = 1024


def kernel(x_prompt, x_sample, c, cache_mla_ckv, cache_mla_krope, cache_gqa_k, cache_gqa_v, cache_diff_k, cache_diff_v, c_ctx, norm1_g, norm2_g, w_mod, b_mod, w_in, mla_kv_norm_g, mla_w_uk, mla_w_uv, gqa_q_norm_g, gqa_k_norm_g, diff_lambda, diff_norm_g, w_out, moe_w_group, moe_b_group, moe_w_router, moe_b_router, moe_w_gate, moe_w_up, moe_w_down, final_norm_g):
    B, S, _ = x_prompt.shape
    Bl, Sl, _ = x_sample.shape
    n_ctx, n_lat = B * S, Bl * Sl
    total = n_ctx + n_lat
    assert Bl + 1 <= MOD_ROWS and DEPTH == 2 and total % SC_ROWS == 0
    assert all(Sl % tile == 0 for tile in (PRE_TILE, LAT_ATTN_TILE, POST_TILE, FINAL_TILE))
    assert n_ctx % FINAL_TILE == 0 and n_ctx % POST_TILE == 0 and B % CTX_REQUESTS_PER_STEP == 0
    slot_rows = -(-(total + N_CLASSES * EXPERT_TILE) // SC_ROWS) * SC_ROWS
    max_tiles = slot_rows // EXPERT_TILE
    w = _layout_weights(norm1_g, norm2_g, w_in, mla_kv_norm_g, mla_w_uk, mla_w_uv, gqa_q_norm_g, gqa_k_norm_g,
                        diff_lambda, diff_norm_g, w_out, moe_w_group, moe_b_group, moe_w_router, moe_b_router,
                        moe_w_gate, moe_w_up, moe_w_down, final_norm_g)
    cond = jnp.concatenate([c_ctx[None, :], c, jnp.zeros((MOD_ROWS - 1 - Bl, D_MODEL), f32)], axis=0)
    mod = _modulation(cond, w_mod, b_mod).reshape(DEPTH, MOD_ROWS, N_MOD, D_MODEL)
    ctx_row = lambda token: 0
    lat_row = lambda token: 1 + token // Sl
    tabs = _rope_tables(Sl)
    kv_past = _cache_rows((cache_mla_ckv, cache_mla_krope, cache_gqa_k, cache_gqa_v, cache_diff_k, cache_diff_v), w)
    per_b = Sl // LAT_ATTN_TILE

    x_ctx, x_lat = x_prompt.reshape(n_ctx, D_MODEL), x_sample.reshape(n_lat, D_MODEL)
    cache = ()
    x1 = y = None
    for i in range(DEPTH):
        if i == 0:
            *cache, x1_c, hx_c, cls_c = _ctx_layer(i, x_ctx, n_ctx, S, total, mod, w)
        else:
            *cache, x1_c, hx_c, cls_c = _ctx_layer(i, x1, n_ctx, S, total, mod, w, prev_cache=cache, resid=y)
        if i == 0:
            q_l, kv_l = _pre_latent(i, x_lat, n_lat, 0, Sl, mod, lat_row, w, PRE_TILE, tabs)
        else:
            q_l, kv_l, x_lat = _pre_latent(i, x1, n_lat, n_ctx, Sl, mod, lat_row, w, PRE_TILE, tabs, resid=y)
        past = (kv_past, PAST_LEN, lambda t, i=i: (i, t // per_b, 0, 0))
        own = (kv_l.reshape(1, Bl, Sl, KV_COLS), Sl, lambda t: (0, t // per_b, 0, 0))
        o_l = _attention(i, q_l, [past, own], w, LAT_ATTN_TILE, LAT_HEADS_PER_ROUND)
        x1, hx, cls = _post(i, o_l, x_lat, n_ctx, total, mod, lat_row, w, POST_TILE, merged=(x1_c, hx_c, cls_c))
        if i == DEPTH - 1:
            outs = (cache[0], jnp.swapaxes(cache[1], 2, 3), cache[2].reshape(B, DEPTH, S, GQA_KV_HEADS, HEAD_DIM),
                    cache[3].reshape(B, DEPTH, S, GQA_KV_HEADS, HEAD_DIM),
                    cache[4].reshape(B, DEPTH, S, DIFF_HEADS, 2, DIFF_QK),
                    cache[5].reshape(B, DEPTH, S, DIFF_HEADS, DIFF_V))
            cls, outs = jax.lax.optimization_barrier((cls, outs))
        slot, e_lo, e_hi, valid, n_tiles = _plan(cls, max_tiles)
        xs = _move_rows(hx, slot, slot_rows, scatter=True)
        ys = _experts(i, xs, e_lo, e_hi, valid, n_tiles, w)
        y = _move_rows(ys, slot, total, scatter=False)

    y_prompt = _final(x1, y, n_ctx, 0, mod, ctx_row, w, FINAL_TILE).reshape(B, S, D_MODEL)
    y_sample = _final(x1, y, n_lat, n_ctx, mod, lat_row, w, FINAL_TILE).reshape(Bl, Sl, D_MODEL)
    return (y_prompt, y_sample, *outs)
```

```python
import functools
import math

import jax
import jax.numpy as jnp
import numpy as np
from jax.experimental import pallas as pl
from jax.experimental.pallas import tpu as pltpu
from jax.experimental.pallas import tpu_sc as plsc

D_MODEL = 1024
DEPTH = 2
PAST_LEN = 512
GRID_W = 64
ROPE_THETA = 10000.0
EPS = 1e-6
LOG2E = 1.4426950408889634
N_MOD = 6
HEAD_DIM = 64
MLA_HEADS = 6
MLA_NOPE = 32
MLA_ROPE = 32
MLA_V = 64
KV_RANK = 128
GQA_HEADS = 6
GQA_KV_HEADS = 2
GQA_GROUP = GQA_HEADS // GQA_KV_HEADS
DIFF_HEADS = 4
DIFF_QK = 32
DIFF_V = 64
N_GROUPS = 4
EXPERTS_PER_GROUP = 4
N_EXPERTS = N_GROUPS * EXPERTS_PER_GROUP
D_FF_EXPERT = 256

LANES = 128
MOD_ROWS = 8

IN_COLS = 1952
IN_KR = 512
Z_QA, Z_CKV, Z_QG, Z_KG, Z_VG, Z_QD, Z_KD, Z_VD, Z_KR = 0, 384, 512, 896, 1024, 1152, 1408, 1664, 1920
Z_COLS = 2048
Q_A, Q_G, Q_D, Q_COLS = 0, 384, 768, 1024
KV_KA, KV_VA, KV_KG, KV_VG, KV_KD, KV_VD, KV_COLS = 0, 384, 768, 896, 1024, 1280, 1536
CACHE_WIDTHS = (128, 32, 128, 128, 256, 256)

PAIR_LO = (0, 0, 0, 1, 1, 2)
PAIR_HI = (1, 2, 3, 3, 2, 3)
N_PAIRS = len(PAIR_LO)
N_CLASSES = N_GROUPS * N_PAIRS
HX_HALF = D_MODEL // 2
HX_COLS = HX_HALF + LANES
EXPERT_TILE = 256

SC_CORES, SC_SUBCORES = 2, 16
SC_BUFFER_BYTES = 400 * 1024
SC_ROWS = SC_CORES * SC_SUBCORES * 8

VMEM_LIMIT = 56 * 1024 * 1024

bf16 = jnp.bfloat16
f32 = jnp.float32
i32 = jnp.int32


def _dot(a, b):
    return jnp.dot(a, b, preferred_element_type=f32)


def _dot_nt(a, b):
    return jax.lax.dot_general(a, b, (((1,), (1,)), ((), ())), preferred_element_type=f32)


def _rms(x, width):
    return x * jax.lax.rsqrt(jnp.sum(x * x, axis=-1, keepdims=True) * (1.0 / width) + EPS)


def _silu(x):
    return x * (1.0 / (1.0 + jnp.exp(-x)))


def _params(sem):
    return pltpu.CompilerParams(dimension_semantics=sem, vmem_limit_bytes=VMEM_LIMIT)


def _mod_kernel(cond_ref, w_ref, b_ref, o_ref):
    o_ref[0] = _dot(_silu(cond_ref[...]).astype(bf16), w_ref[0].astype(bf16)) + b_ref[0]


def _modulation(cond, w_mod, b_mod):
    return pl.pallas_call(
        _mod_kernel,
        out_shape=jax.ShapeDtypeStruct((DEPTH, MOD_ROWS, N_MOD * D_MODEL), f32),
        grid=(DEPTH, N_MOD),
        in_specs=[
            pl.BlockSpec((MOD_ROWS, D_MODEL), lambda i, j: (0, 0)),
            pl.BlockSpec((1, D_MODEL, D_MODEL), lambda i, j: (i, 0, j)),
            pl.BlockSpec((1, 1, D_MODEL), lambda i, j: (i, 0, j)),
        ],
        out_specs=pl.BlockSpec((1, MOD_ROWS, D_MODEL), lambda i, j: (i, 0, j)),
        compiler_params=_params(("arbitrary", "arbitrary")),
        name="modulation",
    )(cond, w_mod, b_mod.reshape(DEPTH, 1, N_MOD * D_MODEL))


def _swap_halves(x, half):
    lane = jax.lax.broadcasted_iota(i32, x.shape, 1)
    fwd = pltpu.roll(x, LANES - half, 1)
    bwd = pltpu.roll(x, half, 1)
    return jnp.where((lane & (2 * half - 1)) < half, fwd, bwd)


def _rope_block(x, cos, sin, half):
    return x * cos + _swap_halves(x, half) * sin


def _pre_kernel(rope, n_prev, resid, *refs):
    it = iter(refs)
    x_ref, mod_ref, g1_ref, w_in_ref, kvg_ref, qkg_ref, seg_ref, wka_ref, wuv_ref = (next(it) for _ in range(9))
    if resid:
        y_ref, pmod_ref = next(it), next(it)
    if rope:
        ca_ref, sa_ref, c32_ref, s32_ref, c64_ref, s64_ref = (next(it) for _ in range(6))
    prev_refs = [next(it) for _ in range(n_prev)]
    q_ref, kv_ref = next(it), next(it)
    if resid:
        x2_ref = next(it)
    cache_refs = [] if rope else [next(it) for _ in range(len(CACHE_WIDTHS))]
    w_scr = next(it)

    @pl.when(pl.program_id(0) == 0)
    def _():
        w_scr[0:IN_KR] = w_in_ref[0, 0:IN_KR].astype(bf16)
        w_scr[IN_KR:Z_KR] = w_in_ref[0, IN_KR + MLA_ROPE:IN_COLS].astype(bf16)
        w_scr[Z_KR:Z_KR + MLA_ROPE] = w_in_ref[0, IN_KR:IN_KR + MLA_ROPE].astype(bf16)
        w_scr[Z_KR + MLA_ROPE:Z_COLS] = jnp.zeros((Z_COLS - Z_KR - MLA_ROPE, D_MODEL), bf16)

    x = x_ref[...]
    if resid:
        x = x + pmod_ref[0, 0, 5:6, :] * y_ref[...]
        x2_ref[...] = x
    shift1 = mod_ref[0, 0, 0:1, :]
    scale1 = mod_ref[0, 0, 1:2, :]
    h = (_rms(x, D_MODEL) * g1_ref[0]) * (1.0 + scale1) + shift1
    z = _dot_nt(h.astype(bf16), w_scr[...])

    ckv = _rms(z[:, Z_CKV:Z_CKV + KV_RANK], KV_RANK) * kvg_ref[0]

    qk = z[:, Z_QG:Z_VG]
    sq = qk * qk
    sq_hi = sq.astype(bf16)
    sq_lo = (sq - sq_hi.astype(f32)).astype(bf16)
    seg = seg_ref[...]
    ms = jnp.concatenate(
        [_dot(jnp.concatenate([sq_hi[:, LANES * j:LANES * (j + 1)], sq_lo[:, LANES * j:LANES * (j + 1)]], axis=1), seg)
         for j in range(qk.shape[1] // LANES)], axis=1) * (1.0 / HEAD_DIM)
    qk = qk * jax.lax.rsqrt(ms + EPS) * qkg_ref[0]

    def blocks(arr, n):
        return [arr[:, LANES * j:LANES * (j + 1)] for j in range(n)]

    qa = blocks(z[:, Z_QA:Z_QA + 384], 3)
    qkb = blocks(qk, 4)
    qd = blocks(z[:, Z_QD:Z_QD + 256], 2)
    kd = blocks(z[:, Z_KD:Z_KD + 256], 2)
    kr = z[:, Z_KR:Z_KR + LANES]
    if rope:
        ca, sa, c32, s32, c64, s64 = (r[...] for r in (ca_ref, sa_ref, c32_ref, s32_ref, c64_ref, s64_ref))
        qa = [_rope_block(b, ca, sa, MLA_ROPE // 2) for b in qa]
        qkb = [_rope_block(b, c64, s64, HEAD_DIM // 2) for b in qkb]
        qd = [_rope_block(b, c32, s32, DIFF_QK // 2) for b in qd]
        kd = [_rope_block(b, c32, s32, DIFF_QK // 2) for b in kd]
        kr = _rope_block(kr, c32, s32, MLA_ROPE // 2)

    vg = z[:, Z_VG:Z_VG + 128]
    vd = z[:, Z_VD:Z_VD + 256]
    ckv_b = ckv.astype(bf16)
    k_a = _dot(jnp.concatenate([ckv_b, kr.astype(bf16)], axis=1), wka_ref[0])
    v_a = _dot(ckv_b, wuv_ref[0].astype(bf16))

    for j in range(3):
        q_ref[:, Q_A + LANES * j:Q_A + LANES * (j + 1)] = (qa[j] * (HEAD_DIM ** -0.5 * LOG2E)).astype(bf16)
        q_ref[:, Q_G + LANES * j:Q_G + LANES * (j + 1)] = (qkb[j] * (HEAD_DIM ** -0.5 * LOG2E)).astype(bf16)
    for j in range(2):
        q_ref[:, Q_D + LANES * j:Q_D + LANES * (j + 1)] = (qd[j] * (DIFF_QK ** -0.5 * LOG2E)).astype(bf16)
        kv_ref[:, KV_KD + LANES * j:KV_KD + LANES * (j + 1)] = kd[j].astype(bf16)
    kv_ref[:, KV_KA:KV_KA + 384] = k_a.astype(bf16)
    kv_ref[:, KV_VA:KV_VA + 384] = v_a.astype(bf16)
    kv_ref[:, KV_KG:KV_KG + 128] = qkb[3].astype(bf16)
    kv_ref[:, KV_VG:KV_VG + 128] = vg.astype(bf16)
    kv_ref[:, KV_VD:KV_VD + 256] = vd.astype(bf16)
    if not rope:
        rows = [ckv, None, qkb[3], vg, jnp.concatenate(kd, axis=1), vd]
        for out, new in zip(cache_refs, rows):
            if new is None:
                seq = out.shape[3]
                for r in range(out.shape[0]):
                    out[r, 0] = kr[r * seq:(r + 1) * seq].T[:MLA_ROPE, :]
            else:
                reqs, _, seq, width = out.shape
                out[:, 0] = new.reshape(reqs, seq, width)


def _pre_latent(layer, x, n, row0, seq, mod, mod_row, w, tile, rope_tabs, resid=None):
    lay = lambda t: (layer, 0, 0)
    row = lambda t: (t, 0)
    off = row0 // tile
    src_row = lambda t: (off + t, 0)
    in_specs = [
        pl.BlockSpec((tile, D_MODEL), src_row),
        pl.BlockSpec((1, 1, N_MOD, D_MODEL), lambda t: (layer, mod_row(t * tile), 0, 0)),
        pl.BlockSpec((1, 1, D_MODEL), lay),
        pl.BlockSpec((1, IN_COLS, D_MODEL), lay),
        pl.BlockSpec((1, 1, KV_RANK), lay),
        pl.BlockSpec((1, 1, 512), lay),
        pl.BlockSpec((2 * LANES, LANES), lambda t: (0, 0)),
        pl.BlockSpec((1, 256, 384), lay),
        pl.BlockSpec((1, KV_RANK, 384), lay),
    ]
    args = [x, mod, w["g1"], w["w_in"], w["kv_g"], w["qk_g"], w["seg"], w["w_ka"], w["w_uv"]]
    if resid is not None:
        in_specs += [pl.BlockSpec((tile, D_MODEL), src_row),
                     pl.BlockSpec((1, 1, N_MOD, D_MODEL), lambda t: (layer - 1, mod_row(t * tile), 0, 0))]
        args += [resid, mod]
    per_b = seq // tile
    in_specs += [pl.BlockSpec((tile, LANES), lambda t: (t % per_b, 0))] * 6
    args += list(rope_tabs)
    out_shape = [jax.ShapeDtypeStruct((n, Q_COLS), bf16), jax.ShapeDtypeStruct((n, KV_COLS), bf16)]
    out_specs = [pl.BlockSpec((tile, Q_COLS), row), pl.BlockSpec((tile, KV_COLS), row)]
    if resid is not None:
        out_shape.append(jax.ShapeDtypeStruct((n, D_MODEL), f32))
        out_specs.append(pl.BlockSpec((tile, D_MODEL), row))
    return pl.pallas_call(
        functools.partial(_pre_kernel, True, 0, resid is not None),
        out_shape=out_shape,
        grid=(n // tile,),
        in_specs=in_specs,
        out_specs=out_specs,
        scratch_shapes=[pltpu.VMEM((Z_COLS, D_MODEL), bf16)],
        compiler_params=_params(("arbitrary",)),
        name="pre_latent",
    )(*args)


PAST_CKV, PAST_KG, PAST_VG, PAST_KD, PAST_VD, PAST_KR, PAST_COLS = 0, 128, 256, 384, 640, 896, 928


def _cache_kernel(past_ref, wka_ref, wuv_ref, kv_ref):
    ckv_b = past_ref[0, 0, :, PAST_CKV:PAST_CKV + KV_RANK].astype(bf16)
    kr_b = past_ref[0, 0, :, PAST_KR:PAST_KR + MLA_ROPE].astype(bf16)
    wka = wka_ref[0]
    k_a = _dot(ckv_b, wka[:KV_RANK]) + _dot(kr_b, wka[KV_RANK:KV_RANK + MLA_ROPE])
    kv_ref[0, 0, :, KV_KA:KV_KA + 384] = k_a.astype(bf16)
    kv_ref[0, 0, :, KV_VA:KV_VA + 384] = _dot(ckv_b, wuv_ref[0].astype(bf16)).astype(bf16)
    kv_ref[0, 0, :, KV_KG:KV_KG + 128] = past_ref[0, 0, :, PAST_KG:PAST_KG + 128].astype(bf16)
    kv_ref[0, 0, :, KV_VG:KV_VG + 128] = past_ref[0, 0, :, PAST_VG:PAST_VG + 128].astype(bf16)
    kv_ref[0, 0, :, KV_KD:KV_KD + 256] = past_ref[0, 0, :, PAST_KD:PAST_KD + 256].astype(bf16)
    kv_ref[0, 0, :, KV_VD:KV_VD + 256] = past_ref[0, 0, :, PAST_VD:PAST_VD + 256].astype(bf16)


def _cache_rows(caches, w):
    ckv, kr, kg, vg, kd, vd = caches
    B = ckv.shape[0]
    flat = lambda a: a.reshape(B, DEPTH, PAST_LEN, -1)
    past = jnp.concatenate([flat(ckv), flat(kg), flat(vg), flat(kd), flat(vd), flat(kr)], axis=-1)
    return pl.pallas_call(
        _cache_kernel,
        out_shape=jax.ShapeDtypeStruct((DEPTH, B, PAST_LEN, KV_COLS), bf16),
        grid=(DEPTH, B),
        in_specs=[pl.BlockSpec((1, 1, PAST_LEN, PAST_COLS), lambda i, b: (b, i, 0, 0)),
                  pl.BlockSpec((1, 256, 384), lambda i, b: (i, 0, 0)), pl.BlockSpec((1, KV_RANK, 384), lambda i, b: (i, 0, 0))],
        out_specs=pl.BlockSpec((1, 1, PAST_LEN, KV_COLS), lambda i, b: (i, b, 0, 0)),
        compiler_params=_params(("arbitrary", "arbitrary")),
        name="cache_rows",
    )(past, w["w_ka"], w["w_uv"])


_SCORE_HEADS = (
    [(Q_A + 64 * h, KV_KA + 64 * h, 64, KV_VA + MLA_V * h) for h in range(MLA_HEADS)]
    + [(Q_G + 64 * h, KV_KG + 64 * (h // GQA_GROUP), 64, KV_VG + 64 * (h // GQA_GROUP)) for h in range(GQA_HEADS)]
    + [(Q_D + 64 * h + DIFF_QK * c, KV_KD + 64 * h + DIFF_QK * c, DIFF_QK, KV_VD + DIFF_V * h)
       for h in range(DIFF_HEADS) for c in range(2)])


def _attn_kernel(lam_init, per_round, n_src, q_ref, *refs):
    kv_refs = refs[:n_src]
    lam_ref, dg_ref, o_ref, s_ref, p_ref = refs[n_src:]
    spans, start = [], 0
    for r in kv_refs:
        spans.append((r, start, r.shape[2]))
        start += r.shape[2]
    mxu_sum = start >= MXU_SUM_MIN_KEYS

    outs = []
    for first in range(0, len(_SCORE_HEADS), per_round):
        chunk = _SCORE_HEADS[first:first + per_round]
        for j, (q_off, k_off, width, _) in enumerate(chunk):
            for r, lo, size in spans:
                s_ref[j, :, lo:lo + size] = _dot_nt(q_ref[:, q_off:q_off + width], r[0, 0, :, k_off:k_off + width])
        s = s_ref[...]
        p = jnp.exp2(s - jnp.max(s, axis=-1, keepdims=True))
        if mxu_sum:
            p_ref[...] = p.astype(bf16)
            for j, (_, _, _, v_off) in enumerate(chunk):
                o = sum(_dot(p_ref[j, :, lo:lo + size],
                             jnp.concatenate([r[0, 0, :, v_off:v_off + DIFF_V],
                                              jnp.ones((size, LANES - DIFF_V), bf16)], axis=1))
                        for r, lo, size in spans)
                outs.append((o * pltpu.roll(1.0 / o, DIFF_V, 1))[:, :DIFF_V])
        else:
            inv = 1.0 / jnp.sum(p, axis=-1, keepdims=True)
            p_ref[...] = p.astype(bf16)
            for j, (_, _, _, v_off) in enumerate(chunk):
                o = sum(_dot(p_ref[j, :, lo:lo + size], r[0, 0, :, v_off:v_off + DIFF_V]) for r, lo, size in spans)
                outs.append(o * inv[j])

    lp = lam_ref[0]
    e1 = jnp.exp(jnp.sum(lp[0:1] * lp[1:2], axis=-1, keepdims=True))
    e2 = jnp.exp(jnp.sum(lp[2:3] * lp[3:4], axis=-1, keepdims=True))
    lam = e1 - e2 + lam_init
    heads = outs[:MLA_HEADS + GQA_HEADS]
    for h in range(DIFF_HEADS):
        o1, o2 = outs[MLA_HEADS + GQA_HEADS + 2 * h:MLA_HEADS + GQA_HEADS + 2 * h + 2]
        heads.append(_rms(o1 - lam * o2, DIFF_V) * dg_ref[0] * (1.0 - lam_init))
    for j in range(len(heads) // 2):
        o_ref[:, LANES * j:LANES * (j + 1)] = jnp.concatenate(heads[2 * j:2 * j + 2], axis=1).astype(bf16)


def _attention(layer, q, sources, w, tile, per_round):
    n = q.shape[0]
    lam_init = 0.8 - 0.6 * math.exp(-0.3 * layer)
    s_kv = sum(rows for _, rows, _ in sources)
    assert len(_SCORE_HEADS) % per_round == 0
    return pl.pallas_call(
        functools.partial(_attn_kernel, lam_init, per_round, len(sources)),
        out_shape=jax.ShapeDtypeStruct((n, D_MODEL), bf16),
        scratch_shapes=[pltpu.VMEM((per_round, tile, s_kv), f32), pltpu.VMEM((per_round, tile, s_kv), bf16)],
        grid=(n // tile,),
        in_specs=[pl.BlockSpec((tile, Q_COLS), lambda t: (t, 0))]
        + [pl.BlockSpec((1, 1, rows, KV_COLS), index) for _, rows, index in sources]
        + [pl.BlockSpec((1, 4, DIFF_QK), lambda t: (layer, 0, 0)), pl.BlockSpec((1, 1, DIFF_V), lambda t: (layer, 0, 0))],
        out_specs=pl.BlockSpec((tile, D_MODEL), lambda t: (t, 0)),
        compiler_params=_params(("arbitrary",)),
        name="attention",
    )(q, *[arr for arr, _, _ in sources], w["lam"], w["diff_g"])


def _post_kernel(merge, *refs):
    it = iter(refs)
    o_ref, x_ref, mod_ref, w_out_ref, g2_ref, wg_ref, bg_ref, we_ref, be_ref = (next(it) for _ in range(9))
    if merge:
        next(it), next(it), next(it)
    x1_ref, hx_ref, cls_ref, w_scr = (next(it) for _ in range(4))

    @pl.when(pl.program_id(0) == 0)
    def _():
        w_scr[...] = w_out_ref[0].astype(bf16)

    gate1 = mod_ref[0, 0, 2:3, :]
    shift2 = mod_ref[0, 0, 3:4, :]
    scale2 = mod_ref[0, 0, 4:5, :]
    x1 = x_ref[...] + gate1 * _dot(o_ref[...], w_scr[...])
    x1_ref[...] = x1
    h2 = ((_rms(x1, D_MODEL) * g2_ref[0]) * (1.0 + scale2) + shift2).astype(bf16)
    bits = pltpu.bitcast(h2.astype(f32), i32)
    hx_ref[:, 0:HX_HALF] = bits[:, 0:HX_HALF] | jax.lax.shift_right_logical(bits[:, HX_HALF:D_MODEL], 16)

    def first_lane(mask, lane_f):
        return jnp.min(jnp.where(mask, lane_f, float(LANES)), axis=-1, keepdims=True)

    gl = _dot(h2, wg_ref[0].astype(bf16)) + bg_ref[0]
    glane = jax.lax.broadcasted_iota(i32, gl.shape, 1).astype(f32)
    ge = jnp.exp(gl - jnp.max(gl, axis=-1, keepdims=True))
    gprob = ge / jnp.sum(ge, axis=-1, keepdims=True)
    g_top = jnp.max(gprob, axis=-1, keepdims=True)
    g_idx = first_lane(gprob == g_top, glane)

    el = _dot(h2, we_ref[0].astype(bf16)) + be_ref[0]
    lane = jax.lax.broadcasted_iota(i32, el.shape, 1)
    lane_f = lane.astype(f32)
    emask = (lane >> 2).astype(f32) == g_idx
    em = jnp.where(emask, el, -jnp.inf)
    ee = jnp.where(emask, jnp.exp(em - jnp.max(em, axis=-1, keepdims=True)), 0.0)
    ep = ee / jnp.sum(ee, axis=-1, keepdims=True)
    p1 = jnp.max(jnp.where(emask, ep, -1.0), axis=-1, keepdims=True)
    i1 = first_lane(emask & (ep == p1), lane_f)
    rest = emask & (lane_f != i1)
    p2 = jnp.max(jnp.where(rest, ep, -1.0), axis=-1, keepdims=True)
    i2 = first_lane(rest & (ep == p2), lane_f)
    tot = p1 + p2
    w1 = g_top * (p1 / tot)
    w2 = g_top * (p2 / tot)

    lo = jnp.minimum(i1, i2) - EXPERTS_PER_GROUP * g_idx
    hi = jnp.maximum(i1, i2) - EXPERTS_PER_GROUP * g_idx
    pair = jnp.where(lo == 0.0, hi - 1.0, jnp.where(lo == 1.0, jnp.where(hi == 3.0, 3.0, 4.0), 5.0))
    cls_ref[...] = (N_PAIRS * g_idx + pair).astype(i32)
    g_lo = jnp.where(i1 < i2, w1, w2)
    g_hi = jnp.where(i1 < i2, w2, w1)
    tail_lane = jax.lax.broadcasted_iota(i32, (h2.shape[0], LANES), 1)
    hx_ref[:, HX_HALF:HX_COLS] = pltpu.bitcast(
        jnp.where(tail_lane == 0, g_lo, jnp.where(tail_lane == 1, g_hi, 0.0)), i32)


def _post(layer, o, x, row0, total, mod, mod_row, w, tile, merged=None):
    n = o.shape[0]
    lay = lambda t: (layer, 0, 0)
    row = lambda t: (t, 0)
    off = row0 // tile
    out_row = lambda t: (off + t, 0)
    in_specs = [
        pl.BlockSpec((tile, D_MODEL), row),
        pl.BlockSpec((tile, D_MODEL), row),
        pl.BlockSpec((1, 1, N_MOD, D_MODEL), lambda t: (layer, mod_row(t * tile), 0, 0)),
        pl.BlockSpec((1, D_MODEL, D_MODEL), lay),
        pl.BlockSpec((1, 1, D_MODEL), lay),
        pl.BlockSpec((1, D_MODEL, N_GROUPS), lay),
        pl.BlockSpec((1, 1, N_GROUPS), lay),
        pl.BlockSpec((1, D_MODEL, N_EXPERTS), lay),
        pl.BlockSpec((1, 1, N_EXPERTS), lay),
    ]
    args = [o, x, mod, w["w_out"], w["g2"], w["w_grp"], w["b_grp"], w["w_rtr"], w["b_rtr"]]
    aliases = {}
    if merged is not None:
        aliases = {len(args) + j: j for j in range(3)}
        in_specs += [pl.BlockSpec(memory_space=pl.ANY)] * 3
        args += list(merged)
    return pl.pallas_call(
        functools.partial(_post_kernel, merged is not None),
        out_shape=[jax.ShapeDtypeStruct((total, D_MODEL), f32), jax.ShapeDtypeStruct((total, HX_COLS), i32),
                   jax.ShapeDtypeStruct((total, 1), i32)],
        grid=(n // tile,),
        in_specs=in_specs,
        out_specs=[pl.BlockSpec((tile, D_MODEL), out_row), pl.BlockSpec((tile, HX_COLS), out_row),
                   pl.BlockSpec((tile, 1), out_row)],
        scratch_shapes=[pltpu.VMEM((D_MODEL, D_MODEL), bf16)],
        input_output_aliases=aliases,
        compiler_params=_params(("arbitrary",)),
        name="post_attention",
    )(*args)


def _ctx_kernel(lam_init, n_prev, resid, *refs):
    it = iter(refs)
    pre_in = [next(it) for _ in range(9 + (2 if resid else 0) + n_prev)]
    lam_ref, dg_ref = next(it), next(it)
    post_w = [next(it) for _ in range(6)]
    cache_refs = [next(it) for _ in range(len(CACHE_WIDTHS))]
    x1_ref, hx_ref, cls_ref = (next(it) for _ in range(3))
    w_in_scr, q_scr, kv_scr, o_scr, s_scr, p_scr, w_out_scr, x2_scr = (next(it) for _ in range(8))
    x_ref, mod_ref = pre_in[0], pre_in[1]
    x2_ref = [x2_scr] if resid else []

    _pre_kernel(False, n_prev, resid, *pre_in, q_scr, kv_scr.at[0, 0], *x2_ref, *cache_refs, w_in_scr)
    seq = s_scr.shape[1]
    for r in range(q_scr.shape[0] // seq):
        rows = pl.ds(r * seq, seq)
        _attn_kernel(lam_init, len(_SCORE_HEADS), 1, q_scr.at[rows], kv_scr.at[:, :, rows], lam_ref, dg_ref,
                     o_scr.at[rows], s_scr, p_scr)
    _post_kernel(False, o_scr, x2_scr if resid else x_ref, mod_ref, *post_w, x1_ref, hx_ref, cls_ref, w_out_scr)


def _ctx_layer(layer, x, n, seq, total, mod, w, prev_cache=(), resid=None):
    lay = lambda t: (layer, 0, 0)
    row = lambda t: (t, 0)
    reqs = CTX_REQUESTS_PER_STEP
    tile = reqs * seq
    mod_spec = lambda l: pl.BlockSpec((1, 1, N_MOD, D_MODEL), lambda t: (l, 0, 0, 0))
    in_specs = [
        pl.BlockSpec((tile, D_MODEL), row), mod_spec(layer),
        pl.BlockSpec((1, 1, D_MODEL), lay), pl.BlockSpec((1, IN_COLS, D_MODEL), lay),
        pl.BlockSpec((1, 1, KV_RANK), lay), pl.BlockSpec((1, 1, 512), lay), pl.BlockSpec((2 * LANES, LANES), lambda t: (0, 0)),
        pl.BlockSpec((1, 256, 384), lay), pl.BlockSpec((1, KV_RANK, 384), lay),
    ]
    args = [x, mod, w["g1"], w["w_in"], w["kv_g"], w["qk_g"], w["seg"], w["w_ka"], w["w_uv"]]
    if resid is not None:
        in_specs += [pl.BlockSpec((tile, D_MODEL), row), mod_spec(layer - 1)]
        args += [resid, mod]
    out_shape, out_specs, aliases = [], [], {}
    for j, width in enumerate(CACHE_WIDTHS):
        if prev_cache:
            aliases[len(args)] = len(out_shape)
            in_specs.append(pl.BlockSpec(memory_space=pl.ANY))
            args.append(prev_cache[j])
        shape = (MLA_ROPE, seq) if j == 1 else (seq, width)
        out_shape.append(jax.ShapeDtypeStruct((n // seq, DEPTH) + shape, f32))
        out_specs.append(pl.BlockSpec((reqs, 1) + shape, lambda t: (t, layer, 0, 0)))
    in_specs += [
        pl.BlockSpec((1, 4, DIFF_QK), lay), pl.BlockSpec((1, 1, DIFF_V), lay),
        pl.BlockSpec((1, D_MODEL, D_MODEL), lay), pl.BlockSpec((1, 1, D_MODEL), lay),
        pl.BlockSpec((1, D_MODEL, N_GROUPS), lay), pl.BlockSpec((1, 1, N_GROUPS), lay),
        pl.BlockSpec((1, D_MODEL, N_EXPERTS), lay), pl.BlockSpec((1, 1, N_EXPERTS), lay),
    ]
    args += [w["lam"], w["diff_g"], w["w_out"], w["g2"], w["w_grp"], w["b_grp"], w["w_rtr"], w["b_rtr"]]
    out_shape += [jax.ShapeDtypeStruct((total, D_MODEL), f32), jax.ShapeDtypeStruct((total, HX_COLS), i32),
                  jax.ShapeDtypeStruct((total, 1), i32)]
    out_specs += [pl.BlockSpec((tile, D_MODEL), row), pl.BlockSpec((tile, HX_COLS), row), pl.BlockSpec((tile, 1), row)]
    heads = len(_SCORE_HEADS)
    return pl.pallas_call(
        functools.partial(_ctx_kernel, 0.8 - 0.6 * math.exp(-0.3 * layer), len(prev_cache), resid is not None),
        out_shape=out_shape,
        grid=(n // tile,),
        in_specs=in_specs,
        out_specs=out_specs,
        scratch_shapes=[pltpu.VMEM((Z_COLS, D_MODEL), bf16), pltpu.VMEM((tile, Q_COLS), bf16),
                        pltpu.VMEM((1, 1, tile, KV_COLS), bf16), pltpu.VMEM((tile, D_MODEL), bf16),
                        pltpu.VMEM((heads, seq, seq), f32), pltpu.VMEM((heads, seq, seq), bf16),
                        pltpu.VMEM((D_MODEL, D_MODEL), bf16), pltpu.VMEM((tile, D_MODEL), f32)],
        input_output_aliases=aliases,
        compiler_params=_params(("arbitrary",)),
        name="context_layer",
    )(*args)


PLAN_CHUNK = 1024
TAB_ROWS = LANES


def _plan_kernel(cls_ref, slot_ref, tab_ref, rank_scr):
    n = cls_ref.shape[0]
    lane = jax.lax.broadcasted_iota(i32, (PLAN_CHUNK, LANES), 1)
    r = jax.lax.broadcasted_iota(i32, (LANES, LANES), 0)
    c = jax.lax.broadcasted_iota(i32, (LANES, LANES), 1)
    before = (c < r).astype(bf16)

    def count(b, seen):
        base = pl.multiple_of(b * PLAN_CHUNK, PLAN_CHUNK)
        onehot = (cls_ref[pl.ds(base, PLAN_CHUNK), :] == lane).astype(f32)
        for blk in range(PLAN_CHUNK // LANES):
            part = onehot[blk * LANES:(blk + 1) * LANES]
            ahead = _dot(before, part.astype(bf16)) + seen
            rank_scr[pl.ds(base + blk * LANES, LANES), :] = jnp.sum(part * ahead, axis=-1, keepdims=True)
            seen = seen + jnp.sum(part, axis=0, keepdims=True)
        return seen

    counts = jax.lax.fori_loop(0, n // PLAN_CHUNK, count, jnp.zeros((1, LANES), f32))
    tiles = jnp.floor((counts + (EXPERT_TILE - 1)) * (1.0 / EXPERT_TILE))
    rr = jax.lax.broadcasted_iota(i32, (LANES, LANES), 0)
    cc = jax.lax.broadcasted_iota(i32, (LANES, LANES), 1)
    ends = _dot(jnp.broadcast_to(tiles, (8, LANES)).astype(bf16), (rr <= cc).astype(bf16))[0:1]
    starts = ends - tiles
    diag = rr == cc

    def place(b, carry):
        rows = pl.ds(pl.multiple_of(b * PLAN_CHUNK, PLAN_CHUNK), PLAN_CHUNK)
        first = jnp.sum(jnp.where(cls_ref[rows, :] == lane, starts, 0.0), axis=-1, keepdims=True)
        slot = first * EXPERT_TILE + rank_scr[rows, :]
        for blk in range(PLAN_CHUNK // LANES):
            col = slot[blk * LANES : (blk + 1) * LANES]
            row = jnp.sum(jnp.where(diag, col, 0.0), axis=0, keepdims=True)
            slot_ref[pl.ds(b * (PLAN_CHUNK // LANES) + blk, 1), :] = row.astype(i32)
        return carry

    jax.lax.fori_loop(0, n // PLAN_CHUNK, place, 0)

    tl = jax.lax.broadcasted_iota(i32, (TAB_ROWS, LANES), 1)
    n_tiles = jnp.sum(jnp.where(tl[0:1] == N_CLASSES - 1, ends, 0.0), axis=-1, keepdims=True)
    k = jnp.minimum(jax.lax.broadcasted_iota(i32, (TAB_ROWS, 1), 0).astype(f32), n_tiles - 1.0)
    cls_k = jnp.sum(jnp.where((tl < N_CLASSES) & (ends <= k), 1.0, 0.0), axis=-1, keepdims=True)
    cls_k = jnp.minimum(cls_k, N_CLASSES - 1.0)
    mine = tl.astype(f32) == cls_k
    used = jnp.sum(jnp.where(mine, counts, 0.0), axis=-1, keepdims=True)
    first = jnp.sum(jnp.where(mine, starts, 0.0), axis=-1, keepdims=True)
    valid = jnp.clip(used - (k - first) * EXPERT_TILE, 0.0, float(EXPERT_TILE))
    group = jnp.floor((cls_k + 0.5) * (1.0 / N_PAIRS))
    pair = cls_k - N_PAIRS * group
    lo = hi = jnp.zeros_like(pair)
    for p in range(N_PAIRS):
        lo = jnp.where(pair == p, float(PAIR_LO[p]), lo)
        hi = jnp.where(pair == p, float(PAIR_HI[p]), hi)
    e_lo = EXPERTS_PER_GROUP * group + lo
    e_hi = EXPERTS_PER_GROUP * group + hi
    tab = jnp.where(tl == 0, e_lo, jnp.where(tl == 1, e_hi, jnp.where(tl == 2, valid, jnp.where(tl == 3, n_tiles, 0.0))))
    tab_ref[...] = tab.astype(i32)


def _plan(cls, max_tiles):
    n = cls.shape[0]
    assert n % PLAN_CHUNK == 0 and max_tiles <= TAB_ROWS
    slot, tab = pl.pallas_call(
        _plan_kernel,
        out_shape=[jax.ShapeDtypeStruct((n // LANES, LANES), i32), jax.ShapeDtypeStruct((TAB_ROWS, LANES), i32)],
        scratch_shapes=[pltpu.VMEM((n, 1), f32)],
        compiler_params=_params(None),
        name="dispatch_plan",
    )(cls)
    return slot.reshape(n), tab[:max_tiles, 0], tab[:max_tiles, 1], tab[:max_tiles, 2], tab[0, 3:4]


def _move_rows(src, idx, n_out, scatter):
    n = idx.shape[0]
    width = src.shape[1]
    per_worker = n // (SC_CORES * SC_SUBCORES)
    assert n % SC_ROWS == 0
    chunk = max(c for c in (64, 40, 32, 16, 8)
                if per_worker % c == 0 and 2 * c * width * src.dtype.itemsize <= SC_BUFFER_BYTES)
    n_chunks = per_worker // chunk
    mesh = plsc.VectorSubcoreMesh(core_axis_name="c", subcore_axis_name="s")

    @functools.partial(
        pl.kernel, mesh=mesh, out_type=jax.ShapeDtypeStruct((n_out, width), src.dtype),
        scratch_types=[pltpu.VMEM((chunk,), i32), pltpu.VMEM((chunk,), i32),
                       pltpu.VMEM((chunk, width), src.dtype), pltpu.VMEM((chunk, width), src.dtype),
                       pltpu.SemaphoreType.DMA, pltpu.SemaphoreType.DMA, pltpu.SemaphoreType.DMA,
                       pltpu.SemaphoreType.DMA])
    def move(src_hbm, idx_hbm, out_hbm, idx0, idx1, rows0, rows1, in0, in1, out0, out1):
        wid = jax.lax.axis_index("s") * SC_CORES + jax.lax.axis_index("c")
        base = wid * per_worker
        idx_v, rows_v, sem_in, sem_out = (idx0, idx1), (rows0, rows1), (in0, in1), (out0, out1)

        def fill(j):
            b = j % 2
            rows = pl.ds(base + j * chunk, chunk)
            pltpu.sync_copy(idx_hbm.at[rows], idx_v[b])
            src_rows = src_hbm.at[rows] if scatter else src_hbm.at[idx_v[b]]
            return pltpu.async_copy(src_rows, rows_v[b], sem_in[b])

        def drain(j):
            b = j % 2
            dst_rows = out_hbm.at[idx_v[b]] if scatter else out_hbm.at[pl.ds(base + j * chunk, chunk)]
            return pltpu.async_copy(rows_v[b], dst_rows, sem_out[b])

        fills, drains = {0: fill(0)}, {}
        for j in range(n_chunks):
            if j + 1 < n_chunks:
                if j >= 1:
                    drains[j - 1].wait()
                fills[j + 1] = fill(j + 1)
            fills[j].wait()
            drains[j] = drain(j)
        for j in range(max(n_chunks - 2, 0), n_chunks):
            drains[j].wait()

    return move(src, idx)


def _expert_kernel(lo_ref, hi_ref, valid_ref, nt_ref, xs_ref, wg_ref, wu_ref, wd_ref, ys_ref, wg_scr, wu_scr, wd_scr):
    k = pl.program_id(0)
    prev = jnp.maximum(k - 1, 0)

    @pl.when((k == 0) | (lo_ref[k] // EXPERTS_PER_GROUP != lo_ref[prev] // EXPERTS_PER_GROUP))
    def _():
        for j in range(EXPERTS_PER_GROUP):
            wg_scr[j] = wg_ref[0, 0, j].astype(bf16)
            wu_scr[j] = wu_ref[0, 0, j].astype(bf16)
            wd_scr[j] = wd_ref[0, 0, j].astype(bf16)

    @pl.when(k < nt_ref[0])
    def _():
        live = jax.lax.broadcasted_iota(i32, (EXPERT_TILE, 1), 0) < valid_ref[k]
        words = jnp.where(live, xs_ref[:, 0:HX_HALF], 0)
        x = jnp.concatenate([pltpu.bitcast(words & -65536, f32), pltpu.bitcast(words << 16, f32)], axis=1).astype(bf16)
        gates = pltpu.bitcast(jnp.where(live, xs_ref[:, HX_HALF:HX_COLS], 0), f32)
        y = None
        for lane, e_ref in enumerate((lo_ref, hi_ref)):
            j = e_ref[k] % EXPERTS_PER_GROUP
            hid = _silu(_dot(x, wg_scr[j])) * _dot(x, wu_scr[j]) * gates[:, lane:lane + 1]
            part = _dot(hid.astype(bf16), wd_scr[j])
            y = part if y is None else y + part
        ys_ref[...] = y


def _experts(layer, xs, e_lo, e_hi, valid, n_tiles, w):
    max_tiles = e_lo.shape[0]
    row = lambda k, lo, hi, valid, nt: (jnp.minimum(k, nt[0] - 1), 0)
    group = lambda k, lo, hi, valid, nt: (layer, lo[k] // EXPERTS_PER_GROUP, 0, 0, 0)
    by_group = lambda a: a.reshape(DEPTH, N_GROUPS, EXPERTS_PER_GROUP, *a.shape[2:])
    return pl.pallas_call(
        _expert_kernel,
        out_shape=jax.ShapeDtypeStruct((max_tiles * EXPERT_TILE, D_MODEL), f32),
        grid_spec=pltpu.PrefetchScalarGridSpec(
            num_scalar_prefetch=4, grid=(max_tiles,),
            in_specs=[pl.BlockSpec((EXPERT_TILE, HX_COLS), row),
                      pl.BlockSpec((1, 1, EXPERTS_PER_GROUP, D_MODEL, D_FF_EXPERT), group),
                      pl.BlockSpec((1, 1, EXPERTS_PER_GROUP, D_MODEL, D_FF_EXPERT), group),
                      pl.BlockSpec((1, 1, EXPERTS_PER_GROUP, D_FF_EXPERT, D_MODEL), group)],
            out_specs=pl.BlockSpec((EXPERT_TILE, D_MODEL), row),
            scratch_shapes=[pltpu.VMEM((EXPERTS_PER_GROUP, D_MODEL, D_FF_EXPERT), bf16),
                            pltpu.VMEM((EXPERTS_PER_GROUP, D_MODEL, D_FF_EXPERT), bf16),
                            pltpu.VMEM((EXPERTS_PER_GROUP, D_FF_EXPERT, D_MODEL), bf16)]),
        compiler_params=_params(("arbitrary",)),
        name="experts",
    )(e_lo, e_hi, valid, n_tiles, xs, by_group(w["w_gate"]), by_group(w["w_up"]), by_group(w["w_down"]))


def _final_kernel(x1_ref, y_ref, mod_ref, fg_ref, o_ref):
    o_ref[...] = _rms(x1_ref[...] + mod_ref[0, 0, 5:6, :] * y_ref[...], D_MODEL) * fg_ref[...]


def _final(x1, y, n, row0, mod, mod_row, w, tile):
    off = row0 // tile
    src_row = lambda t: (off + t, 0)
    return pl.pallas_call(
        _final_kernel,
        out_shape=jax.ShapeDtypeStruct((n, D_MODEL), f32),
        grid=(n // tile,),
        in_specs=[pl.BlockSpec((tile, D_MODEL), src_row), pl.BlockSpec((tile, D_MODEL), src_row),
                  pl.BlockSpec((1, 1, N_MOD, D_MODEL), lambda t: (DEPTH - 1, mod_row(t * tile), 0, 0)),
                  pl.BlockSpec((1, D_MODEL), lambda t: (0, 0))],
        out_specs=pl.BlockSpec((tile, D_MODEL), lambda t: (t, 0)),
        compiler_params=_params(("arbitrary",)),
        name="final_norm",
    )(x1, y, mod, w["final_g"])


def _rope_tables(n_tokens):
    pos = np.arange(n_tokens)
    row = (pos // GRID_W).astype(np.float64)
    col = (pos % GRID_W).astype(np.float64)

    def cs(rot_dim):
        quarter = rot_dim // 4
        inv = ROPE_THETA ** (-np.arange(quarter, dtype=np.float64) / quarter)
        ang = np.concatenate([row[:, None] * inv, col[:, None] * inv], axis=-1)
        return np.cos(ang), np.sin(ang)

    c32, s32 = cs(MLA_ROPE)
    c64, s64 = cs(HEAD_DIM)
    ones = np.ones((n_tokens, MLA_NOPE))
    zeros = np.zeros((n_tokens, MLA_NOPE))

    def rep(parts):
        period = np.concatenate(parts, axis=-1)
        return jnp.asarray(np.tile(period, (1, LANES // period.shape[-1])), f32)

    return (rep([ones, c32, c32]), rep([zeros, -s32, s32]), rep([c32, c32]), rep([-s32, s32]),
            rep([c64, c64]), rep([-s64, s64]))


def _layout_weights(norm1_g, norm2_g, w_in, mla_kv_norm_g, mla_w_uk, mla_w_uv, gqa_q_norm_g, gqa_k_norm_g,
                    diff_lambda, diff_norm_g, w_out, moe_w_group, moe_b_group, moe_w_router, moe_b_router,
                    moe_w_gate, moe_w_up, moe_w_down, final_norm_g):
    eye = jnp.eye(MLA_ROPE, dtype=f32)
    top = jnp.concatenate([mla_w_uk, jnp.zeros((DEPTH, KV_RANK, MLA_HEADS, MLA_ROPE), f32)], axis=-1)
    mid = jnp.concatenate([jnp.zeros((MLA_ROPE, MLA_HEADS, MLA_NOPE), f32),
                           jnp.broadcast_to(eye[:, None, :], (MLA_ROPE, MLA_HEADS, MLA_ROPE))], axis=-1)
    w_ka = jnp.concatenate([top.reshape(DEPTH, KV_RANK, 384),
                            jnp.broadcast_to(mid.reshape(1, MLA_ROPE, 384), (DEPTH, MLA_ROPE, 384)),
                            jnp.zeros((DEPTH, 256 - KV_RANK - MLA_ROPE, 384), f32)], axis=1).astype(bf16)
    seg_id = np.arange(LANES) // HEAD_DIM
    seg = jnp.asarray(np.tile(seg_id[:, None] == seg_id[None, :], (2, 1)), bf16)
    qk_g = jnp.concatenate([jnp.tile(gqa_q_norm_g, (1, GQA_HEADS)), jnp.tile(gqa_k_norm_g, (1, GQA_KV_HEADS))], axis=-1)
    return dict(
        g1=norm1_g.reshape(DEPTH, 1, D_MODEL), g2=norm2_g.reshape(DEPTH, 1, D_MODEL),
        w_in=jnp.swapaxes(w_in, 1, 2),
        kv_g=mla_kv_norm_g.reshape(DEPTH, 1, KV_RANK), qk_g=qk_g.reshape(DEPTH, 1, 512), seg=seg, w_ka=w_ka,
        w_uv=mla_w_uv.reshape(DEPTH, KV_RANK, 384), lam=diff_lambda, diff_g=diff_norm_g.reshape(DEPTH, 1, DIFF_V),
        w_out=w_out, w_grp=moe_w_group, b_grp=moe_b_group.reshape(DEPTH, 1, N_GROUPS), w_rtr=moe_w_router,
        b_rtr=moe_b_router.reshape(DEPTH, 1, N_EXPERTS), w_gate=moe_w_gate, w_up=moe_w_up, w_down=moe_w_down,
        final_g=final_norm_g.reshape(1, D_MODEL))


PRE_TILE = 512
CTX_REQUESTS_PER_STEP = 2
LAT_ATTN_TILE = 256
LAT_HEADS_PER_ROUND = 10
MXU_SUM_MIN_KEYS = 1024
POST_TILE = 512
FINAL_TILE = 1024


def kernel(x_prompt, x_sample, c, cache_mla_ckv, cache_mla_krope, cache_gqa_k, cache_gqa_v, cache_diff_k, cache_diff_v, c_ctx, norm1_g, norm2_g, w_mod, b_mod, w_in, mla_kv_norm_g, mla_w_uk, mla_w_uv, gqa_q_norm_g, gqa_k_norm_g, diff_lambda, diff_norm_g, w_out, moe_w_group, moe_b_group, moe_w_router, moe_b_router, moe_w_gate, moe_w_up, moe_w_down, final_norm_g):
    B, S, _ = x_prompt.shape
    Bl, Sl, _ = x_sample.shape
    n_ctx, n_lat = B * S, Bl * Sl
    total = n_ctx + n_lat
    assert Bl + 1 <= MOD_ROWS and DEPTH == 2 and total % SC_ROWS == 0
    assert all(Sl % tile == 0 for tile in (PRE_TILE, LAT_ATTN_TILE, POST_TILE, FINAL_TILE))
    assert n_ctx % FINAL_TILE == 0 and n_ctx % POST_TILE == 0 and B % CTX_REQUESTS_PER_STEP == 0
    slot_rows = -(-(total + N_CLASSES * EXPERT_TILE) // SC_ROWS) * SC_ROWS
    max_tiles = slot_rows // EXPERT_TILE
    w = _layout_weights(norm1_g, norm2_g, w_in, mla_kv_norm_g, mla_w_uk, mla_w_uv, gqa_q_norm_g, gqa_k_norm_g,
                        diff_lambda, diff_norm_g, w_out, moe_w_group, moe_b_group, moe_w_router, moe_b_router,
                        moe_w_gate, moe_w_up, moe_w_down, final_norm_g)
    cond = jnp.concatenate([c_ctx[None, :], c, jnp.zeros((MOD_ROWS - 1 - Bl, D_MODEL), f32)], axis=0)
    mod = _modulation(cond, w_mod, b_mod).reshape(DEPTH, MOD_ROWS, N_MOD, D_MODEL)
    ctx_row = lambda token: 0
    lat_row = lambda token: 1 + token // Sl
    tabs = _rope_tables(Sl)
    kv_past = _cache_rows((cache_mla_ckv, cache_mla_krope, cache_gqa_k, cache_gqa_v, cache_diff_k, cache_diff_v), w)
    per_b = Sl // LAT_ATTN_TILE

    x_ctx, x_lat = x_prompt.reshape(n_ctx, D_MODEL), x_sample.reshape(n_lat, D_MODEL)
    cache = ()
    x1 = y = None
    for i in range(DEPTH):
        if i == 0:
            *cache, x1_c, hx_c, cls_c = _ctx_layer(i, x_ctx, n_ctx, S, total, mod, w)
        else:
            *cache, x1_c, hx_c, cls_c = _ctx_layer(i, x1, n_ctx, S, total, mod, w, prev_cache=cache, resid=y)
        if i == 0:
            q_l, kv_l = _pre_latent(i, x_lat, n_lat, 0, Sl, mod, lat_row, w, PRE_TILE, tabs)
        else:
            q_l, kv_l, x_lat = _pre_latent(i, x1, n_lat, n_ctx, Sl, mod, lat_row, w, PRE_TILE, tabs, resid=y)
        past = (kv_past, PAST_LEN, lambda t, i=i: (i, t // per_b, 0, 0))
        own = (kv_l.reshape(1, Bl, Sl, KV_COLS), Sl, lambda t: (0, t // per_b, 0, 0))
        o_l = _attention(i, q_l, [past, own], w, LAT_ATTN_TILE, LAT_HEADS_PER_ROUND)
        x1, hx, cls = _post(i, o_l, x_lat, n_ctx, total, mod, lat_row, w, POST_TILE, merged=(x1_c, hx_c, cls_c))
        if i == DEPTH - 1:
            outs = (cache[0], jnp.swapaxes(cache[1], 2, 3), cache[2].reshape(B, DEPTH, S, GQA_KV_HEADS, HEAD_DIM),
                    cache[3].reshape(B, DEPTH, S, GQA_KV_HEADS, HEAD_DIM),
                    cache[4].reshape(B, DEPTH, S, DIFF_HEADS, 2, DIFF_QK),
                    cache[5].reshape(B, DEPTH, S, DIFF_HEADS, DIFF_V))
            cls, outs = jax.lax.optimization_barrier((cls, outs))
        slot, e_lo, e_hi, valid, n_tiles = _plan(cls, max_tiles)
        xs = _move_rows(hx, slot, slot_rows, scatter=True)
        ys = _experts(i, xs, e_lo, e_hi, valid, n_tiles, w)
        y = _move_rows(ys, slot, total, scatter=False)

    y_prompt = _final(x1, y, n_ctx, 0, mod, ctx_row, w, FINAL_TILE).reshape(B, S, D_MODEL)
    y_sample = _final(x1, y, n_lat, n_ctx, mod, lat_row, w, FINAL_TILE).reshape(Bl, Sl, D_MODEL)
    return (y_prompt, y_sample, *outs)
```
